```python
import jax
import jax.numpy as jnp
from jax import lax
import numpy as np

D_MODEL = 2048
BATCH = 4
SEQ = 2048
DEPTH = 2
DEC_BATCH = 128
DEC_SEQ = 1
PAST_LEN = 16384
PAGE_SIZE = 128

RMS_EPS = 1e-6
N_BRANCHES = 4
BRANCH_WIDTH = 512
LA_CHUNK = 64
GLA_HEADS = 4
GLA_DK = 64
GLA_DV = 128
GLA_GATE_RANK = 16
GLA_GATE_TEMP = 16.0
POOL_WINDOWS = (2, 4, 8, 16)
POOL_GROUPS = 4
POOL_GROUP = 128
POOL_WIDTH = POOL_GROUPS * POOL_GROUP
POOL_BUF = max(POOL_WINDOWS) - 1
RET_HEADS = 4
RET_DK = 64
RET_DV = 128
ROPE_BASE = 10000.0
SGU_CHUNK = 128
SGU_GROUPS = 4
SGU_GROUP = 128
SGU_WIDTH = SGU_GROUPS * SGU_GROUP
N_EXPERTS = 64
N_EXPERT_GROUPS = 8
TOPK_GROUPS = 4
TOP_K = 8
D_EXPERT = 512
ROUTED_SCALE = 2.5
MOE_BLOCK = 128
IN_SIZES = (GLA_HEADS * GLA_DK, GLA_HEADS * GLA_DK, GLA_HEADS * GLA_DV, GLA_HEADS * GLA_DV, GLA_GATE_RANK,
            POOL_WIDTH,
            RET_HEADS * RET_DK, RET_HEADS * RET_DK, RET_HEADS * RET_DV, RET_HEADS * RET_DV,
            SGU_WIDTH, SGU_WIDTH)
D_IN = sum(IN_SIZES)

kernel_name = 'hybrid_gated_branch_decoder_step'


def _split_points():
    pts, acc = [], 0
    for s in IN_SIZES[:-1]:
        acc += s
        pts.append(acc)
    return pts


def rms_norm(x, g):
    xf = x.astype(jnp.float32)
    y = xf * lax.rsqrt(jnp.mean(xf * xf, axis=-1, keepdims=True) + RMS_EPS)
    return (y * g.astype(jnp.float32)).astype(x.dtype)


def rotary(x, pos):
    half = x.shape[-1] // 2
    inv = ROPE_BASE ** (-jnp.arange(half, dtype=jnp.float32) / half)
    ang = pos.astype(jnp.float32)[:, None] * inv[None, :]
    cos = jnp.cos(ang)[None, :, None, :]
    sin = jnp.sin(ang)[None, :, None, :]
    xf = x.astype(jnp.float32)
    x1, x2 = xf[..., :half], xf[..., half:]
    return jnp.concatenate([x1 * cos - x2 * sin, x2 * cos + x1 * sin], axis=-1).astype(x.dtype)


def chunked_gated_linear_attention(q, k, v, log_decay, s0):
    B, T, H, dk = q.shape
    L = min(LA_CHUNK, T)
    n = -(-T // L)
    pad = n * L - T

    def blocks(a):
        a = jnp.pad(a.astype(jnp.float32), ((0, 0), (0, pad), (0, 0), (0, 0)))
        return a.reshape(B, n, L, H, a.shape[-1])

    qc, kc, vc, gc = blocks(q), blocks(k), blocks(v), blocks(log_decay)
    b = jnp.cumsum(gc, axis=2)
    b_last = b[:, :, -1:]
    q_dec = qc * jnp.exp(b)
    k_inv = kc * jnp.exp(-b)
    k_end = kc * jnp.exp(b_last - b)
    causal = jnp.tril(jnp.ones((L, L), dtype=bool))
    scores = jnp.einsum('bnthk,bnshk->bnhts', q_dec, k_inv)
    scores = jnp.where(causal, scores, 0.0)
    o_intra = jnp.einsum('bnhts,bnshv->bnthv', scores, vc)
    kv_chunk = jnp.einsum('bnshk,bnshv->bnhkv', k_end, vc)
    a_chunk = jnp.exp(b_last[:, :, 0])

    def step(S, inp):
        a_i, kv_i = inp
        return a_i[..., None] * S + kv_i, S

    s_final, s_prev = lax.scan(step, s0.astype(jnp.float32),
                               (jnp.moveaxis(a_chunk, 1, 0), jnp.moveaxis(kv_chunk, 1, 0)))
    s_prev = jnp.moveaxis(s_prev, 0, 1)
    o_inter = jnp.einsum('bnthk,bnhkv->bnthv', q_dec, s_prev)
    o = (o_intra + o_inter).reshape(B, n * L, H, v.shape[-1])[:, :T]
    return o, s_final


def gla_branch(q, k, v, r, g_low, w_gate, b_gate, norm_g, s0):
    B, T, _ = q.shape
    dt = q.dtype
    qh = q.reshape(B, T, GLA_HEADS, GLA_DK) * (GLA_DK ** -0.5)
    kh = k.reshape(B, T, GLA_HEADS, GLA_DK)
    vh = v.reshape(B, T, GLA_HEADS, GLA_DV)
    logit = (g_low @ w_gate + b_gate).astype(jnp.float32)
    log_a = (jax.nn.log_sigmoid(logit) / GLA_GATE_TEMP).reshape(B, T, GLA_HEADS, GLA_DK)
    o, s = chunked_gated_linear_attention(qh, kh, vh, log_a, s0)
    o = rms_norm(o, norm_g.reshape(GLA_HEADS, GLA_DV)).astype(dt).reshape(B, T, GLA_HEADS * GLA_DV)
    return o * jax.nn.silu(r), s


def retention_branch(q, k, v, g, pos, norm_g, s0):
    B, T, _ = q.shape
    dt = q.dtype
    qh = rotary(q.reshape(B, T, RET_HEADS, RET_DK), pos)
    kh = rotary(k.reshape(B, T, RET_HEADS, RET_DK), pos) * (RET_DK ** -0.5)
    vh = v.reshape(B, T, RET_HEADS, RET_DV)
    log_gamma = jnp.log1p(-jnp.exp2(-5.0 - jnp.arange(RET_HEADS, dtype=jnp.float32)))
    log_decay = jnp.broadcast_to(log_gamma[:, None], (B, T, RET_HEADS, RET_DK))
    o, s = chunked_gated_linear_attention(qh, kh, vh, log_decay, s0)
    o = rms_norm(o, norm_g.reshape(RET_HEADS, RET_DV)).astype(dt).reshape(B, T, RET_HEADS * RET_DV)
    return o * jax.nn.silu(g), s


def pool_branch(p, buf, n_past, w_grp, scale):
    B, T, _ = p.shape
    full = jnp.concatenate([buf.astype(p.dtype), p], axis=1)
    cs = jnp.pad(jnp.cumsum(full.astype(jnp.float32), axis=1), ((0, 0), (1, 0), (0, 0)))
    pos = n_past + jnp.arange(T)
    end = POOL_BUF + 1
    means = []
    for gi, w in enumerate(POOL_WINDOWS):
        sl = slice(gi * POOL_GROUP, (gi + 1) * POOL_GROUP)
        win_sum = cs[:, end:end + T, sl] - cs[:, end - w:end - w + T, sl]
        count = jnp.minimum(w, pos + 1).astype(jnp.float32)
        means.append(win_sum / count[None, :, None])
    y = jnp.concatenate(means, axis=-1) - p.astype(jnp.float32)
    y = jnp.einsum('btgc,gcd->btgd', y.reshape(B, T, POOL_GROUPS, POOL_GROUP), w_grp.astype(jnp.float32))
    y = y.reshape(B, T, POOL_WIDTH) * scale.astype(jnp.float32)
    return y.astype(p.dtype), full[:, -POOL_BUF:]


def sgu_branch(u, v, norm_g, w_s, b_s):
    B, T, _ = u.shape
    vn = rms_norm(v, norm_g)
    L = min(SGU_CHUNK, T)
    n = -(-T // L)
    vc = jnp.pad(vn, ((0, 0), (0, n * L - T), (0, 0))).reshape(B, n, L, SGU_GROUPS, SGU_GROUP)
    w = jnp.tril(w_s[:, :L, :L])
    mixed = jnp.einsum('gts,bnsgc->bntgc', w, vc) + b_s[:, :L].T[None, None, :, :, None]
    mixed = mixed.reshape(B, n * L, SGU_WIDTH)[:, :T]
    return u * mixed, vn


def mixer_block(h, pos, n_past, s_gla, s_pool, s_ret, w_in, w_gla_gate, b_gla_gate, gla_norm,
                pool_w, pool_scale, ret_norm, sgu_norm, sgu_w, sgu_b, w_branch, w_merge_gate,
                b_merge_gate, w_out):
    B, T, _ = h.shape
    gq, gk, gv, gr, glow, pin, rq, rk, rv, rg, su, sv = jnp.split(h @ w_in, _split_points(), axis=-1)
    o_a, s_gla = gla_branch(gq, gk, gv, gr, glow, w_gla_gate, b_gla_gate, gla_norm, s_gla)
    o_b, s_pool = pool_branch(pin, s_pool, n_past, pool_w, pool_scale)
    o_c, s_ret = retention_branch(rq, rk, rv, rg, pos, ret_norm, s_ret)
    o_d, v_rows = sgu_branch(su, sv, sgu_norm, sgu_w, sgu_b)
    branches = jnp.stack([o_a, o_b, o_c, o_d], axis=2)
    up = jnp.einsum('btim,imd->btid', branches, w_branch)
    gates = jax.nn.sigmoid(h @ w_merge_gate + b_merge_gate).reshape(B, T, N_BRANCHES, D_MODEL)
    merged = jnp.sum(gates * up, axis=2)
    return merged @ w_out, s_gla, s_pool, s_ret, v_rows


def route(h, router_w, router_bias):
    N = h.shape[0]
    scores = jax.nn.sigmoid((h @ router_w).astype(jnp.float32))
    sel = scores + router_bias.astype(jnp.float32)
    grp = sel.reshape(N, N_EXPERT_GROUPS, N_EXPERTS // N_EXPERT_GROUPS)
    grp_score = jnp.sum(lax.top_k(grp, 2)[0], axis=-1)
    _, top_g = lax.top_k(grp_score, TOPK_GROUPS)
    gmask = jnp.sum(jax.nn.one_hot(top_g, N_EXPERT_GROUPS, dtype=jnp.float32), axis=1) > 0
    emask = jnp.repeat(gmask, N_EXPERTS // N_EXPERT_GROUPS, axis=1)
    _, idx = lax.top_k(jnp.where(emask, sel, -jnp.inf), TOP_K)
    w = jnp.take_along_axis(scores, idx, axis=-1)
    w = w / jnp.sum(w, axis=-1, keepdims=True) * ROUTED_SCALE
    return idx, w


def routed_experts(h, idx, wts, w1, w3, w2):
    N, D = h.shape
    A = N * TOP_K
    flat_e = idx.reshape(A)
    order = jnp.argsort(flat_e)
    e_sorted = flat_e[order]
    tok_sorted = (order // TOP_K).astype(jnp.int32)
    w_sorted = wts.reshape(A)[order]
    counts = jnp.bincount(flat_e, length=N_EXPERTS)
    padded = (counts + MOE_BLOCK - 1) // MOE_BLOCK * MOE_BLOCK
    pad_end = jnp.cumsum(padded)
    pad_start = pad_end - padded
    grp_start = jnp.cumsum(counts) - counts
    dest = pad_start[e_sorted] + jnp.arange(A) - grp_start[e_sorted]
    n_blocks = -(-(A + N_EXPERTS * (MOE_BLOCK - 1)) // MOE_BLOCK)
    L = n_blocks * MOE_BLOCK
    slot_tok = jnp.full((L,), N, jnp.int32).at[dest].set(tok_sorted)
    slot_w = jnp.zeros((L,), jnp.float32).at[dest].set(w_sorted)
    block_e = jnp.minimum(jnp.searchsorted(pad_end, jnp.arange(n_blocks) * MOE_BLOCK, side='right'),
                          N_EXPERTS - 1)
    h_pad = jnp.concatenate([h, jnp.zeros((1, D), h.dtype)], axis=0)

    def run_block(args):
        tok_b, w_b, e = args
        xb = h_pad[tok_b]
        hid = jax.nn.silu(xb @ w1[e]) * (xb @ w3[e])
        return (hid @ w2[e]) * w_b[:, None].astype(h.dtype)

    out = lax.map(run_block, (slot_tok.reshape(n_blocks, MOE_BLOCK),
                              slot_w.reshape(n_blocks, MOE_BLOCK), block_e))
    return jax.ops.segment_sum(out.reshape(L, D), slot_tok, num_segments=N + 1)[:N]


def moe_ffn(h, router_w, router_bias, expert_w1, expert_w3, expert_w2, shared_w1, shared_w3, shared_w2):
    B, T, D = h.shape
    hf = h.reshape(B * T, D)
    idx, wts = route(hf, router_w, router_bias)
    routed = routed_experts(hf, idx, wts, expert_w1, expert_w3, expert_w2)
    shared = (jax.nn.silu(hf @ shared_w1) * (hf @ shared_w3)) @ shared_w2
    return (routed + shared).reshape(B, T, D)


def decoder_layer(x, c, pos, n_past, s_gla, s_pool, s_ret, lw):
    (w_ada, b_ada, norm_mix_pre, norm_mix_post, norm_ffn_pre, norm_ffn_post,
     w_in, w_gla_gate, b_gla_gate, gla_norm, pool_w, pool_scale, ret_norm,
     sgu_norm, sgu_w, sgu_b, w_branch, w_merge_gate, b_merge_gate, w_out,
     router_w, router_bias, expert_w1, expert_w3, expert_w2,
     shared_w1, shared_w3, shared_w2) = lw
    mod = (jax.nn.silu(c) @ w_ada + b_ada)[:, None, :]
    shift_m, scale_m, gate_m, shift_f, scale_f, gate_f = jnp.split(mod, 6, axis=-1)
    h = rms_norm(x, norm_mix_pre) * (1.0 + scale_m) + shift_m
    mix, s_gla, s_pool, s_ret, v_rows = mixer_block(
        h, pos, n_past, s_gla, s_pool, s_ret, w_in, w_gla_gate, b_gla_gate, gla_norm,
        pool_w, pool_scale, ret_norm, sgu_norm, sgu_w, sgu_b, w_branch, w_merge_gate,
        b_merge_gate, w_out)
    x = x + gate_m * rms_norm(mix, norm_mix_post)
    h = rms_norm(x, norm_ffn_pre) * (1.0 + scale_f) + shift_f
    f = moe_ffn(h, router_w, router_bias, expert_w1, expert_w3, expert_w2, shared_w1, shared_w3, shared_w2)
    x = x + gate_f * rms_norm(f, norm_ffn_post)
    return x, s_gla, s_pool, s_ret, v_rows


def run_group(x, c, pos, n_past, s_gla, s_pool, s_ret, weights):
    gla_out, pool_out, ret_out, v_out = [], [], [], []
    for l in range(DEPTH):
        lw = tuple(w[l] for w in weights)
        x, sg, sp, sr, vr = decoder_layer(x, c, pos, n_past, s_gla[l], s_pool[l], s_ret[l], lw)
        gla_out.append(sg)
        pool_out.append(sp)
        ret_out.append(sr)
        v_out.append(vr)
    return x, jnp.stack(gla_out), jnp.stack(pool_out), jnp.stack(ret_out), jnp.stack(v_out)


def setup_inputs(seed: int = 0) -> dict:
    key = jax.random.key(seed)
    ks = iter(jax.random.split(key, 40))

    def nrm(shape, scale):
        return jax.random.normal(next(ks), shape, jnp.float32) * scale

    def gain(shape):
        return 1.0 + nrm(shape, 0.02)

    D = D_MODEL
    return {
        'x_prompt': nrm((BATCH, SEQ, D), 1.0),
        'x_sample': nrm((DEC_BATCH, DEC_SEQ, D), 1.0),
        'c_prompt': nrm((BATCH, D), 1.0),
        'c_sample': nrm((DEC_BATCH, D), 1.0),
        'state_gla': nrm((DEPTH, DEC_BATCH, GLA_HEADS, GLA_DK, GLA_DV), 0.3),
        'state_pool': nrm((DEPTH, DEC_BATCH, POOL_BUF, POOL_WIDTH), 1.0),
        'state_ret': nrm((DEPTH, DEC_BATCH, RET_HEADS, RET_DK, RET_DV), 0.3),
        'w_ada': nrm((DEPTH, D, 6 * D), 0.5 * D ** -0.5),
        'b_ada': nrm((DEPTH, 6 * D), 0.02),
        'norm_mix_pre': gain((DEPTH, D)),
        'norm_mix_post': gain((DEPTH, D)),
        'norm_ffn_pre': gain((DEPTH, D)),
        'norm_ffn_post': gain((DEPTH, D)),
        'w_in': nrm((DEPTH, D, D_IN), D ** -0.5),
        'w_gla_gate': nrm((DEPTH, GLA_GATE_RANK, GLA_HEADS * GLA_DK), GLA_GATE_RANK ** -0.5),
        'b_gla_gate': 1.0 + nrm((DEPTH, GLA_HEADS * GLA_DK), 0.1),
        'gla_norm': gain((DEPTH, GLA_HEADS * GLA_DV)),
        'pool_w': nrm((DEPTH, POOL_GROUPS, POOL_GROUP, POOL_GROUP), POOL_GROUP ** -0.5),
        'pool_scale': gain((DEPTH, POOL_WIDTH)),
        'ret_norm': gain((DEPTH, RET_HEADS * RET_DV)),
        'sgu_norm': gain((DEPTH, SGU_WIDTH)),
        'sgu_w': nrm((DEPTH, SGU_GROUPS, SGU_CHUNK, SGU_CHUNK), SGU_CHUNK ** -0.5),
        'sgu_b': 1.0 + nrm((DEPTH, SGU_GROUPS, SGU_CHUNK), 0.1),
        'w_branch': nrm((DEPTH, N_BRANCHES, BRANCH_WIDTH, D), BRANCH_WIDTH ** -0.5),
        'w_merge_gate': nrm((DEPTH, D, N_BRANCHES * D), D ** -0.5),
        'b_merge_gate': nrm((DEPTH, N_BRANCHES * D), 0.1),
        'w_out': nrm((DEPTH, D, D), D ** -0.5),
        'router_w': nrm((DEPTH, D, N_EXPERTS), D ** -0.5),
        'router_bias': nrm((DEPTH, N_EXPERTS), 0.01),
        'expert_w1': nrm((DEPTH, N_EXPERTS, D, D_EXPERT), D ** -0.5),
        'expert_w3': nrm((DEPTH, N_EXPERTS, D, D_EXPERT), D ** -0.5),
        'expert_w2': nrm((DEPTH, N_EXPERTS, D_EXPERT, D), D_EXPERT ** -0.5),
        'shared_w1': nrm((DEPTH, D, D_EXPERT), D ** -0.5),
        'shared_w3': nrm((DEPTH, D, D_EXPERT), D ** -0.5),
        'shared_w2': nrm((DEPTH, D_EXPERT, D), D_EXPERT ** -0.5),
    }


def reference(x_prompt, x_sample, c_prompt, c_sample, state_gla, state_pool, state_ret,
              w_ada, b_ada, norm_mix_pre, norm_mix_post, norm_ffn_pre, norm_ffn_post,
              w_in, w_gla_gate, b_gla_gate, gla_norm, pool_w, pool_scale, ret_norm,
              sgu_norm, sgu_w, sgu_b, w_branch, w_merge_gate, b_merge_gate, w_out,
              router_w, router_bias, expert_w1, expert_w3, expert_w2,
              shared_w1, shared_w3, shared_w2):
    weights = (w_ada, b_ada, norm_mix_pre, norm_mix_post, norm_ffn_pre, norm_ffn_post,
               w_in, w_gla_gate, b_gla_gate, gla_norm, pool_w, pool_scale, ret_norm,
               sgu_norm, sgu_w, sgu_b, w_branch, w_merge_gate, b_merge_gate, w_out,
               router_w, router_bias, expert_w1, expert_w3, expert_w2,
               shared_w1, shared_w3, shared_w2)
    B, T, _ = x_prompt.shape
    zeros_gla = jnp.zeros((DEPTH, B, GLA_HEADS, GLA_DK, GLA_DV), jnp.float32)
    zeros_pool = jnp.zeros((DEPTH, B, POOL_BUF, POOL_WIDTH), x_prompt.dtype)
    zeros_ret = jnp.zeros((DEPTH, B, RET_HEADS, RET_DK, RET_DV), jnp.float32)
    y_prompt, gla_prompt, pool_prompt, ret_prompt, _ = run_group(
        x_prompt, c_prompt, jnp.arange(T), 0, zeros_gla, zeros_pool, zeros_ret, weights)
    Ts = x_sample.shape[1]
    y_sample, gla_sample, pool_sample, ret_sample, sgu_v_sample = run_group(
        x_sample, c_sample, PAST_LEN + jnp.arange(Ts), PAST_LEN, state_gla, state_pool, state_ret, weights)
    return (y_prompt, y_sample, gla_prompt, gla_sample, pool_prompt, pool_sample,
            ret_prompt, ret_sample, sgu_v_sample)
```

```python
import functools

import jax
import jax.numpy as jnp
from jax import lax
from jax.experimental import pallas as pl
from jax.experimental.pallas import tpu as pltpu

F32 = jnp.float32
BF16 = jnp.bfloat16
HIGHEST = lax.Precision.HIGHEST

D = 2048
B_P, T_P = 4, 2048
N_P = B_P * T_P
N_S = 128
N = N_P + N_S
DEPTH = 2
PAST_LEN = 16384
EPS = 1e-6
HEADS, DK, DV = 4, 64, 128
CHUNK = 64
GATE_TEMP = 16.0
POOL_WINDOWS = (2, 4, 8, 16)
POOL_BUF = 15
ROPE_BASE = 10000.0
N_EXPERTS = 64
TOP_K = 8
D_EXPERT = 512
ROUTED_SCALE = 2.5

LANES = 128
SUB = 128
MOD_ROWS = 256
EXP_BLOCK = 256
N_ASSIGN = N * TOP_K
N_BLOCKS = -(-(N_ASSIGN + N_EXPERTS * (EXP_BLOCK - 1)) // EXP_BLOCK)
L_SLOTS = N_BLOCKS * EXP_BLOCK
VMEM_LIMIT = 56 * 1024 * 1024

C_GQ, C_GK, C_GV, C_GR, C_PIN, C_RQ, C_RK, C_RV, C_RG, C_SU, C_SV = (
    0, 256, 512, 1024, 1536, 2048, 2304, 2560, 3072, 3584, 4096)
P_MAIN = 4608


def _cparams(n_axes=1):
    return pltpu.CompilerParams(dimension_semantics=("arbitrary",) * n_axes,
                                vmem_limit_bytes=VMEM_LIMIT)


def _silu(x):
    return x * jax.nn.sigmoid(x)


def _mod_rows(t, mp_ref, ms_ref):
    b = jnp.minimum(t // (T_P // SUB), B_P - 1)
    return jnp.where(t >= N_P // SUB, ms_ref[...], mp_ref[pl.ds(b, 1), :])


def _mod_specs(layer, part):
    return [pl.BlockSpec((None, 8, D), lambda i, l=layer, p=part: (l, 0, p)),
            pl.BlockSpec((None, SUB, D), lambda i, l=layer, p=part: (l, 1, p))]


def _pack_bf16_pair(lo, hi):
    lo_u = lax.bitcast_convert_type(lo.astype(BF16).astype(F32), jnp.uint32)
    hi_u = lax.bitcast_convert_type(hi.astype(BF16).astype(F32), jnp.uint32)
    return (hi_u & jnp.uint32(0xFFFF0000)) | (lo_u >> 16)


def _unpack_bf16_pair(u):
    lo = lax.bitcast_convert_type(u << 16, F32)
    hi = lax.bitcast_convert_type(u & jnp.uint32(0xFFFF0000), F32)
    return lo, hi


ROW_TILE = (8, LANES)


def _load_row_tiles(ref):
    return jnp.concatenate([ref[:, c, :] for c in range(ROW_TILE[0])], axis=1)


def _store_row_tiles(ref, val):
    for c in range(ROW_TILE[0]):
        ref[:, c, :] = val[:, c * LANES:(c + 1) * LANES]


def _ada_body(c_ref, w_ref, b_ref, o_ref):
    s = _silu(c_ref[...]).astype(BF16)
    o_ref[...] = jnp.dot(s, w_ref[...].astype(BF16), preferred_element_type=F32) + b_ref[...]


def _ada(c_all, w_ada, b_ada):
    tn = 1024
    return pl.pallas_call(
        _ada_body,
        grid=(DEPTH, 6 * D // tn),
        in_specs=[pl.BlockSpec((MOD_ROWS, D), lambda l, j: (0, 0)),
                  pl.BlockSpec((None, D, tn), lambda l, j: (l, 0, j)),
                  pl.BlockSpec((None, 1, tn), lambda l, j: (l, 0, j))],
        out_specs=pl.BlockSpec((None, MOD_ROWS, tn), lambda l, j: (l, 0, j)),
        out_shape=jax.ShapeDtypeStruct((DEPTH, MOD_ROWS, 6 * D), F32),
        compiler_params=_cparams(2), name="ada")(c_all, w_ada, b_ada.reshape(DEPTH, 1, 6 * D))


def _prenorm_body(x_ref, g_ref, shp_ref, shs_ref, scp_ref, scs_ref, h_ref, *extra):
    t = pl.program_id(0)
    x = x_ref[...]
    y = x * lax.rsqrt(jnp.mean(x * x, axis=-1, keepdims=True) + EPS) * g_ref[...]
    h = y * (1.0 + _mod_rows(t, scp_ref, scs_ref)) + _mod_rows(t, shp_ref, shs_ref)
    h_ref[...] = h.astype(BF16)
    if extra:
        hf_ref, hp_ref = extra
        hf_ref[...] = h
        _store_row_tiles(hp_ref, _pack_bf16_pair(h[:, :D // 2], h[:, D // 2:]))


def _prenorm(x, g, mod, layer, shift_part, scale_part, with_extra):
    out_shape = [jax.ShapeDtypeStruct((N, D), BF16)]
    out_specs = [pl.BlockSpec((SUB, D), lambda i: (i, 0))]
    if with_extra:
        out_shape += [jax.ShapeDtypeStruct((N, D), F32), jax.ShapeDtypeStruct((N,) + ROW_TILE, jnp.uint32)]
        out_specs += [pl.BlockSpec((SUB, D), lambda i: (i, 0)), pl.BlockSpec((SUB,) + ROW_TILE, lambda i: (i, 0, 0))]
    return pl.pallas_call(
        _prenorm_body,
        grid=(N // SUB,),
        in_specs=[pl.BlockSpec((SUB, D), lambda i: (i, 0)),
                  pl.BlockSpec((1, D), lambda i: (0, 0))]
        + _mod_specs(layer, shift_part) + _mod_specs(layer, scale_part),
        out_specs=out_specs, out_shape=out_shape,
        compiler_params=_cparams(1), name="prenorm")(x, g.reshape(1, D), mod, mod, mod, mod)


def _mm_body(x_ref, w_ref, o_ref):
    o_ref[...] = jnp.dot(x_ref[...], w_ref[...], preferred_element_type=F32).astype(o_ref.dtype)


def _matmul(x, w, tm, tn, out_dtype=F32, name="mm"):
    m, k = x.shape
    n = w.shape[1]
    return pl.pallas_call(
        _mm_body,
        grid=(m // tm, n // tn),
        in_specs=[pl.BlockSpec((tm, k), lambda i, j: (i, 0)),
                  pl.BlockSpec((k, tn), lambda i, j: (0, j))],
        out_specs=pl.BlockSpec((tm, tn), lambda i, j: (i, j)),
        out_shape=jax.ShapeDtypeStruct((m, n), out_dtype),
        compiler_params=_cparams(2), name=name)(x, w)


ROWS_LA = 256


def _swap_halves_lanes(x):
    lane = lax.broadcasted_iota(jnp.int32, x.shape, 1)
    return jnp.where((lane % 64) < 32, pltpu.roll(x, 96, 1), pltpu.roll(x, 32, 1))


def _rope_lanes(x, cos, sin_signed):
    parts = []
    for half in range(2):
        xh = x[:, half * LANES:(half + 1) * LANES]
        parts.append(xh * cos + _swap_halves_lanes(xh) * sin_signed)
    return jnp.concatenate(parts, axis=1)


def _la_prompt_body(q_ref, k_ref, v_ref, r_ref, aux_ref, aux2_ref, dec_ref, bias_ref, g_ref,
                    o_ref, st_out_ref, st_ref, *, retention):
    t = pl.program_id(1)

    @pl.when(t == 0)
    def _():
        st_ref[...] = jnp.zeros_like(st_ref)

    ri = lax.broadcasted_iota(jnp.int32, (CHUNK, CHUNK), 0)
    ci = lax.broadcasted_iota(jnp.int32, (CHUNK, CHUNK), 1)
    causal = ri >= ci
    tril = causal.astype(F32)
    scale = DK ** -0.5

    for c in range(ROWS_LA // CHUNK):
        rows = pl.ds(c * CHUNK, CHUNK)
        q = q_ref[rows, :]
        k = k_ref[rows, :]
        v = v_ref[rows, :]
        if retention:
            cos = aux_ref[rows, :]
            sin = aux2_ref[rows, :]
            q = _rope_lanes(q, cos, sin)
            k = _rope_lanes(k, cos, sin) * scale
            la = jnp.broadcast_to(dec_ref[...], (CHUNK, HEADS * DK))
        else:
            q = q * scale
            logit = jnp.dot(aux_ref[rows, :], dec_ref[...], precision=HIGHEST,
                            preferred_element_type=F32) + bias_ref[...]
            la = jax.nn.log_sigmoid(logit) / GATE_TEMP
        bc = jnp.dot(tril, la, precision=HIGHEST, preferred_element_type=F32)
        bl = bc[CHUNK - 1:CHUNK, :]
        qd = q * jnp.exp(bc)
        ki = k * jnp.exp(-bc)
        ke = k * jnp.exp(bl - bc)
        ac = jnp.exp(bl)
        outs = []
        for h in range(HEADS):
            ks = slice(h * DK, (h + 1) * DK)
            vs = slice(h * DV, (h + 1) * DV)
            qd_h = qd[:, ks].astype(BF16)
            ki_h = ki[:, ks].astype(BF16)
            ke_h = ke[:, ks].astype(BF16)
            v_h = v[:, vs].astype(BF16)
            sc = lax.dot_general(qd_h, ki_h, (((1,), (1,)), ((), ())), preferred_element_type=F32)
            sc = jnp.where(causal, sc, 0.0)
            o_h = jnp.dot(sc.astype(BF16), v_h, preferred_element_type=F32)
            st = st_ref[h]
            o_h = o_h + lax.dot_general(qd_h, st.astype(BF16), (((1,), (1,)), ((), ())),
                                        preferred_element_type=F32)
            kv_t = lax.dot_general(v_h, ke_h, (((0,), (0,)), ((), ())), preferred_element_type=F32)
            st_ref[h] = st * ac[:, ks] + kv_t
            o_n = o_h * lax.rsqrt(jnp.mean(o_h * o_h, axis=-1, keepdims=True) + EPS) * g_ref[:, vs]
            outs.append(o_n)
        o = jnp.concatenate(outs, axis=1) * _silu(r_ref[rows, :])
        o_ref[rows, :] = o.astype(BF16)

    st_out_ref[...] = st_ref[...]


def _la_prompt(p_main, cq, ck, cv, cr, aux, aux2, dec, bias, g, retention):
    nt = T_P // ROWS_LA
    rowblk = lambda b, t: b * nt + t
    if retention:
        aux_specs = [pl.BlockSpec((ROWS_LA, LANES), lambda b, t: (t, 0)),
                     pl.BlockSpec((ROWS_LA, LANES), lambda b, t: (t, 0))]
    else:
        aux_specs = [pl.BlockSpec((ROWS_LA, LANES), lambda b, t: (rowblk(b, t), 0)),
                     pl.BlockSpec((8, LANES), lambda b, t: (0, 0))]
    o, st = pl.pallas_call(
        functools.partial(_la_prompt_body, retention=retention),
        grid=(B_P, nt),
        in_specs=[pl.BlockSpec((ROWS_LA, 256), lambda b, t: (rowblk(b, t), cq // 256)),
                  pl.BlockSpec((ROWS_LA, 256), lambda b, t: (rowblk(b, t), ck // 256)),
                  pl.BlockSpec((ROWS_LA, 512), lambda b, t: (rowblk(b, t), cv // 512)),
                  pl.BlockSpec((ROWS_LA, 512), lambda b, t: (rowblk(b, t), cr // 512))]
        + aux_specs
        + [pl.BlockSpec(dec.shape, lambda b, t: (0, 0)),
           pl.BlockSpec((1, HEADS * DK), lambda b, t: (0, 0)),
           pl.BlockSpec((1, HEADS * DV), lambda b, t: (0, 0))],
        out_specs=[pl.BlockSpec((ROWS_LA, HEADS * DV), lambda b, t: (rowblk(b, t), 0)),
                   pl.BlockSpec((None, HEADS, DV, DK), lambda b, t: (b, 0, 0, 0))],
        out_shape=[jax.ShapeDtypeStruct((N_P, HEADS * DV), BF16),
                   jax.ShapeDtypeStruct((B_P, HEADS, DV, DK), F32)],
        scratch_shapes=[pltpu.VMEM((HEADS, DV, DK), F32)],
        compiler_params=_cparams(2), name="ret_prompt" if retention else "gla_prompt",
    )(p_main, p_main, p_main, p_main, aux, aux2, dec, bias, g)
    return o, jnp.swapaxes(st, -1, -2)


SAMPLE_TILE = 8


def _la_sample_body(qt_ref, kt_ref, lt_ref, cos_ref, sin_ref, v_ref, r_ref, g_ref, s_ref,
                    o_ref, s_out_ref, *, retention):
    scale = DK ** -0.5
    qt = qt_ref[...]
    kt = kt_ref[...]
    if retention:
        def rope(x):
            sw = jnp.concatenate(
                [x[h * DK + (DK // 2) * (1 - j): h * DK + (DK // 2) * (2 - j), :]
                 for h in range(HEADS) for j in range(2)], axis=0)
            return x * cos_ref[...] + sw * sin_ref[...]
        qt = rope(qt)
        kt = rope(kt) * scale
        la = lt_ref[...]
    else:
        qt = qt * scale
        la = jax.nn.log_sigmoid(lt_ref[...]) / GATE_TEMP
    at = jnp.exp(la)
    qd = qt * at
    ki = kt * jnp.exp(-la)
    prod = qd * ki
    v8 = v_ref[...]
    r8 = r_ref[...]
    g = g_ref[...]
    for j in range(SAMPLE_TILE):
        for h in range(HEADS):
            ks = slice(h * DK, (h + 1) * DK)
            vs = slice(h * DV, (h + 1) * DV)
            a_c = jnp.broadcast_to(at[ks, j:j + 1], (DK, DV))
            k_c = jnp.broadcast_to(kt[ks, j:j + 1], (DK, DV))
            q_c = jnp.broadcast_to(qd[ks, j:j + 1], (DK, DV))
            s_c = jnp.broadcast_to(jnp.sum(prod[ks, j:j + 1], axis=0, keepdims=True), (1, DV))
            s0 = s_ref[j, h]
            v_row = v8[j:j + 1, vs]
            s_out_ref[j, h] = a_c * s0 + k_c * v_row
            o_row = s_c * v_row + jnp.sum(q_c * s0, axis=0, keepdims=True)
            o_n = o_row * lax.rsqrt(jnp.mean(o_row * o_row, axis=-1, keepdims=True) + EPS) * g[:, vs]
            o_ref[j:j + 1, vs] = o_n * _silu(r8[j:j + 1, vs])


def _la_sample(qt, kt, lt, cos_t, sin_t, p_main, cv, cr, g, s0, retention):
    nt = N_S // SAMPLE_TILE
    row0 = N_P // SAMPLE_TILE
    tile = pl.BlockSpec((None, HEADS * DK, LANES), lambda i: (i, 0, 0))
    full = pl.BlockSpec((HEADS * DK, LANES), lambda i: (0, 0))
    lt_spec = full if retention else tile
    return pl.pallas_call(
        functools.partial(_la_sample_body, retention=retention),
        grid=(nt,),
        in_specs=[tile, tile, lt_spec, full, full,
                  pl.BlockSpec((SAMPLE_TILE, 512), lambda i: (row0 + i, cv // 512)),
                  pl.BlockSpec((SAMPLE_TILE, 512), lambda i: (row0 + i, cr // 512)),
                  pl.BlockSpec((1, HEADS * DV), lambda i: (0, 0)),
                  pl.BlockSpec((SAMPLE_TILE, HEADS, DK, DV), lambda i: (i, 0, 0, 0))],
        out_specs=[pl.BlockSpec((SAMPLE_TILE, HEADS * DV), lambda i: (i, 0)),
                   pl.BlockSpec((SAMPLE_TILE, HEADS, DK, DV), lambda i: (i, 0, 0, 0))],
        out_shape=[jax.ShapeDtypeStruct((N_S, HEADS * DV), F32),
                   jax.ShapeDtypeStruct((N_S, HEADS, DK, DV), F32)],
        compiler_params=_cparams(1), name="ret_sample" if retention else "gla_sample",
    )(qt, kt, lt, cos_t, sin_t, p_main, p_main, g, s0)


def _gate_logits_t_body(w_ref, x_ref, b_ref, o_ref):
    o_ref[...] = jnp.dot(w_ref[...], x_ref[...], precision=HIGHEST, preferred_element_type=F32) + b_ref[...]


def _gate_logits_t(w_gate_t, glow_t, b_col):
    nt = N_S // SAMPLE_TILE
    return pl.pallas_call(
        _gate_logits_t_body,
        grid=(nt,),
        in_specs=[pl.BlockSpec((HEADS * DK, LANES), lambda i: (0, 0)),
                  pl.BlockSpec((None, LANES, LANES), lambda i: (i, 0, 0)),
                  pl.BlockSpec((HEADS * DK, LANES), lambda i: (0, 0))],
        out_specs=pl.BlockSpec((None, HEADS * DK, LANES), lambda i: (i, 0, 0)),
        out_shape=jax.ShapeDtypeStruct((nt, HEADS * DK, LANES), F32),
        compiler_params=_cparams(1), name="gate_logits_t")(w_gate_t, glow_t, b_col)


def _to_tiles_t(x):
    c = x.shape[1]
    xt = jnp.swapaxes(x.reshape(N_S // SAMPLE_TILE, SAMPLE_TILE, c), 1, 2)
    return jnp.pad(xt, ((0, 0), (0, 0), (0, LANES - SAMPLE_TILE)))


ROWS_POOL = 512


def _pool_mix(y, w_ref, sc_ref):
    outs = []
    for gi in range(4):
        cs = slice(gi * LANES, (gi + 1) * LANES)
        outs.append(jnp.dot(y[:, cs].astype(BF16), w_ref[gi], preferred_element_type=F32))
    return jnp.concatenate(outs, axis=1) * sc_ref[...]


def _pool_prompt_body(p_ref, halo_ref, w_ref, sc_ref, o_ref):
    t = pl.program_id(1)
    p = p_ref[...]
    halo = jnp.where(t == 0, 0.0, halo_ref[...])
    full = jnp.concatenate([halo, p], axis=0)
    pos = t * ROWS_POOL + lax.broadcasted_iota(jnp.int32, (ROWS_POOL, LANES), 0)
    means = []
    for gi, w in enumerate(POOL_WINDOWS):
        s = full[:, gi * LANES:(gi + 1) * LANES]
        step = 1
        while step < w:
            s = s + pltpu.roll(s, step, 0)
            step *= 2
        win = s[16:, :]
        cnt = jnp.minimum(w, pos + 1).astype(F32)
        means.append(win / cnt)
    y = jnp.concatenate(means, axis=1) - p
    o_ref[...] = _pool_mix(y, w_ref, sc_ref).astype(BF16)


def _pool_prompt(p_main, w_bf, scale):
    nt = T_P // ROWS_POOL
    return pl.pallas_call(
        _pool_prompt_body,
        grid=(B_P, nt),
        in_specs=[pl.BlockSpec((ROWS_POOL, 512), lambda b, t: (b * nt + t, C_PIN // 512)),
                  pl.BlockSpec((16, 512), lambda b, t: (jnp.maximum((b * nt + t) * (ROWS_POOL // 16) - 1, 0),
                                                        C_PIN // 512)),
                  pl.BlockSpec((4, LANES, LANES), lambda b, t: (0, 0, 0)),
                  pl.BlockSpec((1, 512), lambda b, t: (0, 0))],
        out_specs=pl.BlockSpec((ROWS_POOL, 512), lambda b, t: (b * nt + t, 0)),
        out_shape=jax.ShapeDtypeStruct((N_P, 512), BF16),
        compiler_params=_cparams(2), name="pool_prompt")(p_main, p_main, w_bf, scale)


def _small_sample_body(p_ref, buf_ref, pw_ref, psc_ref, u_ref, sv_ref, sg_ref, sw_ref, sb_ref,
                       ob_ref, od_ref, vn_ref):
    p = p_ref[...]
    means = []
    for gi, w in enumerate(POOL_WINDOWS):
        cs = slice(gi * LANES, (gi + 1) * LANES)
        s = p[:, cs]
        for j in range(1, w):
            s = s + buf_ref[:, POOL_BUF - j, cs]
        means.append(s / float(min(w, PAST_LEN + 1)))
    y = jnp.concatenate(means, axis=1) - p
    ob_ref[...] = _pool_mix(y, pw_ref, psc_ref)
    sv = sv_ref[...]
    vn = sv * lax.rsqrt(jnp.mean(sv * sv, axis=-1, keepdims=True) + EPS) * sg_ref[...]
    vn_ref[...] = vn
    od_ref[...] = u_ref[...] * (sw_ref[...] * vn + sb_ref[...])


def _small_sample(p_main, buf, pw_bf, pscale, sgu_g, sgu_w0, sgu_b0):
    row = N_P // N_S
    col = lambda c: pl.BlockSpec((N_S, 512), lambda i, c=c: (row, c // 512))
    vec = pl.BlockSpec((1, 512), lambda i: (0, 0))
    return pl.pallas_call(
        _small_sample_body,
        grid=(1,),
        in_specs=[col(C_PIN), pl.BlockSpec((N_S, POOL_BUF, 512), lambda i: (0, 0, 0)),
                  pl.BlockSpec((4, LANES, LANES), lambda i: (0, 0, 0)), vec,
                  col(C_SU), col(C_SV), vec, vec, vec],
        out_specs=[pl.BlockSpec((N_S, 512), lambda i: (0, 0))] * 3,
        out_shape=[jax.ShapeDtypeStruct((N_S, 512), F32)] * 3,
        compiler_params=_cparams(1), name="small_sample",
    )(p_main, buf, pw_bf, pscale, p_main, p_main, sgu_g, sgu_w0, sgu_b0)


ROWS_SGU = 512
SGU_CHUNK = 128


def _sgu_prompt_body(u_ref, v_ref, g_ref, w_ref, bt_ref, o_ref):
    ri = lax.broadcasted_iota(jnp.int32, (SGU_CHUNK, SGU_CHUNK), 0)
    ci = lax.broadcasted_iota(jnp.int32, (SGU_CHUNK, SGU_CHUNK), 1)
    causal = ri >= ci
    for c in range(ROWS_SGU // SGU_CHUNK):
        rows = pl.ds(c * SGU_CHUNK, SGU_CHUNK)
        v = v_ref[rows, :]
        vn = (v * lax.rsqrt(jnp.mean(v * v, axis=-1, keepdims=True) + EPS) * g_ref[...]).astype(BF16)
        outs = []
        for gi in range(4):
            cs = slice(gi * LANES, (gi + 1) * LANES)
            w = jnp.where(causal, w_ref[gi], 0.0).astype(BF16)
            mixed = jnp.dot(w, vn[:, cs], preferred_element_type=F32)
            outs.append(mixed + jnp.broadcast_to(bt_ref[:, gi:gi + 1], (SGU_CHUNK, LANES)))
        o_ref[rows, :] = (u_ref[rows, :] * jnp.concatenate(outs, axis=1)).astype(BF16)


def _sgu_prompt(p_main, g, w, b_t):
    return pl.pallas_call(
        _sgu_prompt_body,
        grid=(N_P // ROWS_SGU,),
        in_specs=[pl.BlockSpec((ROWS_SGU, 512), lambda i: (i, C_SU // 512)),
                  pl.BlockSpec((ROWS_SGU, 512), lambda i: (i, C_SV // 512)),
                  pl.BlockSpec((1, 512), lambda i: (0, 0)),
                  pl.BlockSpec((4, SGU_CHUNK, SGU_CHUNK), lambda i: (0, 0, 0)),
                  pl.BlockSpec((SGU_CHUNK, LANES), lambda i: (0, 0))],
        out_specs=pl.BlockSpec((ROWS_SGU, 512), lambda i: (i, 0)),
        out_shape=jax.ShapeDtypeStruct((N_P, 512), BF16),
        compiler_params=_cparams(1), name="sgu_prompt")(p_main, p_main, g, w, b_t)


TM_MERGE = 640
TN_MERGE = 512


def _merge_body(h_ref, ba_ref, bb_ref, bc_ref, bd_ref, g0, g1, g2, g3, u0, u1, u2, u3,
                c0, c1, c2, c3, o_ref):
    h = h_ref[...]
    acc = None
    for br, gw, uw, gb in ((ba_ref, g0, u0, c0), (bb_ref, g1, u1, c1), (bc_ref, g2, u2, c2), (bd_ref, g3, u3, c3)):
        gate = jax.nn.sigmoid(jnp.dot(h, gw[...], preferred_element_type=F32) + gb[...])
        up = jnp.dot(br[...], uw[...], preferred_element_type=F32)
        acc = gate * up if acc is None else acc + gate * up
    o_ref[...] = acc.astype(BF16)


def _merge(h, branches, w_mg, b_mg, w_br):
    nj = D // TN_MERGE
    row = lambda w: pl.BlockSpec((TM_MERGE, w), lambda i, j: (i, 0))
    gate_w = [pl.BlockSpec((D, TN_MERGE), lambda i, j, b=b: (0, b * nj + j)) for b in range(4)]
    up_w = [pl.BlockSpec((None, 512, TN_MERGE), lambda i, j, b=b: (b, 0, j)) for b in range(4)]
    gate_b = [pl.BlockSpec((1, TN_MERGE), lambda i, j, b=b: (0, b * nj + j)) for b in range(4)]
    return pl.pallas_call(
        _merge_body,
        grid=(N // TM_MERGE, nj),
        in_specs=[row(D)] + [row(512)] * 4 + gate_w + up_w + gate_b,
        out_specs=pl.BlockSpec((TM_MERGE, TN_MERGE), lambda i, j: (i, j)),
        out_shape=jax.ShapeDtypeStruct((N, D), BF16),
        compiler_params=_cparams(2), name="merge",
    )(h, *branches, w_mg, w_mg, w_mg, w_mg, w_br, w_br, w_br, w_br, b_mg, b_mg, b_mg, b_mg)


TM_OUT = 640
TN_OUT = 512


def _post_residual(t0, x_ref, get_y, gn_ref, gp_ref, gs_ref, o_ref, rows_total):
    for s in range(rows_total // SUB):
        rows = pl.ds(s * SUB, SUB)
        y = get_y(rows)
        yn = y * lax.rsqrt(jnp.mean(y * y, axis=-1, keepdims=True) + EPS) * gn_ref[...]
        o_ref[rows, :] = x_ref[rows, :] + _mod_rows(t0 + s, gp_ref, gs_ref) * yn


def _outproj_body(m_ref, w_ref, x_ref, gn_ref, gp_ref, gs_ref, o_ref, acc_ref):
    j = pl.program_id(1)
    acc_ref[j] = jnp.dot(m_ref[...], w_ref[...], preferred_element_type=F32)

    @pl.when(j == D // TN_OUT - 1)
    def _():
        get_y = lambda rows: jnp.concatenate([acc_ref[c, rows, :] for c in range(D // TN_OUT)], axis=1)
        _post_residual(pl.program_id(0) * (TM_OUT // SUB), x_ref, get_y, gn_ref, gp_ref, gs_ref,
                       o_ref, TM_OUT)


def _outproj(merged, w_out, x, g_post, mod, layer):
    mspec = [pl.BlockSpec((None, 8, D), lambda i, j, l=layer: (l, 0, 2)),
             pl.BlockSpec((None, SUB, D), lambda i, j, l=layer: (l, 1, 2))]
    return pl.pallas_call(
        _outproj_body,
        grid=(N // TM_OUT, D // TN_OUT),
        in_specs=[pl.BlockSpec((TM_OUT, D), lambda i, j: (i, 0)),
                  pl.BlockSpec((D, TN_OUT), lambda i, j: (0, j)),
                  pl.BlockSpec((TM_OUT, D), lambda i, j: (i, 0)),
                  pl.BlockSpec((1, D), lambda i, j: (0, 0))] + mspec,
        out_specs=pl.BlockSpec((TM_OUT, D), lambda i, j: (i, 0)),
        out_shape=jax.ShapeDtypeStruct((N, D), F32),
        scratch_shapes=[pltpu.VMEM((D // TN_OUT, TM_OUT, TN_OUT), F32)],
        compiler_params=_cparams(2), name="outproj",
    )(merged, w_out, x, g_post.reshape(1, D), mod, mod)


def _group_butterfly(x, lane, op):
    for s in (1, 2, 4):
        other = jnp.where((lane & s) == 0, pltpu.roll(x, LANES - s, 1), pltpu.roll(x, s, 1))
        x = op(x, other)
    return x


def _router_body(h_ref, w_ref, b_ref, eid_ref, pos_ref, wt_ref, cnt_ref, run_ref):
    i = pl.program_id(0)

    @pl.when(i == 0)
    def _():
        run_ref[...] = jnp.zeros_like(run_ref)

    shape = (SUB, LANES)
    lane = lax.broadcasted_iota(jnp.int32, shape, 1)
    e_lane = lane % N_EXPERTS
    neg = -jnp.inf
    scores = jax.nn.sigmoid(jnp.dot(h_ref[...], w_ref[...], precision=HIGHEST, preferred_element_type=F32))
    sel = scores + b_ref[...]
    gmax = _group_butterfly(sel, lane, jnp.maximum)
    first = _group_butterfly(jnp.where(sel == gmax, lane, 4 * LANES), lane, jnp.minimum)
    gmax2 = _group_butterfly(jnp.where(lane == first, neg, sel), lane, jnp.maximum)
    gscore = gmax + gmax2
    grp = e_lane // 8
    grank = jnp.zeros(shape, jnp.int32)
    for s in range(1, 8):
        other = pltpu.roll(gscore, LANES - 8 * s, 1)
        wrapped = (grp + s) >= 8
        grank += ((other > gscore) | ((other == gscore) & wrapped)).astype(jnp.int32)
    masked = jnp.where(grank < 4, sel, neg)
    rank = jnp.zeros(shape, jnp.int32)
    for s in range(1, N_EXPERTS):
        other = pltpu.roll(masked, LANES - s, 1)
        wrapped = (e_lane + s) >= N_EXPERTS
        rank += ((other > masked) | ((other == masked) & wrapped)).astype(jnp.int32)
    chosen = (rank < TOP_K) & (lane < N_EXPERTS)
    w_sel = jnp.where(chosen, scores, 0.0)
    w_sel = w_sel / jnp.sum(w_sel, axis=-1, keepdims=True) * ROUTED_SCALE
    ri = lax.broadcasted_iota(jnp.int32, (SUB, SUB), 0)
    ci = lax.broadcasted_iota(jnp.int32, (SUB, SUB), 1)
    onehot = chosen.astype(BF16)
    pos = jnp.dot((ri > ci).astype(BF16), onehot, preferred_element_type=F32) + run_ref[...]
    run_ref[...] = run_ref[...] + jnp.sum(chosen.astype(F32), axis=0, keepdims=True)
    cnt_ref[...] = run_ref[...]
    eid = jnp.zeros(shape, F32)
    posc = jnp.zeros(shape, F32)
    wtc = jnp.zeros(shape, F32)
    lane_f = lane.astype(F32)
    for kk in range(TOP_K):
        m = chosen & (rank == kk)
        eid = jnp.where(lane == kk, jnp.sum(jnp.where(m, lane_f, 0.0), axis=-1, keepdims=True), eid)
        posc = jnp.where(lane == kk, jnp.sum(jnp.where(m, pos, 0.0), axis=-1, keepdims=True), posc)
        wtc = jnp.where(lane == kk, jnp.sum(jnp.where(m, w_sel, 0.0), axis=-1, keepdims=True), wtc)
    eid_ref[...] = eid.astype(jnp.int32)
    pos_ref[...] = posc.astype(jnp.int32)
    wt_ref[...] = wtc


def _router(h_f32, rw_dup, rb_dup):
    tile = pl.BlockSpec((SUB, LANES), lambda i: (i, 0))
    return pl.pallas_call(
        _router_body,
        grid=(N // SUB,),
        in_specs=[pl.BlockSpec((SUB, D), lambda i: (i, 0)),
                  pl.BlockSpec((D, LANES), lambda i: (0, 0)),
                  pl.BlockSpec((1, LANES), lambda i: (0, 0))],
        out_specs=[tile, tile, tile, pl.BlockSpec((1, LANES), lambda i: (0, 0))],
        out_shape=[jax.ShapeDtypeStruct((N, LANES), jnp.int32), jax.ShapeDtypeStruct((N, LANES), jnp.int32),
                   jax.ShapeDtypeStruct((N, LANES), F32), jax.ShapeDtypeStruct((1, LANES), F32)],
        scratch_shapes=[pltpu.VMEM((1, LANES), F32)],
        compiler_params=_cparams(1), name="router")(h_f32, rw_dup, rb_dup)


def _dispatch_body(tail_ref, slot_ref, hp_ref, xs_ref, zero_ref, sem, zsem):
    i = pl.program_id(0)

    @pl.when(i == 0)
    def _():
        zero_ref[...] = jnp.zeros_like(zero_ref)

        def zstart(e, c):
            @pl.when(tail_ref[e] >= 0)
            def _():
                pltpu.make_async_copy(zero_ref, xs_ref.at[pl.ds(tail_ref[e], EXP_BLOCK)], zsem).start()
            return c
        lax.fori_loop(0, N_EXPERTS + N_BLOCKS, zstart, 0)

        def zwait(e, c):
            @pl.when(tail_ref[e] >= 0)
            def _():
                pltpu.make_async_copy(zero_ref, xs_ref.at[pl.ds(0, EXP_BLOCK)], zsem).wait()
            return c
        lax.fori_loop(0, N_EXPERTS + N_BLOCKS, zwait, 0)

    def issue(a, c):
        r = a // TOP_K
        pltpu.make_async_copy(hp_ref.at[pl.ds(r, 1)], xs_ref.at[pl.ds(slot_ref[0, 0, a], 1)], sem).start()
        return c
    lax.fori_loop(0, SUB * TOP_K, issue, 0)
    for _ in range(TOP_K):
        pltpu.make_async_copy(hp_ref, xs_ref.at[pl.ds(0, SUB)], sem).wait()


def _dispatch(tails, slots, h_packed):
    grid_spec = pltpu.PrefetchScalarGridSpec(
        num_scalar_prefetch=1,
        grid=(N // SUB,),
        in_specs=[pl.BlockSpec((1, 1, SUB * TOP_K), lambda i, *_: (i, 0, 0), memory_space=pltpu.SMEM),
                  pl.BlockSpec((SUB,) + ROW_TILE, lambda i, *_: (i, 0, 0))],
        out_specs=pl.BlockSpec(memory_space=pl.ANY),
        scratch_shapes=[pltpu.VMEM((EXP_BLOCK,) + ROW_TILE, jnp.uint32),
                        pltpu.SemaphoreType.DMA, pltpu.SemaphoreType.DMA])
    return pl.pallas_call(
        _dispatch_body, grid_spec=grid_spec,
        out_shape=jax.ShapeDtypeStruct((L_SLOTS,) + ROW_TILE, jnp.uint32),
        compiler_params=_cparams(1), name="dispatch")(tails, slots, h_packed)


def _experts_body(be_ref, first_ref, nused_ref, x_ref, w1_ref, w3_ref, w2_ref, y_ref, w1b, w3b, w2b):
    b = pl.program_id(0)

    @pl.when(first_ref[b] == 1)
    def _():
        w1b[...] = w1_ref[...].astype(BF16)
        w3b[...] = w3_ref[...].astype(BF16)
        w2b[...] = w2_ref[...].astype(BF16)

    @pl.when(b < nused_ref[0])
    def _():
        lo, hi = _unpack_bf16_pair(_load_row_tiles(x_ref))
        lo = lo.astype(BF16)
        hi = hi.astype(BF16)
        half = D // 2
        h1 = (jnp.dot(lo, w1b[:half, :], preferred_element_type=F32)
              + jnp.dot(hi, w1b[half:, :], preferred_element_type=F32))
        h3 = (jnp.dot(lo, w3b[:half, :], preferred_element_type=F32)
              + jnp.dot(hi, w3b[half:, :], preferred_element_type=F32))
        hid = (_silu(h1) * h3).astype(BF16)
        y = jnp.dot(hid, w2b[...], preferred_element_type=F32)
        _store_row_tiles(y_ref, _pack_bf16_pair(y[:, :half], y[:, half:]))

    @pl.when(b >= nused_ref[0])
    def _():
        y_ref[...] = jnp.zeros_like(y_ref)


def _experts(block_e, first, nused, xs, w1, w3, w2, layer):
    blk = lambda b, be, fi, nu: (jnp.minimum(b, nu[0] - 1), 0, 0)
    grid_spec = pltpu.PrefetchScalarGridSpec(
        num_scalar_prefetch=3,
        grid=(N_BLOCKS,),
        in_specs=[pl.BlockSpec((EXP_BLOCK,) + ROW_TILE, blk),
                  pl.BlockSpec((None, None, D, D_EXPERT), lambda b, be, fi, nu, l=layer: (l, be[b], 0, 0)),
                  pl.BlockSpec((None, None, D, D_EXPERT), lambda b, be, fi, nu, l=layer: (l, be[b], 0, 0)),
                  pl.BlockSpec((None, None, D_EXPERT, D), lambda b, be, fi, nu, l=layer: (l, be[b], 0, 0))],
        out_specs=pl.BlockSpec((EXP_BLOCK,) + ROW_TILE, lambda b, be, fi, nu: (b, 0, 0)),
        scratch_shapes=[pltpu.VMEM((D, D_EXPERT), BF16), pltpu.VMEM((D, D_EXPERT), BF16),
                        pltpu.VMEM((D_EXPERT, D), BF16)])
    return pl.pallas_call(
        _experts_body, grid_spec=grid_spec,
        out_shape=jax.ShapeDtypeStruct((L_SLOTS,) + ROW_TILE, jnp.uint32),
        compiler_params=_cparams(1), name="experts")(block_e, first, nused, xs, w1, w3, w2)


TM_SHARED = 640


def _shared_body(h_ref, w13_ref, w2_ref, o_ref):
    up = jnp.dot(h_ref[...], w13_ref[...], preferred_element_type=F32)
    hid = (_silu(up[:, :D_EXPERT]) * up[:, D_EXPERT:]).astype(BF16)
    o_ref[...] = jnp.dot(hid, w2_ref[...], preferred_element_type=F32)


def _shared(h, w13, w2):
    return pl.pallas_call(
        _shared_body,
        grid=(N // TM_SHARED,),
        in_specs=[pl.BlockSpec((TM_SHARED, D), lambda i: (i, 0)),
                  pl.BlockSpec((D, 2 * D_EXPERT), lambda i: (0, 0)),
                  pl.BlockSpec((D_EXPERT, D), lambda i: (0, 0))],
        out_specs=pl.BlockSpec((TM_SHARED, D), lambda i: (i, 0)),
        out_shape=jax.ShapeDtypeStruct((N, D), F32),
        compiler_params=_cparams(1), name="shared")(h, w13, w2)


def _combine_body(slot_ref, wt_ref, sh_ref, x_ref, gn_ref, gp_ref, gs_ref, ys_ref, o_ref,
                  gbuf, f_ref, sem):
    def issue(a, c):
        r = a // TOP_K
        k = a % TOP_K
        pltpu.make_async_copy(ys_ref.at[pl.ds(slot_ref[0, 0, a], 1)], gbuf.at[k, pl.ds(r, 1)], sem).start()
        return c
    lax.fori_loop(0, SUB * TOP_K, issue, 0)
    for k in range(TOP_K):
        pltpu.make_async_copy(ys_ref.at[pl.ds(0, SUB)], gbuf.at[k], sem).wait()
    half = D // 2
    acc_lo = sh_ref[:, :half]
    acc_hi = sh_ref[:, half:]
    wt = wt_ref[...]
    for k in range(TOP_K):
        lo, hi = _unpack_bf16_pair(_load_row_tiles(gbuf.at[k]))
        w_c = wt[:, k:k + 1]
        acc_lo = acc_lo + w_c * lo
        acc_hi = acc_hi + w_c * hi
    f_ref[:, :half] = acc_lo
    f_ref[:, half:] = acc_hi
    _post_residual(pl.program_id(0), x_ref, lambda rows: f_ref[rows, :], gn_ref, gp_ref, gs_ref, o_ref, SUB)


def _combine(slots, wts, shared, x, g_post, mod, layer, ys):
    grid_spec = pltpu.PrefetchScalarGridSpec(
        num_scalar_prefetch=0,
        grid=(N // SUB,),
        in_specs=[pl.BlockSpec((1, 1, SUB * TOP_K), lambda i: (i, 0, 0), memory_space=pltpu.SMEM),
                  pl.BlockSpec((SUB, LANES), lambda i: (i, 0)),
                  pl.BlockSpec((SUB, D), lambda i: (i, 0)),
                  pl.BlockSpec((SUB, D), lambda i: (i, 0)),
                  pl.BlockSpec((1, D), lambda i: (0, 0)),
                  pl.BlockSpec((None, 8, D), lambda i, l=layer: (l, 0, 5)),
                  pl.BlockSpec((None, SUB, D), lambda i, l=layer: (l, 1, 5)),
                  pl.BlockSpec(memory_space=pl.ANY)],
        out_specs=pl.BlockSpec((SUB, D), lambda i: (i, 0)),
        scratch_shapes=[pltpu.VMEM((TOP_K, SUB) + ROW_TILE, jnp.uint32), pltpu.VMEM((SUB, D), F32),
                        pltpu.SemaphoreType.DMA])
    return pl.pallas_call(
        _combine_body, grid_spec=grid_spec,
        out_shape=jax.ShapeDtypeStruct((N, D), F32),
        compiler_params=_cparams(1), name="combine",
    )(slots, wts, shared, x, g_post.reshape(1, D), mod, mod, ys)


def _reorder_w_in(w):
    return jnp.concatenate([w[:, :1536], w[:, 1552:]], axis=1), jnp.pad(w[:, 1536:1552], ((0, 0), (0, LANES - 16)))


def _rope_tables(pos):
    half = DK // 2
    inv = ROPE_BASE ** (-jnp.arange(half, dtype=F32) / half)
    ang = pos.astype(F32)[:, None] * inv[None, :]
    cos = jnp.cos(ang)
    sin = jnp.sin(ang)
    return jnp.concatenate([cos, cos], axis=1), jnp.concatenate([-sin, sin], axis=1)


def _layer(l, x, mod, s_gla, s_pool, s_ret, wts):
    (norm_mix_pre, norm_mix_post, norm_ffn_pre, norm_ffn_post, w_in, w_gla_gate, b_gla_gate, gla_norm,
     pool_w, pool_scale, ret_norm, sgu_norm, sgu_w, sgu_b, w_branch, w_merge_gate, b_merge_gate, w_out,
     router_w, router_bias, expert_w1, expert_w3, expert_w2, shared_w1, shared_w3, shared_w2) = wts

    h = _prenorm(x, norm_mix_pre[l], mod, l, 0, 1, False)[0]
    w_main, w_low = _reorder_w_in(w_in[l])
    p_main = _matmul(h, w_main.astype(BF16), 1664, 512, name="inproj")
    p_low = _matmul(h, w_low.astype(BF16), 1664, LANES, name="inproj_low")

    w_gate_pad = jnp.pad(w_gla_gate[l], ((0, LANES - 16), (0, 0)))
    b_gate = b_gla_gate[l].reshape(1, HEADS * DK)
    log_gamma = jnp.log1p(-jnp.exp2(-5.0 - jnp.arange(HEADS, dtype=F32)))
    dec_row = jnp.repeat(log_gamma, DK).reshape(1, HEADS * DK)
    cos_p, sin_p = _rope_tables(jnp.arange(T_P))
    cos_p = jnp.tile(cos_p, (1, 2))
    sin_p = jnp.tile(sin_p, (1, 2))
    g_gla = gla_norm[l].reshape(1, HEADS * DV)
    g_ret = ret_norm[l].reshape(1, HEADS * DV)

    oa_p, gla_p = _la_prompt(p_main, C_GQ, C_GK, C_GV, C_GR, p_low, p_low, w_gate_pad, b_gate, g_gla, False)
    oc_p, ret_p = _la_prompt(p_main, C_RQ, C_RK, C_RV, C_RG, cos_p, sin_p, dec_row, b_gate, g_ret, True)
    pw_bf = pool_w[l].astype(BF16)
    pscale = pool_scale[l].reshape(1, 512)
    ob_p = _pool_prompt(p_main, pw_bf, pscale)
    sgu_g = sgu_norm[l].reshape(1, 512)
    od_p = _sgu_prompt(p_main, sgu_g, sgu_w[l], jnp.pad(sgu_b[l].T, ((0, 0), (0, LANES - 4))))

    ps = p_main[N_P:]
    q_t = _to_tiles_t(ps[:, C_GQ:C_GQ + 256])
    k_t = _to_tiles_t(ps[:, C_GK:C_GK + 256])
    glow_t = jnp.pad(_to_tiles_t(p_low[N_P:, :16]), ((0, 0), (0, LANES - 16), (0, 0)))
    w_gate_t = jnp.pad(w_gla_gate[l].T, ((0, 0), (0, LANES - 16)))
    b_col = jnp.broadcast_to(b_gla_gate[l][:, None], (HEADS * DK, LANES))
    logit_t = _gate_logits_t(w_gate_t, glow_t, b_col)
    dummy = jnp.zeros((HEADS * DK, LANES), F32)
    oa_s, gla_s = _la_sample(q_t, k_t, logit_t, dummy, dummy, p_main, C_GV, C_GR, g_gla, s_gla[l], False)
    cos_s, sin_s = _rope_tables(jnp.full((1,), PAST_LEN))
    cos_c = jnp.broadcast_to(jnp.tile(cos_s[0], HEADS)[:, None], (HEADS * DK, LANES))
    sin_c = jnp.broadcast_to(jnp.tile(sin_s[0], HEADS)[:, None], (HEADS * DK, LANES))
    dec_c = jnp.broadcast_to(jnp.repeat(log_gamma, DK)[:, None], (HEADS * DK, LANES))
    rq_t = _to_tiles_t(ps[:, C_RQ:C_RQ + 256])
    rk_t = _to_tiles_t(ps[:, C_RK:C_RK + 256])
    oc_s, ret_s = _la_sample(rq_t, rk_t, dec_c, cos_c, sin_c, p_main, C_RV, C_RG, g_ret, s_ret[l], True)
    sgu_w0 = jnp.repeat(sgu_w[l][:, 0, 0], LANES).reshape(1, 512)
    sgu_b0 = jnp.repeat(sgu_b[l][:, 0], LANES).reshape(1, 512)
    ob_s, od_s, vn_s = _small_sample(p_main, s_pool[l], pw_bf, pscale, sgu_g, sgu_w0, sgu_b0)
    pool_p = p_main[:N_P, C_PIN:C_PIN + 512].reshape(B_P, T_P, 512)[:, T_P - POOL_BUF:]
    pool_s = jnp.concatenate([s_pool[l][:, 1:], ps[:, None, C_PIN:C_PIN + 512]], axis=1)

    branches = [jnp.concatenate([p, s.astype(BF16)], axis=0)
                for p, s in ((oa_p, oa_s), (ob_p, ob_s), (oc_p, oc_s), (od_p, od_s))]
    merged = _merge(h, branches, w_merge_gate[l].astype(BF16), b_merge_gate[l].reshape(1, 4 * D),
                    w_branch[l].astype(BF16))
    x = _outproj(merged, w_out[l].astype(BF16), x, norm_mix_post[l], mod, l)

    h2, h2_f32, h2_packed = _prenorm(x, norm_ffn_pre[l], mod, l, 3, 4, True)
    rw = jnp.concatenate([router_w[l], router_w[l]], axis=1)
    rb = jnp.concatenate([router_bias[l], router_bias[l]]).reshape(1, LANES)
    eid, pos, wt, counts = _router(h2_f32, rw, rb)
    counts = counts[0, :N_EXPERTS].astype(jnp.int32)
    padded = (counts + EXP_BLOCK - 1) // EXP_BLOCK * EXP_BLOCK
    pad_end = jnp.cumsum(padded)
    pad_start = pad_end - padded
    nused = (pad_end[-1] // EXP_BLOCK).astype(jnp.int32).reshape(1)
    blk_row = jnp.arange(N_BLOCKS, dtype=jnp.int32) * EXP_BLOCK
    tails = jnp.concatenate([jnp.where(padded > 0, pad_end - EXP_BLOCK, -1),
                             jnp.where(blk_row >= pad_end[-1], blk_row, -1)]).astype(jnp.int32)
    block_e = jnp.minimum(jnp.searchsorted(pad_end, blk_row, side='right'), N_EXPERTS - 1).astype(jnp.int32)
    first = jnp.concatenate([jnp.ones((1,), jnp.int32), (block_e[1:] != block_e[:-1]).astype(jnp.int32)])
    slots = (pad_start.astype(jnp.int32)[eid[:, :TOP_K]] + pos[:, :TOP_K]).reshape(N // SUB, 1, SUB * TOP_K)
    xs = _dispatch(tails, slots, h2_packed)
    ys = _experts(block_e, first, nused, xs, expert_w1, expert_w3, expert_w2, l)
    w13 = jnp.concatenate([shared_w1[l], shared_w3[l]], axis=1).astype(BF16)
    shared = _shared(h2, w13, shared_w2[l].astype(BF16))
    x = _combine(slots, wt, shared, x, norm_ffn_post[l], mod, l, ys)
    return x, (gla_p, gla_s, pool_p, pool_s, ret_p, ret_s, vn_s)


def kernel(x_prompt, x_sample, c_prompt, c_sample, state_gla, state_pool, state_ret, w_ada, b_ada, norm_mix_pre, norm_mix_post, norm_ffn_pre, norm_ffn_post, w_in, w_gla_gate, b_gla_gate, gla_norm, pool_w, pool_scale, ret_norm, sgu_norm, sgu_w, sgu_b, w_branch, w_merge_gate, b_merge_gate, w_out, router_w, router_bias, expert_w1, expert_w3, expert_w2, shared_w1, shared_w3, shared_w2):
    wts = (norm_mix_pre, norm_mix_post, norm_ffn_pre, norm_ffn_post, w_in, w_gla_gate, b_gla_gate, gla_norm,
           pool_w, pool_scale, ret_norm, sgu_norm, sgu_w, sgu_b, w_branch, w_merge_gate, b_merge_gate, w_out,
           router_w, router_bias, expert_w1, expert_w3, expert_w2, shared_w1, shared_w3, shared_w2)
    c_all = jnp.zeros((MOD_ROWS, D), F32).at[:B_P].set(c_prompt).at[SUB:SUB + N_S].set(c_sample)
    mod = _ada(c_all, w_ada, b_ada)
    x = jnp.concatenate([x_prompt.reshape(N_P, D), x_sample.reshape(N_S, D)], axis=0)
    per_layer = []
    for l in range(DEPTH):
        x, states = _layer(l, x, mod, state_gla, state_pool, state_ret, wts)
        per_layer.append(states)
    gla_p, gla_s, pool_p, pool_s, ret_p, ret_s, vn_s = (jnp.stack(z) for z in zip(*per_layer))
    return (x[:N_P].reshape(B_P, T_P, D), x[N_P:].reshape(N_S, 1, D),
            gla_p, gla_s, pool_p, pool_s, ret_p, ret_s, vn_s.reshape(DEPTH, N_S, 1, 512))
```

```python
import functools

import jax
import jax.numpy as jnp
from jax import lax
from jax.experimental import pallas as pl
from jax.experimental.pallas import tpu as pltpu

F32 = jnp.float32
BF16 = jnp.bfloat16
HIGHEST = lax.Precision.HIGHEST

D = 2048
B_P, T_P = 4, 2048
N_P = B_P * T_P
N_S = 128
N = N_P + N_S
DEPTH = 2
PAST_LEN = 16384
EPS = 1e-6
HEADS, DK, DV = 4, 64, 128
CHUNK = 64
GATE_TEMP = 16.0
POOL_WINDOWS = (2, 4, 8, 16)
POOL_BUF = 15
ROPE_BASE = 10000.0
N_EXPERTS = 64
TOP_K = 8
D_EXPERT = 512
ROUTED_SCALE = 2.5

LANES = 128
SUB = 128
MOD_ROWS = 256
EXP_BLOCK = 256
N_ASSIGN = N * TOP_K
N_BLOCKS = -(-(N_ASSIGN + N_EXPERTS * (EXP_BLOCK - 1)) // EXP_BLOCK)
L_SLOTS = N_BLOCKS * EXP_BLOCK
VMEM_LIMIT = 56 * 1024 * 1024

C_GQ, C_GK, C_GV, C_GR, C_PIN, C_RQ, C_RK, C_RV, C_RG, C_SU, C_SV = (
    0, 256, 512, 1024, 1536, 2048, 2304, 2560, 3072, 3584, 4096)
P_MAIN = 4608


def _cparams(n_axes=1):
    return pltpu.CompilerParams(dimension_semantics=("arbitrary",) * n_axes,
                                vmem_limit_bytes=VMEM_LIMIT)


def _silu(x):
    return x * jax.nn.sigmoid(x)


def _mod_rows(t, mp_ref, ms_ref):
    b = jnp.minimum(t // (T_P // SUB), B_P - 1)
    return jnp.where(t >= N_P // SUB, ms_ref[...], mp_ref[pl.ds(b, 1), :])


def _mod_specs(layer, part):
    return [pl.BlockSpec((None, 8, D), lambda i, l=layer, p=part: (l, 0, p)),
            pl.BlockSpec((None, SUB, D), lambda i, l=layer, p=part: (l, 1, p))]


def _pack_bf16_pair(lo, hi):
    lo_u = lax.bitcast_convert_type(lo.astype(BF16).astype(F32), jnp.uint32)
    hi_u = lax.bitcast_convert_type(hi.astype(BF16).astype(F32), jnp.uint32)
    return (hi_u & jnp.uint32(0xFFFF0000)) | (lo_u >> 16)


def _unpack_bf16_pair(u):
    lo = lax.bitcast_convert_type(u << 16, F32)
    hi = lax.bitcast_convert_type(u & jnp.uint32(0xFFFF0000), F32)
    return lo, hi


ROW_TILE = (8, LANES)


def _load_row_tiles(ref):
    return jnp.concatenate([ref[:, c, :] for c in range(ROW_TILE[0])], axis=1)


def _store_row_tiles(ref, val):
    for c in range(ROW_TILE[0]):
        ref[:, c, :] = val[:, c * LANES:(c + 1) * LANES]


def _load_row_tiles_2d(ref, rows):
    return jnp.concatenate([ref[pl.ds(c, rows, stride=ROW_TILE[0]), :] for c in range(ROW_TILE[0])], axis=1)


def _store_row_tiles_2d(ref, val, rows):
    for c in range(ROW_TILE[0]):
        ref[pl.ds(c, rows, stride=ROW_TILE[0]), :] = val[:, c * LANES:(c + 1) * LANES]


def _ada_body(c_ref, w_ref, b_ref, o_ref):
    s = _silu(c_ref[...]).astype(BF16)
    o_ref[...] = jnp.dot(s, w_ref[...].astype(BF16), preferred_element_type=F32) + b_ref[...]


def _ada(c_all, w_ada, b_ada):
    tn = 1024
    return pl.pallas_call(
        _ada_body,
        grid=(DEPTH, 6 * D // tn),
        in_specs=[pl.BlockSpec((MOD_ROWS, D), lambda l, j: (0, 0)),
                  pl.BlockSpec((None, D, tn), lambda l, j: (l, 0, j)),
                  pl.BlockSpec((None, 1, tn), lambda l, j: (l, 0, j))],
        out_specs=pl.BlockSpec((None, MOD_ROWS, tn), lambda l, j: (l, 0, j)),
        out_shape=jax.ShapeDtypeStruct((DEPTH, MOD_ROWS, 6 * D), F32),
        compiler_params=_cparams(2), name="ada")(c_all, w_ada, b_ada.reshape(DEPTH, 1, 6 * D))


def _prenorm_body(x_ref, g_ref, shp_ref, shs_ref, scp_ref, scs_ref, h_ref, *extra):
    t = pl.program_id(0)
    x = x_ref[...]
    y = x * lax.rsqrt(jnp.mean(x * x, axis=-1, keepdims=True) + EPS) * g_ref[...]
    h = y * (1.0 + _mod_rows(t, scp_ref, scs_ref)) + _mod_rows(t, shp_ref, shs_ref)
    h_ref[...] = h.astype(BF16)
    if extra:
        hf_ref, hp_ref = extra
        hf_ref[...] = h
        _store_row_tiles(hp_ref, _pack_bf16_pair(h[:, :D // 2], h[:, D // 2:]))


def _prenorm(x, g, mod, layer, shift_part, scale_part, with_extra):
    out_shape = [jax.ShapeDtypeStruct((N, D), BF16)]
    out_specs = [pl.BlockSpec((SUB, D), lambda i: (i, 0))]
    if with_extra:
        out_shape += [jax.ShapeDtypeStruct((N, D), F32), jax.ShapeDtypeStruct((N,) + ROW_TILE, jnp.uint32)]
        out_specs += [pl.BlockSpec((SUB, D), lambda i: (i, 0)), pl.BlockSpec((SUB,) + ROW_TILE, lambda i: (i, 0, 0))]
    return pl.pallas_call(
        _prenorm_body,
        grid=(N // SUB,),
        in_specs=[pl.BlockSpec((SUB, D), lambda i: (i, 0)),
                  pl.BlockSpec((1, D), lambda i: (0, 0))]
        + _mod_specs(layer, shift_part) + _mod_specs(layer, scale_part),
        out_specs=out_specs, out_shape=out_shape,
        compiler_params=_cparams(1), name="prenorm")(x, g.reshape(1, D), mod, mod, mod, mod)


def _mm_body(x_ref, w_ref, o_ref):
    o_ref[...] = jnp.dot(x_ref[...], w_ref[...], preferred_element_type=F32).astype(o_ref.dtype)


def _matmul(x, w, tm, tn, out_dtype=F32, name="mm"):
    m, k = x.shape
    n = w.shape[1]
    return pl.pallas_call(
        _mm_body,
        grid=(m // tm, n // tn),
        in_specs=[pl.BlockSpec((tm, k), lambda i, j: (i, 0)),
                  pl.BlockSpec((k, tn), lambda i, j: (0, j))],
        out_specs=pl.BlockSpec((tm, tn), lambda i, j: (i, j)),
        out_shape=jax.ShapeDtypeStruct((m, n), out_dtype),
        compiler_params=_cparams(2), name=name)(x, w)


ROWS_LA = 256


def _swap_halves_lanes(x):
    lane = lax.broadcasted_iota(jnp.int32, x.shape, 1)
    return jnp.where((lane % 64) < 32, pltpu.roll(x, 96, 1), pltpu.roll(x, 32, 1))


def _rope_lanes(x, cos, sin_signed):
    parts = []
    for half in range(2):
        xh = x[:, half * LANES:(half + 1) * LANES]
        parts.append(xh * cos + _swap_halves_lanes(xh) * sin_signed)
    return jnp.concatenate(parts, axis=1)


def _la_prompt_body(q_ref, k_ref, v_ref, r_ref, aux_ref, aux2_ref, dec_ref, bias_ref, g_ref,
                    o_ref, st_out_ref, st_ref, *, retention):
    t = pl.program_id(1)

    @pl.when(t == 0)
    def _():
        st_ref[...] = jnp.zeros_like(st_ref)

    ri = lax.broadcasted_iota(jnp.int32, (CHUNK, CHUNK), 0)
    ci = lax.broadcasted_iota(jnp.int32, (CHUNK, CHUNK), 1)
    causal = ri >= ci
    tril = causal.astype(F32)
    scale = DK ** -0.5

    for c in range(ROWS_LA // CHUNK):
        rows = pl.ds(c * CHUNK, CHUNK)
        q = q_ref[rows, :]
        k = k_ref[rows, :]
        v = v_ref[rows, :]
        if retention:
            cos = aux_ref[rows, :]
            sin = aux2_ref[rows, :]
            q = _rope_lanes(q, cos, sin)
            k = _rope_lanes(k, cos, sin) * scale
            la = jnp.broadcast_to(dec_ref[...], (CHUNK, HEADS * DK))
        else:
            q = q * scale
            logit = jnp.dot(aux_ref[rows, :], dec_ref[...], precision=HIGHEST,
                            preferred_element_type=F32) + bias_ref[...]
            la = jax.nn.log_sigmoid(logit) / GATE_TEMP
        bc = jnp.dot(tril, la, precision=HIGHEST, preferred_element_type=F32)
        bl = bc[CHUNK - 1:CHUNK, :]
        qd = q * jnp.exp(bc)
        ki = k * jnp.exp(-bc)
        ke = k * jnp.exp(bl - bc)
        ac = jnp.exp(bl)
        outs = []
        for h in range(HEADS):
            ks = slice(h * DK, (h + 1) * DK)
            vs = slice(h * DV, (h + 1) * DV)
            qd_h = qd[:, ks].astype(BF16)
            ki_h = ki[:, ks].astype(BF16)
            ke_h = ke[:, ks].astype(BF16)
            v_h = v[:, vs].astype(BF16)
            sc = lax.dot_general(qd_h, ki_h, (((1,), (1,)), ((), ())), preferred_element_type=F32)
            sc = jnp.where(causal, sc, 0.0)
            o_h = jnp.dot(sc.astype(BF16), v_h, preferred_element_type=F32)
            st = st_ref[h]
            o_h = o_h + lax.dot_general(qd_h, st.astype(BF16), (((1,), (1,)), ((), ())),
                                        preferred_element_type=F32)
            kv_t = lax.dot_general(v_h, ke_h, (((0,), (0,)), ((), ())), preferred_element_type=F32)
            st_ref[h] = st * ac[:, ks] + kv_t
            o_n = o_h * lax.rsqrt(jnp.mean(o_h * o_h, axis=-1, keepdims=True) + EPS) * g_ref[:, vs]
            outs.append(o_n)
        o = jnp.concatenate(outs, axis=1) * _silu(r_ref[rows, :])
        o_ref[rows, :] = o.astype(BF16)

    st_out_ref[...] = st_ref[...]


def _la_prompt(p_main, cq, ck, cv, cr, aux, aux2, dec, bias, g, retention):
    nt = T_P // ROWS_LA
    rowblk = lambda b, t: b * nt + t
    if retention:
        aux_specs = [pl.BlockSpec((ROWS_LA, LANES), lambda b, t: (t, 0)),
                     pl.BlockSpec((ROWS_LA, LANES), lambda b, t: (t, 0))]
    else:
        aux_specs = [pl.BlockSpec((ROWS_LA, LANES), lambda b, t: (rowblk(b, t), 0)),
                     pl.BlockSpec((8, LANES), lambda b, t: (0, 0))]
    o, st = pl.pallas_call(
        functools.partial(_la_prompt_body, retention=retention),
        grid=(B_P, nt),
        in_specs=[pl.BlockSpec((ROWS_LA, 256), lambda b, t: (rowblk(b, t), cq // 256)),
                  pl.BlockSpec((ROWS_LA, 256), lambda b, t: (rowblk(b, t), ck // 256)),
                  pl.BlockSpec((ROWS_LA, 512), lambda b, t: (rowblk(b, t), cv // 512)),
                  pl.BlockSpec((ROWS_LA, 512), lambda b, t: (rowblk(b, t), cr // 512))]
        + aux_specs
        + [pl.BlockSpec(dec.shape, lambda b, t: (0, 0)),
           pl.BlockSpec((1, HEADS * DK), lambda b, t: (0, 0)),
           pl.BlockSpec((1, HEADS * DV), lambda b, t: (0, 0))],
        out_specs=[pl.BlockSpec((ROWS_LA, HEADS * DV), lambda b, t: (rowblk(b, t), 0)),
                   pl.BlockSpec((None, HEADS, DV, DK), lambda b, t: (b, 0, 0, 0))],
        out_shape=[jax.ShapeDtypeStruct((N_P, HEADS * DV), BF16),
                   jax.ShapeDtypeStruct((B_P, HEADS, DV, DK), F32)],
        scratch_shapes=[pltpu.VMEM((HEADS, DV, DK), F32)],
        compiler_params=_cparams(2), name="ret_prompt" if retention else "gla_prompt",
    )(p_main, p_main, p_main, p_main, aux, aux2, dec, bias, g)
    return o, jnp.swapaxes(st, -1, -2)


SAMPLE_TILE = 8


def _la_sample_body(qt_ref, kt_ref, lt_ref, cos_ref, sin_ref, v_ref, r_ref, g_ref, s_ref,
                    o_ref, s_out_ref, *, retention):
    scale = DK ** -0.5
    qt = qt_ref[...]
    kt = kt_ref[...]
    if retention:
        def rope(x):
            sw = jnp.concatenate(
                [x[h * DK + (DK // 2) * (1 - j): h * DK + (DK // 2) * (2 - j), :]
                 for h in range(HEADS) for j in range(2)], axis=0)
            return x * cos_ref[...] + sw * sin_ref[...]
        qt = rope(qt)
        kt = rope(kt) * scale
        la = lt_ref[...]
    else:
        qt = qt * scale
        la = jax.nn.log_sigmoid(lt_ref[...]) / GATE_TEMP
    at = jnp.exp(la)
    qd = qt * at
    ki = kt * jnp.exp(-la)
    prod = qd * ki
    v8 = v_ref[...]
    r8 = r_ref[...]
    g = g_ref[...]
    for j in range(SAMPLE_TILE):
        for h in range(HEADS):
            ks = slice(h * DK, (h + 1) * DK)
            vs = slice(h * DV, (h + 1) * DV)
            a_c = jnp.broadcast_to(at[ks, j:j + 1], (DK, DV))
            k_c = jnp.broadcast_to(kt[ks, j:j + 1], (DK, DV))
            q_c = jnp.broadcast_to(qd[ks, j:j + 1], (DK, DV))
            s_c = jnp.broadcast_to(jnp.sum(prod[ks, j:j + 1], axis=0, keepdims=True), (1, DV))
            s0 = s_ref[j, h]
            v_row = v8[j:j + 1, vs]
            s_out_ref[j, h] = a_c * s0 + k_c * v_row
            o_row = s_c * v_row + jnp.sum(q_c * s0, axis=0, keepdims=True)
            o_n = o_row * lax.rsqrt(jnp.mean(o_row * o_row, axis=-1, keepdims=True) + EPS) * g[:, vs]
            o_ref[j:j + 1, vs] = o_n * _silu(r8[j:j + 1, vs])


def _la_sample(qt, kt, lt, cos_t, sin_t, p_main, cv, cr, g, s0, retention):
    nt = N_S // SAMPLE_TILE
    row0 = N_P // SAMPLE_TILE
    tile = pl.BlockSpec((None, HEADS * DK, LANES), lambda i: (i, 0, 0))
    full = pl.BlockSpec((HEADS * DK, LANES), lambda i: (0, 0))
    lt_spec = full if retention else tile
    return pl.pallas_call(
        functools.partial(_la_sample_body, retention=retention),
        grid=(nt,),
        in_specs=[tile, tile, lt_spec, full, full,
                  pl.BlockSpec((SAMPLE_TILE, 512), lambda i: (row0 + i, cv // 512)),
                  pl.BlockSpec((SAMPLE_TILE, 512), lambda i: (row0 + i, cr // 512)),
                  pl.BlockSpec((1, HEADS * DV), lambda i: (0, 0)),
                  pl.BlockSpec((SAMPLE_TILE, HEADS, DK, DV), lambda i: (i, 0, 0, 0))],
        out_specs=[pl.BlockSpec((SAMPLE_TILE, HEADS * DV), lambda i: (i, 0)),
                   pl.BlockSpec((SAMPLE_TILE, HEADS, DK, DV), lambda i: (i, 0, 0, 0))],
        out_shape=[jax.ShapeDtypeStruct((N_S, HEADS * DV), F32),
                   jax.ShapeDtypeStruct((N_S, HEADS, DK, DV), F32)],
        compiler_params=_cparams(1), name="ret_sample" if retention else "gla_sample",
    )(qt, kt, lt, cos_t, sin_t, p_main, p_main, g, s0)


def _gate_logits_t_body(w_ref, x_ref, b_ref, o_ref):
    o_ref[...] = jnp.dot(w_ref[...], x_ref[...], precision=HIGHEST, preferred_element_type=F32) + b_ref[...]


def _gate_logits_t(w_gate_t, glow_t, b_col):
    nt = N_S // SAMPLE_TILE
    return pl.pallas_call(
        _gate_logits_t_body,
        grid=(nt,),
        in_specs=[pl.BlockSpec((HEADS * DK, LANES), lambda i: (0, 0)),
                  pl.BlockSpec((None, LANES, LANES), lambda i: (i, 0, 0)),
                  pl.BlockSpec((HEADS * DK, LANES), lambda i: (0, 0))],
        out_specs=pl.BlockSpec((None, HEADS * DK, LANES), lambda i: (i, 0, 0)),
        out_shape=jax.ShapeDtypeStruct((nt, HEADS * DK, LANES), F32),
        compiler_params=_cparams(1), name="gate_logits_t")(w_gate_t, glow_t, b_col)


def _to_tiles_t(x):
    c = x.shape[1]
    xt = jnp.swapaxes(x.reshape(N_S // SAMPLE_TILE, SAMPLE_TILE, c), 1, 2)
    return jnp.pad(xt, ((0, 0), (0, 0), (0, LANES - SAMPLE_TILE)))


ROWS_POOL = 512


def _pool_mix(y, w_ref, sc_ref):
    outs = []
    for gi in range(4):
        cs = slice(gi * LANES, (gi + 1) * LANES)
        outs.append(jnp.dot(y[:, cs].astype(BF16), w_ref[gi], preferred_element_type=F32))
    return jnp.concatenate(outs, axis=1) * sc_ref[...]


def _pool_prompt_body(p_ref, halo_ref, w_ref, sc_ref, o_ref):
    t = pl.program_id(1)
    p = p_ref[...]
    halo = jnp.where(t == 0, 0.0, halo_ref[...])
    full = jnp.concatenate([halo, p], axis=0)
    pos = t * ROWS_POOL + lax.broadcasted_iota(jnp.int32, (ROWS_POOL, LANES), 0)
    means = []
    for gi, w in enumerate(POOL_WINDOWS):
        s = full[:, gi * LANES:(gi + 1) * LANES]
        step = 1
        while step < w:
            s = s + pltpu.roll(s, step, 0)
            step *= 2
        win = s[16:, :]
        cnt = jnp.minimum(w, pos + 1).astype(F32)
        means.append(win / cnt)
    y = jnp.concatenate(means, axis=1) - p
    o_ref[...] = _pool_mix(y, w_ref, sc_ref).astype(BF16)


def _pool_prompt(p_main, w_bf, scale):
    nt = T_P // ROWS_POOL
    return pl.pallas_call(
        _pool_prompt_body,
        grid=(B_P, nt),
        in_specs=[pl.BlockSpec((ROWS_POOL, 512), lambda b, t: (b * nt + t, C_PIN // 512)),
                  pl.BlockSpec((16, 512), lambda b, t: (jnp.maximum((b * nt + t) * (ROWS_POOL // 16) - 1, 0),
                                                        C_PIN // 512)),
                  pl.BlockSpec((4, LANES, LANES), lambda b, t: (0, 0, 0)),
                  pl.BlockSpec((1, 512), lambda b, t: (0, 0))],
        out_specs=pl.BlockSpec((ROWS_POOL, 512), lambda b, t: (b * nt + t, 0)),
        out_shape=jax.ShapeDtypeStruct((N_P, 512), BF16),
        compiler_params=_cparams(2), name="pool_prompt")(p_main, p_main, w_bf, scale)


def _small_sample_body(p_ref, buf_ref, pw_ref, psc_ref, u_ref, sv_ref, sg_ref, sw_ref, sb_ref,
                       ob_ref, od_ref, vn_ref):
    p = p_ref[...]
    means = []
    for gi, w in enumerate(POOL_WINDOWS):
        cs = slice(gi * LANES, (gi + 1) * LANES)
        s = p[:, cs]
        for j in range(1, w):
            s = s + buf_ref[:, POOL_BUF - j, cs]
        means.append(s / float(min(w, PAST_LEN + 1)))
    y = jnp.concatenate(means, axis=1) - p
    ob_ref[...] = _pool_mix(y, pw_ref, psc_ref)
    sv = sv_ref[...]
    vn = sv * lax.rsqrt(jnp.mean(sv * sv, axis=-1, keepdims=True) + EPS) * sg_ref[...]
    vn_ref[...] = vn
    od_ref[...] = u_ref[...] * (sw_ref[...] * vn + sb_ref[...])


def _small_sample(p_main, buf, pw_bf, pscale, sgu_g, sgu_w0, sgu_b0):
    row = N_P // N_S
    col = lambda c: pl.BlockSpec((N_S, 512), lambda i, c=c: (row, c // 512))
    vec = pl.BlockSpec((1, 512), lambda i: (0, 0))
    return pl.pallas_call(
        _small_sample_body,
        grid=(1,),
        in_specs=[col(C_PIN), pl.BlockSpec((N_S, POOL_BUF, 512), lambda i: (0, 0, 0)),
                  pl.BlockSpec((4, LANES, LANES), lambda i: (0, 0, 0)), vec,
                  col(C_SU), col(C_SV), vec, vec, vec],
        out_specs=[pl.BlockSpec((N_S, 512), lambda i: (0, 0))] * 3,
        out_shape=[jax.ShapeDtypeStruct((N_S, 512), F32)] * 3,
        compiler_params=_cparams(1), name="small_sample",
    )(p_main, buf, pw_bf, pscale, p_main, p_main, sgu_g, sgu_w0, sgu_b0)


ROWS_SGU = 512
SGU_CHUNK = 128


def _sgu_prompt_body(u_ref, v_ref, g_ref, w_ref, bt_ref, o_ref):
    ri = lax.broadcasted_iota(jnp.int32, (SGU_CHUNK, SGU_CHUNK), 0)
    ci = lax.broadcasted_iota(jnp.int32, (SGU_CHUNK, SGU_CHUNK), 1)
    causal = ri >= ci
    for c in range(ROWS_SGU // SGU_CHUNK):
        rows = pl.ds(c * SGU_CHUNK, SGU_CHUNK)
        v = v_ref[rows, :]
        vn = (v * lax.rsqrt(jnp.mean(v * v, axis=-1, keepdims=True) + EPS) * g_ref[...]).astype(BF16)
        outs = []
        for gi in range(4):
            cs = slice(gi * LANES, (gi + 1) * LANES)
            w = jnp.where(causal, w_ref[gi], 0.0).astype(BF16)
            mixed = jnp.dot(w, vn[:, cs], preferred_element_type=F32)
            outs.append(mixed + jnp.broadcast_to(bt_ref[:, gi:gi + 1], (SGU_CHUNK, LANES)))
        o_ref[rows, :] = (u_ref[rows, :] * jnp.concatenate(outs, axis=1)).astype(BF16)


def _sgu_prompt(p_main, g, w, b_t):
    return pl.pallas_call(
        _sgu_prompt_body,
        grid=(N_P // ROWS_SGU,),
        in_specs=[pl.BlockSpec((ROWS_SGU, 512), lambda i: (i, C_SU // 512)),
                  pl.BlockSpec((ROWS_SGU, 512), lambda i: (i, C_SV // 512)),
                  pl.BlockSpec((1, 512), lambda i: (0, 0)),
                  pl.BlockSpec((4, SGU_CHUNK, SGU_CHUNK), lambda i: (0, 0, 0)),
                  pl.BlockSpec((SGU_CHUNK, LANES), lambda i: (0, 0))],
        out_specs=pl.BlockSpec((ROWS_SGU, 512), lambda i: (i, 0)),
        out_shape=jax.ShapeDtypeStruct((N_P, 512), BF16),
        compiler_params=_cparams(1), name="sgu_prompt")(p_main, p_main, g, w, b_t)


TM_MERGE = 640
TN_MERGE = 512


def _merge_body(h_ref, ba_ref, bb_ref, bc_ref, bd_ref, g0, g1, g2, g3, u0, u1, u2, u3,
                c0, c1, c2, c3, o_ref):
    h = h_ref[...]
    acc = None
    for br, gw, uw, gb in ((ba_ref, g0, u0, c0), (bb_ref, g1, u1, c1), (bc_ref, g2, u2, c2), (bd_ref, g3, u3, c3)):
        gate = jax.nn.sigmoid(jnp.dot(h, gw[...], preferred_element_type=F32) + gb[...])
        up = jnp.dot(br[...], uw[...], preferred_element_type=F32)
        acc = gate * up if acc is None else acc + gate * up
    o_ref[...] = acc.astype(BF16)


def _merge(h, branches, w_mg, b_mg, w_br):
    nj = D // TN_MERGE
    row = lambda w: pl.BlockSpec((TM_MERGE, w), lambda i, j: (i, 0))
    gate_w = [pl.BlockSpec((D, TN_MERGE), lambda i, j, b=b: (0, b * nj + j)) for b in range(4)]
    up_w = [pl.BlockSpec((None, 512, TN_MERGE), lambda i, j, b=b: (b, 0, j)) for b in range(4)]
    gate_b = [pl.BlockSpec((1, TN_MERGE), lambda i, j, b=b: (0, b * nj + j)) for b in range(4)]
    return pl.pallas_call(
        _merge_body,
        grid=(N // TM_MERGE, nj),
        in_specs=[row(D)] + [row(512)] * 4 + gate_w + up_w + gate_b,
        out_specs=pl.BlockSpec((TM_MERGE, TN_MERGE), lambda i, j: (i, j)),
        out_shape=jax.ShapeDtypeStruct((N, D), BF16),
        compiler_params=_cparams(2), name="merge",
    )(h, *branches, w_mg, w_mg, w_mg, w_mg, w_br, w_br, w_br, w_br, b_mg, b_mg, b_mg, b_mg)


TM_OUT = 640
TN_OUT = 512


def _post_residual(t0, x_ref, get_y, gn_ref, gp_ref, gs_ref, o_ref, rows_total):
    for s in range(rows_total // SUB):
        rows = pl.ds(s * SUB, SUB)
        y = get_y(rows)
        yn = y * lax.rsqrt(jnp.mean(y * y, axis=-1, keepdims=True) + EPS) * gn_ref[...]
        o_ref[rows, :] = x_ref[rows, :] + _mod_rows(t0 + s, gp_ref, gs_ref) * yn


def _outproj_body(m_ref, w_ref, x_ref, gn_ref, gp_ref, gs_ref, o_ref, acc_ref):
    j = pl.program_id(1)
    acc_ref[j] = jnp.dot(m_ref[...], w_ref[...], preferred_element_type=F32)

    @pl.when(j == D // TN_OUT - 1)
    def _():
        get_y = lambda rows: jnp.concatenate([acc_ref[c, rows, :] for c in range(D // TN_OUT)], axis=1)
        _post_residual(pl.program_id(0) * (TM_OUT // SUB), x_ref, get_y, gn_ref, gp_ref, gs_ref,
                       o_ref, TM_OUT)


def _outproj(merged, w_out, x, g_post, mod, layer):
    mspec = [pl.BlockSpec((None, 8, D), lambda i, j, l=layer: (l, 0, 2)),
             pl.BlockSpec((None, SUB, D), lambda i, j, l=layer: (l, 1, 2))]
    return pl.pallas_call(
        _outproj_body,
        grid=(N // TM_OUT, D // TN_OUT),
        in_specs=[pl.BlockSpec((TM_OUT, D), lambda i, j: (i, 0)),
                  pl.BlockSpec((D, TN_OUT), lambda i, j: (0, j)),
                  pl.BlockSpec((TM_OUT, D), lambda i, j: (i, 0)),
                  pl.BlockSpec((1, D), lambda i, j: (0, 0))] + mspec,
        out_specs=pl.BlockSpec((TM_OUT, D), lambda i, j: (i, 0)),
        out_shape=jax.ShapeDtypeStruct((N, D), F32),
        scratch_shapes=[pltpu.VMEM((D // TN_OUT, TM_OUT, TN_OUT), F32)],
        compiler_params=_cparams(2), name="outproj",
    )(merged, w_out, x, g_post.reshape(1, D), mod, mod)


def _router_body(h_ref, w_ref, b_ref, eid_ref, pos_ref, wt_ref, cnt_ref, run_ref):
    i = pl.program_id(0)

    @pl.when(i == 0)
    def _():
        run_ref[...] = jnp.zeros_like(run_ref)

    ng, gs = 8, N_EXPERTS // 8
    neg = -jnp.inf
    logits = jnp.dot(h_ref[...], w_ref[...], precision=HIGHEST, preferred_element_type=F32)
    scores = jax.nn.sigmoid(logits.T[:N_EXPERTS, :])
    sel = scores + b_ref[...]
    sel3 = sel.reshape(ng, gs, SUB)
    sub3 = lax.broadcasted_iota(jnp.int32, (ng, gs, SUB), 1)
    gmax = jnp.max(sel3, axis=1, keepdims=True)
    first = jnp.min(jnp.where(sel3 == gmax, sub3, gs), axis=1, keepdims=True)
    gmax2 = jnp.max(jnp.where(sub3 == first, neg, sel3), axis=1, keepdims=True)
    gscore = (gmax + gmax2).reshape(ng, SUB)
    gidx = lax.broadcasted_iota(jnp.int32, (ng, SUB), 0)
    grank = jnp.zeros((ng, SUB), jnp.int32)
    for s in range(1, ng):
        other = pltpu.roll(gscore, s, 0)
        lower = gidx >= s
        grank += ((other > gscore) | ((other == gscore) & lower)).astype(jnp.int32)
    keep = jnp.broadcast_to((grank < 4).reshape(ng, 1, SUB), (ng, gs, SUB))
    masked = jnp.where(keep, sel3, neg).reshape(N_EXPERTS, SUB)
    eidx = lax.broadcasted_iota(jnp.int32, (N_EXPERTS, SUB), 0)
    rank = jnp.zeros((N_EXPERTS, SUB), jnp.int32)
    for s in range(1, N_EXPERTS):
        other = pltpu.roll(masked, s, 0)
        lower = eidx >= s
        rank += ((other > masked) | ((other == masked) & lower)).astype(jnp.int32)
    chosen = rank < TOP_K
    w_sel = jnp.where(chosen, scores, 0.0)
    w_sel = w_sel / jnp.sum(w_sel, axis=0, keepdims=True) * ROUTED_SCALE
    ri = lax.broadcasted_iota(jnp.int32, (SUB, SUB), 0)
    ci = lax.broadcasted_iota(jnp.int32, (SUB, SUB), 1)
    onehot = chosen.astype(BF16)
    pos = jnp.dot(onehot, (ri < ci).astype(BF16), preferred_element_type=F32) + run_ref[...]
    run_ref[...] = run_ref[...] + jnp.sum(chosen.astype(F32), axis=1, keepdims=True)
    cnt_ref[...] = run_ref[...]
    eidx_f = eidx.astype(F32)
    rows_e, rows_p, rows_w = [], [], []
    for kk in range(TOP_K):
        m = chosen & (rank == kk)
        rows_e.append(jnp.sum(jnp.where(m, eidx_f, 0.0), axis=0, keepdims=True))
        rows_p.append(jnp.sum(jnp.where(m, pos, 0.0), axis=0, keepdims=True))
        rows_w.append(jnp.sum(jnp.where(m, w_sel, 0.0), axis=0, keepdims=True))
    eid_ref[...] = jnp.concatenate(rows_e, axis=0).astype(jnp.int32)
    pos_ref[...] = jnp.concatenate(rows_p, axis=0).astype(jnp.int32)
    wt_ref[...] = jnp.concatenate(rows_w, axis=0)


def _router(h_f32, rw_pad, rb_col):
    tile = pl.BlockSpec((TOP_K, SUB), lambda i: (0, i))
    return pl.pallas_call(
        _router_body,
        grid=(N // SUB,),
        in_specs=[pl.BlockSpec((SUB, D), lambda i: (i, 0)),
                  pl.BlockSpec((D, LANES), lambda i: (0, 0)),
                  pl.BlockSpec((N_EXPERTS, SUB), lambda i: (0, 0))],
        out_specs=[tile, tile, tile, pl.BlockSpec((N_EXPERTS, SUB), lambda i: (0, 0))],
        out_shape=[jax.ShapeDtypeStruct((TOP_K, N), jnp.int32), jax.ShapeDtypeStruct((TOP_K, N), jnp.int32),
                   jax.ShapeDtypeStruct((TOP_K, N), F32), jax.ShapeDtypeStruct((N_EXPERTS, SUB), F32)],
        scratch_shapes=[pltpu.VMEM((N_EXPERTS, SUB), F32)],
        compiler_params=_cparams(1), name="router")(h_f32, rw_pad, rb_col)


def _dispatch_body(tail_ref, slot_ref, hp_ref, xs_ref, zero_ref, sem, zsem):
    i = pl.program_id(0)

    @pl.when(i == 0)
    def _():
        zero_ref[...] = jnp.zeros_like(zero_ref)

        def zstart(e, c):
            @pl.when(tail_ref[e] >= 0)
            def _():
                pltpu.make_async_copy(zero_ref, xs_ref.at[pl.ds(tail_ref[e], EXP_BLOCK)], zsem).start()
            return c
        lax.fori_loop(0, N_EXPERTS + N_BLOCKS, zstart, 0)

        def zwait(e, c):
            @pl.when(tail_ref[e] >= 0)
            def _():
                pltpu.make_async_copy(zero_ref, xs_ref.at[pl.ds(0, EXP_BLOCK)], zsem).wait()
            return c
        lax.fori_loop(0, N_EXPERTS + N_BLOCKS, zwait, 0)

    def issue(a, c):
        r = a // TOP_K
        pltpu.make_async_copy(hp_ref.at[pl.ds(r, 1)], xs_ref.at[pl.ds(slot_ref[0, 0, a], 1)], sem).start()
        return c
    lax.fori_loop(0, SUB * TOP_K, issue, 0)
    for _ in range(TOP_K):
        pltpu.make_async_copy(hp_ref, xs_ref.at[pl.ds(0, SUB)], sem).wait()


def _dispatch(tails, slots, h_packed):
    grid_spec = pltpu.PrefetchScalarGridSpec(
        num_scalar_prefetch=1,
        grid=(N // SUB,),
        in_specs=[pl.BlockSpec((1, 1, SUB * TOP_K), lambda i, *_: (i, 0, 0), memory_space=pltpu.SMEM),
                  pl.BlockSpec((SUB,) + ROW_TILE, lambda i, *_: (i, 0, 0))],
        out_specs=pl.BlockSpec(memory_space=pl.ANY),
        scratch_shapes=[pltpu.VMEM((EXP_BLOCK,) + ROW_TILE, jnp.uint32),
                        pltpu.SemaphoreType.DMA, pltpu.SemaphoreType.DMA])
    return pl.pallas_call(
        _dispatch_body, grid_spec=grid_spec,
        out_shape=jax.ShapeDtypeStruct((L_SLOTS,) + ROW_TILE, jnp.uint32),
        compiler_params=_cparams(1), name="dispatch")(tails, slots, h_packed)


def _experts_body(be_ref, first_ref, nused_ref, x_ref, w1_ref, w3_ref, w2_ref, y_ref, w1b, w3b, w2b):
    b = pl.program_id(0)

    @pl.when(first_ref[b] == 1)
    def _():
        w1b[...] = w1_ref[...].astype(BF16)
        w3b[...] = w3_ref[...].astype(BF16)
        w2b[...] = w2_ref[...].astype(BF16)

    @pl.when(b < nused_ref[0])
    def _():
        lo, hi = _unpack_bf16_pair(_load_row_tiles_2d(x_ref, EXP_BLOCK))
        lo = lo.astype(BF16)
        hi = hi.astype(BF16)
        half = D // 2
        h1 = (jnp.dot(lo, w1b[:half, :], preferred_element_type=F32)
              + jnp.dot(hi, w1b[half:, :], preferred_element_type=F32))
        h3 = (jnp.dot(lo, w3b[:half, :], preferred_element_type=F32)
              + jnp.dot(hi, w3b[half:, :], preferred_element_type=F32))
        hid = (_silu(h1) * h3).astype(BF16)
        y = jnp.dot(hid, w2b[...], preferred_element_type=F32)
        _store_row_tiles_2d(y_ref, _pack_bf16_pair(y[:, :half], y[:, half:]), EXP_BLOCK)

    @pl.when(b >= nused_ref[0])
    def _():
        y_ref[...] = jnp.zeros_like(y_ref)


def _experts(block_e, first, nused, xs, w1, w3, w2, layer):
    blk = lambda b, be, fi, nu: (jnp.minimum(b, nu[0] - 1), 0)
    grid_spec = pltpu.PrefetchScalarGridSpec(
        num_scalar_prefetch=3,
        grid=(N_BLOCKS,),
        in_specs=[pl.BlockSpec((EXP_BLOCK * ROW_TILE[0], LANES), blk),
                  pl.BlockSpec((None, None, D, D_EXPERT), lambda b, be, fi, nu, l=layer: (l, be[b], 0, 0)),
                  pl.BlockSpec((None, None, D, D_EXPERT), lambda b, be, fi, nu, l=layer: (l, be[b], 0, 0)),
                  pl.BlockSpec((None, None, D_EXPERT, D), lambda b, be, fi, nu, l=layer: (l, be[b], 0, 0))],
        out_specs=pl.BlockSpec((EXP_BLOCK * ROW_TILE[0], LANES), lambda b, be, fi, nu: (b, 0)),
        scratch_shapes=[pltpu.VMEM((D, D_EXPERT), BF16), pltpu.VMEM((D, D_EXPERT), BF16),
                        pltpu.VMEM((D_EXPERT, D), BF16)])
    return pl.pallas_call(
        _experts_body, grid_spec=grid_spec,
        out_shape=jax.ShapeDtypeStruct((L_SLOTS * ROW_TILE[0], LANES), jnp.uint32),
        compiler_params=_cparams(1), name="experts",
    )(block_e, first, nused, xs.reshape(L_SLOTS * ROW_TILE[0], LANES), w1, w3, w2).reshape((L_SLOTS,) + ROW_TILE)


TM_SHARED = 640


def _shared_body(h_ref, w13_ref, w2_ref, o_ref):
    up = jnp.dot(h_ref[...], w13_ref[...], preferred_element_type=F32)
    hid = (_silu(up[:, :D_EXPERT]) * up[:, D_EXPERT:]).astype(BF16)
    o_ref[...] = jnp.dot(hid, w2_ref[...], preferred_element_type=F32)


def _shared(h, w13, w2):
    return pl.pallas_call(
        _shared_body,
        grid=(N // TM_SHARED,),
        in_specs=[pl.BlockSpec((TM_SHARED, D), lambda i: (i, 0)),
                  pl.BlockSpec((D, 2 * D_EXPERT), lambda i: (0, 0)),
                  pl.BlockSpec((D_EXPERT, D), lambda i: (0, 0))],
        out_specs=pl.BlockSpec((TM_SHARED, D), lambda i: (i, 0)),
        out_shape=jax.ShapeDtypeStruct((N, D), F32),
        compiler_params=_cparams(1), name="shared")(h, w13, w2)


def _combine_body(slot_ref, wt_ref, sh_ref, x_ref, gn_ref, gp_ref, gs_ref, ys_ref, o_ref,
                  gbuf, f_ref, sem):
    def issue(a, c):
        r = a // TOP_K
        k = a % TOP_K
        pltpu.make_async_copy(ys_ref.at[pl.ds(slot_ref[0, 0, a], 1)], gbuf.at[k, pl.ds(r, 1)], sem).start()
        return c
    lax.fori_loop(0, SUB * TOP_K, issue, 0)
    for k in range(TOP_K):
        pltpu.make_async_copy(ys_ref.at[pl.ds(0, SUB)], gbuf.at[k], sem).wait()
    half = D // 2
    acc_lo = sh_ref[:, :half]
    acc_hi = sh_ref[:, half:]
    wt = wt_ref[...]
    for k in range(TOP_K):
        lo, hi = _unpack_bf16_pair(_load_row_tiles(gbuf.at[k]))
        w_c = wt[:, k:k + 1]
        acc_lo = acc_lo + w_c * lo
        acc_hi = acc_hi + w_c * hi
    f_ref[:, :half] = acc_lo
    f_ref[:, half:] = acc_hi
    _post_residual(pl.program_id(0), x_ref, lambda rows: f_ref[rows, :], gn_ref, gp_ref, gs_ref, o_ref, SUB)


def _combine(slots, wts, shared, x, g_post, mod, layer, ys):
    grid_spec = pltpu.PrefetchScalarGridSpec(
        num_scalar_prefetch=0,
        grid=(N // SUB,),
        in_specs=[pl.BlockSpec((1, 1, SUB * TOP_K), lambda i: (i, 0, 0), memory_space=pltpu.SMEM),
                  pl.BlockSpec((SUB, LANES), lambda i: (i, 0)),
                  pl.BlockSpec((SUB, D), lambda i: (i, 0)),
                  pl.BlockSpec((SUB, D), lambda i: (i, 0)),
                  pl.BlockSpec((1, D), lambda i: (0, 0)),
                  pl.BlockSpec((None, 8, D), lambda i, l=layer: (l, 0, 5)),
                  pl.BlockSpec((None, SUB, D), lambda i, l=layer: (l, 1, 5)),
                  pl.BlockSpec(memory_space=pl.ANY)],
        out_specs=pl.BlockSpec((SUB, D), lambda i: (i, 0)),
        scratch_shapes=[pltpu.VMEM((TOP_K, SUB) + ROW_TILE, jnp.uint32), pltpu.VMEM((SUB, D), F32),
                        pltpu.SemaphoreType.DMA])
    return pl.pallas_call(
        _combine_body, grid_spec=grid_spec,
        out_shape=jax.ShapeDtypeStruct((N, D), F32),
        compiler_params=_cparams(1), name="combine",
    )(slots, wts, shared, x, g_post.reshape(1, D), mod, mod, ys)


def _reorder_w_in(w):
    return jnp.concatenate([w[:, :1536], w[:, 1552:]], axis=1), jnp.pad(w[:, 1536:1552], ((0, 0), (0, LANES - 16)))


def _rope_tables(pos):
    half = DK // 2
    inv = ROPE_BASE ** (-jnp.arange(half, dtype=F32) / half)
    ang = pos.astype(F32)[:, None] * inv[None, :]
    cos = jnp.cos(ang)
    sin = jnp.sin(ang)
    return jnp.concatenate([cos, cos], axis=1), jnp.concatenate([-sin, sin], axis=1)


def _layer(l, x, mod, s_gla, s_pool, s_ret, wts):
    (norm_mix_pre, norm_mix_post, norm_ffn_pre, norm_ffn_post, w_in, w_gla_gate, b_gla_gate, gla_norm,
     pool_w, pool_scale, ret_norm, sgu_norm, sgu_w, sgu_b, w_branch, w_merge_gate, b_merge_gate, w_out,
     router_w, router_bias, expert_w1, expert_w3, expert_w2, shared_w1, shared_w3, shared_w2) = wts

    h = _prenorm(x, norm_mix_pre[l], mod, l, 0, 1, False)[0]
    w_main, w_low = _reorder_w_in(w_in[l])
    p_main = _matmul(h, w_main.astype(BF16), 1664, 512, name="inproj")
    p_low = _matmul(h, w_low.astype(BF16), 1664, LANES, name="inproj_low")

    w_gate_pad = jnp.pad(w_gla_gate[l], ((0, LANES - 16), (0, 0)))
    b_gate = b_gla_gate[l].reshape(1, HEADS * DK)
    log_gamma = jnp.log1p(-jnp.exp2(-5.0 - jnp.arange(HEADS, dtype=F32)))
    dec_row = jnp.repeat(log_gamma, DK).reshape(1, HEADS * DK)
    cos_p, sin_p = _rope_tables(jnp.arange(T_P))
    cos_p = jnp.tile(cos_p, (1, 2))
    sin_p = jnp.tile(sin_p, (1, 2))
    g_gla = gla_norm[l].reshape(1, HEADS * DV)
    g_ret = ret_norm[l].reshape(1, HEADS * DV)

    oa_p, gla_p = _la_prompt(p_main, C_GQ, C_GK, C_GV, C_GR, p_low, p_low, w_gate_pad, b_gate, g_gla, False)
    oc_p, ret_p = _la_prompt(p_main, C_RQ, C_RK, C_RV, C_RG, cos_p, sin_p, dec_row, b_gate, g_ret, True)
    pw_bf = pool_w[l].astype(BF16)
    pscale = pool_scale[l].reshape(1, 512)
    ob_p = _pool_prompt(p_main, pw_bf, pscale)
    sgu_g = sgu_norm[l].reshape(1, 512)
    od_p = _sgu_prompt(p_main, sgu_g, sgu_w[l], jnp.pad(sgu_b[l].T, ((0, 0), (0, LANES - 4))))

    ps = p_main[N_P:]
    q_t = _to_tiles_t(ps[:, C_GQ:C_GQ + 256])
    k_t = _to_tiles_t(ps[:, C_GK:C_GK + 256])
    glow_t = jnp.pad(_to_tiles_t(p_low[N_P:, :16]), ((0, 0), (0, LANES - 16), (0, 0)))
    w_gate_t = jnp.pad(w_gla_gate[l].T, ((0, 0), (0, LANES - 16)))
    b_col = jnp.broadcast_to(b_gla_gate[l][:, None], (HEADS * DK, LANES))
    logit_t = _gate_logits_t(w_gate_t, glow_t, b_col)
    dummy = jnp.zeros((HEADS * DK, LANES), F32)
    oa_s, gla_s = _la_sample(q_t, k_t, logit_t, dummy, dummy, p_main, C_GV, C_GR, g_gla, s_gla[l], False)
    cos_s, sin_s = _rope_tables(jnp.full((1,), PAST_LEN))
    cos_c = jnp.broadcast_to(jnp.tile(cos_s[0], HEADS)[:, None], (HEADS * DK, LANES))
    sin_c = jnp.broadcast_to(jnp.tile(sin_s[0], HEADS)[:, None], (HEADS * DK, LANES))
    dec_c = jnp.broadcast_to(jnp.repeat(log_gamma, DK)[:, None], (HEADS * DK, LANES))
    rq_t = _to_tiles_t(ps[:, C_RQ:C_RQ + 256])
    rk_t = _to_tiles_t(ps[:, C_RK:C_RK + 256])
    oc_s, ret_s = _la_sample(rq_t, rk_t, dec_c, cos_c, sin_c, p_main, C_RV, C_RG, g_ret, s_ret[l], True)
    sgu_w0 = jnp.repeat(sgu_w[l][:, 0, 0], LANES).reshape(1, 512)
    sgu_b0 = jnp.repeat(sgu_b[l][:, 0], LANES).reshape(1, 512)
    ob_s, od_s, vn_s = _small_sample(p_main, s_pool[l], pw_bf, pscale, sgu_g, sgu_w0, sgu_b0)
    pool_p = p_main[:N_P, C_PIN:C_PIN + 512].reshape(B_P, T_P, 512)[:, T_P - POOL_BUF:]
    pool_s = jnp.concatenate([s_pool[l][:, 1:], ps[:, None, C_PIN:C_PIN + 512]], axis=1)

    branches = [jnp.concatenate([p, s.astype(BF16)], axis=0)
                for p, s in ((oa_p, oa_s), (ob_p, ob_s), (oc_p, oc_s), (od_p, od_s))]
    merged = _merge(h, branches, w_merge_gate[l].astype(BF16), b_merge_gate[l].reshape(1, 4 * D),
                    w_branch[l].astype(BF16))
    x = _outproj(merged, w_out[l].astype(BF16), x, norm_mix_post[l], mod, l)

    h2, h2_f32, h2_packed = _prenorm(x, norm_ffn_pre[l], mod, l, 3, 4, True)
    rw = jnp.pad(router_w[l], ((0, 0), (0, LANES - N_EXPERTS)))
    rb = jnp.broadcast_to(router_bias[l][:, None], (N_EXPERTS, SUB))
    eid, pos, wt, counts = _router(h2_f32, rw, rb)
    counts = counts[:, 0].astype(jnp.int32)
    padded = (counts + EXP_BLOCK - 1) // EXP_BLOCK * EXP_BLOCK
    pad_end = jnp.cumsum(padded)
    pad_start = pad_end - padded
    nused = (pad_end[-1] // EXP_BLOCK).astype(jnp.int32).reshape(1)
    blk_row = jnp.arange(N_BLOCKS, dtype=jnp.int32) * EXP_BLOCK
    tails = jnp.concatenate([jnp.where(padded > 0, pad_end - EXP_BLOCK, -1),
                             jnp.where(blk_row >= pad_end[-1], blk_row, -1)]).astype(jnp.int32)
    block_e = jnp.minimum(jnp.sum((blk_row[:, None] >= pad_end[None, :]).astype(jnp.int32), axis=1),
                          N_EXPERTS - 1)
    first = jnp.concatenate([jnp.ones((1,), jnp.int32), (block_e[1:] != block_e[:-1]).astype(jnp.int32)])
    start_of = jnp.sum(jnp.where(eid[:, :, None] == jnp.arange(N_EXPERTS), pad_start.astype(jnp.int32), 0), axis=-1)
    slots = (start_of + pos).T.reshape(N // SUB, 1, SUB * TOP_K)
    wt = jnp.pad(wt.T, ((0, 0), (0, LANES - TOP_K)))
    xs = _dispatch(tails, slots, h2_packed)
    ys = _experts(block_e, first, nused, xs, expert_w1, expert_w3, expert_w2, l)
    w13 = jnp.concatenate([shared_w1[l], shared_w3[l]], axis=1).astype(BF16)
    shared = _shared(h2, w13, shared_w2[l].astype(BF16))
    x = _combine(slots, wt, shared, x, norm_ffn_post[l], mod, l, ys)
    return x, (gla_p, gla_s, pool_p, pool_s, ret_p, ret_s, vn_s)


def kernel(x_prompt, x_sample, c_prompt, c_sample, state_gla, state_pool, state_ret, w_ada, b_ada, norm_mix_pre, norm_mix_post, norm_ffn_pre, norm_ffn_post, w_in, w_gla_gate, b_gla_gate, gla_norm, pool_w, pool_scale, ret_norm, sgu_norm, sgu_w, sgu_b, w_branch, w_merge_gate, b_merge_gate, w_out, router_w, router_bias, expert_w1, expert_w3, expert_w2, shared_w1, shared_w3, shared_w2):
    wts = (norm_mix_pre, norm_mix_post, norm_ffn_pre, norm_ffn_post, w_in, w_gla_gate, b_gla_gate, gla_norm,
           pool_w, pool_scale, ret_norm, sgu_norm, sgu_w, sgu_b, w_branch, w_merge_gate, b_merge_gate, w_out,
           router_w, router_bias, expert_w1, expert_w3, expert_w2, shared_w1, shared_w3, shared_w2)
    c_all = jnp.zeros((MOD_ROWS, D), F32).at[:B_P].set(c_prompt).at[SUB:SUB + N_S].set(c_sample)
    mod = _ada(c_all, w_ada, b_ada)
    x = jnp.concatenate([x_prompt.reshape(N_P, D), x_sample.reshape(N_S, D)], axis=0)
    per_layer = []
    for l in range(DEPTH):
        x, states = _layer(l, x, mod, state_gla, state_pool, state_ret, wts)
        per_layer.append(states)
    gla_p, gla_s, pool_p, pool_s, ret_p, ret_s, vn_s = (jnp.stack(z) for z in zip(*per_layer))
    return (x[:N_P].reshape(B_P, T_P, D), x[N_P:].reshape(N_S, 1, D),
            gla_p, gla_s, pool_p, pool_s, ret_p, ret_s, vn_s.reshape(DEPTH, N_S, 1, 512))
```

```python
import functools

import jax
import jax.numpy as jnp
from jax import lax
from jax.experimental import pallas as pl
from jax.experimental.pallas import tpu as pltpu
from jax.experimental.pallas import tpu_sc as plsc

F32 = jnp.float32
BF16 = jnp.bfloat16
HIGHEST = lax.Precision.HIGHEST

D = 2048
B_P, T_P = 4, 2048
N_P = B_P * T_P
N_S = 128
N = N_P + N_S
DEPTH = 2
PAST_LEN = 16384
EPS = 1e-6
HEADS, DK, DV = 4, 64, 128
CHUNK = 64
GATE_TEMP = 16.0
POOL_WINDOWS = (2, 4, 8, 16)
POOL_BUF = 15
ROPE_BASE = 10000.0
N_EXPERTS = 64
TOP_K = 8
D_EXPERT = 512
ROUTED_SCALE = 2.5

LANES = 128
SUB = 128
MOD_ROWS = 256
EXP_BLOCK = 256
N_ASSIGN = N * TOP_K
N_BLOCKS = -(-(N_ASSIGN + N_EXPERTS * (EXP_BLOCK - 1)) // EXP_BLOCK)
L_SLOTS = N_BLOCKS * EXP_BLOCK
VMEM_LIMIT = 56 * 1024 * 1024

C_GQ, C_GK, C_GV, C_GR, C_PIN, C_RQ, C_RK, C_RV, C_RG, C_SU, C_SV = (
    0, 256, 512, 1024, 1536, 2048, 2304, 2560, 3072, 3584, 4096)
P_MAIN = 4608


def _cparams(n_axes=1):
    return pltpu.CompilerParams(dimension_semantics=("arbitrary",) * n_axes,
                                vmem_limit_bytes=VMEM_LIMIT)


def _silu(x):
    return x * jax.nn.sigmoid(x)


def _mod_rows(t, mp_ref, ms_ref):
    b = jnp.minimum(t // (T_P // SUB), B_P - 1)
    return jnp.where(t >= N_P // SUB, ms_ref[...], mp_ref[pl.ds(b, 1), :])


def _mod_specs(layer, part):
    return [pl.BlockSpec((None, 8, D), lambda i, l=layer, p=part: (l, 0, p)),
            pl.BlockSpec((None, SUB, D), lambda i, l=layer, p=part: (l, 1, p))]


def _pack_bf16_pair(lo, hi):
    lo_u = lax.bitcast_convert_type(lo.astype(BF16).astype(F32), jnp.uint32)
    hi_u = lax.bitcast_convert_type(hi.astype(BF16).astype(F32), jnp.uint32)
    return lax.bitcast_convert_type((hi_u & jnp.uint32(0xFFFF0000)) | (lo_u >> 16), jnp.int32)


def _unpack_bf16_pair(w):
    u = lax.bitcast_convert_type(w, jnp.uint32)
    lo = lax.bitcast_convert_type(u << 16, F32)
    hi = lax.bitcast_convert_type(u & jnp.uint32(0xFFFF0000), F32)
    return lo, hi


ROW_TILE = (8, LANES)


def _load_row_tiles(ref):
    return jnp.concatenate([ref[:, c, :] for c in range(ROW_TILE[0])], axis=1)


def _store_row_tiles(ref, val):
    for c in range(ROW_TILE[0]):
        ref[:, c, :] = val[:, c * LANES:(c + 1) * LANES]


def _load_row_tiles_2d(ref, rows):
    return jnp.concatenate([ref[pl.ds(c, rows, stride=ROW_TILE[0]), :] for c in range(ROW_TILE[0])], axis=1)


def _store_row_tiles_2d(ref, val, rows):
    for c in range(ROW_TILE[0]):
        ref[pl.ds(c, rows, stride=ROW_TILE[0]), :] = val[:, c * LANES:(c + 1) * LANES]


def _ada_body(c_ref, w_ref, b_ref, o_ref):
    s = _silu(c_ref[...]).astype(BF16)
    o_ref[...] = jnp.dot(s, w_ref[...].astype(BF16), preferred_element_type=F32) + b_ref[...]


def _ada(c_all, w_ada, b_ada):
    tn = 1024
    return pl.pallas_call(
        _ada_body,
        grid=(DEPTH, 6 * D // tn),
        in_specs=[pl.BlockSpec((MOD_ROWS, D), lambda l, j: (0, 0)),
                  pl.BlockSpec((None, D, tn), lambda l, j: (l, 0, j)),
                  pl.BlockSpec((None, 1, tn), lambda l, j: (l, 0, j))],
        out_specs=pl.BlockSpec((None, MOD_ROWS, tn), lambda l, j: (l, 0, j)),
        out_shape=jax.ShapeDtypeStruct((DEPTH, MOD_ROWS, 6 * D), F32),
        compiler_params=_cparams(2), name="ada")(c_all, w_ada, b_ada.reshape(DEPTH, 1, 6 * D))


ZERO_ROWS = 2 * SUB


def _prenorm_body(x_ref, g_ref, shp_ref, shs_ref, scp_ref, scs_ref, h_ref, *extra):
    t = jnp.minimum(pl.program_id(0), N // SUB - 1)
    x = x_ref[...]
    y = x * lax.rsqrt(jnp.mean(x * x, axis=-1, keepdims=True) + EPS) * g_ref[...]
    h = y * (1.0 + _mod_rows(t, scp_ref, scs_ref)) + _mod_rows(t, shp_ref, shs_ref)
    h_ref[...] = h.astype(BF16)
    if extra:
        hf_ref, hp_ref = extra
        hf_ref[...] = h
        _store_row_tiles(hp_ref, _pack_bf16_pair(h[:, :D // 2], h[:, D // 2:]))

        @pl.when(pl.program_id(0) >= N // SUB)
        def _():
            hp_ref[...] = jnp.zeros_like(hp_ref)


def _prenorm(x, g, mod, layer, shift_part, scale_part, with_extra):
    last = N // SUB - 1
    n_steps = N // SUB + (ZERO_ROWS // SUB if with_extra else 0)
    row = lambda i: (jnp.minimum(i, last), 0)
    out_shape = [jax.ShapeDtypeStruct((N, D), BF16)]
    out_specs = [pl.BlockSpec((SUB, D), row)]
    if with_extra:
        out_shape += [jax.ShapeDtypeStruct((N, D), F32), jax.ShapeDtypeStruct((N + ZERO_ROWS,) + ROW_TILE, jnp.int32)]
        out_specs += [pl.BlockSpec((SUB, D), row), pl.BlockSpec((SUB,) + ROW_TILE, lambda i: (i, 0, 0))]
    return pl.pallas_call(
        _prenorm_body,
        grid=(n_steps,),
        in_specs=[pl.BlockSpec((SUB, D), row),
                  pl.BlockSpec((1, D), lambda i: (0, 0))]
        + _mod_specs(layer, shift_part) + _mod_specs(layer, scale_part),
        out_specs=out_specs, out_shape=out_shape,
        compiler_params=_cparams(1), name="prenorm")(x, g.reshape(1, D), mod, mod, mod, mod)


def _mm_body(x_ref, w_ref, o_ref):
    o_ref[...] = jnp.dot(x_ref[...], w_ref[...], preferred_element_type=F32).astype(o_ref.dtype)


def _matmul(x, w, tm, tn, out_dtype=F32, name="mm"):
    m, k = x.shape
    n = w.shape[1]
    return pl.pallas_call(
        _mm_body,
        grid=(m // tm, n // tn),
        in_specs=[pl.BlockSpec((tm, k), lambda i, j: (i, 0)),
                  pl.BlockSpec((k, tn), lambda i, j: (0, j))],
        out_specs=pl.BlockSpec((tm, tn), lambda i, j: (i, j)),
        out_shape=jax.ShapeDtypeStruct((m, n), out_dtype),
        compiler_params=_cparams(2), name=name)(x, w)


ROWS_LA = 256


def _swap_halves_lanes(x):
    lane = lax.broadcasted_iota(jnp.int32, x.shape, 1)
    return jnp.where((lane % 64) < 32, pltpu.roll(x, 96, 1), pltpu.roll(x, 32, 1))


def _rope_lanes(x, cos, sin_signed):
    parts = []
    for half in range(2):
        xh = x[:, half * LANES:(half + 1) * LANES]
        parts.append(xh * cos + _swap_halves_lanes(xh) * sin_signed)
    return jnp.concatenate(parts, axis=1)


def _la_prompt_body(q_ref, k_ref, v_ref, r_ref, aux_ref, aux2_ref, dec_ref, bias_ref, g_ref,
                    o_ref, st_out_ref, st_ref, *, retention):
    t = pl.program_id(1)

    @pl.when(t == 0)
    def _():
        st_ref[...] = jnp.zeros_like(st_ref)

    ri = lax.broadcasted_iota(jnp.int32, (CHUNK, CHUNK), 0)
    ci = lax.broadcasted_iota(jnp.int32, (CHUNK, CHUNK), 1)
    causal = ri >= ci
    tril = causal.astype(F32)
    scale = DK ** -0.5

    for c in range(ROWS_LA // CHUNK):
        rows = pl.ds(c * CHUNK, CHUNK)
        q = q_ref[rows, :]
        k = k_ref[rows, :]
        v = v_ref[rows, :]
        if retention:
            cos = aux_ref[rows, :]
            sin = aux2_ref[rows, :]
            q = _rope_lanes(q, cos, sin)
            k = _rope_lanes(k, cos, sin) * scale
            la = jnp.broadcast_to(dec_ref[...], (CHUNK, HEADS * DK))
        else:
            q = q * scale
            logit = jnp.dot(aux_ref[rows, :], dec_ref[...], precision=HIGHEST,
                            preferred_element_type=F32) + bias_ref[...]
            la = jax.nn.log_sigmoid(logit) / GATE_TEMP
        bc = jnp.dot(tril, la, precision=HIGHEST, preferred_element_type=F32)
        bl = bc[CHUNK - 1:CHUNK, :]
        qd = q * jnp.exp(bc)
        ki = k * jnp.exp(-bc)
        ke = k * jnp.exp(bl - bc)
        ac = jnp.exp(bl)
        outs = []
        for h in range(HEADS):
            ks = slice(h * DK, (h + 1) * DK)
            vs = slice(h * DV, (h + 1) * DV)
            qd_h = qd[:, ks].astype(BF16)
            ki_h = ki[:, ks].astype(BF16)
            ke_h = ke[:, ks].astype(BF16)
            v_h = v[:, vs].astype(BF16)
            sc = lax.dot_general(qd_h, ki_h, (((1,), (1,)), ((), ())), preferred_element_type=F32)
            sc = jnp.where(causal, sc, 0.0)
            o_h = jnp.dot(sc.astype(BF16), v_h, preferred_element_type=F32)
            st = st_ref[h]
            o_h = o_h + lax.dot_general(qd_h, st.astype(BF16), (((1,), (1,)), ((), ())),
                                        preferred_element_type=F32)
            kv_t = lax.dot_general(v_h, ke_h, (((0,), (0,)), ((), ())), preferred_element_type=F32)
            st_ref[h] = st * ac[:, ks] + kv_t
            o_n = o_h * lax.rsqrt(jnp.mean(o_h * o_h, axis=-1, keepdims=True) + EPS) * g_ref[:, vs]
            outs.append(o_n)
        o = jnp.concatenate(outs, axis=1) * _silu(r_ref[rows, :])
        o_ref[rows, :] = o.astype(BF16)

    st_out_ref[...] = st_ref[...]


def _la_prompt(p_main, cq, ck, cv, cr, aux, aux2, dec, bias, g, retention):
    nt = T_P // ROWS_LA
    rowblk = lambda b, t: b * nt + t
    if retention:
        aux_specs = [pl.BlockSpec((ROWS_LA, LANES), lambda b, t: (t, 0)),
                     pl.BlockSpec((ROWS_LA, LANES), lambda b, t: (t, 0))]
    else:
        aux_specs = [pl.BlockSpec((ROWS_LA, LANES), lambda b, t: (rowblk(b, t), 0)),
                     pl.BlockSpec((8, LANES), lambda b, t: (0, 0))]
    o, st = pl.pallas_call(
        functools.partial(_la_prompt_body, retention=retention),
        grid=(B_P, nt),
        in_specs=[pl.BlockSpec((ROWS_LA, 256), lambda b, t: (rowblk(b, t), cq // 256)),
                  pl.BlockSpec((ROWS_LA, 256), lambda b, t: (rowblk(b, t), ck // 256)),
                  pl.BlockSpec((ROWS_LA, 512), lambda b, t: (rowblk(b, t), cv // 512)),
                  pl.BlockSpec((ROWS_LA, 512), lambda b, t: (rowblk(b, t), cr // 512))]
        + aux_specs
        + [pl.BlockSpec(dec.shape, lambda b, t: (0, 0)),
           pl.BlockSpec((1, HEADS * DK), lambda b, t: (0, 0)),
           pl.BlockSpec((1, HEADS * DV), lambda b, t: (0, 0))],
        out_specs=[pl.BlockSpec((ROWS_LA, HEADS * DV), lambda b, t: (rowblk(b, t), 0)),
                   pl.BlockSpec((None, HEADS, DV, DK), lambda b, t: (b, 0, 0, 0))],
        out_shape=[jax.ShapeDtypeStruct((N_P, HEADS * DV), BF16),
                   jax.ShapeDtypeStruct((B_P, HEADS, DV, DK), F32)],
        scratch_shapes=[pltpu.VMEM((HEADS, DV, DK), F32)],
        compiler_params=_cparams(2), name="ret_prompt" if retention else "gla_prompt",
    )(p_main, p_main, p_main, p_main, aux, aux2, dec, bias, g)
    return o, jnp.swapaxes(st, -1, -2)


SAMPLE_TILE = 8


def _la_sample_body(qt_ref, kt_ref, lt_ref, cos_ref, sin_ref, v_ref, r_ref, g_ref, s_ref,
                    o_ref, s_out_ref, *, retention):
    scale = DK ** -0.5
    qt = qt_ref[...]
    kt = kt_ref[...]
    if retention:
        def rope(x):
            sw = jnp.concatenate(
                [x[h * DK + (DK // 2) * (1 - j): h * DK + (DK // 2) * (2 - j), :]
                 for h in range(HEADS) for j in range(2)], axis=0)
            return x * cos_ref[...] + sw * sin_ref[...]
        qt = rope(qt)
        kt = rope(kt) * scale
        la = lt_ref[...]
    else:
        qt = qt * scale
        la = jax.nn.log_sigmoid(lt_ref[...]) / GATE_TEMP
    at = jnp.exp(la)
    qd = qt * at
    ki = kt * jnp.exp(-la)
    prod = qd * ki
    v8 = v_ref[...]
    r8 = r_ref[...]
    g = g_ref[...]
    for j in range(SAMPLE_TILE):
        for h in range(HEADS):
            ks = slice(h * DK, (h + 1) * DK)
            vs = slice(h * DV, (h + 1) * DV)
            a_c = jnp.broadcast_to(at[ks, j:j + 1], (DK, DV))
            k_c = jnp.broadcast_to(kt[ks, j:j + 1], (DK, DV))
            q_c = jnp.broadcast_to(qd[ks, j:j + 1], (DK, DV))
            s_c = jnp.broadcast_to(jnp.sum(prod[ks, j:j + 1], axis=0, keepdims=True), (1, DV))
            s0 = s_ref[j, h]
            v_row = v8[j:j + 1, vs]
            s_out_ref[j, h] = a_c * s0 + k_c * v_row
            o_row = s_c * v_row + jnp.sum(q_c * s0, axis=0, keepdims=True)
            o_n = o_row * lax.rsqrt(jnp.mean(o_row * o_row, axis=-1, keepdims=True) + EPS) * g[:, vs]
            o_ref[j:j + 1, vs] = o_n * _silu(r8[j:j + 1, vs])


def _la_sample(qt, kt, lt, cos_t, sin_t, p_main, cv, cr, g, s0, retention):
    nt = N_S // SAMPLE_TILE
    row0 = N_P // SAMPLE_TILE
    tile = pl.BlockSpec((None, HEADS * DK, LANES), lambda i: (i, 0, 0))
    full = pl.BlockSpec((HEADS * DK, LANES), lambda i: (0, 0))
    lt_spec = full if retention else tile
    return pl.pallas_call(
        functools.partial(_la_sample_body, retention=retention),
        grid=(nt,),
        in_specs=[tile, tile, lt_spec, full, full,
                  pl.BlockSpec((SAMPLE_TILE, 512), lambda i: (row0 + i, cv // 512)),
                  pl.BlockSpec((SAMPLE_TILE, 512), lambda i: (row0 + i, cr // 512)),
                  pl.BlockSpec((1, HEADS * DV), lambda i: (0, 0)),
                  pl.BlockSpec((SAMPLE_TILE, HEADS, DK, DV), lambda i: (i, 0, 0, 0))],
        out_specs=[pl.BlockSpec((SAMPLE_TILE, HEADS * DV), lambda i: (i, 0)),
                   pl.BlockSpec((SAMPLE_TILE, HEADS, DK, DV), lambda i: (i, 0, 0, 0))],
        out_shape=[jax.ShapeDtypeStruct((N_S, HEADS * DV), F32),
                   jax.ShapeDtypeStruct((N_S, HEADS, DK, DV), F32)],
        compiler_params=_cparams(1), name="ret_sample" if retention else "gla_sample",
    )(qt, kt, lt, cos_t, sin_t, p_main, p_main, g, s0)


def _gate_logits_t_body(w_ref, x_ref, b_ref, o_ref):
    o_ref[...] = jnp.dot(w_ref[...], x_ref[...], precision=HIGHEST, preferred_element_type=F32) + b_ref[...]


def _gate_logits_t(w_gate_t, glow_t, b_col):
    nt = N_S // SAMPLE_TILE
    return pl.pallas_call(
        _gate_logits_t_body,
        grid=(nt,),
        in_specs=[pl.BlockSpec((HEADS * DK, LANES), lambda i: (0, 0)),
                  pl.BlockSpec((None, LANES, LANES), lambda i: (i, 0, 0)),
                  pl.BlockSpec((HEADS * DK, LANES), lambda i: (0, 0))],
        out_specs=pl.BlockSpec((None, HEADS * DK, LANES), lambda i: (i, 0, 0)),
        out_shape=jax.ShapeDtypeStruct((nt, HEADS * DK, LANES), F32),
        compiler_params=_cparams(1), name="gate_logits_t")(w_gate_t, glow_t, b_col)


def _to_tiles_t(x):
    c = x.shape[1]
    xt = jnp.swapaxes(x.reshape(N_S // SAMPLE_TILE, SAMPLE_TILE, c), 1, 2)
    return jnp.pad(xt, ((0, 0), (0, 0), (0, LANES - SAMPLE_TILE)))


ROWS_POOL = 512


def _pool_mix(y, w_ref, sc_ref):
    outs = []
    for gi in range(4):
        cs = slice(gi * LANES, (gi + 1) * LANES)
        outs.append(jnp.dot(y[:, cs].astype(BF16), w_ref[gi], preferred_element_type=F32))
    return jnp.concatenate(outs, axis=1) * sc_ref[...]


def _pool_prompt_body(p_ref, halo_ref, w_ref, sc_ref, o_ref):
    t = pl.program_id(1)
    p = p_ref[...]
    halo = jnp.where(t == 0, 0.0, halo_ref[...])
    full = jnp.concatenate([halo, p], axis=0)
    pos = t * ROWS_POOL + lax.broadcasted_iota(jnp.int32, (ROWS_POOL, LANES), 0)
    means = []
    for gi, w in enumerate(POOL_WINDOWS):
        s = full[:, gi * LANES:(gi + 1) * LANES]
        step = 1
        while step < w:
            s = s + pltpu.roll(s, step, 0)
            step *= 2
        win = s[16:, :]
        cnt = jnp.minimum(w, pos + 1).astype(F32)
        means.append(win / cnt)
    y = jnp.concatenate(means, axis=1) - p
    o_ref[...] = _pool_mix(y, w_ref, sc_ref).astype(BF16)


def _pool_prompt(p_main, w_bf, scale):
    nt = T_P // ROWS_POOL
    return pl.pallas_call(
        _pool_prompt_body,
        grid=(B_P, nt),
        in_specs=[pl.BlockSpec((ROWS_POOL, 512), lambda b, t: (b * nt + t, C_PIN // 512)),
                  pl.BlockSpec((16, 512), lambda b, t: (jnp.maximum((b * nt + t) * (ROWS_POOL // 16) - 1, 0),
                                                        C_PIN // 512)),
                  pl.BlockSpec((4, LANES, LANES), lambda b, t: (0, 0, 0)),
                  pl.BlockSpec((1, 512), lambda b, t: (0, 0))],
        out_specs=pl.BlockSpec((ROWS_POOL, 512), lambda b, t: (b * nt + t, 0)),
        out_shape=jax.ShapeDtypeStruct((N_P, 512), BF16),
        compiler_params=_cparams(2), name="pool_prompt")(p_main, p_main, w_bf, scale)


def _small_sample_body(p_ref, buf_ref, pw_ref, psc_ref, u_ref, sv_ref, sg_ref, sw_ref, sb_ref,
                       ob_ref, od_ref, vn_ref):
    p = p_ref[...]
    means = []
    for gi, w in enumerate(POOL_WINDOWS):
        cs = slice(gi * LANES, (gi + 1) * LANES)
        s = p[:, cs]
        for j in range(1, w):
            s = s + buf_ref[:, POOL_BUF - j, cs]
        means.append(s / float(min(w, PAST_LEN + 1)))
    y = jnp.concatenate(means, axis=1) - p
    ob_ref[...] = _pool_mix(y, pw_ref, psc_ref)
    sv = sv_ref[...]
    vn = sv * lax.rsqrt(jnp.mean(sv * sv, axis=-1, keepdims=True) + EPS) * sg_ref[...]
    vn_ref[...] = vn
    od_ref[...] = u_ref[...] * (sw_ref[...] * vn + sb_ref[...])


def _small_sample(p_main, buf, pw_bf, pscale, sgu_g, sgu_w0, sgu_b0):
    row = N_P // N_S
    col = lambda c: pl.BlockSpec((N_S, 512), lambda i, c=c: (row, c // 512))
    vec = pl.BlockSpec((1, 512), lambda i: (0, 0))
    return pl.pallas_call(
        _small_sample_body,
        grid=(1,),
        in_specs=[col(C_PIN), pl.BlockSpec((N_S, POOL_BUF, 512), lambda i: (0, 0, 0)),
                  pl.BlockSpec((4, LANES, LANES), lambda i: (0, 0, 0)), vec,
                  col(C_SU), col(C_SV), vec, vec, vec],
        out_specs=[pl.BlockSpec((N_S, 512), lambda i: (0, 0))] * 3,
        out_shape=[jax.ShapeDtypeStruct((N_S, 512), F32)] * 3,
        compiler_params=_cparams(1), name="small_sample",
    )(p_main, buf, pw_bf, pscale, p_main, p_main, sgu_g, sgu_w0, sgu_b0)


ROWS_SGU = 512
SGU_CHUNK = 128


def _sgu_prompt_body(u_ref, v_ref, g_ref, w_ref, bt_ref, o_ref):
    ri = lax.broadcasted_iota(jnp.int32, (SGU_CHUNK, SGU_CHUNK), 0)
    ci = lax.broadcasted_iota(jnp.int32, (SGU_CHUNK, SGU_CHUNK), 1)
    causal = ri >= ci
    for c in range(ROWS_SGU // SGU_CHUNK):
        rows = pl.ds(c * SGU_CHUNK, SGU_CHUNK)
        v = v_ref[rows, :]
        vn = (v * lax.rsqrt(jnp.mean(v * v, axis=-1, keepdims=True) + EPS) * g_ref[...]).astype(BF16)
        outs = []
        for gi in range(4):
            cs = slice(gi * LANES, (gi + 1) * LANES)
            w = jnp.where(causal, w_ref[gi], 0.0).astype(BF16)
            mixed = jnp.dot(w, vn[:, cs], preferred_element_type=F32)
            outs.append(mixed + jnp.broadcast_to(bt_ref[:, gi:gi + 1], (SGU_CHUNK, LANES)))
        o_ref[rows, :] = (u_ref[rows, :] * jnp.concatenate(outs, axis=1)).astype(BF16)


def _sgu_prompt(p_main, g, w, b_t):
    return pl.pallas_call(
        _sgu_prompt_body,
        grid=(N_P // ROWS_SGU,),
        in_specs=[pl.BlockSpec((ROWS_SGU, 512), lambda i: (i, C_SU // 512)),
                  pl.BlockSpec((ROWS_SGU, 512), lambda i: (i, C_SV // 512)),
                  pl.BlockSpec((1, 512), lambda i: (0, 0)),
                  pl.BlockSpec((4, SGU_CHUNK, SGU_CHUNK), lambda i: (0, 0, 0)),
                  pl.BlockSpec((SGU_CHUNK, LANES), lambda i: (0, 0))],
        out_specs=pl.BlockSpec((ROWS_SGU, 512), lambda i: (i, 0)),
        out_shape=jax.ShapeDtypeStruct((N_P, 512), BF16),
        compiler_params=_cparams(1), name="sgu_prompt")(p_main, p_main, g, w, b_t)


TM_MERGE = 640
TN_MERGE = 512


def _merge_body(h_ref, ba_ref, bb_ref, bc_ref, bd_ref, g0, g1, g2, g3, u0, u1, u2, u3,
                c0, c1, c2, c3, o_ref):
    h = h_ref[...]
    acc = None
    for br, gw, uw, gb in ((ba_ref, g0, u0, c0), (bb_ref, g1, u1, c1), (bc_ref, g2, u2, c2), (bd_ref, g3, u3, c3)):
        gate = jax.nn.sigmoid(jnp.dot(h, gw[...], preferred_element_type=F32) + gb[...])
        up = jnp.dot(br[...], uw[...], preferred_element_type=F32)
        acc = gate * up if acc is None else acc + gate * up
    o_ref[...] = acc.astype(BF16)


def _merge(h, branches, w_mg, b_mg, w_br):
    nj = D // TN_MERGE
    row = lambda w: pl.BlockSpec((TM_MERGE, w), lambda i, j: (i, 0))
    gate_w = [pl.BlockSpec((D, TN_MERGE), lambda i, j, b=b: (0, b * nj + j)) for b in range(4)]
    up_w = [pl.BlockSpec((None, 512, TN_MERGE), lambda i, j, b=b: (b, 0, j)) for b in range(4)]
    gate_b = [pl.BlockSpec((1, TN_MERGE), lambda i, j, b=b: (0, b * nj + j)) for b in range(4)]
    return pl.pallas_call(
        _merge_body,
        grid=(N // TM_MERGE, nj),
        in_specs=[row(D)] + [row(512)] * 4 + gate_w + up_w + gate_b,
        out_specs=pl.BlockSpec((TM_MERGE, TN_MERGE), lambda i, j: (i, j)),
        out_shape=jax.ShapeDtypeStruct((N, D), BF16),
        compiler_params=_cparams(2), name="merge",
    )(h, *branches, w_mg, w_mg, w_mg, w_mg, w_br, w_br, w_br, w_br, b_mg, b_mg, b_mg, b_mg)


TM_OUT = 640
TN_OUT = 512


def _post_residual(t0, x_ref, get_y, gn_ref, gp_ref, gs_ref, o_ref, rows_total):
    for s in range(rows_total // SUB):
        rows = pl.ds(s * SUB, SUB)
        y = get_y(rows)
        yn = y * lax.rsqrt(jnp.mean(y * y, axis=-1, keepdims=True) + EPS) * gn_ref[...]
        o_ref[rows, :] = x_ref[rows, :] + _mod_rows(t0 + s, gp_ref, gs_ref) * yn


def _outproj_body(m_ref, w_ref, x_ref, gn_ref, gp_ref, gs_ref, o_ref, acc_ref):
    j = pl.program_id(1)
    acc_ref[j] = jnp.dot(m_ref[...], w_ref[...], preferred_element_type=F32)

    @pl.when(j == D // TN_OUT - 1)
    def _():
        get_y = lambda rows: jnp.concatenate([acc_ref[c, rows, :] for c in range(D // TN_OUT)], axis=1)
        _post_residual(pl.program_id(0) * (TM_OUT // SUB), x_ref, get_y, gn_ref, gp_ref, gs_ref,
                       o_ref, TM_OUT)


def _outproj(merged, w_out, x, g_post, mod, layer):
    mspec = [pl.BlockSpec((None, 8, D), lambda i, j, l=layer: (l, 0, 2)),
             pl.BlockSpec((None, SUB, D), lambda i, j, l=layer: (l, 1, 2))]
    return pl.pallas_call(
        _outproj_body,
        grid=(N // TM_OUT, D // TN_OUT),
        in_specs=[pl.BlockSpec((TM_OUT, D), lambda i, j: (i, 0)),
                  pl.BlockSpec((D, TN_OUT), lambda i, j: (0, j)),
                  pl.BlockSpec((TM_OUT, D), lambda i, j: (i, 0)),
                  pl.BlockSpec((1, D), lambda i, j: (0, 0))] + mspec,
        out_specs=pl.BlockSpec((TM_OUT, D), lambda i, j: (i, 0)),
        out_shape=jax.ShapeDtypeStruct((N, D), F32),
        scratch_shapes=[pltpu.VMEM((D // TN_OUT, TM_OUT, TN_OUT), F32)],
        compiler_params=_cparams(2), name="outproj",
    )(merged, w_out, x, g_post.reshape(1, D), mod, mod)


def _router_body(h_ref, w_ref, b_ref, eid_ref, pos_ref, wt_ref, cnt_ref, run_ref):
    i = pl.program_id(0)

    @pl.when(i == 0)
    def _():
        run_ref[...] = jnp.zeros_like(run_ref)

    ng, gs = 8, N_EXPERTS // 8
    neg = -jnp.inf
    logits = jnp.dot(h_ref[...], w_ref[...], precision=HIGHEST, preferred_element_type=F32)
    scores = jax.nn.sigmoid(logits.T[:N_EXPERTS, :])
    sel = scores + b_ref[...]
    sel3 = sel.reshape(ng, gs, SUB)
    sub3 = lax.broadcasted_iota(jnp.int32, (ng, gs, SUB), 1)
    gmax = jnp.max(sel3, axis=1, keepdims=True)
    first = jnp.min(jnp.where(sel3 == gmax, sub3, gs), axis=1, keepdims=True)
    gmax2 = jnp.max(jnp.where(sub3 == first, neg, sel3), axis=1, keepdims=True)
    gscore = (gmax + gmax2).reshape(ng, SUB)
    gidx = lax.broadcasted_iota(jnp.int32, (ng, SUB), 0)
    grank = jnp.zeros((ng, SUB), jnp.int32)
    for s in range(1, ng):
        other = pltpu.roll(gscore, s, 0)
        lower = gidx >= s
        grank += ((other > gscore) | ((other == gscore) & lower)).astype(jnp.int32)
    keep = jnp.broadcast_to((grank < 4).reshape(ng, 1, SUB), (ng, gs, SUB))
    masked = jnp.where(keep, sel3, neg).reshape(N_EXPERTS, SUB)
    eidx = lax.broadcasted_iota(jnp.int32, (N_EXPERTS, SUB), 0)
    rank = jnp.zeros((N_EXPERTS, SUB), jnp.int32)
    for s in range(1, N_EXPERTS):
        other = pltpu.roll(masked, s, 0)
        lower = eidx >= s
        rank += ((other > masked) | ((other == masked) & lower)).astype(jnp.int32)
    chosen = rank < TOP_K
    w_sel = jnp.where(chosen, scores, 0.0)
    w_sel = w_sel / jnp.sum(w_sel, axis=0, keepdims=True) * ROUTED_SCALE
    ri = lax.broadcasted_iota(jnp.int32, (SUB, SUB), 0)
    ci = lax.broadcasted_iota(jnp.int32, (SUB, SUB), 1)
    onehot = chosen.astype(BF16)
    pos = jnp.dot(onehot, (ri < ci).astype(BF16), preferred_element_type=F32) + run_ref[...]
    run_ref[...] = run_ref[...] + jnp.sum(chosen.astype(F32), axis=1, keepdims=True)
    cnt_ref[...] = run_ref[...]
    eidx_f = eidx.astype(F32)
    rows_e, rows_p, rows_w = [], [], []
    for kk in range(TOP_K):
        m = chosen & (rank == kk)
        rows_e.append(jnp.sum(jnp.where(m, eidx_f, 0.0), axis=0, keepdims=True))
        rows_p.append(jnp.sum(jnp.where(m, pos, 0.0), axis=0, keepdims=True))
        rows_w.append(jnp.sum(jnp.where(m, w_sel, 0.0), axis=0, keepdims=True))
    eid_ref[...] = jnp.concatenate(rows_e, axis=0).astype(jnp.int32)
    pos_ref[...] = jnp.concatenate(rows_p, axis=0).astype(jnp.int32)
    wt_ref[...] = jnp.concatenate(rows_w, axis=0)


def _router(h_f32, rw_pad, rb_col):
    tile = pl.BlockSpec((TOP_K, SUB), lambda i: (0, i))
    return pl.pallas_call(
        _router_body,
        grid=(N // SUB,),
        in_specs=[pl.BlockSpec((SUB, D), lambda i: (i, 0)),
                  pl.BlockSpec((D, LANES), lambda i: (0, 0)),
                  pl.BlockSpec((N_EXPERTS, SUB), lambda i: (0, 0))],
        out_specs=[tile, tile, tile, pl.BlockSpec((N_EXPERTS, SUB), lambda i: (0, 0))],
        out_shape=[jax.ShapeDtypeStruct((TOP_K, N), jnp.int32), jax.ShapeDtypeStruct((TOP_K, N), jnp.int32),
                   jax.ShapeDtypeStruct((TOP_K, N), F32), jax.ShapeDtypeStruct((N_EXPERTS, SUB), F32)],
        scratch_shapes=[pltpu.VMEM((N_EXPERTS, SUB), F32)],
        compiler_params=_cparams(1), name="router")(h_f32, rw_pad, rb_col)


SC_CORES, SC_SUBCORES = 2, 16
SC_WORKERS = SC_CORES * SC_SUBCORES
SC_CHUNK = 32


def _sc_gather(table, idx):
    n_out = idx.shape[0]
    per_w = n_out // SC_WORKERS
    assert per_w * SC_WORKERS == n_out and per_w % SC_CHUNK == 0
    mesh = plsc.VectorSubcoreMesh(core_axis_name="c", subcore_axis_name="s",
                                  num_cores=SC_CORES, num_subcores=SC_SUBCORES)

    def body(table_hbm, idx_hbm, out_hbm, idx_v, rows_v, sem):
        wid = lax.axis_index("s") * SC_CORES + lax.axis_index("c")
        base = wid * per_w
        pltpu.sync_copy(idx_hbm.at[pl.ds(base, per_w)], idx_v)

        @pl.loop(0, per_w // SC_CHUNK)
        def _(j):
            off = pl.multiple_of(j * SC_CHUNK, SC_CHUNK)
            pltpu.async_copy(table_hbm.at[idx_v.at[pl.ds(off, SC_CHUNK)]], rows_v, sem).wait()
            pltpu.sync_copy(rows_v, out_hbm.at[pl.ds(base + off, SC_CHUNK)])

    return pl.kernel(
        body, out_type=jax.ShapeDtypeStruct((n_out,) + ROW_TILE, jnp.int32), mesh=mesh,
        scratch_types=[pltpu.VMEM((per_w,), jnp.int32), pltpu.VMEM((SC_CHUNK,) + ROW_TILE, jnp.int32),
                       pltpu.SemaphoreType.DMA],
        name="sc_gather")(table, idx)


def _experts_body(be_ref, first_ref, nused_ref, x_ref, w1_ref, w3_ref, w2_ref, y_ref, w1b, w3b, w2b):
    b = pl.program_id(0)

    @pl.when(first_ref[b] == 1)
    def _():
        w1b[...] = w1_ref[...].astype(BF16)
        w3b[...] = w3_ref[...].astype(BF16)
        w2b[...] = w2_ref[...].astype(BF16)

    @pl.when(b < nused_ref[0])
    def _():
        lo, hi = _unpack_bf16_pair(_load_row_tiles_2d(x_ref, EXP_BLOCK))
        lo = lo.astype(BF16)
        hi = hi.astype(BF16)
        half = D // 2
        h1 = (jnp.dot(lo, w1b[:half, :], preferred_element_type=F32)
              + jnp.dot(hi, w1b[half:, :], preferred_element_type=F32))
        h3 = (jnp.dot(lo, w3b[:half, :], preferred_element_type=F32)
              + jnp.dot(hi, w3b[half:, :], preferred_element_type=F32))
        hid = (_silu(h1) * h3).astype(BF16)
        y = jnp.dot(hid, w2b[...], preferred_element_type=F32)
        _store_row_tiles_2d(y_ref, _pack_bf16_pair(y[:, :half], y[:, half:]), EXP_BLOCK)

    @pl.when(b >= nused_ref[0])
    def _():
        y_ref[...] = jnp.zeros_like(y_ref)


def _experts(block_e, first, nused, xs, w1, w3, w2, layer):
    blk = lambda b, be, fi, nu: (jnp.minimum(b, nu[0] - 1), 0)
    grid_spec = pltpu.PrefetchScalarGridSpec(
        num_scalar_prefetch=3,
        grid=(N_BLOCKS,),
        in_specs=[pl.BlockSpec((EXP_BLOCK * ROW_TILE[0], LANES), blk),
                  pl.BlockSpec((None, None, D, D_EXPERT), lambda b, be, fi, nu, l=layer: (l, be[b], 0, 0)),
                  pl.BlockSpec((None, None, D, D_EXPERT), lambda b, be, fi, nu, l=layer: (l, be[b], 0, 0)),
                  pl.BlockSpec((None, None, D_EXPERT, D), lambda b, be, fi, nu, l=layer: (l, be[b], 0, 0))],
        out_specs=pl.BlockSpec((EXP_BLOCK * ROW_TILE[0], LANES), lambda b, be, fi, nu: (b, 0)),
        scratch_shapes=[pltpu.VMEM((D, D_EXPERT), BF16), pltpu.VMEM((D, D_EXPERT), BF16),
                        pltpu.VMEM((D_EXPERT, D), BF16)])
    return pl.pallas_call(
        _experts_body, grid_spec=grid_spec,
        out_shape=jax.ShapeDtypeStruct((L_SLOTS * ROW_TILE[0], LANES), jnp.int32),
        compiler_params=_cparams(1), name="experts",
    )(block_e, first, nused, xs.reshape(L_SLOTS * ROW_TILE[0], LANES), w1, w3, w2).reshape((L_SLOTS,) + ROW_TILE)


TM_SHARED = 640


def _shared_body(h_ref, w13_ref, w2_ref, o_ref):
    up = jnp.dot(h_ref[...], w13_ref[...], preferred_element_type=F32)
    hid = (_silu(up[:, :D_EXPERT]) * up[:, D_EXPERT:]).astype(BF16)
    o_ref[...] = jnp.dot(hid, w2_ref[...], preferred_element_type=F32)


def _shared(h, w13, w2):
    return pl.pallas_call(
        _shared_body,
        grid=(N // TM_SHARED,),
        in_specs=[pl.BlockSpec((TM_SHARED, D), lambda i: (i, 0)),
                  pl.BlockSpec((D, 2 * D_EXPERT), lambda i: (0, 0)),
                  pl.BlockSpec((D_EXPERT, D), lambda i: (0, 0))],
        out_specs=pl.BlockSpec((TM_SHARED, D), lambda i: (i, 0)),
        out_shape=jax.ShapeDtypeStruct((N, D), F32),
        compiler_params=_cparams(1), name="shared")(h, w13, w2)


def _combine_body(g_ref, wt_ref, sh_ref, x_ref, gn_ref, gp_ref, gs_ref, o_ref, f_ref):
    half = D // 2
    acc_lo = sh_ref[:, :half]
    acc_hi = sh_ref[:, half:]
    wt = wt_ref[...]
    per_tok = TOP_K * ROW_TILE[0]
    for k in range(TOP_K):
        packed = jnp.concatenate([g_ref[pl.ds(k * ROW_TILE[0] + c, SUB, stride=per_tok), :]
                                  for c in range(ROW_TILE[0])], axis=1)
        lo, hi = _unpack_bf16_pair(packed)
        w_c = wt[:, k:k + 1]
        acc_lo = acc_lo + w_c * lo
        acc_hi = acc_hi + w_c * hi
    f_ref[:, :half] = acc_lo
    f_ref[:, half:] = acc_hi
    _post_residual(pl.program_id(0), x_ref, lambda rows: f_ref[rows, :], gn_ref, gp_ref, gs_ref, o_ref, SUB)


def _combine(gathered, wts, shared, x, g_post, mod, layer):
    per_tok = TOP_K * ROW_TILE[0]
    return pl.pallas_call(
        _combine_body,
        grid=(N // SUB,),
        in_specs=[pl.BlockSpec((SUB * per_tok, LANES), lambda i: (i, 0)),
                  pl.BlockSpec((SUB, LANES), lambda i: (i, 0)),
                  pl.BlockSpec((SUB, D), lambda i: (i, 0)),
                  pl.BlockSpec((SUB, D), lambda i: (i, 0)),
                  pl.BlockSpec((1, D), lambda i: (0, 0)),
                  pl.BlockSpec((None, 8, D), lambda i, l=layer: (l, 0, 5)),
                  pl.BlockSpec((None, SUB, D), lambda i, l=layer: (l, 1, 5))],
        out_specs=pl.BlockSpec((SUB, D), lambda i: (i, 0)),
        out_shape=jax.ShapeDtypeStruct((N, D), F32),
        scratch_shapes=[pltpu.VMEM((SUB, D), F32)],
        compiler_params=_cparams(1), name="combine",
    )(gathered.reshape(N * per_tok, LANES), wts, shared, x, g_post.reshape(1, D), mod, mod)


def _reorder_w_in(w):
    return jnp.concatenate([w[:, :1536], w[:, 1552:]], axis=1), jnp.pad(w[:, 1536:1552], ((0, 0), (0, LANES - 16)))


def _rope_tables(pos):
    half = DK // 2
    inv = ROPE_BASE ** (-jnp.arange(half, dtype=F32) / half)
    ang = pos.astype(F32)[:, None] * inv[None, :]
    cos = jnp.cos(ang)
    sin = jnp.sin(ang)
    return jnp.concatenate([cos, cos], axis=1), jnp.concatenate([-sin, sin], axis=1)


def _layer(l, x, mod, s_gla, s_pool, s_ret, wts):
    (norm_mix_pre, norm_mix_post, norm_ffn_pre, norm_ffn_post, w_in, w_gla_gate, b_gla_gate, gla_norm,
     pool_w, pool_scale, ret_norm, sgu_norm, sgu_w, sgu_b, w_branch, w_merge_gate, b_merge_gate, w_out,
     router_w, router_bias, expert_w1, expert_w3, expert_w2, shared_w1, shared_w3, shared_w2) = wts

    h = _prenorm(x, norm_mix_pre[l], mod, l, 0, 1, False)[0]
    w_main, w_low = _reorder_w_in(w_in[l])
    p_main = _matmul(h, w_main.astype(BF16), 1664, 512, name="inproj")
    p_low = _matmul(h, w_low.astype(BF16), 1664, LANES, name="inproj_low")

    w_gate_pad = jnp.pad(w_gla_gate[l], ((0, LANES - 16), (0, 0)))
    b_gate = b_gla_gate[l].reshape(1, HEADS * DK)
    log_gamma = jnp.log1p(-jnp.exp2(-5.0 - jnp.arange(HEADS, dtype=F32)))
    dec_row = jnp.repeat(log_gamma, DK).reshape(1, HEADS * DK)
    cos_p, sin_p = _rope_tables(jnp.arange(T_P))
    cos_p = jnp.tile(cos_p, (1, 2))
    sin_p = jnp.tile(sin_p, (1, 2))
    g_gla = gla_norm[l].reshape(1, HEADS * DV)
    g_ret = ret_norm[l].reshape(1, HEADS * DV)

    oa_p, gla_p = _la_prompt(p_main, C_GQ, C_GK, C_GV, C_GR, p_low, p_low, w_gate_pad, b_gate, g_gla, False)
    oc_p, ret_p = _la_prompt(p_main, C_RQ, C_RK, C_RV, C_RG, cos_p, sin_p, dec_row, b_gate, g_ret, True)
    pw_bf = pool_w[l].astype(BF16)
    pscale = pool_scale[l].reshape(1, 512)
    ob_p = _pool_prompt(p_main, pw_bf, pscale)
    sgu_g = sgu_norm[l].reshape(1, 512)
    od_p = _sgu_prompt(p_main, sgu_g, sgu_w[l], jnp.pad(sgu_b[l].T, ((0, 0), (0, LANES - 4))))

    ps = p_main[N_P:]
    q_t = _to_tiles_t(ps[:, C_GQ:C_GQ + 256])
    k_t = _to_tiles_t(ps[:, C_GK:C_GK + 256])
    glow_t = jnp.pad(_to_tiles_t(p_low[N_P:, :16]), ((0, 0), (0, LANES - 16), (0, 0)))
    w_gate_t = jnp.pad(w_gla_gate[l].T, ((0, 0), (0, LANES - 16)))
    b_col = jnp.broadcast_to(b_gla_gate[l][:, None], (HEADS * DK, LANES))
    logit_t = _gate_logits_t(w_gate_t, glow_t, b_col)
    dummy = jnp.zeros((HEADS * DK, LANES), F32)
    oa_s, gla_s = _la_sample(q_t, k_t, logit_t, dummy, dummy, p_main, C_GV, C_GR, g_gla, s_gla[l], False)
    cos_s, sin_s = _rope_tables(jnp.full((1,), PAST_LEN))
    cos_c = jnp.broadcast_to(jnp.tile(cos_s[0], HEADS)[:, None], (HEADS * DK, LANES))
    sin_c = jnp.broadcast_to(jnp.tile(sin_s[0], HEADS)[:, None], (HEADS * DK, LANES))
    dec_c = jnp.broadcast_to(jnp.repeat(log_gamma, DK)[:, None], (HEADS * DK, LANES))
    rq_t = _to_tiles_t(ps[:, C_RQ:C_RQ + 256])
    rk_t = _to_tiles_t(ps[:, C_RK:C_RK + 256])
    oc_s, ret_s = _la_sample(rq_t, rk_t, dec_c, cos_c, sin_c, p_main, C_RV, C_RG, g_ret, s_ret[l], True)
    sgu_w0 = jnp.repeat(sgu_w[l][:, 0, 0], LANES).reshape(1, 512)
    sgu_b0 = jnp.repeat(sgu_b[l][:, 0], LANES).reshape(1, 512)
    ob_s, od_s, vn_s = _small_sample(p_main, s_pool[l], pw_bf, pscale, sgu_g, sgu_w0, sgu_b0)
    pool_p = p_main[:N_P, C_PIN:C_PIN + 512].reshape(B_P, T_P, 512)[:, T_P - POOL_BUF:]
    pool_s = jnp.concatenate([s_pool[l][:, 1:], ps[:, None, C_PIN:C_PIN + 512]], axis=1)

    branches = [jnp.concatenate([p, s.astype(BF16)], axis=0)
                for p, s in ((oa_p, oa_s), (ob_p, ob_s), (oc_p, oc_s), (od_p, od_s))]
    merged = _merge(h, branches, w_merge_gate[l].astype(BF16), b_merge_gate[l].reshape(1, 4 * D),
                    w_branch[l].astype(BF16))
    x = _outproj(merged, w_out[l].astype(BF16), x, norm_mix_post[l], mod, l)

    h2, h2_f32, h2_packed = _prenorm(x, norm_ffn_pre[l], mod, l, 3, 4, True)
    rw = jnp.pad(router_w[l], ((0, 0), (0, LANES - N_EXPERTS)))
    rb = jnp.broadcast_to(router_bias[l][:, None], (N_EXPERTS, SUB))
    eid, pos, wt, counts = _router(h2_f32, rw, rb)
    counts = counts[:, 0].astype(jnp.int32)
    padded = (counts + EXP_BLOCK - 1) // EXP_BLOCK * EXP_BLOCK
    pad_end = jnp.cumsum(padded)
    pad_start = pad_end - padded
    nused = (pad_end[-1] // EXP_BLOCK).astype(jnp.int32).reshape(1)
    blk_row = jnp.arange(N_BLOCKS, dtype=jnp.int32) * EXP_BLOCK
    block_e = jnp.minimum(jnp.sum((blk_row[:, None] >= pad_end[None, :]).astype(jnp.int32), axis=1),
                          N_EXPERTS - 1)
    first = jnp.concatenate([jnp.ones((1,), jnp.int32), (block_e[1:] != block_e[:-1]).astype(jnp.int32)])
    start_of = jnp.sum(jnp.where(eid[:, :, None] == jnp.arange(N_EXPERTS), pad_start.astype(jnp.int32), 0), axis=-1)
    slots = (start_of + pos).T.reshape(N_ASSIGN)
    slot_src = (N + jnp.arange(L_SLOTS, dtype=jnp.int32) % ZERO_ROWS).at[slots].set(
        jnp.arange(N_ASSIGN, dtype=jnp.int32) // TOP_K)
    wt = jnp.pad(wt.T, ((0, 0), (0, LANES - TOP_K)))
    xs = _sc_gather(h2_packed, slot_src)
    ys = _experts(block_e, first, nused, xs, expert_w1, expert_w3, expert_w2, l)
    w13 = jnp.concatenate([shared_w1[l], shared_w3[l]], axis=1).astype(BF16)
    shared = _shared(h2, w13, shared_w2[l].astype(BF16))
    x = _combine(_sc_gather(ys, slots), wt, shared, x, norm_ffn_post[l], mod, l)
    return x, (gla_p, gla_s, pool_p, pool_s, ret_p, ret_s, vn_s)


def kernel(x_prompt, x_sample, c_prompt, c_sample, state_gla, state_pool, state_ret, w_ada, b_ada, norm_mix_pre, norm_mix_post, norm_ffn_pre, norm_ffn_post, w_in, w_gla_gate, b_gla_gate, gla_norm, pool_w, pool_scale, ret_norm, sgu_norm, sgu_w, sgu_b, w_branch, w_merge_gate, b_merge_gate, w_out, router_w, router_bias, expert_w1, expert_w3, expert_w2, shared_w1, shared_w3, shared_w2):
    wts = (norm_mix_pre, norm_mix_post, norm_ffn_pre, norm_ffn_post, w_in, w_gla_gate, b_gla_gate, gla_norm,
           pool_w, pool_scale, ret_norm, sgu_norm, sgu_w, sgu_b, w_branch, w_merge_gate, b_merge_gate, w_out,
           router_w, router_bias, expert_w1, expert_w3, expert_w2, shared_w1, shared_w3, shared_w2)
    c_all = jnp.zeros((MOD_ROWS, D), F32).at[:B_P].set(c_prompt).at[SUB:SUB + N_S].set(c_sample)
    mod = _ada(c_all, w_ada, b_ada)
    x = jnp.concatenate([x_prompt.reshape(N_P, D), x_sample.reshape(N_S, D)], axis=0)
    per_layer = []
    for l in range(DEPTH):
        x, states = _layer(l, x, mod, state_gla, state_pool, state_ret, wts)
        per_layer.append(states)
    gla_p, gla_s, pool_p, pool_s, ret_p, ret_s, vn_s = (jnp.stack(z) for z in zip(*per_layer))
    return (x[:N_P].reshape(B_P, T_P, D), x[N_P:].reshape(N_S, 1, D),
            gla_p, gla_s, pool_p, pool_s, ret_p, ret_s, vn_s.reshape(DEPTH, N_S, 1, 512))
```

```python
import functools

import jax
import jax.numpy as jnp
from jax import lax
from jax.experimental import pallas as pl
from jax.experimental.pallas import tpu as pltpu
from jax.experimental.pallas import tpu_sc as plsc

F32 = jnp.float32
BF16 = jnp.bfloat16
HIGHEST = lax.Precision.HIGHEST

D = 2048
B_P, T_P = 4, 2048
N_P = B_P * T_P
N_S = 128
N = N_P + N_S
DEPTH = 2
PAST_LEN = 16384
EPS = 1e-6
HEADS, DK, DV = 4, 64, 128
CHUNK = 64
GATE_TEMP = 16.0
POOL_WINDOWS = (2, 4, 8, 16)
POOL_BUF = 15
ROPE_BASE = 10000.0
N_EXPERTS = 64
TOP_K = 8
D_EXPERT = 512
ROUTED_SCALE = 2.5

LANES = 128
SUB = 128
MOD_ROWS = 256
EXP_BLOCK = 256
N_ASSIGN = N * TOP_K
N_BLOCKS = -(-(N_ASSIGN + N_EXPERTS * (EXP_BLOCK - 1)) // EXP_BLOCK)
L_SLOTS = N_BLOCKS * EXP_BLOCK
VMEM_LIMIT = 56 * 1024 * 1024

C_GQ, C_GK, C_GV, C_GR, C_PIN, C_RQ, C_RK, C_RV, C_RG, C_SU, C_SV = (
    0, 256, 512, 1024, 1536, 2048, 2304, 2560, 3072, 3584, 4096)
P_MAIN = 4608


def _cparams(n_axes=1):
    return pltpu.CompilerParams(dimension_semantics=("arbitrary",) * n_axes,
                                vmem_limit_bytes=VMEM_LIMIT)


def _silu(x):
    return x * jax.nn.sigmoid(x)


def _mod_rows(t, mp_ref, ms_ref):
    b = jnp.minimum(t // (T_P // SUB), B_P - 1)
    return jnp.where(t >= N_P // SUB, ms_ref[...], mp_ref[pl.ds(b, 1), :])


def _mod_specs(layer, part):
    return [pl.BlockSpec((None, 8, D), lambda i, l=layer, p=part: (l, 0, p)),
            pl.BlockSpec((None, SUB, D), lambda i, l=layer, p=part: (l, 1, p))]


def _pack_bf16_pair(lo, hi):
    lo_u = lax.bitcast_convert_type(lo.astype(BF16).astype(F32), jnp.uint32)
    hi_u = lax.bitcast_convert_type(hi.astype(BF16).astype(F32), jnp.uint32)
    return lax.bitcast_convert_type((hi_u & jnp.uint32(0xFFFF0000)) | (lo_u >> 16), jnp.int32)


def _unpack_bf16_pair(w):
    u = lax.bitcast_convert_type(w, jnp.uint32)
    lo = lax.bitcast_convert_type(u << 16, F32)
    hi = lax.bitcast_convert_type(u & jnp.uint32(0xFFFF0000), F32)
    return lo, hi


ROW_TILE = (8, LANES)


def _load_row_tiles(ref):
    return jnp.concatenate([ref[:, c, :] for c in range(ROW_TILE[0])], axis=1)


def _store_row_tiles(ref, val):
    for c in range(ROW_TILE[0]):
        ref[:, c, :] = val[:, c * LANES:(c + 1) * LANES]


def _load_row_tiles_2d(ref, rows):
    return jnp.concatenate([ref[pl.ds(c, rows, stride=ROW_TILE[0]), :] for c in range(ROW_TILE[0])], axis=1)


def _store_row_tiles_2d(ref, val, rows):
    for c in range(ROW_TILE[0]):
        ref[pl.ds(c, rows, stride=ROW_TILE[0]), :] = val[:, c * LANES:(c + 1) * LANES]


def _ada_body(c_ref, w_ref, b_ref, o_ref):
    s = _silu(c_ref[...]).astype(BF16)
    o_ref[...] = jnp.dot(s, w_ref[...].astype(BF16), preferred_element_type=F32) + b_ref[...]


def _ada(c_all, w_ada, b_ada):
    tn = 1024
    return pl.pallas_call(
        _ada_body,
        grid=(DEPTH, 6 * D // tn),
        in_specs=[pl.BlockSpec((MOD_ROWS, D), lambda l, j: (0, 0)),
                  pl.BlockSpec((None, D, tn), lambda l, j: (l, 0, j)),
                  pl.BlockSpec((None, 1, tn), lambda l, j: (l, 0, j))],
        out_specs=pl.BlockSpec((None, MOD_ROWS, tn), lambda l, j: (l, 0, j)),
        out_shape=jax.ShapeDtypeStruct((DEPTH, MOD_ROWS, 6 * D), F32),
        compiler_params=_cparams(2), name="ada")(c_all, w_ada, b_ada.reshape(DEPTH, 1, 6 * D))


ZERO_ROWS = 2 * SUB


def _prenorm_body(x_ref, g_ref, shp_ref, shs_ref, scp_ref, scs_ref, h_ref, *extra):
    t = jnp.minimum(pl.program_id(0), N // SUB - 1)
    x = x_ref[...]
    y = x * lax.rsqrt(jnp.mean(x * x, axis=-1, keepdims=True) + EPS) * g_ref[...]
    h = y * (1.0 + _mod_rows(t, scp_ref, scs_ref)) + _mod_rows(t, shp_ref, shs_ref)
    h_ref[...] = h.astype(BF16)
    if extra:
        hf_ref, hp_ref = extra
        hf_ref[...] = h
        _store_row_tiles(hp_ref, _pack_bf16_pair(h[:, :D // 2], h[:, D // 2:]))

        @pl.when(pl.program_id(0) >= N // SUB)
        def _():
            hp_ref[...] = jnp.zeros_like(hp_ref)


def _prenorm(x, g, mod, layer, shift_part, scale_part, with_extra):
    last = N // SUB - 1
    n_steps = N // SUB + (ZERO_ROWS // SUB if with_extra else 0)
    row = lambda i: (jnp.minimum(i, last), 0)
    out_shape = [jax.ShapeDtypeStruct((N, D), BF16)]
    out_specs = [pl.BlockSpec((SUB, D), row)]
    if with_extra:
        out_shape += [jax.ShapeDtypeStruct((N, D), F32), jax.ShapeDtypeStruct((N + ZERO_ROWS,) + ROW_TILE, jnp.int32)]
        out_specs += [pl.BlockSpec((SUB, D), row), pl.BlockSpec((SUB,) + ROW_TILE, lambda i: (i, 0, 0))]
    return pl.pallas_call(
        _prenorm_body,
        grid=(n_steps,),
        in_specs=[pl.BlockSpec((SUB, D), row),
                  pl.BlockSpec((1, D), lambda i: (0, 0))]
        + _mod_specs(layer, shift_part) + _mod_specs(layer, scale_part),
        out_specs=out_specs, out_shape=out_shape,
        compiler_params=_cparams(1), name="prenorm")(x, g.reshape(1, D), mod, mod, mod, mod)


def _mm_body(x_ref, w_ref, o_ref):
    o_ref[...] = jnp.dot(x_ref[...], w_ref[...], preferred_element_type=F32).astype(o_ref.dtype)


def _matmul(x, w, tm, tn, out_dtype=F32, name="mm"):
    m, k = x.shape
    n = w.shape[1]
    return pl.pallas_call(
        _mm_body,
        grid=(m // tm, n // tn),
        in_specs=[pl.BlockSpec((tm, k), lambda i, j: (i, 0)),
                  pl.BlockSpec((k, tn), lambda i, j: (0, j))],
        out_specs=pl.BlockSpec((tm, tn), lambda i, j: (i, j)),
        out_shape=jax.ShapeDtypeStruct((m, n), out_dtype),
        compiler_params=_cparams(2), name=name)(x, w)


ROWS_LA = 256


def _swap_halves_lanes(x):
    lane = lax.broadcasted_iota(jnp.int32, x.shape, 1)
    return jnp.where((lane % 64) < 32, pltpu.roll(x, 96, 1), pltpu.roll(x, 32, 1))


def _rope_lanes(x, cos, sin_signed):
    parts = []
    for half in range(2):
        xh = x[:, half * LANES:(half + 1) * LANES]
        parts.append(xh * cos + _swap_halves_lanes(xh) * sin_signed)
    return jnp.concatenate(parts, axis=1)


def _la_prompt_body(q_ref, k_ref, v_ref, r_ref, aux_ref, aux2_ref, dec_ref, bias_ref, g_ref,
                    o_ref, st_out_ref, st_ref, *, retention):
    t = pl.program_id(1)

    @pl.when(t == 0)
    def _():
        st_ref[...] = jnp.zeros_like(st_ref)

    ri = lax.broadcasted_iota(jnp.int32, (CHUNK, CHUNK), 0)
    ci = lax.broadcasted_iota(jnp.int32, (CHUNK, CHUNK), 1)
    causal = ri >= ci
    tril = causal.astype(F32)
    scale = DK ** -0.5

    for c in range(ROWS_LA // CHUNK):
        rows = pl.ds(c * CHUNK, CHUNK)
        q = q_ref[rows, :]
        k = k_ref[rows, :]
        v = v_ref[rows, :]
        if retention:
            cos = aux_ref[rows, :]
            sin = aux2_ref[rows, :]
            q = _rope_lanes(q, cos, sin)
            k = _rope_lanes(k, cos, sin) * scale
            la = jnp.broadcast_to(dec_ref[...], (CHUNK, HEADS * DK))
        else:
            q = q * scale
            logit = jnp.dot(aux_ref[rows, :], dec_ref[...], precision=HIGHEST,
                            preferred_element_type=F32) + bias_ref[...]
            la = jax.nn.log_sigmoid(logit) / GATE_TEMP
        bc = jnp.dot(tril, la, precision=HIGHEST, preferred_element_type=F32)
        bl = bc[CHUNK - 1:CHUNK, :]
        qd = q * jnp.exp(bc)
        ki = k * jnp.exp(-bc)
        ke = k * jnp.exp(bl - bc)
        ac = jnp.exp(bl)
        outs = []
        for h in range(HEADS):
            ks = slice(h * DK, (h + 1) * DK)
            vs = slice(h * DV, (h + 1) * DV)
            qd_h = qd[:, ks].astype(BF16)
            ki_h = ki[:, ks].astype(BF16)
            ke_h = ke[:, ks].astype(BF16)
            v_h = v[:, vs].astype(BF16)
            sc = lax.dot_general(qd_h, ki_h, (((1,), (1,)), ((), ())), preferred_element_type=F32)
            sc = jnp.where(causal, sc, 0.0)
            o_h = jnp.dot(sc.astype(BF16), v_h, preferred_element_type=F32)
            st = st_ref[h]
            o_h = o_h + lax.dot_general(qd_h, st.astype(BF16), (((1,), (1,)), ((), ())),
                                        preferred_element_type=F32)
            kv_t = lax.dot_general(v_h, ke_h, (((0,), (0,)), ((), ())), preferred_element_type=F32)
            st_ref[h] = st * ac[:, ks] + kv_t
            o_n = o_h * lax.rsqrt(jnp.mean(o_h * o_h, axis=-1, keepdims=True) + EPS) * g_ref[:, vs]
            outs.append(o_n)
        o = jnp.concatenate(outs, axis=1) * _silu(r_ref[rows, :])
        o_ref[rows, :] = o.astype(BF16)

    st_out_ref[...] = st_ref[...]


def _la_prompt(p_main, cq, ck, cv, cr, aux, aux2, dec, bias, g, retention):
    nt = T_P // ROWS_LA
    rowblk = lambda b, t: b * nt + t
    if retention:
        aux_specs = [pl.BlockSpec((ROWS_LA, LANES), lambda b, t: (t, 0)),
                     pl.BlockSpec((ROWS_LA, LANES), lambda b, t: (t, 0))]
    else:
        aux_specs = [pl.BlockSpec((ROWS_LA, LANES), lambda b, t: (rowblk(b, t), 0)),
                     pl.BlockSpec((8, LANES), lambda b, t: (0, 0))]
    o, st = pl.pallas_call(
        functools.partial(_la_prompt_body, retention=retention),
        grid=(B_P, nt),
        in_specs=[pl.BlockSpec((ROWS_LA, 256), lambda b, t: (rowblk(b, t), cq // 256)),
                  pl.BlockSpec((ROWS_LA, 256), lambda b, t: (rowblk(b, t), ck // 256)),
                  pl.BlockSpec((ROWS_LA, 512), lambda b, t: (rowblk(b, t), cv // 512)),
                  pl.BlockSpec((ROWS_LA, 512), lambda b, t: (rowblk(b, t), cr // 512))]
        + aux_specs
        + [pl.BlockSpec(dec.shape, lambda b, t: (0, 0)),
           pl.BlockSpec((1, HEADS * DK), lambda b, t: (0, 0)),
           pl.BlockSpec((1, HEADS * DV), lambda b, t: (0, 0))],
        out_specs=[pl.BlockSpec((ROWS_LA, HEADS * DV), lambda b, t: (rowblk(b, t), 0)),
                   pl.BlockSpec((None, HEADS, DV, DK), lambda b, t: (b, 0, 0, 0))],
        out_shape=[jax.ShapeDtypeStruct((N_P, HEADS * DV), BF16),
                   jax.ShapeDtypeStruct((B_P, HEADS, DV, DK), F32)],
        scratch_shapes=[pltpu.VMEM((HEADS, DV, DK), F32)],
        compiler_params=_cparams(2), name="ret_prompt" if retention else "gla_prompt",
    )(p_main, p_main, p_main, p_main, aux, aux2, dec, bias, g)
    return o, jnp.swapaxes(st, -1, -2)


SAMPLE_TILE = 8


def _la_sample_body(qt_ref, kt_ref, lt_ref, cos_ref, sin_ref, v_ref, r_ref, g_ref, s_ref,
                    o_ref, s_out_ref, *, retention):
    scale = DK ** -0.5
    qt = qt_ref[...]
    kt = kt_ref[...]
    if retention:
        def rope(x):
            sw = jnp.concatenate(
                [x[h * DK + (DK // 2) * (1 - j): h * DK + (DK // 2) * (2 - j), :]
                 for h in range(HEADS) for j in range(2)], axis=0)
            return x * cos_ref[...] + sw * sin_ref[...]
        qt = rope(qt)
        kt = rope(kt) * scale
        la = lt_ref[...]
    else:
        qt = qt * scale
        la = jax.nn.log_sigmoid(lt_ref[...]) / GATE_TEMP
    at = jnp.exp(la)
    qd = qt * at
    ki = kt * jnp.exp(-la)
    prod = qd * ki
    v8 = v_ref[...]
    r8 = r_ref[...]
    g = g_ref[...]
    for j in range(SAMPLE_TILE):
        for h in range(HEADS):
            ks = slice(h * DK, (h + 1) * DK)
            vs = slice(h * DV, (h + 1) * DV)
            a_c = jnp.broadcast_to(at[ks, j:j + 1], (DK, DV))
            k_c = jnp.broadcast_to(kt[ks, j:j + 1], (DK, DV))
            q_c = jnp.broadcast_to(qd[ks, j:j + 1], (DK, DV))
            s_c = jnp.broadcast_to(jnp.sum(prod[ks, j:j + 1], axis=0, keepdims=True), (1, DV))
            s0 = s_ref[j, h]
            v_row = v8[j:j + 1, vs]
            s_out_ref[j, h] = a_c * s0 + k_c * v_row
            o_row = s_c * v_row + jnp.sum(q_c * s0, axis=0, keepdims=True)
            o_n = o_row * lax.rsqrt(jnp.mean(o_row * o_row, axis=-1, keepdims=True) + EPS) * g[:, vs]
            o_ref[j:j + 1, vs] = o_n * _silu(r8[j:j + 1, vs])


def _la_sample(qt, kt, lt, cos_t, sin_t, p_main, cv, cr, g, s0, retention):
    nt = N_S // SAMPLE_TILE
    row0 = N_P // SAMPLE_TILE
    tile = pl.BlockSpec((None, HEADS * DK, LANES), lambda i: (i, 0, 0))
    full = pl.BlockSpec((HEADS * DK, LANES), lambda i: (0, 0))
    lt_spec = full if retention else tile
    return pl.pallas_call(
        functools.partial(_la_sample_body, retention=retention),
        grid=(nt,),
        in_specs=[tile, tile, lt_spec, full, full,
                  pl.BlockSpec((SAMPLE_TILE, 512), lambda i: (row0 + i, cv // 512)),
                  pl.BlockSpec((SAMPLE_TILE, 512), lambda i: (row0 + i, cr // 512)),
                  pl.BlockSpec((1, HEADS * DV), lambda i: (0, 0)),
                  pl.BlockSpec((SAMPLE_TILE, HEADS, DK, DV), lambda i: (i, 0, 0, 0))],
        out_specs=[pl.BlockSpec((SAMPLE_TILE, HEADS * DV), lambda i: (i, 0)),
                   pl.BlockSpec((SAMPLE_TILE, HEADS, DK, DV), lambda i: (i, 0, 0, 0))],
        out_shape=[jax.ShapeDtypeStruct((N_S, HEADS * DV), F32),
                   jax.ShapeDtypeStruct((N_S, HEADS, DK, DV), F32)],
        compiler_params=_cparams(1), name="ret_sample" if retention else "gla_sample",
    )(qt, kt, lt, cos_t, sin_t, p_main, p_main, g, s0)


def _gate_logits_t_body(w_ref, x_ref, b_ref, o_ref):
    o_ref[...] = jnp.dot(w_ref[...], x_ref[...], precision=HIGHEST, preferred_element_type=F32) + b_ref[...]


def _gate_logits_t(w_gate_t, glow_t, b_col):
    nt = N_S // SAMPLE_TILE
    return pl.pallas_call(
        _gate_logits_t_body,
        grid=(nt,),
        in_specs=[pl.BlockSpec((HEADS * DK, LANES), lambda i: (0, 0)),
                  pl.BlockSpec((None, LANES, LANES), lambda i: (i, 0, 0)),
                  pl.BlockSpec((HEADS * DK, LANES), lambda i: (0, 0))],
        out_specs=pl.BlockSpec((None, HEADS * DK, LANES), lambda i: (i, 0, 0)),
        out_shape=jax.ShapeDtypeStruct((nt, HEADS * DK, LANES), F32),
        compiler_params=_cparams(1), name="gate_logits_t")(w_gate_t, glow_t, b_col)


def _to_tiles_t(x):
    c = x.shape[1]
    xt = jnp.swapaxes(x.reshape(N_S // SAMPLE_TILE, SAMPLE_TILE, c), 1, 2)
    return jnp.pad(xt, ((0, 0), (0, 0), (0, LANES - SAMPLE_TILE)))


ROWS_POOL = 512


def _pool_mix(y, w_ref, sc_ref):
    outs = []
    for gi in range(4):
        cs = slice(gi * LANES, (gi + 1) * LANES)
        outs.append(jnp.dot(y[:, cs].astype(BF16), w_ref[gi], preferred_element_type=F32))
    return jnp.concatenate(outs, axis=1) * sc_ref[...]


def _pool_prompt_body(p_ref, halo_ref, w_ref, sc_ref, o_ref):
    t = pl.program_id(1)
    p = p_ref[...]
    halo = jnp.where(t == 0, 0.0, halo_ref[...])
    full = jnp.concatenate([halo, p], axis=0)
    pos = t * ROWS_POOL + lax.broadcasted_iota(jnp.int32, (ROWS_POOL, LANES), 0)
    means = []
    for gi, w in enumerate(POOL_WINDOWS):
        s = full[:, gi * LANES:(gi + 1) * LANES]
        step = 1
        while step < w:
            s = s + pltpu.roll(s, step, 0)
            step *= 2
        win = s[16:, :]
        cnt = jnp.minimum(w, pos + 1).astype(F32)
        means.append(win / cnt)
    y = jnp.concatenate(means, axis=1) - p
    o_ref[...] = _pool_mix(y, w_ref, sc_ref).astype(BF16)


def _pool_prompt(p_main, w_bf, scale):
    nt = T_P // ROWS_POOL
    return pl.pallas_call(
        _pool_prompt_body,
        grid=(B_P, nt),
        in_specs=[pl.BlockSpec((ROWS_POOL, 512), lambda b, t: (b * nt + t, C_PIN // 512)),
                  pl.BlockSpec((16, 512), lambda b, t: (jnp.maximum((b * nt + t) * (ROWS_POOL // 16) - 1, 0),
                                                        C_PIN // 512)),
                  pl.BlockSpec((4, LANES, LANES), lambda b, t: (0, 0, 0)),
                  pl.BlockSpec((1, 512), lambda b, t: (0, 0))],
        out_specs=pl.BlockSpec((ROWS_POOL, 512), lambda b, t: (b * nt + t, 0)),
        out_shape=jax.ShapeDtypeStruct((N_P, 512), BF16),
        compiler_params=_cparams(2), name="pool_prompt")(p_main, p_main, w_bf, scale)


def _small_sample_body(p_ref, buf_ref, pw_ref, psc_ref, u_ref, sv_ref, sg_ref, sw_ref, sb_ref,
                       ob_ref, od_ref, vn_ref):
    p = p_ref[...]
    means = []
    for gi, w in enumerate(POOL_WINDOWS):
        cs = slice(gi * LANES, (gi + 1) * LANES)
        s = p[:, cs]
        for j in range(1, w):
            s = s + buf_ref[:, POOL_BUF - j, cs]
        means.append(s / float(min(w, PAST_LEN + 1)))
    y = jnp.concatenate(means, axis=1) - p
    ob_ref[...] = _pool_mix(y, pw_ref, psc_ref)
    sv = sv_ref[...]
    vn = sv * lax.rsqrt(jnp.mean(sv * sv, axis=-1, keepdims=True) + EPS) * sg_ref[...]
    vn_ref[...] = vn
    od_ref[...] = u_ref[...] * (sw_ref[...] * vn + sb_ref[...])


def _small_sample(p_main, buf, pw_bf, pscale, sgu_g, sgu_w0, sgu_b0):
    row = N_P // N_S
    col = lambda c: pl.BlockSpec((N_S, 512), lambda i, c=c: (row, c // 512))
    vec = pl.BlockSpec((1, 512), lambda i: (0, 0))
    return pl.pallas_call(
        _small_sample_body,
        grid=(1,),
        in_specs=[col(C_PIN), pl.BlockSpec((N_S, POOL_BUF, 512), lambda i: (0, 0, 0)),
                  pl.BlockSpec((4, LANES, LANES), lambda i: (0, 0, 0)), vec,
                  col(C_SU), col(C_SV), vec, vec, vec],
        out_specs=[pl.BlockSpec((N_S, 512), lambda i: (0, 0))] * 3,
        out_shape=[jax.ShapeDtypeStruct((N_S, 512), F32)] * 3,
        compiler_params=_cparams(1), name="small_sample",
    )(p_main, buf, pw_bf, pscale, p_main, p_main, sgu_g, sgu_w0, sgu_b0)


ROWS_SGU = 512
SGU_CHUNK = 128


def _sgu_prompt_body(u_ref, v_ref, g_ref, w_ref, bt_ref, o_ref):
    ri = lax.broadcasted_iota(jnp.int32, (SGU_CHUNK, SGU_CHUNK), 0)
    ci = lax.broadcasted_iota(jnp.int32, (SGU_CHUNK, SGU_CHUNK), 1)
    causal = ri >= ci
    for c in range(ROWS_SGU // SGU_CHUNK):
        rows = pl.ds(c * SGU_CHUNK, SGU_CHUNK)
        v = v_ref[rows, :]
        vn = (v * lax.rsqrt(jnp.mean(v * v, axis=-1, keepdims=True) + EPS) * g_ref[...]).astype(BF16)
        outs = []
        for gi in range(4):
            cs = slice(gi * LANES, (gi + 1) * LANES)
            w = jnp.where(causal, w_ref[gi], 0.0).astype(BF16)
            mixed = jnp.dot(w, vn[:, cs], preferred_element_type=F32)
            outs.append(mixed + jnp.broadcast_to(bt_ref[:, gi:gi + 1], (SGU_CHUNK, LANES)))
        o_ref[rows, :] = (u_ref[rows, :] * jnp.concatenate(outs, axis=1)).astype(BF16)


def _sgu_prompt(p_main, g, w, b_t):
    return pl.pallas_call(
        _sgu_prompt_body,
        grid=(N_P // ROWS_SGU,),
        in_specs=[pl.BlockSpec((ROWS_SGU, 512), lambda i: (i, C_SU // 512)),
                  pl.BlockSpec((ROWS_SGU, 512), lambda i: (i, C_SV // 512)),
                  pl.BlockSpec((1, 512), lambda i: (0, 0)),
                  pl.BlockSpec((4, SGU_CHUNK, SGU_CHUNK), lambda i: (0, 0, 0)),
                  pl.BlockSpec((SGU_CHUNK, LANES), lambda i: (0, 0))],
        out_specs=pl.BlockSpec((ROWS_SGU, 512), lambda i: (i, 0)),
        out_shape=jax.ShapeDtypeStruct((N_P, 512), BF16),
        compiler_params=_cparams(1), name="sgu_prompt")(p_main, p_main, g, w, b_t)


TM_MERGE = 640
TN_MERGE = 512


def _merge_body(h_ref, ba_ref, bb_ref, bc_ref, bd_ref, g0, g1, g2, g3, u0, u1, u2, u3,
                c0, c1, c2, c3, o_ref):
    h = h_ref[...]
    acc = None
    for br, gw, uw, gb in ((ba_ref, g0, u0, c0), (bb_ref, g1, u1, c1), (bc_ref, g2, u2, c2), (bd_ref, g3, u3, c3)):
        gate = jax.nn.sigmoid(jnp.dot(h, gw[...], preferred_element_type=F32) + gb[...])
        up = jnp.dot(br[...], uw[...], preferred_element_type=F32)
        acc = gate * up if acc is None else acc + gate * up
    o_ref[...] = acc.astype(BF16)


def _merge(h, branches, w_mg, b_mg, w_br):
    nj = D // TN_MERGE
    row = lambda w: pl.BlockSpec((TM_MERGE, w), lambda i, j: (i, 0))
    gate_w = [pl.BlockSpec((D, TN_MERGE), lambda i, j, b=b: (0, b * nj + j)) for b in range(4)]
    up_w = [pl.BlockSpec((None, 512, TN_MERGE), lambda i, j, b=b: (b, 0, j)) for b in range(4)]
    gate_b = [pl.BlockSpec((1, TN_MERGE), lambda i, j, b=b: (0, b * nj + j)) for b in range(4)]
    return pl.pallas_call(
        _merge_body,
        grid=(N // TM_MERGE, nj),
        in_specs=[row(D)] + [row(512)] * 4 + gate_w + up_w + gate_b,
        out_specs=pl.BlockSpec((TM_MERGE, TN_MERGE), lambda i, j: (i, j)),
        out_shape=jax.ShapeDtypeStruct((N, D), BF16),
        compiler_params=_cparams(2), name="merge",
    )(h, *branches, w_mg, w_mg, w_mg, w_mg, w_br, w_br, w_br, w_br, b_mg, b_mg, b_mg, b_mg)


TM_OUT = 640
TN_OUT = 512


def _post_residual(t0, x_ref, get_y, gn_ref, gp_ref, gs_ref, o_ref, rows_total):
    for s in range(rows_total // SUB):
        rows = pl.ds(s * SUB, SUB)
        y = get_y(rows)
        yn = y * lax.rsqrt(jnp.mean(y * y, axis=-1, keepdims=True) + EPS) * gn_ref[...]
        o_ref[rows, :] = x_ref[rows, :] + _mod_rows(t0 + s, gp_ref, gs_ref) * yn


def _outproj_body(m_ref, w_ref, x_ref, gn_ref, gp_ref, gs_ref, o_ref, acc_ref):
    j = pl.program_id(1)
    acc_ref[j] = jnp.dot(m_ref[...], w_ref[...], preferred_element_type=F32)

    @pl.when(j == D // TN_OUT - 1)
    def _():
        get_y = lambda rows: jnp.concatenate([acc_ref[c, rows, :] for c in range(D // TN_OUT)], axis=1)
        _post_residual(pl.program_id(0) * (TM_OUT // SUB), x_ref, get_y, gn_ref, gp_ref, gs_ref,
                       o_ref, TM_OUT)


def _outproj(merged, w_out, x, g_post, mod, layer):
    mspec = [pl.BlockSpec((None, 8, D), lambda i, j, l=layer: (l, 0, 2)),
             pl.BlockSpec((None, SUB, D), lambda i, j, l=layer: (l, 1, 2))]
    return pl.pallas_call(
        _outproj_body,
        grid=(N // TM_OUT, D // TN_OUT),
        in_specs=[pl.BlockSpec((TM_OUT, D), lambda i, j: (i, 0)),
                  pl.BlockSpec((D, TN_OUT), lambda i, j: (0, j)),
                  pl.BlockSpec((TM_OUT, D), lambda i, j: (i, 0)),
                  pl.BlockSpec((1, D), lambda i, j: (0, 0))] + mspec,
        out_specs=pl.BlockSpec((TM_OUT, D), lambda i, j: (i, 0)),
        out_shape=jax.ShapeDtypeStruct((N, D), F32),
        scratch_shapes=[pltpu.VMEM((D // TN_OUT, TM_OUT, TN_OUT), F32)],
        compiler_params=_cparams(2), name="outproj",
    )(merged, w_out, x, g_post.reshape(1, D), mod, mod)


def _router_body(h_ref, w_ref, b_ref, eid_ref, pos_ref, wt_ref, cnt_ref, run_ref):
    i = pl.program_id(0)

    @pl.when(i == 0)
    def _():
        run_ref[...] = jnp.zeros_like(run_ref)

    ng, gs = 8, N_EXPERTS // 8
    neg = -jnp.inf
    logits = jnp.dot(h_ref[...], w_ref[...], precision=HIGHEST, preferred_element_type=F32)
    scores = jax.nn.sigmoid(logits.T[:N_EXPERTS, :])
    sel = scores + b_ref[...]
    sel3 = sel.reshape(ng, gs, SUB)
    sub3 = lax.broadcasted_iota(jnp.int32, (ng, gs, SUB), 1)
    gmax = jnp.max(sel3, axis=1, keepdims=True)
    first = jnp.min(jnp.where(sel3 == gmax, sub3, gs), axis=1, keepdims=True)
    gmax2 = jnp.max(jnp.where(sub3 == first, neg, sel3), axis=1, keepdims=True)
    gscore = (gmax + gmax2).reshape(ng, SUB)
    gidx = lax.broadcasted_iota(jnp.int32, (ng, SUB), 0)
    grank = jnp.zeros((ng, SUB), jnp.int32)
    for s in range(1, ng):
        other = pltpu.roll(gscore, s, 0)
        lower = gidx >= s
        grank += ((other > gscore) | ((other == gscore) & lower)).astype(jnp.int32)
    keep = jnp.broadcast_to((grank < 4).reshape(ng, 1, SUB), (ng, gs, SUB))
    masked = jnp.where(keep, sel3, neg).reshape(N_EXPERTS, SUB)
    eidx = lax.broadcasted_iota(jnp.int32, (N_EXPERTS, SUB), 0)
    rank = jnp.zeros((N_EXPERTS, SUB), jnp.int32)
    for s in range(1, N_EXPERTS):
        other = pltpu.roll(masked, s, 0)
        lower = eidx >= s
        rank += ((other > masked) | ((other == masked) & lower)).astype(jnp.int32)
    chosen = rank < TOP_K
    w_sel = jnp.where(chosen, scores, 0.0)
    w_sel = w_sel / jnp.sum(w_sel, axis=0, keepdims=True) * ROUTED_SCALE
    ri = lax.broadcasted_iota(jnp.int32, (SUB, SUB), 0)
    ci = lax.broadcasted_iota(jnp.int32, (SUB, SUB), 1)
    onehot = chosen.astype(BF16)
    pos = jnp.dot(onehot, (ri < ci).astype(BF16), preferred_element_type=F32) + run_ref[...]
    run_ref[...] = run_ref[...] + jnp.sum(chosen.astype(F32), axis=1, keepdims=True)
    cnt_ref[...] = run_ref[...]
    eidx_f = eidx.astype(F32)
    rows_e, rows_p, rows_w = [], [], []
    for kk in range(TOP_K):
        m = chosen & (rank == kk)
        rows_e.append(jnp.sum(jnp.where(m, eidx_f, 0.0), axis=0, keepdims=True))
        rows_p.append(jnp.sum(jnp.where(m, pos, 0.0), axis=0, keepdims=True))
        rows_w.append(jnp.sum(jnp.where(m, w_sel, 0.0), axis=0, keepdims=True))
    eid_ref[...] = jnp.concatenate(rows_e, axis=0).astype(jnp.int32)
    pos_ref[...] = jnp.concatenate(rows_p, axis=0).astype(jnp.int32)
    wt_ref[...] = jnp.concatenate(rows_w, axis=0)


def _router(h_f32, rw_pad, rb_col):
    tile = pl.BlockSpec((TOP_K, SUB), lambda i: (0, i))
    return pl.pallas_call(
        _router_body,
        grid=(N // SUB,),
        in_specs=[pl.BlockSpec((SUB, D), lambda i: (i, 0)),
                  pl.BlockSpec((D, LANES), lambda i: (0, 0)),
                  pl.BlockSpec((N_EXPERTS, SUB), lambda i: (0, 0))],
        out_specs=[tile, tile, tile, pl.BlockSpec((N_EXPERTS, SUB), lambda i: (0, 0))],
        out_shape=[jax.ShapeDtypeStruct((TOP_K, N), jnp.int32), jax.ShapeDtypeStruct((TOP_K, N), jnp.int32),
                   jax.ShapeDtypeStruct((TOP_K, N), F32), jax.ShapeDtypeStruct((N_EXPERTS, SUB), F32)],
        scratch_shapes=[pltpu.VMEM((N_EXPERTS, SUB), F32)],
        compiler_params=_cparams(1), name="router")(h_f32, rw_pad, rb_col)


SC_CORES, SC_SUBCORES = 2, 16
SC_WORKERS = SC_CORES * SC_SUBCORES
SC_CHUNK = 32


def _sc_gather(table, idx):
    n_out = idx.shape[0]
    per_w = n_out // SC_WORKERS
    assert per_w * SC_WORKERS == n_out and per_w % SC_CHUNK == 0
    mesh = plsc.VectorSubcoreMesh(core_axis_name="c", subcore_axis_name="s",
                                  num_cores=SC_CORES, num_subcores=SC_SUBCORES)

    def body(table_hbm, idx_hbm, out_hbm, idx_v, rows_v, sem):
        wid = lax.axis_index("s") * SC_CORES + lax.axis_index("c")
        base = wid * per_w
        pltpu.sync_copy(idx_hbm.at[pl.ds(base, per_w)], idx_v)

        @pl.loop(0, per_w // SC_CHUNK)
        def _(j):
            off = pl.multiple_of(j * SC_CHUNK, SC_CHUNK)
            pltpu.async_copy(table_hbm.at[idx_v.at[pl.ds(off, SC_CHUNK)]], rows_v, sem).wait()
            pltpu.sync_copy(rows_v, out_hbm.at[pl.ds(base + off, SC_CHUNK)])

    return pl.kernel(
        body, out_type=jax.ShapeDtypeStruct((n_out,) + ROW_TILE, jnp.int32), mesh=mesh,
        scratch_types=[pltpu.VMEM((per_w,), jnp.int32), pltpu.VMEM((SC_CHUNK,) + ROW_TILE, jnp.int32),
                       pltpu.SemaphoreType.DMA],
        name="sc_gather")(table, idx)


def _experts_body(be_ref, first_ref, par_ref, next_ref, nused_ref, x_ref, w1_hbm, w3_hbm, w2_hbm, y_ref,
                  w1f, w3f, w2f, w1b, w3b, w2b, sem, *, layer):
    b = pl.program_id(0)
    used = b < nused_ref[0]

    def copies(e, slot):
        return (pltpu.make_async_copy(w1_hbm.at[layer, e], w1f.at[slot], sem.at[0, slot]),
                pltpu.make_async_copy(w3_hbm.at[layer, e], w3f.at[slot], sem.at[1, slot]),
                pltpu.make_async_copy(w2_hbm.at[layer, e], w2f.at[slot], sem.at[2, slot]))

    @pl.when(b == 0)
    def _():
        for c in copies(be_ref[0], 0):
            c.start()

    @pl.when(used & (first_ref[b] == 1))
    def _():
        slot = par_ref[b]
        for c in copies(be_ref[b], slot):
            c.wait()

        @pl.when(next_ref[b] >= 0)
        def _():
            for c in copies(next_ref[b], 1 - slot):
                c.start()
        w1b[...] = w1f[slot].astype(BF16)
        w3b[...] = w3f[slot].astype(BF16)
        w2b[...] = w2f[slot].astype(BF16)

    @pl.when(used)
    def _():
        lo, hi = _unpack_bf16_pair(_load_row_tiles_2d(x_ref, EXP_BLOCK))
        lo = lo.astype(BF16)
        hi = hi.astype(BF16)
        half = D // 2
        h1 = (jnp.dot(lo, w1b[:half, :], preferred_element_type=F32)
              + jnp.dot(hi, w1b[half:, :], preferred_element_type=F32))
        h3 = (jnp.dot(lo, w3b[:half, :], preferred_element_type=F32)
              + jnp.dot(hi, w3b[half:, :], preferred_element_type=F32))
        hid = (_silu(h1) * h3).astype(BF16)
        y = jnp.dot(hid, w2b[...], preferred_element_type=F32)
        _store_row_tiles_2d(y_ref, _pack_bf16_pair(y[:, :half], y[:, half:]), EXP_BLOCK)

    @pl.when(jnp.logical_not(used))
    def _():
        y_ref[...] = jnp.zeros_like(y_ref)


def _experts(block_e, first, par, next_e, nused, xs, w1, w3, w2, layer):
    blk = lambda b, *refs: (jnp.minimum(b, refs[-1][0] - 1), 0)
    grid_spec = pltpu.PrefetchScalarGridSpec(
        num_scalar_prefetch=5,
        grid=(N_BLOCKS,),
        in_specs=[pl.BlockSpec((EXP_BLOCK * ROW_TILE[0], LANES), blk),
                  pl.BlockSpec(memory_space=pl.ANY), pl.BlockSpec(memory_space=pl.ANY),
                  pl.BlockSpec(memory_space=pl.ANY)],
        out_specs=pl.BlockSpec((EXP_BLOCK * ROW_TILE[0], LANES), lambda b, *refs: (b, 0)),
        scratch_shapes=[pltpu.VMEM((2, D, D_EXPERT), F32), pltpu.VMEM((2, D, D_EXPERT), F32),
                        pltpu.VMEM((2, D_EXPERT, D), F32),
                        pltpu.VMEM((D, D_EXPERT), BF16), pltpu.VMEM((D, D_EXPERT), BF16),
                        pltpu.VMEM((D_EXPERT, D), BF16), pltpu.SemaphoreType.DMA((3, 2))])
    return pl.pallas_call(
        functools.partial(_experts_body, layer=layer), grid_spec=grid_spec,
        out_shape=jax.ShapeDtypeStruct((L_SLOTS * ROW_TILE[0], LANES), jnp.int32),
        compiler_params=_cparams(1), name="experts",
    )(block_e, first, par, next_e, nused, xs.reshape(L_SLOTS * ROW_TILE[0], LANES), w1, w3, w2
      ).reshape((L_SLOTS,) + ROW_TILE)


TM_SHARED = 640


def _shared_body(h_ref, w13_ref, w2_ref, o_ref):
    up = jnp.dot(h_ref[...], w13_ref[...], preferred_element_type=F32)
    hid = (_silu(up[:, :D_EXPERT]) * up[:, D_EXPERT:]).astype(BF16)
    o_ref[...] = jnp.dot(hid, w2_ref[...], preferred_element_type=F32)


def _shared(h, w13, w2):
    return pl.pallas_call(
        _shared_body,
        grid=(N // TM_SHARED,),
        in_specs=[pl.BlockSpec((TM_SHARED, D), lambda i: (i, 0)),
                  pl.BlockSpec((D, 2 * D_EXPERT), lambda i: (0, 0)),
                  pl.BlockSpec((D_EXPERT, D), lambda i: (0, 0))],
        out_specs=pl.BlockSpec((TM_SHARED, D), lambda i: (i, 0)),
        out_shape=jax.ShapeDtypeStruct((N, D), F32),
        compiler_params=_cparams(1), name="shared")(h, w13, w2)


def _combine_body(g_ref, wt_ref, sh_ref, x_ref, gn_ref, gp_ref, gs_ref, o_ref, f_ref):
    half = D // 2
    acc_lo = sh_ref[:, :half]
    acc_hi = sh_ref[:, half:]
    wt = wt_ref[...]
    per_tok = TOP_K * ROW_TILE[0]
    for k in range(TOP_K):
        packed = jnp.concatenate([g_ref[pl.ds(k * ROW_TILE[0] + c, SUB, stride=per_tok), :]
                                  for c in range(ROW_TILE[0])], axis=1)
        lo, hi = _unpack_bf16_pair(packed)
        w_c = wt[:, k:k + 1]
        acc_lo = acc_lo + w_c * lo
        acc_hi = acc_hi + w_c * hi
    f_ref[:, :half] = acc_lo
    f_ref[:, half:] = acc_hi
    _post_residual(pl.program_id(0), x_ref, lambda rows: f_ref[rows, :], gn_ref, gp_ref, gs_ref, o_ref, SUB)


def _combine(gathered, wts, shared, x, g_post, mod, layer):
    per_tok = TOP_K * ROW_TILE[0]
    return pl.pallas_call(
        _combine_body,
        grid=(N // SUB,),
        in_specs=[pl.BlockSpec((SUB * per_tok, LANES), lambda i: (i, 0)),
                  pl.BlockSpec((SUB, LANES), lambda i: (i, 0)),
                  pl.BlockSpec((SUB, D), lambda i: (i, 0)),
                  pl.BlockSpec((SUB, D), lambda i: (i, 0)),
                  pl.BlockSpec((1, D), lambda i: (0, 0)),
                  pl.BlockSpec((None, 8, D), lambda i, l=layer: (l, 0, 5)),
                  pl.BlockSpec((None, SUB, D), lambda i, l=layer: (l, 1, 5))],
        out_specs=pl.BlockSpec((SUB, D), lambda i: (i, 0)),
        out_shape=jax.ShapeDtypeStruct((N, D), F32),
        scratch_shapes=[pltpu.VMEM((SUB, D), F32)],
        compiler_params=_cparams(1), name="combine",
    )(gathered.reshape(N * per_tok, LANES), wts, shared, x, g_post.reshape(1, D), mod, mod)


def _reorder_w_in(w):
    return jnp.concatenate([w[:, :1536], w[:, 1552:]], axis=1), jnp.pad(w[:, 1536:1552], ((0, 0), (0, LANES - 16)))


def _rope_tables(pos):
    half = DK // 2
    inv = ROPE_BASE ** (-jnp.arange(half, dtype=F32) / half)
    ang = pos.astype(F32)[:, None] * inv[None, :]
    cos = jnp.cos(ang)
    sin = jnp.sin(ang)
    return jnp.concatenate([cos, cos], axis=1), jnp.concatenate([-sin, sin], axis=1)


def _layer(l, x, mod, s_gla, s_pool, s_ret, wts):
    (norm_mix_pre, norm_mix_post, norm_ffn_pre, norm_ffn_post, w_in, w_gla_gate, b_gla_gate, gla_norm,
     pool_w, pool_scale, ret_norm, sgu_norm, sgu_w, sgu_b, w_branch, w_merge_gate, b_merge_gate, w_out,
     router_w, router_bias, expert_w1, expert_w3, expert_w2, shared_w1, shared_w3, shared_w2) = wts

    h = _prenorm(x, norm_mix_pre[l], mod, l, 0, 1, False)[0]
    w_main, w_low = _reorder_w_in(w_in[l])
    p_main = _matmul(h, w_main.astype(BF16), 1664, 512, name="inproj")
    p_low = _matmul(h, w_low.astype(BF16), 1664, LANES, name="inproj_low")

    w_gate_pad = jnp.pad(w_gla_gate[l], ((0, LANES - 16), (0, 0)))
    b_gate = b_gla_gate[l].reshape(1, HEADS * DK)
    log_gamma = jnp.log1p(-jnp.exp2(-5.0 - jnp.arange(HEADS, dtype=F32)))
    dec_row = jnp.repeat(log_gamma, DK).reshape(1, HEADS * DK)
    cos_p, sin_p = _rope_tables(jnp.arange(T_P))
    cos_p = jnp.tile(cos_p, (1, 2))
    sin_p = jnp.tile(sin_p, (1, 2))
    g_gla = gla_norm[l].reshape(1, HEADS * DV)
    g_ret = ret_norm[l].reshape(1, HEADS * DV)

    oa_p, gla_p = _la_prompt(p_main, C_GQ, C_GK, C_GV, C_GR, p_low, p_low, w_gate_pad, b_gate, g_gla, False)
    oc_p, ret_p = _la_prompt(p_main, C_RQ, C_RK, C_RV, C_RG, cos_p, sin_p, dec_row, b_gate, g_ret, True)
    pw_bf = pool_w[l].astype(BF16)
    pscale = pool_scale[l].reshape(1, 512)
    ob_p = _pool_prompt(p_main, pw_bf, pscale)
    sgu_g = sgu_norm[l].reshape(1, 512)
    od_p = _sgu_prompt(p_main, sgu_g, sgu_w[l], jnp.pad(sgu_b[l].T, ((0, 0), (0, LANES - 4))))

    ps = p_main[N_P:]
    q_t = _to_tiles_t(ps[:, C_GQ:C_GQ + 256])
    k_t = _to_tiles_t(ps[:, C_GK:C_GK + 256])
    glow_t = jnp.pad(_to_tiles_t(p_low[N_P:, :16]), ((0, 0), (0, LANES - 16), (0, 0)))
    w_gate_t = jnp.pad(w_gla_gate[l].T, ((0, 0), (0, LANES - 16)))
    b_col = jnp.broadcast_to(b_gla_gate[l][:, None], (HEADS * DK, LANES))
    logit_t = _gate_logits_t(w_gate_t, glow_t, b_col)
    dummy = jnp.zeros((HEADS * DK, LANES), F32)
    oa_s, gla_s = _la_sample(q_t, k_t, logit_t, dummy, dummy, p_main, C_GV, C_GR, g_gla, s_gla[l], False)
    cos_s, sin_s = _rope_tables(jnp.full((1,), PAST_LEN))
    cos_c = jnp.broadcast_to(jnp.tile(cos_s[0], HEADS)[:, None], (HEADS * DK, LANES))
    sin_c = jnp.broadcast_to(jnp.tile(sin_s[0], HEADS)[:, None], (HEADS * DK, LANES))
    dec_c = jnp.broadcast_to(jnp.repeat(log_gamma, DK)[:, None], (HEADS * DK, LANES))
    rq_t = _to_tiles_t(ps[:, C_RQ:C_RQ + 256])
    rk_t = _to_tiles_t(ps[:, C_RK:C_RK + 256])
    oc_s, ret_s = _la_sample(rq_t, rk_t, dec_c, cos_c, sin_c, p_main, C_RV, C_RG, g_ret, s_ret[l], True)
    sgu_w0 = jnp.repeat(sgu_w[l][:, 0, 0], LANES).reshape(1, 512)
    sgu_b0 = jnp.repeat(sgu_b[l][:, 0], LANES).reshape(1, 512)
    ob_s, od_s, vn_s = _small_sample(p_main, s_pool[l], pw_bf, pscale, sgu_g, sgu_w0, sgu_b0)
    pool_p = p_main[:N_P, C_PIN:C_PIN + 512].reshape(B_P, T_P, 512)[:, T_P - POOL_BUF:]
    pool_s = jnp.concatenate([s_pool[l][:, 1:], ps[:, None, C_PIN:C_PIN + 512]], axis=1)

    branches = [jnp.concatenate([p, s.astype(BF16)], axis=0)
                for p, s in ((oa_p, oa_s), (ob_p, ob_s), (oc_p, oc_s), (od_p, od_s))]
    merged = _merge(h, branches, w_merge_gate[l].astype(BF16), b_merge_gate[l].reshape(1, 4 * D),
                    w_branch[l].astype(BF16))
    x = _outproj(merged, w_out[l].astype(BF16), x, norm_mix_post[l], mod, l)

    h2, h2_f32, h2_packed = _prenorm(x, norm_ffn_pre[l], mod, l, 3, 4, True)
    rw = jnp.pad(router_w[l], ((0, 0), (0, LANES - N_EXPERTS)))
    rb = jnp.broadcast_to(router_bias[l][:, None], (N_EXPERTS, SUB))
    eid, pos, wt, counts = _router(h2_f32, rw, rb)
    counts = counts[:, 0].astype(jnp.int32)
    padded = (counts + EXP_BLOCK - 1) // EXP_BLOCK * EXP_BLOCK
    pad_end = jnp.cumsum(padded)
    pad_start = pad_end - padded
    nused = (pad_end[-1] // EXP_BLOCK).astype(jnp.int32).reshape(1)
    blk_row = jnp.arange(N_BLOCKS, dtype=jnp.int32) * EXP_BLOCK
    block_e = jnp.minimum(jnp.sum((blk_row[:, None] >= pad_end[None, :]).astype(jnp.int32), axis=1),
                          N_EXPERTS - 1)
    first = jnp.concatenate([jnp.ones((1,), jnp.int32), (block_e[1:] != block_e[:-1]).astype(jnp.int32)])
    first = jnp.where(blk_row < pad_end[-1], first, 0)
    par = (jnp.cumsum(first) - 1) % 2
    live = jnp.where(padded > 0, jnp.arange(N_EXPERTS), N_EXPERTS)
    after = jnp.concatenate([lax.cummin(live, reverse=True)[1:], jnp.full((1,), N_EXPERTS)])
    next_e = jnp.where(after < N_EXPERTS, after, -1)[block_e].astype(jnp.int32)
    start_of = jnp.sum(jnp.where(eid[:, :, None] == jnp.arange(N_EXPERTS), pad_start.astype(jnp.int32), 0), axis=-1)
    slots = (start_of + pos).T.reshape(N_ASSIGN)
    slot_src = (N + jnp.arange(L_SLOTS, dtype=jnp.int32) % ZERO_ROWS).at[slots].set(
        jnp.arange(N_ASSIGN, dtype=jnp.int32) // TOP_K)
    wt = jnp.pad(wt.T, ((0, 0), (0, LANES - TOP_K)))
    xs = _sc_gather(h2_packed, slot_src)
    w13 = jnp.concatenate([shared_w1[l], shared_w3[l]], axis=1).astype(BF16)
    shared = _shared(h2, w13, shared_w2[l].astype(BF16))
    ys = _experts(block_e, first, par.astype(jnp.int32), next_e, nused, xs, expert_w1, expert_w3, expert_w2, l)
    x = _combine(_sc_gather(ys, slots), wt, shared, x, norm_ffn_post[l], mod, l)
    return x, (gla_p, gla_s, pool_p, pool_s, ret_p, ret_s, vn_s)


def kernel(x_prompt, x_sample, c_prompt, c_sample, state_gla, state_pool, state_ret, w_ada, b_ada, norm_mix_pre, norm_mix_post, norm_ffn_pre, norm_ffn_post, w_in, w_gla_gate, b_gla_gate, gla_norm, pool_w, pool_scale, ret_norm, sgu_norm, sgu_w, sgu_b, w_branch, w_merge_gate, b_merge_gate, w_out, router_w, router_bias, expert_w1, expert_w3, expert_w2, shared_w1, shared_w3, shared_w2):
    wts = (norm_mix_pre, norm_mix_post, norm_ffn_pre, norm_ffn_post, w_in, w_gla_gate, b_gla_gate, gla_norm,
           pool_w, pool_scale, ret_norm, sgu_norm, sgu_w, sgu_b, w_branch, w_merge_gate, b_merge_gate, w_out,
           router_w, router_bias, expert_w1, expert_w3, expert_w2, shared_w1, shared_w3, shared_w2)
    c_all = jnp.zeros((MOD_ROWS, D), F32).at[:B_P].set(c_prompt).at[SUB:SUB + N_S].set(c_sample)
    mod = _ada(c_all, w_ada, b_ada)
    x = jnp.concatenate([x_prompt.reshape(N_P, D), x_sample.reshape(N_S, D)], axis=0)
    per_layer = []
    for l in range(DEPTH):
        x, states = _layer(l, x, mod, state_gla, state_pool, state_ret, wts)
        per_layer.append(states)
    gla_p, gla_s, pool_p, pool_s, ret_p, ret_s, vn_s = (jnp.stack(z) for z in zip(*per_layer))
    return (x[:N_P].reshape(B_P, T_P, D), x[N_P:].reshape(N_S, 1, D),
            gla_p, gla_s, pool_p, pool_s, ret_p, ret_s, vn_s.reshape(DEPTH, N_S, 1, 512))
```

```python
import functools

import jax
import jax.numpy as jnp
from jax import lax
from jax.experimental import pallas as pl
from jax.experimental.pallas import tpu as pltpu
from jax.experimental.pallas import tpu_sc as plsc

F32 = jnp.float32
BF16 = jnp.bfloat16
HIGHEST = lax.Precision.HIGHEST

D = 2048
B_P, T_P = 4, 2048
N_P = B_P * T_P
N_S = 128
N = N_P + N_S
DEPTH = 2
PAST_LEN = 16384
EPS = 1e-6
HEADS, DK, DV = 4, 64, 128
CHUNK = 64
GATE_TEMP = 16.0
POOL_WINDOWS = (2, 4, 8, 16)
POOL_BUF = 15
ROPE_BASE = 10000.0
N_EXPERTS = 64
TOP_K = 8
D_EXPERT = 512
ROUTED_SCALE = 2.5

LANES = 128
SUB = 128
MOD_ROWS = 256
EXP_BLOCK = 256
N_ASSIGN = N * TOP_K
N_BLOCKS = -(-(N_ASSIGN + N_EXPERTS * (EXP_BLOCK - 1)) // EXP_BLOCK)
L_SLOTS = N_BLOCKS * EXP_BLOCK
VMEM_LIMIT = 56 * 1024 * 1024

C_GQ, C_GK, C_GV, C_GR, C_PIN, C_RQ, C_RK, C_RV, C_RG, C_SU, C_SV = (
    0, 256, 512, 1024, 1536, 2048, 2304, 2560, 3072, 3584, 4096)
P_MAIN = 4608


def _cparams(n_axes=1):
    return pltpu.CompilerParams(dimension_semantics=("arbitrary",) * n_axes,
                                vmem_limit_bytes=VMEM_LIMIT)


def _silu(x):
    return x * jax.nn.sigmoid(x)


def _mod_rows(t, mp_ref, ms_ref):
    b = jnp.minimum(t // (T_P // SUB), B_P - 1)
    return jnp.where(t >= N_P // SUB, ms_ref[...], mp_ref[pl.ds(b, 1), :])


def _mod_specs(layer, part):
    return [pl.BlockSpec((None, 8, D), lambda i, l=layer, p=part: (l, 0, p)),
            pl.BlockSpec((None, SUB, D), lambda i, l=layer, p=part: (l, 1, p))]


def _pack_bf16_pair(lo, hi):
    lo_u = lax.bitcast_convert_type(lo.astype(BF16).astype(F32), jnp.uint32)
    hi_u = lax.bitcast_convert_type(hi.astype(BF16).astype(F32), jnp.uint32)
    return lax.bitcast_convert_type((hi_u & jnp.uint32(0xFFFF0000)) | (lo_u >> 16), jnp.int32)


def _unpack_bf16_pair(w):
    u = lax.bitcast_convert_type(w, jnp.uint32)
    lo = lax.bitcast_convert_type(u << 16, F32)
    hi = lax.bitcast_convert_type(u & jnp.uint32(0xFFFF0000), F32)
    return lo, hi


ROW_TILE = (8, LANES)


def _load_row_tiles(ref):
    return jnp.concatenate([ref[:, c, :] for c in range(ROW_TILE[0])], axis=1)


def _store_row_tiles(ref, val):
    for c in range(ROW_TILE[0]):
        ref[:, c, :] = val[:, c * LANES:(c + 1) * LANES]


def _load_row_tiles_2d(ref, rows):
    return jnp.concatenate([ref[pl.ds(c, rows, stride=ROW_TILE[0]), :] for c in range(ROW_TILE[0])], axis=1)


def _store_row_tiles_2d(ref, val, rows):
    for c in range(ROW_TILE[0]):
        ref[pl.ds(c, rows, stride=ROW_TILE[0]), :] = val[:, c * LANES:(c + 1) * LANES]


def _ada_body(c_ref, w_ref, b_ref, o_ref):
    s = _silu(c_ref[...]).astype(BF16)
    o_ref[...] = jnp.dot(s, w_ref[...].astype(BF16), preferred_element_type=F32) + b_ref[...]


def _ada(c_all, w_ada, b_ada):
    tn = 1024
    return pl.pallas_call(
        _ada_body,
        grid=(DEPTH, 6 * D // tn),
        in_specs=[pl.BlockSpec((MOD_ROWS, D), lambda l, j: (0, 0)),
                  pl.BlockSpec((None, D, tn), lambda l, j: (l, 0, j)),
                  pl.BlockSpec((None, 1, tn), lambda l, j: (l, 0, j))],
        out_specs=pl.BlockSpec((None, MOD_ROWS, tn), lambda l, j: (l, 0, j)),
        out_shape=jax.ShapeDtypeStruct((DEPTH, MOD_ROWS, 6 * D), F32),
        compiler_params=_cparams(2), name="ada")(c_all, w_ada, b_ada.reshape(DEPTH, 1, 6 * D))


ZERO_ROWS = 2 * SUB


def _prenorm_body(x_ref, g_ref, shp_ref, shs_ref, scp_ref, scs_ref, h_ref, *extra):
    t = jnp.minimum(pl.program_id(0), N // SUB - 1)
    x = x_ref[...]
    y = x * lax.rsqrt(jnp.mean(x * x, axis=-1, keepdims=True) + EPS) * g_ref[...]
    h = y * (1.0 + _mod_rows(t, scp_ref, scs_ref)) + _mod_rows(t, shp_ref, shs_ref)
    h_ref[...] = h.astype(BF16)
    if extra:
        hf_ref, hp_ref = extra
        hf_ref[...] = h
        _store_row_tiles(hp_ref, _pack_bf16_pair(h[:, :D // 2], h[:, D // 2:]))

        @pl.when(pl.program_id(0) >= N // SUB)
        def _():
            hp_ref[...] = jnp.zeros_like(hp_ref)


def _prenorm(x, g, mod, layer, shift_part, scale_part, with_extra):
    last = N // SUB - 1
    n_steps = N // SUB + (ZERO_ROWS // SUB if with_extra else 0)
    row = lambda i: (jnp.minimum(i, last), 0)
    out_shape = [jax.ShapeDtypeStruct((N, D), BF16)]
    out_specs = [pl.BlockSpec((SUB, D), row)]
    if with_extra:
        out_shape += [jax.ShapeDtypeStruct((N, D), F32), jax.ShapeDtypeStruct((N + ZERO_ROWS,) + ROW_TILE, jnp.int32)]
        out_specs += [pl.BlockSpec((SUB, D), row), pl.BlockSpec((SUB,) + ROW_TILE, lambda i: (i, 0, 0))]
    return pl.pallas_call(
        _prenorm_body,
        grid=(n_steps,),
        in_specs=[pl.BlockSpec((SUB, D), row),
                  pl.BlockSpec((1, D), lambda i: (0, 0))]
        + _mod_specs(layer, shift_part) + _mod_specs(layer, scale_part),
        out_specs=out_specs, out_shape=out_shape,
        compiler_params=_cparams(1), name="prenorm")(x, g.reshape(1, D), mod, mod, mod, mod)


def _mm_body(x_ref, w_ref, o_ref):
    o_ref[...] = jnp.dot(x_ref[...], w_ref[...], preferred_element_type=F32).astype(o_ref.dtype)


def _matmul(x, w, tm, tn, out_dtype=F32, name="mm"):
    m, k = x.shape
    n = w.shape[1]
    return pl.pallas_call(
        _mm_body,
        grid=(m // tm, n // tn),
        in_specs=[pl.BlockSpec((tm, k), lambda i, j: (i, 0)),
                  pl.BlockSpec((k, tn), lambda i, j: (0, j))],
        out_specs=pl.BlockSpec((tm, tn), lambda i, j: (i, j)),
        out_shape=jax.ShapeDtypeStruct((m, n), out_dtype),
        compiler_params=_cparams(2), name=name)(x, w)


ROWS_LA = 256


def _swap_halves_lanes(x):
    lane = lax.broadcasted_iota(jnp.int32, x.shape, 1)
    return jnp.where((lane % 64) < 32, pltpu.roll(x, 96, 1), pltpu.roll(x, 32, 1))


def _rope_lanes(x, cos, sin_signed):
    parts = []
    for half in range(2):
        xh = x[:, half * LANES:(half + 1) * LANES]
        parts.append(xh * cos + _swap_halves_lanes(xh) * sin_signed)
    return jnp.concatenate(parts, axis=1)


def _la_prompt_body(q_ref, k_ref, v_ref, r_ref, aux_ref, aux2_ref, dec_ref, bias_ref, g_ref,
                    o_ref, st_out_ref, st_ref, *, retention):
    t = pl.program_id(1)

    @pl.when(t == 0)
    def _():
        st_ref[...] = jnp.zeros_like(st_ref)

    ri = lax.broadcasted_iota(jnp.int32, (CHUNK, CHUNK), 0)
    ci = lax.broadcasted_iota(jnp.int32, (CHUNK, CHUNK), 1)
    causal = ri >= ci
    tril = causal.astype(F32)
    scale = DK ** -0.5

    for c in range(ROWS_LA // CHUNK):
        rows = pl.ds(c * CHUNK, CHUNK)
        q = q_ref[rows, :]
        k = k_ref[rows, :]
        v = v_ref[rows, :]
        if retention:
            cos = aux_ref[rows, :]
            sin = aux2_ref[rows, :]
            q = _rope_lanes(q, cos, sin)
            k = _rope_lanes(k, cos, sin) * scale
            la = jnp.broadcast_to(dec_ref[...], (CHUNK, HEADS * DK))
        else:
            q = q * scale
            logit = jnp.dot(aux_ref[rows, :], dec_ref[...], precision=HIGHEST,
                            preferred_element_type=F32) + bias_ref[...]
            la = jax.nn.log_sigmoid(logit) / GATE_TEMP
        bc = jnp.dot(tril, la, precision=HIGHEST, preferred_element_type=F32)
        bl = bc[CHUNK - 1:CHUNK, :]
        qd = q * jnp.exp(bc)
        ki = k * jnp.exp(-bc)
        ke = k * jnp.exp(bl - bc)
        ac = jnp.exp(bl)
        outs = []
        for h in range(HEADS):
            ks = slice(h * DK, (h + 1) * DK)
            vs = slice(h * DV, (h + 1) * DV)
            qd_h = qd[:, ks].astype(BF16)
            ki_h = ki[:, ks].astype(BF16)
            ke_h = ke[:, ks].astype(BF16)
            v_h = v[:, vs].astype(BF16)
            sc = lax.dot_general(qd_h, ki_h, (((1,), (1,)), ((), ())), preferred_element_type=F32)
            sc = jnp.where(causal, sc, 0.0)
            o_h = jnp.dot(sc.astype(BF16), v_h, preferred_element_type=F32)
            st = st_ref[h]
            o_h = o_h + lax.dot_general(qd_h, st.astype(BF16), (((1,), (1,)), ((), ())),
                                        preferred_element_type=F32)
            kv_t = lax.dot_general(v_h, ke_h, (((0,), (0,)), ((), ())), preferred_element_type=F32)
            st_ref[h] = st * ac[:, ks] + kv_t
            o_n = o_h * lax.rsqrt(jnp.mean(o_h * o_h, axis=-1, keepdims=True) + EPS) * g_ref[:, vs]
            outs.append(o_n)
        o = jnp.concatenate(outs, axis=1) * _silu(r_ref[rows, :])
        o_ref[rows, :] = o.astype(BF16)

    st_out_ref[...] = st_ref[...]


def _la_prompt(p_main, cq, ck, cv, cr, aux, aux2, dec, bias, g, retention):
    nt = T_P // ROWS_LA
    rowblk = lambda b, t: b * nt + t
    if retention:
        aux_specs = [pl.BlockSpec((ROWS_LA, LANES), lambda b, t: (t, 0)),
                     pl.BlockSpec((ROWS_LA, LANES), lambda b, t: (t, 0))]
    else:
        aux_specs = [pl.BlockSpec((ROWS_LA, LANES), lambda b, t: (rowblk(b, t), 0)),
                     pl.BlockSpec((8, LANES), lambda b, t: (0, 0))]
    o, st = pl.pallas_call(
        functools.partial(_la_prompt_body, retention=retention),
        grid=(B_P, nt),
        in_specs=[pl.BlockSpec((ROWS_LA, 256), lambda b, t: (rowblk(b, t), cq // 256)),
                  pl.BlockSpec((ROWS_LA, 256), lambda b, t: (rowblk(b, t), ck // 256)),
                  pl.BlockSpec((ROWS_LA, 512), lambda b, t: (rowblk(b, t), cv // 512)),
                  pl.BlockSpec((ROWS_LA, 512), lambda b, t: (rowblk(b, t), cr // 512))]
        + aux_specs
        + [pl.BlockSpec(dec.shape, lambda b, t: (0, 0)),
           pl.BlockSpec((1, HEADS * DK), lambda b, t: (0, 0)),
           pl.BlockSpec((1, HEADS * DV), lambda b, t: (0, 0))],
        out_specs=[pl.BlockSpec((ROWS_LA, HEADS * DV), lambda b, t: (rowblk(b, t), 0)),
                   pl.BlockSpec((None, HEADS, DV, DK), lambda b, t: (b, 0, 0, 0))],
        out_shape=[jax.ShapeDtypeStruct((N_P, HEADS * DV), BF16),
                   jax.ShapeDtypeStruct((B_P, HEADS, DV, DK), F32)],
        scratch_shapes=[pltpu.VMEM((HEADS, DV, DK), F32)],
        compiler_params=_cparams(2), name="ret_prompt" if retention else "gla_prompt",
    )(p_main, p_main, p_main, p_main, aux, aux2, dec, bias, g)
    return o, jnp.swapaxes(st, -1, -2)


SAMPLE_TILE = 8


def _la_sample_body(qt_ref, kt_ref, lt_ref, cos_ref, sin_ref, v_ref, r_ref, g_ref, s_ref,
                    o_ref, s_out_ref, *, retention):
    scale = DK ** -0.5
    qt = qt_ref[...]
    kt = kt_ref[...]
    if retention:
        def rope(x):
            sw = jnp.concatenate(
                [x[h * DK + (DK // 2) * (1 - j): h * DK + (DK // 2) * (2 - j), :]
                 for h in range(HEADS) for j in range(2)], axis=0)
            return x * cos_ref[...] + sw * sin_ref[...]
        qt = rope(qt)
        kt = rope(kt) * scale
        la = lt_ref[...]
    else:
        qt = qt * scale
        la = jax.nn.log_sigmoid(lt_ref[...]) / GATE_TEMP
    at = jnp.exp(la)
    qd = qt * at
    ki = kt * jnp.exp(-la)
    prod = qd * ki
    v8 = v_ref[...]
    r8 = r_ref[...]
    g = g_ref[...]
    for j in range(SAMPLE_TILE):
        for h in range(HEADS):
            ks = slice(h * DK, (h + 1) * DK)
            vs = slice(h * DV, (h + 1) * DV)
            a_c = jnp.broadcast_to(at[ks, j:j + 1], (DK, DV))
            k_c = jnp.broadcast_to(kt[ks, j:j + 1], (DK, DV))
            q_c = jnp.broadcast_to(qd[ks, j:j + 1], (DK, DV))
            s_c = jnp.broadcast_to(jnp.sum(prod[ks, j:j + 1], axis=0, keepdims=True), (1, DV))
            s0 = s_ref[j, h]
            v_row = v8[j:j + 1, vs]
            s_out_ref[j, h] = a_c * s0 + k_c * v_row
            o_row = s_c * v_row + jnp.sum(q_c * s0, axis=0, keepdims=True)
            o_n = o_row * lax.rsqrt(jnp.mean(o_row * o_row, axis=-1, keepdims=True) + EPS) * g[:, vs]
            o_ref[j:j + 1, vs] = o_n * _silu(r8[j:j + 1, vs])


def _la_sample(qt, kt, lt, cos_t, sin_t, p_main, cv, cr, g, s0, retention):
    nt = N_S // SAMPLE_TILE
    row0 = N_P // SAMPLE_TILE
    tile = pl.BlockSpec((None, HEADS * DK, LANES), lambda i: (i, 0, 0))
    full = pl.BlockSpec((HEADS * DK, LANES), lambda i: (0, 0))
    lt_spec = full if retention else tile
    return pl.pallas_call(
        functools.partial(_la_sample_body, retention=retention),
        grid=(nt,),
        in_specs=[tile, tile, lt_spec, full, full,
                  pl.BlockSpec((SAMPLE_TILE, 512), lambda i: (row0 + i, cv // 512)),
                  pl.BlockSpec((SAMPLE_TILE, 512), lambda i: (row0 + i, cr // 512)),
                  pl.BlockSpec((1, HEADS * DV), lambda i: (0, 0)),
                  pl.BlockSpec((SAMPLE_TILE, HEADS, DK, DV), lambda i: (i, 0, 0, 0))],
        out_specs=[pl.BlockSpec((SAMPLE_TILE, HEADS * DV), lambda i: (i, 0)),
                   pl.BlockSpec((SAMPLE_TILE, HEADS, DK, DV), lambda i: (i, 0, 0, 0))],
        out_shape=[jax.ShapeDtypeStruct((N_S, HEADS * DV), F32),
                   jax.ShapeDtypeStruct((N_S, HEADS, DK, DV), F32)],
        compiler_params=_cparams(1), name="ret_sample" if retention else "gla_sample",
    )(qt, kt, lt, cos_t, sin_t, p_main, p_main, g, s0)


def _gate_logits_t_body(w_ref, x_ref, b_ref, o_ref):
    o_ref[...] = jnp.dot(w_ref[...], x_ref[...], precision=HIGHEST, preferred_element_type=F32) + b_ref[...]


def _gate_logits_t(w_gate_t, glow_t, b_col):
    nt = N_S // SAMPLE_TILE
    return pl.pallas_call(
        _gate_logits_t_body,
        grid=(nt,),
        in_specs=[pl.BlockSpec((HEADS * DK, LANES), lambda i: (0, 0)),
                  pl.BlockSpec((None, LANES, LANES), lambda i: (i, 0, 0)),
                  pl.BlockSpec((HEADS * DK, LANES), lambda i: (0, 0))],
        out_specs=pl.BlockSpec((None, HEADS * DK, LANES), lambda i: (i, 0, 0)),
        out_shape=jax.ShapeDtypeStruct((nt, HEADS * DK, LANES), F32),
        compiler_params=_cparams(1), name="gate_logits_t")(w_gate_t, glow_t, b_col)


def _to_tiles_t(x):
    c = x.shape[1]
    xt = jnp.swapaxes(x.reshape(N_S // SAMPLE_TILE, SAMPLE_TILE, c), 1, 2)
    return jnp.pad(xt, ((0, 0), (0, 0), (0, LANES - SAMPLE_TILE)))


ROWS_POOL = 512


def _pool_mix(y, w_ref, sc_ref):
    outs = []
    for gi in range(4):
        cs = slice(gi * LANES, (gi + 1) * LANES)
        outs.append(jnp.dot(y[:, cs].astype(BF16), w_ref[gi], preferred_element_type=F32))
    return jnp.concatenate(outs, axis=1) * sc_ref[...]


def _pool_prompt_body(p_ref, halo_ref, w_ref, sc_ref, o_ref):
    t = pl.program_id(1)
    p = p_ref[...]
    halo = jnp.where(t == 0, 0.0, halo_ref[...])
    full = jnp.concatenate([halo, p], axis=0)
    pos = t * ROWS_POOL + lax.broadcasted_iota(jnp.int32, (ROWS_POOL, LANES), 0)
    means = []
    for gi, w in enumerate(POOL_WINDOWS):
        s = full[:, gi * LANES:(gi + 1) * LANES]
        step = 1
        while step < w:
            s = s + pltpu.roll(s, step, 0)
            step *= 2
        win = s[16:, :]
        cnt = jnp.minimum(w, pos + 1).astype(F32)
        means.append(win / cnt)
    y = jnp.concatenate(means, axis=1) - p
    o_ref[...] = _pool_mix(y, w_ref, sc_ref).astype(BF16)


def _pool_prompt(p_main, w_bf, scale):
    nt = T_P // ROWS_POOL
    return pl.pallas_call(
        _pool_prompt_body,
        grid=(B_P, nt),
        in_specs=[pl.BlockSpec((ROWS_POOL, 512), lambda b, t: (b * nt + t, C_PIN // 512)),
                  pl.BlockSpec((16, 512), lambda b, t: (jnp.maximum((b * nt + t) * (ROWS_POOL // 16) - 1, 0),
                                                        C_PIN // 512)),
                  pl.BlockSpec((4, LANES, LANES), lambda b, t: (0, 0, 0)),
                  pl.BlockSpec((1, 512), lambda b, t: (0, 0))],
        out_specs=pl.BlockSpec((ROWS_POOL, 512), lambda b, t: (b * nt + t, 0)),
        out_shape=jax.ShapeDtypeStruct((N_P, 512), BF16),
        compiler_params=_cparams(2), name="pool_prompt")(p_main, p_main, w_bf, scale)


def _small_sample_body(p_ref, buf_ref, pw_ref, psc_ref, u_ref, sv_ref, sg_ref, sw_ref, sb_ref,
                       ob_ref, od_ref, vn_ref):
    p = p_ref[...]
    means = []
    for gi, w in enumerate(POOL_WINDOWS):
        cs = slice(gi * LANES, (gi + 1) * LANES)
        s = p[:, cs]
        for j in range(1, w):
            s = s + buf_ref[:, POOL_BUF - j, cs]
        means.append(s / float(min(w, PAST_LEN + 1)))
    y = jnp.concatenate(means, axis=1) - p
    ob_ref[...] = _pool_mix(y, pw_ref, psc_ref)
    sv = sv_ref[...]
    vn = sv * lax.rsqrt(jnp.mean(sv * sv, axis=-1, keepdims=True) + EPS) * sg_ref[...]
    vn_ref[...] = vn
    od_ref[...] = u_ref[...] * (sw_ref[...] * vn + sb_ref[...])


def _small_sample(p_main, buf, pw_bf, pscale, sgu_g, sgu_w0, sgu_b0):
    row = N_P // N_S
    col = lambda c: pl.BlockSpec((N_S, 512), lambda i, c=c: (row, c // 512))
    vec = pl.BlockSpec((1, 512), lambda i: (0, 0))
    return pl.pallas_call(
        _small_sample_body,
        grid=(1,),
        in_specs=[col(C_PIN), pl.BlockSpec((N_S, POOL_BUF, 512), lambda i: (0, 0, 0)),
                  pl.BlockSpec((4, LANES, LANES), lambda i: (0, 0, 0)), vec,
                  col(C_SU), col(C_SV), vec, vec, vec],
        out_specs=[pl.BlockSpec((N_S, 512), lambda i: (0, 0))] * 3,
        out_shape=[jax.ShapeDtypeStruct((N_S, 512), F32)] * 3,
        compiler_params=_cparams(1), name="small_sample",
    )(p_main, buf, pw_bf, pscale, p_main, p_main, sgu_g, sgu_w0, sgu_b0)


ROWS_SGU = 512
SGU_CHUNK = 128


def _sgu_prompt_body(u_ref, v_ref, g_ref, w_ref, bt_ref, o_ref):
    ri = lax.broadcasted_iota(jnp.int32, (SGU_CHUNK, SGU_CHUNK), 0)
    ci = lax.broadcasted_iota(jnp.int32, (SGU_CHUNK, SGU_CHUNK), 1)
    causal = ri >= ci
    for c in range(ROWS_SGU // SGU_CHUNK):
        rows = pl.ds(c * SGU_CHUNK, SGU_CHUNK)
        v = v_ref[rows, :]
        vn = (v * lax.rsqrt(jnp.mean(v * v, axis=-1, keepdims=True) + EPS) * g_ref[...]).astype(BF16)
        outs = []
        for gi in range(4):
            cs = slice(gi * LANES, (gi + 1) * LANES)
            w = jnp.where(causal, w_ref[gi], 0.0).astype(BF16)
            mixed = jnp.dot(w, vn[:, cs], preferred_element_type=F32)
            outs.append(mixed + jnp.broadcast_to(bt_ref[:, gi:gi + 1], (SGU_CHUNK, LANES)))
        o_ref[rows, :] = (u_ref[rows, :] * jnp.concatenate(outs, axis=1)).astype(BF16)


def _sgu_prompt(p_main, g, w, b_t):
    return pl.pallas_call(
        _sgu_prompt_body,
        grid=(N_P // ROWS_SGU,),
        in_specs=[pl.BlockSpec((ROWS_SGU, 512), lambda i: (i, C_SU // 512)),
                  pl.BlockSpec((ROWS_SGU, 512), lambda i: (i, C_SV // 512)),
                  pl.BlockSpec((1, 512), lambda i: (0, 0)),
                  pl.BlockSpec((4, SGU_CHUNK, SGU_CHUNK), lambda i: (0, 0, 0)),
                  pl.BlockSpec((SGU_CHUNK, LANES), lambda i: (0, 0))],
        out_specs=pl.BlockSpec((ROWS_SGU, 512), lambda i: (i, 0)),
        out_shape=jax.ShapeDtypeStruct((N_P, 512), BF16),
        compiler_params=_cparams(1), name="sgu_prompt")(p_main, p_main, g, w, b_t)


TM_MERGE = 640
TN_MERGE = 512


def _merge_body(h_ref, ba_ref, bb_ref, bc_ref, bd_ref, g0, g1, g2, g3, u0, u1, u2, u3,
                c0, c1, c2, c3, o_ref):
    h = h_ref[...]
    acc = None
    for br, gw, uw, gb in ((ba_ref, g0, u0, c0), (bb_ref, g1, u1, c1), (bc_ref, g2, u2, c2), (bd_ref, g3, u3, c3)):
        gate = jax.nn.sigmoid(jnp.dot(h, gw[...], preferred_element_type=F32) + gb[...])
        up = jnp.dot(br[...], uw[...], preferred_element_type=F32)
        acc = gate * up if acc is None else acc + gate * up
    o_ref[...] = acc.astype(BF16)


def _merge(h, branches, w_mg, b_mg, w_br):
    nj = D // TN_MERGE
    row = lambda w: pl.BlockSpec((TM_MERGE, w), lambda i, j: (i, 0))
    gate_w = [pl.BlockSpec((D, TN_MERGE), lambda i, j, b=b: (0, b * nj + j)) for b in range(4)]
    up_w = [pl.BlockSpec((None, 512, TN_MERGE), lambda i, j, b=b: (b, 0, j)) for b in range(4)]
    gate_b = [pl.BlockSpec((1, TN_MERGE), lambda i, j, b=b: (0, b * nj + j)) for b in range(4)]
    return pl.pallas_call(
        _merge_body,
        grid=(N // TM_MERGE, nj),
        in_specs=[row(D)] + [row(512)] * 4 + gate_w + up_w + gate_b,
        out_specs=pl.BlockSpec((TM_MERGE, TN_MERGE), lambda i, j: (i, j)),
        out_shape=jax.ShapeDtypeStruct((N, D), BF16),
        compiler_params=_cparams(2), name="merge",
    )(h, *branches, w_mg, w_mg, w_mg, w_mg, w_br, w_br, w_br, w_br, b_mg, b_mg, b_mg, b_mg)


TM_OUT = 640
TN_OUT = 512


def _post_residual(t0, x_ref, get_y, gn_ref, gp_ref, gs_ref, o_ref, rows_total):
    for s in range(rows_total // SUB):
        rows = pl.ds(s * SUB, SUB)
        y = get_y(rows)
        yn = y * lax.rsqrt(jnp.mean(y * y, axis=-1, keepdims=True) + EPS) * gn_ref[...]
        o_ref[rows, :] = x_ref[rows, :] + _mod_rows(t0 + s, gp_ref, gs_ref) * yn


def _outproj_body(m_ref, w_ref, x_ref, gn_ref, gp_ref, gs_ref, o_ref, acc_ref):
    j = pl.program_id(1)
    acc_ref[j] = jnp.dot(m_ref[...], w_ref[...], preferred_element_type=F32)

    @pl.when(j == D // TN_OUT - 1)
    def _():
        get_y = lambda rows: jnp.concatenate([acc_ref[c, rows, :] for c in range(D // TN_OUT)], axis=1)
        _post_residual(pl.program_id(0) * (TM_OUT // SUB), x_ref, get_y, gn_ref, gp_ref, gs_ref,
                       o_ref, TM_OUT)


def _outproj(merged, w_out, x, g_post, mod, layer):
    mspec = [pl.BlockSpec((None, 8, D), lambda i, j, l=layer: (l, 0, 2)),
             pl.BlockSpec((None, SUB, D), lambda i, j, l=layer: (l, 1, 2))]
    return pl.pallas_call(
        _outproj_body,
        grid=(N // TM_OUT, D // TN_OUT),
        in_specs=[pl.BlockSpec((TM_OUT, D), lambda i, j: (i, 0)),
                  pl.BlockSpec((D, TN_OUT), lambda i, j: (0, j)),
                  pl.BlockSpec((TM_OUT, D), lambda i, j: (i, 0)),
                  pl.BlockSpec((1, D), lambda i, j: (0, 0))] + mspec,
        out_specs=pl.BlockSpec((TM_OUT, D), lambda i, j: (i, 0)),
        out_shape=jax.ShapeDtypeStruct((N, D), F32),
        scratch_shapes=[pltpu.VMEM((D // TN_OUT, TM_OUT, TN_OUT), F32)],
        compiler_params=_cparams(2), name="outproj",
    )(merged, w_out, x, g_post.reshape(1, D), mod, mod)


def _router_body(h_ref, w_ref, b_ref, eid_ref, pos_ref, wt_ref, cnt_ref, run_ref):
    i = pl.program_id(0)

    @pl.when(i == 0)
    def _():
        run_ref[...] = jnp.zeros_like(run_ref)

    ng, gs = 8, N_EXPERTS // 8
    neg = -jnp.inf
    logits = jnp.dot(h_ref[...], w_ref[...], precision=HIGHEST, preferred_element_type=F32)
    scores = jax.nn.sigmoid(logits.T[:N_EXPERTS, :])
    sel = scores + b_ref[...]
    sel3 = sel.reshape(ng, gs, SUB)
    sub3 = lax.broadcasted_iota(jnp.int32, (ng, gs, SUB), 1)
    gmax = jnp.max(sel3, axis=1, keepdims=True)
    first = jnp.min(jnp.where(sel3 == gmax, sub3, gs), axis=1, keepdims=True)
    gmax2 = jnp.max(jnp.where(sub3 == first, neg, sel3), axis=1, keepdims=True)
    gscore = (gmax + gmax2).reshape(ng, SUB)
    gidx = lax.broadcasted_iota(jnp.int32, (ng, SUB), 0)
    grank = jnp.zeros((ng, SUB), jnp.int32)
    for s in range(1, ng):
        other = pltpu.roll(gscore, s, 0)
        lower = gidx >= s
        grank += ((other > gscore) | ((other == gscore) & lower)).astype(jnp.int32)
    keep = jnp.broadcast_to((grank < 4).reshape(ng, 1, SUB), (ng, gs, SUB))
    masked = jnp.where(keep, sel3, neg).reshape(N_EXPERTS, SUB)
    eidx = lax.broadcasted_iota(jnp.int32, (N_EXPERTS, SUB), 0)
    rank = jnp.zeros((N_EXPERTS, SUB), jnp.int32)
    for s in range(1, N_EXPERTS):
        other = pltpu.roll(masked, s, 0)
        lower = eidx >= s
        rank += ((other > masked) | ((other == masked) & lower)).astype(jnp.int32)
    chosen = rank < TOP_K
    w_sel = jnp.where(chosen, scores, 0.0)
    w_sel = w_sel / jnp.sum(w_sel, axis=0, keepdims=True) * ROUTED_SCALE
    ri = lax.broadcasted_iota(jnp.int32, (SUB, SUB), 0)
    ci = lax.broadcasted_iota(jnp.int32, (SUB, SUB), 1)
    onehot = chosen.astype(BF16)
    pos = jnp.dot(onehot, (ri < ci).astype(BF16), preferred_element_type=F32) + run_ref[...]
    run_ref[...] = run_ref[...] + jnp.sum(chosen.astype(F32), axis=1, keepdims=True)
    cnt_ref[...] = run_ref[...]
    eidx_f = eidx.astype(F32)
    rows_e, rows_p, rows_w = [], [], []
    for kk in range(TOP_K):
        m = chosen & (rank == kk)
        rows_e.append(jnp.sum(jnp.where(m, eidx_f, 0.0), axis=0, keepdims=True))
        rows_p.append(jnp.sum(jnp.where(m, pos, 0.0), axis=0, keepdims=True))
        rows_w.append(jnp.sum(jnp.where(m, w_sel, 0.0), axis=0, keepdims=True))
    eid_ref[...] = jnp.concatenate(rows_e, axis=0).astype(jnp.int32)
    pos_ref[...] = jnp.concatenate(rows_p, axis=0).astype(jnp.int32)
    wt_ref[...] = jnp.concatenate(rows_w, axis=0)


def _router(h_f32, rw_pad, rb_col):
    tile = pl.BlockSpec((TOP_K, SUB), lambda i: (0, i))
    return pl.pallas_call(
        _router_body,
        grid=(N // SUB,),
        in_specs=[pl.BlockSpec((SUB, D), lambda i: (i, 0)),
                  pl.BlockSpec((D, LANES), lambda i: (0, 0)),
                  pl.BlockSpec((N_EXPERTS, SUB), lambda i: (0, 0))],
        out_specs=[tile, tile, tile, pl.BlockSpec((N_EXPERTS, SUB), lambda i: (0, 0))],
        out_shape=[jax.ShapeDtypeStruct((TOP_K, N), jnp.int32), jax.ShapeDtypeStruct((TOP_K, N), jnp.int32),
                   jax.ShapeDtypeStruct((TOP_K, N), F32), jax.ShapeDtypeStruct((N_EXPERTS, SUB), F32)],
        scratch_shapes=[pltpu.VMEM((N_EXPERTS, SUB), F32)],
        compiler_params=_cparams(1), name="router")(h_f32, rw_pad, rb_col)


SC_CORES, SC_SUBCORES = 2, 16
SC_WORKERS = SC_CORES * SC_SUBCORES
SC_LANES = 16
SC_CHUNK = 16
SC_SCAN = N_ASSIGN // SC_WORKERS


def _sc_mesh():
    return plsc.VectorSubcoreMesh(core_axis_name="c", subcore_axis_name="s",
                                  num_cores=SC_CORES, num_subcores=SC_SUBCORES)


def _sc_worker_base(per_w):
    return (lax.axis_index("s") * SC_CORES + lax.axis_index("c")) * per_w


def _sc_gather_rows(table_hbm, out_hbm, idx_v, rows_v, gsem, wsem, base, per_w):
    n_ch = per_w // SC_CHUNK
    assert n_ch % 2 == 0

    def gather(j, p):
        off = pl.multiple_of(j * SC_CHUNK, SC_CHUNK)
        return pltpu.make_async_copy(table_hbm.at[idx_v.at[pl.ds(off, SC_CHUNK)]], rows_v.at[p], gsem.at[p])

    def write(j, p):
        off = pl.multiple_of(j * SC_CHUNK, SC_CHUNK)
        return pltpu.make_async_copy(rows_v.at[p], out_hbm.at[pl.ds(base + off, SC_CHUNK)], wsem.at[p])

    gather(0, 0).start()

    @pl.loop(0, n_ch, step=2)
    def _(j0):
        for p in range(2):
            j = j0 + p
            gather(j, p).wait()

            @pl.when(j >= 1)
            def _():
                write(j - 1, 1 - p).wait()

            @pl.when(j + 1 < n_ch)
            def _():
                gather(j + 1, 1 - p).start()
            write(j, p).start()

    write(n_ch - 1, 1).wait()


_SC_ROW_SCRATCH = [pltpu.VMEM((2, SC_CHUNK) + ROW_TILE, jnp.int32),
                   pltpu.SemaphoreType.DMA((2,)), pltpu.SemaphoreType.DMA((2,))]


def _sc_gather(table, idx):
    n_out = idx.shape[0]
    per_w = n_out // SC_WORKERS
    assert per_w * SC_WORKERS == n_out and per_w % (2 * SC_CHUNK) == 0

    def body(table_hbm, idx_hbm, out_hbm, idx_v, rows_v, gsem, wsem):
        base = _sc_worker_base(per_w)
        pltpu.sync_copy(idx_hbm.at[pl.ds(base, per_w)], idx_v)
        _sc_gather_rows(table_hbm, out_hbm, idx_v, rows_v, gsem, wsem, base, per_w)

    return pl.kernel(
        body, out_type=jax.ShapeDtypeStruct((n_out,) + ROW_TILE, jnp.int32), mesh=_sc_mesh(),
        scratch_types=[pltpu.VMEM((per_w,), jnp.int32)] + _SC_ROW_SCRATCH, name="sc_gather")(table, idx)


def _sc_dispatch(table, slots):
    per_w = L_SLOTS // SC_WORKERS
    assert per_w * SC_WORKERS == L_SLOTS and per_w % (2 * SC_CHUNK) == 0 and SC_SCAN % SC_LANES == 0
    assert ZERO_ROWS & (ZERO_ROWS - 1) == 0

    def body(table_hbm, slots_hbm, out_hbm, idx_v, sl_v, rows_v, gsem, wsem):
        base = _sc_worker_base(per_w)
        lane = lax.iota(jnp.int32, SC_LANES)

        @pl.loop(0, per_w // SC_LANES)
        def _(j):
            off = pl.multiple_of(j * SC_LANES, SC_LANES)
            idx_v[pl.ds(off, SC_LANES)] = N + ((base + off + lane) & (ZERO_ROWS - 1))

        @pl.loop(0, N_ASSIGN // SC_SCAN)
        def _(c):
            pltpu.sync_copy(slots_hbm.at[pl.ds(pl.multiple_of(c * SC_SCAN, 8), SC_SCAN)], sl_v)

            @pl.loop(0, SC_SCAN // SC_LANES)
            def _(j):
                off = pl.multiple_of(j * SC_LANES, SC_LANES)
                loc = sl_v[pl.ds(off, SC_LANES)] - base
                mine = (loc >= 0) & (loc < per_w)
                tok = lax.shift_right_logical(c * SC_SCAN + off + lane, 3)
                plsc.store_scatter(idx_v, [jnp.where(mine, loc, 0)], tok, mask=mine)

        _sc_gather_rows(table_hbm, out_hbm, idx_v, rows_v, gsem, wsem, base, per_w)

    return pl.kernel(
        body, out_type=jax.ShapeDtypeStruct((L_SLOTS,) + ROW_TILE, jnp.int32), mesh=_sc_mesh(),
        scratch_types=[pltpu.VMEM((per_w,), jnp.int32), pltpu.VMEM((SC_SCAN,), jnp.int32)] + _SC_ROW_SCRATCH,
        compiler_params=pltpu.CompilerParams(needs_layout_passes=False),
        name="sc_dispatch")(table, slots)


def _experts_body(be_ref, first_ref, par_ref, next_ref, nused_ref, x_ref, w1_hbm, w3_hbm, w2_hbm, y_ref,
                  w1f, w3f, w2f, w1b, w3b, w2b, sem, *, layer):
    b = pl.program_id(0)
    used = b < nused_ref[0]

    def copies(e, slot):
        return (pltpu.make_async_copy(w1_hbm.at[layer, e], w1f.at[slot], sem.at[0, slot]),
                pltpu.make_async_copy(w3_hbm.at[layer, e], w3f.at[slot], sem.at[1, slot]),
                pltpu.make_async_copy(w2_hbm.at[layer, e], w2f.at[slot], sem.at[2, slot]))

    @pl.when(b == 0)
    def _():
        for c in copies(be_ref[0], 0):
            c.start()

    @pl.when(used & (first_ref[b] == 1))
    def _():
        slot = par_ref[b]
        for c in copies(be_ref[b], slot):
            c.wait()

        @pl.when(next_ref[b] >= 0)
        def _():
            for c in copies(next_ref[b], 1 - slot):
                c.start(priority=1)
        w1b[...] = w1f[slot].astype(BF16)
        w3b[...] = w3f[slot].astype(BF16)
        w2b[...] = w2f[slot].astype(BF16)

    @pl.when(used)
    def _():
        lo, hi = _unpack_bf16_pair(_load_row_tiles_2d(x_ref, EXP_BLOCK))
        lo = lo.astype(BF16)
        hi = hi.astype(BF16)
        half = D // 2
        h1 = (jnp.dot(lo, w1b[:half, :], preferred_element_type=F32)
              + jnp.dot(hi, w1b[half:, :], preferred_element_type=F32))
        h3 = (jnp.dot(lo, w3b[:half, :], preferred_element_type=F32)
              + jnp.dot(hi, w3b[half:, :], preferred_element_type=F32))
        hid = (_silu(h1) * h3).astype(BF16)
        y = jnp.dot(hid, w2b[...], preferred_element_type=F32)
        _store_row_tiles_2d(y_ref, _pack_bf16_pair(y[:, :half], y[:, half:]), EXP_BLOCK)

    @pl.when(jnp.logical_not(used))
    def _():
        y_ref[...] = jnp.zeros_like(y_ref)


def _experts(block_e, first, par, next_e, nused, xs, w1, w3, w2, layer):
    blk = lambda b, *refs: (jnp.minimum(b, refs[-1][0] - 1), 0)
    grid_spec = pltpu.PrefetchScalarGridSpec(
        num_scalar_prefetch=5,
        grid=(N_BLOCKS,),
        in_specs=[pl.BlockSpec((EXP_BLOCK * ROW_TILE[0], LANES), blk),
                  pl.BlockSpec(memory_space=pl.ANY), pl.BlockSpec(memory_space=pl.ANY),
                  pl.BlockSpec(memory_space=pl.ANY)],
        out_specs=pl.BlockSpec((EXP_BLOCK * ROW_TILE[0], LANES), lambda b, *refs: (b, 0)),
        scratch_shapes=[pltpu.VMEM((2, D, D_EXPERT), F32), pltpu.VMEM((2, D, D_EXPERT), F32),
                        pltpu.VMEM((2, D_EXPERT, D), F32),
                        pltpu.VMEM((D, D_EXPERT), BF16), pltpu.VMEM((D, D_EXPERT), BF16),
                        pltpu.VMEM((D_EXPERT, D), BF16), pltpu.SemaphoreType.DMA((3, 2))])
    return pl.pallas_call(
        functools.partial(_experts_body, layer=layer), grid_spec=grid_spec,
        out_shape=jax.ShapeDtypeStruct((L_SLOTS * ROW_TILE[0], LANES), jnp.int32),
        compiler_params=_cparams(1), name="experts",
    )(block_e, first, par, next_e, nused, xs.reshape(L_SLOTS * ROW_TILE[0], LANES), w1, w3, w2
      ).reshape((L_SLOTS,) + ROW_TILE)


TM_SHARED = 640


def _shared_body(h_ref, w13_ref, w2_ref, o_ref):
    up = jnp.dot(h_ref[...], w13_ref[...], preferred_element_type=F32)
    hid = (_silu(up[:, :D_EXPERT]) * up[:, D_EXPERT:]).astype(BF16)
    o_ref[...] = jnp.dot(hid, w2_ref[...], preferred_element_type=F32)


def _shared(h, w13, w2):
    return pl.pallas_call(
        _shared_body,
        grid=(N // TM_SHARED,),
        in_specs=[pl.BlockSpec((TM_SHARED, D), lambda i: (i, 0)),
                  pl.BlockSpec((D, 2 * D_EXPERT), lambda i: (0, 0)),
                  pl.BlockSpec((D_EXPERT, D), lambda i: (0, 0))],
        out_specs=pl.BlockSpec((TM_SHARED, D), lambda i: (i, 0)),
        out_shape=jax.ShapeDtypeStruct((N, D), F32),
        compiler_params=_cparams(1), name="shared")(h, w13, w2)


def _combine_body(g_ref, wt_ref, sh_ref, x_ref, gn_ref, gp_ref, gs_ref, o_ref, f_ref):
    half = D // 2
    acc_lo = sh_ref[:, :half]
    acc_hi = sh_ref[:, half:]
    wt = wt_ref[...]
    per_tok = TOP_K * ROW_TILE[0]
    for k in range(TOP_K):
        packed = jnp.concatenate([g_ref[pl.ds(k * ROW_TILE[0] + c, SUB, stride=per_tok), :]
                                  for c in range(ROW_TILE[0])], axis=1)
        lo, hi = _unpack_bf16_pair(packed)
        w_c = wt[:, k:k + 1]
        acc_lo = acc_lo + w_c * lo
        acc_hi = acc_hi + w_c * hi
    f_ref[:, :half] = acc_lo
    f_ref[:, half:] = acc_hi
    _post_residual(pl.program_id(0), x_ref, lambda rows: f_ref[rows, :], gn_ref, gp_ref, gs_ref, o_ref, SUB)


def _combine(gathered, wts, shared, x, g_post, mod, layer):
    per_tok = TOP_K * ROW_TILE[0]
    return pl.pallas_call(
        _combine_body,
        grid=(N // SUB,),
        in_specs=[pl.BlockSpec((SUB * per_tok, LANES), lambda i: (i, 0)),
                  pl.BlockSpec((SUB, LANES), lambda i: (i, 0)),
                  pl.BlockSpec((SUB, D), lambda i: (i, 0)),
                  pl.BlockSpec((SUB, D), lambda i: (i, 0)),
                  pl.BlockSpec((1, D), lambda i: (0, 0)),
                  pl.BlockSpec((None, 8, D), lambda i, l=layer: (l, 0, 5)),
                  pl.BlockSpec((None, SUB, D), lambda i, l=layer: (l, 1, 5))],
        out_specs=pl.BlockSpec((SUB, D), lambda i: (i, 0)),
        out_shape=jax.ShapeDtypeStruct((N, D), F32),
        scratch_shapes=[pltpu.VMEM((SUB, D), F32)],
        compiler_params=_cparams(1), name="combine",
    )(gathered.reshape(N * per_tok, LANES), wts, shared, x, g_post.reshape(1, D), mod, mod)


def _reorder_w_in(w):
    return jnp.concatenate([w[:, :1536], w[:, 1552:]], axis=1), jnp.pad(w[:, 1536:1552], ((0, 0), (0, LANES - 16)))


def _rope_tables(pos):
    half = DK // 2
    inv = ROPE_BASE ** (-jnp.arange(half, dtype=F32) / half)
    ang = pos.astype(F32)[:, None] * inv[None, :]
    cos = jnp.cos(ang)
    sin = jnp.sin(ang)
    return jnp.concatenate([cos, cos], axis=1), jnp.concatenate([-sin, sin], axis=1)


def _layer(l, x, mod, s_gla, s_pool, s_ret, wts):
    (norm_mix_pre, norm_mix_post, norm_ffn_pre, norm_ffn_post, w_in, w_gla_gate, b_gla_gate, gla_norm,
     pool_w, pool_scale, ret_norm, sgu_norm, sgu_w, sgu_b, w_branch, w_merge_gate, b_merge_gate, w_out,
     router_w, router_bias, expert_w1, expert_w3, expert_w2, shared_w1, shared_w3, shared_w2) = wts

    h = _prenorm(x, norm_mix_pre[l], mod, l, 0, 1, False)[0]
    w_main, w_low = _reorder_w_in(w_in[l])
    p_main = _matmul(h, w_main.astype(BF16), 1664, 512, name="inproj")
    p_low = _matmul(h, w_low.astype(BF16), 1664, LANES, name="inproj_low")

    w_gate_pad = jnp.pad(w_gla_gate[l], ((0, LANES - 16), (0, 0)))
    b_gate = b_gla_gate[l].reshape(1, HEADS * DK)
    log_gamma = jnp.log1p(-jnp.exp2(-5.0 - jnp.arange(HEADS, dtype=F32)))
    dec_row = jnp.repeat(log_gamma, DK).reshape(1, HEADS * DK)
    cos_p, sin_p = _rope_tables(jnp.arange(T_P))
    cos_p = jnp.tile(cos_p, (1, 2))
    sin_p = jnp.tile(sin_p, (1, 2))
    g_gla = gla_norm[l].reshape(1, HEADS * DV)
    g_ret = ret_norm[l].reshape(1, HEADS * DV)

    oa_p, gla_p = _la_prompt(p_main, C_GQ, C_GK, C_GV, C_GR, p_low, p_low, w_gate_pad, b_gate, g_gla, False)
    oc_p, ret_p = _la_prompt(p_main, C_RQ, C_RK, C_RV, C_RG, cos_p, sin_p, dec_row, b_gate, g_ret, True)
    pw_bf = pool_w[l].astype(BF16)
    pscale = pool_scale[l].reshape(1, 512)
    ob_p = _pool_prompt(p_main, pw_bf, pscale)
    sgu_g = sgu_norm[l].reshape(1, 512)
    od_p = _sgu_prompt(p_main, sgu_g, sgu_w[l], jnp.pad(sgu_b[l].T, ((0, 0), (0, LANES - 4))))

    ps = p_main[N_P:]
    q_t = _to_tiles_t(ps[:, C_GQ:C_GQ + 256])
    k_t = _to_tiles_t(ps[:, C_GK:C_GK + 256])
    glow_t = jnp.pad(_to_tiles_t(p_low[N_P:, :16]), ((0, 0), (0, LANES - 16), (0, 0)))
    w_gate_t = jnp.pad(w_gla_gate[l].T, ((0, 0), (0, LANES - 16)))
    b_col = jnp.broadcast_to(b_gla_gate[l][:, None], (HEADS * DK, LANES))
    logit_t = _gate_logits_t(w_gate_t, glow_t, b_col)
    dummy = jnp.zeros((HEADS * DK, LANES), F32)
    oa_s, gla_s = _la_sample(q_t, k_t, logit_t, dummy, dummy, p_main, C_GV, C_GR, g_gla, s_gla[l], False)
    cos_s, sin_s = _rope_tables(jnp.full((1,), PAST_LEN))
    cos_c = jnp.broadcast_to(jnp.tile(cos_s[0], HEADS)[:, None], (HEADS * DK, LANES))
    sin_c = jnp.broadcast_to(jnp.tile(sin_s[0], HEADS)[:, None], (HEADS * DK, LANES))
    dec_c = jnp.broadcast_to(jnp.repeat(log_gamma, DK)[:, None], (HEADS * DK, LANES))
    rq_t = _to_tiles_t(ps[:, C_RQ:C_RQ + 256])
    rk_t = _to_tiles_t(ps[:, C_RK:C_RK + 256])
    oc_s, ret_s = _la_sample(rq_t, rk_t, dec_c, cos_c, sin_c, p_main, C_RV, C_RG, g_ret, s_ret[l], True)
    sgu_w0 = jnp.repeat(sgu_w[l][:, 0, 0], LANES).reshape(1, 512)
    sgu_b0 = jnp.repeat(sgu_b[l][:, 0], LANES).reshape(1, 512)
    ob_s, od_s, vn_s = _small_sample(p_main, s_pool[l], pw_bf, pscale, sgu_g, sgu_w0, sgu_b0)
    pool_p = p_main[:N_P, C_PIN:C_PIN + 512].reshape(B_P, T_P, 512)[:, T_P - POOL_BUF:]
    pool_s = jnp.concatenate([s_pool[l][:, 1:], ps[:, None, C_PIN:C_PIN + 512]], axis=1)

    branches = [jnp.concatenate([p, s.astype(BF16)], axis=0)
                for p, s in ((oa_p, oa_s), (ob_p, ob_s), (oc_p, oc_s), (od_p, od_s))]
    merged = _merge(h, branches, w_merge_gate[l].astype(BF16), b_merge_gate[l].reshape(1, 4 * D),
                    w_branch[l].astype(BF16))
    x = _outproj(merged, w_out[l].astype(BF16), x, norm_mix_post[l], mod, l)

    h2, h2_f32, h2_packed = _prenorm(x, norm_ffn_pre[l], mod, l, 3, 4, True)
    rw = jnp.pad(router_w[l], ((0, 0), (0, LANES - N_EXPERTS)))
    rb = jnp.broadcast_to(router_bias[l][:, None], (N_EXPERTS, SUB))
    eid, pos, wt, counts = _router(h2_f32, rw, rb)
    counts = counts[:, 0].astype(jnp.int32)
    padded = (counts + EXP_BLOCK - 1) // EXP_BLOCK * EXP_BLOCK
    pad_end = jnp.cumsum(padded)
    pad_start = pad_end - padded
    nused = (pad_end[-1] // EXP_BLOCK).astype(jnp.int32).reshape(1)
    blk_row = jnp.arange(N_BLOCKS, dtype=jnp.int32) * EXP_BLOCK
    block_e = jnp.minimum(jnp.sum((blk_row[:, None] >= pad_end[None, :]).astype(jnp.int32), axis=1),
                          N_EXPERTS - 1)
    first = jnp.concatenate([jnp.ones((1,), jnp.int32), (block_e[1:] != block_e[:-1]).astype(jnp.int32)])
    first = jnp.where(blk_row < pad_end[-1], first, 0)
    par = (jnp.cumsum(first) - 1) % 2
    live = jnp.where(padded > 0, jnp.arange(N_EXPERTS), N_EXPERTS)
    after = jnp.concatenate([lax.cummin(live, reverse=True)[1:], jnp.full((1,), N_EXPERTS)])
    next_e = jnp.where(after < N_EXPERTS, after, -1)[block_e].astype(jnp.int32)
    start_of = jnp.sum(jnp.where(eid[:, :, None] == jnp.arange(N_EXPERTS), pad_start.astype(jnp.int32), 0), axis=-1)
    slots = (start_of + pos).T.reshape(N_ASSIGN)
    wt = jnp.pad(wt.T, ((0, 0), (0, LANES - TOP_K)))
    xs = _sc_dispatch(h2_packed, slots)
    w13 = jnp.concatenate([shared_w1[l], shared_w3[l]], axis=1).astype(BF16)
    shared = _shared(h2, w13, shared_w2[l].astype(BF16))
    ys = _experts(block_e, first, par.astype(jnp.int32), next_e, nused, xs, expert_w1, expert_w3, expert_w2, l)
    x = _combine(_sc_gather(ys, slots), wt, shared, x, norm_ffn_post[l], mod, l)
    return x, (gla_p, gla_s, pool_p, pool_s, ret_p, ret_s, vn_s)


def kernel(x_prompt, x_sample, c_prompt, c_sample, state_gla, state_pool, state_ret, w_ada, b_ada, norm_mix_pre, norm_mix_post, norm_ffn_pre, norm_ffn_post, w_in, w_gla_gate, b_gla_gate, gla_norm, pool_w, pool_scale, ret_norm, sgu_norm, sgu_w, sgu_b, w_branch, w_merge_gate, b_merge_gate, w_out, router_w, router_bias, expert_w1, expert_w3, expert_w2, shared_w1, shared_w3, shared_w2):
    wts = (norm_mix_pre, norm_mix_post, norm_ffn_pre, norm_ffn_post, w_in, w_gla_gate, b_gla_gate, gla_norm,
           pool_w, pool_scale, ret_norm, sgu_norm, sgu_w, sgu_b, w_branch, w_merge_gate, b_merge_gate, w_out,
           router_w, router_bias, expert_w1, expert_w3, expert_w2, shared_w1, shared_w3, shared_w2)
    c_all = jnp.zeros((MOD_ROWS, D), F32).at[:B_P].set(c_prompt).at[SUB:SUB + N_S].set(c_sample)
    mod = _ada(c_all, w_ada, b_ada)
    x = jnp.concatenate([x_prompt.reshape(N_P, D), x_sample.reshape(N_S, D)], axis=0)
    per_layer = []
    for l in range(DEPTH):
        x, states = _layer(l, x, mod, state_gla, state_pool, state_ret, wts)
        per_layer.append(states)
    gla_p, gla_s, pool_p, pool_s, ret_p, ret_s, vn_s = (jnp.stack(z) for z in zip(*per_layer))
    return (x[:N_P].reshape(B_P, T_P, D), x[N_P:].reshape(N_S, 1, D),
            gla_p, gla_s, pool_p, pool_s, ret_p, ret_s, vn_s.reshape(DEPTH, N_S, 1, 512))
```

```python
import functools

import jax
import jax.numpy as jnp
from jax import lax
from jax.experimental import pallas as pl
from jax.experimental.pallas import tpu as pltpu
from jax.experimental.pallas import tpu_sc as plsc

F32 = jnp.float32
BF16 = jnp.bfloat16
HIGHEST = lax.Precision.HIGHEST

D = 2048
B_P, T_P = 4, 2048
N_P = B_P * T_P
N_S = 128
N = N_P + N_S
DEPTH = 2
PAST_LEN = 16384
EPS = 1e-6
HEADS, DK, DV = 4, 64, 128
CHUNK = 64
GATE_TEMP = 16.0
POOL_WINDOWS = (2, 4, 8, 16)
POOL_BUF = 15
ROPE_BASE = 10000.0
N_EXPERTS = 64
TOP_K = 8
D_EXPERT = 512
ROUTED_SCALE = 2.5

LANES = 128
SUB = 128
MOD_ROWS = 256
EXP_BLOCK = 256
N_ASSIGN = N * TOP_K
N_BLOCKS = -(-(N_ASSIGN + N_EXPERTS * (EXP_BLOCK - 1)) // EXP_BLOCK)
L_SLOTS = N_BLOCKS * EXP_BLOCK
VMEM_LIMIT = 56 * 1024 * 1024

C_GQ, C_GK, C_GV, C_GR, C_PIN, C_RQ, C_RK, C_RV, C_RG, C_SU, C_SV = (
    0, 256, 512, 1024, 1536, 2048, 2304, 2560, 3072, 3584, 4096)
P_MAIN = 4608


def _cparams(n_axes=1):
    return pltpu.CompilerParams(dimension_semantics=("arbitrary",) * n_axes,
                                vmem_limit_bytes=VMEM_LIMIT)


def _silu(x):
    return x * jax.nn.sigmoid(x)


def _mod_rows(t, mp_ref, ms_ref):
    b = jnp.minimum(t // (T_P // SUB), B_P - 1)
    return jnp.where(t >= N_P // SUB, ms_ref[...], mp_ref[pl.ds(b, 1), :])


def _mod_specs(layer, part):
    return [pl.BlockSpec((None, 8, D), lambda i, l=layer, p=part: (l, 0, p)),
            pl.BlockSpec((None, SUB, D), lambda i, l=layer, p=part: (l, 1, p))]


def _pack_bf16_pair(lo, hi):
    lo_u = lax.bitcast_convert_type(lo.astype(BF16).astype(F32), jnp.uint32)
    hi_u = lax.bitcast_convert_type(hi.astype(BF16).astype(F32), jnp.uint32)
    return lax.bitcast_convert_type((hi_u & jnp.uint32(0xFFFF0000)) | (lo_u >> 16), jnp.int32)


def _unpack_bf16_pair(w):
    u = lax.bitcast_convert_type(w, jnp.uint32)
    lo = lax.bitcast_convert_type(u << 16, F32)
    hi = lax.bitcast_convert_type(u & jnp.uint32(0xFFFF0000), F32)
    return lo, hi


ROW_TILE = (8, LANES)


def _load_row_tiles(ref):
    return jnp.concatenate([ref[:, c, :] for c in range(ROW_TILE[0])], axis=1)


def _store_row_tiles(ref, val):
    for c in range(ROW_TILE[0]):
        ref[:, c, :] = val[:, c * LANES:(c + 1) * LANES]


def _load_row_tiles_2d(ref, rows):
    return jnp.concatenate([ref[pl.ds(c, rows, stride=ROW_TILE[0]), :] for c in range(ROW_TILE[0])], axis=1)


def _store_row_tiles_2d(ref, val, rows):
    for c in range(ROW_TILE[0]):
        ref[pl.ds(c, rows, stride=ROW_TILE[0]), :] = val[:, c * LANES:(c + 1) * LANES]


def _ada_body(c_ref, w_ref, b_ref, o_ref):
    s = _silu(c_ref[...]).astype(BF16)
    o_ref[...] = jnp.dot(s, w_ref[...].astype(BF16), preferred_element_type=F32) + b_ref[...]


def _ada(c_all, w_ada, b_ada):
    tn = 1024
    return pl.pallas_call(
        _ada_body,
        grid=(DEPTH, 6 * D // tn),
        in_specs=[pl.BlockSpec((MOD_ROWS, D), lambda l, j: (0, 0)),
                  pl.BlockSpec((None, D, tn), lambda l, j: (l, 0, j)),
                  pl.BlockSpec((None, 1, tn), lambda l, j: (l, 0, j))],
        out_specs=pl.BlockSpec((None, MOD_ROWS, tn), lambda l, j: (l, 0, j)),
        out_shape=jax.ShapeDtypeStruct((DEPTH, MOD_ROWS, 6 * D), F32),
        compiler_params=_cparams(2), name="ada")(c_all, w_ada, b_ada.reshape(DEPTH, 1, 6 * D))


ZERO_ROWS = 2 * SUB


def _prenorm_rows(t, x, g_ref, shp_ref, shs_ref, scp_ref, scs_ref):
    y = x * lax.rsqrt(jnp.mean(x * x, axis=-1, keepdims=True) + EPS) * g_ref[...]
    return y * (1.0 + _mod_rows(t, scp_ref, scs_ref)) + _mod_rows(t, shp_ref, shs_ref)


TM_IN = 640
TN_IN = 512


def _inproj_body(x_ref, g_ref, shp_ref, shs_ref, scp_ref, scs_ref, w_ref, h_ref, p_ref):
    @pl.when(pl.program_id(1) == 0)
    def _():
        for sidx in range(TM_IN // SUB):
            rows = pl.ds(sidx * SUB, SUB)
            t = pl.program_id(0) * (TM_IN // SUB) + sidx
            h_ref[rows, :] = _prenorm_rows(t, x_ref[rows, :], g_ref, shp_ref, shs_ref, scp_ref, scs_ref).astype(BF16)

    p_ref[...] = jnp.dot(h_ref[...], w_ref[...], preferred_element_type=F32)


def _inproj(x, g, mod, layer, w):
    mspec = lambda part, rows, blk: pl.BlockSpec((None, rows, D), lambda i, j, l=layer, p=part, b=blk: (l, b, p))
    return pl.pallas_call(
        _inproj_body,
        grid=(N // TM_IN, P_MAIN // TN_IN),
        in_specs=[pl.BlockSpec((TM_IN, D), lambda i, j: (i, 0)),
                  pl.BlockSpec((1, D), lambda i, j: (0, 0)),
                  mspec(0, 8, 0), mspec(0, SUB, 1), mspec(1, 8, 0), mspec(1, SUB, 1),
                  pl.BlockSpec((D, TN_IN), lambda i, j: (0, j))],
        out_specs=[pl.BlockSpec((TM_IN, D), lambda i, j: (i, 0)),
                   pl.BlockSpec((TM_IN, TN_IN), lambda i, j: (i, j))],
        out_shape=[jax.ShapeDtypeStruct((N, D), BF16), jax.ShapeDtypeStruct((N, P_MAIN), F32)],
        compiler_params=_cparams(2), name="inproj")(x, g.reshape(1, D), mod, mod, mod, mod, w)


def _mm_body(x_ref, w_ref, o_ref):
    o_ref[...] = jnp.dot(x_ref[...], w_ref[...], preferred_element_type=F32).astype(o_ref.dtype)


def _matmul(x, w, tm, tn, out_dtype=F32, name="mm"):
    m, k = x.shape
    n = w.shape[1]
    return pl.pallas_call(
        _mm_body,
        grid=(m // tm, n // tn),
        in_specs=[pl.BlockSpec((tm, k), lambda i, j: (i, 0)),
                  pl.BlockSpec((k, tn), lambda i, j: (0, j))],
        out_specs=pl.BlockSpec((tm, tn), lambda i, j: (i, j)),
        out_shape=jax.ShapeDtypeStruct((m, n), out_dtype),
        compiler_params=_cparams(2), name=name)(x, w)


ROWS_LA = 256


def _swap_halves_lanes(x):
    lane = lax.broadcasted_iota(jnp.int32, x.shape, 1)
    return jnp.where((lane % 64) < 32, pltpu.roll(x, 96, 1), pltpu.roll(x, 32, 1))


def _rope_lanes(x, cos, sin_signed):
    parts = []
    for half in range(2):
        xh = x[:, half * LANES:(half + 1) * LANES]
        parts.append(xh * cos + _swap_halves_lanes(xh) * sin_signed)
    return jnp.concatenate(parts, axis=1)


def _la_prompt_body(q_ref, k_ref, v_ref, r_ref, aux_ref, aux2_ref, dec_ref, bias_ref, g_ref,
                    o_ref, st_out_ref, st_ref, *, retention):
    t = pl.program_id(1)

    @pl.when(t == 0)
    def _():
        st_ref[...] = jnp.zeros_like(st_ref)

    ri = lax.broadcasted_iota(jnp.int32, (CHUNK, CHUNK), 0)
    ci = lax.broadcasted_iota(jnp.int32, (CHUNK, CHUNK), 1)
    causal = ri >= ci
    tril = causal.astype(F32)
    scale = DK ** -0.5

    for c in range(ROWS_LA // CHUNK):
        rows = pl.ds(c * CHUNK, CHUNK)
        q = q_ref[rows, :]
        k = k_ref[rows, :]
        v = v_ref[rows, :]
        if retention:
            cos = aux_ref[rows, :]
            sin = aux2_ref[rows, :]
            q = _rope_lanes(q, cos, sin)
            k = _rope_lanes(k, cos, sin) * scale
            la = jnp.broadcast_to(dec_ref[...], (CHUNK, HEADS * DK))
        else:
            q = q * scale
            logit = jnp.dot(aux_ref[rows, :], dec_ref[...], precision=HIGHEST,
                            preferred_element_type=F32) + bias_ref[...]
            la = jax.nn.log_sigmoid(logit) / GATE_TEMP
        bc = jnp.dot(tril, la, precision=HIGHEST, preferred_element_type=F32)
        bl = bc[CHUNK - 1:CHUNK, :]
        qd = q * jnp.exp(bc)
        ki = k * jnp.exp(-bc)
        ke = k * jnp.exp(bl - bc)
        ac = jnp.exp(bl)
        outs = []
        for h in range(HEADS):
            ks = slice(h * DK, (h + 1) * DK)
            vs = slice(h * DV, (h + 1) * DV)
            qd_h = qd[:, ks].astype(BF16)
            ki_h = ki[:, ks].astype(BF16)
            ke_h = ke[:, ks].astype(BF16)
            v_h = v[:, vs].astype(BF16)
            sc = lax.dot_general(qd_h, ki_h, (((1,), (1,)), ((), ())), preferred_element_type=F32)
            sc = jnp.where(causal, sc, 0.0)
            o_h = jnp.dot(sc.astype(BF16), v_h, preferred_element_type=F32)
            st = st_ref[h]
            o_h = o_h + lax.dot_general(qd_h, st.astype(BF16), (((1,), (1,)), ((), ())),
                                        preferred_element_type=F32)
            kv_t = lax.dot_general(v_h, ke_h, (((0,), (0,)), ((), ())), preferred_element_type=F32)
            st_ref[h] = st * ac[:, ks] + kv_t
            o_n = o_h * lax.rsqrt(jnp.mean(o_h * o_h, axis=-1, keepdims=True) + EPS) * g_ref[:, vs]
            outs.append(o_n)
        o = jnp.concatenate(outs, axis=1) * _silu(r_ref[rows, :])
        o_ref[rows, :] = o.astype(BF16)

    st_out_ref[...] = st_ref[...]


def _la_prompt(p_main, cq, ck, cv, cr, aux, aux2, dec, bias, g, retention):
    nt = T_P // ROWS_LA
    rowblk = lambda b, t: b * nt + t
    if retention:
        aux_specs = [pl.BlockSpec((ROWS_LA, LANES), lambda b, t: (t, 0)),
                     pl.BlockSpec((ROWS_LA, LANES), lambda b, t: (t, 0))]
    else:
        aux_specs = [pl.BlockSpec((ROWS_LA, LANES), lambda b, t: (rowblk(b, t), 0)),
                     pl.BlockSpec((8, LANES), lambda b, t: (0, 0))]
    o, st = pl.pallas_call(
        functools.partial(_la_prompt_body, retention=retention),
        grid=(B_P, nt),
        in_specs=[pl.BlockSpec((ROWS_LA, 256), lambda b, t: (rowblk(b, t), cq // 256)),
                  pl.BlockSpec((ROWS_LA, 256), lambda b, t: (rowblk(b, t), ck // 256)),
                  pl.BlockSpec((ROWS_LA, 512), lambda b, t: (rowblk(b, t), cv // 512)),
                  pl.BlockSpec((ROWS_LA, 512), lambda b, t: (rowblk(b, t), cr // 512))]
        + aux_specs
        + [pl.BlockSpec(dec.shape, lambda b, t: (0, 0)),
           pl.BlockSpec((1, HEADS * DK), lambda b, t: (0, 0)),
           pl.BlockSpec((1, HEADS * DV), lambda b, t: (0, 0))],
        out_specs=[pl.BlockSpec((ROWS_LA, HEADS * DV), lambda b, t: (rowblk(b, t), 0)),
                   pl.BlockSpec((None, HEADS, DV, DK), lambda b, t: (b, 0, 0, 0))],
        out_shape=[jax.ShapeDtypeStruct((N_P, HEADS * DV), BF16),
                   jax.ShapeDtypeStruct((B_P, HEADS, DV, DK), F32)],
        scratch_shapes=[pltpu.VMEM((HEADS, DV, DK), F32)],
        compiler_params=_cparams(2), name="ret_prompt" if retention else "gla_prompt",
    )(p_main, p_main, p_main, p_main, aux, aux2, dec, bias, g)
    return o, jnp.swapaxes(st, -1, -2)


SAMPLE_TILE = 8


def _la_sample_body(qt_ref, kt_ref, lt_ref, cos_ref, sin_ref, v_ref, r_ref, g_ref, s_ref,
                    o_ref, s_out_ref, *, retention):
    scale = DK ** -0.5
    qt = qt_ref[...]
    kt = kt_ref[...]
    if retention:
        def rope(x):
            sw = jnp.concatenate(
                [x[h * DK + (DK // 2) * (1 - j): h * DK + (DK // 2) * (2 - j), :]
                 for h in range(HEADS) for j in range(2)], axis=0)
            return x * cos_ref[...] + sw * sin_ref[...]
        qt = rope(qt)
        kt = rope(kt) * scale
        la = lt_ref[...]
    else:
        qt = qt * scale
        la = jax.nn.log_sigmoid(lt_ref[...]) / GATE_TEMP
    at = jnp.exp(la)
    qd = qt * at
    ki = kt * jnp.exp(-la)
    prod = qd * ki
    v8 = v_ref[...]
    r8 = r_ref[...]
    g = g_ref[...]
    for j in range(SAMPLE_TILE):
        for h in range(HEADS):
            ks = slice(h * DK, (h + 1) * DK)
            vs = slice(h * DV, (h + 1) * DV)
            a_c = jnp.broadcast_to(at[ks, j:j + 1], (DK, DV))
            k_c = jnp.broadcast_to(kt[ks, j:j + 1], (DK, DV))
            q_c = jnp.broadcast_to(qd[ks, j:j + 1], (DK, DV))
            s_c = jnp.broadcast_to(jnp.sum(prod[ks, j:j + 1], axis=0, keepdims=True), (1, DV))
            s0 = s_ref[j, h]
            v_row = v8[j:j + 1, vs]
            s_out_ref[j, h] = a_c * s0 + k_c * v_row
            o_row = s_c * v_row + jnp.sum(q_c * s0, axis=0, keepdims=True)
            o_n = o_row * lax.rsqrt(jnp.mean(o_row * o_row, axis=-1, keepdims=True) + EPS) * g[:, vs]
            o_ref[j:j + 1, vs] = o_n * _silu(r8[j:j + 1, vs])


def _la_sample(qt, kt, lt, cos_t, sin_t, p_main, cv, cr, g, s0, retention):
    nt = N_S // SAMPLE_TILE
    row0 = N_P // SAMPLE_TILE
    tile = pl.BlockSpec((None, HEADS * DK, LANES), lambda i: (i, 0, 0))
    full = pl.BlockSpec((HEADS * DK, LANES), lambda i: (0, 0))
    lt_spec = full if retention else tile
    return pl.pallas_call(
        functools.partial(_la_sample_body, retention=retention),
        grid=(nt,),
        in_specs=[tile, tile, lt_spec, full, full,
                  pl.BlockSpec((SAMPLE_TILE, 512), lambda i: (row0 + i, cv // 512)),
                  pl.BlockSpec((SAMPLE_TILE, 512), lambda i: (row0 + i, cr // 512)),
                  pl.BlockSpec((1, HEADS * DV), lambda i: (0, 0)),
                  pl.BlockSpec((SAMPLE_TILE, HEADS, DK, DV), lambda i: (i, 0, 0, 0))],
        out_specs=[pl.BlockSpec((SAMPLE_TILE, HEADS * DV), lambda i: (i, 0)),
                   pl.BlockSpec((SAMPLE_TILE, HEADS, DK, DV), lambda i: (i, 0, 0, 0))],
        out_shape=[jax.ShapeDtypeStruct((N_S, HEADS * DV), F32),
                   jax.ShapeDtypeStruct((N_S, HEADS, DK, DV), F32)],
        compiler_params=_cparams(1), name="ret_sample" if retention else "gla_sample",
    )(qt, kt, lt, cos_t, sin_t, p_main, p_main, g, s0)


def _gate_logits_t_body(w_ref, x_ref, b_ref, o_ref):
    o_ref[...] = jnp.dot(w_ref[...], x_ref[...], precision=HIGHEST, preferred_element_type=F32) + b_ref[...]


def _gate_logits_t(w_gate_t, glow_t, b_col):
    nt = N_S // SAMPLE_TILE
    return pl.pallas_call(
        _gate_logits_t_body,
        grid=(nt,),
        in_specs=[pl.BlockSpec((HEADS * DK, LANES), lambda i: (0, 0)),
                  pl.BlockSpec((None, LANES, LANES), lambda i: (i, 0, 0)),
                  pl.BlockSpec((HEADS * DK, LANES), lambda i: (0, 0))],
        out_specs=pl.BlockSpec((None, HEADS * DK, LANES), lambda i: (i, 0, 0)),
        out_shape=jax.ShapeDtypeStruct((nt, HEADS * DK, LANES), F32),
        compiler_params=_cparams(1), name="gate_logits_t")(w_gate_t, glow_t, b_col)


def _to_tiles_t(x):
    c = x.shape[1]
    xt = jnp.swapaxes(x.reshape(N_S // SAMPLE_TILE, SAMPLE_TILE, c), 1, 2)
    return jnp.pad(xt, ((0, 0), (0, 0), (0, LANES - SAMPLE_TILE)))


ROWS_POOL = 512


def _pool_mix(y, w_ref, sc_ref):
    outs = []
    for gi in range(4):
        cs = slice(gi * LANES, (gi + 1) * LANES)
        outs.append(jnp.dot(y[:, cs].astype(BF16), w_ref[gi], preferred_element_type=F32))
    return jnp.concatenate(outs, axis=1) * sc_ref[...]


def _pool_prompt_body(p_ref, halo_ref, w_ref, sc_ref, o_ref):
    t = pl.program_id(1)
    p = p_ref[...]
    halo = jnp.where(t == 0, 0.0, halo_ref[...])
    full = jnp.concatenate([halo, p], axis=0)
    pos = t * ROWS_POOL + lax.broadcasted_iota(jnp.int32, (ROWS_POOL, LANES), 0)
    means = []
    for gi, w in enumerate(POOL_WINDOWS):
        s = full[:, gi * LANES:(gi + 1) * LANES]
        step = 1
        while step < w:
            s = s + pltpu.roll(s, step, 0)
            step *= 2
        win = s[16:, :]
        cnt = jnp.minimum(w, pos + 1).astype(F32)
        means.append(win / cnt)
    y = jnp.concatenate(means, axis=1) - p
    o_ref[...] = _pool_mix(y, w_ref, sc_ref).astype(BF16)


def _pool_prompt(p_main, w_bf, scale):
    nt = T_P // ROWS_POOL
    return pl.pallas_call(
        _pool_prompt_body,
        grid=(B_P, nt),
        in_specs=[pl.BlockSpec((ROWS_POOL, 512), lambda b, t: (b * nt + t, C_PIN // 512)),
                  pl.BlockSpec((16, 512), lambda b, t: (jnp.maximum((b * nt + t) * (ROWS_POOL // 16) - 1, 0),
                                                        C_PIN // 512)),
                  pl.BlockSpec((4, LANES, LANES), lambda b, t: (0, 0, 0)),
                  pl.BlockSpec((1, 512), lambda b, t: (0, 0))],
        out_specs=pl.BlockSpec((ROWS_POOL, 512), lambda b, t: (b * nt + t, 0)),
        out_shape=jax.ShapeDtypeStruct((N_P, 512), BF16),
        compiler_params=_cparams(2), name="pool_prompt")(p_main, p_main, w_bf, scale)


def _small_sample_body(p_ref, buf_ref, pw_ref, psc_ref, u_ref, sv_ref, sg_ref, sw_ref, sb_ref,
                       ob_ref, od_ref, vn_ref):
    p = p_ref[...]
    means = []
    for gi, w in enumerate(POOL_WINDOWS):
        cs = slice(gi * LANES, (gi + 1) * LANES)
        s = p[:, cs]
        for j in range(1, w):
            s = s + buf_ref[:, POOL_BUF - j, cs]
        means.append(s / float(min(w, PAST_LEN + 1)))
    y = jnp.concatenate(means, axis=1) - p
    ob_ref[...] = _pool_mix(y, pw_ref, psc_ref)
    sv = sv_ref[...]
    vn = sv * lax.rsqrt(jnp.mean(sv * sv, axis=-1, keepdims=True) + EPS) * sg_ref[...]
    vn_ref[...] = vn
    od_ref[...] = u_ref[...] * (sw_ref[...] * vn + sb_ref[...])


def _small_sample(p_main, buf, pw_bf, pscale, sgu_g, sgu_w0, sgu_b0):
    row = N_P // N_S
    col = lambda c: pl.BlockSpec((N_S, 512), lambda i, c=c: (row, c // 512))
    vec = pl.BlockSpec((1, 512), lambda i: (0, 0))
    return pl.pallas_call(
        _small_sample_body,
        grid=(1,),
        in_specs=[col(C_PIN), pl.BlockSpec((N_S, POOL_BUF, 512), lambda i: (0, 0, 0)),
                  pl.BlockSpec((4, LANES, LANES), lambda i: (0, 0, 0)), vec,
                  col(C_SU), col(C_SV), vec, vec, vec],
        out_specs=[pl.BlockSpec((N_S, 512), lambda i: (0, 0))] * 3,
        out_shape=[jax.ShapeDtypeStruct((N_S, 512), F32)] * 3,
        compiler_params=_cparams(1), name="small_sample",
    )(p_main, buf, pw_bf, pscale, p_main, p_main, sgu_g, sgu_w0, sgu_b0)


ROWS_SGU = 512
SGU_CHUNK = 128


def _sgu_prompt_body(u_ref, v_ref, g_ref, w_ref, bt_ref, o_ref):
    ri = lax.broadcasted_iota(jnp.int32, (SGU_CHUNK, SGU_CHUNK), 0)
    ci = lax.broadcasted_iota(jnp.int32, (SGU_CHUNK, SGU_CHUNK), 1)
    causal = ri >= ci
    for c in range(ROWS_SGU // SGU_CHUNK):
        rows = pl.ds(c * SGU_CHUNK, SGU_CHUNK)
        v = v_ref[rows, :]
        vn = (v * lax.rsqrt(jnp.mean(v * v, axis=-1, keepdims=True) + EPS) * g_ref[...]).astype(BF16)
        outs = []
        for gi in range(4):
            cs = slice(gi * LANES, (gi + 1) * LANES)
            w = jnp.where(causal, w_ref[gi], 0.0).astype(BF16)
            mixed = jnp.dot(w, vn[:, cs], preferred_element_type=F32)
            outs.append(mixed + jnp.broadcast_to(bt_ref[:, gi:gi + 1], (SGU_CHUNK, LANES)))
        o_ref[rows, :] = (u_ref[rows, :] * jnp.concatenate(outs, axis=1)).astype(BF16)


def _sgu_prompt(p_main, g, w, b_t):
    return pl.pallas_call(
        _sgu_prompt_body,
        grid=(N_P // ROWS_SGU,),
        in_specs=[pl.BlockSpec((ROWS_SGU, 512), lambda i: (i, C_SU // 512)),
                  pl.BlockSpec((ROWS_SGU, 512), lambda i: (i, C_SV // 512)),
                  pl.BlockSpec((1, 512), lambda i: (0, 0)),
                  pl.BlockSpec((4, SGU_CHUNK, SGU_CHUNK), lambda i: (0, 0, 0)),
                  pl.BlockSpec((SGU_CHUNK, LANES), lambda i: (0, 0))],
        out_specs=pl.BlockSpec((ROWS_SGU, 512), lambda i: (i, 0)),
        out_shape=jax.ShapeDtypeStruct((N_P, 512), BF16),
        compiler_params=_cparams(1), name="sgu_prompt")(p_main, p_main, g, w, b_t)


TM_MERGE = 640
TN_MERGE = 512


def _merge_body(h_ref, ba_ref, bb_ref, bc_ref, bd_ref, g0, g1, g2, g3, u0, u1, u2, u3,
                c0, c1, c2, c3, o_ref):
    h = h_ref[...]
    acc = None
    for br, gw, uw, gb in ((ba_ref, g0, u0, c0), (bb_ref, g1, u1, c1), (bc_ref, g2, u2, c2), (bd_ref, g3, u3, c3)):
        gate = jax.nn.sigmoid(jnp.dot(h, gw[...], preferred_element_type=F32) + gb[...])
        up = jnp.dot(br[...], uw[...], preferred_element_type=F32)
        acc = gate * up if acc is None else acc + gate * up
    o_ref[...] = acc.astype(BF16)


def _merge(h, branches, w_mg, b_mg, w_br):
    nj = D // TN_MERGE
    row = lambda w: pl.BlockSpec((TM_MERGE, w), lambda i, j: (i, 0))
    gate_w = [pl.BlockSpec((D, TN_MERGE), lambda i, j, b=b: (0, b * nj + j)) for b in range(4)]
    up_w = [pl.BlockSpec((None, 512, TN_MERGE), lambda i, j, b=b: (b, 0, j)) for b in range(4)]
    gate_b = [pl.BlockSpec((1, TN_MERGE), lambda i, j, b=b: (0, b * nj + j)) for b in range(4)]
    return pl.pallas_call(
        _merge_body,
        grid=(N // TM_MERGE, nj),
        in_specs=[row(D)] + [row(512)] * 4 + gate_w + up_w + gate_b,
        out_specs=pl.BlockSpec((TM_MERGE, TN_MERGE), lambda i, j: (i, j)),
        out_shape=jax.ShapeDtypeStruct((N, D), BF16),
        compiler_params=_cparams(2), name="merge",
    )(h, *branches, w_mg, w_mg, w_mg, w_mg, w_br, w_br, w_br, w_br, b_mg, b_mg, b_mg, b_mg)


TM_OUT = 640
TN_OUT = 512


def _post_residual(t0, x_ref, get_y, gn_ref, gp_ref, gs_ref, o_ref, rows_total):
    for s in range(rows_total // SUB):
        rows = pl.ds(s * SUB, SUB)
        y = get_y(rows)
        yn = y * lax.rsqrt(jnp.mean(y * y, axis=-1, keepdims=True) + EPS) * gn_ref[...]
        o_ref[rows, :] = x_ref[rows, :] + _mod_rows(t0 + s, gp_ref, gs_ref) * yn


def _outproj_body(m_ref, w_ref, x_ref, gn_ref, gp_ref, gs_ref, o_ref, acc_ref):
    j = pl.program_id(1)
    acc_ref[j] = jnp.dot(m_ref[...], w_ref[...], preferred_element_type=F32)

    @pl.when(j == D // TN_OUT - 1)
    def _():
        get_y = lambda rows: jnp.concatenate([acc_ref[c, rows, :] for c in range(D // TN_OUT)], axis=1)
        _post_residual(pl.program_id(0) * (TM_OUT // SUB), x_ref, get_y, gn_ref, gp_ref, gs_ref,
                       o_ref, TM_OUT)


def _outproj(merged, w_out, x, g_post, mod, layer):
    mspec = [pl.BlockSpec((None, 8, D), lambda i, j, l=layer: (l, 0, 2)),
             pl.BlockSpec((None, SUB, D), lambda i, j, l=layer: (l, 1, 2))]
    return pl.pallas_call(
        _outproj_body,
        grid=(N // TM_OUT, D // TN_OUT),
        in_specs=[pl.BlockSpec((TM_OUT, D), lambda i, j: (i, 0)),
                  pl.BlockSpec((D, TN_OUT), lambda i, j: (0, j)),
                  pl.BlockSpec((TM_OUT, D), lambda i, j: (i, 0)),
                  pl.BlockSpec((1, D), lambda i, j: (0, 0))] + mspec,
        out_specs=pl.BlockSpec((TM_OUT, D), lambda i, j: (i, 0)),
        out_shape=jax.ShapeDtypeStruct((N, D), F32),
        scratch_shapes=[pltpu.VMEM((D // TN_OUT, TM_OUT, TN_OUT), F32)],
        compiler_params=_cparams(2), name="outproj",
    )(merged, w_out, x, g_post.reshape(1, D), mod, mod)


def _router_body(lg_ref, b_ref, eid_ref, pos_ref, wt_ref, cnt_ref, run_ref):
    i = pl.program_id(0)

    @pl.when(i == 0)
    def _():
        run_ref[...] = jnp.zeros_like(run_ref)

    ng, gs = 8, N_EXPERTS // 8
    neg = -jnp.inf
    scores = jax.nn.sigmoid(lg_ref[...].T[:N_EXPERTS, :])
    sel = scores + b_ref[...]
    sel3 = sel.reshape(ng, gs, SUB)
    sub3 = lax.broadcasted_iota(jnp.int32, (ng, gs, SUB), 1)
    gmax = jnp.max(sel3, axis=1, keepdims=True)
    first = jnp.min(jnp.where(sel3 == gmax, sub3, gs), axis=1, keepdims=True)
    gmax2 = jnp.max(jnp.where(sub3 == first, neg, sel3), axis=1, keepdims=True)
    gscore = (gmax + gmax2).reshape(ng, SUB)
    gidx = lax.broadcasted_iota(jnp.int32, (ng, SUB), 0)
    grank = jnp.zeros((ng, SUB), jnp.int32)
    for s in range(1, ng):
        other = pltpu.roll(gscore, s, 0)
        lower = gidx >= s
        grank += ((other > gscore) | ((other == gscore) & lower)).astype(jnp.int32)
    keep = jnp.broadcast_to((grank < 4).reshape(ng, 1, SUB), (ng, gs, SUB))
    masked = jnp.where(keep, sel3, neg).reshape(N_EXPERTS, SUB)
    eidx = lax.broadcasted_iota(jnp.int32, (N_EXPERTS, SUB), 0)
    rank = jnp.zeros((N_EXPERTS, SUB), jnp.int32)
    for s in range(1, N_EXPERTS):
        other = pltpu.roll(masked, s, 0)
        lower = eidx >= s
        rank += ((other > masked) | ((other == masked) & lower)).astype(jnp.int32)
    chosen = rank < TOP_K
    w_sel = jnp.where(chosen, scores, 0.0)
    w_sel = w_sel / jnp.sum(w_sel, axis=0, keepdims=True) * ROUTED_SCALE
    ri = lax.broadcasted_iota(jnp.int32, (SUB, SUB), 0)
    ci = lax.broadcasted_iota(jnp.int32, (SUB, SUB), 1)
    onehot = chosen.astype(BF16)
    pos = jnp.dot(onehot, (ri < ci).astype(BF16), preferred_element_type=F32) + run_ref[...]
    run_ref[...] = run_ref[...] + jnp.sum(chosen.astype(F32), axis=1, keepdims=True)
    cnt_ref[...] = run_ref[...]
    eidx_f = eidx.astype(F32)
    rows_e, rows_p, rows_w = [], [], []
    for kk in range(TOP_K):
        m = chosen & (rank == kk)
        rows_e.append(jnp.sum(jnp.where(m, eidx_f, 0.0), axis=0, keepdims=True))
        rows_p.append(jnp.sum(jnp.where(m, pos, 0.0), axis=0, keepdims=True))
        rows_w.append(jnp.sum(jnp.where(m, w_sel, 0.0), axis=0, keepdims=True))
    eid_ref[...] = jnp.concatenate(rows_e, axis=0).astype(jnp.int32)
    pos_ref[...] = jnp.concatenate(rows_p, axis=0).astype(jnp.int32)
    wt_ref[...] = jnp.concatenate(rows_w, axis=0)


def _router(logits, rb_col):
    tile = pl.BlockSpec((TOP_K, SUB), lambda i: (0, i))
    return pl.pallas_call(
        _router_body,
        grid=(N // SUB,),
        in_specs=[pl.BlockSpec((SUB, LANES), lambda i: (i, 0)),
                  pl.BlockSpec((N_EXPERTS, SUB), lambda i: (0, 0))],
        out_specs=[tile, tile, tile, pl.BlockSpec((N_EXPERTS, SUB), lambda i: (0, 0))],
        out_shape=[jax.ShapeDtypeStruct((TOP_K, N), jnp.int32), jax.ShapeDtypeStruct((TOP_K, N), jnp.int32),
                   jax.ShapeDtypeStruct((TOP_K, N), F32), jax.ShapeDtypeStruct((N_EXPERTS, SUB), F32)],
        scratch_shapes=[pltpu.VMEM((N_EXPERTS, SUB), F32)],
        compiler_params=_cparams(1), name="router")(logits, rb_col)


SC_CORES, SC_SUBCORES = 2, 16
SC_WORKERS = SC_CORES * SC_SUBCORES
SC_LANES = 16
SC_CHUNK = 16
SC_SCAN = N_ASSIGN // SC_WORKERS


def _sc_mesh():
    return plsc.VectorSubcoreMesh(core_axis_name="c", subcore_axis_name="s",
                                  num_cores=SC_CORES, num_subcores=SC_SUBCORES)


def _sc_worker_base(per_w):
    return (lax.axis_index("s") * SC_CORES + lax.axis_index("c")) * per_w


def _sc_gather_rows(table_hbm, out_hbm, idx_v, rows_v, gsem, wsem, base, per_w):
    n_ch = per_w // SC_CHUNK
    assert n_ch % 2 == 0

    def gather(j, p):
        off = pl.multiple_of(j * SC_CHUNK, SC_CHUNK)
        return pltpu.make_async_copy(table_hbm.at[idx_v.at[pl.ds(off, SC_CHUNK)]], rows_v.at[p], gsem.at[p])

    def write(j, p):
        off = pl.multiple_of(j * SC_CHUNK, SC_CHUNK)
        return pltpu.make_async_copy(rows_v.at[p], out_hbm.at[pl.ds(base + off, SC_CHUNK)], wsem.at[p])

    gather(0, 0).start()

    @pl.loop(0, n_ch, step=2)
    def _(j0):
        for p in range(2):
            j = j0 + p
            gather(j, p).wait()

            @pl.when(j >= 1)
            def _():
                write(j - 1, 1 - p).wait()

            @pl.when(j + 1 < n_ch)
            def _():
                gather(j + 1, 1 - p).start()
            write(j, p).start()

    write(n_ch - 1, 1).wait()


_SC_ROW_SCRATCH = [pltpu.VMEM((2, SC_CHUNK) + ROW_TILE, jnp.int32),
                   pltpu.SemaphoreType.DMA((2,)), pltpu.SemaphoreType.DMA((2,))]


def _sc_gather(table, idx):
    n_out = idx.shape[0]
    per_w = n_out // SC_WORKERS
    assert per_w * SC_WORKERS == n_out and per_w % (2 * SC_CHUNK) == 0

    def body(table_hbm, idx_hbm, out_hbm, idx_v, rows_v, gsem, wsem):
        base = _sc_worker_base(per_w)
        pltpu.sync_copy(idx_hbm.at[pl.ds(base, per_w)], idx_v)
        _sc_gather_rows(table_hbm, out_hbm, idx_v, rows_v, gsem, wsem, base, per_w)

    return pl.kernel(
        body, out_type=jax.ShapeDtypeStruct((n_out,) + ROW_TILE, jnp.int32), mesh=_sc_mesh(),
        scratch_types=[pltpu.VMEM((per_w,), jnp.int32)] + _SC_ROW_SCRATCH, name="sc_gather")(table, idx)


def _sc_dispatch(table, slots):
    per_w = L_SLOTS // SC_WORKERS
    assert per_w * SC_WORKERS == L_SLOTS and per_w % (2 * SC_CHUNK) == 0 and SC_SCAN % SC_LANES == 0
    assert ZERO_ROWS & (ZERO_ROWS - 1) == 0

    def body(table_hbm, slots_hbm, out_hbm, idx_v, sl_v, rows_v, gsem, wsem):
        base = _sc_worker_base(per_w)
        lane = lax.iota(jnp.int32, SC_LANES)

        @pl.loop(0, per_w // SC_LANES)
        def _(j):
            off = pl.multiple_of(j * SC_LANES, SC_LANES)
            idx_v[pl.ds(off, SC_LANES)] = N + ((base + off + lane) & (ZERO_ROWS - 1))

        @pl.loop(0, N_ASSIGN // SC_SCAN)
        def _(c):
            pltpu.sync_copy(slots_hbm.at[pl.ds(pl.multiple_of(c * SC_SCAN, 8), SC_SCAN)], sl_v)

            @pl.loop(0, SC_SCAN // SC_LANES)
            def _(j):
                off = pl.multiple_of(j * SC_LANES, SC_LANES)
                loc = sl_v[pl.ds(off, SC_LANES)] - base
                mine = (loc >= 0) & (loc < per_w)
                tok = lax.shift_right_logical(c * SC_SCAN + off + lane, 3)
                plsc.store_scatter(idx_v, [jnp.where(mine, loc, 0)], tok, mask=mine)

        _sc_gather_rows(table_hbm, out_hbm, idx_v, rows_v, gsem, wsem, base, per_w)

    return pl.kernel(
        body, out_type=jax.ShapeDtypeStruct((L_SLOTS,) + ROW_TILE, jnp.int32), mesh=_sc_mesh(),
        scratch_types=[pltpu.VMEM((per_w,), jnp.int32), pltpu.VMEM((SC_SCAN,), jnp.int32)] + _SC_ROW_SCRATCH,
        compiler_params=pltpu.CompilerParams(needs_layout_passes=False),
        name="sc_dispatch")(table, slots)


def _experts_body(be_ref, first_ref, par_ref, next_ref, nused_ref, x_ref, w1_hbm, w3_hbm, w2_hbm, y_ref,
                  w1f, w3f, w2f, w1b, w3b, w2b, sem, *, layer):
    b = pl.program_id(0)
    used = b < nused_ref[0]

    def copies(e, slot):
        return (pltpu.make_async_copy(w1_hbm.at[layer, e], w1f.at[slot], sem.at[0, slot]),
                pltpu.make_async_copy(w3_hbm.at[layer, e], w3f.at[slot], sem.at[1, slot]),
                pltpu.make_async_copy(w2_hbm.at[layer, e], w2f.at[slot], sem.at[2, slot]))

    @pl.when(b == 0)
    def _():
        for c in copies(be_ref[0], 0):
            c.start()

    @pl.when(used & (first_ref[b] == 1))
    def _():
        slot = par_ref[b]
        for c in copies(be_ref[b], slot):
            c.wait()

        @pl.when(next_ref[b] >= 0)
        def _():
            for c in copies(next_ref[b], 1 - slot):
                c.start(priority=1)
        w1b[...] = w1f[slot].astype(BF16)
        w3b[...] = w3f[slot].astype(BF16)
        w2b[...] = w2f[slot].astype(BF16)

    @pl.when(used)
    def _():
        lo, hi = _unpack_bf16_pair(_load_row_tiles_2d(x_ref, EXP_BLOCK))
        lo = lo.astype(BF16)
        hi = hi.astype(BF16)
        half = D // 2
        h1 = (jnp.dot(lo, w1b[:half, :], preferred_element_type=F32)
              + jnp.dot(hi, w1b[half:, :], preferred_element_type=F32))
        h3 = (jnp.dot(lo, w3b[:half, :], preferred_element_type=F32)
              + jnp.dot(hi, w3b[half:, :], preferred_element_type=F32))
        hid = (_silu(h1) * h3).astype(BF16)
        y = jnp.dot(hid, w2b[...], preferred_element_type=F32)
        _store_row_tiles_2d(y_ref, _pack_bf16_pair(y[:, :half], y[:, half:]), EXP_BLOCK)

    @pl.when(jnp.logical_not(used))
    def _():
        y_ref[...] = jnp.zeros_like(y_ref)


def _experts(block_e, first, par, next_e, nused, xs, w1, w3, w2, layer):
    blk = lambda b, *refs: (jnp.minimum(b, refs[-1][0] - 1), 0)
    grid_spec = pltpu.PrefetchScalarGridSpec(
        num_scalar_prefetch=5,
        grid=(N_BLOCKS,),
        in_specs=[pl.BlockSpec((EXP_BLOCK * ROW_TILE[0], LANES), blk),
                  pl.BlockSpec(memory_space=pl.ANY), pl.BlockSpec(memory_space=pl.ANY),
                  pl.BlockSpec(memory_space=pl.ANY)],
        out_specs=pl.BlockSpec((EXP_BLOCK * ROW_TILE[0], LANES), lambda b, *refs: (b, 0)),
        scratch_shapes=[pltpu.VMEM((2, D, D_EXPERT), F32), pltpu.VMEM((2, D, D_EXPERT), F32),
                        pltpu.VMEM((2, D_EXPERT, D), F32),
                        pltpu.VMEM((D, D_EXPERT), BF16), pltpu.VMEM((D, D_EXPERT), BF16),
                        pltpu.VMEM((D_EXPERT, D), BF16), pltpu.SemaphoreType.DMA((3, 2))])
    return pl.pallas_call(
        functools.partial(_experts_body, layer=layer), grid_spec=grid_spec,
        out_shape=jax.ShapeDtypeStruct((L_SLOTS * ROW_TILE[0], LANES), jnp.int32),
        compiler_params=_cparams(1), name="experts",
    )(block_e, first, par, next_e, nused, xs.reshape(L_SLOTS * ROW_TILE[0], LANES), w1, w3, w2
      ).reshape((L_SLOTS,) + ROW_TILE)


TM_FFN = 640


def _ffn_pre_body(x_ref, g_ref, shp_ref, shs_ref, scp_ref, scs_ref, rw_ref, w13_ref, w2_ref,
                  sh_ref, lg_ref, hp_ref, hb_ref):
    i = pl.program_id(0)

    @pl.when(i < N // TM_FFN)
    def _():
        for sidx in range(TM_FFN // SUB):
            rows = pl.ds(sidx * SUB, SUB)
            h = _prenorm_rows(i * (TM_FFN // SUB) + sidx, x_ref[rows, :], g_ref, shp_ref, shs_ref, scp_ref, scs_ref)
            lg_ref[rows, :] = jnp.dot(h, rw_ref[...], precision=HIGHEST, preferred_element_type=F32)
            packed = _pack_bf16_pair(h[:, :D // 2], h[:, D // 2:])
            for c in range(ROW_TILE[0]):
                hp_ref[pl.ds(sidx * SUB * ROW_TILE[0] + c, SUB, stride=ROW_TILE[0]), :] = (
                    packed[:, c * LANES:(c + 1) * LANES])
            hb_ref[rows, :] = h.astype(BF16)
        up = jnp.dot(hb_ref[...], w13_ref[...], preferred_element_type=F32)
        hid = (_silu(up[:, :D_EXPERT]) * up[:, D_EXPERT:]).astype(BF16)
        sh_ref[...] = jnp.dot(hid, w2_ref[...], preferred_element_type=F32)

    @pl.when(i >= N // TM_FFN)
    def _():
        hp_ref[...] = jnp.zeros_like(hp_ref)


def _ffn_pre(x, g, mod, layer, rw, w13, w2):
    last = N // TM_FFN - 1
    row = lambda i: (jnp.minimum(i, last), 0)
    mspec = lambda part, rows, blk: pl.BlockSpec((None, rows, D), lambda i, l=layer, p=part, b=blk: (l, b, p))
    full = lambda shape: pl.BlockSpec(shape, lambda i: (0, 0))
    return pl.pallas_call(
        _ffn_pre_body,
        grid=(N // TM_FFN + 1,),
        in_specs=[pl.BlockSpec((TM_FFN, D), row), full((1, D)),
                  mspec(3, 8, 0), mspec(3, SUB, 1), mspec(4, 8, 0), mspec(4, SUB, 1),
                  full((D, LANES)), full((D, 2 * D_EXPERT)), full((D_EXPERT, D))],
        out_specs=[pl.BlockSpec((TM_FFN, D), row), pl.BlockSpec((TM_FFN, LANES), row),
                   pl.BlockSpec((TM_FFN * ROW_TILE[0], LANES), lambda i: (i, 0))],
        out_shape=[jax.ShapeDtypeStruct((N, D), F32), jax.ShapeDtypeStruct((N, LANES), F32),
                   jax.ShapeDtypeStruct(((N + TM_FFN) * ROW_TILE[0], LANES), jnp.int32)],
        scratch_shapes=[pltpu.VMEM((TM_FFN, D), BF16)],
        compiler_params=_cparams(1), name="ffn_pre")(x, g.reshape(1, D), mod, mod, mod, mod, rw, w13, w2)


def _combine_body(g_ref, wt_ref, sh_ref, x_ref, gn_ref, gp_ref, gs_ref, o_ref, f_ref):
    half = D // 2
    acc_lo = sh_ref[:, :half]
    acc_hi = sh_ref[:, half:]
    wt = wt_ref[...]
    per_tok = TOP_K * ROW_TILE[0]
    for k in range(TOP_K):
        packed = jnp.concatenate([g_ref[pl.ds(k * ROW_TILE[0] + c, SUB, stride=per_tok), :]
                                  for c in range(ROW_TILE[0])], axis=1)
        lo, hi = _unpack_bf16_pair(packed)
        w_c = wt[:, k:k + 1]
        acc_lo = acc_lo + w_c * lo
        acc_hi = acc_hi + w_c * hi
    f_ref[:, :half] = acc_lo
    f_ref[:, half:] = acc_hi
    _post_residual(pl.program_id(0), x_ref, lambda rows: f_ref[rows, :], gn_ref, gp_ref, gs_ref, o_ref, SUB)


def _combine(gathered, wts, shared, x, g_post, mod, layer):
    per_tok = TOP_K * ROW_TILE[0]
    return pl.pallas_call(
        _combine_body,
        grid=(N // SUB,),
        in_specs=[pl.BlockSpec((SUB * per_tok, LANES), lambda i: (i, 0)),
                  pl.BlockSpec((SUB, LANES), lambda i: (i, 0)),
                  pl.BlockSpec((SUB, D), lambda i: (i, 0)),
                  pl.BlockSpec((SUB, D), lambda i: (i, 0)),
                  pl.BlockSpec((1, D), lambda i: (0, 0)),
                  pl.BlockSpec((None, 8, D), lambda i, l=layer: (l, 0, 5)),
                  pl.BlockSpec((None, SUB, D), lambda i, l=layer: (l, 1, 5))],
        out_specs=pl.BlockSpec((SUB, D), lambda i: (i, 0)),
        out_shape=jax.ShapeDtypeStruct((N, D), F32),
        scratch_shapes=[pltpu.VMEM((SUB, D), F32)],
        compiler_params=_cparams(1), name="combine",
    )(gathered.reshape(N * per_tok, LANES), wts, shared, x, g_post.reshape(1, D), mod, mod)


def _reorder_w_in(w):
    return jnp.concatenate([w[:, :1536], w[:, 1552:]], axis=1), jnp.pad(w[:, 1536:1552], ((0, 0), (0, LANES - 16)))


def _rope_tables(pos):
    half = DK // 2
    inv = ROPE_BASE ** (-jnp.arange(half, dtype=F32) / half)
    ang = pos.astype(F32)[:, None] * inv[None, :]
    cos = jnp.cos(ang)
    sin = jnp.sin(ang)
    return jnp.concatenate([cos, cos], axis=1), jnp.concatenate([-sin, sin], axis=1)


def _layer(l, x, mod, s_gla, s_pool, s_ret, wts):
    (norm_mix_pre, norm_mix_post, norm_ffn_pre, norm_ffn_post, w_in, w_gla_gate, b_gla_gate, gla_norm,
     pool_w, pool_scale, ret_norm, sgu_norm, sgu_w, sgu_b, w_branch, w_merge_gate, b_merge_gate, w_out,
     router_w, router_bias, expert_w1, expert_w3, expert_w2, shared_w1, shared_w3, shared_w2) = wts

    w_main, w_low = _reorder_w_in(w_in[l])
    h, p_main = _inproj(x, norm_mix_pre[l], mod, l, w_main.astype(BF16))
    p_low = _matmul(h, w_low.astype(BF16), 1664, LANES, name="inproj_low")

    w_gate_pad = jnp.pad(w_gla_gate[l], ((0, LANES - 16), (0, 0)))
    b_gate = b_gla_gate[l].reshape(1, HEADS * DK)
    log_gamma = jnp.log1p(-jnp.exp2(-5.0 - jnp.arange(HEADS, dtype=F32)))
    dec_row = jnp.repeat(log_gamma, DK).reshape(1, HEADS * DK)
    cos_p, sin_p = _rope_tables(jnp.arange(T_P))
    cos_p = jnp.tile(cos_p, (1, 2))
    sin_p = jnp.tile(sin_p, (1, 2))
    g_gla = gla_norm[l].reshape(1, HEADS * DV)
    g_ret = ret_norm[l].reshape(1, HEADS * DV)

    oa_p, gla_p = _la_prompt(p_main, C_GQ, C_GK, C_GV, C_GR, p_low, p_low, w_gate_pad, b_gate, g_gla, False)
    oc_p, ret_p = _la_prompt(p_main, C_RQ, C_RK, C_RV, C_RG, cos_p, sin_p, dec_row, b_gate, g_ret, True)
    pw_bf = pool_w[l].astype(BF16)
    pscale = pool_scale[l].reshape(1, 512)
    ob_p = _pool_prompt(p_main, pw_bf, pscale)
    sgu_g = sgu_norm[l].reshape(1, 512)
    od_p = _sgu_prompt(p_main, sgu_g, sgu_w[l], jnp.pad(sgu_b[l].T, ((0, 0), (0, LANES - 4))))

    ps = p_main[N_P:]
    q_t = _to_tiles_t(ps[:, C_GQ:C_GQ + 256])
    k_t = _to_tiles_t(ps[:, C_GK:C_GK + 256])
    glow_t = jnp.pad(_to_tiles_t(p_low[N_P:, :16]), ((0, 0), (0, LANES - 16), (0, 0)))
    w_gate_t = jnp.pad(w_gla_gate[l].T, ((0, 0), (0, LANES - 16)))
    b_col = jnp.broadcast_to(b_gla_gate[l][:, None], (HEADS * DK, LANES))
    logit_t = _gate_logits_t(w_gate_t, glow_t, b_col)
    dummy = jnp.zeros((HEADS * DK, LANES), F32)
    oa_s, gla_s = _la_sample(q_t, k_t, logit_t, dummy, dummy, p_main, C_GV, C_GR, g_gla, s_gla[l], False)
    cos_s, sin_s = _rope_tables(jnp.full((1,), PAST_LEN))
    cos_c = jnp.broadcast_to(jnp.tile(cos_s[0], HEADS)[:, None], (HEADS * DK, LANES))
    sin_c = jnp.broadcast_to(jnp.tile(sin_s[0], HEADS)[:, None], (HEADS * DK, LANES))
    dec_c = jnp.broadcast_to(jnp.repeat(log_gamma, DK)[:, None], (HEADS * DK, LANES))
    rq_t = _to_tiles_t(ps[:, C_RQ:C_RQ + 256])
    rk_t = _to_tiles_t(ps[:, C_RK:C_RK + 256])
    oc_s, ret_s = _la_sample(rq_t, rk_t, dec_c, cos_c, sin_c, p_main, C_RV, C_RG, g_ret, s_ret[l], True)
    sgu_w0 = jnp.repeat(sgu_w[l][:, 0, 0], LANES).reshape(1, 512)
    sgu_b0 = jnp.repeat(sgu_b[l][:, 0], LANES).reshape(1, 512)
    ob_s, od_s, vn_s = _small_sample(p_main, s_pool[l], pw_bf, pscale, sgu_g, sgu_w0, sgu_b0)
    pool_p = p_main[:N_P, C_PIN:C_PIN + 512].reshape(B_P, T_P, 512)[:, T_P - POOL_BUF:]
    pool_s = jnp.concatenate([s_pool[l][:, 1:], ps[:, None, C_PIN:C_PIN + 512]], axis=1)

    branches = [jnp.concatenate([p, s.astype(BF16)], axis=0)
                for p, s in ((oa_p, oa_s), (ob_p, ob_s), (oc_p, oc_s), (od_p, od_s))]
    merged = _merge(h, branches, w_merge_gate[l].astype(BF16), b_merge_gate[l].reshape(1, 4 * D),
                    w_branch[l].astype(BF16))
    x = _outproj(merged, w_out[l].astype(BF16), x, norm_mix_post[l], mod, l)

    rw = jnp.pad(router_w[l], ((0, 0), (0, LANES - N_EXPERTS)))
    rb = jnp.broadcast_to(router_bias[l][:, None], (N_EXPERTS, SUB))
    w13 = jnp.concatenate([shared_w1[l], shared_w3[l]], axis=1).astype(BF16)
    shared, logits, h2_packed = _ffn_pre(x, norm_ffn_pre[l], mod, l, rw, w13, shared_w2[l].astype(BF16))
    eid, pos, wt, counts = _router(logits, rb)
    counts = counts[:, 0].astype(jnp.int32)
    padded = (counts + EXP_BLOCK - 1) // EXP_BLOCK * EXP_BLOCK
    pad_end = jnp.cumsum(padded)
    pad_start = pad_end - padded
    nused = (pad_end[-1] // EXP_BLOCK).astype(jnp.int32).reshape(1)
    blk_row = jnp.arange(N_BLOCKS, dtype=jnp.int32) * EXP_BLOCK
    block_e = jnp.minimum(jnp.sum((blk_row[:, None] >= pad_end[None, :]).astype(jnp.int32), axis=1),
                          N_EXPERTS - 1)
    first = jnp.concatenate([jnp.ones((1,), jnp.int32), (block_e[1:] != block_e[:-1]).astype(jnp.int32)])
    first = jnp.where(blk_row < pad_end[-1], first, 0)
    par = (jnp.cumsum(first) - 1) % 2
    live = jnp.where(padded > 0, jnp.arange(N_EXPERTS), N_EXPERTS)
    after = jnp.concatenate([lax.cummin(live, reverse=True)[1:], jnp.full((1,), N_EXPERTS)])
    next_e = jnp.where(after < N_EXPERTS, after, -1)[block_e].astype(jnp.int32)
    start_of = jnp.sum(jnp.where(eid[:, :, None] == jnp.arange(N_EXPERTS), pad_start.astype(jnp.int32), 0), axis=-1)
    slots = (start_of + pos).T.reshape(N_ASSIGN)
    wt = jnp.pad(wt.T, ((0, 0), (0, LANES - TOP_K)))
    xs = _sc_dispatch(h2_packed.reshape((N + TM_FFN,) + ROW_TILE), slots)
    ys = _experts(block_e, first, par.astype(jnp.int32), next_e, nused, xs, expert_w1, expert_w3, expert_w2, l)
    x = _combine(_sc_gather(ys, slots), wt, shared, x, norm_ffn_post[l], mod, l)
    return x, (gla_p, gla_s, pool_p, pool_s, ret_p, ret_s, vn_s)


def kernel(x_prompt, x_sample, c_prompt, c_sample, state_gla, state_pool, state_ret, w_ada, b_ada, norm_mix_pre, norm_mix_post, norm_ffn_pre, norm_ffn_post, w_in, w_gla_gate, b_gla_gate, gla_norm, pool_w, pool_scale, ret_norm, sgu_norm, sgu_w, sgu_b, w_branch, w_merge_gate, b_merge_gate, w_out, router_w, router_bias, expert_w1, expert_w3, expert_w2, shared_w1, shared_w3, shared_w2):
    wts = (norm_mix_pre, norm_mix_post, norm_ffn_pre, norm_ffn_post, w_in, w_gla_gate, b_gla_gate, gla_norm,
           pool_w, pool_scale, ret_norm, sgu_norm, sgu_w, sgu_b, w_branch, w_merge_gate, b_merge_gate, w_out,
           router_w, router_bias, expert_w1, expert_w3, expert_w2, shared_w1, shared_w3, shared_w2)
    c_all = jnp.zeros((MOD_ROWS, D), F32).at[:B_P].set(c_prompt).at[SUB:SUB + N_S].set(c_sample)
    mod = _ada(c_all, w_ada, b_ada)
    x = jnp.concatenate([x_prompt.reshape(N_P, D), x_sample.reshape(N_S, D)], axis=0)
    per_layer = []
    for l in range(DEPTH):
        x, states = _layer(l, x, mod, state_gla, state_pool, state_ret, wts)
        per_layer.append(states)
    gla_p, gla_s, pool_p, pool_s, ret_p, ret_s, vn_s = (jnp.stack(z) for z in zip(*per_layer))
    return (x[:N_P].reshape(B_P, T_P, D), x[N_P:].reshape(N_S, 1, D),
            gla_p, gla_s, pool_p, pool_s, ret_p, ret_s, vn_s.reshape(DEPTH, N_S, 1, 512))
```

```python
import functools

import jax
import jax.numpy as jnp
from jax import lax
from jax.experimental import pallas as pl
from jax.experimental.pallas import tpu as pltpu
from jax.experimental.pallas import tpu_sc as plsc

F32 = jnp.float32
BF16 = jnp.bfloat16
HIGHEST = lax.Precision.HIGHEST

D = 2048
B_P, T_P = 4, 2048
N_P = B_P * T_P
N_S = 128
N = N_P + N_S
DEPTH = 2
PAST_LEN = 16384
EPS = 1e-6
HEADS, DK, DV = 4, 64, 128
CHUNK = 64
GATE_TEMP = 16.0
POOL_WINDOWS = (2, 4, 8, 16)
POOL_BUF = 15
ROPE_BASE = 10000.0
N_EXPERTS = 64
TOP_K = 8
D_EXPERT = 512
ROUTED_SCALE = 2.5

LANES = 128
SUB = 128
MOD_ROWS = 256
EXP_BLOCK = 256
N_ASSIGN = N * TOP_K
N_BLOCKS = -(-(N_ASSIGN + N_EXPERTS * (EXP_BLOCK - 1)) // EXP_BLOCK)
L_SLOTS = N_BLOCKS * EXP_BLOCK
VMEM_LIMIT = 56 * 1024 * 1024

C_GQ, C_GK, C_GV, C_GR, C_PIN, C_RQ, C_RK, C_RV, C_RG, C_SU, C_SV = (
    0, 256, 512, 1024, 1536, 2048, 2304, 2560, 3072, 3584, 4096)
P_MAIN = 4608


def _cparams(n_axes=1):
    return pltpu.CompilerParams(dimension_semantics=("arbitrary",) * n_axes,
                                vmem_limit_bytes=VMEM_LIMIT)


def _silu(x):
    return x * jax.nn.sigmoid(x)


def _mod_rows(t, mp_ref, ms_ref):
    b = jnp.minimum(t // (T_P // SUB), B_P - 1)
    return jnp.where(t >= N_P // SUB, ms_ref[...], mp_ref[pl.ds(b, 1), :])


def _mod_specs(layer, part):
    return [pl.BlockSpec((None, 8, D), lambda i, l=layer, p=part: (l, 0, p)),
            pl.BlockSpec((None, SUB, D), lambda i, l=layer, p=part: (l, 1, p))]


def _pack_bf16_pair(lo, hi):
    lo_u = lax.bitcast_convert_type(lo.astype(BF16).astype(F32), jnp.uint32)
    hi_u = lax.bitcast_convert_type(hi.astype(BF16).astype(F32), jnp.uint32)
    return lax.bitcast_convert_type((hi_u & jnp.uint32(0xFFFF0000)) | (lo_u >> 16), jnp.int32)


def _unpack_bf16_pair(w):
    u = lax.bitcast_convert_type(w, jnp.uint32)
    lo = lax.bitcast_convert_type(u << 16, F32)
    hi = lax.bitcast_convert_type(u & jnp.uint32(0xFFFF0000), F32)
    return lo, hi


ROW_TILE = (8, LANES)


def _load_row_tiles(ref):
    return jnp.concatenate([ref[:, c, :] for c in range(ROW_TILE[0])], axis=1)


def _store_row_tiles(ref, val):
    for c in range(ROW_TILE[0]):
        ref[:, c, :] = val[:, c * LANES:(c + 1) * LANES]


def _load_row_tiles_2d(ref, rows):
    return jnp.concatenate([ref[pl.ds(c, rows, stride=ROW_TILE[0]), :] for c in range(ROW_TILE[0])], axis=1)


def _store_row_tiles_2d(ref, val, rows):
    for c in range(ROW_TILE[0]):
        ref[pl.ds(c, rows, stride=ROW_TILE[0]), :] = val[:, c * LANES:(c + 1) * LANES]


def _ada_body(c_ref, w_ref, b_ref, o_ref):
    s = _silu(c_ref[...]).astype(BF16)
    o_ref[...] = jnp.dot(s, w_ref[...].astype(BF16), preferred_element_type=F32) + b_ref[...]


def _ada(c_all, w_ada, b_ada):
    tn = 1024
    return pl.pallas_call(
        _ada_body,
        grid=(DEPTH, 6 * D // tn),
        in_specs=[pl.BlockSpec((MOD_ROWS, D), lambda l, j: (0, 0)),
                  pl.BlockSpec((None, D, tn), lambda l, j: (l, 0, j)),
                  pl.BlockSpec((None, 1, tn), lambda l, j: (l, 0, j))],
        out_specs=pl.BlockSpec((None, MOD_ROWS, tn), lambda l, j: (l, 0, j)),
        out_shape=jax.ShapeDtypeStruct((DEPTH, MOD_ROWS, 6 * D), F32),
        compiler_params=_cparams(2), name="ada")(c_all, w_ada, b_ada.reshape(DEPTH, 1, 6 * D))


ZERO_ROWS = 2 * SUB


def _prenorm_rows(t, x, g_ref, shp_ref, shs_ref, scp_ref, scs_ref):
    y = x * lax.rsqrt(jnp.mean(x * x, axis=-1, keepdims=True) + EPS) * g_ref[...]
    return y * (1.0 + _mod_rows(t, scp_ref, scs_ref)) + _mod_rows(t, shp_ref, shs_ref)


TM_NORM = 640


def _prenorm_body(x_ref, g_ref, shp_ref, shs_ref, scp_ref, scs_ref, h_ref):
    for sidx in range(TM_NORM // SUB):
        rows = pl.ds(sidx * SUB, SUB)
        t = pl.program_id(0) * (TM_NORM // SUB) + sidx
        h_ref[rows, :] = _prenorm_rows(t, x_ref[rows, :], g_ref, shp_ref, shs_ref, scp_ref, scs_ref).astype(BF16)


def _prenorm(x, g, mod, layer):
    return pl.pallas_call(
        _prenorm_body,
        grid=(N // TM_NORM,),
        in_specs=[pl.BlockSpec((TM_NORM, D), lambda i: (i, 0)), pl.BlockSpec((1, D), lambda i: (0, 0))]
        + _mod_specs(layer, 0) + _mod_specs(layer, 1),
        out_specs=pl.BlockSpec((TM_NORM, D), lambda i: (i, 0)),
        out_shape=jax.ShapeDtypeStruct((N, D), BF16),
        compiler_params=_cparams(1), name="prenorm")(x, g.reshape(1, D), mod, mod, mod, mod)


def _mm_body(x_ref, w_ref, o_ref):
    o_ref[...] = jnp.dot(x_ref[...], w_ref[...], preferred_element_type=F32).astype(o_ref.dtype)


def _matmul(x, w, tm, tn, out_dtype=F32, name="mm"):
    m, k = x.shape
    n = w.shape[1]
    return pl.pallas_call(
        _mm_body,
        grid=(m // tm, n // tn),
        in_specs=[pl.BlockSpec((tm, k), lambda i, j: (i, 0)),
                  pl.BlockSpec((k, tn), lambda i, j: (0, j))],
        out_specs=pl.BlockSpec((tm, tn), lambda i, j: (i, j)),
        out_shape=jax.ShapeDtypeStruct((m, n), out_dtype),
        compiler_params=_cparams(2), name=name)(x, w)


ROWS_LA = 256


def _swap_halves_lanes(x):
    lane = lax.broadcasted_iota(jnp.int32, x.shape, 1)
    return jnp.where((lane % 64) < 32, pltpu.roll(x, 96, 1), pltpu.roll(x, 32, 1))


def _rope_lanes(x, cos, sin_signed):
    parts = []
    for half in range(2):
        xh = x[:, half * LANES:(half + 1) * LANES]
        parts.append(xh * cos + _swap_halves_lanes(xh) * sin_signed)
    return jnp.concatenate(parts, axis=1)


def _la_prompt_body(q_ref, k_ref, v_ref, r_ref, aux_ref, aux2_ref, dec_ref, bias_ref, g_ref,
                    o_ref, st_out_ref, st_ref, *, retention):
    t = pl.program_id(1)

    @pl.when(t == 0)
    def _():
        st_ref[...] = jnp.zeros_like(st_ref)

    ri = lax.broadcasted_iota(jnp.int32, (CHUNK, CHUNK), 0)
    ci = lax.broadcasted_iota(jnp.int32, (CHUNK, CHUNK), 1)
    causal = ri >= ci
    tril = causal.astype(F32)
    scale = DK ** -0.5

    for c in range(ROWS_LA // CHUNK):
        rows = pl.ds(c * CHUNK, CHUNK)
        q = q_ref[rows, :]
        k = k_ref[rows, :]
        v = v_ref[rows, :]
        if retention:
            cos = aux_ref[rows, :]
            sin = aux2_ref[rows, :]
            q = _rope_lanes(q, cos, sin)
            k = _rope_lanes(k, cos, sin) * scale
            la = jnp.broadcast_to(dec_ref[...], (CHUNK, HEADS * DK))
        else:
            q = q * scale
            logit = jnp.dot(aux_ref[rows, :], dec_ref[...], precision=HIGHEST,
                            preferred_element_type=F32) + bias_ref[...]
            la = jax.nn.log_sigmoid(logit) / GATE_TEMP
        bc = jnp.dot(tril, la, precision=HIGHEST, preferred_element_type=F32)
        bl = bc[CHUNK - 1:CHUNK, :]
        qd = q * jnp.exp(bc)
        ki = k * jnp.exp(-bc)
        ke = k * jnp.exp(bl - bc)
        ac = jnp.exp(bl)
        outs = []
        for h in range(HEADS):
            ks = slice(h * DK, (h + 1) * DK)
            vs = slice(h * DV, (h + 1) * DV)
            qd_h = qd[:, ks].astype(BF16)
            ki_h = ki[:, ks].astype(BF16)
            ke_h = ke[:, ks].astype(BF16)
            v_h = v[:, vs].astype(BF16)
            sc = lax.dot_general(qd_h, ki_h, (((1,), (1,)), ((), ())), preferred_element_type=F32)
            sc = jnp.where(causal, sc, 0.0)
            o_h = jnp.dot(sc.astype(BF16), v_h, preferred_element_type=F32)
            st = st_ref[h]
            o_h = o_h + lax.dot_general(qd_h, st.astype(BF16), (((1,), (1,)), ((), ())),
                                        preferred_element_type=F32)
            kv_t = lax.dot_general(v_h, ke_h, (((0,), (0,)), ((), ())), preferred_element_type=F32)
            st_ref[h] = st * ac[:, ks] + kv_t
            o_n = o_h * lax.rsqrt(jnp.mean(o_h * o_h, axis=-1, keepdims=True) + EPS) * g_ref[:, vs]
            outs.append(o_n)
        o = jnp.concatenate(outs, axis=1) * _silu(r_ref[rows, :])
        o_ref[rows, :] = o.astype(BF16)

    st_out_ref[...] = st_ref[...]


def _la_prompt(p_main, cq, ck, cv, cr, aux, aux2, dec, bias, g, retention):
    nt = T_P // ROWS_LA
    rowblk = lambda b, t: b * nt + t
    if retention:
        aux_specs = [pl.BlockSpec((ROWS_LA, LANES), lambda b, t: (t, 0)),
                     pl.BlockSpec((ROWS_LA, LANES), lambda b, t: (t, 0))]
    else:
        aux_specs = [pl.BlockSpec((ROWS_LA, LANES), lambda b, t: (rowblk(b, t), 0)),
                     pl.BlockSpec((8, LANES), lambda b, t: (0, 0))]
    o, st = pl.pallas_call(
        functools.partial(_la_prompt_body, retention=retention),
        grid=(B_P, nt),
        in_specs=[pl.BlockSpec((ROWS_LA, 256), lambda b, t: (rowblk(b, t), cq // 256)),
                  pl.BlockSpec((ROWS_LA, 256), lambda b, t: (rowblk(b, t), ck // 256)),
                  pl.BlockSpec((ROWS_LA, 512), lambda b, t: (rowblk(b, t), cv // 512)),
                  pl.BlockSpec((ROWS_LA, 512), lambda b, t: (rowblk(b, t), cr // 512))]
        + aux_specs
        + [pl.BlockSpec(dec.shape, lambda b, t: (0, 0)),
           pl.BlockSpec((1, HEADS * DK), lambda b, t: (0, 0)),
           pl.BlockSpec((1, HEADS * DV), lambda b, t: (0, 0))],
        out_specs=[pl.BlockSpec((ROWS_LA, HEADS * DV), lambda b, t: (rowblk(b, t), 0)),
                   pl.BlockSpec((None, HEADS, DV, DK), lambda b, t: (b, 0, 0, 0))],
        out_shape=[jax.ShapeDtypeStruct((N_P, HEADS * DV), BF16),
                   jax.ShapeDtypeStruct((B_P, HEADS, DV, DK), F32)],
        scratch_shapes=[pltpu.VMEM((HEADS, DV, DK), F32)],
        compiler_params=_cparams(2), name="ret_prompt" if retention else "gla_prompt",
    )(p_main, p_main, p_main, p_main, aux, aux2, dec, bias, g)
    return o, jnp.swapaxes(st, -1, -2)


SAMPLE_TILE = 8


def _la_sample_body(qt_ref, kt_ref, lt_ref, cos_ref, sin_ref, v_ref, r_ref, g_ref, s_ref,
                    o_ref, s_out_ref, *, retention):
    scale = DK ** -0.5
    qt = qt_ref[...]
    kt = kt_ref[...]
    if retention:
        def rope(x):
            sw = jnp.concatenate(
                [x[h * DK + (DK // 2) * (1 - j): h * DK + (DK // 2) * (2 - j), :]
                 for h in range(HEADS) for j in range(2)], axis=0)
            return x * cos_ref[...] + sw * sin_ref[...]
        qt = rope(qt)
        kt = rope(kt) * scale
        la = lt_ref[...]
    else:
        qt = qt * scale
        la = jax.nn.log_sigmoid(lt_ref[...]) / GATE_TEMP
    at = jnp.exp(la)
    qd = qt * at
    ki = kt * jnp.exp(-la)
    prod = qd * ki
    v8 = v_ref[...]
    r8 = r_ref[...]
    g = g_ref[...]
    for j in range(SAMPLE_TILE):
        for h in range(HEADS):
            ks = slice(h * DK, (h + 1) * DK)
            vs = slice(h * DV, (h + 1) * DV)
            a_c = jnp.broadcast_to(at[ks, j:j + 1], (DK, DV))
            k_c = jnp.broadcast_to(kt[ks, j:j + 1], (DK, DV))
            q_c = jnp.broadcast_to(qd[ks, j:j + 1], (DK, DV))
            s_c = jnp.broadcast_to(jnp.sum(prod[ks, j:j + 1], axis=0, keepdims=True), (1, DV))
            s0 = s_ref[j, h]
            v_row = v8[j:j + 1, vs]
            s_out_ref[j, h] = a_c * s0 + k_c * v_row
            o_row = s_c * v_row + jnp.sum(q_c * s0, axis=0, keepdims=True)
            o_n = o_row * lax.rsqrt(jnp.mean(o_row * o_row, axis=-1, keepdims=True) + EPS) * g[:, vs]
            o_ref[j:j + 1, vs] = o_n * _silu(r8[j:j + 1, vs])


def _la_sample(qt, kt, lt, cos_t, sin_t, p_main, cv, cr, g, s0, retention):
    nt = N_S // SAMPLE_TILE
    row0 = N_P // SAMPLE_TILE
    tile = pl.BlockSpec((None, HEADS * DK, LANES), lambda i: (i, 0, 0))
    full = pl.BlockSpec((HEADS * DK, LANES), lambda i: (0, 0))
    lt_spec = full if retention else tile
    return pl.pallas_call(
        functools.partial(_la_sample_body, retention=retention),
        grid=(nt,),
        in_specs=[tile, tile, lt_spec, full, full,
                  pl.BlockSpec((SAMPLE_TILE, 512), lambda i: (row0 + i, cv // 512)),
                  pl.BlockSpec((SAMPLE_TILE, 512), lambda i: (row0 + i, cr // 512)),
                  pl.BlockSpec((1, HEADS * DV), lambda i: (0, 0)),
                  pl.BlockSpec((SAMPLE_TILE, HEADS, DK, DV), lambda i: (i, 0, 0, 0))],
        out_specs=[pl.BlockSpec((SAMPLE_TILE, HEADS * DV), lambda i: (i, 0)),
                   pl.BlockSpec((SAMPLE_TILE, HEADS, DK, DV), lambda i: (i, 0, 0, 0))],
        out_shape=[jax.ShapeDtypeStruct((N_S, HEADS * DV), F32),
                   jax.ShapeDtypeStruct((N_S, HEADS, DK, DV), F32)],
        compiler_params=_cparams(1), name="ret_sample" if retention else "gla_sample",
    )(qt, kt, lt, cos_t, sin_t, p_main, p_main, g, s0)


def _gate_logits_t_body(w_ref, x_ref, b_ref, o_ref):
    o_ref[...] = jnp.dot(w_ref[...], x_ref[...], precision=HIGHEST, preferred_element_type=F32) + b_ref[...]


def _gate_logits_t(w_gate_t, glow_t, b_col):
    nt = N_S // SAMPLE_TILE
    return pl.pallas_call(
        _gate_logits_t_body,
        grid=(nt,),
        in_specs=[pl.BlockSpec((HEADS * DK, LANES), lambda i: (0, 0)),
                  pl.BlockSpec((None, LANES, LANES), lambda i: (i, 0, 0)),
                  pl.BlockSpec((HEADS * DK, LANES), lambda i: (0, 0))],
        out_specs=pl.BlockSpec((None, HEADS * DK, LANES), lambda i: (i, 0, 0)),
        out_shape=jax.ShapeDtypeStruct((nt, HEADS * DK, LANES), F32),
        compiler_params=_cparams(1), name="gate_logits_t")(w_gate_t, glow_t, b_col)


def _to_tiles_t(x):
    c = x.shape[1]
    xt = jnp.swapaxes(x.reshape(N_S // SAMPLE_TILE, SAMPLE_TILE, c), 1, 2)
    return jnp.pad(xt, ((0, 0), (0, 0), (0, LANES - SAMPLE_TILE)))


ROWS_POOL = 512


def _pool_mix(y, w_ref, sc_ref):
    outs = []
    for gi in range(4):
        cs = slice(gi * LANES, (gi + 1) * LANES)
        outs.append(jnp.dot(y[:, cs].astype(BF16), w_ref[gi], preferred_element_type=F32))
    return jnp.concatenate(outs, axis=1) * sc_ref[...]


def _pool_prompt_body(p_ref, halo_ref, w_ref, sc_ref, o_ref):
    t = pl.program_id(1)
    p = p_ref[...]
    halo = jnp.where(t == 0, 0.0, halo_ref[...])
    full = jnp.concatenate([halo, p], axis=0)
    pos = t * ROWS_POOL + lax.broadcasted_iota(jnp.int32, (ROWS_POOL, LANES), 0)
    means = []
    for gi, w in enumerate(POOL_WINDOWS):
        s = full[:, gi * LANES:(gi + 1) * LANES]
        step = 1
        while step < w:
            s = s + pltpu.roll(s, step, 0)
            step *= 2
        win = s[16:, :]
        cnt = jnp.minimum(w, pos + 1).astype(F32)
        means.append(win / cnt)
    y = jnp.concatenate(means, axis=1) - p
    o_ref[...] = _pool_mix(y, w_ref, sc_ref).astype(BF16)


def _pool_prompt(p_main, w_bf, scale):
    nt = T_P // ROWS_POOL
    return pl.pallas_call(
        _pool_prompt_body,
        grid=(B_P, nt),
        in_specs=[pl.BlockSpec((ROWS_POOL, 512), lambda b, t: (b * nt + t, C_PIN // 512)),
                  pl.BlockSpec((16, 512), lambda b, t: (jnp.maximum((b * nt + t) * (ROWS_POOL // 16) - 1, 0),
                                                        C_PIN // 512)),
                  pl.BlockSpec((4, LANES, LANES), lambda b, t: (0, 0, 0)),
                  pl.BlockSpec((1, 512), lambda b, t: (0, 0))],
        out_specs=pl.BlockSpec((ROWS_POOL, 512), lambda b, t: (b * nt + t, 0)),
        out_shape=jax.ShapeDtypeStruct((N_P, 512), BF16),
        compiler_params=_cparams(2), name="pool_prompt")(p_main, p_main, w_bf, scale)


def _small_sample_body(p_ref, buf_ref, pw_ref, psc_ref, u_ref, sv_ref, sg_ref, sw_ref, sb_ref,
                       ob_ref, od_ref, vn_ref):
    p = p_ref[...]
    means = []
    for gi, w in enumerate(POOL_WINDOWS):
        cs = slice(gi * LANES, (gi + 1) * LANES)
        s = p[:, cs]
        for j in range(1, w):
            s = s + buf_ref[:, POOL_BUF - j, cs]
        means.append(s / float(min(w, PAST_LEN + 1)))
    y = jnp.concatenate(means, axis=1) - p
    ob_ref[...] = _pool_mix(y, pw_ref, psc_ref)
    sv = sv_ref[...]
    vn = sv * lax.rsqrt(jnp.mean(sv * sv, axis=-1, keepdims=True) + EPS) * sg_ref[...]
    vn_ref[...] = vn
    od_ref[...] = u_ref[...] * (sw_ref[...] * vn + sb_ref[...])


def _small_sample(p_main, buf, pw_bf, pscale, sgu_g, sgu_w0, sgu_b0):
    row = N_P // N_S
    col = lambda c: pl.BlockSpec((N_S, 512), lambda i, c=c: (row, c // 512))
    vec = pl.BlockSpec((1, 512), lambda i: (0, 0))
    return pl.pallas_call(
        _small_sample_body,
        grid=(1,),
        in_specs=[col(C_PIN), pl.BlockSpec((N_S, POOL_BUF, 512), lambda i: (0, 0, 0)),
                  pl.BlockSpec((4, LANES, LANES), lambda i: (0, 0, 0)), vec,
                  col(C_SU), col(C_SV), vec, vec, vec],
        out_specs=[pl.BlockSpec((N_S, 512), lambda i: (0, 0))] * 3,
        out_shape=[jax.ShapeDtypeStruct((N_S, 512), F32)] * 3,
        compiler_params=_cparams(1), name="small_sample",
    )(p_main, buf, pw_bf, pscale, p_main, p_main, sgu_g, sgu_w0, sgu_b0)


ROWS_SGU = 512
SGU_CHUNK = 128


def _sgu_prompt_body(u_ref, v_ref, g_ref, w_ref, bt_ref, o_ref):
    ri = lax.broadcasted_iota(jnp.int32, (SGU_CHUNK, SGU_CHUNK), 0)
    ci = lax.broadcasted_iota(jnp.int32, (SGU_CHUNK, SGU_CHUNK), 1)
    causal = ri >= ci
    for c in range(ROWS_SGU // SGU_CHUNK):
        rows = pl.ds(c * SGU_CHUNK, SGU_CHUNK)
        v = v_ref[rows, :]
        vn = (v * lax.rsqrt(jnp.mean(v * v, axis=-1, keepdims=True) + EPS) * g_ref[...]).astype(BF16)
        outs = []
        for gi in range(4):
            cs = slice(gi * LANES, (gi + 1) * LANES)
            w = jnp.where(causal, w_ref[gi], 0.0).astype(BF16)
            mixed = jnp.dot(w, vn[:, cs], preferred_element_type=F32)
            outs.append(mixed + jnp.broadcast_to(bt_ref[:, gi:gi + 1], (SGU_CHUNK, LANES)))
        o_ref[rows, :] = (u_ref[rows, :] * jnp.concatenate(outs, axis=1)).astype(BF16)


def _sgu_prompt(p_main, g, w, b_t):
    return pl.pallas_call(
        _sgu_prompt_body,
        grid=(N_P // ROWS_SGU,),
        in_specs=[pl.BlockSpec((ROWS_SGU, 512), lambda i: (i, C_SU // 512)),
                  pl.BlockSpec((ROWS_SGU, 512), lambda i: (i, C_SV // 512)),
                  pl.BlockSpec((1, 512), lambda i: (0, 0)),
                  pl.BlockSpec((4, SGU_CHUNK, SGU_CHUNK), lambda i: (0, 0, 0)),
                  pl.BlockSpec((SGU_CHUNK, LANES), lambda i: (0, 0))],
        out_specs=pl.BlockSpec((ROWS_SGU, 512), lambda i: (i, 0)),
        out_shape=jax.ShapeDtypeStruct((N_P, 512), BF16),
        compiler_params=_cparams(1), name="sgu_prompt")(p_main, p_main, g, w, b_t)


TM_MERGE = 640
TN_MERGE = 512


def _merge_body(h_ref, ba_ref, bb_ref, bc_ref, bd_ref, g0, g1, g2, g3, u0, u1, u2, u3,
                c0, c1, c2, c3, o_ref):
    h = h_ref[...]
    acc = None
    for br, gw, uw, gb in ((ba_ref, g0, u0, c0), (bb_ref, g1, u1, c1), (bc_ref, g2, u2, c2), (bd_ref, g3, u3, c3)):
        gate = jax.nn.sigmoid(jnp.dot(h, gw[...], preferred_element_type=F32) + gb[...])
        up = jnp.dot(br[...], uw[...], preferred_element_type=F32)
        acc = gate * up if acc is None else acc + gate * up
    o_ref[...] = acc.astype(BF16)


def _merge(h, branches, w_mg, b_mg, w_br):
    nj = D // TN_MERGE
    row = lambda w: pl.BlockSpec((TM_MERGE, w), lambda i, j: (i, 0))
    gate_w = [pl.BlockSpec((D, TN_MERGE), lambda i, j, b=b: (0, b * nj + j)) for b in range(4)]
    up_w = [pl.BlockSpec((None, 512, TN_MERGE), lambda i, j, b=b: (b, 0, j)) for b in range(4)]
    gate_b = [pl.BlockSpec((1, TN_MERGE), lambda i, j, b=b: (0, b * nj + j)) for b in range(4)]
    return pl.pallas_call(
        _merge_body,
        grid=(N // TM_MERGE, nj),
        in_specs=[row(D)] + [row(512)] * 4 + gate_w + up_w + gate_b,
        out_specs=pl.BlockSpec((TM_MERGE, TN_MERGE), lambda i, j: (i, j)),
        out_shape=jax.ShapeDtypeStruct((N, D), BF16),
        compiler_params=_cparams(2), name="merge",
    )(h, *branches, w_mg, w_mg, w_mg, w_mg, w_br, w_br, w_br, w_br, b_mg, b_mg, b_mg, b_mg)


TM_OUT = 640
TN_OUT = 512


def _post_residual(t0, x_ref, get_y, gn_ref, gp_ref, gs_ref, o_ref, rows_total):
    for s in range(rows_total // SUB):
        rows = pl.ds(s * SUB, SUB)
        y = get_y(rows)
        yn = y * lax.rsqrt(jnp.mean(y * y, axis=-1, keepdims=True) + EPS) * gn_ref[...]
        o_ref[rows, :] = x_ref[rows, :] + _mod_rows(t0 + s, gp_ref, gs_ref) * yn


def _outproj_body(m_ref, w_ref, x_ref, gn_ref, gp_ref, gs_ref, o_ref, acc_ref):
    j = pl.program_id(1)
    acc_ref[j] = jnp.dot(m_ref[...], w_ref[...], preferred_element_type=F32)

    @pl.when(j == D // TN_OUT - 1)
    def _():
        get_y = lambda rows: jnp.concatenate([acc_ref[c, rows, :] for c in range(D // TN_OUT)], axis=1)
        _post_residual(pl.program_id(0) * (TM_OUT // SUB), x_ref, get_y, gn_ref, gp_ref, gs_ref,
                       o_ref, TM_OUT)


def _outproj(merged, w_out, x, g_post, mod, layer):
    mspec = [pl.BlockSpec((None, 8, D), lambda i, j, l=layer: (l, 0, 2)),
             pl.BlockSpec((None, SUB, D), lambda i, j, l=layer: (l, 1, 2))]
    return pl.pallas_call(
        _outproj_body,
        grid=(N // TM_OUT, D // TN_OUT),
        in_specs=[pl.BlockSpec((TM_OUT, D), lambda i, j: (i, 0)),
                  pl.BlockSpec((D, TN_OUT), lambda i, j: (0, j)),
                  pl.BlockSpec((TM_OUT, D), lambda i, j: (i, 0)),
                  pl.BlockSpec((1, D), lambda i, j: (0, 0))] + mspec,
        out_specs=pl.BlockSpec((TM_OUT, D), lambda i, j: (i, 0)),
        out_shape=jax.ShapeDtypeStruct((N, D), F32),
        scratch_shapes=[pltpu.VMEM((D // TN_OUT, TM_OUT, TN_OUT), F32)],
        compiler_params=_cparams(2), name="outproj",
    )(merged, w_out, x, g_post.reshape(1, D), mod, mod)


def _router_body(lg_ref, b_ref, eid_ref, pos_ref, wt_ref, cnt_ref, run_ref):
    i = pl.program_id(0)

    @pl.when(i == 0)
    def _():
        run_ref[...] = jnp.zeros_like(run_ref)

    ng, gs = 8, N_EXPERTS // 8
    neg = -jnp.inf
    scores = jax.nn.sigmoid(lg_ref[...].T[:N_EXPERTS, :])
    sel = scores + b_ref[...]
    sel3 = sel.reshape(ng, gs, SUB)
    sub3 = lax.broadcasted_iota(jnp.int32, (ng, gs, SUB), 1)
    gmax = jnp.max(sel3, axis=1, keepdims=True)
    first = jnp.min(jnp.where(sel3 == gmax, sub3, gs), axis=1, keepdims=True)
    gmax2 = jnp.max(jnp.where(sub3 == first, neg, sel3), axis=1, keepdims=True)
    gscore = (gmax + gmax2).reshape(ng, SUB)
    gidx = lax.broadcasted_iota(jnp.int32, (ng, SUB), 0)
    grank = jnp.zeros((ng, SUB), jnp.int32)
    for s in range(1, ng):
        other = pltpu.roll(gscore, s, 0)
        lower = gidx >= s
        grank += ((other > gscore) | ((other == gscore) & lower)).astype(jnp.int32)
    keep = jnp.broadcast_to((grank < 4).reshape(ng, 1, SUB), (ng, gs, SUB))
    masked = jnp.where(keep, sel3, neg).reshape(N_EXPERTS, SUB)
    eidx = lax.broadcasted_iota(jnp.int32, (N_EXPERTS, SUB), 0)
    rank = jnp.zeros((N_EXPERTS, SUB), jnp.int32)
    for s in range(1, N_EXPERTS):
        other = pltpu.roll(masked, s, 0)
        lower = eidx >= s
        rank += ((other > masked) | ((other == masked) & lower)).astype(jnp.int32)
    chosen = rank < TOP_K
    w_sel = jnp.where(chosen, scores, 0.0)
    w_sel = w_sel / jnp.sum(w_sel, axis=0, keepdims=True) * ROUTED_SCALE
    ri = lax.broadcasted_iota(jnp.int32, (SUB, SUB), 0)
    ci = lax.broadcasted_iota(jnp.int32, (SUB, SUB), 1)
    onehot = chosen.astype(BF16)
    pos = jnp.dot(onehot, (ri < ci).astype(BF16), preferred_element_type=F32) + run_ref[...]
    run_ref[...] = run_ref[...] + jnp.sum(chosen.astype(F32), axis=1, keepdims=True)
    cnt_ref[...] = run_ref[...]
    eidx_f = eidx.astype(F32)
    rows_e, rows_p, rows_w = [], [], []
    for kk in range(TOP_K):
        m = chosen & (rank == kk)
        rows_e.append(jnp.sum(jnp.where(m, eidx_f, 0.0), axis=0, keepdims=True))
        rows_p.append(jnp.sum(jnp.where(m, pos, 0.0), axis=0, keepdims=True))
        rows_w.append(jnp.sum(jnp.where(m, w_sel, 0.0), axis=0, keepdims=True))
    eid_ref[...] = jnp.concatenate(rows_e, axis=0).astype(jnp.int32)
    pos_ref[...] = jnp.concatenate(rows_p, axis=0).astype(jnp.int32)
    wt_ref[...] = jnp.concatenate(rows_w, axis=0)


def _router(logits, rb_col):
    tile = pl.BlockSpec((TOP_K, SUB), lambda i: (0, i))
    return pl.pallas_call(
        _router_body,
        grid=(N // SUB,),
        in_specs=[pl.BlockSpec((SUB, LANES), lambda i: (i, 0)),
                  pl.BlockSpec((N_EXPERTS, SUB), lambda i: (0, 0))],
        out_specs=[tile, tile, tile, pl.BlockSpec((N_EXPERTS, SUB), lambda i: (0, 0))],
        out_shape=[jax.ShapeDtypeStruct((TOP_K, N), jnp.int32), jax.ShapeDtypeStruct((TOP_K, N), jnp.int32),
                   jax.ShapeDtypeStruct((TOP_K, N), F32), jax.ShapeDtypeStruct((N_EXPERTS, SUB), F32)],
        scratch_shapes=[pltpu.VMEM((N_EXPERTS, SUB), F32)],
        compiler_params=_cparams(1), name="router")(logits, rb_col)


SC_CORES, SC_SUBCORES = 2, 16
SC_WORKERS = SC_CORES * SC_SUBCORES
SC_LANES = 16
SC_CHUNK = 16
SC_SCAN = N_ASSIGN // SC_WORKERS


def _sc_mesh():
    return plsc.VectorSubcoreMesh(core_axis_name="c", subcore_axis_name="s",
                                  num_cores=SC_CORES, num_subcores=SC_SUBCORES)


def _sc_worker_base(per_w):
    return (lax.axis_index("s") * SC_CORES + lax.axis_index("c")) * per_w


def _sc_gather_rows(table_hbm, out_hbm, idx_v, rows_v, gsem, wsem, base, per_w):
    n_ch = per_w // SC_CHUNK
    assert n_ch % 2 == 0

    def gather(j, p):
        off = pl.multiple_of(j * SC_CHUNK, SC_CHUNK)
        return pltpu.make_async_copy(table_hbm.at[idx_v.at[pl.ds(off, SC_CHUNK)]], rows_v.at[p], gsem.at[p])

    def write(j, p):
        off = pl.multiple_of(j * SC_CHUNK, SC_CHUNK)
        return pltpu.make_async_copy(rows_v.at[p], out_hbm.at[pl.ds(base + off, SC_CHUNK)], wsem.at[p])

    gather(0, 0).start()

    @pl.loop(0, n_ch, step=2)
    def _(j0):
        for p in range(2):
            j = j0 + p
            gather(j, p).wait()

            @pl.when(j >= 1)
            def _():
                write(j - 1, 1 - p).wait()

            @pl.when(j + 1 < n_ch)
            def _():
                gather(j + 1, 1 - p).start()
            write(j, p).start()

    write(n_ch - 1, 1).wait()


_SC_ROW_SCRATCH = [pltpu.VMEM((2, SC_CHUNK) + ROW_TILE, jnp.int32),
                   pltpu.SemaphoreType.DMA((2,)), pltpu.SemaphoreType.DMA((2,))]


def _sc_gather(table, idx):
    n_out = idx.shape[0]
    per_w = n_out // SC_WORKERS
    assert per_w * SC_WORKERS == n_out and per_w % (2 * SC_CHUNK) == 0

    def body(table_hbm, idx_hbm, out_hbm, idx_v, rows_v, gsem, wsem):
        base = _sc_worker_base(per_w)
        pltpu.sync_copy(idx_hbm.at[pl.ds(base, per_w)], idx_v)
        _sc_gather_rows(table_hbm, out_hbm, idx_v, rows_v, gsem, wsem, base, per_w)

    return pl.kernel(
        body, out_type=jax.ShapeDtypeStruct((n_out,) + ROW_TILE, jnp.int32), mesh=_sc_mesh(),
        scratch_types=[pltpu.VMEM((per_w,), jnp.int32)] + _SC_ROW_SCRATCH, name="sc_gather")(table, idx)


N_PARTS = 3
PART_BLOCKS = N_BLOCKS // N_PARTS
PART_SLOTS = PART_BLOCKS * EXP_BLOCK
assert PART_BLOCKS * N_PARTS == N_BLOCKS


def _sc_dispatch(table, slots, part):
    per_w = PART_SLOTS // SC_WORKERS
    assert per_w * SC_WORKERS == PART_SLOTS and per_w % (2 * SC_CHUNK) == 0 and SC_SCAN % SC_LANES == 0
    assert ZERO_ROWS & (ZERO_ROWS - 1) == 0

    def body(table_hbm, slots_hbm, out_hbm, idx_v, sl_v, rows_v, gsem, wsem):
        local = _sc_worker_base(per_w)
        base = part * PART_SLOTS + local
        lane = lax.iota(jnp.int32, SC_LANES)

        @pl.loop(0, per_w // SC_LANES)
        def _(j):
            off = pl.multiple_of(j * SC_LANES, SC_LANES)
            idx_v[pl.ds(off, SC_LANES)] = N + ((base + off + lane) & (ZERO_ROWS - 1))

        @pl.loop(0, N_ASSIGN // SC_SCAN)
        def _(c):
            pltpu.sync_copy(slots_hbm.at[pl.ds(pl.multiple_of(c * SC_SCAN, 8), SC_SCAN)], sl_v)

            @pl.loop(0, SC_SCAN // SC_LANES)
            def _(j):
                off = pl.multiple_of(j * SC_LANES, SC_LANES)
                loc = sl_v[pl.ds(off, SC_LANES)] - base
                mine = (loc >= 0) & (loc < per_w)
                tok = lax.shift_right_logical(c * SC_SCAN + off + lane, 3)
                plsc.store_scatter(idx_v, [jnp.where(mine, loc, 0)], tok, mask=mine)

        _sc_gather_rows(table_hbm, out_hbm, idx_v, rows_v, gsem, wsem, local, per_w)

    return pl.kernel(
        body, out_type=jax.ShapeDtypeStruct((PART_SLOTS,) + ROW_TILE, jnp.int32), mesh=_sc_mesh(),
        scratch_types=[pltpu.VMEM((per_w,), jnp.int32), pltpu.VMEM((SC_SCAN,), jnp.int32)] + _SC_ROW_SCRATCH,
        compiler_params=pltpu.CompilerParams(needs_layout_passes=False),
        name="sc_dispatch")(table, slots)


def _experts_body(be_ref, first_ref, par_ref, next_ref, nextblk_ref, nused_ref, x_ref, w1_hbm, w3_hbm, w2_hbm,
                  *rest, layer, part):
    y_ref, w1f, w3f, w2f, w1b, w3b, w2b, sem = rest[-8:]
    i = pl.program_id(0)
    b = part * PART_BLOCKS + i
    used = b < nused_ref[0]

    def copies(e, slot):
        return (pltpu.make_async_copy(w1_hbm.at[layer, e], w1f.at[slot], sem.at[0, slot]),
                pltpu.make_async_copy(w3_hbm.at[layer, e], w3f.at[slot], sem.at[1, slot]),
                pltpu.make_async_copy(w2_hbm.at[layer, e], w2f.at[slot], sem.at[2, slot]))

    @pl.when(used & (i == 0))
    def _():
        for c in copies(be_ref[b], par_ref[b]):
            c.start()

    @pl.when(used & ((i == 0) | (first_ref[b] == 1)))
    def _():
        slot = par_ref[b]
        for c in copies(be_ref[b], slot):
            c.wait()

        @pl.when((next_ref[b] >= 0) & (nextblk_ref[b] < (part + 1) * PART_BLOCKS))
        def _():
            for c in copies(next_ref[b], 1 - slot):
                c.start(priority=1)
        w1b[...] = w1f[slot].astype(BF16)
        w3b[...] = w3f[slot].astype(BF16)
        w2b[...] = w2f[slot].astype(BF16)

    @pl.when(used)
    def _():
        lo, hi = _unpack_bf16_pair(_load_row_tiles_2d(x_ref, EXP_BLOCK))
        lo = lo.astype(BF16)
        hi = hi.astype(BF16)
        half = D // 2
        h1 = (jnp.dot(lo, w1b[:half, :], preferred_element_type=F32)
              + jnp.dot(hi, w1b[half:, :], preferred_element_type=F32))
        h3 = (jnp.dot(lo, w3b[:half, :], preferred_element_type=F32)
              + jnp.dot(hi, w3b[half:, :], preferred_element_type=F32))
        hid = (_silu(h1) * h3).astype(BF16)
        y = jnp.dot(hid, w2b[...], preferred_element_type=F32)
        _store_row_tiles_2d(y_ref, _pack_bf16_pair(y[:, :half], y[:, half:]), EXP_BLOCK)

    @pl.when(jnp.logical_not(used))
    def _():
        y_ref[...] = jnp.zeros_like(y_ref)


def _experts(ctl, xs_part, w1, w3, w2, layer, part, ys_prev):
    def x_blk(i, *refs):
        n_here = jnp.clip(refs[-1][0] - part * PART_BLOCKS, 1, PART_BLOCKS)
        return (jnp.minimum(i, n_here - 1), 0)
    any_spec = pl.BlockSpec(memory_space=pl.ANY)
    in_specs = [pl.BlockSpec((EXP_BLOCK * ROW_TILE[0], LANES), x_blk), any_spec, any_spec, any_spec]
    args = [xs_part.reshape(PART_SLOTS * ROW_TILE[0], LANES), w1, w3, w2]
    aliases = {}
    if ys_prev is not None:
        in_specs.append(any_spec)
        args.append(ys_prev)
        aliases = {len(ctl) + 4: 0}
    grid_spec = pltpu.PrefetchScalarGridSpec(
        num_scalar_prefetch=len(ctl),
        grid=(PART_BLOCKS,),
        in_specs=in_specs,
        out_specs=pl.BlockSpec((EXP_BLOCK * ROW_TILE[0], LANES), lambda i, *refs: (part * PART_BLOCKS + i, 0)),
        scratch_shapes=[pltpu.VMEM((2, D, D_EXPERT), F32), pltpu.VMEM((2, D, D_EXPERT), F32),
                        pltpu.VMEM((2, D_EXPERT, D), F32),
                        pltpu.VMEM((D, D_EXPERT), BF16), pltpu.VMEM((D, D_EXPERT), BF16),
                        pltpu.VMEM((D_EXPERT, D), BF16), pltpu.SemaphoreType.DMA((3, 2))])
    return pl.pallas_call(
        functools.partial(_experts_body, layer=layer, part=part), grid_spec=grid_spec,
        out_shape=jax.ShapeDtypeStruct((L_SLOTS * ROW_TILE[0], LANES), jnp.int32),
        input_output_aliases=aliases,
        compiler_params=_cparams(1), name="experts")(*ctl, *args)


TM_FFN = 640


def _ffn_pre_body(x_ref, g_ref, shp_ref, shs_ref, scp_ref, scs_ref, rw_ref, w13_ref, w2_ref,
                  sh_ref, lg_ref, hp_ref, hb_ref):
    i = pl.program_id(0)

    @pl.when(i < N // TM_FFN)
    def _():
        for sidx in range(TM_FFN // SUB):
            rows = pl.ds(sidx * SUB, SUB)
            h = _prenorm_rows(i * (TM_FFN // SUB) + sidx, x_ref[rows, :], g_ref, shp_ref, shs_ref, scp_ref, scs_ref)
            lg_ref[rows, :] = jnp.dot(h, rw_ref[...], precision=HIGHEST, preferred_element_type=F32)
            packed = _pack_bf16_pair(h[:, :D // 2], h[:, D // 2:])
            for c in range(ROW_TILE[0]):
                hp_ref[pl.ds(sidx * SUB * ROW_TILE[0] + c, SUB, stride=ROW_TILE[0]), :] = (
                    packed[:, c * LANES:(c + 1) * LANES])
            hb_ref[rows, :] = h.astype(BF16)
        up = jnp.dot(hb_ref[...], w13_ref[...], preferred_element_type=F32)
        hid = (_silu(up[:, :D_EXPERT]) * up[:, D_EXPERT:]).astype(BF16)
        sh_ref[...] = jnp.dot(hid, w2_ref[...], preferred_element_type=F32)

    @pl.when(i >= N // TM_FFN)
    def _():
        hp_ref[...] = jnp.zeros_like(hp_ref)


def _ffn_pre(x, g, mod, layer, rw, w13, w2):
    last = N // TM_FFN - 1
    row = lambda i: (jnp.minimum(i, last), 0)
    mspec = lambda part, rows, blk: pl.BlockSpec((None, rows, D), lambda i, l=layer, p=part, b=blk: (l, b, p))
    full = lambda shape: pl.BlockSpec(shape, lambda i: (0, 0))
    return pl.pallas_call(
        _ffn_pre_body,
        grid=(N // TM_FFN + 1,),
        in_specs=[pl.BlockSpec((TM_FFN, D), row), full((1, D)),
                  mspec(3, 8, 0), mspec(3, SUB, 1), mspec(4, 8, 0), mspec(4, SUB, 1),
                  full((D, LANES)), full((D, 2 * D_EXPERT)), full((D_EXPERT, D))],
        out_specs=[pl.BlockSpec((TM_FFN, D), row), pl.BlockSpec((TM_FFN, LANES), row),
                   pl.BlockSpec((TM_FFN * ROW_TILE[0], LANES), lambda i: (i, 0))],
        out_shape=[jax.ShapeDtypeStruct((N, D), F32), jax.ShapeDtypeStruct((N, LANES), F32),
                   jax.ShapeDtypeStruct(((N + TM_FFN) * ROW_TILE[0], LANES), jnp.int32)],
        scratch_shapes=[pltpu.VMEM((TM_FFN, D), BF16)],
        compiler_params=_cparams(1), name="ffn_pre")(x, g.reshape(1, D), mod, mod, mod, mod, rw, w13, w2)


def _combine_body(g_ref, wt_ref, sh_ref, x_ref, gn_ref, gp_ref, gs_ref, *rest, tile0):
    o_ref, f_ref = rest[-2:]
    half = D // 2
    acc_lo = sh_ref[:, :half]
    acc_hi = sh_ref[:, half:]
    wt = wt_ref[...]
    per_tok = TOP_K * ROW_TILE[0]
    for k in range(TOP_K):
        packed = jnp.concatenate([g_ref[pl.ds(k * ROW_TILE[0] + c, SUB, stride=per_tok), :]
                                  for c in range(ROW_TILE[0])], axis=1)
        lo, hi = _unpack_bf16_pair(packed)
        w_c = wt[:, k:k + 1]
        acc_lo = acc_lo + w_c * lo
        acc_hi = acc_hi + w_c * hi
    f_ref[:, :half] = acc_lo
    f_ref[:, half:] = acc_hi
    _post_residual(tile0 + pl.program_id(0), x_ref, lambda rows: f_ref[rows, :], gn_ref, gp_ref, gs_ref, o_ref, SUB)


def _combine(gathered, wts, shared, x, g_post, mod, layer, tile0, n_tiles, out_prev):
    per_tok = TOP_K * ROW_TILE[0]
    row = lambda i: (tile0 + i, 0)
    in_specs = [pl.BlockSpec((SUB * per_tok, LANES), lambda i: (i, 0)),
                pl.BlockSpec((SUB, LANES), row), pl.BlockSpec((SUB, D), row), pl.BlockSpec((SUB, D), row),
                pl.BlockSpec((1, D), lambda i: (0, 0)),
                pl.BlockSpec((None, 8, D), lambda i, l=layer: (l, 0, 5)),
                pl.BlockSpec((None, SUB, D), lambda i, l=layer: (l, 1, 5))]
    args = [gathered.reshape(n_tiles * SUB * per_tok, LANES), wts, shared, x, g_post.reshape(1, D), mod, mod]
    aliases = {}
    if out_prev is not None:
        in_specs.append(pl.BlockSpec(memory_space=pl.ANY))
        args.append(out_prev)
        aliases = {len(args) - 1: 0}
    return pl.pallas_call(
        functools.partial(_combine_body, tile0=tile0),
        grid=(n_tiles,),
        in_specs=in_specs,
        out_specs=pl.BlockSpec((SUB, D), row),
        out_shape=jax.ShapeDtypeStruct((N, D), F32),
        scratch_shapes=[pltpu.VMEM((SUB, D), F32)],
        input_output_aliases=aliases,
        compiler_params=_cparams(1), name="combine")(*args)


COMBINE_RANGES = ((0, 33), (33, 32))

def _reorder_w_in(w):
    return jnp.concatenate([w[:, :1536], w[:, 1552:]], axis=1), jnp.pad(w[:, 1536:1552], ((0, 0), (0, LANES - 16)))


def _rope_tables(pos):
    half = DK // 2
    inv = ROPE_BASE ** (-jnp.arange(half, dtype=F32) / half)
    ang = pos.astype(F32)[:, None] * inv[None, :]
    cos = jnp.cos(ang)
    sin = jnp.sin(ang)
    return jnp.concatenate([cos, cos], axis=1), jnp.concatenate([-sin, sin], axis=1)


def _layer(l, x, mod, s_gla, s_pool, s_ret, wts):
    (norm_mix_pre, norm_mix_post, norm_ffn_pre, norm_ffn_post, w_in, w_gla_gate, b_gla_gate, gla_norm,
     pool_w, pool_scale, ret_norm, sgu_norm, sgu_w, sgu_b, w_branch, w_merge_gate, b_merge_gate, w_out,
     router_w, router_bias, expert_w1, expert_w3, expert_w2, shared_w1, shared_w3, shared_w2) = wts

    w_main, w_low = _reorder_w_in(w_in[l])
    h = _prenorm(x, norm_mix_pre[l], mod, l)
    p_main = _matmul(h, w_main.astype(BF16), 1664, 512, name="inproj")
    p_low = _matmul(h, w_low.astype(BF16), 1664, LANES, name="inproj_low")

    w_gate_pad = jnp.pad(w_gla_gate[l], ((0, LANES - 16), (0, 0)))
    b_gate = b_gla_gate[l].reshape(1, HEADS * DK)
    log_gamma = jnp.log1p(-jnp.exp2(-5.0 - jnp.arange(HEADS, dtype=F32)))
    dec_row = jnp.repeat(log_gamma, DK).reshape(1, HEADS * DK)
    cos_p, sin_p = _rope_tables(jnp.arange(T_P))
    cos_p = jnp.tile(cos_p, (1, 2))
    sin_p = jnp.tile(sin_p, (1, 2))
    g_gla = gla_norm[l].reshape(1, HEADS * DV)
    g_ret = ret_norm[l].reshape(1, HEADS * DV)

    oa_p, gla_p = _la_prompt(p_main, C_GQ, C_GK, C_GV, C_GR, p_low, p_low, w_gate_pad, b_gate, g_gla, False)
    oc_p, ret_p = _la_prompt(p_main, C_RQ, C_RK, C_RV, C_RG, cos_p, sin_p, dec_row, b_gate, g_ret, True)
    pw_bf = pool_w[l].astype(BF16)
    pscale = pool_scale[l].reshape(1, 512)
    ob_p = _pool_prompt(p_main, pw_bf, pscale)
    sgu_g = sgu_norm[l].reshape(1, 512)
    od_p = _sgu_prompt(p_main, sgu_g, sgu_w[l], jnp.pad(sgu_b[l].T, ((0, 0), (0, LANES - 4))))

    ps = p_main[N_P:]
    q_t = _to_tiles_t(ps[:, C_GQ:C_GQ + 256])
    k_t = _to_tiles_t(ps[:, C_GK:C_GK + 256])
    glow_t = jnp.pad(_to_tiles_t(p_low[N_P:, :16]), ((0, 0), (0, LANES - 16), (0, 0)))
    w_gate_t = jnp.pad(w_gla_gate[l].T, ((0, 0), (0, LANES - 16)))
    b_col = jnp.broadcast_to(b_gla_gate[l][:, None], (HEADS * DK, LANES))
    logit_t = _gate_logits_t(w_gate_t, glow_t, b_col)
    dummy = jnp.zeros((HEADS * DK, LANES), F32)
    oa_s, gla_s = _la_sample(q_t, k_t, logit_t, dummy, dummy, p_main, C_GV, C_GR, g_gla, s_gla[l], False)
    cos_s, sin_s = _rope_tables(jnp.full((1,), PAST_LEN))
    cos_c = jnp.broadcast_to(jnp.tile(cos_s[0], HEADS)[:, None], (HEADS * DK, LANES))
    sin_c = jnp.broadcast_to(jnp.tile(sin_s[0], HEADS)[:, None], (HEADS * DK, LANES))
    dec_c = jnp.broadcast_to(jnp.repeat(log_gamma, DK)[:, None], (HEADS * DK, LANES))
    rq_t = _to_tiles_t(ps[:, C_RQ:C_RQ + 256])
    rk_t = _to_tiles_t(ps[:, C_RK:C_RK + 256])
    oc_s, ret_s = _la_sample(rq_t, rk_t, dec_c, cos_c, sin_c, p_main, C_RV, C_RG, g_ret, s_ret[l], True)
    sgu_w0 = jnp.repeat(sgu_w[l][:, 0, 0], LANES).reshape(1, 512)
    sgu_b0 = jnp.repeat(sgu_b[l][:, 0], LANES).reshape(1, 512)
    ob_s, od_s, vn_s = _small_sample(p_main, s_pool[l], pw_bf, pscale, sgu_g, sgu_w0, sgu_b0)
    pool_p = p_main[:N_P, C_PIN:C_PIN + 512].reshape(B_P, T_P, 512)[:, T_P - POOL_BUF:]
    pool_s = jnp.concatenate([s_pool[l][:, 1:], ps[:, None, C_PIN:C_PIN + 512]], axis=1)

    branches = [jnp.concatenate([p, s.astype(BF16)], axis=0)
                for p, s in ((oa_p, oa_s), (ob_p, ob_s), (oc_p, oc_s), (od_p, od_s))]
    merged = _merge(h, branches, w_merge_gate[l].astype(BF16), b_merge_gate[l].reshape(1, 4 * D),
                    w_branch[l].astype(BF16))
    x = _outproj(merged, w_out[l].astype(BF16), x, norm_mix_post[l], mod, l)

    rw = jnp.pad(router_w[l], ((0, 0), (0, LANES - N_EXPERTS)))
    rb = jnp.broadcast_to(router_bias[l][:, None], (N_EXPERTS, SUB))
    w13 = jnp.concatenate([shared_w1[l], shared_w3[l]], axis=1).astype(BF16)
    shared, logits, h2_packed = _ffn_pre(x, norm_ffn_pre[l], mod, l, rw, w13, shared_w2[l].astype(BF16))
    eid, pos, wt, counts = _router(logits, rb)
    counts = counts[:, 0].astype(jnp.int32)
    padded = (counts + EXP_BLOCK - 1) // EXP_BLOCK * EXP_BLOCK
    pad_end = jnp.cumsum(padded)
    pad_start = pad_end - padded
    nused = (pad_end[-1] // EXP_BLOCK).astype(jnp.int32).reshape(1)
    blk_row = jnp.arange(N_BLOCKS, dtype=jnp.int32) * EXP_BLOCK
    block_e = jnp.minimum(jnp.sum((blk_row[:, None] >= pad_end[None, :]).astype(jnp.int32), axis=1),
                          N_EXPERTS - 1)
    first = jnp.concatenate([jnp.ones((1,), jnp.int32), (block_e[1:] != block_e[:-1]).astype(jnp.int32)])
    first = jnp.where(blk_row < pad_end[-1], first, 0)
    par = (jnp.cumsum(first) - 1) % 2
    live = jnp.where(padded > 0, jnp.arange(N_EXPERTS), N_EXPERTS)
    after = jnp.concatenate([lax.cummin(live, reverse=True)[1:], jnp.full((1,), N_EXPERTS)])
    next_e = jnp.where(after < N_EXPERTS, after, -1)[block_e].astype(jnp.int32)
    next_blk = jnp.where(next_e >= 0, (pad_start // EXP_BLOCK)[jnp.maximum(next_e, 0)], N_BLOCKS).astype(jnp.int32)
    start_of = jnp.sum(jnp.where(eid[:, :, None] == jnp.arange(N_EXPERTS), pad_start.astype(jnp.int32), 0), axis=-1)
    slots = (start_of + pos).T.reshape(N_ASSIGN)
    wt = jnp.pad(wt.T, ((0, 0), (0, LANES - TOP_K)))
    table = h2_packed.reshape((N + TM_FFN,) + ROW_TILE)
    ctl = (block_e, first, par.astype(jnp.int32), next_e, next_blk, nused)
    xs_parts = [_sc_dispatch(table, slots, part) for part in range(N_PARTS)]
    ys = None
    for part in range(N_PARTS):
        ys = _experts(ctl, xs_parts[part], expert_w1, expert_w3, expert_w2, l, part, ys)
    ys = ys.reshape((L_SLOTS,) + ROW_TILE)
    x_new = None
    for tile0, n_tiles in COMBINE_RANGES:
        a0, a1 = tile0 * SUB * TOP_K, (tile0 + n_tiles) * SUB * TOP_K
        x_new = _combine(_sc_gather(ys, slots[a0:a1]), wt, shared, x, norm_ffn_post[l], mod, l, tile0, n_tiles, x_new)
    x = x_new
    return x, (gla_p, gla_s, pool_p, pool_s, ret_p, ret_s, vn_s)


def kernel(x_prompt, x_sample, c_prompt, c_sample, state_gla, state_pool, state_ret, w_ada, b_ada, norm_mix_pre, norm_mix_post, norm_ffn_pre, norm_ffn_post, w_in, w_gla_gate, b_gla_gate, gla_norm, pool_w, pool_scale, ret_norm, sgu_norm, sgu_w, sgu_b, w_branch, w_merge_gate, b_merge_gate, w_out, router_w, router_bias, expert_w1, expert_w3, expert_w2, shared_w1, shared_w3, shared_w2):
    wts = (norm_mix_pre, norm_mix_post, norm_ffn_pre, norm_ffn_post, w_in, w_gla_gate, b_gla_gate, gla_norm,
           pool_w, pool_scale, ret_norm, sgu_norm, sgu_w, sgu_b, w_branch, w_merge_gate, b_merge_gate, w_out,
           router_w, router_bias, expert_w1, expert_w3, expert_w2, shared_w1, shared_w3, shared_w2)
    c_all = jnp.zeros((MOD_ROWS, D), F32).at[:B_P].set(c_prompt).at[SUB:SUB + N_S].set(c_sample)
    mod = _ada(c_all, w_ada, b_ada)
    x = jnp.concatenate([x_prompt.reshape(N_P, D), x_sample.reshape(N_S, D)], axis=0)
    per_layer = []
    for l in range(DEPTH):
        x, states = _layer(l, x, mod, state_gla, state_pool, state_ret, wts)
        per_layer.append(states)
    gla_p, gla_s, pool_p, pool_s, ret_p, ret_s, vn_s = (jnp.stack(z) for z in zip(*per_layer))
    return (x[:N_P].reshape(B_P, T_P, D), x[N_P:].reshape(N_S, 1, D),
            gla_p, gla_s, pool_p, pool_s, ret_p, ret_s, vn_s.reshape(DEPTH, N_S, 1, 512))
```

```python
import functools

import jax
import jax.numpy as jnp
from jax import lax
from jax.experimental import pallas as pl
from jax.experimental.pallas import tpu as pltpu
from jax.experimental.pallas import tpu_sc as plsc

F32 = jnp.float32
BF16 = jnp.bfloat16
HIGHEST = lax.Precision.HIGHEST

D = 2048
B_P, T_P = 4, 2048
N_P = B_P * T_P
N_S = 128
N = N_P + N_S
DEPTH = 2
PAST_LEN = 16384
EPS = 1e-6
HEADS, DK, DV = 4, 64, 128
CHUNK = 64
GATE_TEMP = 16.0
POOL_WINDOWS = (2, 4, 8, 16)
POOL_BUF = 15
ROPE_BASE = 10000.0
N_EXPERTS = 64
TOP_K = 8
D_EXPERT = 512
ROUTED_SCALE = 2.5

LANES = 128
SUB = 128
MOD_ROWS = 256
EXP_BLOCK = 256
N_ASSIGN = N * TOP_K
N_BLOCKS = -(-(N_ASSIGN + N_EXPERTS * (EXP_BLOCK - 1)) // EXP_BLOCK)
L_SLOTS = N_BLOCKS * EXP_BLOCK
VMEM_LIMIT = 56 * 1024 * 1024

C_GQ, C_GK, C_GV, C_GR, C_PIN, C_RQ, C_RK, C_RV, C_RG, C_SU, C_SV = (
    0, 256, 512, 1024, 1536, 2048, 2304, 2560, 3072, 3584, 4096)
P_MAIN = 4608


def _cparams(n_axes=1):
    return pltpu.CompilerParams(dimension_semantics=("arbitrary",) * n_axes,
                                vmem_limit_bytes=VMEM_LIMIT)


def _silu(x):
    return x * jax.nn.sigmoid(x)


def _mod_rows(t, mp_ref, ms_ref):
    b = jnp.minimum(t // (T_P // SUB), B_P - 1)
    return jnp.where(t >= N_P // SUB, ms_ref[...], mp_ref[pl.ds(b, 1), :])


def _mod_specs(layer, part):
    return [pl.BlockSpec((None, 8, D), lambda i, l=layer, p=part: (l, 0, p)),
            pl.BlockSpec((None, SUB, D), lambda i, l=layer, p=part: (l, 1, p))]


def _pack_bf16_pair(lo, hi):
    lo_u = lax.bitcast_convert_type(lo.astype(BF16).astype(F32), jnp.uint32)
    hi_u = lax.bitcast_convert_type(hi.astype(BF16).astype(F32), jnp.uint32)
    return lax.bitcast_convert_type((hi_u & jnp.uint32(0xFFFF0000)) | (lo_u >> 16), jnp.int32)


def _unpack_bf16_pair(w):
    u = lax.bitcast_convert_type(w, jnp.uint32)
    lo = lax.bitcast_convert_type(u << 16, F32)
    hi = lax.bitcast_convert_type(u & jnp.uint32(0xFFFF0000), F32)
    return lo, hi


ROW_TILE = (8, LANES)


def _load_row_tiles(ref):
    return jnp.concatenate([ref[:, c, :] for c in range(ROW_TILE[0])], axis=1)


def _store_row_tiles(ref, val):
    for c in range(ROW_TILE[0]):
        ref[:, c, :] = val[:, c * LANES:(c + 1) * LANES]


def _load_row_tiles_2d(ref, rows):
    return jnp.concatenate([ref[pl.ds(c, rows, stride=ROW_TILE[0]), :] for c in range(ROW_TILE[0])], axis=1)


def _store_row_tiles_2d(ref, val, rows):
    for c in range(ROW_TILE[0]):
        ref[pl.ds(c, rows, stride=ROW_TILE[0]), :] = val[:, c * LANES:(c + 1) * LANES]


def _ada_body(c_ref, w_ref, b_ref, o_ref):
    s = _silu(c_ref[...]).astype(BF16)
    o_ref[...] = jnp.dot(s, w_ref[...].astype(BF16), preferred_element_type=F32) + b_ref[...]


def _ada(c_all, w_ada, b_ada):
    tn = 1024
    return pl.pallas_call(
        _ada_body,
        grid=(DEPTH, 6 * D // tn),
        in_specs=[pl.BlockSpec((MOD_ROWS, D), lambda l, j: (0, 0)),
                  pl.BlockSpec((None, D, tn), lambda l, j: (l, 0, j)),
                  pl.BlockSpec((None, 1, tn), lambda l, j: (l, 0, j))],
        out_specs=pl.BlockSpec((None, MOD_ROWS, tn), lambda l, j: (l, 0, j)),
        out_shape=jax.ShapeDtypeStruct((DEPTH, MOD_ROWS, 6 * D), F32),
        compiler_params=_cparams(2), name="ada")(c_all, w_ada, b_ada.reshape(DEPTH, 1, 6 * D))


ZERO_ROWS = 2 * SUB


def _x_specs(x, tm, n_axes):
    row = (lambda i: (i, 0)) if n_axes == 1 else (lambda i, j: (i, 0))
    if not isinstance(x, tuple):
        return [pl.BlockSpec((tm, D), row)], [x]
    zero = (lambda i: (0, 0)) if n_axes == 1 else (lambda i, j: (0, 0))
    return [pl.BlockSpec((tm, D), row), pl.BlockSpec((N_S, D), zero)], list(x)


def _x_rows(x_refs, t, rows):
    if len(x_refs) == 1:
        return x_refs[0][rows, :]
    return jnp.where(t >= N_P // SUB, x_refs[1][...], x_refs[0][rows, :])


def _prenorm_rows(t, x, g_ref, shp_ref, shs_ref, scp_ref, scs_ref):
    y = x * lax.rsqrt(jnp.mean(x * x, axis=-1, keepdims=True) + EPS) * g_ref[...]
    return y * (1.0 + _mod_rows(t, scp_ref, scs_ref)) + _mod_rows(t, shp_ref, shs_ref)


TM_NORM = 640


def _prenorm_body(*refs):
    g_ref, shp_ref, shs_ref, scp_ref, scs_ref, h_ref = refs[-6:]
    for sidx in range(TM_NORM // SUB):
        rows = pl.ds(sidx * SUB, SUB)
        t = pl.program_id(0) * (TM_NORM // SUB) + sidx
        h_ref[rows, :] = _prenorm_rows(t, _x_rows(refs[:-6], t, rows), g_ref, shp_ref, shs_ref, scp_ref,
                                       scs_ref).astype(BF16)


def _prenorm(x, g, mod, layer):
    x_specs, x_args = _x_specs(x, TM_NORM, 1)
    return pl.pallas_call(
        _prenorm_body,
        grid=(N // TM_NORM,),
        in_specs=x_specs + [pl.BlockSpec((1, D), lambda i: (0, 0))] + _mod_specs(layer, 0) + _mod_specs(layer, 1),
        out_specs=pl.BlockSpec((TM_NORM, D), lambda i: (i, 0)),
        out_shape=jax.ShapeDtypeStruct((N, D), BF16),
        compiler_params=_cparams(1), name="prenorm")(*x_args, g.reshape(1, D), mod, mod, mod, mod)


def _mm_body(x_ref, w_ref, o_ref):
    o_ref[...] = jnp.dot(x_ref[...], w_ref[...], preferred_element_type=F32).astype(o_ref.dtype)


def _matmul(x, w, tm, tn, out_dtype=F32, name="mm"):
    m, k = x.shape
    n = w.shape[1]
    return pl.pallas_call(
        _mm_body,
        grid=(m // tm, n // tn),
        in_specs=[pl.BlockSpec((tm, k), lambda i, j: (i, 0)),
                  pl.BlockSpec((k, tn), lambda i, j: (0, j))],
        out_specs=pl.BlockSpec((tm, tn), lambda i, j: (i, j)),
        out_shape=jax.ShapeDtypeStruct((m, n), out_dtype),
        compiler_params=_cparams(2), name=name)(x, w)


ROWS_LA = 256


def _swap_halves_lanes(x):
    lane = lax.broadcasted_iota(jnp.int32, x.shape, 1)
    return jnp.where((lane % 64) < 32, pltpu.roll(x, 96, 1), pltpu.roll(x, 32, 1))


def _rope_lanes(x, cos, sin_signed):
    parts = []
    for half in range(2):
        xh = x[:, half * LANES:(half + 1) * LANES]
        parts.append(xh * cos + _swap_halves_lanes(xh) * sin_signed)
    return jnp.concatenate(parts, axis=1)


def _la_prompt_body(q_ref, k_ref, v_ref, r_ref, aux_ref, aux2_ref, dec_ref, bias_ref, g_ref,
                    o_ref, st_out_ref, st_ref, *, retention):
    t = pl.program_id(1)

    @pl.when(t == 0)
    def _():
        st_ref[...] = jnp.zeros_like(st_ref)

    ri = lax.broadcasted_iota(jnp.int32, (CHUNK, CHUNK), 0)
    ci = lax.broadcasted_iota(jnp.int32, (CHUNK, CHUNK), 1)
    causal = ri >= ci
    tril = causal.astype(F32)
    scale = DK ** -0.5

    for c in range(ROWS_LA // CHUNK):
        rows = pl.ds(c * CHUNK, CHUNK)
        q = q_ref[rows, :]
        k = k_ref[rows, :]
        v = v_ref[rows, :]
        if retention:
            cos = aux_ref[rows, :]
            sin = aux2_ref[rows, :]
            q = _rope_lanes(q, cos, sin)
            k = _rope_lanes(k, cos, sin) * scale
            la = jnp.broadcast_to(dec_ref[...], (CHUNK, HEADS * DK))
        else:
            q = q * scale
            logit = jnp.dot(aux_ref[rows, :], dec_ref[...], precision=HIGHEST,
                            preferred_element_type=F32) + bias_ref[...]
            la = jax.nn.log_sigmoid(logit) / GATE_TEMP
        bc = jnp.dot(tril, la, precision=HIGHEST, preferred_element_type=F32)
        bl = bc[CHUNK - 1:CHUNK, :]
        qd = q * jnp.exp(bc)
        ki = k * jnp.exp(-bc)
        ke = k * jnp.exp(bl - bc)
        ac = jnp.exp(bl)
        outs = []
        for h in range(HEADS):
            ks = slice(h * DK, (h + 1) * DK)
            vs = slice(h * DV, (h + 1) * DV)
            qd_h = qd[:, ks].astype(BF16)
            ki_h = ki[:, ks].astype(BF16)
            ke_h = ke[:, ks].astype(BF16)
            v_h = v[:, vs].astype(BF16)
            sc = lax.dot_general(qd_h, ki_h, (((1,), (1,)), ((), ())), preferred_element_type=F32)
            sc = jnp.where(causal, sc, 0.0)
            o_h = jnp.dot(sc.astype(BF16), v_h, preferred_element_type=F32)
            st = st_ref[h]
            o_h = o_h + lax.dot_general(qd_h, st.astype(BF16), (((1,), (1,)), ((), ())),
                                        preferred_element_type=F32)
            kv_t = lax.dot_general(v_h, ke_h, (((0,), (0,)), ((), ())), preferred_element_type=F32)
            st_ref[h] = st * ac[:, ks] + kv_t
            o_n = o_h * lax.rsqrt(jnp.mean(o_h * o_h, axis=-1, keepdims=True) + EPS) * g_ref[:, vs]
            outs.append(o_n)
        o = jnp.concatenate(outs, axis=1) * _silu(r_ref[rows, :])
        o_ref[rows, :] = o.astype(BF16)

    st_out_ref[...] = st_ref[...]


def _la_prompt(p_main, cq, ck, cv, cr, aux, aux2, dec, bias, g, retention):
    nt = T_P // ROWS_LA
    rowblk = lambda b, t: b * nt + t
    if retention:
        aux_specs = [pl.BlockSpec((ROWS_LA, LANES), lambda b, t: (t, 0)),
                     pl.BlockSpec((ROWS_LA, LANES), lambda b, t: (t, 0))]
    else:
        aux_specs = [pl.BlockSpec((ROWS_LA, LANES), lambda b, t: (rowblk(b, t), 0)),
                     pl.BlockSpec((8, LANES), lambda b, t: (0, 0))]
    o, st = pl.pallas_call(
        functools.partial(_la_prompt_body, retention=retention),
        grid=(B_P, nt),
        in_specs=[pl.BlockSpec((ROWS_LA, 256), lambda b, t: (rowblk(b, t), cq // 256)),
                  pl.BlockSpec((ROWS_LA, 256), lambda b, t: (rowblk(b, t), ck // 256)),
                  pl.BlockSpec((ROWS_LA, 512), lambda b, t: (rowblk(b, t), cv // 512)),
                  pl.BlockSpec((ROWS_LA, 512), lambda b, t: (rowblk(b, t), cr // 512))]
        + aux_specs
        + [pl.BlockSpec(dec.shape, lambda b, t: (0, 0)),
           pl.BlockSpec((1, HEADS * DK), lambda b, t: (0, 0)),
           pl.BlockSpec((1, HEADS * DV), lambda b, t: (0, 0))],
        out_specs=[pl.BlockSpec((ROWS_LA, HEADS * DV), lambda b, t: (rowblk(b, t), 0)),
                   pl.BlockSpec((None, HEADS, DV, DK), lambda b, t: (b, 0, 0, 0))],
        out_shape=[jax.ShapeDtypeStruct((N, HEADS * DV), BF16),
                   jax.ShapeDtypeStruct((B_P, HEADS, DV, DK), F32)],
        scratch_shapes=[pltpu.VMEM((HEADS, DV, DK), F32)],
        compiler_params=_cparams(2), name="ret_prompt" if retention else "gla_prompt",
    )(p_main, p_main, p_main, p_main, aux, aux2, dec, bias, g)
    return o, jnp.swapaxes(st, -1, -2)


SAMPLE_TILE = 8


def _la_sample_body(qt_ref, kt_ref, lt_ref, cos_ref, sin_ref, v_ref, r_ref, g_ref, s_ref,
                    o_ref, s_out_ref, *, retention):
    scale = DK ** -0.5
    qt = qt_ref[...]
    kt = kt_ref[...]
    if retention:
        def rope(x):
            sw = jnp.concatenate(
                [x[h * DK + (DK // 2) * (1 - j): h * DK + (DK // 2) * (2 - j), :]
                 for h in range(HEADS) for j in range(2)], axis=0)
            return x * cos_ref[...] + sw * sin_ref[...]
        qt = rope(qt)
        kt = rope(kt) * scale
        la = lt_ref[...]
    else:
        qt = qt * scale
        la = jax.nn.log_sigmoid(lt_ref[...]) / GATE_TEMP
    at = jnp.exp(la)
    qd = qt * at
    ki = kt * jnp.exp(-la)
    prod = qd * ki
    v8 = v_ref[...]
    r8 = r_ref[...]
    g = g_ref[...]
    for j in range(SAMPLE_TILE):
        for h in range(HEADS):
            ks = slice(h * DK, (h + 1) * DK)
            vs = slice(h * DV, (h + 1) * DV)
            a_c = jnp.broadcast_to(at[ks, j:j + 1], (DK, DV))
            k_c = jnp.broadcast_to(kt[ks, j:j + 1], (DK, DV))
            q_c = jnp.broadcast_to(qd[ks, j:j + 1], (DK, DV))
            s_c = jnp.broadcast_to(jnp.sum(prod[ks, j:j + 1], axis=0, keepdims=True), (1, DV))
            s0 = s_ref[j, h]
            v_row = v8[j:j + 1, vs]
            s_out_ref[j, h] = a_c * s0 + k_c * v_row
            o_row = s_c * v_row + jnp.sum(q_c * s0, axis=0, keepdims=True)
            o_n = o_row * lax.rsqrt(jnp.mean(o_row * o_row, axis=-1, keepdims=True) + EPS) * g[:, vs]
            o_ref[j:j + 1, vs] = o_n * _silu(r8[j:j + 1, vs])


def _la_sample(qt, kt, lt, cos_t, sin_t, p_main, cv, cr, g, s0, retention):
    nt = N_S // SAMPLE_TILE
    row0 = N_P // SAMPLE_TILE
    tile = pl.BlockSpec((None, HEADS * DK, LANES), lambda i: (i, 0, 0))
    full = pl.BlockSpec((HEADS * DK, LANES), lambda i: (0, 0))
    lt_spec = full if retention else tile
    return pl.pallas_call(
        functools.partial(_la_sample_body, retention=retention),
        grid=(nt,),
        in_specs=[tile, tile, lt_spec, full, full,
                  pl.BlockSpec((SAMPLE_TILE, 512), lambda i: (row0 + i, cv // 512)),
                  pl.BlockSpec((SAMPLE_TILE, 512), lambda i: (row0 + i, cr // 512)),
                  pl.BlockSpec((1, HEADS * DV), lambda i: (0, 0)),
                  pl.BlockSpec((SAMPLE_TILE, HEADS, DK, DV), lambda i: (i, 0, 0, 0))],
        out_specs=[pl.BlockSpec((SAMPLE_TILE, HEADS * DV), lambda i: (i, 0)),
                   pl.BlockSpec((SAMPLE_TILE, HEADS, DK, DV), lambda i: (i, 0, 0, 0))],
        out_shape=[jax.ShapeDtypeStruct((N_S, HEADS * DV), F32),
                   jax.ShapeDtypeStruct((N_S, HEADS, DK, DV), F32)],
        compiler_params=_cparams(1), name="ret_sample" if retention else "gla_sample",
    )(qt, kt, lt, cos_t, sin_t, p_main, p_main, g, s0)


def _gate_logits_t_body(w_ref, x_ref, b_ref, o_ref):
    o_ref[...] = jnp.dot(w_ref[...], x_ref[...], precision=HIGHEST, preferred_element_type=F32) + b_ref[...]


def _gate_logits_t(w_gate_t, glow_t, b_col):
    nt = N_S // SAMPLE_TILE
    return pl.pallas_call(
        _gate_logits_t_body,
        grid=(nt,),
        in_specs=[pl.BlockSpec((HEADS * DK, LANES), lambda i: (0, 0)),
                  pl.BlockSpec((None, LANES, LANES), lambda i: (i, 0, 0)),
                  pl.BlockSpec((HEADS * DK, LANES), lambda i: (0, 0))],
        out_specs=pl.BlockSpec((None, HEADS * DK, LANES), lambda i: (i, 0, 0)),
        out_shape=jax.ShapeDtypeStruct((nt, HEADS * DK, LANES), F32),
        compiler_params=_cparams(1), name="gate_logits_t")(w_gate_t, glow_t, b_col)


def _to_tiles_t(x):
    c = x.shape[1]
    xt = jnp.swapaxes(x.reshape(N_S // SAMPLE_TILE, SAMPLE_TILE, c), 1, 2)
    return jnp.pad(xt, ((0, 0), (0, 0), (0, LANES - SAMPLE_TILE)))


ROWS_POOL = 512


def _pool_mix(y, w_ref, sc_ref):
    outs = []
    for gi in range(4):
        cs = slice(gi * LANES, (gi + 1) * LANES)
        outs.append(jnp.dot(y[:, cs].astype(BF16), w_ref[gi], preferred_element_type=F32))
    return jnp.concatenate(outs, axis=1) * sc_ref[...]


def _pool_prompt_body(p_ref, halo_ref, w_ref, sc_ref, o_ref):
    t = pl.program_id(1)
    p = p_ref[...]
    halo = jnp.where(t == 0, 0.0, halo_ref[...])
    full = jnp.concatenate([halo, p], axis=0)
    pos = t * ROWS_POOL + lax.broadcasted_iota(jnp.int32, (ROWS_POOL, LANES), 0)
    means = []
    for gi, w in enumerate(POOL_WINDOWS):
        s = full[:, gi * LANES:(gi + 1) * LANES]
        step = 1
        while step < w:
            s = s + pltpu.roll(s, step, 0)
            step *= 2
        win = s[16:, :]
        cnt = jnp.minimum(w, pos + 1).astype(F32)
        means.append(win / cnt)
    y = jnp.concatenate(means, axis=1) - p
    o_ref[...] = _pool_mix(y, w_ref, sc_ref).astype(BF16)


def _pool_prompt(p_main, w_bf, scale):
    nt = T_P // ROWS_POOL
    return pl.pallas_call(
        _pool_prompt_body,
        grid=(B_P, nt),
        in_specs=[pl.BlockSpec((ROWS_POOL, 512), lambda b, t: (b * nt + t, C_PIN // 512)),
                  pl.BlockSpec((16, 512), lambda b, t: (jnp.maximum((b * nt + t) * (ROWS_POOL // 16) - 1, 0),
                                                        C_PIN // 512)),
                  pl.BlockSpec((4, LANES, LANES), lambda b, t: (0, 0, 0)),
                  pl.BlockSpec((1, 512), lambda b, t: (0, 0))],
        out_specs=pl.BlockSpec((ROWS_POOL, 512), lambda b, t: (b * nt + t, 0)),
        out_shape=jax.ShapeDtypeStruct((N, 512), BF16),
        compiler_params=_cparams(2), name="pool_prompt")(p_main, p_main, w_bf, scale)


def _small_sample_body(p_ref, buf_ref, pw_ref, psc_ref, u_ref, sv_ref, sg_ref, sw_ref, sb_ref,
                       ob_ref, od_ref, vn_ref):
    p = p_ref[...]
    means = []
    for gi, w in enumerate(POOL_WINDOWS):
        cs = slice(gi * LANES, (gi + 1) * LANES)
        s = p[:, cs]
        for j in range(1, w):
            s = s + buf_ref[:, POOL_BUF - j, cs]
        means.append(s / float(min(w, PAST_LEN + 1)))
    y = jnp.concatenate(means, axis=1) - p
    ob_ref[...] = _pool_mix(y, pw_ref, psc_ref)
    sv = sv_ref[...]
    vn = sv * lax.rsqrt(jnp.mean(sv * sv, axis=-1, keepdims=True) + EPS) * sg_ref[...]
    vn_ref[...] = vn
    od_ref[...] = u_ref[...] * (sw_ref[...] * vn + sb_ref[...])


def _small_sample(p_main, buf, pw_bf, pscale, sgu_g, sgu_w0, sgu_b0):
    row = N_P // N_S
    col = lambda c: pl.BlockSpec((N_S, 512), lambda i, c=c: (row, c // 512))
    vec = pl.BlockSpec((1, 512), lambda i: (0, 0))
    return pl.pallas_call(
        _small_sample_body,
        grid=(1,),
        in_specs=[col(C_PIN), pl.BlockSpec((N_S, POOL_BUF, 512), lambda i: (0, 0, 0)),
                  pl.BlockSpec((4, LANES, LANES), lambda i: (0, 0, 0)), vec,
                  col(C_SU), col(C_SV), vec, vec, vec],
        out_specs=[pl.BlockSpec((N_S, 512), lambda i: (0, 0))] * 3,
        out_shape=[jax.ShapeDtypeStruct((N_S, 512), F32)] * 3,
        compiler_params=_cparams(1), name="small_sample",
    )(p_main, buf, pw_bf, pscale, p_main, p_main, sgu_g, sgu_w0, sgu_b0)


ROWS_SGU = 512
SGU_CHUNK = 128


def _sgu_prompt_body(u_ref, v_ref, g_ref, w_ref, bt_ref, o_ref):
    ri = lax.broadcasted_iota(jnp.int32, (SGU_CHUNK, SGU_CHUNK), 0)
    ci = lax.broadcasted_iota(jnp.int32, (SGU_CHUNK, SGU_CHUNK), 1)
    causal = ri >= ci
    for c in range(ROWS_SGU // SGU_CHUNK):
        rows = pl.ds(c * SGU_CHUNK, SGU_CHUNK)
        v = v_ref[rows, :]
        vn = (v * lax.rsqrt(jnp.mean(v * v, axis=-1, keepdims=True) + EPS) * g_ref[...]).astype(BF16)
        outs = []
        for gi in range(4):
            cs = slice(gi * LANES, (gi + 1) * LANES)
            w = jnp.where(causal, w_ref[gi], 0.0).astype(BF16)
            mixed = jnp.dot(w, vn[:, cs], preferred_element_type=F32)
            outs.append(mixed + jnp.broadcast_to(bt_ref[:, gi:gi + 1], (SGU_CHUNK, LANES)))
        o_ref[rows, :] = (u_ref[rows, :] * jnp.concatenate(outs, axis=1)).astype(BF16)


def _sgu_prompt(p_main, g, w, b_t):
    return pl.pallas_call(
        _sgu_prompt_body,
        grid=(N_P // ROWS_SGU,),
        in_specs=[pl.BlockSpec((ROWS_SGU, 512), lambda i: (i, C_SU // 512)),
                  pl.BlockSpec((ROWS_SGU, 512), lambda i: (i, C_SV // 512)),
                  pl.BlockSpec((1, 512), lambda i: (0, 0)),
                  pl.BlockSpec((4, SGU_CHUNK, SGU_CHUNK), lambda i: (0, 0, 0)),
                  pl.BlockSpec((SGU_CHUNK, LANES), lambda i: (0, 0))],
        out_specs=pl.BlockSpec((ROWS_SGU, 512), lambda i: (i, 0)),
        out_shape=jax.ShapeDtypeStruct((N, 512), BF16),
        compiler_params=_cparams(1), name="sgu_prompt")(p_main, p_main, g, w, b_t)


TM_MERGE = 640
TN_MERGE = 512


def _merge_body(h_ref, ba_ref, bb_ref, bc_ref, bd_ref, g0, g1, g2, g3, u0, u1, u2, u3,
                c0, c1, c2, c3, o_ref):
    h = h_ref[...]
    acc = None
    for br, gw, uw, gb in ((ba_ref, g0, u0, c0), (bb_ref, g1, u1, c1), (bc_ref, g2, u2, c2), (bd_ref, g3, u3, c3)):
        gate = jax.nn.sigmoid(jnp.dot(h, gw[...], preferred_element_type=F32) + gb[...])
        up = jnp.dot(br[...], uw[...], preferred_element_type=F32)
        acc = gate * up if acc is None else acc + gate * up
    o_ref[...] = acc.astype(BF16)


def _merge(h, branches, w_mg, b_mg, w_br):
    nj = D // TN_MERGE
    row = lambda w: pl.BlockSpec((TM_MERGE, w), lambda i, j: (i, 0))
    gate_w = [pl.BlockSpec((D, TN_MERGE), lambda i, j, b=b: (0, b * nj + j)) for b in range(4)]
    up_w = [pl.BlockSpec((None, 512, TN_MERGE), lambda i, j, b=b: (b, 0, j)) for b in range(4)]
    gate_b = [pl.BlockSpec((1, TN_MERGE), lambda i, j, b=b: (0, b * nj + j)) for b in range(4)]
    return pl.pallas_call(
        _merge_body,
        grid=(N // TM_MERGE, nj),
        in_specs=[row(D)] + [row(512)] * 4 + gate_w + up_w + gate_b,
        out_specs=pl.BlockSpec((TM_MERGE, TN_MERGE), lambda i, j: (i, j)),
        out_shape=jax.ShapeDtypeStruct((N, D), BF16),
        compiler_params=_cparams(2), name="merge",
    )(h, *branches, w_mg, w_mg, w_mg, w_mg, w_br, w_br, w_br, w_br, b_mg, b_mg, b_mg, b_mg)


TM_OUT = 640
TN_OUT = 512


def _post_value(t, x, y, gn_ref, gp_ref, gs_ref):
    yn = y * lax.rsqrt(jnp.mean(y * y, axis=-1, keepdims=True) + EPS) * gn_ref[...]
    return x + _mod_rows(t, gp_ref, gs_ref) * yn


def _outproj_body(m_ref, w_ref, *refs):
    gn_ref, gp_ref, gs_ref, o_ref, acc_ref = refs[-5:]
    j = pl.program_id(1)
    acc_ref[j] = jnp.dot(m_ref[...], w_ref[...], preferred_element_type=F32)

    @pl.when(j == D // TN_OUT - 1)
    def _():
        for sidx in range(TM_OUT // SUB):
            rows = pl.ds(sidx * SUB, SUB)
            t = pl.program_id(0) * (TM_OUT // SUB) + sidx
            y = jnp.concatenate([acc_ref[c, rows, :] for c in range(D // TN_OUT)], axis=1)
            o_ref[rows, :] = _post_value(t, _x_rows(refs[:-5], t, rows), y, gn_ref, gp_ref, gs_ref)


def _outproj(merged, w_out, x, g_post, mod, layer):
    mspec = [pl.BlockSpec((None, 8, D), lambda i, j, l=layer: (l, 0, 2)),
             pl.BlockSpec((None, SUB, D), lambda i, j, l=layer: (l, 1, 2))]
    x_specs, x_args = _x_specs(x, TM_OUT, 2)
    return pl.pallas_call(
        _outproj_body,
        grid=(N // TM_OUT, D // TN_OUT),
        in_specs=[pl.BlockSpec((TM_OUT, D), lambda i, j: (i, 0)),
                  pl.BlockSpec((D, TN_OUT), lambda i, j: (0, j))] + x_specs
        + [pl.BlockSpec((1, D), lambda i, j: (0, 0))] + mspec,
        out_specs=pl.BlockSpec((TM_OUT, D), lambda i, j: (i, 0)),
        out_shape=jax.ShapeDtypeStruct((N, D), F32),
        scratch_shapes=[pltpu.VMEM((D // TN_OUT, TM_OUT, TN_OUT), F32)],
        compiler_params=_cparams(2), name="outproj",
    )(merged, w_out, *x_args, g_post.reshape(1, D), mod, mod)


def _router_body(lg_ref, b_ref, eid_ref, pos_ref, wt_ref, cnt_ref, run_ref):
    i = pl.program_id(0)

    @pl.when(i == 0)
    def _():
        run_ref[...] = jnp.zeros_like(run_ref)

    ng, gs = 8, N_EXPERTS // 8
    neg = -jnp.inf
    scores = jax.nn.sigmoid(lg_ref[...].T[:N_EXPERTS, :])
    sel = scores + b_ref[...]
    sel3 = sel.reshape(ng, gs, SUB)
    sub3 = lax.broadcasted_iota(jnp.int32, (ng, gs, SUB), 1)
    gmax = jnp.max(sel3, axis=1, keepdims=True)
    first = jnp.min(jnp.where(sel3 == gmax, sub3, gs), axis=1, keepdims=True)
    gmax2 = jnp.max(jnp.where(sub3 == first, neg, sel3), axis=1, keepdims=True)
    gscore = (gmax + gmax2).reshape(ng, SUB)
    gidx = lax.broadcasted_iota(jnp.int32, (ng, SUB), 0)
    grank = jnp.zeros((ng, SUB), jnp.int32)
    for s in range(1, ng):
        other = pltpu.roll(gscore, s, 0)
        lower = gidx >= s
        grank += ((other > gscore) | ((other == gscore) & lower)).astype(jnp.int32)
    keep = jnp.broadcast_to((grank < 4).reshape(ng, 1, SUB), (ng, gs, SUB))
    masked = jnp.where(keep, sel3, neg).reshape(N_EXPERTS, SUB)
    eidx = lax.broadcasted_iota(jnp.int32, (N_EXPERTS, SUB), 0)
    rank = jnp.zeros((N_EXPERTS, SUB), jnp.int32)
    for s in range(1, N_EXPERTS):
        other = pltpu.roll(masked, s, 0)
        lower = eidx >= s
        rank += ((other > masked) | ((other == masked) & lower)).astype(jnp.int32)
    chosen = rank < TOP_K
    w_sel = jnp.where(chosen, scores, 0.0)
    w_sel = w_sel / jnp.sum(w_sel, axis=0, keepdims=True) * ROUTED_SCALE
    ri = lax.broadcasted_iota(jnp.int32, (SUB, SUB), 0)
    ci = lax.broadcasted_iota(jnp.int32, (SUB, SUB), 1)
    onehot = chosen.astype(BF16)
    pos = jnp.dot(onehot, (ri < ci).astype(BF16), preferred_element_type=F32) + run_ref[...]
    run_ref[...] = run_ref[...] + jnp.sum(chosen.astype(F32), axis=1, keepdims=True)
    cnt_ref[...] = run_ref[...]
    eidx_f = eidx.astype(F32)
    rows_e, rows_p, rows_w = [], [], []
    for kk in range(TOP_K):
        m = chosen & (rank == kk)
        rows_e.append(jnp.sum(jnp.where(m, eidx_f, 0.0), axis=0, keepdims=True))
        rows_p.append(jnp.sum(jnp.where(m, pos, 0.0), axis=0, keepdims=True))
        rows_w.append(jnp.sum(jnp.where(m, w_sel, 0.0), axis=0, keepdims=True))
    eid_ref[...] = jnp.concatenate(rows_e, axis=0).astype(jnp.int32)
    pos_ref[...] = jnp.concatenate(rows_p, axis=0).astype(jnp.int32)
    wt_ref[...] = jnp.concatenate(rows_w, axis=0)


def _router(logits, rb_col):
    tile = pl.BlockSpec((TOP_K, SUB), lambda i: (0, i))
    return pl.pallas_call(
        _router_body,
        grid=(N // SUB,),
        in_specs=[pl.BlockSpec((SUB, LANES), lambda i: (i, 0)),
                  pl.BlockSpec((N_EXPERTS, SUB), lambda i: (0, 0))],
        out_specs=[tile, tile, tile, pl.BlockSpec((N_EXPERTS, SUB), lambda i: (0, 0))],
        out_shape=[jax.ShapeDtypeStruct((TOP_K, N), jnp.int32), jax.ShapeDtypeStruct((TOP_K, N), jnp.int32),
                   jax.ShapeDtypeStruct((TOP_K, N), F32), jax.ShapeDtypeStruct((N_EXPERTS, SUB), F32)],
        scratch_shapes=[pltpu.VMEM((N_EXPERTS, SUB), F32)],
        compiler_params=_cparams(1), name="router")(logits, rb_col)


SC_CORES, SC_SUBCORES = 2, 16
SC_WORKERS = SC_CORES * SC_SUBCORES
SC_LANES = 16
SC_CHUNK = 16
SC_SCAN = N_ASSIGN // SC_WORKERS


def _sc_mesh():
    return plsc.VectorSubcoreMesh(core_axis_name="c", subcore_axis_name="s",
                                  num_cores=SC_CORES, num_subcores=SC_SUBCORES)


def _sc_worker_base(per_w):
    return (lax.axis_index("s") * SC_CORES + lax.axis_index("c")) * per_w


def _sc_gather_rows(table_hbm, out_hbm, idx_v, rows_v, gsem, wsem, base, per_w):
    n_ch = per_w // SC_CHUNK
    assert n_ch % 2 == 0

    def gather(j, p):
        off = pl.multiple_of(j * SC_CHUNK, SC_CHUNK)
        return pltpu.make_async_copy(table_hbm.at[idx_v.at[pl.ds(off, SC_CHUNK)]], rows_v.at[p], gsem.at[p])

    def write(j, p):
        off = pl.multiple_of(j * SC_CHUNK, SC_CHUNK)
        return pltpu.make_async_copy(rows_v.at[p], out_hbm.at[pl.ds(base + off, SC_CHUNK)], wsem.at[p])

    gather(0, 0).start()

    @pl.loop(0, n_ch, step=2)
    def _(j0):
        for p in range(2):
            j = j0 + p
            gather(j, p).wait()

            @pl.when(j >= 1)
            def _():
                write(j - 1, 1 - p).wait()

            @pl.when(j + 1 < n_ch)
            def _():
                gather(j + 1, 1 - p).start()
            write(j, p).start()

    write(n_ch - 1, 1).wait()


_SC_ROW_SCRATCH = [pltpu.VMEM((2, SC_CHUNK) + ROW_TILE, jnp.int32),
                   pltpu.SemaphoreType.DMA((2,)), pltpu.SemaphoreType.DMA((2,))]


def _sc_gather(table, idx):
    n_out = idx.shape[0]
    per_w = n_out // SC_WORKERS
    assert per_w * SC_WORKERS == n_out and per_w % (2 * SC_CHUNK) == 0

    def body(table_hbm, idx_hbm, out_hbm, idx_v, rows_v, gsem, wsem):
        base = _sc_worker_base(per_w)
        pltpu.sync_copy(idx_hbm.at[pl.ds(base, per_w)], idx_v)
        _sc_gather_rows(table_hbm, out_hbm, idx_v, rows_v, gsem, wsem, base, per_w)

    return pl.kernel(
        body, out_type=jax.ShapeDtypeStruct((n_out,) + ROW_TILE, jnp.int32), mesh=_sc_mesh(),
        scratch_types=[pltpu.VMEM((per_w,), jnp.int32)] + _SC_ROW_SCRATCH, name="sc_gather")(table, idx)


N_PARTS = 3
PART_BLOCKS = N_BLOCKS // N_PARTS
PART_SLOTS = PART_BLOCKS * EXP_BLOCK
assert PART_BLOCKS * N_PARTS == N_BLOCKS


def _sc_dispatch(table, slots, part):
    per_w = PART_SLOTS // SC_WORKERS
    assert per_w * SC_WORKERS == PART_SLOTS and per_w % (2 * SC_CHUNK) == 0 and SC_SCAN % SC_LANES == 0
    assert ZERO_ROWS & (ZERO_ROWS - 1) == 0

    def body(table_hbm, slots_hbm, out_hbm, idx_v, sl_v, rows_v, gsem, wsem):
        local = _sc_worker_base(per_w)
        base = part * PART_SLOTS + local
        lane = lax.iota(jnp.int32, SC_LANES)

        @pl.loop(0, per_w // SC_LANES)
        def _(j):
            off = pl.multiple_of(j * SC_LANES, SC_LANES)
            idx_v[pl.ds(off, SC_LANES)] = N + ((base + off + lane) & (ZERO_ROWS - 1))

        @pl.loop(0, N_ASSIGN // SC_SCAN)
        def _(c):
            pltpu.sync_copy(slots_hbm.at[pl.ds(pl.multiple_of(c * SC_SCAN, 8), SC_SCAN)], sl_v)

            @pl.loop(0, SC_SCAN // SC_LANES)
            def _(j):
                off = pl.multiple_of(j * SC_LANES, SC_LANES)
                loc = sl_v[pl.ds(off, SC_LANES)] - base
                mine = (loc >= 0) & (loc < per_w)
                tok = lax.shift_right_logical(c * SC_SCAN + off + lane, 3)
                plsc.store_scatter(idx_v, [jnp.where(mine, loc, 0)], tok, mask=mine)

        _sc_gather_rows(table_hbm, out_hbm, idx_v, rows_v, gsem, wsem, local, per_w)

    return pl.kernel(
        body, out_type=jax.ShapeDtypeStruct((PART_SLOTS,) + ROW_TILE, jnp.int32), mesh=_sc_mesh(),
        scratch_types=[pltpu.VMEM((per_w,), jnp.int32), pltpu.VMEM((SC_SCAN,), jnp.int32)] + _SC_ROW_SCRATCH,
        compiler_params=pltpu.CompilerParams(needs_layout_passes=False),
        name="sc_dispatch")(table, slots)


def _experts_body(be_ref, first_ref, par_ref, next_ref, nextblk_ref, nused_ref, x_ref, w1_hbm, w3_hbm, w2_hbm,
                  *rest, layer, part):
    y_ref, w1f, w3f, w2f, w1b, w3b, w2b, sem = rest[-8:]
    i = pl.program_id(0)
    b = part * PART_BLOCKS + i
    used = b < nused_ref[0]

    def copies(e, slot):
        return (pltpu.make_async_copy(w1_hbm.at[layer, e], w1f.at[slot], sem.at[0, slot]),
                pltpu.make_async_copy(w3_hbm.at[layer, e], w3f.at[slot], sem.at[1, slot]),
                pltpu.make_async_copy(w2_hbm.at[layer, e], w2f.at[slot], sem.at[2, slot]))

    @pl.when(used & (i == 0))
    def _():
        for c in copies(be_ref[b], par_ref[b]):
            c.start()

    @pl.when(used & ((i == 0) | (first_ref[b] == 1)))
    def _():
        slot = par_ref[b]
        for c in copies(be_ref[b], slot):
            c.wait()

        @pl.when((next_ref[b] >= 0) & (nextblk_ref[b] < (part + 1) * PART_BLOCKS))
        def _():
            for c in copies(next_ref[b], 1 - slot):
                c.start(priority=1)
        w1b[...] = w1f[slot].astype(BF16)
        w3b[...] = w3f[slot].astype(BF16)
        w2b[...] = w2f[slot].astype(BF16)

    @pl.when(used)
    def _():
        lo, hi = _unpack_bf16_pair(_load_row_tiles_2d(x_ref, EXP_BLOCK))
        lo = lo.astype(BF16)
        hi = hi.astype(BF16)
        half = D // 2
        h1 = (jnp.dot(lo, w1b[:half, :], preferred_element_type=F32)
              + jnp.dot(hi, w1b[half:, :], preferred_element_type=F32))
        h3 = (jnp.dot(lo, w3b[:half, :], preferred_element_type=F32)
              + jnp.dot(hi, w3b[half:, :], preferred_element_type=F32))
        hid = (_silu(h1) * h3).astype(BF16)
        y = jnp.dot(hid, w2b[...], preferred_element_type=F32)
        _store_row_tiles_2d(y_ref, _pack_bf16_pair(y[:, :half], y[:, half:]), EXP_BLOCK)

    @pl.when(jnp.logical_not(used))
    def _():
        y_ref[...] = jnp.zeros_like(y_ref)


def _experts(ctl, xs_part, w1, w3, w2, layer, part, ys_prev):
    def x_blk(i, *refs):
        n_here = jnp.clip(refs[-1][0] - part * PART_BLOCKS, 1, PART_BLOCKS)
        return (jnp.minimum(i, n_here - 1), 0)
    any_spec = pl.BlockSpec(memory_space=pl.ANY)
    in_specs = [pl.BlockSpec((EXP_BLOCK * ROW_TILE[0], LANES), x_blk), any_spec, any_spec, any_spec]
    args = [xs_part.reshape(PART_SLOTS * ROW_TILE[0], LANES), w1, w3, w2]
    aliases = {}
    if ys_prev is not None:
        in_specs.append(any_spec)
        args.append(ys_prev)
        aliases = {len(ctl) + 4: 0}
    grid_spec = pltpu.PrefetchScalarGridSpec(
        num_scalar_prefetch=len(ctl),
        grid=(PART_BLOCKS,),
        in_specs=in_specs,
        out_specs=pl.BlockSpec((EXP_BLOCK * ROW_TILE[0], LANES), lambda i, *refs: (part * PART_BLOCKS + i, 0)),
        scratch_shapes=[pltpu.VMEM((2, D, D_EXPERT), F32), pltpu.VMEM((2, D, D_EXPERT), F32),
                        pltpu.VMEM((2, D_EXPERT, D), F32),
                        pltpu.VMEM((D, D_EXPERT), BF16), pltpu.VMEM((D, D_EXPERT), BF16),
                        pltpu.VMEM((D_EXPERT, D), BF16), pltpu.SemaphoreType.DMA((3, 2))])
    return pl.pallas_call(
        functools.partial(_experts_body, layer=layer, part=part), grid_spec=grid_spec,
        out_shape=jax.ShapeDtypeStruct((L_SLOTS * ROW_TILE[0], LANES), jnp.int32),
        input_output_aliases=aliases,
        compiler_params=_cparams(1), name="experts")(*ctl, *args)


TM_FFN = 640


def _ffn_pre_body(x_ref, g_ref, shp_ref, shs_ref, scp_ref, scs_ref, rw_ref, w13_ref, w2_ref,
                  sh_ref, lg_ref, hp_ref, hb_ref):
    i = pl.program_id(0)

    @pl.when(i < N // TM_FFN)
    def _():
        for sidx in range(TM_FFN // SUB):
            rows = pl.ds(sidx * SUB, SUB)
            h = _prenorm_rows(i * (TM_FFN // SUB) + sidx, x_ref[rows, :], g_ref, shp_ref, shs_ref, scp_ref, scs_ref)
            lg_ref[rows, :] = jnp.dot(h, rw_ref[...], precision=HIGHEST, preferred_element_type=F32)
            packed = _pack_bf16_pair(h[:, :D // 2], h[:, D // 2:])
            for c in range(ROW_TILE[0]):
                hp_ref[pl.ds(sidx * SUB * ROW_TILE[0] + c, SUB, stride=ROW_TILE[0]), :] = (
                    packed[:, c * LANES:(c + 1) * LANES])
            hb_ref[rows, :] = h.astype(BF16)
        up = jnp.dot(hb_ref[...], w13_ref[...], preferred_element_type=F32)
        hid = (_silu(up[:, :D_EXPERT]) * up[:, D_EXPERT:]).astype(BF16)
        sh_ref[...] = jnp.dot(hid, w2_ref[...], preferred_element_type=F32)

    @pl.when(i >= N // TM_FFN)
    def _():
        hp_ref[...] = jnp.zeros_like(hp_ref)


def _ffn_pre(x, g, mod, layer, rw, w13, w2):
    last = N // TM_FFN - 1
    row = lambda i: (jnp.minimum(i, last), 0)
    mspec = lambda part, rows, blk: pl.BlockSpec((None, rows, D), lambda i, l=layer, p=part, b=blk: (l, b, p))
    full = lambda shape: pl.BlockSpec(shape, lambda i: (0, 0))
    return pl.pallas_call(
        _ffn_pre_body,
        grid=(N // TM_FFN + 1,),
        in_specs=[pl.BlockSpec((TM_FFN, D), row), full((1, D)),
                  mspec(3, 8, 0), mspec(3, SUB, 1), mspec(4, 8, 0), mspec(4, SUB, 1),
                  full((D, LANES)), full((D, 2 * D_EXPERT)), full((D_EXPERT, D))],
        out_specs=[pl.BlockSpec((TM_FFN, D), row), pl.BlockSpec((TM_FFN, LANES), row),
                   pl.BlockSpec((TM_FFN * ROW_TILE[0], LANES), lambda i: (i, 0))],
        out_shape=[jax.ShapeDtypeStruct((N, D), F32), jax.ShapeDtypeStruct((N, LANES), F32),
                   jax.ShapeDtypeStruct(((N + TM_FFN) * ROW_TILE[0], LANES), jnp.int32)],
        scratch_shapes=[pltpu.VMEM((TM_FFN, D), BF16)],
        compiler_params=_cparams(1), name="ffn_pre")(x, g.reshape(1, D), mod, mod, mod, mod, rw, w13, w2)


def _combine_body(g_ref, wt_ref, sh_ref, x_ref, gn_ref, gp_ref, gs_ref, *rest, tile0, n_out, final):
    outs = rest[-n_out:]
    half = D // 2
    acc_lo = sh_ref[:, :half]
    acc_hi = sh_ref[:, half:]
    wt = wt_ref[...]
    per_tok = TOP_K * ROW_TILE[0]
    for k in range(TOP_K):
        packed = jnp.concatenate([g_ref[pl.ds(k * ROW_TILE[0] + c, SUB, stride=per_tok), :]
                                  for c in range(ROW_TILE[0])], axis=1)
        lo, hi = _unpack_bf16_pair(packed)
        w_c = wt[:, k:k + 1]
        acc_lo = acc_lo + w_c * lo
        acc_hi = acc_hi + w_c * hi
    t = tile0 + pl.program_id(0)
    val = _post_value(t, x_ref[...], jnp.concatenate([acc_lo, acc_hi], axis=1), gn_ref, gp_ref, gs_ref)
    if not final:
        outs[0][...] = val
    else:
        @pl.when(t < N_P // SUB)
        def _():
            outs[0][...] = val
        if n_out == 2:
            @pl.when(t >= N_P // SUB)
            def _():
                outs[1][...] = val


def _combine(gathered, wts, shared, x, g_post, mod, layer, tile0, n_tiles, out_prev, final):
    per_tok = TOP_K * ROW_TILE[0]
    row = lambda i: (tile0 + i, 0)
    in_specs = [pl.BlockSpec((SUB * per_tok, LANES), lambda i: (i, 0)),
                pl.BlockSpec((SUB, LANES), row), pl.BlockSpec((SUB, D), row), pl.BlockSpec((SUB, D), row),
                pl.BlockSpec((1, D), lambda i: (0, 0)),
                pl.BlockSpec((None, 8, D), lambda i, l=layer: (l, 0, 5)),
                pl.BlockSpec((None, SUB, D), lambda i, l=layer: (l, 1, 5))]
    args = [gathered.reshape(n_tiles * SUB * per_tok, LANES), wts, shared, x, g_post.reshape(1, D), mod, mod]
    aliases = {}
    if out_prev is not None:
        in_specs.append(pl.BlockSpec(memory_space=pl.ANY))
        args.append(out_prev)
        aliases = {len(args) - 1: 0}
    if not final:
        out_specs = [pl.BlockSpec((SUB, D), row)]
        out_shape = [jax.ShapeDtypeStruct((N, D), F32)]
    else:
        last_p = N_P // SUB - 1
        out_specs = [pl.BlockSpec((SUB, D), lambda i: (jnp.minimum(tile0 + i, last_p), 0))]
        out_shape = [jax.ShapeDtypeStruct((N_P, D), F32)]
        if tile0 + n_tiles > N_P // SUB:
            out_specs.append(pl.BlockSpec((N_S, D), lambda i: (0, 0)))
            out_shape.append(jax.ShapeDtypeStruct((N_S, D), F32))
    return pl.pallas_call(
        functools.partial(_combine_body, tile0=tile0, n_out=len(out_shape), final=final),
        grid=(n_tiles,),
        in_specs=in_specs, out_specs=out_specs, out_shape=out_shape,
        input_output_aliases=aliases,
        compiler_params=_cparams(1), name="combine")(*args)


def _slots_body(start_ref, eid_ref, pos_ref, o_ref):
    eid = eid_ref[...]
    acc = pos_ref[...]
    for e in range(N_EXPERTS):
        acc = acc + jnp.where(eid == e, start_ref[e], 0)
    o_ref[...] = acc


def _slots(pad_start, eid, pos):
    grid_spec = pltpu.PrefetchScalarGridSpec(
        num_scalar_prefetch=1, grid=(1,),
        in_specs=[pl.BlockSpec((TOP_K, N), lambda i, s: (0, 0)), pl.BlockSpec((TOP_K, N), lambda i, s: (0, 0))],
        out_specs=pl.BlockSpec((TOP_K, N), lambda i, s: (0, 0)))
    return pl.pallas_call(_slots_body, grid_spec=grid_spec,
                          out_shape=jax.ShapeDtypeStruct((TOP_K, N), jnp.int32),
                          compiler_params=_cparams(1), name="slots")(pad_start, eid, pos)


def _put_sample_rows_body(*refs):
    n = len(refs) // 3
    for src, dst in zip(refs[:n], refs[2 * n:]):
        dst[...] = src[...].astype(BF16)


def _put_sample_rows(sample_rows, full):
    n = len(full)
    return pl.pallas_call(
        _put_sample_rows_body,
        grid=(1,),
        in_specs=[pl.BlockSpec((N_S, 512), lambda i: (0, 0))] * n + [pl.BlockSpec(memory_space=pl.ANY)] * n,
        out_specs=[pl.BlockSpec((N_S, 512), lambda i: (N_P // N_S, 0))] * n,
        out_shape=[jax.ShapeDtypeStruct((N, 512), BF16)] * n,
        input_output_aliases={n + k: k for k in range(n)},
        compiler_params=_cparams(1), name="put_sample_rows")(*sample_rows, *full)


COMBINE_RANGES = ((0, 33), (33, 32))

def _reorder_w_in(w):
    return jnp.concatenate([w[:, :1536], w[:, 1552:]], axis=1), jnp.pad(w[:, 1536:1552], ((0, 0), (0, LANES - 16)))


def _rope_tables(pos):
    half = DK // 2
    inv = ROPE_BASE ** (-jnp.arange(half, dtype=F32) / half)
    ang = pos.astype(F32)[:, None] * inv[None, :]
    cos = jnp.cos(ang)
    sin = jnp.sin(ang)
    return jnp.concatenate([cos, cos], axis=1), jnp.concatenate([-sin, sin], axis=1)


def _layer(l, x, mod, s_gla, s_pool, s_ret, wts, final):
    (norm_mix_pre, norm_mix_post, norm_ffn_pre, norm_ffn_post, w_in, w_gla_gate, b_gla_gate, gla_norm,
     pool_w, pool_scale, ret_norm, sgu_norm, sgu_w, sgu_b, w_branch, w_merge_gate, b_merge_gate, w_out,
     router_w, router_bias, expert_w1, expert_w3, expert_w2, shared_w1, shared_w3, shared_w2) = wts

    w_main, w_low = _reorder_w_in(w_in[l])
    h = _prenorm(x, norm_mix_pre[l], mod, l)
    p_main = _matmul(h, w_main.astype(BF16), 1664, 512, name="inproj")
    p_low = _matmul(h, w_low.astype(BF16), 1664, LANES, name="inproj_low")

    w_gate_pad = jnp.pad(w_gla_gate[l], ((0, LANES - 16), (0, 0)))
    b_gate = b_gla_gate[l].reshape(1, HEADS * DK)
    log_gamma = jnp.log1p(-jnp.exp2(-5.0 - jnp.arange(HEADS, dtype=F32)))
    dec_row = jnp.repeat(log_gamma, DK).reshape(1, HEADS * DK)
    cos_p, sin_p = _rope_tables(jnp.arange(T_P))
    cos_p = jnp.tile(cos_p, (1, 2))
    sin_p = jnp.tile(sin_p, (1, 2))
    g_gla = gla_norm[l].reshape(1, HEADS * DV)
    g_ret = ret_norm[l].reshape(1, HEADS * DV)

    oa_p, gla_p = _la_prompt(p_main, C_GQ, C_GK, C_GV, C_GR, p_low, p_low, w_gate_pad, b_gate, g_gla, False)
    oc_p, ret_p = _la_prompt(p_main, C_RQ, C_RK, C_RV, C_RG, cos_p, sin_p, dec_row, b_gate, g_ret, True)
    pw_bf = pool_w[l].astype(BF16)
    pscale = pool_scale[l].reshape(1, 512)
    ob_p = _pool_prompt(p_main, pw_bf, pscale)
    sgu_g = sgu_norm[l].reshape(1, 512)
    od_p = _sgu_prompt(p_main, sgu_g, sgu_w[l], jnp.pad(sgu_b[l].T, ((0, 0), (0, LANES - 4))))

    ps = p_main[N_P:]
    q_t = _to_tiles_t(ps[:, C_GQ:C_GQ + 256])
    k_t = _to_tiles_t(ps[:, C_GK:C_GK + 256])
    glow_t = jnp.pad(_to_tiles_t(p_low[N_P:, :16]), ((0, 0), (0, LANES - 16), (0, 0)))
    w_gate_t = jnp.pad(w_gla_gate[l].T, ((0, 0), (0, LANES - 16)))
    b_col = jnp.broadcast_to(b_gla_gate[l][:, None], (HEADS * DK, LANES))
    logit_t = _gate_logits_t(w_gate_t, glow_t, b_col)
    dummy = jnp.zeros((HEADS * DK, LANES), F32)
    oa_s, gla_s = _la_sample(q_t, k_t, logit_t, dummy, dummy, p_main, C_GV, C_GR, g_gla, s_gla[l], False)
    cos_s, sin_s = _rope_tables(jnp.full((1,), PAST_LEN))
    cos_c = jnp.broadcast_to(jnp.tile(cos_s[0], HEADS)[:, None], (HEADS * DK, LANES))
    sin_c = jnp.broadcast_to(jnp.tile(sin_s[0], HEADS)[:, None], (HEADS * DK, LANES))
    dec_c = jnp.broadcast_to(jnp.repeat(log_gamma, DK)[:, None], (HEADS * DK, LANES))
    rq_t = _to_tiles_t(ps[:, C_RQ:C_RQ + 256])
    rk_t = _to_tiles_t(ps[:, C_RK:C_RK + 256])
    oc_s, ret_s = _la_sample(rq_t, rk_t, dec_c, cos_c, sin_c, p_main, C_RV, C_RG, g_ret, s_ret[l], True)
    sgu_w0 = jnp.repeat(sgu_w[l][:, 0, 0], LANES).reshape(1, 512)
    sgu_b0 = jnp.repeat(sgu_b[l][:, 0], LANES).reshape(1, 512)
    ob_s, od_s, vn_s = _small_sample(p_main, s_pool[l], pw_bf, pscale, sgu_g, sgu_w0, sgu_b0)
    pool_p = p_main[:N_P, C_PIN:C_PIN + 512].reshape(B_P, T_P, 512)[:, T_P - POOL_BUF:]
    pool_s = jnp.concatenate([s_pool[l][:, 1:], ps[:, None, C_PIN:C_PIN + 512]], axis=1)

    branches = _put_sample_rows([oa_s, ob_s, oc_s, od_s], [oa_p, ob_p, oc_p, od_p])
    merged = _merge(h, branches, w_merge_gate[l].astype(BF16), b_merge_gate[l].reshape(1, 4 * D),
                    w_branch[l].astype(BF16))
    x = _outproj(merged, w_out[l].astype(BF16), x, norm_mix_post[l], mod, l)

    rw = jnp.pad(router_w[l], ((0, 0), (0, LANES - N_EXPERTS)))
    rb = jnp.broadcast_to(router_bias[l][:, None], (N_EXPERTS, SUB))
    w13 = jnp.concatenate([shared_w1[l], shared_w3[l]], axis=1).astype(BF16)
    shared, logits, h2_packed = _ffn_pre(x, norm_ffn_pre[l], mod, l, rw, w13, shared_w2[l].astype(BF16))
    eid, pos, wt, counts = _router(logits, rb)
    counts = counts[:, 0].astype(jnp.int32)
    padded = (counts + EXP_BLOCK - 1) // EXP_BLOCK * EXP_BLOCK
    pad_end = jnp.cumsum(padded)
    pad_start = pad_end - padded
    nused = (pad_end[-1] // EXP_BLOCK).astype(jnp.int32).reshape(1)
    blk_row = jnp.arange(N_BLOCKS, dtype=jnp.int32) * EXP_BLOCK
    block_e = jnp.minimum(jnp.sum((blk_row[:, None] >= pad_end[None, :]).astype(jnp.int32), axis=1),
                          N_EXPERTS - 1)
    first = jnp.concatenate([jnp.ones((1,), jnp.int32), (block_e[1:] != block_e[:-1]).astype(jnp.int32)])
    first = jnp.where(blk_row < pad_end[-1], first, 0)
    par = (jnp.cumsum(first) - 1) % 2
    live = jnp.where(padded > 0, jnp.arange(N_EXPERTS), N_EXPERTS)
    after = jnp.concatenate([lax.cummin(live, reverse=True)[1:], jnp.full((1,), N_EXPERTS)])
    next_e = jnp.where(after < N_EXPERTS, after, -1)[block_e].astype(jnp.int32)
    next_blk = jnp.where(next_e >= 0, (pad_start // EXP_BLOCK)[jnp.maximum(next_e, 0)], N_BLOCKS).astype(jnp.int32)
    slots = _slots(pad_start.astype(jnp.int32), eid, pos).T.reshape(N_ASSIGN)
    wt = jnp.pad(wt.T, ((0, 0), (0, LANES - TOP_K)))
    table = h2_packed.reshape((N + TM_FFN,) + ROW_TILE)
    ctl = (block_e, first, par.astype(jnp.int32), next_e, next_blk, nused)
    xs_parts = [_sc_dispatch(table, slots, part) for part in range(N_PARTS)]
    ys = None
    for part in range(N_PARTS):
        ys = _experts(ctl, xs_parts[part], expert_w1, expert_w3, expert_w2, l, part, ys)
    ys = ys.reshape((L_SLOTS,) + ROW_TILE)
    outs = [None]
    for tile0, n_tiles in COMBINE_RANGES:
        a0, a1 = tile0 * SUB * TOP_K, (tile0 + n_tiles) * SUB * TOP_K
        outs = _combine(_sc_gather(ys, slots[a0:a1]), wt, shared, x, norm_ffn_post[l], mod, l, tile0, n_tiles,
                        outs[0], final)
    x = tuple(outs) if final else outs[0]
    return x, (gla_p, gla_s, pool_p, pool_s, ret_p, ret_s, vn_s)


def kernel(x_prompt, x_sample, c_prompt, c_sample, state_gla, state_pool, state_ret, w_ada, b_ada, norm_mix_pre, norm_mix_post, norm_ffn_pre, norm_ffn_post, w_in, w_gla_gate, b_gla_gate, gla_norm, pool_w, pool_scale, ret_norm, sgu_norm, sgu_w, sgu_b, w_branch, w_merge_gate, b_merge_gate, w_out, router_w, router_bias, expert_w1, expert_w3, expert_w2, shared_w1, shared_w3, shared_w2):
    wts = (norm_mix_pre, norm_mix_post, norm_ffn_pre, norm_ffn_post, w_in, w_gla_gate, b_gla_gate, gla_norm,
           pool_w, pool_scale, ret_norm, sgu_norm, sgu_w, sgu_b, w_branch, w_merge_gate, b_merge_gate, w_out,
           router_w, router_bias, expert_w1, expert_w3, expert_w2, shared_w1, shared_w3, shared_w2)
    c_all = jnp.zeros((MOD_ROWS, D), F32).at[:B_P].set(c_prompt).at[SUB:SUB + N_S].set(c_sample)
    mod = _ada(c_all, w_ada, b_ada)
    x = (x_prompt.reshape(N_P, D), x_sample.reshape(N_S, D))
    per_layer = []
    for l in range(DEPTH):
        x, states = _layer(l, x, mod, state_gla, state_pool, state_ret, wts, l == DEPTH - 1)
        per_layer.append(states)
    gla_p, gla_s, pool_p, pool_s, ret_p, ret_s, vn_s = (jnp.stack(z) for z in zip(*per_layer))
    return (x[0].reshape(B_P, T_P, D), x[1].reshape(N_S, 1, D),
            gla_p, gla_s, pool_p, pool_s, ret_p, ret_s, vn_s.reshape(DEPTH, N_S, 1, 512))
```

```python
import functools

import jax
import jax.numpy as jnp
from jax import lax
from jax.experimental import pallas as pl
from jax.experimental.pallas import tpu as pltpu
from jax.experimental.pallas import tpu_sc as plsc

F32 = jnp.float32
BF16 = jnp.bfloat16
HIGHEST = lax.Precision.HIGHEST

D = 2048
B_P, T_P = 4, 2048
N_P = B_P * T_P
N_S = 128
N = N_P + N_S
DEPTH = 2
PAST_LEN = 16384
EPS = 1e-6
HEADS, DK, DV = 4, 64, 128
CHUNK = 64
GATE_TEMP = 16.0
POOL_WINDOWS = (2, 4, 8, 16)
POOL_BUF = 15
ROPE_BASE = 10000.0
N_EXPERTS = 64
TOP_K = 8
D_EXPERT = 512
ROUTED_SCALE = 2.5

LANES = 128
SUB = 128
MOD_ROWS = 256
EXP_BLOCK = 256
N_ASSIGN = N * TOP_K
N_BLOCKS = -(-(N_ASSIGN + N_EXPERTS * (EXP_BLOCK - 1)) // EXP_BLOCK)
L_SLOTS = N_BLOCKS * EXP_BLOCK
VMEM_LIMIT = 56 * 1024 * 1024

C_GQ, C_GK, C_GV, C_GR, C_PIN, C_RQ, C_RK, C_RV, C_RG, C_SU, C_SV = (
    0, 256, 512, 1024, 1536, 2048, 2304, 2560, 3072, 3584, 4096)
P_MAIN = 4608


def _cparams(n_axes=1):
    return pltpu.CompilerParams(dimension_semantics=("arbitrary",) * n_axes,
                                vmem_limit_bytes=VMEM_LIMIT)


def _silu(x):
    return x * jax.nn.sigmoid(x)


def _mod_rows(t, mp_ref, ms_ref):
    b = jnp.minimum(t // (T_P // SUB), B_P - 1)
    return jnp.where(t >= N_P // SUB, ms_ref[...], mp_ref[pl.ds(b, 1), :])


def _mod_specs(layer, part):
    return [pl.BlockSpec((None, 8, D), lambda i, l=layer, p=part: (l, 0, p)),
            pl.BlockSpec((None, SUB, D), lambda i, l=layer, p=part: (l, 1, p))]


def _pack_bf16_pair(lo, hi):
    lo_u = lax.bitcast_convert_type(lo.astype(BF16).astype(F32), jnp.uint32)
    hi_u = lax.bitcast_convert_type(hi.astype(BF16).astype(F32), jnp.uint32)
    return lax.bitcast_convert_type((hi_u & jnp.uint32(0xFFFF0000)) | (lo_u >> 16), jnp.int32)


def _unpack_bf16_pair(w):
    u = lax.bitcast_convert_type(w, jnp.uint32)
    lo = lax.bitcast_convert_type(u << 16, F32)
    hi = lax.bitcast_convert_type(u & jnp.uint32(0xFFFF0000), F32)
    return lo, hi


ROW_TILE = (8, LANES)


def _load_row_tiles(ref):
    return jnp.concatenate([ref[:, c, :] for c in range(ROW_TILE[0])], axis=1)


def _store_row_tiles(ref, val):
    for c in range(ROW_TILE[0]):
        ref[:, c, :] = val[:, c * LANES:(c + 1) * LANES]


def _load_row_tiles_2d(ref, rows):
    return jnp.concatenate([ref[pl.ds(c, rows, stride=ROW_TILE[0]), :] for c in range(ROW_TILE[0])], axis=1)


def _store_row_tiles_2d(ref, val, rows):
    for c in range(ROW_TILE[0]):
        ref[pl.ds(c, rows, stride=ROW_TILE[0]), :] = val[:, c * LANES:(c + 1) * LANES]


def _ada_body(c_ref, w_ref, b_ref, o_ref):
    s = _silu(c_ref[...]).astype(BF16)
    o_ref[...] = jnp.dot(s, w_ref[...].astype(BF16), preferred_element_type=F32) + b_ref[...]


def _ada(c_all, w_ada, b_ada):
    tn = 1024
    return pl.pallas_call(
        _ada_body,
        grid=(DEPTH, 6 * D // tn),
        in_specs=[pl.BlockSpec((MOD_ROWS, D), lambda l, j: (0, 0)),
                  pl.BlockSpec((None, D, tn), lambda l, j: (l, 0, j)),
                  pl.BlockSpec((None, 1, tn), lambda l, j: (l, 0, j))],
        out_specs=pl.BlockSpec((None, MOD_ROWS, tn), lambda l, j: (l, 0, j)),
        out_shape=jax.ShapeDtypeStruct((DEPTH, MOD_ROWS, 6 * D), F32),
        compiler_params=_cparams(2), name="ada")(c_all, w_ada, b_ada.reshape(DEPTH, 1, 6 * D))


ZERO_ROWS = 2 * SUB


def _x_specs(x, tm, n_axes):
    row = (lambda i: (i, 0)) if n_axes == 1 else (lambda i, j: (i, 0))
    if not isinstance(x, tuple):
        return [pl.BlockSpec((tm, D), row)], [x]
    zero = (lambda i: (0, 0)) if n_axes == 1 else (lambda i, j: (0, 0))
    return [pl.BlockSpec((tm, D), row), pl.BlockSpec((N_S, D), zero)], list(x)


def _x_rows(x_refs, t, rows):
    if len(x_refs) == 1:
        return x_refs[0][rows, :]
    return jnp.where(t >= N_P // SUB, x_refs[1][...], x_refs[0][rows, :])


def _prenorm_rows(t, x, g_ref, shp_ref, shs_ref, scp_ref, scs_ref):
    y = x * lax.rsqrt(jnp.mean(x * x, axis=-1, keepdims=True) + EPS) * g_ref[...]
    return y * (1.0 + _mod_rows(t, scp_ref, scs_ref)) + _mod_rows(t, shp_ref, shs_ref)


TM_NORM = 640


def _prenorm_body(*refs):
    g_ref, shp_ref, shs_ref, scp_ref, scs_ref, h_ref = refs[-6:]
    for sidx in range(TM_NORM // SUB):
        rows = pl.ds(sidx * SUB, SUB)
        t = pl.program_id(0) * (TM_NORM // SUB) + sidx
        h_ref[rows, :] = _prenorm_rows(t, _x_rows(refs[:-6], t, rows), g_ref, shp_ref, shs_ref, scp_ref,
                                       scs_ref).astype(BF16)


def _prenorm(x, g, mod, layer):
    x_specs, x_args = _x_specs(x, TM_NORM, 1)
    return pl.pallas_call(
        _prenorm_body,
        grid=(N // TM_NORM,),
        in_specs=x_specs + [pl.BlockSpec((1, D), lambda i: (0, 0))] + _mod_specs(layer, 0) + _mod_specs(layer, 1),
        out_specs=pl.BlockSpec((TM_NORM, D), lambda i: (i, 0)),
        out_shape=jax.ShapeDtypeStruct((N, D), BF16),
        compiler_params=_cparams(1), name="prenorm")(*x_args, g.reshape(1, D), mod, mod, mod, mod)


def _mm_body(x_ref, w_ref, o_ref):
    o_ref[...] = jnp.dot(x_ref[...], w_ref[...], preferred_element_type=F32).astype(o_ref.dtype)


def _matmul(x, w, tm, tn, out_dtype=F32, name="mm"):
    m, k = x.shape
    n = w.shape[1]
    return pl.pallas_call(
        _mm_body,
        grid=(m // tm, n // tn),
        in_specs=[pl.BlockSpec((tm, k), lambda i, j: (i, 0)),
                  pl.BlockSpec((k, tn), lambda i, j: (0, j))],
        out_specs=pl.BlockSpec((tm, tn), lambda i, j: (i, j)),
        out_shape=jax.ShapeDtypeStruct((m, n), out_dtype),
        compiler_params=_cparams(2), name=name)(x, w)


ROWS_LA = 256


def _swap_halves_lanes(x):
    lane = lax.broadcasted_iota(jnp.int32, x.shape, 1)
    return jnp.where((lane % 64) < 32, pltpu.roll(x, 96, 1), pltpu.roll(x, 32, 1))


def _rope_lanes(x, cos, sin_signed):
    parts = []
    for half in range(2):
        xh = x[:, half * LANES:(half + 1) * LANES]
        parts.append(xh * cos + _swap_halves_lanes(xh) * sin_signed)
    return jnp.concatenate(parts, axis=1)


def _la_prompt_body(q_ref, k_ref, v_ref, r_ref, aux_ref, aux2_ref, dec_ref, bias_ref, g_ref,
                    o_ref, st_out_ref, st_ref, *, retention):
    t = pl.program_id(1)

    @pl.when(t == 0)
    def _():
        st_ref[...] = jnp.zeros_like(st_ref)

    ri = lax.broadcasted_iota(jnp.int32, (CHUNK, CHUNK), 0)
    ci = lax.broadcasted_iota(jnp.int32, (CHUNK, CHUNK), 1)
    causal = ri >= ci
    tril = causal.astype(F32)
    scale = DK ** -0.5

    for c in range(ROWS_LA // CHUNK):
        rows = pl.ds(c * CHUNK, CHUNK)
        q = q_ref[rows, :]
        k = k_ref[rows, :]
        v = v_ref[rows, :]
        if retention:
            cos = aux_ref[rows, :]
            sin = aux2_ref[rows, :]
            q = _rope_lanes(q, cos, sin)
            k = _rope_lanes(k, cos, sin) * scale
            la = jnp.broadcast_to(dec_ref[...], (CHUNK, HEADS * DK))
        else:
            q = q * scale
            logit = jnp.dot(aux_ref[rows, :], dec_ref[...], precision=HIGHEST,
                            preferred_element_type=F32) + bias_ref[...]
            la = jax.nn.log_sigmoid(logit) / GATE_TEMP
        bc = jnp.dot(tril, la, precision=HIGHEST, preferred_element_type=F32)
        bl = bc[CHUNK - 1:CHUNK, :]
        qd = q * jnp.exp(bc)
        ki = k * jnp.exp(-bc)
        ke = k * jnp.exp(bl - bc)
        ac = jnp.exp(bl)
        outs = []
        for h in range(HEADS):
            ks = slice(h * DK, (h + 1) * DK)
            vs = slice(h * DV, (h + 1) * DV)
            qd_h = qd[:, ks].astype(BF16)
            ki_h = ki[:, ks].astype(BF16)
            ke_h = ke[:, ks].astype(BF16)
            v_h = v[:, vs].astype(BF16)
            sc = lax.dot_general(qd_h, ki_h, (((1,), (1,)), ((), ())), preferred_element_type=F32)
            sc = jnp.where(causal, sc, 0.0)
            o_h = jnp.dot(sc.astype(BF16), v_h, preferred_element_type=F32)
            st = st_ref[h]
            o_h = o_h + lax.dot_general(qd_h, st.astype(BF16), (((1,), (1,)), ((), ())),
                                        preferred_element_type=F32)
            kv_t = lax.dot_general(v_h, ke_h, (((0,), (0,)), ((), ())), preferred_element_type=F32)
            st_ref[h] = st * ac[:, ks] + kv_t
            o_n = o_h * lax.rsqrt(jnp.mean(o_h * o_h, axis=-1, keepdims=True) + EPS) * g_ref[:, vs]
            outs.append(o_n)
        o = jnp.concatenate(outs, axis=1) * _silu(r_ref[rows, :])
        o_ref[rows, :] = o.astype(BF16)

    st_out_ref[...] = st_ref[...]


def _la_prompt(p_main, cq, ck, cv, cr, aux, aux2, dec, bias, g, retention):
    nt = T_P // ROWS_LA
    rowblk = lambda b, t: b * nt + t
    if retention:
        aux_specs = [pl.BlockSpec((ROWS_LA, LANES), lambda b, t: (t, 0)),
                     pl.BlockSpec((ROWS_LA, LANES), lambda b, t: (t, 0))]
    else:
        aux_specs = [pl.BlockSpec((ROWS_LA, LANES), lambda b, t: (rowblk(b, t), 0)),
                     pl.BlockSpec((8, LANES), lambda b, t: (0, 0))]
    o, st = pl.pallas_call(
        functools.partial(_la_prompt_body, retention=retention),
        grid=(B_P, nt),
        in_specs=[pl.BlockSpec((ROWS_LA, 256), lambda b, t: (rowblk(b, t), cq // 256)),
                  pl.BlockSpec((ROWS_LA, 256), lambda b, t: (rowblk(b, t), ck // 256)),
                  pl.BlockSpec((ROWS_LA, 512), lambda b, t: (rowblk(b, t), cv // 512)),
                  pl.BlockSpec((ROWS_LA, 512), lambda b, t: (rowblk(b, t), cr // 512))]
        + aux_specs
        + [pl.BlockSpec(dec.shape, lambda b, t: (0, 0)),
           pl.BlockSpec((1, HEADS * DK), lambda b, t: (0, 0)),
           pl.BlockSpec((1, HEADS * DV), lambda b, t: (0, 0))],
        out_specs=[pl.BlockSpec((ROWS_LA, HEADS * DV), lambda b, t: (rowblk(b, t), 0)),
                   pl.BlockSpec((None, HEADS, DV, DK), lambda b, t: (b, 0, 0, 0))],
        out_shape=[jax.ShapeDtypeStruct((N, HEADS * DV), BF16),
                   jax.ShapeDtypeStruct((B_P, HEADS, DV, DK), F32)],
        scratch_shapes=[pltpu.VMEM((HEADS, DV, DK), F32)],
        compiler_params=_cparams(2), name="ret_prompt" if retention else "gla_prompt",
    )(p_main, p_main, p_main, p_main, aux, aux2, dec, bias, g)
    return o, jnp.swapaxes(st, -1, -2)


SAMPLE_TILE = 8


def _la_sample_body(qt_ref, kt_ref, lt_ref, cos_ref, sin_ref, v_ref, r_ref, g_ref, s_ref, *rest, retention):
    o_ref, s_out_ref = rest[-2:]
    scale = DK ** -0.5
    qt = qt_ref[...]
    kt = kt_ref[...]
    if retention:
        def rope(x):
            sw = jnp.concatenate(
                [x[h * DK + (DK // 2) * (1 - j): h * DK + (DK // 2) * (2 - j), :]
                 for h in range(HEADS) for j in range(2)], axis=0)
            return x * cos_ref[...] + sw * sin_ref[...]
        qt = rope(qt)
        kt = rope(kt) * scale
        la = lt_ref[...]
    else:
        qt = qt * scale
        la = jax.nn.log_sigmoid(lt_ref[...]) / GATE_TEMP
    at = jnp.exp(la)
    qd = qt * at
    ki = kt * jnp.exp(-la)
    prod = qd * ki
    v8 = v_ref[...]
    r8 = r_ref[...]
    g = g_ref[...]
    for j in range(SAMPLE_TILE):
        for h in range(HEADS):
            ks = slice(h * DK, (h + 1) * DK)
            vs = slice(h * DV, (h + 1) * DV)
            a_c = jnp.broadcast_to(at[ks, j:j + 1], (DK, DV))
            k_c = jnp.broadcast_to(kt[ks, j:j + 1], (DK, DV))
            q_c = jnp.broadcast_to(qd[ks, j:j + 1], (DK, DV))
            s_c = jnp.broadcast_to(jnp.sum(prod[ks, j:j + 1], axis=0, keepdims=True), (1, DV))
            s0 = s_ref[j, h]
            v_row = v8[j:j + 1, vs]
            s_out_ref[j, h] = a_c * s0 + k_c * v_row
            o_row = s_c * v_row + jnp.sum(q_c * s0, axis=0, keepdims=True)
            o_n = o_row * lax.rsqrt(jnp.mean(o_row * o_row, axis=-1, keepdims=True) + EPS) * g[:, vs]
            o_ref[j:j + 1, vs] = o_n * _silu(r8[j:j + 1, vs])


def _la_sample(qt, kt, lt, cos_t, sin_t, p_main, cv, cr, g, s0_all, layer, s_prev, retention):
    nt = N_S // SAMPLE_TILE
    row0 = N_P // SAMPLE_TILE
    tile = pl.BlockSpec((None, HEADS * DK, LANES), lambda i: (i, 0, 0))
    full = pl.BlockSpec((HEADS * DK, LANES), lambda i: (0, 0))
    lt_spec = full if retention else tile
    return pl.pallas_call(
        functools.partial(_la_sample_body, retention=retention),
        grid=(nt,),
        in_specs=[tile, tile, lt_spec, full, full,
                  pl.BlockSpec((SAMPLE_TILE, 512), lambda i: (row0 + i, cv // 512)),
                  pl.BlockSpec((SAMPLE_TILE, 512), lambda i: (row0 + i, cr // 512)),
                  pl.BlockSpec((1, HEADS * DV), lambda i: (0, 0)),
                  pl.BlockSpec((None, SAMPLE_TILE, HEADS, DK, DV), lambda i, l=layer: (l, i, 0, 0, 0))]
        + ([] if s_prev is None else [pl.BlockSpec(memory_space=pl.ANY)]),
        out_specs=[pl.BlockSpec((SAMPLE_TILE, HEADS * DV), lambda i: (i, 0)),
                   pl.BlockSpec((None, SAMPLE_TILE, HEADS, DK, DV), lambda i, l=layer: (l, i, 0, 0, 0))],
        out_shape=[jax.ShapeDtypeStruct((N_S, HEADS * DV), F32),
                   jax.ShapeDtypeStruct((DEPTH, N_S, HEADS, DK, DV), F32)],
        input_output_aliases={} if s_prev is None else {9: 1},
        compiler_params=_cparams(1), name="ret_sample" if retention else "gla_sample",
    )(qt, kt, lt, cos_t, sin_t, p_main, p_main, g, s0_all, *([] if s_prev is None else [s_prev]))


def _gate_logits_t_body(w_ref, x_ref, b_ref, o_ref):
    o_ref[...] = jnp.dot(w_ref[...], x_ref[...], precision=HIGHEST, preferred_element_type=F32) + b_ref[...]


def _gate_logits_t(w_gate_t, glow_t, b_col):
    nt = N_S // SAMPLE_TILE
    return pl.pallas_call(
        _gate_logits_t_body,
        grid=(nt,),
        in_specs=[pl.BlockSpec((HEADS * DK, LANES), lambda i: (0, 0)),
                  pl.BlockSpec((None, LANES, LANES), lambda i: (i, 0, 0)),
                  pl.BlockSpec((HEADS * DK, LANES), lambda i: (0, 0))],
        out_specs=pl.BlockSpec((None, HEADS * DK, LANES), lambda i: (i, 0, 0)),
        out_shape=jax.ShapeDtypeStruct((nt, HEADS * DK, LANES), F32),
        compiler_params=_cparams(1), name="gate_logits_t")(w_gate_t, glow_t, b_col)


def _to_tiles_t(x):
    c = x.shape[1]
    xt = jnp.swapaxes(x.reshape(N_S // SAMPLE_TILE, SAMPLE_TILE, c), 1, 2)
    return jnp.pad(xt, ((0, 0), (0, 0), (0, LANES - SAMPLE_TILE)))


ROWS_POOL = 512


def _pool_mix(y, w_ref, sc_ref):
    outs = []
    for gi in range(4):
        cs = slice(gi * LANES, (gi + 1) * LANES)
        outs.append(jnp.dot(y[:, cs].astype(BF16), w_ref[gi], preferred_element_type=F32))
    return jnp.concatenate(outs, axis=1) * sc_ref[...]


def _pool_prompt_body(p_ref, halo_ref, w_ref, sc_ref, o_ref):
    t = pl.program_id(1)
    p = p_ref[...]
    halo = jnp.where(t == 0, 0.0, halo_ref[...])
    full = jnp.concatenate([halo, p], axis=0)
    pos = t * ROWS_POOL + lax.broadcasted_iota(jnp.int32, (ROWS_POOL, LANES), 0)
    means = []
    for gi, w in enumerate(POOL_WINDOWS):
        s = full[:, gi * LANES:(gi + 1) * LANES]
        step = 1
        while step < w:
            s = s + pltpu.roll(s, step, 0)
            step *= 2
        win = s[16:, :]
        cnt = jnp.minimum(w, pos + 1).astype(F32)
        means.append(win / cnt)
    y = jnp.concatenate(means, axis=1) - p
    o_ref[...] = _pool_mix(y, w_ref, sc_ref).astype(BF16)


def _pool_prompt(p_main, w_bf, scale):
    nt = T_P // ROWS_POOL
    return pl.pallas_call(
        _pool_prompt_body,
        grid=(B_P, nt),
        in_specs=[pl.BlockSpec((ROWS_POOL, 512), lambda b, t: (b * nt + t, C_PIN // 512)),
                  pl.BlockSpec((16, 512), lambda b, t: (jnp.maximum((b * nt + t) * (ROWS_POOL // 16) - 1, 0),
                                                        C_PIN // 512)),
                  pl.BlockSpec((4, LANES, LANES), lambda b, t: (0, 0, 0)),
                  pl.BlockSpec((1, 512), lambda b, t: (0, 0))],
        out_specs=pl.BlockSpec((ROWS_POOL, 512), lambda b, t: (b * nt + t, 0)),
        out_shape=jax.ShapeDtypeStruct((N, 512), BF16),
        compiler_params=_cparams(2), name="pool_prompt")(p_main, p_main, w_bf, scale)


def _small_sample_body(p_ref, buf_ref, pw_ref, psc_ref, u_ref, sv_ref, sg_ref, sw_ref, sb_ref,
                       ob_ref, od_ref, vn_ref):
    p = p_ref[...]
    means = []
    for gi, w in enumerate(POOL_WINDOWS):
        cs = slice(gi * LANES, (gi + 1) * LANES)
        s = p[:, cs]
        for j in range(1, w):
            s = s + buf_ref[:, POOL_BUF - j, cs]
        means.append(s / float(min(w, PAST_LEN + 1)))
    y = jnp.concatenate(means, axis=1) - p
    ob_ref[...] = _pool_mix(y, pw_ref, psc_ref)
    sv = sv_ref[...]
    vn = sv * lax.rsqrt(jnp.mean(sv * sv, axis=-1, keepdims=True) + EPS) * sg_ref[...]
    vn_ref[...] = vn
    od_ref[...] = u_ref[...] * (sw_ref[...] * vn + sb_ref[...])


def _small_sample(p_main, buf, pw_bf, pscale, sgu_g, sgu_w0, sgu_b0):
    row = N_P // N_S
    col = lambda c: pl.BlockSpec((N_S, 512), lambda i, c=c: (row, c // 512))
    vec = pl.BlockSpec((1, 512), lambda i: (0, 0))
    return pl.pallas_call(
        _small_sample_body,
        grid=(1,),
        in_specs=[col(C_PIN), pl.BlockSpec((N_S, POOL_BUF, 512), lambda i: (0, 0, 0)),
                  pl.BlockSpec((4, LANES, LANES), lambda i: (0, 0, 0)), vec,
                  col(C_SU), col(C_SV), vec, vec, vec],
        out_specs=[pl.BlockSpec((N_S, 512), lambda i: (0, 0))] * 3,
        out_shape=[jax.ShapeDtypeStruct((N_S, 512), F32)] * 3,
        compiler_params=_cparams(1), name="small_sample",
    )(p_main, buf, pw_bf, pscale, p_main, p_main, sgu_g, sgu_w0, sgu_b0)


ROWS_SGU = 512
SGU_CHUNK = 128


def _sgu_prompt_body(u_ref, v_ref, g_ref, w_ref, bt_ref, o_ref):
    ri = lax.broadcasted_iota(jnp.int32, (SGU_CHUNK, SGU_CHUNK), 0)
    ci = lax.broadcasted_iota(jnp.int32, (SGU_CHUNK, SGU_CHUNK), 1)
    causal = ri >= ci
    for c in range(ROWS_SGU // SGU_CHUNK):
        rows = pl.ds(c * SGU_CHUNK, SGU_CHUNK)
        v = v_ref[rows, :]
        vn = (v * lax.rsqrt(jnp.mean(v * v, axis=-1, keepdims=True) + EPS) * g_ref[...]).astype(BF16)
        outs = []
        for gi in range(4):
            cs = slice(gi * LANES, (gi + 1) * LANES)
            w = jnp.where(causal, w_ref[gi], 0.0).astype(BF16)
            mixed = jnp.dot(w, vn[:, cs], preferred_element_type=F32)
            outs.append(mixed + jnp.broadcast_to(bt_ref[:, gi:gi + 1], (SGU_CHUNK, LANES)))
        o_ref[rows, :] = (u_ref[rows, :] * jnp.concatenate(outs, axis=1)).astype(BF16)


def _sgu_prompt(p_main, g, w, b_t):
    return pl.pallas_call(
        _sgu_prompt_body,
        grid=(N_P // ROWS_SGU,),
        in_specs=[pl.BlockSpec((ROWS_SGU, 512), lambda i: (i, C_SU // 512)),
                  pl.BlockSpec((ROWS_SGU, 512), lambda i: (i, C_SV // 512)),
                  pl.BlockSpec((1, 512), lambda i: (0, 0)),
                  pl.BlockSpec((4, SGU_CHUNK, SGU_CHUNK), lambda i: (0, 0, 0)),
                  pl.BlockSpec((SGU_CHUNK, LANES), lambda i: (0, 0))],
        out_specs=pl.BlockSpec((ROWS_SGU, 512), lambda i: (i, 0)),
        out_shape=jax.ShapeDtypeStruct((N, 512), BF16),
        compiler_params=_cparams(1), name="sgu_prompt")(p_main, p_main, g, w, b_t)


TM_MERGE = 640
TN_MERGE = 512


def _merge_body(h_ref, ba_ref, bb_ref, bc_ref, bd_ref, g0, g1, g2, g3, u0, u1, u2, u3,
                c0, c1, c2, c3, o_ref):
    h = h_ref[...]
    acc = None
    for br, gw, uw, gb in ((ba_ref, g0, u0, c0), (bb_ref, g1, u1, c1), (bc_ref, g2, u2, c2), (bd_ref, g3, u3, c3)):
        gate = jax.nn.sigmoid(jnp.dot(h, gw[...], preferred_element_type=F32) + gb[...])
        up = jnp.dot(br[...], uw[...], preferred_element_type=F32)
        acc = gate * up if acc is None else acc + gate * up
    o_ref[...] = acc.astype(BF16)


def _merge(h, branches, w_mg, b_mg, w_br):
    nj = D // TN_MERGE
    row = lambda w: pl.BlockSpec((TM_MERGE, w), lambda i, j: (i, 0))
    gate_w = [pl.BlockSpec((D, TN_MERGE), lambda i, j, b=b: (0, b * nj + j)) for b in range(4)]
    up_w = [pl.BlockSpec((None, 512, TN_MERGE), lambda i, j, b=b: (b, 0, j)) for b in range(4)]
    gate_b = [pl.BlockSpec((1, TN_MERGE), lambda i, j, b=b: (0, b * nj + j)) for b in range(4)]
    return pl.pallas_call(
        _merge_body,
        grid=(N // TM_MERGE, nj),
        in_specs=[row(D)] + [row(512)] * 4 + gate_w + up_w + gate_b,
        out_specs=pl.BlockSpec((TM_MERGE, TN_MERGE), lambda i, j: (i, j)),
        out_shape=jax.ShapeDtypeStruct((N, D), BF16),
        compiler_params=_cparams(2), name="merge",
    )(h, *branches, w_mg, w_mg, w_mg, w_mg, w_br, w_br, w_br, w_br, b_mg, b_mg, b_mg, b_mg)


TM_OUT = 640
TN_OUT = 512


def _post_value(t, x, y, gn_ref, gp_ref, gs_ref):
    yn = y * lax.rsqrt(jnp.mean(y * y, axis=-1, keepdims=True) + EPS) * gn_ref[...]
    return x + _mod_rows(t, gp_ref, gs_ref) * yn


def _outproj_body(m_ref, w_ref, *refs):
    gn_ref, gp_ref, gs_ref, o_ref, acc_ref = refs[-5:]
    j = pl.program_id(1)
    acc_ref[j] = jnp.dot(m_ref[...], w_ref[...], preferred_element_type=F32)

    @pl.when(j == D // TN_OUT - 1)
    def _():
        for sidx in range(TM_OUT // SUB):
            rows = pl.ds(sidx * SUB, SUB)
            t = pl.program_id(0) * (TM_OUT // SUB) + sidx
            y = jnp.concatenate([acc_ref[c, rows, :] for c in range(D // TN_OUT)], axis=1)
            o_ref[rows, :] = _post_value(t, _x_rows(refs[:-5], t, rows), y, gn_ref, gp_ref, gs_ref)


def _outproj(merged, w_out, x, g_post, mod, layer):
    mspec = [pl.BlockSpec((None, 8, D), lambda i, j, l=layer: (l, 0, 2)),
             pl.BlockSpec((None, SUB, D), lambda i, j, l=layer: (l, 1, 2))]
    x_specs, x_args = _x_specs(x, TM_OUT, 2)
    return pl.pallas_call(
        _outproj_body,
        grid=(N // TM_OUT, D // TN_OUT),
        in_specs=[pl.BlockSpec((TM_OUT, D), lambda i, j: (i, 0)),
                  pl.BlockSpec((D, TN_OUT), lambda i, j: (0, j))] + x_specs
        + [pl.BlockSpec((1, D), lambda i, j: (0, 0))] + mspec,
        out_specs=pl.BlockSpec((TM_OUT, D), lambda i, j: (i, 0)),
        out_shape=jax.ShapeDtypeStruct((N, D), F32),
        scratch_shapes=[pltpu.VMEM((D // TN_OUT, TM_OUT, TN_OUT), F32)],
        compiler_params=_cparams(2), name="outproj",
    )(merged, w_out, *x_args, g_post.reshape(1, D), mod, mod)


def _router_body(lg_ref, b_ref, eid_ref, pos_ref, wt_ref, cnt_ref, run_ref):
    i = pl.program_id(0)

    @pl.when(i == 0)
    def _():
        run_ref[...] = jnp.zeros_like(run_ref)

    ng, gs = 8, N_EXPERTS // 8
    neg = -jnp.inf
    scores = jax.nn.sigmoid(lg_ref[...].T[:N_EXPERTS, :])
    sel = scores + b_ref[...]
    sel3 = sel.reshape(ng, gs, SUB)
    sub3 = lax.broadcasted_iota(jnp.int32, (ng, gs, SUB), 1)
    gmax = jnp.max(sel3, axis=1, keepdims=True)
    first = jnp.min(jnp.where(sel3 == gmax, sub3, gs), axis=1, keepdims=True)
    gmax2 = jnp.max(jnp.where(sub3 == first, neg, sel3), axis=1, keepdims=True)
    gscore = (gmax + gmax2).reshape(ng, SUB)
    gidx = lax.broadcasted_iota(jnp.int32, (ng, SUB), 0)
    grank = jnp.zeros((ng, SUB), jnp.int32)
    for s in range(1, ng):
        other = pltpu.roll(gscore, s, 0)
        lower = gidx >= s
        grank += ((other > gscore) | ((other == gscore) & lower)).astype(jnp.int32)
    keep = jnp.broadcast_to((grank < 4).reshape(ng, 1, SUB), (ng, gs, SUB))
    masked = jnp.where(keep, sel3, neg).reshape(N_EXPERTS, SUB)
    eidx = lax.broadcasted_iota(jnp.int32, (N_EXPERTS, SUB), 0)
    rank = jnp.zeros((N_EXPERTS, SUB), jnp.int32)
    for s in range(1, N_EXPERTS):
        other = pltpu.roll(masked, s, 0)
        lower = eidx >= s
        rank += ((other > masked) | ((other == masked) & lower)).astype(jnp.int32)
    chosen = rank < TOP_K
    w_sel = jnp.where(chosen, scores, 0.0)
    w_sel = w_sel / jnp.sum(w_sel, axis=0, keepdims=True) * ROUTED_SCALE
    ri = lax.broadcasted_iota(jnp.int32, (SUB, SUB), 0)
    ci = lax.broadcasted_iota(jnp.int32, (SUB, SUB), 1)
    onehot = chosen.astype(BF16)
    pos = jnp.dot(onehot, (ri < ci).astype(BF16), preferred_element_type=F32) + run_ref[...]
    run_ref[...] = run_ref[...] + jnp.sum(chosen.astype(F32), axis=1, keepdims=True)
    cnt_ref[...] = run_ref[...]
    eidx_f = eidx.astype(F32)
    rows_e, rows_p, rows_w = [], [], []
    for kk in range(TOP_K):
        m = chosen & (rank == kk)
        rows_e.append(jnp.sum(jnp.where(m, eidx_f, 0.0), axis=0, keepdims=True))
        rows_p.append(jnp.sum(jnp.where(m, pos, 0.0), axis=0, keepdims=True))
        rows_w.append(jnp.sum(jnp.where(m, w_sel, 0.0), axis=0, keepdims=True))
    eid_ref[...] = jnp.concatenate(rows_e, axis=0).astype(jnp.int32)
    pos_ref[...] = jnp.concatenate(rows_p, axis=0).astype(jnp.int32)
    wt_ref[...] = jnp.concatenate(rows_w, axis=0)


def _router(logits, rb_col):
    tile = pl.BlockSpec((TOP_K, SUB), lambda i: (0, i))
    return pl.pallas_call(
        _router_body,
        grid=(N // SUB,),
        in_specs=[pl.BlockSpec((SUB, LANES), lambda i: (i, 0)),
                  pl.BlockSpec((N_EXPERTS, SUB), lambda i: (0, 0))],
        out_specs=[tile, tile, tile, pl.BlockSpec((N_EXPERTS, SUB), lambda i: (0, 0))],
        out_shape=[jax.ShapeDtypeStruct((TOP_K, N), jnp.int32), jax.ShapeDtypeStruct((TOP_K, N), jnp.int32),
                   jax.ShapeDtypeStruct((TOP_K, N), F32), jax.ShapeDtypeStruct((N_EXPERTS, SUB), F32)],
        scratch_shapes=[pltpu.VMEM((N_EXPERTS, SUB), F32)],
        compiler_params=_cparams(1), name="router")(logits, rb_col)


SC_CORES, SC_SUBCORES = 2, 16
SC_WORKERS = SC_CORES * SC_SUBCORES
SC_LANES = 16
SC_CHUNK = 16
SC_SCAN = N_ASSIGN // SC_WORKERS


def _sc_mesh():
    return plsc.VectorSubcoreMesh(core_axis_name="c", subcore_axis_name="s",
                                  num_cores=SC_CORES, num_subcores=SC_SUBCORES)


def _sc_worker_base(per_w):
    return (lax.axis_index("s") * SC_CORES + lax.axis_index("c")) * per_w


def _sc_gather_rows(table_hbm, out_hbm, idx_v, rows_v, gsem, wsem, base, per_w):
    n_ch = per_w // SC_CHUNK
    assert n_ch % 2 == 0

    def gather(j, p):
        off = pl.multiple_of(j * SC_CHUNK, SC_CHUNK)
        return pltpu.make_async_copy(table_hbm.at[idx_v.at[pl.ds(off, SC_CHUNK)]], rows_v.at[p], gsem.at[p])

    def write(j, p):
        off = pl.multiple_of(j * SC_CHUNK, SC_CHUNK)
        return pltpu.make_async_copy(rows_v.at[p], out_hbm.at[pl.ds(base + off, SC_CHUNK)], wsem.at[p])

    gather(0, 0).start()

    @pl.loop(0, n_ch, step=2)
    def _(j0):
        for p in range(2):
            j = j0 + p
            gather(j, p).wait()

            @pl.when(j >= 1)
            def _():
                write(j - 1, 1 - p).wait()

            @pl.when(j + 1 < n_ch)
            def _():
                gather(j + 1, 1 - p).start()
            write(j, p).start()

    write(n_ch - 1, 1).wait()


_SC_ROW_SCRATCH = [pltpu.VMEM((2, SC_CHUNK) + ROW_TILE, jnp.int32),
                   pltpu.SemaphoreType.DMA((2,)), pltpu.SemaphoreType.DMA((2,))]


def _sc_gather(table, idx):
    n_out = idx.shape[0]
    per_w = n_out // SC_WORKERS
    assert per_w * SC_WORKERS == n_out and per_w % (2 * SC_CHUNK) == 0

    def body(table_hbm, idx_hbm, out_hbm, idx_v, rows_v, gsem, wsem):
        base = _sc_worker_base(per_w)
        pltpu.sync_copy(idx_hbm.at[pl.ds(base, per_w)], idx_v)
        _sc_gather_rows(table_hbm, out_hbm, idx_v, rows_v, gsem, wsem, base, per_w)

    return pl.kernel(
        body, out_type=jax.ShapeDtypeStruct((n_out,) + ROW_TILE, jnp.int32), mesh=_sc_mesh(),
        scratch_types=[pltpu.VMEM((per_w,), jnp.int32)] + _SC_ROW_SCRATCH, name="sc_gather")(table, idx)


N_PARTS = 3
PART_BLOCKS = N_BLOCKS // N_PARTS
PART_SLOTS = PART_BLOCKS * EXP_BLOCK
assert PART_BLOCKS * N_PARTS == N_BLOCKS


def _sc_dispatch(table, slots, part):
    per_w = PART_SLOTS // SC_WORKERS
    assert per_w * SC_WORKERS == PART_SLOTS and per_w % (2 * SC_CHUNK) == 0 and SC_SCAN % SC_LANES == 0
    assert ZERO_ROWS & (ZERO_ROWS - 1) == 0

    def body(table_hbm, slots_hbm, out_hbm, idx_v, sl_v, rows_v, gsem, wsem):
        local = _sc_worker_base(per_w)
        base = part * PART_SLOTS + local
        lane = lax.iota(jnp.int32, SC_LANES)

        @pl.loop(0, per_w // SC_LANES)
        def _(j):
            off = pl.multiple_of(j * SC_LANES, SC_LANES)
            idx_v[pl.ds(off, SC_LANES)] = N + ((base + off + lane) & (ZERO_ROWS - 1))

        @pl.loop(0, N_ASSIGN // SC_SCAN)
        def _(c):
            pltpu.sync_copy(slots_hbm.at[pl.ds(pl.multiple_of(c * SC_SCAN, 8), SC_SCAN)], sl_v)

            @pl.loop(0, SC_SCAN // SC_LANES)
            def _(j):
                off = pl.multiple_of(j * SC_LANES, SC_LANES)
                loc = sl_v[pl.ds(off, SC_LANES)] - base
                mine = (loc >= 0) & (loc < per_w)
                tok = lax.shift_right_logical(c * SC_SCAN + off + lane, 3)
                plsc.store_scatter(idx_v, [jnp.where(mine, loc, 0)], tok, mask=mine)

        _sc_gather_rows(table_hbm, out_hbm, idx_v, rows_v, gsem, wsem, local, per_w)

    return pl.kernel(
        body, out_type=jax.ShapeDtypeStruct((PART_SLOTS,) + ROW_TILE, jnp.int32), mesh=_sc_mesh(),
        scratch_types=[pltpu.VMEM((per_w,), jnp.int32), pltpu.VMEM((SC_SCAN,), jnp.int32)] + _SC_ROW_SCRATCH,
        compiler_params=pltpu.CompilerParams(needs_layout_passes=False),
        name="sc_dispatch")(table, slots)


def _experts_body(be_ref, first_ref, par_ref, next_ref, nextblk_ref, nused_ref, x_ref, w1_hbm, w3_hbm, w2_hbm,
                  *rest, layer, part):
    y_ref, w1f, w3f, w2f, w1b, w3b, w2b, sem = rest[-8:]
    i = pl.program_id(0)
    b = part * PART_BLOCKS + i
    used = b < nused_ref[0]

    def copies(e, slot):
        return (pltpu.make_async_copy(w1_hbm.at[layer, e], w1f.at[slot], sem.at[0, slot]),
                pltpu.make_async_copy(w3_hbm.at[layer, e], w3f.at[slot], sem.at[1, slot]),
                pltpu.make_async_copy(w2_hbm.at[layer, e], w2f.at[slot], sem.at[2, slot]))

    @pl.when(used & (i == 0))
    def _():
        for c in copies(be_ref[b], par_ref[b]):
            c.start()

    @pl.when(used & ((i == 0) | (first_ref[b] == 1)))
    def _():
        slot = par_ref[b]
        for c in copies(be_ref[b], slot):
            c.wait()

        @pl.when((next_ref[b] >= 0) & (nextblk_ref[b] < (part + 1) * PART_BLOCKS))
        def _():
            for c in copies(next_ref[b], 1 - slot):
                c.start(priority=1)
        w1b[...] = w1f[slot].astype(BF16)
        w3b[...] = w3f[slot].astype(BF16)
        w2b[...] = w2f[slot].astype(BF16)

    @pl.when(used)
    def _():
        lo, hi = _unpack_bf16_pair(_load_row_tiles_2d(x_ref, EXP_BLOCK))
        lo = lo.astype(BF16)
        hi = hi.astype(BF16)
        half = D // 2
        h1 = (jnp.dot(lo, w1b[:half, :], preferred_element_type=F32)
              + jnp.dot(hi, w1b[half:, :], preferred_element_type=F32))
        h3 = (jnp.dot(lo, w3b[:half, :], preferred_element_type=F32)
              + jnp.dot(hi, w3b[half:, :], preferred_element_type=F32))
        hid = (_silu(h1) * h3).astype(BF16)
        y = jnp.dot(hid, w2b[...], preferred_element_type=F32)
        _store_row_tiles_2d(y_ref, _pack_bf16_pair(y[:, :half], y[:, half:]), EXP_BLOCK)

    @pl.when(jnp.logical_not(used))
    def _():
        y_ref[...] = jnp.zeros_like(y_ref)


def _experts(ctl, xs_part, w1, w3, w2, layer, part, ys_prev):
    def x_blk(i, *refs):
        n_here = jnp.clip(refs[-1][0] - part * PART_BLOCKS, 1, PART_BLOCKS)
        return (jnp.minimum(i, n_here - 1), 0)
    any_spec = pl.BlockSpec(memory_space=pl.ANY)
    in_specs = [pl.BlockSpec((EXP_BLOCK * ROW_TILE[0], LANES), x_blk), any_spec, any_spec, any_spec]
    args = [xs_part.reshape(PART_SLOTS * ROW_TILE[0], LANES), w1, w3, w2]
    aliases = {}
    if ys_prev is not None:
        in_specs.append(any_spec)
        args.append(ys_prev)
        aliases = {len(ctl) + 4: 0}
    grid_spec = pltpu.PrefetchScalarGridSpec(
        num_scalar_prefetch=len(ctl),
        grid=(PART_BLOCKS,),
        in_specs=in_specs,
        out_specs=pl.BlockSpec((EXP_BLOCK * ROW_TILE[0], LANES), lambda i, *refs: (part * PART_BLOCKS + i, 0)),
        scratch_shapes=[pltpu.VMEM((2, D, D_EXPERT), F32), pltpu.VMEM((2, D, D_EXPERT), F32),
                        pltpu.VMEM((2, D_EXPERT, D), F32),
                        pltpu.VMEM((D, D_EXPERT), BF16), pltpu.VMEM((D, D_EXPERT), BF16),
                        pltpu.VMEM((D_EXPERT, D), BF16), pltpu.SemaphoreType.DMA((3, 2))])
    return pl.pallas_call(
        functools.partial(_experts_body, layer=layer, part=part), grid_spec=grid_spec,
        out_shape=jax.ShapeDtypeStruct((L_SLOTS * ROW_TILE[0], LANES), jnp.int32),
        input_output_aliases=aliases,
        compiler_params=_cparams(1), name="experts")(*ctl, *args)


TM_FFN = 640


def _ffn_pre_body(x_ref, g_ref, shp_ref, shs_ref, scp_ref, scs_ref, rw_ref, w13_ref, w2_ref,
                  sh_ref, lg_ref, hp_ref, hb_ref):
    i = pl.program_id(0)

    @pl.when(i < N // TM_FFN)
    def _():
        for sidx in range(TM_FFN // SUB):
            rows = pl.ds(sidx * SUB, SUB)
            h = _prenorm_rows(i * (TM_FFN // SUB) + sidx, x_ref[rows, :], g_ref, shp_ref, shs_ref, scp_ref, scs_ref)
            lg_ref[rows, :] = jnp.dot(h, rw_ref[...], precision=HIGHEST, preferred_element_type=F32)
            packed = _pack_bf16_pair(h[:, :D // 2], h[:, D // 2:])
            for c in range(ROW_TILE[0]):
                hp_ref[pl.ds(sidx * SUB * ROW_TILE[0] + c, SUB, stride=ROW_TILE[0]), :] = (
                    packed[:, c * LANES:(c + 1) * LANES])
            hb_ref[rows, :] = h.astype(BF16)
        up = jnp.dot(hb_ref[...], w13_ref[...], preferred_element_type=F32)
        hid = (_silu(up[:, :D_EXPERT]) * up[:, D_EXPERT:]).astype(BF16)
        sh_ref[...] = jnp.dot(hid, w2_ref[...], preferred_element_type=F32)

    @pl.when(i >= N // TM_FFN)
    def _():
        hp_ref[...] = jnp.zeros_like(hp_ref)


def _ffn_pre(x, g, mod, layer, rw, w13, w2):
    last = N // TM_FFN - 1
    row = lambda i: (jnp.minimum(i, last), 0)
    mspec = lambda part, rows, blk: pl.BlockSpec((None, rows, D), lambda i, l=layer, p=part, b=blk: (l, b, p))
    full = lambda shape: pl.BlockSpec(shape, lambda i: (0, 0))
    return pl.pallas_call(
        _ffn_pre_body,
        grid=(N // TM_FFN + 1,),
        in_specs=[pl.BlockSpec((TM_FFN, D), row), full((1, D)),
                  mspec(3, 8, 0), mspec(3, SUB, 1), mspec(4, 8, 0), mspec(4, SUB, 1),
                  full((D, LANES)), full((D, 2 * D_EXPERT)), full((D_EXPERT, D))],
        out_specs=[pl.BlockSpec((TM_FFN, D), row), pl.BlockSpec((TM_FFN, LANES), row),
                   pl.BlockSpec((TM_FFN * ROW_TILE[0], LANES), lambda i: (i, 0))],
        out_shape=[jax.ShapeDtypeStruct((N, D), F32), jax.ShapeDtypeStruct((N, LANES), F32),
                   jax.ShapeDtypeStruct(((N + TM_FFN) * ROW_TILE[0], LANES), jnp.int32)],
        scratch_shapes=[pltpu.VMEM((TM_FFN, D), BF16)],
        compiler_params=_cparams(1), name="ffn_pre")(x, g.reshape(1, D), mod, mod, mod, mod, rw, w13, w2)


def _combine_body(g_ref, wt_ref, sh_ref, x_ref, gn_ref, gp_ref, gs_ref, *rest, tile0, n_out, final):
    outs = rest[-n_out:]
    half = D // 2
    acc_lo = sh_ref[:, :half]
    acc_hi = sh_ref[:, half:]
    wt = wt_ref[...]
    per_tok = TOP_K * ROW_TILE[0]
    for k in range(TOP_K):
        packed = jnp.concatenate([g_ref[pl.ds(k * ROW_TILE[0] + c, SUB, stride=per_tok), :]
                                  for c in range(ROW_TILE[0])], axis=1)
        lo, hi = _unpack_bf16_pair(packed)
        w_c = wt[:, k:k + 1]
        acc_lo = acc_lo + w_c * lo
        acc_hi = acc_hi + w_c * hi
    t = tile0 + pl.program_id(0)
    val = _post_value(t, x_ref[...], jnp.concatenate([acc_lo, acc_hi], axis=1), gn_ref, gp_ref, gs_ref)
    if not final:
        outs[0][...] = val
    else:
        @pl.when(t < N_P // SUB)
        def _():
            outs[0][...] = val
        if n_out == 2:
            @pl.when(t >= N_P // SUB)
            def _():
                outs[1][...] = val


def _combine(gathered, wts, shared, x, g_post, mod, layer, tile0, n_tiles, out_prev, final):
    per_tok = TOP_K * ROW_TILE[0]
    row = lambda i: (tile0 + i, 0)
    in_specs = [pl.BlockSpec((SUB * per_tok, LANES), lambda i: (i, 0)),
                pl.BlockSpec((SUB, LANES), row), pl.BlockSpec((SUB, D), row), pl.BlockSpec((SUB, D), row),
                pl.BlockSpec((1, D), lambda i: (0, 0)),
                pl.BlockSpec((None, 8, D), lambda i, l=layer: (l, 0, 5)),
                pl.BlockSpec((None, SUB, D), lambda i, l=layer: (l, 1, 5))]
    args = [gathered.reshape(n_tiles * SUB * per_tok, LANES), wts, shared, x, g_post.reshape(1, D), mod, mod]
    aliases = {}
    if out_prev is not None:
        in_specs.append(pl.BlockSpec(memory_space=pl.ANY))
        args.append(out_prev)
        aliases = {len(args) - 1: 0}
    if not final:
        out_specs = [pl.BlockSpec((SUB, D), row)]
        out_shape = [jax.ShapeDtypeStruct((N, D), F32)]
    else:
        last_p = N_P // SUB - 1
        out_specs = [pl.BlockSpec((SUB, D), lambda i: (jnp.minimum(tile0 + i, last_p), 0))]
        out_shape = [jax.ShapeDtypeStruct((N_P, D), F32)]
        if tile0 + n_tiles > N_P // SUB:
            out_specs.append(pl.BlockSpec((N_S, D), lambda i: (0, 0)))
            out_shape.append(jax.ShapeDtypeStruct((N_S, D), F32))
    return pl.pallas_call(
        functools.partial(_combine_body, tile0=tile0, n_out=len(out_shape), final=final),
        grid=(n_tiles,),
        in_specs=in_specs, out_specs=out_specs, out_shape=out_shape,
        input_output_aliases=aliases,
        compiler_params=_cparams(1), name="combine")(*args)


def _slots_body(start_ref, eid_ref, pos_ref, o_ref):
    eid = eid_ref[...]
    acc = pos_ref[...]
    for e in range(N_EXPERTS):
        acc = acc + jnp.where(eid == e, start_ref[e], 0)
    o_ref[...] = acc


def _slots(pad_start, eid, pos):
    grid_spec = pltpu.PrefetchScalarGridSpec(
        num_scalar_prefetch=1, grid=(1,),
        in_specs=[pl.BlockSpec((TOP_K, N), lambda i, s: (0, 0)), pl.BlockSpec((TOP_K, N), lambda i, s: (0, 0))],
        out_specs=pl.BlockSpec((TOP_K, N), lambda i, s: (0, 0)))
    return pl.pallas_call(_slots_body, grid_spec=grid_spec,
                          out_shape=jax.ShapeDtypeStruct((TOP_K, N), jnp.int32),
                          compiler_params=_cparams(1), name="slots")(pad_start, eid, pos)


def _put_sample_rows_body(*refs):
    n = len(refs) // 3
    for src, dst in zip(refs[:n], refs[2 * n:]):
        dst[...] = src[...].astype(BF16)


def _put_sample_rows(sample_rows, full):
    n = len(full)
    return pl.pallas_call(
        _put_sample_rows_body,
        grid=(1,),
        in_specs=[pl.BlockSpec((N_S, 512), lambda i: (0, 0))] * n + [pl.BlockSpec(memory_space=pl.ANY)] * n,
        out_specs=[pl.BlockSpec((N_S, 512), lambda i: (N_P // N_S, 0))] * n,
        out_shape=[jax.ShapeDtypeStruct((N, 512), BF16)] * n,
        input_output_aliases={n + k: k for k in range(n)},
        compiler_params=_cparams(1), name="put_sample_rows")(*sample_rows, *full)


COMBINE_RANGES = ((0, 33), (33, 32))

def _reorder_w_in(w):
    return jnp.concatenate([w[:, :1536], w[:, 1552:]], axis=1), jnp.pad(w[:, 1536:1552], ((0, 0), (0, LANES - 16)))


def _rope_tables(pos):
    half = DK // 2
    inv = ROPE_BASE ** (-jnp.arange(half, dtype=F32) / half)
    ang = pos.astype(F32)[:, None] * inv[None, :]
    cos = jnp.cos(ang)
    sin = jnp.sin(ang)
    return jnp.concatenate([cos, cos], axis=1), jnp.concatenate([-sin, sin], axis=1)


def _layer(l, x, mod, s_gla, s_pool, s_ret, wts, final, prev_gla, prev_ret):
    (norm_mix_pre, norm_mix_post, norm_ffn_pre, norm_ffn_post, w_in, w_gla_gate, b_gla_gate, gla_norm,
     pool_w, pool_scale, ret_norm, sgu_norm, sgu_w, sgu_b, w_branch, w_merge_gate, b_merge_gate, w_out,
     router_w, router_bias, expert_w1, expert_w3, expert_w2, shared_w1, shared_w3, shared_w2) = wts

    w_main, w_low = _reorder_w_in(w_in[l])
    h = _prenorm(x, norm_mix_pre[l], mod, l)
    p_main = _matmul(h, w_main.astype(BF16), 1664, 512, name="inproj")
    p_low = _matmul(h, w_low.astype(BF16), 1664, LANES, name="inproj_low")

    w_gate_pad = jnp.pad(w_gla_gate[l], ((0, LANES - 16), (0, 0)))
    b_gate = b_gla_gate[l].reshape(1, HEADS * DK)
    log_gamma = jnp.log1p(-jnp.exp2(-5.0 - jnp.arange(HEADS, dtype=F32)))
    dec_row = jnp.repeat(log_gamma, DK).reshape(1, HEADS * DK)
    cos_p, sin_p = _rope_tables(jnp.arange(T_P))
    cos_p = jnp.tile(cos_p, (1, 2))
    sin_p = jnp.tile(sin_p, (1, 2))
    g_gla = gla_norm[l].reshape(1, HEADS * DV)
    g_ret = ret_norm[l].reshape(1, HEADS * DV)

    oa_p, gla_p = _la_prompt(p_main, C_GQ, C_GK, C_GV, C_GR, p_low, p_low, w_gate_pad, b_gate, g_gla, False)
    oc_p, ret_p = _la_prompt(p_main, C_RQ, C_RK, C_RV, C_RG, cos_p, sin_p, dec_row, b_gate, g_ret, True)
    pw_bf = pool_w[l].astype(BF16)
    pscale = pool_scale[l].reshape(1, 512)
    ob_p = _pool_prompt(p_main, pw_bf, pscale)
    sgu_g = sgu_norm[l].reshape(1, 512)
    od_p = _sgu_prompt(p_main, sgu_g, sgu_w[l], jnp.pad(sgu_b[l].T, ((0, 0), (0, LANES - 4))))

    ps = p_main[N_P:]
    q_t = _to_tiles_t(ps[:, C_GQ:C_GQ + 256])
    k_t = _to_tiles_t(ps[:, C_GK:C_GK + 256])
    glow_t = jnp.pad(_to_tiles_t(p_low[N_P:, :16]), ((0, 0), (0, LANES - 16), (0, 0)))
    w_gate_t = jnp.pad(w_gla_gate[l].T, ((0, 0), (0, LANES - 16)))
    b_col = jnp.broadcast_to(b_gla_gate[l][:, None], (HEADS * DK, LANES))
    logit_t = _gate_logits_t(w_gate_t, glow_t, b_col)
    dummy = jnp.zeros((HEADS * DK, LANES), F32)
    oa_s, gla_s = _la_sample(q_t, k_t, logit_t, dummy, dummy, p_main, C_GV, C_GR, g_gla, s_gla, l, prev_gla, False)
    cos_s, sin_s = _rope_tables(jnp.full((1,), PAST_LEN))
    cos_c = jnp.broadcast_to(jnp.tile(cos_s[0], HEADS)[:, None], (HEADS * DK, LANES))
    sin_c = jnp.broadcast_to(jnp.tile(sin_s[0], HEADS)[:, None], (HEADS * DK, LANES))
    dec_c = jnp.broadcast_to(jnp.repeat(log_gamma, DK)[:, None], (HEADS * DK, LANES))
    rq_t = _to_tiles_t(ps[:, C_RQ:C_RQ + 256])
    rk_t = _to_tiles_t(ps[:, C_RK:C_RK + 256])
    oc_s, ret_s = _la_sample(rq_t, rk_t, dec_c, cos_c, sin_c, p_main, C_RV, C_RG, g_ret, s_ret, l, prev_ret, True)
    sgu_w0 = jnp.repeat(sgu_w[l][:, 0, 0], LANES).reshape(1, 512)
    sgu_b0 = jnp.repeat(sgu_b[l][:, 0], LANES).reshape(1, 512)
    ob_s, od_s, vn_s = _small_sample(p_main, s_pool[l], pw_bf, pscale, sgu_g, sgu_w0, sgu_b0)
    pool_p = jnp.stack([p_main[(b + 1) * T_P - POOL_BUF:(b + 1) * T_P, C_PIN:C_PIN + 512] for b in range(B_P)])
    pool_s = jnp.concatenate([s_pool[l][:, 1:], ps[:, None, C_PIN:C_PIN + 512]], axis=1)

    branches = _put_sample_rows([oa_s, ob_s, oc_s, od_s], [oa_p, ob_p, oc_p, od_p])
    merged = _merge(h, branches, w_merge_gate[l].astype(BF16), b_merge_gate[l].reshape(1, 4 * D),
                    w_branch[l].astype(BF16))
    x = _outproj(merged, w_out[l].astype(BF16), x, norm_mix_post[l], mod, l)

    rw = jnp.pad(router_w[l], ((0, 0), (0, LANES - N_EXPERTS)))
    rb = jnp.broadcast_to(router_bias[l][:, None], (N_EXPERTS, SUB))
    w13 = jnp.concatenate([shared_w1[l], shared_w3[l]], axis=1).astype(BF16)
    shared, logits, h2_packed = _ffn_pre(x, norm_ffn_pre[l], mod, l, rw, w13, shared_w2[l].astype(BF16))
    eid, pos, wt, counts = _router(logits, rb)
    counts = counts[:, 0].astype(jnp.int32)
    padded = (counts + EXP_BLOCK - 1) // EXP_BLOCK * EXP_BLOCK
    pad_end = jnp.cumsum(padded)
    pad_start = pad_end - padded
    nused = (pad_end[-1] // EXP_BLOCK).astype(jnp.int32).reshape(1)
    blk_row = jnp.arange(N_BLOCKS, dtype=jnp.int32) * EXP_BLOCK
    block_e = jnp.minimum(jnp.sum((blk_row[:, None] >= pad_end[None, :]).astype(jnp.int32), axis=1),
                          N_EXPERTS - 1)
    first = jnp.concatenate([jnp.ones((1,), jnp.int32), (block_e[1:] != block_e[:-1]).astype(jnp.int32)])
    first = jnp.where(blk_row < pad_end[-1], first, 0)
    par = (jnp.cumsum(first) - 1) % 2
    live = jnp.where(padded > 0, jnp.arange(N_EXPERTS), N_EXPERTS)
    after = jnp.concatenate([lax.cummin(live, reverse=True)[1:], jnp.full((1,), N_EXPERTS)])
    of_block = block_e[:, None] == jnp.arange(N_EXPERTS)
    next_e = jnp.sum(jnp.where(of_block, jnp.where(after < N_EXPERTS, after, -1), 0), axis=1).astype(jnp.int32)
    next_blk = jnp.sum(jnp.where(of_block, pad_end // EXP_BLOCK, 0), axis=1).astype(jnp.int32)
    slots = _slots(pad_start.astype(jnp.int32), eid, pos).T.reshape(N_ASSIGN)
    wt = jnp.pad(wt.T, ((0, 0), (0, LANES - TOP_K)))
    table = h2_packed.reshape((N + TM_FFN,) + ROW_TILE)
    ctl = (block_e, first, par.astype(jnp.int32), next_e, next_blk, nused)
    xs_parts = [_sc_dispatch(table, slots, part) for part in range(N_PARTS)]
    ys = None
    for part in range(N_PARTS):
        ys = _experts(ctl, xs_parts[part], expert_w1, expert_w3, expert_w2, l, part, ys)
    ys = ys.reshape((L_SLOTS,) + ROW_TILE)
    outs = [None]
    for tile0, n_tiles in COMBINE_RANGES:
        a0, a1 = tile0 * SUB * TOP_K, (tile0 + n_tiles) * SUB * TOP_K
        outs = _combine(_sc_gather(ys, slots[a0:a1]), wt, shared, x, norm_ffn_post[l], mod, l, tile0, n_tiles,
                        outs[0], final)
    x = tuple(outs) if final else outs[0]
    return x, (gla_p, pool_p, pool_s, ret_p, vn_s), gla_s, ret_s


def kernel(x_prompt, x_sample, c_prompt, c_sample, state_gla, state_pool, state_ret, w_ada, b_ada, norm_mix_pre, norm_mix_post, norm_ffn_pre, norm_ffn_post, w_in, w_gla_gate, b_gla_gate, gla_norm, pool_w, pool_scale, ret_norm, sgu_norm, sgu_w, sgu_b, w_branch, w_merge_gate, b_merge_gate, w_out, router_w, router_bias, expert_w1, expert_w3, expert_w2, shared_w1, shared_w3, shared_w2):
    wts = (norm_mix_pre, norm_mix_post, norm_ffn_pre, norm_ffn_post, w_in, w_gla_gate, b_gla_gate, gla_norm,
           pool_w, pool_scale, ret_norm, sgu_norm, sgu_w, sgu_b, w_branch, w_merge_gate, b_merge_gate, w_out,
           router_w, router_bias, expert_w1, expert_w3, expert_w2, shared_w1, shared_w3, shared_w2)
    c_all = jnp.zeros((MOD_ROWS, D), F32).at[:B_P].set(c_prompt).at[SUB:SUB + N_S].set(c_sample)
    mod = _ada(c_all, w_ada, b_ada)
    x = (x_prompt.reshape(N_P, D), x_sample.reshape(N_S, D))
    per_layer = []
    gla_s = ret_s = None
    for l in range(DEPTH):
        x, states, gla_s, ret_s = _layer(l, x, mod, state_gla, state_pool, state_ret, wts, l == DEPTH - 1,
                                         gla_s, ret_s)
        per_layer.append(states)
    gla_p, pool_p, pool_s, ret_p, vn_s = (jnp.stack(z) for z in zip(*per_layer))
    return (x[0].reshape(B_P, T_P, D), x[1].reshape(N_S, 1, D),
            gla_p, gla_s, pool_p, pool_s, ret_p, ret_s, vn_s.reshape(DEPTH, N_S, 1, 512))
```

```python
import functools

import jax
import jax.numpy as jnp
from jax import lax
from jax.experimental import pallas as pl
from jax.experimental.pallas import tpu as pltpu
from jax.experimental.pallas import tpu_sc as plsc

F32 = jnp.float32
BF16 = jnp.bfloat16
HIGHEST = lax.Precision.HIGHEST

D = 2048
B_P, T_P = 4, 2048
N_P = B_P * T_P
N_S = 128
N = N_P + N_S
DEPTH = 2
PAST_LEN = 16384
EPS = 1e-6
HEADS, DK, DV = 4, 64, 128
CHUNK = 64
GATE_TEMP = 16.0
POOL_WINDOWS = (2, 4, 8, 16)
POOL_BUF = 15
ROPE_BASE = 10000.0
N_EXPERTS = 64
TOP_K = 8
D_EXPERT = 512
ROUTED_SCALE = 2.5

LANES = 128
SUB = 128
MOD_ROWS = 256
EXP_BLOCK = 256
N_ASSIGN = N * TOP_K
N_BLOCKS = -(-(N_ASSIGN + N_EXPERTS * (EXP_BLOCK - 1)) // EXP_BLOCK)
L_SLOTS = N_BLOCKS * EXP_BLOCK
VMEM_LIMIT = 56 * 1024 * 1024

C_GQ, C_GK, C_GV, C_GR, C_PIN, C_RQ, C_RK, C_RV, C_RG, C_SU, C_SV = (
    0, 256, 512, 1024, 1536, 2048, 2304, 2560, 3072, 3584, 4096)
P_MAIN = 4608


def _cparams(n_axes=1):
    return pltpu.CompilerParams(dimension_semantics=("arbitrary",) * n_axes,
                                vmem_limit_bytes=VMEM_LIMIT)


def _silu(x):
    return x * jax.nn.sigmoid(x)


def _mod_rows(t, mp_ref, ms_ref):
    b = jnp.minimum(t // (T_P // SUB), B_P - 1)
    return jnp.where(t >= N_P // SUB, ms_ref[...], mp_ref[pl.ds(b, 1), :])


def _mod_specs(layer, part):
    return [pl.BlockSpec((None, 8, D), lambda i, l=layer, p=part: (l, 0, p)),
            pl.BlockSpec((None, SUB, D), lambda i, l=layer, p=part: (l, 1, p))]


def _pack_bf16_pair(lo, hi):
    lo_u = lax.bitcast_convert_type(lo.astype(BF16).astype(F32), jnp.uint32)
    hi_u = lax.bitcast_convert_type(hi.astype(BF16).astype(F32), jnp.uint32)
    return lax.bitcast_convert_type((hi_u & jnp.uint32(0xFFFF0000)) | (lo_u >> 16), jnp.int32)


def _unpack_bf16_pair(w):
    u = lax.bitcast_convert_type(w, jnp.uint32)
    lo = lax.bitcast_convert_type(u << 16, F32)
    hi = lax.bitcast_convert_type(u & jnp.uint32(0xFFFF0000), F32)
    return lo, hi


ROW_TILE = (8, LANES)


def _load_row_tiles(ref):
    return jnp.concatenate([ref[:, c, :] for c in range(ROW_TILE[0])], axis=1)


def _store_row_tiles(ref, val):
    for c in range(ROW_TILE[0]):
        ref[:, c, :] = val[:, c * LANES:(c + 1) * LANES]


def _load_row_tiles_2d(ref, rows):
    return jnp.concatenate([ref[pl.ds(c, rows, stride=ROW_TILE[0]), :] for c in range(ROW_TILE[0])], axis=1)


def _store_row_tiles_2d(ref, val, rows):
    for c in range(ROW_TILE[0]):
        ref[pl.ds(c, rows, stride=ROW_TILE[0]), :] = val[:, c * LANES:(c + 1) * LANES]


def _ada_body(c_ref, w_ref, b_ref, o_ref):
    s = _silu(c_ref[...]).astype(BF16)
    o_ref[...] = jnp.dot(s, w_ref[...].astype(BF16), preferred_element_type=F32) + b_ref[...]


def _ada(c_all, w_ada, b_ada):
    tn = 1024
    return pl.pallas_call(
        _ada_body,
        grid=(DEPTH, 6 * D // tn),
        in_specs=[pl.BlockSpec((MOD_ROWS, D), lambda l, j: (0, 0)),
                  pl.BlockSpec((None, D, tn), lambda l, j: (l, 0, j)),
                  pl.BlockSpec((None, 1, tn), lambda l, j: (l, 0, j))],
        out_specs=pl.BlockSpec((None, MOD_ROWS, tn), lambda l, j: (l, 0, j)),
        out_shape=jax.ShapeDtypeStruct((DEPTH, MOD_ROWS, 6 * D), F32),
        compiler_params=_cparams(2), name="ada")(c_all, w_ada, b_ada.reshape(DEPTH, 1, 6 * D))


ZERO_ROWS = 2 * SUB


def _x_specs(x, tm, n_axes):
    row = (lambda i: (i, 0)) if n_axes == 1 else (lambda i, j: (i, 0))
    if not isinstance(x, tuple):
        return [pl.BlockSpec((tm, D), row)], [x]
    zero = (lambda i: (0, 0)) if n_axes == 1 else (lambda i, j: (0, 0))
    return [pl.BlockSpec((tm, D), row), pl.BlockSpec((N_S, D), zero)], list(x)


def _x_rows(x_refs, t, rows):
    if len(x_refs) == 1:
        return x_refs[0][rows, :]
    return jnp.where(t >= N_P // SUB, x_refs[1][...], x_refs[0][rows, :])


def _prenorm_rows(t, x, g_ref, shp_ref, shs_ref, scp_ref, scs_ref):
    y = x * lax.rsqrt(jnp.mean(x * x, axis=-1, keepdims=True) + EPS) * g_ref[...]
    return y * (1.0 + _mod_rows(t, scp_ref, scs_ref)) + _mod_rows(t, shp_ref, shs_ref)


TM_NORM = 640


def _prenorm_body(*refs):
    g_ref, shp_ref, shs_ref, scp_ref, scs_ref, h_ref = refs[-6:]
    for sidx in range(TM_NORM // SUB):
        rows = pl.ds(sidx * SUB, SUB)
        t = pl.program_id(0) * (TM_NORM // SUB) + sidx
        h_ref[rows, :] = _prenorm_rows(t, _x_rows(refs[:-6], t, rows), g_ref, shp_ref, shs_ref, scp_ref,
                                       scs_ref).astype(BF16)


def _prenorm(x, g, mod, layer):
    x_specs, x_args = _x_specs(x, TM_NORM, 1)
    return pl.pallas_call(
        _prenorm_body,
        grid=(N // TM_NORM,),
        in_specs=x_specs + [pl.BlockSpec((1, D), lambda i: (0, 0))] + _mod_specs(layer, 0) + _mod_specs(layer, 1),
        out_specs=pl.BlockSpec((TM_NORM, D), lambda i: (i, 0)),
        out_shape=jax.ShapeDtypeStruct((N, D), BF16),
        compiler_params=_cparams(1), name="prenorm")(*x_args, g.reshape(1, D), mod, mod, mod, mod)


def _mm_body(x_ref, w_ref, o_ref):
    o_ref[...] = jnp.dot(x_ref[...], w_ref[...], preferred_element_type=F32).astype(o_ref.dtype)


def _matmul(x, w_all, layer, tm, tn, out_dtype=F32, name="mm"):
    m, k = x.shape
    n = w_all.shape[2]
    return pl.pallas_call(
        _mm_body,
        grid=(m // tm, n // tn),
        in_specs=[pl.BlockSpec((tm, k), lambda i, j: (i, 0)),
                  pl.BlockSpec((None, k, tn), lambda i, j, l=layer: (l, 0, j))],
        out_specs=pl.BlockSpec((tm, tn), lambda i, j: (i, j)),
        out_shape=jax.ShapeDtypeStruct((m, n), out_dtype),
        compiler_params=_cparams(2), name=name)(x, w_all)


ROWS_LA = 256


def _swap_halves_lanes(x):
    lane = lax.broadcasted_iota(jnp.int32, x.shape, 1)
    return jnp.where((lane % 64) < 32, pltpu.roll(x, 96, 1), pltpu.roll(x, 32, 1))


def _rope_lanes(x, cos, sin_signed):
    parts = []
    for half in range(2):
        xh = x[:, half * LANES:(half + 1) * LANES]
        parts.append(xh * cos + _swap_halves_lanes(xh) * sin_signed)
    return jnp.concatenate(parts, axis=1)


def _la_prompt_body(q_ref, k_ref, v_ref, r_ref, aux_ref, aux2_ref, dec_ref, bias_ref, g_ref,
                    o_ref, st_out_ref, st_ref, *, retention):
    t = pl.program_id(1)

    @pl.when(t == 0)
    def _():
        st_ref[...] = jnp.zeros_like(st_ref)

    ri = lax.broadcasted_iota(jnp.int32, (CHUNK, CHUNK), 0)
    ci = lax.broadcasted_iota(jnp.int32, (CHUNK, CHUNK), 1)
    causal = ri >= ci
    tril = causal.astype(F32)
    scale = DK ** -0.5

    for c in range(ROWS_LA // CHUNK):
        rows = pl.ds(c * CHUNK, CHUNK)
        q = q_ref[rows, :]
        k = k_ref[rows, :]
        v = v_ref[rows, :]
        if retention:
            cos = aux_ref[rows, :]
            sin = aux2_ref[rows, :]
            q = _rope_lanes(q, cos, sin)
            k = _rope_lanes(k, cos, sin) * scale
            la = jnp.broadcast_to(dec_ref[...], (CHUNK, HEADS * DK))
        else:
            q = q * scale
            logit = jnp.dot(aux_ref[rows, :], dec_ref[...], precision=HIGHEST,
                            preferred_element_type=F32) + bias_ref[...]
            la = jax.nn.log_sigmoid(logit) / GATE_TEMP
        bc = jnp.dot(tril, la, precision=HIGHEST, preferred_element_type=F32)
        bl = bc[CHUNK - 1:CHUNK, :]
        qd = q * jnp.exp(bc)
        ki = k * jnp.exp(-bc)
        ke = k * jnp.exp(bl - bc)
        ac = jnp.exp(bl)
        outs = []
        for h in range(HEADS):
            ks = slice(h * DK, (h + 1) * DK)
            vs = slice(h * DV, (h + 1) * DV)
            qd_h = qd[:, ks].astype(BF16)
            ki_h = ki[:, ks].astype(BF16)
            ke_h = ke[:, ks].astype(BF16)
            v_h = v[:, vs].astype(BF16)
            sc = lax.dot_general(qd_h, ki_h, (((1,), (1,)), ((), ())), preferred_element_type=F32)
            sc = jnp.where(causal, sc, 0.0)
            o_h = jnp.dot(sc.astype(BF16), v_h, preferred_element_type=F32)
            st = st_ref[h]
            o_h = o_h + lax.dot_general(qd_h, st.astype(BF16), (((1,), (1,)), ((), ())),
                                        preferred_element_type=F32)
            kv_t = lax.dot_general(v_h, ke_h, (((0,), (0,)), ((), ())), preferred_element_type=F32)
            st_ref[h] = st * ac[:, ks] + kv_t
            o_n = o_h * lax.rsqrt(jnp.mean(o_h * o_h, axis=-1, keepdims=True) + EPS) * g_ref[:, vs]
            outs.append(o_n)
        o = jnp.concatenate(outs, axis=1) * _silu(r_ref[rows, :])
        o_ref[rows, :] = o.astype(BF16)

    st_out_ref[...] = st_ref[...]


def _la_prompt(p_main, cq, ck, cv, cr, aux, aux2, dec, bias, g, retention):
    nt = T_P // ROWS_LA
    rowblk = lambda b, t: b * nt + t
    if retention:
        aux_specs = [pl.BlockSpec((ROWS_LA, LANES), lambda b, t: (t, 0)),
                     pl.BlockSpec((ROWS_LA, LANES), lambda b, t: (t, 0))]
    else:
        aux_specs = [pl.BlockSpec((ROWS_LA, LANES), lambda b, t: (rowblk(b, t), 0)),
                     pl.BlockSpec((8, LANES), lambda b, t: (0, 0))]
    o, st = pl.pallas_call(
        functools.partial(_la_prompt_body, retention=retention),
        grid=(B_P, nt),
        in_specs=[pl.BlockSpec((ROWS_LA, 256), lambda b, t: (rowblk(b, t), cq // 256)),
                  pl.BlockSpec((ROWS_LA, 256), lambda b, t: (rowblk(b, t), ck // 256)),
                  pl.BlockSpec((ROWS_LA, 512), lambda b, t: (rowblk(b, t), cv // 512)),
                  pl.BlockSpec((ROWS_LA, 512), lambda b, t: (rowblk(b, t), cr // 512))]
        + aux_specs
        + [pl.BlockSpec(dec.shape, lambda b, t: (0, 0)),
           pl.BlockSpec((1, HEADS * DK), lambda b, t: (0, 0)),
           pl.BlockSpec((1, HEADS * DV), lambda b, t: (0, 0))],
        out_specs=[pl.BlockSpec((ROWS_LA, HEADS * DV), lambda b, t: (rowblk(b, t), 0)),
                   pl.BlockSpec((None, HEADS, DV, DK), lambda b, t: (b, 0, 0, 0))],
        out_shape=[jax.ShapeDtypeStruct((N, HEADS * DV), BF16),
                   jax.ShapeDtypeStruct((B_P, HEADS, DV, DK), F32)],
        scratch_shapes=[pltpu.VMEM((HEADS, DV, DK), F32)],
        compiler_params=_cparams(2), name="ret_prompt" if retention else "gla_prompt",
    )(p_main, p_main, p_main, p_main, aux, aux2, dec, bias, g)
    return o, jnp.swapaxes(st, -1, -2)


SAMPLE_TILE = 8


def _la_sample_body(qt_ref, kt_ref, lt_ref, cos_ref, sin_ref, v_ref, r_ref, g_ref, s_ref, *rest, retention):
    o_ref, s_out_ref = rest[-2:]
    scale = DK ** -0.5
    qt = qt_ref[...]
    kt = kt_ref[...]
    if retention:
        def rope(x):
            sw = jnp.concatenate(
                [x[h * DK + (DK // 2) * (1 - j): h * DK + (DK // 2) * (2 - j), :]
                 for h in range(HEADS) for j in range(2)], axis=0)
            return x * cos_ref[...] + sw * sin_ref[...]
        qt = rope(qt)
        kt = rope(kt) * scale
        la = lt_ref[...]
    else:
        qt = qt * scale
        la = jax.nn.log_sigmoid(lt_ref[...]) / GATE_TEMP
    at = jnp.exp(la)
    qd = qt * at
    ki = kt * jnp.exp(-la)
    prod = qd * ki
    v8 = v_ref[...]
    r8 = r_ref[...]
    g = g_ref[...]
    for j in range(SAMPLE_TILE):
        for h in range(HEADS):
            ks = slice(h * DK, (h + 1) * DK)
            vs = slice(h * DV, (h + 1) * DV)
            a_c = jnp.broadcast_to(at[ks, j:j + 1], (DK, DV))
            k_c = jnp.broadcast_to(kt[ks, j:j + 1], (DK, DV))
            q_c = jnp.broadcast_to(qd[ks, j:j + 1], (DK, DV))
            s_c = jnp.broadcast_to(jnp.sum(prod[ks, j:j + 1], axis=0, keepdims=True), (1, DV))
            s0 = s_ref[j, h]
            v_row = v8[j:j + 1, vs]
            s_out_ref[j, h] = a_c * s0 + k_c * v_row
            o_row = s_c * v_row + jnp.sum(q_c * s0, axis=0, keepdims=True)
            o_n = o_row * lax.rsqrt(jnp.mean(o_row * o_row, axis=-1, keepdims=True) + EPS) * g[:, vs]
            o_ref[j:j + 1, vs] = o_n * _silu(r8[j:j + 1, vs])


def _la_sample(qt, kt, lt, cos_t, sin_t, p_main, cv, cr, g, s0_all, layer, s_prev, retention):
    nt = N_S // SAMPLE_TILE
    row0 = N_P // SAMPLE_TILE
    tile = pl.BlockSpec((None, HEADS * DK, LANES), lambda i: (i, 0, 0))
    full = pl.BlockSpec((HEADS * DK, LANES), lambda i: (0, 0))
    lt_spec = full if retention else tile
    return pl.pallas_call(
        functools.partial(_la_sample_body, retention=retention),
        grid=(nt,),
        in_specs=[tile, tile, lt_spec, full, full,
                  pl.BlockSpec((SAMPLE_TILE, 512), lambda i: (row0 + i, cv // 512)),
                  pl.BlockSpec((SAMPLE_TILE, 512), lambda i: (row0 + i, cr // 512)),
                  pl.BlockSpec((1, HEADS * DV), lambda i: (0, 0)),
                  pl.BlockSpec((None, SAMPLE_TILE, HEADS, DK, DV), lambda i, l=layer: (l, i, 0, 0, 0))]
        + ([] if s_prev is None else [pl.BlockSpec(memory_space=pl.ANY)]),
        out_specs=[pl.BlockSpec((SAMPLE_TILE, HEADS * DV), lambda i: (i, 0)),
                   pl.BlockSpec((None, SAMPLE_TILE, HEADS, DK, DV), lambda i, l=layer: (l, i, 0, 0, 0))],
        out_shape=[jax.ShapeDtypeStruct((N_S, HEADS * DV), F32),
                   jax.ShapeDtypeStruct((DEPTH, N_S, HEADS, DK, DV), F32)],
        input_output_aliases={} if s_prev is None else {9: 1},
        compiler_params=_cparams(1), name="ret_sample" if retention else "gla_sample",
    )(qt, kt, lt, cos_t, sin_t, p_main, p_main, g, s0_all, *([] if s_prev is None else [s_prev]))


def _gate_logits_t_body(w_ref, x_ref, b_ref, o_ref):
    o_ref[...] = jnp.dot(w_ref[...], x_ref[...], precision=HIGHEST, preferred_element_type=F32) + b_ref[...]


def _gate_logits_t(w_gate_t, glow_t, b_col):
    nt = N_S // SAMPLE_TILE
    return pl.pallas_call(
        _gate_logits_t_body,
        grid=(nt,),
        in_specs=[pl.BlockSpec((HEADS * DK, LANES), lambda i: (0, 0)),
                  pl.BlockSpec((None, LANES, LANES), lambda i: (i, 0, 0)),
                  pl.BlockSpec((HEADS * DK, LANES), lambda i: (0, 0))],
        out_specs=pl.BlockSpec((None, HEADS * DK, LANES), lambda i: (i, 0, 0)),
        out_shape=jax.ShapeDtypeStruct((nt, HEADS * DK, LANES), F32),
        compiler_params=_cparams(1), name="gate_logits_t")(w_gate_t, glow_t, b_col)


def _to_tiles_t(x):
    c = x.shape[1]
    xt = jnp.swapaxes(x.reshape(N_S // SAMPLE_TILE, SAMPLE_TILE, c), 1, 2)
    return jnp.pad(xt, ((0, 0), (0, 0), (0, LANES - SAMPLE_TILE)))


ROWS_POOL = 512


def _pool_mix(y, w_ref, sc_ref):
    outs = []
    for gi in range(4):
        cs = slice(gi * LANES, (gi + 1) * LANES)
        outs.append(jnp.dot(y[:, cs].astype(BF16), w_ref[gi], preferred_element_type=F32))
    return jnp.concatenate(outs, axis=1) * sc_ref[...]


def _pool_prompt_body(p_ref, halo_ref, w_ref, sc_ref, o_ref):
    t = pl.program_id(1)
    p = p_ref[...]
    halo = jnp.where(t == 0, 0.0, halo_ref[...])
    full = jnp.concatenate([halo, p], axis=0)
    pos = t * ROWS_POOL + lax.broadcasted_iota(jnp.int32, (ROWS_POOL, LANES), 0)
    means = []
    for gi, w in enumerate(POOL_WINDOWS):
        s = full[:, gi * LANES:(gi + 1) * LANES]
        step = 1
        while step < w:
            s = s + pltpu.roll(s, step, 0)
            step *= 2
        win = s[16:, :]
        cnt = jnp.minimum(w, pos + 1).astype(F32)
        means.append(win / cnt)
    y = jnp.concatenate(means, axis=1) - p
    o_ref[...] = _pool_mix(y, w_ref, sc_ref).astype(BF16)


def _pool_prompt(p_main, w_bf, scale):
    nt = T_P // ROWS_POOL
    return pl.pallas_call(
        _pool_prompt_body,
        grid=(B_P, nt),
        in_specs=[pl.BlockSpec((ROWS_POOL, 512), lambda b, t: (b * nt + t, C_PIN // 512)),
                  pl.BlockSpec((16, 512), lambda b, t: (jnp.maximum((b * nt + t) * (ROWS_POOL // 16) - 1, 0),
                                                        C_PIN // 512)),
                  pl.BlockSpec((4, LANES, LANES), lambda b, t: (0, 0, 0)),
                  pl.BlockSpec((1, 512), lambda b, t: (0, 0))],
        out_specs=pl.BlockSpec((ROWS_POOL, 512), lambda b, t: (b * nt + t, 0)),
        out_shape=jax.ShapeDtypeStruct((N, 512), BF16),
        compiler_params=_cparams(2), name="pool_prompt")(p_main, p_main, w_bf, scale)


def _small_sample_body(p_ref, buf_ref, pw_ref, psc_ref, u_ref, sv_ref, sg_ref, sw_ref, sb_ref,
                       ob_ref, od_ref, vn_ref):
    p = p_ref[...]
    means = []
    for gi, w in enumerate(POOL_WINDOWS):
        cs = slice(gi * LANES, (gi + 1) * LANES)
        s = p[:, cs]
        for j in range(1, w):
            s = s + buf_ref[:, POOL_BUF - j, cs]
        means.append(s / float(min(w, PAST_LEN + 1)))
    y = jnp.concatenate(means, axis=1) - p
    ob_ref[...] = _pool_mix(y, pw_ref, psc_ref)
    sv = sv_ref[...]
    vn = sv * lax.rsqrt(jnp.mean(sv * sv, axis=-1, keepdims=True) + EPS) * sg_ref[...]
    vn_ref[...] = vn
    od_ref[...] = u_ref[...] * (sw_ref[...] * vn + sb_ref[...])


def _small_sample(p_main, buf, pw_bf, pscale, sgu_g, sgu_w0, sgu_b0):
    row = N_P // N_S
    col = lambda c: pl.BlockSpec((N_S, 512), lambda i, c=c: (row, c // 512))
    vec = pl.BlockSpec((1, 512), lambda i: (0, 0))
    return pl.pallas_call(
        _small_sample_body,
        grid=(1,),
        in_specs=[col(C_PIN), pl.BlockSpec((N_S, POOL_BUF, 512), lambda i: (0, 0, 0)),
                  pl.BlockSpec((4, LANES, LANES), lambda i: (0, 0, 0)), vec,
                  col(C_SU), col(C_SV), vec, vec, vec],
        out_specs=[pl.BlockSpec((N_S, 512), lambda i: (0, 0))] * 3,
        out_shape=[jax.ShapeDtypeStruct((N_S, 512), F32)] * 3,
        compiler_params=_cparams(1), name="small_sample",
    )(p_main, buf, pw_bf, pscale, p_main, p_main, sgu_g, sgu_w0, sgu_b0)


ROWS_SGU = 512
SGU_CHUNK = 128


def _sgu_prompt_body(u_ref, v_ref, g_ref, w_ref, bt_ref, o_ref):
    ri = lax.broadcasted_iota(jnp.int32, (SGU_CHUNK, SGU_CHUNK), 0)
    ci = lax.broadcasted_iota(jnp.int32, (SGU_CHUNK, SGU_CHUNK), 1)
    causal = ri >= ci
    for c in range(ROWS_SGU // SGU_CHUNK):
        rows = pl.ds(c * SGU_CHUNK, SGU_CHUNK)
        v = v_ref[rows, :]
        vn = (v * lax.rsqrt(jnp.mean(v * v, axis=-1, keepdims=True) + EPS) * g_ref[...]).astype(BF16)
        outs = []
        for gi in range(4):
            cs = slice(gi * LANES, (gi + 1) * LANES)
            w = jnp.where(causal, w_ref[gi], 0.0).astype(BF16)
            mixed = jnp.dot(w, vn[:, cs], preferred_element_type=F32)
            outs.append(mixed + jnp.broadcast_to(bt_ref[:, gi:gi + 1], (SGU_CHUNK, LANES)))
        o_ref[rows, :] = (u_ref[rows, :] * jnp.concatenate(outs, axis=1)).astype(BF16)


def _sgu_prompt(p_main, g, w, b_t):
    return pl.pallas_call(
        _sgu_prompt_body,
        grid=(N_P // ROWS_SGU,),
        in_specs=[pl.BlockSpec((ROWS_SGU, 512), lambda i: (i, C_SU // 512)),
                  pl.BlockSpec((ROWS_SGU, 512), lambda i: (i, C_SV // 512)),
                  pl.BlockSpec((1, 512), lambda i: (0, 0)),
                  pl.BlockSpec((4, SGU_CHUNK, SGU_CHUNK), lambda i: (0, 0, 0)),
                  pl.BlockSpec((SGU_CHUNK, LANES), lambda i: (0, 0))],
        out_specs=pl.BlockSpec((ROWS_SGU, 512), lambda i: (i, 0)),
        out_shape=jax.ShapeDtypeStruct((N, 512), BF16),
        compiler_params=_cparams(1), name="sgu_prompt")(p_main, p_main, g, w, b_t)


TM_MERGE = 640
TN_MERGE = 512


def _merge_body(h_ref, ba_ref, bb_ref, bc_ref, bd_ref, g0, g1, g2, g3, u0, u1, u2, u3,
                c0, c1, c2, c3, o_ref):
    h = h_ref[...]
    acc = None
    for br, gw, uw, gb in ((ba_ref, g0, u0, c0), (bb_ref, g1, u1, c1), (bc_ref, g2, u2, c2), (bd_ref, g3, u3, c3)):
        gate = jax.nn.sigmoid(jnp.dot(h, gw[...], preferred_element_type=F32) + gb[...])
        up = jnp.dot(br[...], uw[...], preferred_element_type=F32)
        acc = gate * up if acc is None else acc + gate * up
    o_ref[...] = acc.astype(BF16)


def _merge(h, branches, w_mg, b_mg, w_br, layer):
    nj = D // TN_MERGE
    row = lambda w: pl.BlockSpec((TM_MERGE, w), lambda i, j: (i, 0))
    gate_w = [pl.BlockSpec((None, D, TN_MERGE), lambda i, j, b=b, l=layer: (l, 0, b * nj + j)) for b in range(4)]
    up_w = [pl.BlockSpec((None, None, 512, TN_MERGE), lambda i, j, b=b, l=layer: (l, b, 0, j)) for b in range(4)]
    gate_b = [pl.BlockSpec((None, 1, TN_MERGE), lambda i, j, b=b, l=layer: (l, 0, b * nj + j)) for b in range(4)]
    return pl.pallas_call(
        _merge_body,
        grid=(N // TM_MERGE, nj),
        in_specs=[row(D)] + [row(512)] * 4 + gate_w + up_w + gate_b,
        out_specs=pl.BlockSpec((TM_MERGE, TN_MERGE), lambda i, j: (i, j)),
        out_shape=jax.ShapeDtypeStruct((N, D), BF16),
        compiler_params=_cparams(2), name="merge",
    )(h, *branches, w_mg, w_mg, w_mg, w_mg, w_br, w_br, w_br, w_br, b_mg, b_mg, b_mg, b_mg)


TM_OUT = 640
TN_OUT = 512


def _post_value(t, x, y, gn_ref, gp_ref, gs_ref):
    yn = y * lax.rsqrt(jnp.mean(y * y, axis=-1, keepdims=True) + EPS) * gn_ref[...]
    return x + _mod_rows(t, gp_ref, gs_ref) * yn


def _outproj_body(m_ref, w_ref, *refs):
    gn_ref, gp_ref, gs_ref, o_ref, acc_ref = refs[-5:]
    j = pl.program_id(1)
    acc_ref[j] = jnp.dot(m_ref[...], w_ref[...], preferred_element_type=F32)

    @pl.when(j == D // TN_OUT - 1)
    def _():
        for sidx in range(TM_OUT // SUB):
            rows = pl.ds(sidx * SUB, SUB)
            t = pl.program_id(0) * (TM_OUT // SUB) + sidx
            y = jnp.concatenate([acc_ref[c, rows, :] for c in range(D // TN_OUT)], axis=1)
            o_ref[rows, :] = _post_value(t, _x_rows(refs[:-5], t, rows), y, gn_ref, gp_ref, gs_ref)


def _outproj(merged, w_out, x, g_post, mod, layer):
    mspec = [pl.BlockSpec((None, 8, D), lambda i, j, l=layer: (l, 0, 2)),
             pl.BlockSpec((None, SUB, D), lambda i, j, l=layer: (l, 1, 2))]
    x_specs, x_args = _x_specs(x, TM_OUT, 2)
    return pl.pallas_call(
        _outproj_body,
        grid=(N // TM_OUT, D // TN_OUT),
        in_specs=[pl.BlockSpec((TM_OUT, D), lambda i, j: (i, 0)),
                  pl.BlockSpec((None, D, TN_OUT), lambda i, j, l=layer: (l, 0, j))] + x_specs
        + [pl.BlockSpec((1, D), lambda i, j: (0, 0))] + mspec,
        out_specs=pl.BlockSpec((TM_OUT, D), lambda i, j: (i, 0)),
        out_shape=jax.ShapeDtypeStruct((N, D), F32),
        scratch_shapes=[pltpu.VMEM((D // TN_OUT, TM_OUT, TN_OUT), F32)],
        compiler_params=_cparams(2), name="outproj",
    )(merged, w_out, *x_args, g_post.reshape(1, D), mod, mod)


def _router_body(lg_ref, b_ref, eid_ref, pos_ref, wt_ref, cnt_ref, run_ref):
    i = pl.program_id(0)

    @pl.when(i == 0)
    def _():
        run_ref[...] = jnp.zeros_like(run_ref)

    ng, gs = 8, N_EXPERTS // 8
    neg = -jnp.inf
    scores = jax.nn.sigmoid(lg_ref[...].T[:N_EXPERTS, :])
    sel = scores + b_ref[...]
    sel3 = sel.reshape(ng, gs, SUB)
    sub3 = lax.broadcasted_iota(jnp.int32, (ng, gs, SUB), 1)
    gmax = jnp.max(sel3, axis=1, keepdims=True)
    first = jnp.min(jnp.where(sel3 == gmax, sub3, gs), axis=1, keepdims=True)
    gmax2 = jnp.max(jnp.where(sub3 == first, neg, sel3), axis=1, keepdims=True)
    gscore = (gmax + gmax2).reshape(ng, SUB)
    gidx = lax.broadcasted_iota(jnp.int32, (ng, SUB), 0)
    grank = jnp.zeros((ng, SUB), jnp.int32)
    for s in range(1, ng):
        other = pltpu.roll(gscore, s, 0)
        lower = gidx >= s
        grank += ((other > gscore) | ((other == gscore) & lower)).astype(jnp.int32)
    keep = jnp.broadcast_to((grank < 4).reshape(ng, 1, SUB), (ng, gs, SUB))
    masked = jnp.where(keep, sel3, neg).reshape(N_EXPERTS, SUB)
    eidx = lax.broadcasted_iota(jnp.int32, (N_EXPERTS, SUB), 0)
    rank = jnp.zeros((N_EXPERTS, SUB), jnp.int32)
    for s in range(1, N_EXPERTS):
        other = pltpu.roll(masked, s, 0)
        lower = eidx >= s
        rank += ((other > masked) | ((other == masked) & lower)).astype(jnp.int32)
    chosen = rank < TOP_K
    w_sel = jnp.where(chosen, scores, 0.0)
    w_sel = w_sel / jnp.sum(w_sel, axis=0, keepdims=True) * ROUTED_SCALE
    ri = lax.broadcasted_iota(jnp.int32, (SUB, SUB), 0)
    ci = lax.broadcasted_iota(jnp.int32, (SUB, SUB), 1)
    onehot = chosen.astype(BF16)
    pos = jnp.dot(onehot, (ri < ci).astype(BF16), preferred_element_type=F32) + run_ref[...]
    run_ref[...] = run_ref[...] + jnp.sum(chosen.astype(F32), axis=1, keepdims=True)
    cnt_ref[...] = run_ref[...]
    eidx_f = eidx.astype(F32)
    rows_e, rows_p, rows_w = [], [], []
    for kk in range(TOP_K):
        m = chosen & (rank == kk)
        rows_e.append(jnp.sum(jnp.where(m, eidx_f, 0.0), axis=0, keepdims=True))
        rows_p.append(jnp.sum(jnp.where(m, pos, 0.0), axis=0, keepdims=True))
        rows_w.append(jnp.sum(jnp.where(m, w_sel, 0.0), axis=0, keepdims=True))
    eid_ref[...] = jnp.concatenate(rows_e, axis=0).astype(jnp.int32)
    pos_ref[...] = jnp.concatenate(rows_p, axis=0).astype(jnp.int32)
    wt_ref[...] = jnp.concatenate(rows_w, axis=0)


def _router(logits, rb_col):
    tile = pl.BlockSpec((TOP_K, SUB), lambda i: (0, i))
    return pl.pallas_call(
        _router_body,
        grid=(N // SUB,),
        in_specs=[pl.BlockSpec((SUB, LANES), lambda i: (i, 0)),
                  pl.BlockSpec((N_EXPERTS, SUB), lambda i: (0, 0))],
        out_specs=[tile, tile, tile, pl.BlockSpec((N_EXPERTS, SUB), lambda i: (0, 0))],
        out_shape=[jax.ShapeDtypeStruct((TOP_K, N), jnp.int32), jax.ShapeDtypeStruct((TOP_K, N), jnp.int32),
                   jax.ShapeDtypeStruct((TOP_K, N), F32), jax.ShapeDtypeStruct((N_EXPERTS, SUB), F32)],
        scratch_shapes=[pltpu.VMEM((N_EXPERTS, SUB), F32)],
        compiler_params=_cparams(1), name="router")(logits, rb_col)


SC_CORES, SC_SUBCORES = 2, 16
SC_WORKERS = SC_CORES * SC_SUBCORES
SC_LANES = 16
SC_CHUNK = 16
SC_SCAN = N_ASSIGN // SC_WORKERS


def _sc_mesh():
    return plsc.VectorSubcoreMesh(core_axis_name="c", subcore_axis_name="s",
                                  num_cores=SC_CORES, num_subcores=SC_SUBCORES)


def _sc_worker_base(per_w):
    return (lax.axis_index("s") * SC_CORES + lax.axis_index("c")) * per_w


def _sc_gather_rows(table_hbm, out_hbm, idx_v, rows_v, gsem, wsem, base, per_w):
    n_ch = per_w // SC_CHUNK
    assert n_ch % 2 == 0

    def gather(j, p):
        off = pl.multiple_of(j * SC_CHUNK, SC_CHUNK)
        return pltpu.make_async_copy(table_hbm.at[idx_v.at[pl.ds(off, SC_CHUNK)]], rows_v.at[p], gsem.at[p])

    def write(j, p):
        off = pl.multiple_of(j * SC_CHUNK, SC_CHUNK)
        return pltpu.make_async_copy(rows_v.at[p], out_hbm.at[pl.ds(base + off, SC_CHUNK)], wsem.at[p])

    gather(0, 0).start()

    @pl.loop(0, n_ch, step=2)
    def _(j0):
        for p in range(2):
            j = j0 + p
            gather(j, p).wait()

            @pl.when(j >= 1)
            def _():
                write(j - 1, 1 - p).wait()

            @pl.when(j + 1 < n_ch)
            def _():
                gather(j + 1, 1 - p).start()
            write(j, p).start()

    write(n_ch - 1, 1).wait()


_SC_ROW_SCRATCH = [pltpu.VMEM((2, SC_CHUNK) + ROW_TILE, jnp.int32),
                   pltpu.SemaphoreType.DMA((2,)), pltpu.SemaphoreType.DMA((2,))]


def _sc_gather(table, idx):
    n_out = idx.shape[0]
    per_w = n_out // SC_WORKERS
    assert per_w * SC_WORKERS == n_out and per_w % (2 * SC_CHUNK) == 0

    def body(table_hbm, idx_hbm, out_hbm, idx_v, rows_v, gsem, wsem):
        base = _sc_worker_base(per_w)
        pltpu.sync_copy(idx_hbm.at[pl.ds(base, per_w)], idx_v)
        _sc_gather_rows(table_hbm, out_hbm, idx_v, rows_v, gsem, wsem, base, per_w)

    return pl.kernel(
        body, out_type=jax.ShapeDtypeStruct((n_out,) + ROW_TILE, jnp.int32), mesh=_sc_mesh(),
        scratch_types=[pltpu.VMEM((per_w,), jnp.int32)] + _SC_ROW_SCRATCH, name="sc_gather")(table, idx)


N_PARTS = 9
PART_BLOCKS = N_BLOCKS // N_PARTS
PART_SLOTS = PART_BLOCKS * EXP_BLOCK
assert PART_BLOCKS * N_PARTS == N_BLOCKS


def _sc_dispatch(table, slots, part):
    per_w = PART_SLOTS // SC_WORKERS
    assert per_w * SC_WORKERS == PART_SLOTS and per_w % (2 * SC_CHUNK) == 0 and SC_SCAN % SC_LANES == 0
    assert ZERO_ROWS & (ZERO_ROWS - 1) == 0

    def body(table_hbm, slots_hbm, out_hbm, idx_v, sl_v, rows_v, gsem, wsem):
        local = _sc_worker_base(per_w)
        base = part * PART_SLOTS + local
        lane = lax.iota(jnp.int32, SC_LANES)

        @pl.loop(0, per_w // SC_LANES)
        def _(j):
            off = pl.multiple_of(j * SC_LANES, SC_LANES)
            idx_v[pl.ds(off, SC_LANES)] = N + ((base + off + lane) & (ZERO_ROWS - 1))

        @pl.loop(0, N_ASSIGN // SC_SCAN)
        def _(c):
            pltpu.sync_copy(slots_hbm.at[pl.ds(pl.multiple_of(c * SC_SCAN, 8), SC_SCAN)], sl_v)

            @pl.loop(0, SC_SCAN // SC_LANES)
            def _(j):
                off = pl.multiple_of(j * SC_LANES, SC_LANES)
                loc = sl_v[pl.ds(off, SC_LANES)] - base
                mine = (loc >= 0) & (loc < per_w)
                tok = lax.shift_right_logical(c * SC_SCAN + off + lane, 3)
                plsc.store_scatter(idx_v, [jnp.where(mine, loc, 0)], tok, mask=mine)

        _sc_gather_rows(table_hbm, out_hbm, idx_v, rows_v, gsem, wsem, local, per_w)

    return pl.kernel(
        body, out_type=jax.ShapeDtypeStruct((PART_SLOTS,) + ROW_TILE, jnp.int32), mesh=_sc_mesh(),
        scratch_types=[pltpu.VMEM((per_w,), jnp.int32), pltpu.VMEM((SC_SCAN,), jnp.int32)] + _SC_ROW_SCRATCH,
        compiler_params=pltpu.CompilerParams(needs_layout_passes=False),
        name="sc_dispatch")(table, slots)


def _experts_body(be_ref, first_ref, par_ref, next_ref, nextblk_ref, nused_ref, x_ref, w1_hbm, w3_hbm, w2_hbm,
                  *rest, layer, part):
    y_ref, w1f, w3f, w2f, w1b, w3b, w2b, sem = rest[-8:]
    i = pl.program_id(0)
    b = part * PART_BLOCKS + i
    used = b < nused_ref[0]

    def copies(e, slot):
        return (pltpu.make_async_copy(w1_hbm.at[layer, e], w1f.at[slot], sem.at[0, slot]),
                pltpu.make_async_copy(w3_hbm.at[layer, e], w3f.at[slot], sem.at[1, slot]),
                pltpu.make_async_copy(w2_hbm.at[layer, e], w2f.at[slot], sem.at[2, slot]))

    @pl.when(used & (i == 0))
    def _():
        for c in copies(be_ref[b], par_ref[b]):
            c.start()

    @pl.when(used & ((i == 0) | (first_ref[b] == 1)))
    def _():
        slot = par_ref[b]
        for c in copies(be_ref[b], slot):
            c.wait()

        @pl.when((next_ref[b] >= 0) & (nextblk_ref[b] < (part + 1) * PART_BLOCKS))
        def _():
            for c in copies(next_ref[b], 1 - slot):
                c.start(priority=1)
        w1b[...] = w1f[slot].astype(BF16)
        w3b[...] = w3f[slot].astype(BF16)
        w2b[...] = w2f[slot].astype(BF16)

    @pl.when(used)
    def _():
        lo, hi = _unpack_bf16_pair(_load_row_tiles_2d(x_ref, EXP_BLOCK))
        lo = lo.astype(BF16)
        hi = hi.astype(BF16)
        half = D // 2
        h1 = (jnp.dot(lo, w1b[:half, :], preferred_element_type=F32)
              + jnp.dot(hi, w1b[half:, :], preferred_element_type=F32))
        h3 = (jnp.dot(lo, w3b[:half, :], preferred_element_type=F32)
              + jnp.dot(hi, w3b[half:, :], preferred_element_type=F32))
        hid = (_silu(h1) * h3).astype(BF16)
        y = jnp.dot(hid, w2b[...], preferred_element_type=F32)
        _store_row_tiles_2d(y_ref, _pack_bf16_pair(y[:, :half], y[:, half:]), EXP_BLOCK)

    @pl.when(jnp.logical_not(used))
    def _():
        y_ref[...] = jnp.zeros_like(y_ref)


def _experts(ctl, xs_part, w1, w3, w2, layer, part, ys_prev):
    def x_blk(i, *refs):
        n_here = jnp.clip(refs[-1][0] - part * PART_BLOCKS, 1, PART_BLOCKS)
        return (jnp.minimum(i, n_here - 1), 0)
    any_spec = pl.BlockSpec(memory_space=pl.ANY)
    in_specs = [pl.BlockSpec((EXP_BLOCK * ROW_TILE[0], LANES), x_blk), any_spec, any_spec, any_spec]
    args = [xs_part.reshape(PART_SLOTS * ROW_TILE[0], LANES), w1, w3, w2]
    aliases = {}
    if ys_prev is not None:
        in_specs.append(any_spec)
        args.append(ys_prev)
        aliases = {len(ctl) + 4: 0}
    grid_spec = pltpu.PrefetchScalarGridSpec(
        num_scalar_prefetch=len(ctl),
        grid=(PART_BLOCKS,),
        in_specs=in_specs,
        out_specs=pl.BlockSpec((EXP_BLOCK * ROW_TILE[0], LANES), lambda i, *refs: (part * PART_BLOCKS + i, 0)),
        scratch_shapes=[pltpu.VMEM((2, D, D_EXPERT), F32), pltpu.VMEM((2, D, D_EXPERT), F32),
                        pltpu.VMEM((2, D_EXPERT, D), F32),
                        pltpu.VMEM((D, D_EXPERT), BF16), pltpu.VMEM((D, D_EXPERT), BF16),
                        pltpu.VMEM((D_EXPERT, D), BF16), pltpu.SemaphoreType.DMA((3, 2))])
    return pl.pallas_call(
        functools.partial(_experts_body, layer=layer, part=part), grid_spec=grid_spec,
        out_shape=jax.ShapeDtypeStruct((L_SLOTS * ROW_TILE[0], LANES), jnp.int32),
        input_output_aliases=aliases,
        compiler_params=_cparams(1), name="experts")(*ctl, *args)


TM_FFN = 640


def _ffn_pre_body(x_ref, g_ref, shp_ref, shs_ref, scp_ref, scs_ref, rw_ref, w13_ref, w2_ref,
                  sh_ref, lg_ref, hp_ref, hb_ref):
    i = pl.program_id(0)

    @pl.when(i < N // TM_FFN)
    def _():
        for sidx in range(TM_FFN // SUB):
            rows = pl.ds(sidx * SUB, SUB)
            h = _prenorm_rows(i * (TM_FFN // SUB) + sidx, x_ref[rows, :], g_ref, shp_ref, shs_ref, scp_ref, scs_ref)
            lg_ref[rows, :] = jnp.dot(h, rw_ref[...], precision=HIGHEST, preferred_element_type=F32)
            packed = _pack_bf16_pair(h[:, :D // 2], h[:, D // 2:])
            for c in range(ROW_TILE[0]):
                hp_ref[pl.ds(sidx * SUB * ROW_TILE[0] + c, SUB, stride=ROW_TILE[0]), :] = (
                    packed[:, c * LANES:(c + 1) * LANES])
            hb_ref[rows, :] = h.astype(BF16)
        up = jnp.dot(hb_ref[...], w13_ref[...], preferred_element_type=F32)
        hid = (_silu(up[:, :D_EXPERT]) * up[:, D_EXPERT:]).astype(BF16)
        sh_ref[...] = jnp.dot(hid, w2_ref[...], preferred_element_type=F32)

    @pl.when(i >= N // TM_FFN)
    def _():
        hp_ref[...] = jnp.zeros_like(hp_ref)


def _ffn_pre(x, g, mod, layer, rw, w13, w2):
    last = N // TM_FFN - 1
    row = lambda i: (jnp.minimum(i, last), 0)
    mspec = lambda part, rows, blk: pl.BlockSpec((None, rows, D), lambda i, l=layer, p=part, b=blk: (l, b, p))
    full = lambda shape: pl.BlockSpec(shape, lambda i: (0, 0))
    of_layer = lambda shape: pl.BlockSpec((None,) + shape, lambda i, l=layer: (l, 0, 0))
    return pl.pallas_call(
        _ffn_pre_body,
        grid=(N // TM_FFN + 1,),
        in_specs=[pl.BlockSpec((TM_FFN, D), row), full((1, D)),
                  mspec(3, 8, 0), mspec(3, SUB, 1), mspec(4, 8, 0), mspec(4, SUB, 1),
                  of_layer((D, LANES)), of_layer((D, 2 * D_EXPERT)), of_layer((D_EXPERT, D))],
        out_specs=[pl.BlockSpec((TM_FFN, D), row), pl.BlockSpec((TM_FFN, LANES), row),
                   pl.BlockSpec((TM_FFN * ROW_TILE[0], LANES), lambda i: (i, 0))],
        out_shape=[jax.ShapeDtypeStruct((N, D), F32), jax.ShapeDtypeStruct((N, LANES), F32),
                   jax.ShapeDtypeStruct(((N + TM_FFN) * ROW_TILE[0], LANES), jnp.int32)],
        scratch_shapes=[pltpu.VMEM((TM_FFN, D), BF16)],
        compiler_params=_cparams(1), name="ffn_pre")(x, g.reshape(1, D), mod, mod, mod, mod, rw, w13, w2)


def _combine_body(g_ref, wt_ref, sh_ref, x_ref, gn_ref, gp_ref, gs_ref, *rest, tile0, n_out, final):
    outs = rest[-n_out:]
    half = D // 2
    acc_lo = sh_ref[:, :half]
    acc_hi = sh_ref[:, half:]
    wt = wt_ref[...]
    per_tok = TOP_K * ROW_TILE[0]
    for k in range(TOP_K):
        packed = jnp.concatenate([g_ref[pl.ds(k * ROW_TILE[0] + c, SUB, stride=per_tok), :]
                                  for c in range(ROW_TILE[0])], axis=1)
        lo, hi = _unpack_bf16_pair(packed)
        w_c = wt[:, k:k + 1]
        acc_lo = acc_lo + w_c * lo
        acc_hi = acc_hi + w_c * hi
    t = tile0 + pl.program_id(0)
    val = _post_value(t, x_ref[...], jnp.concatenate([acc_lo, acc_hi], axis=1), gn_ref, gp_ref, gs_ref)
    if not final:
        outs[0][...] = val
    else:
        @pl.when(t < N_P // SUB)
        def _():
            outs[0][...] = val
        if n_out == 2:
            @pl.when(t >= N_P // SUB)
            def _():
                outs[1][...] = val


def _combine(gathered, wts, shared, x, g_post, mod, layer, tile0, n_tiles, out_prev, final):
    per_tok = TOP_K * ROW_TILE[0]
    row = lambda i: (tile0 + i, 0)
    in_specs = [pl.BlockSpec((SUB * per_tok, LANES), lambda i: (i, 0)),
                pl.BlockSpec((SUB, LANES), row), pl.BlockSpec((SUB, D), row), pl.BlockSpec((SUB, D), row),
                pl.BlockSpec((1, D), lambda i: (0, 0)),
                pl.BlockSpec((None, 8, D), lambda i, l=layer: (l, 0, 5)),
                pl.BlockSpec((None, SUB, D), lambda i, l=layer: (l, 1, 5))]
    args = [gathered.reshape(n_tiles * SUB * per_tok, LANES), wts, shared, x, g_post.reshape(1, D), mod, mod]
    aliases = {}
    if out_prev is not None:
        in_specs.append(pl.BlockSpec(memory_space=pl.ANY))
        args.append(out_prev)
        aliases = {len(args) - 1: 0}
    if not final:
        out_specs = [pl.BlockSpec((SUB, D), row)]
        out_shape = [jax.ShapeDtypeStruct((N, D), F32)]
    else:
        last_p = N_P // SUB - 1
        out_specs = [pl.BlockSpec((SUB, D), lambda i: (jnp.minimum(tile0 + i, last_p), 0))]
        out_shape = [jax.ShapeDtypeStruct((N_P, D), F32)]
        if tile0 + n_tiles > N_P // SUB:
            out_specs.append(pl.BlockSpec((N_S, D), lambda i: (0, 0)))
            out_shape.append(jax.ShapeDtypeStruct((N_S, D), F32))
    return pl.pallas_call(
        functools.partial(_combine_body, tile0=tile0, n_out=len(out_shape), final=final),
        grid=(n_tiles,),
        in_specs=in_specs, out_specs=out_specs, out_shape=out_shape,
        input_output_aliases=aliases,
        compiler_params=_cparams(1), name="combine")(*args)


def _slots_body(start_ref, eid_ref, pos_ref, o_ref):
    eid = eid_ref[...]
    acc = pos_ref[...]
    for e in range(N_EXPERTS):
        acc = acc + jnp.where(eid == e, start_ref[e], 0)
    o_ref[...] = acc


def _slots(pad_start, eid, pos):
    grid_spec = pltpu.PrefetchScalarGridSpec(
        num_scalar_prefetch=1, grid=(1,),
        in_specs=[pl.BlockSpec((TOP_K, N), lambda i, s: (0, 0)), pl.BlockSpec((TOP_K, N), lambda i, s: (0, 0))],
        out_specs=pl.BlockSpec((TOP_K, N), lambda i, s: (0, 0)))
    return pl.pallas_call(_slots_body, grid_spec=grid_spec,
                          out_shape=jax.ShapeDtypeStruct((TOP_K, N), jnp.int32),
                          compiler_params=_cparams(1), name="slots")(pad_start, eid, pos)


def _put_sample_rows_body(*refs):
    n = len(refs) // 3
    for src, dst in zip(refs[:n], refs[2 * n:]):
        dst[...] = src[...].astype(BF16)


def _put_sample_rows(sample_rows, full):
    n = len(full)
    return pl.pallas_call(
        _put_sample_rows_body,
        grid=(1,),
        in_specs=[pl.BlockSpec((N_S, 512), lambda i: (0, 0))] * n + [pl.BlockSpec(memory_space=pl.ANY)] * n,
        out_specs=[pl.BlockSpec((N_S, 512), lambda i: (N_P // N_S, 0))] * n,
        out_shape=[jax.ShapeDtypeStruct((N, 512), BF16)] * n,
        input_output_aliases={n + k: k for k in range(n)},
        compiler_params=_cparams(1), name="put_sample_rows")(*sample_rows, *full)


COMBINE_RANGES = tuple((t, 13) for t in range(0, N // SUB, 13))

def _prepare_weights(w_in, w_merge_gate, w_branch, w_out, router_w, shared_w1, shared_w3, shared_w2):
    return dict(
        w_main=jnp.concatenate([w_in[:, :, :1536], w_in[:, :, 1552:]], axis=2).astype(BF16),
        w_low=jnp.pad(w_in[:, :, 1536:1552], ((0, 0), (0, 0), (0, LANES - 16))).astype(BF16),
        w_mg=w_merge_gate.astype(BF16), w_br=w_branch.astype(BF16), w_out=w_out.astype(BF16),
        rw=jnp.pad(router_w, ((0, 0), (0, 0), (0, LANES - N_EXPERTS))),
        w13=jnp.concatenate([shared_w1, shared_w3], axis=2).astype(BF16), sw2=shared_w2.astype(BF16))


def _rope_tables(pos):
    half = DK // 2
    inv = ROPE_BASE ** (-jnp.arange(half, dtype=F32) / half)
    ang = pos.astype(F32)[:, None] * inv[None, :]
    cos = jnp.cos(ang)
    sin = jnp.sin(ang)
    return jnp.concatenate([cos, cos], axis=1), jnp.concatenate([-sin, sin], axis=1)


def _layer(l, x, mod, s_gla, s_pool, s_ret, wts, prep, final, prev_gla, prev_ret):
    (norm_mix_pre, norm_mix_post, norm_ffn_pre, norm_ffn_post, w_in, w_gla_gate, b_gla_gate, gla_norm,
     pool_w, pool_scale, ret_norm, sgu_norm, sgu_w, sgu_b, w_branch, w_merge_gate, b_merge_gate, w_out,
     router_w, router_bias, expert_w1, expert_w3, expert_w2, shared_w1, shared_w3, shared_w2) = wts

    h = _prenorm(x, norm_mix_pre[l], mod, l)
    p_main = _matmul(h, prep["w_main"], l, 1664, 512, name="inproj")
    p_low = _matmul(h, prep["w_low"], l, 1664, LANES, name="inproj_low")

    w_gate_pad = jnp.pad(w_gla_gate[l], ((0, LANES - 16), (0, 0)))
    b_gate = b_gla_gate[l].reshape(1, HEADS * DK)
    log_gamma = jnp.log1p(-jnp.exp2(-5.0 - jnp.arange(HEADS, dtype=F32)))
    dec_row = jnp.repeat(log_gamma, DK).reshape(1, HEADS * DK)
    cos_p, sin_p = _rope_tables(jnp.arange(T_P))
    cos_p = jnp.tile(cos_p, (1, 2))
    sin_p = jnp.tile(sin_p, (1, 2))
    g_gla = gla_norm[l].reshape(1, HEADS * DV)
    g_ret = ret_norm[l].reshape(1, HEADS * DV)

    oa_p, gla_p = _la_prompt(p_main, C_GQ, C_GK, C_GV, C_GR, p_low, p_low, w_gate_pad, b_gate, g_gla, False)
    oc_p, ret_p = _la_prompt(p_main, C_RQ, C_RK, C_RV, C_RG, cos_p, sin_p, dec_row, b_gate, g_ret, True)
    pw_bf = pool_w[l].astype(BF16)
    pscale = pool_scale[l].reshape(1, 512)
    ob_p = _pool_prompt(p_main, pw_bf, pscale)
    sgu_g = sgu_norm[l].reshape(1, 512)
    od_p = _sgu_prompt(p_main, sgu_g, sgu_w[l], jnp.pad(sgu_b[l].T, ((0, 0), (0, LANES - 4))))

    ps = p_main[N_P:]
    q_t = _to_tiles_t(ps[:, C_GQ:C_GQ + 256])
    k_t = _to_tiles_t(ps[:, C_GK:C_GK + 256])
    glow_t = jnp.pad(_to_tiles_t(p_low[N_P:, :16]), ((0, 0), (0, LANES - 16), (0, 0)))
    w_gate_t = jnp.pad(w_gla_gate[l].T, ((0, 0), (0, LANES - 16)))
    b_col = jnp.broadcast_to(b_gla_gate[l][:, None], (HEADS * DK, LANES))
    logit_t = _gate_logits_t(w_gate_t, glow_t, b_col)
    dummy = jnp.zeros((HEADS * DK, LANES), F32)
    oa_s, gla_s = _la_sample(q_t, k_t, logit_t, dummy, dummy, p_main, C_GV, C_GR, g_gla, s_gla, l, prev_gla, False)
    cos_s, sin_s = _rope_tables(jnp.full((1,), PAST_LEN))
    cos_c = jnp.broadcast_to(jnp.tile(cos_s[0], HEADS)[:, None], (HEADS * DK, LANES))
    sin_c = jnp.broadcast_to(jnp.tile(sin_s[0], HEADS)[:, None], (HEADS * DK, LANES))
    dec_c = jnp.broadcast_to(jnp.repeat(log_gamma, DK)[:, None], (HEADS * DK, LANES))
    rq_t = _to_tiles_t(ps[:, C_RQ:C_RQ + 256])
    rk_t = _to_tiles_t(ps[:, C_RK:C_RK + 256])
    oc_s, ret_s = _la_sample(rq_t, rk_t, dec_c, cos_c, sin_c, p_main, C_RV, C_RG, g_ret, s_ret, l, prev_ret, True)
    sgu_w0 = jnp.repeat(sgu_w[l][:, 0, 0], LANES).reshape(1, 512)
    sgu_b0 = jnp.repeat(sgu_b[l][:, 0], LANES).reshape(1, 512)
    ob_s, od_s, vn_s = _small_sample(p_main, s_pool[l], pw_bf, pscale, sgu_g, sgu_w0, sgu_b0)
    pool_p = jnp.stack([p_main[(b + 1) * T_P - POOL_BUF:(b + 1) * T_P, C_PIN:C_PIN + 512] for b in range(B_P)])
    pool_s = jnp.concatenate([s_pool[l][:, 1:], ps[:, None, C_PIN:C_PIN + 512]], axis=1)

    branches = _put_sample_rows([oa_s, ob_s, oc_s, od_s], [oa_p, ob_p, oc_p, od_p])
    merged = _merge(h, branches, prep["w_mg"], b_merge_gate.reshape(DEPTH, 1, 4 * D), prep["w_br"], l)
    x = _outproj(merged, prep["w_out"], x, norm_mix_post[l], mod, l)

    rb = jnp.broadcast_to(router_bias[l][:, None], (N_EXPERTS, SUB))
    shared, logits, h2_packed = _ffn_pre(x, norm_ffn_pre[l], mod, l, prep["rw"], prep["w13"], prep["sw2"])
    eid, pos, wt, counts = _router(logits, rb)
    counts = counts[:, 0].astype(jnp.int32)
    padded = (counts + EXP_BLOCK - 1) // EXP_BLOCK * EXP_BLOCK
    pad_end = jnp.cumsum(padded)
    pad_start = pad_end - padded
    nused = (pad_end[-1] // EXP_BLOCK).astype(jnp.int32).reshape(1)
    blk_row = jnp.arange(N_BLOCKS, dtype=jnp.int32) * EXP_BLOCK
    block_e = jnp.minimum(jnp.sum((blk_row[:, None] >= pad_end[None, :]).astype(jnp.int32), axis=1),
                          N_EXPERTS - 1)
    first = jnp.concatenate([jnp.ones((1,), jnp.int32), (block_e[1:] != block_e[:-1]).astype(jnp.int32)])
    first = jnp.where(blk_row < pad_end[-1], first, 0)
    par = (jnp.cumsum(first) - 1) % 2
    live = jnp.where(padded > 0, jnp.arange(N_EXPERTS), N_EXPERTS)
    after = jnp.concatenate([lax.cummin(live, reverse=True)[1:], jnp.full((1,), N_EXPERTS)])
    of_block = block_e[:, None] == jnp.arange(N_EXPERTS)
    next_e = jnp.sum(jnp.where(of_block, jnp.where(after < N_EXPERTS, after, -1), 0), axis=1).astype(jnp.int32)
    next_blk = jnp.sum(jnp.where(of_block, pad_end // EXP_BLOCK, 0), axis=1).astype(jnp.int32)
    slots = _slots(pad_start.astype(jnp.int32), eid, pos).T.reshape(N_ASSIGN)
    wt = jnp.pad(wt.T, ((0, 0), (0, LANES - TOP_K)))
    table = h2_packed.reshape((N + TM_FFN,) + ROW_TILE)
    ctl = (block_e, first, par.astype(jnp.int32), next_e, next_blk, nused)
    xs_parts = [_sc_dispatch(table, slots, part) for part in range(N_PARTS)]
    ys = None
    for part in range(N_PARTS):
        ys = _experts(ctl, xs_parts[part], expert_w1, expert_w3, expert_w2, l, part, ys)
    ys = ys.reshape((L_SLOTS,) + ROW_TILE)
    outs = [None]
    for tile0, n_tiles in COMBINE_RANGES:
        a0, a1 = tile0 * SUB * TOP_K, (tile0 + n_tiles) * SUB * TOP_K
        outs = _combine(_sc_gather(ys, slots[a0:a1]), wt, shared, x, norm_ffn_post[l], mod, l, tile0, n_tiles,
                        outs[0], final)
    x = tuple(outs) if final else outs[0]
    return x, (gla_p, pool_p, pool_s, ret_p, vn_s), gla_s, ret_s


def kernel(x_prompt, x_sample, c_prompt, c_sample, state_gla, state_pool, state_ret, w_ada, b_ada, norm_mix_pre, norm_mix_post, norm_ffn_pre, norm_ffn_post, w_in, w_gla_gate, b_gla_gate, gla_norm, pool_w, pool_scale, ret_norm, sgu_norm, sgu_w, sgu_b, w_branch, w_merge_gate, b_merge_gate, w_out, router_w, router_bias, expert_w1, expert_w3, expert_w2, shared_w1, shared_w3, shared_w2):
    wts = (norm_mix_pre, norm_mix_post, norm_ffn_pre, norm_ffn_post, w_in, w_gla_gate, b_gla_gate, gla_norm,
           pool_w, pool_scale, ret_norm, sgu_norm, sgu_w, sgu_b, w_branch, w_merge_gate, b_merge_gate, w_out,
           router_w, router_bias, expert_w1, expert_w3, expert_w2, shared_w1, shared_w3, shared_w2)
    c_all = jnp.zeros((MOD_ROWS, D), F32).at[:B_P].set(c_prompt).at[SUB:SUB + N_S].set(c_sample)
    mod = _ada(c_all, w_ada, b_ada)
    x = (x_prompt.reshape(N_P, D), x_sample.reshape(N_S, D))
    prep = _prepare_weights(w_in, w_merge_gate, w_branch, w_out, router_w, shared_w1, shared_w3, shared_w2)
    per_layer = []
    gla_s = ret_s = None
    for l in range(DEPTH):
        x, states, gla_s, ret_s = _layer(l, x, mod, state_gla, state_pool, state_ret, wts, prep, l == DEPTH - 1,
                                         gla_s, ret_s)
        per_layer.append(states)
    gla_p, pool_p, pool_s, ret_p, vn_s = (jnp.stack(z) for z in zip(*per_layer))
    return (x[0].reshape(B_P, T_P, D), x[1].reshape(N_S, 1, D),
            gla_p, gla_s, pool_p, pool_s, ret_p, ret_s, vn_s.reshape(DEPTH, N_S, 1, 512))
```

```python
import functools

import jax
import jax.numpy as jnp
from jax import lax
from jax.experimental import pallas as pl
from jax.experimental.pallas import tpu as pltpu
from jax.experimental.pallas import tpu_sc as plsc

F32 = jnp.float32
BF16 = jnp.bfloat16
HIGHEST = lax.Precision.HIGHEST

D = 2048
B_P, T_P = 4, 2048
N_P = B_P * T_P
N_S = 128
N = N_P + N_S
DEPTH = 2
PAST_LEN = 16384
EPS = 1e-6
HEADS, DK, DV = 4, 64, 128
CHUNK = 64
GATE_TEMP = 16.0
POOL_WINDOWS = (2, 4, 8, 16)
POOL_BUF = 15
ROPE_BASE = 10000.0
N_EXPERTS = 64
TOP_K = 8
D_EXPERT = 512
ROUTED_SCALE = 2.5

LANES = 128
SUB = 128
MOD_ROWS = 256
EXP_BLOCK = 256
N_ASSIGN = N * TOP_K
N_BLOCKS = -(-(N_ASSIGN + N_EXPERTS * (EXP_BLOCK - 1)) // EXP_BLOCK)
L_SLOTS = N_BLOCKS * EXP_BLOCK
VMEM_LIMIT = 56 * 1024 * 1024

C_GQ, C_GK, C_GV, C_GR, C_PIN, C_RQ, C_RK, C_RV, C_RG, C_SU, C_SV = (
    0, 256, 512, 1024, 1536, 2048, 2304, 2560, 3072, 3584, 4096)
P_MAIN = 4608


def _cparams(n_axes=1):
    return pltpu.CompilerParams(dimension_semantics=("arbitrary",) * n_axes,
                                vmem_limit_bytes=VMEM_LIMIT)


def _silu(x):
    return x * jax.nn.sigmoid(x)


def _mod_rows(t, mp_ref, ms_ref):
    b = jnp.minimum(t // (T_P // SUB), B_P - 1)
    return jnp.where(t >= N_P // SUB, ms_ref[...], mp_ref[pl.ds(b, 1), :])


def _mod_specs(layer, part):
    return [pl.BlockSpec((None, 8, D), lambda i, l=layer, p=part: (l, 0, p)),
            pl.BlockSpec((None, SUB, D), lambda i, l=layer, p=part: (l, 1, p))]


def _pack_bf16_pair(lo, hi):
    lo_u = lax.bitcast_convert_type(lo.astype(BF16).astype(F32), jnp.uint32)
    hi_u = lax.bitcast_convert_type(hi.astype(BF16).astype(F32), jnp.uint32)
    return lax.bitcast_convert_type((hi_u & jnp.uint32(0xFFFF0000)) | (lo_u >> 16), jnp.int32)


def _unpack_bf16_pair(w):
    u = lax.bitcast_convert_type(w, jnp.uint32)
    lo = lax.bitcast_convert_type(u << 16, F32)
    hi = lax.bitcast_convert_type(u & jnp.uint32(0xFFFF0000), F32)
    return lo, hi


ROW_TILE = (8, LANES)


def _load_row_tiles(ref):
    return jnp.concatenate([ref[:, c, :] for c in range(ROW_TILE[0])], axis=1)


def _store_row_tiles(ref, val):
    for c in range(ROW_TILE[0]):
        ref[:, c, :] = val[:, c * LANES:(c + 1) * LANES]


def _load_row_tiles_2d(ref, rows):
    return jnp.concatenate([ref[pl.ds(c, rows, stride=ROW_TILE[0]), :] for c in range(ROW_TILE[0])], axis=1)


def _store_row_tiles_2d(ref, val, rows):
    for c in range(ROW_TILE[0]):
        ref[pl.ds(c, rows, stride=ROW_TILE[0]), :] = val[:, c * LANES:(c + 1) * LANES]


def _ada_body(c_ref, w_ref, b_ref, o_ref):
    s = _silu(c_ref[...]).astype(BF16)
    o_ref[...] = jnp.dot(s, w_ref[...].astype(BF16), preferred_element_type=F32) + b_ref[...]


def _ada(c_all, w_ada, b_ada):
    tn = 1024
    return pl.pallas_call(
        _ada_body,
        grid=(DEPTH, 6 * D // tn),
        in_specs=[pl.BlockSpec((MOD_ROWS, D), lambda l, j: (0, 0)),
                  pl.BlockSpec((None, D, tn), lambda l, j: (l, 0, j)),
                  pl.BlockSpec((None, 1, tn), lambda l, j: (l, 0, j))],
        out_specs=pl.BlockSpec((None, MOD_ROWS, tn), lambda l, j: (l, 0, j)),
        out_shape=jax.ShapeDtypeStruct((DEPTH, MOD_ROWS, 6 * D), F32),
        compiler_params=_cparams(2), name="ada")(c_all, w_ada, b_ada.reshape(DEPTH, 1, 6 * D))


ZERO_ROWS = 2 * SUB


def _x_specs(x, tm, n_axes):
    row = (lambda i: (i, 0)) if n_axes == 1 else (lambda i, j: (i, 0))
    if not isinstance(x, tuple):
        return [pl.BlockSpec((tm, D), row)], [x]
    zero = (lambda i: (0, 0)) if n_axes == 1 else (lambda i, j: (0, 0))
    return [pl.BlockSpec((tm, D), row), pl.BlockSpec((N_S, D), zero)], list(x)


def _x_rows(x_refs, t, rows):
    if len(x_refs) == 1:
        return x_refs[0][rows, :]
    return jnp.where(t >= N_P // SUB, x_refs[1][...], x_refs[0][rows, :])


def _prenorm_rows(t, x, g_ref, shp_ref, shs_ref, scp_ref, scs_ref):
    y = x * lax.rsqrt(jnp.mean(x * x, axis=-1, keepdims=True) + EPS) * g_ref[...]
    return y * (1.0 + _mod_rows(t, scp_ref, scs_ref)) + _mod_rows(t, shp_ref, shs_ref)


TM_NORM = 640


def _prenorm_body(*refs):
    g_ref, shp_ref, shs_ref, scp_ref, scs_ref, h_ref = refs[-6:]
    for sidx in range(TM_NORM // SUB):
        rows = pl.ds(sidx * SUB, SUB)
        t = pl.program_id(0) * (TM_NORM // SUB) + sidx
        h_ref[rows, :] = _prenorm_rows(t, _x_rows(refs[:-6], t, rows), g_ref, shp_ref, shs_ref, scp_ref,
                                       scs_ref).astype(BF16)


def _prenorm(x, g, mod, layer):
    x_specs, x_args = _x_specs(x, TM_NORM, 1)
    return pl.pallas_call(
        _prenorm_body,
        grid=(N // TM_NORM,),
        in_specs=x_specs + [pl.BlockSpec((1, D), lambda i: (0, 0))] + _mod_specs(layer, 0) + _mod_specs(layer, 1),
        out_specs=pl.BlockSpec((TM_NORM, D), lambda i: (i, 0)),
        out_shape=jax.ShapeDtypeStruct((N, D), BF16),
        compiler_params=_cparams(1), name="prenorm")(*x_args, g.reshape(1, D), mod, mod, mod, mod)


def _mm_body(x_ref, w_ref, o_ref):
    o_ref[...] = jnp.dot(x_ref[...], w_ref[...], preferred_element_type=F32).astype(o_ref.dtype)


def _matmul(x, w_all, layer, tm, tn, out_dtype=F32, name="mm"):
    m, k = x.shape
    n = w_all.shape[2]
    return pl.pallas_call(
        _mm_body,
        grid=(m // tm, n // tn),
        in_specs=[pl.BlockSpec((tm, k), lambda i, j: (i, 0)),
                  pl.BlockSpec((None, k, tn), lambda i, j, l=layer: (l, 0, j))],
        out_specs=pl.BlockSpec((tm, tn), lambda i, j: (i, j)),
        out_shape=jax.ShapeDtypeStruct((m, n), out_dtype),
        compiler_params=_cparams(2), name=name)(x, w_all)


ROWS_LA = 256


def _swap_halves_lanes(x):
    lane = lax.broadcasted_iota(jnp.int32, x.shape, 1)
    return jnp.where((lane % 64) < 32, pltpu.roll(x, 96, 1), pltpu.roll(x, 32, 1))


def _rope_lanes(x, cos, sin_signed):
    parts = []
    for half in range(2):
        xh = x[:, half * LANES:(half + 1) * LANES]
        parts.append(xh * cos + _swap_halves_lanes(xh) * sin_signed)
    return jnp.concatenate(parts, axis=1)


def _la_prompt_body(q_ref, k_ref, v_ref, r_ref, aux_ref, aux2_ref, dec_ref, bias_ref, g_ref,
                    o_ref, st_out_ref, st_ref, *, retention):
    t = pl.program_id(1)

    @pl.when(t == 0)
    def _():
        st_ref[...] = jnp.zeros_like(st_ref)

    ri = lax.broadcasted_iota(jnp.int32, (CHUNK, CHUNK), 0)
    ci = lax.broadcasted_iota(jnp.int32, (CHUNK, CHUNK), 1)
    causal = ri >= ci
    tril = causal.astype(F32)
    scale = DK ** -0.5

    for c in range(ROWS_LA // CHUNK):
        rows = pl.ds(c * CHUNK, CHUNK)
        q = q_ref[rows, :]
        k = k_ref[rows, :]
        v = v_ref[rows, :]
        if retention:
            cos = aux_ref[rows, :]
            sin = aux2_ref[rows, :]
            q = _rope_lanes(q, cos, sin)
            k = _rope_lanes(k, cos, sin) * scale
            la = jnp.broadcast_to(dec_ref[...], (CHUNK, HEADS * DK))
        else:
            q = q * scale
            logit = jnp.dot(aux_ref[rows, :], dec_ref[...], precision=HIGHEST,
                            preferred_element_type=F32) + bias_ref[...]
            la = jax.nn.log_sigmoid(logit) / GATE_TEMP
        bc = jnp.dot(tril, la, precision=HIGHEST, preferred_element_type=F32)
        bl = bc[CHUNK - 1:CHUNK, :]
        qd = q * jnp.exp(bc)
        ki = k * jnp.exp(-bc)
        ke = k * jnp.exp(bl - bc)
        ac = jnp.exp(bl)
        outs = []
        for h in range(HEADS):
            ks = slice(h * DK, (h + 1) * DK)
            vs = slice(h * DV, (h + 1) * DV)
            qd_h = qd[:, ks].astype(BF16)
            ki_h = ki[:, ks].astype(BF16)
            ke_h = ke[:, ks].astype(BF16)
            v_h = v[:, vs].astype(BF16)
            sc = lax.dot_general(qd_h, ki_h, (((1,), (1,)), ((), ())), preferred_element_type=F32)
            sc = jnp.where(causal, sc, 0.0)
            o_h = jnp.dot(sc.astype(BF16), v_h, preferred_element_type=F32)
            st = st_ref[h]
            o_h = o_h + lax.dot_general(qd_h, st.astype(BF16), (((1,), (1,)), ((), ())),
                                        preferred_element_type=F32)
            kv_t = lax.dot_general(v_h, ke_h, (((0,), (0,)), ((), ())), preferred_element_type=F32)
            st_ref[h] = st * ac[:, ks] + kv_t
            o_n = o_h * lax.rsqrt(jnp.mean(o_h * o_h, axis=-1, keepdims=True) + EPS) * g_ref[:, vs]
            outs.append(o_n)
        o = jnp.concatenate(outs, axis=1) * _silu(r_ref[rows, :])
        o_ref[rows, :] = o.astype(BF16)

    st_out_ref[...] = st_ref[...]


def _la_prompt(p_main, cq, ck, cv, cr, aux, aux2, dec, bias, g, retention):
    nt = T_P // ROWS_LA
    rowblk = lambda b, t: b * nt + t
    if retention:
        aux_specs = [pl.BlockSpec((ROWS_LA, LANES), lambda b, t: (t, 0)),
                     pl.BlockSpec((ROWS_LA, LANES), lambda b, t: (t, 0))]
    else:
        aux_specs = [pl.BlockSpec((ROWS_LA, LANES), lambda b, t: (rowblk(b, t), 0)),
                     pl.BlockSpec((8, LANES), lambda b, t: (0, 0))]
    o, st = pl.pallas_call(
        functools.partial(_la_prompt_body, retention=retention),
        grid=(B_P, nt),
        in_specs=[pl.BlockSpec((ROWS_LA, 256), lambda b, t: (rowblk(b, t), cq // 256)),
                  pl.BlockSpec((ROWS_LA, 256), lambda b, t: (rowblk(b, t), ck // 256)),
                  pl.BlockSpec((ROWS_LA, 512), lambda b, t: (rowblk(b, t), cv // 512)),
                  pl.BlockSpec((ROWS_LA, 512), lambda b, t: (rowblk(b, t), cr // 512))]
        + aux_specs
        + [pl.BlockSpec(dec.shape, lambda b, t: (0, 0)),
           pl.BlockSpec((1, HEADS * DK), lambda b, t: (0, 0)),
           pl.BlockSpec((1, HEADS * DV), lambda b, t: (0, 0))],
        out_specs=[pl.BlockSpec((ROWS_LA, HEADS * DV), lambda b, t: (rowblk(b, t), 0)),
                   pl.BlockSpec((None, HEADS, DV, DK), lambda b, t: (b, 0, 0, 0))],
        out_shape=[jax.ShapeDtypeStruct((N, HEADS * DV), BF16),
                   jax.ShapeDtypeStruct((B_P, HEADS, DV, DK), F32)],
        scratch_shapes=[pltpu.VMEM((HEADS, DV, DK), F32)],
        compiler_params=_cparams(2), name="ret_prompt" if retention else "gla_prompt",
    )(p_main, p_main, p_main, p_main, aux, aux2, dec, bias, g)
    return o, jnp.swapaxes(st, -1, -2)


SAMPLE_TILE = 8


def _la_sample_body(qt_ref, kt_ref, lt_ref, cos_ref, sin_ref, v_ref, r_ref, g_ref, s_ref, *rest, retention):
    o_ref, s_out_ref = rest[-2:]
    scale = DK ** -0.5
    qt = qt_ref[...]
    kt = kt_ref[...]
    if retention:
        def rope(x):
            sw = jnp.concatenate(
                [x[h * DK + (DK // 2) * (1 - j): h * DK + (DK // 2) * (2 - j), :]
                 for h in range(HEADS) for j in range(2)], axis=0)
            return x * cos_ref[...] + sw * sin_ref[...]
        qt = rope(qt)
        kt = rope(kt) * scale
        la = lt_ref[...]
    else:
        qt = qt * scale
        la = jax.nn.log_sigmoid(lt_ref[...]) / GATE_TEMP
    at = jnp.exp(la)
    qd = qt * at
    ki = kt * jnp.exp(-la)
    prod = qd * ki
    v8 = v_ref[...]
    r8 = r_ref[...]
    g = g_ref[...]
    for j in range(SAMPLE_TILE):
        for h in range(HEADS):
            ks = slice(h * DK, (h + 1) * DK)
            vs = slice(h * DV, (h + 1) * DV)
            a_c = jnp.broadcast_to(at[ks, j:j + 1], (DK, DV))
            k_c = jnp.broadcast_to(kt[ks, j:j + 1], (DK, DV))
            q_c = jnp.broadcast_to(qd[ks, j:j + 1], (DK, DV))
            s_c = jnp.broadcast_to(jnp.sum(prod[ks, j:j + 1], axis=0, keepdims=True), (1, DV))
            s0 = s_ref[j, h]
            v_row = v8[j:j + 1, vs]
            s_out_ref[j, h] = a_c * s0 + k_c * v_row
            o_row = s_c * v_row + jnp.sum(q_c * s0, axis=0, keepdims=True)
            o_n = o_row * lax.rsqrt(jnp.mean(o_row * o_row, axis=-1, keepdims=True) + EPS) * g[:, vs]
            o_ref[j:j + 1, vs] = o_n * _silu(r8[j:j + 1, vs])


def _la_sample(qt, kt, lt, cos_t, sin_t, p_main, cv, cr, g, s0_all, layer, s_prev, retention):
    nt = N_S // SAMPLE_TILE
    row0 = N_P // SAMPLE_TILE
    tile = pl.BlockSpec((None, HEADS * DK, LANES), lambda i: (i, 0, 0))
    full = pl.BlockSpec((HEADS * DK, LANES), lambda i: (0, 0))
    lt_spec = full if retention else tile
    return pl.pallas_call(
        functools.partial(_la_sample_body, retention=retention),
        grid=(nt,),
        in_specs=[tile, tile, lt_spec, full, full,
                  pl.BlockSpec((SAMPLE_TILE, 512), lambda i: (row0 + i, cv // 512)),
                  pl.BlockSpec((SAMPLE_TILE, 512), lambda i: (row0 + i, cr // 512)),
                  pl.BlockSpec((1, HEADS * DV), lambda i: (0, 0)),
                  pl.BlockSpec((None, SAMPLE_TILE, HEADS, DK, DV), lambda i, l=layer: (l, i, 0, 0, 0))]
        + ([] if s_prev is None else [pl.BlockSpec(memory_space=pl.ANY)]),
        out_specs=[pl.BlockSpec((SAMPLE_TILE, HEADS * DV), lambda i: (i, 0)),
                   pl.BlockSpec((None, SAMPLE_TILE, HEADS, DK, DV), lambda i, l=layer: (l, i, 0, 0, 0))],
        out_shape=[jax.ShapeDtypeStruct((N_S, HEADS * DV), F32),
                   jax.ShapeDtypeStruct((DEPTH, N_S, HEADS, DK, DV), F32)],
        input_output_aliases={} if s_prev is None else {9: 1},
        compiler_params=_cparams(1), name="ret_sample" if retention else "gla_sample",
    )(qt, kt, lt, cos_t, sin_t, p_main, p_main, g, s0_all, *([] if s_prev is None else [s_prev]))


def _gate_logits_t_body(w_ref, x_ref, b_ref, o_ref):
    o_ref[...] = jnp.dot(w_ref[...], x_ref[...], precision=HIGHEST, preferred_element_type=F32) + b_ref[...]


def _gate_logits_t(w_gate_t, glow_t, b_col):
    nt = N_S // SAMPLE_TILE
    return pl.pallas_call(
        _gate_logits_t_body,
        grid=(nt,),
        in_specs=[pl.BlockSpec((HEADS * DK, LANES), lambda i: (0, 0)),
                  pl.BlockSpec((None, LANES, LANES), lambda i: (i, 0, 0)),
                  pl.BlockSpec((HEADS * DK, LANES), lambda i: (0, 0))],
        out_specs=pl.BlockSpec((None, HEADS * DK, LANES), lambda i: (i, 0, 0)),
        out_shape=jax.ShapeDtypeStruct((nt, HEADS * DK, LANES), F32),
        compiler_params=_cparams(1), name="gate_logits_t")(w_gate_t, glow_t, b_col)


def _to_tiles_t(x):
    c = x.shape[1]
    xt = jnp.swapaxes(x.reshape(N_S // SAMPLE_TILE, SAMPLE_TILE, c), 1, 2)
    return jnp.pad(xt, ((0, 0), (0, 0), (0, LANES - SAMPLE_TILE)))


ROWS_POOL = 512


def _pool_mix(y, w_ref, sc_ref):
    outs = []
    for gi in range(4):
        cs = slice(gi * LANES, (gi + 1) * LANES)
        outs.append(jnp.dot(y[:, cs].astype(BF16), w_ref[gi], preferred_element_type=F32))
    return jnp.concatenate(outs, axis=1) * sc_ref[...]


def _pool_prompt_body(p_ref, halo_ref, w_ref, sc_ref, o_ref):
    t = pl.program_id(1)
    p = p_ref[...]
    halo = jnp.where(t == 0, 0.0, halo_ref[...])
    full = jnp.concatenate([halo, p], axis=0)
    pos = t * ROWS_POOL + lax.broadcasted_iota(jnp.int32, (ROWS_POOL, LANES), 0)
    means = []
    for gi, w in enumerate(POOL_WINDOWS):
        s = full[:, gi * LANES:(gi + 1) * LANES]
        step = 1
        while step < w:
            s = s + pltpu.roll(s, step, 0)
            step *= 2
        win = s[16:, :]
        cnt = jnp.minimum(w, pos + 1).astype(F32)
        means.append(win / cnt)
    y = jnp.concatenate(means, axis=1) - p
    o_ref[...] = _pool_mix(y, w_ref, sc_ref).astype(BF16)


def _pool_prompt(p_main, w_bf, scale):
    nt = T_P // ROWS_POOL
    return pl.pallas_call(
        _pool_prompt_body,
        grid=(B_P, nt),
        in_specs=[pl.BlockSpec((ROWS_POOL, 512), lambda b, t: (b * nt + t, C_PIN // 512)),
                  pl.BlockSpec((16, 512), lambda b, t: (jnp.maximum((b * nt + t) * (ROWS_POOL // 16) - 1, 0),
                                                        C_PIN // 512)),
                  pl.BlockSpec((4, LANES, LANES), lambda b, t: (0, 0, 0)),
                  pl.BlockSpec((1, 512), lambda b, t: (0, 0))],
        out_specs=pl.BlockSpec((ROWS_POOL, 512), lambda b, t: (b * nt + t, 0)),
        out_shape=jax.ShapeDtypeStruct((N, 512), BF16),
        compiler_params=_cparams(2), name="pool_prompt")(p_main, p_main, w_bf, scale)


def _small_sample_body(p_ref, buf_ref, pw_ref, psc_ref, u_ref, sv_ref, sg_ref, sw_ref, sb_ref,
                       ob_ref, od_ref, vn_ref):
    p = p_ref[...]
    means = []
    for gi, w in enumerate(POOL_WINDOWS):
        cs = slice(gi * LANES, (gi + 1) * LANES)
        s = p[:, cs]
        for j in range(1, w):
            s = s + buf_ref[:, POOL_BUF - j, cs]
        means.append(s / float(min(w, PAST_LEN + 1)))
    y = jnp.concatenate(means, axis=1) - p
    ob_ref[...] = _pool_mix(y, pw_ref, psc_ref)
    sv = sv_ref[...]
    vn = sv * lax.rsqrt(jnp.mean(sv * sv, axis=-1, keepdims=True) + EPS) * sg_ref[...]
    vn_ref[...] = vn
    od_ref[...] = u_ref[...] * (sw_ref[...] * vn + sb_ref[...])


def _small_sample(p_main, buf, pw_bf, pscale, sgu_g, sgu_w0, sgu_b0):
    row = N_P // N_S
    col = lambda c: pl.BlockSpec((N_S, 512), lambda i, c=c: (row, c // 512))
    vec = pl.BlockSpec((1, 512), lambda i: (0, 0))
    return pl.pallas_call(
        _small_sample_body,
        grid=(1,),
        in_specs=[col(C_PIN), pl.BlockSpec((N_S, POOL_BUF, 512), lambda i: (0, 0, 0)),
                  pl.BlockSpec((4, LANES, LANES), lambda i: (0, 0, 0)), vec,
                  col(C_SU), col(C_SV), vec, vec, vec],
        out_specs=[pl.BlockSpec((N_S, 512), lambda i: (0, 0))] * 3,
        out_shape=[jax.ShapeDtypeStruct((N_S, 512), F32)] * 3,
        compiler_params=_cparams(1), name="small_sample",
    )(p_main, buf, pw_bf, pscale, p_main, p_main, sgu_g, sgu_w0, sgu_b0)


ROWS_SGU = 512
SGU_CHUNK = 128


def _sgu_prompt_body(u_ref, v_ref, g_ref, w_ref, bt_ref, o_ref):
    ri = lax.broadcasted_iota(jnp.int32, (SGU_CHUNK, SGU_CHUNK), 0)
    ci = lax.broadcasted_iota(jnp.int32, (SGU_CHUNK, SGU_CHUNK), 1)
    causal = ri >= ci
    for c in range(ROWS_SGU // SGU_CHUNK):
        rows = pl.ds(c * SGU_CHUNK, SGU_CHUNK)
        v = v_ref[rows, :]
        vn = (v * lax.rsqrt(jnp.mean(v * v, axis=-1, keepdims=True) + EPS) * g_ref[...]).astype(BF16)
        outs = []
        for gi in range(4):
            cs = slice(gi * LANES, (gi + 1) * LANES)
            w = jnp.where(causal, w_ref[gi], 0.0).astype(BF16)
            mixed = jnp.dot(w, vn[:, cs], preferred_element_type=F32)
            outs.append(mixed + jnp.broadcast_to(bt_ref[:, gi:gi + 1], (SGU_CHUNK, LANES)))
        o_ref[rows, :] = (u_ref[rows, :] * jnp.concatenate(outs, axis=1)).astype(BF16)


def _sgu_prompt(p_main, g, w, b_t):
    return pl.pallas_call(
        _sgu_prompt_body,
        grid=(N_P // ROWS_SGU,),
        in_specs=[pl.BlockSpec((ROWS_SGU, 512), lambda i: (i, C_SU // 512)),
                  pl.BlockSpec((ROWS_SGU, 512), lambda i: (i, C_SV // 512)),
                  pl.BlockSpec((1, 512), lambda i: (0, 0)),
                  pl.BlockSpec((4, SGU_CHUNK, SGU_CHUNK), lambda i: (0, 0, 0)),
                  pl.BlockSpec((SGU_CHUNK, LANES), lambda i: (0, 0))],
        out_specs=pl.BlockSpec((ROWS_SGU, 512), lambda i: (i, 0)),
        out_shape=jax.ShapeDtypeStruct((N, 512), BF16),
        compiler_params=_cparams(1), name="sgu_prompt")(p_main, p_main, g, w, b_t)


TM_MERGE = 640
TN_MERGE = 512


def _merge_body(h_ref, ba_ref, bb_ref, bc_ref, bd_ref, g0, g1, g2, g3, u0, u1, u2, u3,
                c0, c1, c2, c3, o_ref):
    h = h_ref[...]
    acc = None
    for br, gw, uw, gb in ((ba_ref, g0, u0, c0), (bb_ref, g1, u1, c1), (bc_ref, g2, u2, c2), (bd_ref, g3, u3, c3)):
        gate = jax.nn.sigmoid(jnp.dot(h, gw[...], preferred_element_type=F32) + gb[...])
        up = jnp.dot(br[...], uw[...], preferred_element_type=F32)
        acc = gate * up if acc is None else acc + gate * up
    o_ref[...] = acc.astype(BF16)


def _merge(h, branches, w_mg, b_mg, w_br, layer):
    nj = D // TN_MERGE
    row = lambda w: pl.BlockSpec((TM_MERGE, w), lambda i, j: (i, 0))
    gate_w = [pl.BlockSpec((None, D, TN_MERGE), lambda i, j, b=b, l=layer: (l, 0, b * nj + j)) for b in range(4)]
    up_w = [pl.BlockSpec((None, None, 512, TN_MERGE), lambda i, j, b=b, l=layer: (l, b, 0, j)) for b in range(4)]
    gate_b = [pl.BlockSpec((None, 1, TN_MERGE), lambda i, j, b=b, l=layer: (l, 0, b * nj + j)) for b in range(4)]
    return pl.pallas_call(
        _merge_body,
        grid=(N // TM_MERGE, nj),
        in_specs=[row(D)] + [row(512)] * 4 + gate_w + up_w + gate_b,
        out_specs=pl.BlockSpec((TM_MERGE, TN_MERGE), lambda i, j: (i, j)),
        out_shape=jax.ShapeDtypeStruct((N, D), BF16),
        compiler_params=_cparams(2), name="merge",
    )(h, *branches, w_mg, w_mg, w_mg, w_mg, w_br, w_br, w_br, w_br, b_mg, b_mg, b_mg, b_mg)


TM_OUT = 640
TN_OUT = 512


def _post_value(t, x, y, gn_ref, gp_ref, gs_ref):
    yn = y * lax.rsqrt(jnp.mean(y * y, axis=-1, keepdims=True) + EPS) * gn_ref[...]
    return x + _mod_rows(t, gp_ref, gs_ref) * yn


def _outproj_body(m_ref, w_ref, *refs):
    gn_ref, gp_ref, gs_ref, o_ref, acc_ref = refs[-5:]
    j = pl.program_id(1)
    acc_ref[j] = jnp.dot(m_ref[...], w_ref[...], preferred_element_type=F32)

    @pl.when(j == D // TN_OUT - 1)
    def _():
        for sidx in range(TM_OUT // SUB):
            rows = pl.ds(sidx * SUB, SUB)
            t = pl.program_id(0) * (TM_OUT // SUB) + sidx
            y = jnp.concatenate([acc_ref[c, rows, :] for c in range(D // TN_OUT)], axis=1)
            o_ref[rows, :] = _post_value(t, _x_rows(refs[:-5], t, rows), y, gn_ref, gp_ref, gs_ref)


def _outproj(merged, w_out, x, g_post, mod, layer):
    mspec = [pl.BlockSpec((None, 8, D), lambda i, j, l=layer: (l, 0, 2)),
             pl.BlockSpec((None, SUB, D), lambda i, j, l=layer: (l, 1, 2))]
    x_specs, x_args = _x_specs(x, TM_OUT, 2)
    return pl.pallas_call(
        _outproj_body,
        grid=(N // TM_OUT, D // TN_OUT),
        in_specs=[pl.BlockSpec((TM_OUT, D), lambda i, j: (i, 0)),
                  pl.BlockSpec((None, D, TN_OUT), lambda i, j, l=layer: (l, 0, j))] + x_specs
        + [pl.BlockSpec((1, D), lambda i, j: (0, 0))] + mspec,
        out_specs=pl.BlockSpec((TM_OUT, D), lambda i, j: (i, 0)),
        out_shape=jax.ShapeDtypeStruct((N, D), F32),
        scratch_shapes=[pltpu.VMEM((D // TN_OUT, TM_OUT, TN_OUT), F32)],
        compiler_params=_cparams(2), name="outproj",
    )(merged, w_out, *x_args, g_post.reshape(1, D), mod, mod)


def _router_body(lg_ref, b_ref, eid_ref, pos_ref, wt_ref, cnt_ref, run_ref):
    i = pl.program_id(0)

    @pl.when(i == 0)
    def _():
        run_ref[...] = jnp.zeros_like(run_ref)

    ng, gs = 8, N_EXPERTS // 8
    neg = -jnp.inf
    scores = jax.nn.sigmoid(lg_ref[...].T[:N_EXPERTS, :])
    sel = scores + b_ref[...]
    sel3 = sel.reshape(ng, gs, SUB)
    sub3 = lax.broadcasted_iota(jnp.int32, (ng, gs, SUB), 1)
    gmax = jnp.max(sel3, axis=1, keepdims=True)
    first = jnp.min(jnp.where(sel3 == gmax, sub3, gs), axis=1, keepdims=True)
    gmax2 = jnp.max(jnp.where(sub3 == first, neg, sel3), axis=1, keepdims=True)
    gscore = (gmax + gmax2).reshape(ng, SUB)
    gidx = lax.broadcasted_iota(jnp.int32, (ng, SUB), 0)
    grank = jnp.zeros((ng, SUB), jnp.int32)
    for s in range(1, ng):
        other = pltpu.roll(gscore, s, 0)
        lower = gidx >= s
        grank += ((other > gscore) | ((other == gscore) & lower)).astype(jnp.int32)
    keep = jnp.broadcast_to((grank < 4).reshape(ng, 1, SUB), (ng, gs, SUB))
    masked = jnp.where(keep, sel3, neg).reshape(N_EXPERTS, SUB)
    eidx = lax.broadcasted_iota(jnp.int32, (N_EXPERTS, SUB), 0)
    rank = jnp.zeros((N_EXPERTS, SUB), jnp.int32)
    for s in range(1, N_EXPERTS):
        other = pltpu.roll(masked, s, 0)
        lower = eidx >= s
        rank += ((other > masked) | ((other == masked) & lower)).astype(jnp.int32)
    chosen = rank < TOP_K
    w_sel = jnp.where(chosen, scores, 0.0)
    w_sel = w_sel / jnp.sum(w_sel, axis=0, keepdims=True) * ROUTED_SCALE
    ri = lax.broadcasted_iota(jnp.int32, (SUB, SUB), 0)
    ci = lax.broadcasted_iota(jnp.int32, (SUB, SUB), 1)
    onehot = chosen.astype(BF16)
    pos = jnp.dot(onehot, (ri < ci).astype(BF16), preferred_element_type=F32) + run_ref[...]
    run_ref[...] = run_ref[...] + jnp.sum(chosen.astype(F32), axis=1, keepdims=True)
    cnt_ref[...] = run_ref[...]
    eidx_f = eidx.astype(F32)
    rows_e, rows_p, rows_w = [], [], []
    for kk in range(TOP_K):
        m = chosen & (rank == kk)
        rows_e.append(jnp.sum(jnp.where(m, eidx_f, 0.0), axis=0, keepdims=True))
        rows_p.append(jnp.sum(jnp.where(m, pos, 0.0), axis=0, keepdims=True))
        rows_w.append(jnp.sum(jnp.where(m, w_sel, 0.0), axis=0, keepdims=True))
    eid_ref[...] = jnp.concatenate(rows_e, axis=0).astype(jnp.int32)
    pos_ref[...] = jnp.concatenate(rows_p, axis=0).astype(jnp.int32)
    wt_ref[...] = jnp.concatenate(rows_w, axis=0)


def _router(logits, rb_col):
    tile = pl.BlockSpec((TOP_K, SUB), lambda i: (0, i))
    return pl.pallas_call(
        _router_body,
        grid=(N // SUB,),
        in_specs=[pl.BlockSpec((SUB, LANES), lambda i: (i, 0)),
                  pl.BlockSpec((N_EXPERTS, SUB), lambda i: (0, 0))],
        out_specs=[tile, tile, tile, pl.BlockSpec((N_EXPERTS, SUB), lambda i: (0, 0))],
        out_shape=[jax.ShapeDtypeStruct((TOP_K, N), jnp.int32), jax.ShapeDtypeStruct((TOP_K, N), jnp.int32),
                   jax.ShapeDtypeStruct((TOP_K, N), F32), jax.ShapeDtypeStruct((N_EXPERTS, SUB), F32)],
        scratch_shapes=[pltpu.VMEM((N_EXPERTS, SUB), F32)],
        compiler_params=_cparams(1), name="router")(logits, rb_col)


SC_CORES, SC_SUBCORES = 2, 16
SC_WORKERS = SC_CORES * SC_SUBCORES
SC_LANES = 16
SC_CHUNK = 16
SC_SCAN = N_ASSIGN // SC_WORKERS


def _sc_mesh():
    return plsc.VectorSubcoreMesh(core_axis_name="c", subcore_axis_name="s",
                                  num_cores=SC_CORES, num_subcores=SC_SUBCORES)


def _sc_worker_base(per_w):
    return (lax.axis_index("s") * SC_CORES + lax.axis_index("c")) * per_w


def _sc_gather_rows(table_hbm, out_hbm, idx_v, rows_v, gsem, wsem, base, per_w):
    n_ch = per_w // SC_CHUNK
    assert n_ch % 2 == 0

    def gather(j, p):
        off = pl.multiple_of(j * SC_CHUNK, SC_CHUNK)
        return pltpu.make_async_copy(table_hbm.at[idx_v.at[pl.ds(off, SC_CHUNK)]], rows_v.at[p], gsem.at[p])

    def write(j, p):
        off = pl.multiple_of(j * SC_CHUNK, SC_CHUNK)
        return pltpu.make_async_copy(rows_v.at[p], out_hbm.at[pl.ds(base + off, SC_CHUNK)], wsem.at[p])

    gather(0, 0).start()

    @pl.loop(0, n_ch, step=2)
    def _(j0):
        for p in range(2):
            j = j0 + p
            gather(j, p).wait()

            @pl.when(j >= 1)
            def _():
                write(j - 1, 1 - p).wait()

            @pl.when(j + 1 < n_ch)
            def _():
                gather(j + 1, 1 - p).start()
            write(j, p).start()

    write(n_ch - 1, 1).wait()


_SC_ROW_SCRATCH = [pltpu.VMEM((2, SC_CHUNK) + ROW_TILE, jnp.int32),
                   pltpu.SemaphoreType.DMA((2,)), pltpu.SemaphoreType.DMA((2,))]


def _sc_gather(table, idx):
    n_out = idx.shape[0]
    per_w = n_out // SC_WORKERS
    assert per_w * SC_WORKERS == n_out and per_w % (2 * SC_CHUNK) == 0

    def body(table_hbm, idx_hbm, out_hbm, idx_v, rows_v, gsem, wsem):
        base = _sc_worker_base(per_w)
        pltpu.sync_copy(idx_hbm.at[pl.ds(base, per_w)], idx_v)
        _sc_gather_rows(table_hbm, out_hbm, idx_v, rows_v, gsem, wsem, base, per_w)

    return pl.kernel(
        body, out_type=jax.ShapeDtypeStruct((n_out,) + ROW_TILE, jnp.int32), mesh=_sc_mesh(),
        scratch_types=[pltpu.VMEM((per_w,), jnp.int32)] + _SC_ROW_SCRATCH, name="sc_gather")(table, idx)


N_PARTS = 3
PART_BLOCKS = N_BLOCKS // N_PARTS
PART_SLOTS = PART_BLOCKS * EXP_BLOCK
assert PART_BLOCKS * N_PARTS == N_BLOCKS


def _sc_dispatch(table, slots, part):
    per_w = PART_SLOTS // SC_WORKERS
    assert per_w * SC_WORKERS == PART_SLOTS and per_w % (2 * SC_CHUNK) == 0 and SC_SCAN % SC_LANES == 0
    assert ZERO_ROWS & (ZERO_ROWS - 1) == 0

    def body(table_hbm, slots_hbm, out_hbm, idx_v, sl_v, rows_v, gsem, wsem):
        local = _sc_worker_base(per_w)
        base = part * PART_SLOTS + local
        lane = lax.iota(jnp.int32, SC_LANES)

        @pl.loop(0, per_w // SC_LANES)
        def _(j):
            off = pl.multiple_of(j * SC_LANES, SC_LANES)
            idx_v[pl.ds(off, SC_LANES)] = N + ((base + off + lane) & (ZERO_ROWS - 1))

        @pl.loop(0, N_ASSIGN // SC_SCAN)
        def _(c):
            pltpu.sync_copy(slots_hbm.at[pl.ds(pl.multiple_of(c * SC_SCAN, 8), SC_SCAN)], sl_v)

            @pl.loop(0, SC_SCAN // SC_LANES)
            def _(j):
                off = pl.multiple_of(j * SC_LANES, SC_LANES)
                loc = sl_v[pl.ds(off, SC_LANES)] - base
                mine = (loc >= 0) & (loc < per_w)
                tok = lax.shift_right_logical(c * SC_SCAN + off + lane, 3)
                plsc.store_scatter(idx_v, [jnp.where(mine, loc, 0)], tok, mask=mine)

        _sc_gather_rows(table_hbm, out_hbm, idx_v, rows_v, gsem, wsem, local, per_w)

    return pl.kernel(
        body, out_type=jax.ShapeDtypeStruct((PART_SLOTS,) + ROW_TILE, jnp.int32), mesh=_sc_mesh(),
        scratch_types=[pltpu.VMEM((per_w,), jnp.int32), pltpu.VMEM((SC_SCAN,), jnp.int32)] + _SC_ROW_SCRATCH,
        compiler_params=pltpu.CompilerParams(needs_layout_passes=False),
        name="sc_dispatch")(table, slots)


def _experts_body(be_ref, first_ref, par_ref, next_ref, nextblk_ref, nused_ref, x_ref, w1_hbm, w3_hbm, w2_hbm,
                  *rest, layer, part):
    y_ref, w1f, w3f, w2f, w1b, w3b, w2b, sem = rest[-8:]
    i = pl.program_id(0)
    b = part * PART_BLOCKS + i
    used = b < nused_ref[0]

    def copies(e, slot):
        return (pltpu.make_async_copy(w1_hbm.at[layer, e], w1f.at[slot], sem.at[0, slot]),
                pltpu.make_async_copy(w3_hbm.at[layer, e], w3f.at[slot], sem.at[1, slot]),
                pltpu.make_async_copy(w2_hbm.at[layer, e], w2f.at[slot], sem.at[2, slot]))

    @pl.when(used & (i == 0))
    def _():
        for c in copies(be_ref[b], par_ref[b]):
            c.start()

    @pl.when(used & ((i == 0) | (first_ref[b] == 1)))
    def _():
        slot = par_ref[b]
        for c in copies(be_ref[b], slot):
            c.wait()

        @pl.when((next_ref[b] >= 0) & (nextblk_ref[b] < (part + 1) * PART_BLOCKS))
        def _():
            for c in copies(next_ref[b], 1 - slot):
                c.start(priority=1)
        w1b[...] = w1f[slot].astype(BF16)
        w3b[...] = w3f[slot].astype(BF16)
        w2b[...] = w2f[slot].astype(BF16)

    @pl.when(used)
    def _():
        lo, hi = _unpack_bf16_pair(_load_row_tiles_2d(x_ref, EXP_BLOCK))
        lo = lo.astype(BF16)
        hi = hi.astype(BF16)
        half = D // 2
        h1 = (jnp.dot(lo, w1b[:half, :], preferred_element_type=F32)
              + jnp.dot(hi, w1b[half:, :], preferred_element_type=F32))
        h3 = (jnp.dot(lo, w3b[:half, :], preferred_element_type=F32)
              + jnp.dot(hi, w3b[half:, :], preferred_element_type=F32))
        hid = (_silu(h1) * h3).astype(BF16)
        y = jnp.dot(hid, w2b[...], preferred_element_type=F32)
        _store_row_tiles_2d(y_ref, _pack_bf16_pair(y[:, :half], y[:, half:]), EXP_BLOCK)

    @pl.when(jnp.logical_not(used))
    def _():
        y_ref[...] = jnp.zeros_like(y_ref)


def _experts(ctl, xs_part, w1, w3, w2, layer, part, ys_prev):
    def x_blk(i, *refs):
        n_here = jnp.clip(refs[-1][0] - part * PART_BLOCKS, 1, PART_BLOCKS)
        return (jnp.minimum(i, n_here - 1), 0)
    any_spec = pl.BlockSpec(memory_space=pl.ANY)
    in_specs = [pl.BlockSpec((EXP_BLOCK * ROW_TILE[0], LANES), x_blk), any_spec, any_spec, any_spec]
    args = [xs_part.reshape(PART_SLOTS * ROW_TILE[0], LANES), w1, w3, w2]
    aliases = {}
    if ys_prev is not None:
        in_specs.append(any_spec)
        args.append(ys_prev)
        aliases = {len(ctl) + 4: 0}
    grid_spec = pltpu.PrefetchScalarGridSpec(
        num_scalar_prefetch=len(ctl),
        grid=(PART_BLOCKS,),
        in_specs=in_specs,
        out_specs=pl.BlockSpec((EXP_BLOCK * ROW_TILE[0], LANES), lambda i, *refs: (part * PART_BLOCKS + i, 0)),
        scratch_shapes=[pltpu.VMEM((2, D, D_EXPERT), F32), pltpu.VMEM((2, D, D_EXPERT), F32),
                        pltpu.VMEM((2, D_EXPERT, D), F32),
                        pltpu.VMEM((D, D_EXPERT), BF16), pltpu.VMEM((D, D_EXPERT), BF16),
                        pltpu.VMEM((D_EXPERT, D), BF16), pltpu.SemaphoreType.DMA((3, 2))])
    return pl.pallas_call(
        functools.partial(_experts_body, layer=layer, part=part), grid_spec=grid_spec,
        out_shape=jax.ShapeDtypeStruct((L_SLOTS * ROW_TILE[0], LANES), jnp.int32),
        input_output_aliases=aliases,
        compiler_params=_cparams(1), name="experts")(*ctl, *args)


TM_FFN = 640


def _ffn_pre_body(x_ref, g_ref, shp_ref, shs_ref, scp_ref, scs_ref, rw_ref, hb_ref, lg_ref, hp_ref):
    i = pl.program_id(0)

    @pl.when(i < N // TM_FFN)
    def _():
        for sidx in range(TM_FFN // SUB):
            rows = pl.ds(sidx * SUB, SUB)
            h = _prenorm_rows(i * (TM_FFN // SUB) + sidx, x_ref[rows, :], g_ref, shp_ref, shs_ref, scp_ref, scs_ref)
            lg_ref[rows, :] = jnp.dot(h, rw_ref[...], precision=HIGHEST, preferred_element_type=F32)
            packed = _pack_bf16_pair(h[:, :D // 2], h[:, D // 2:])
            for c in range(ROW_TILE[0]):
                hp_ref[pl.ds(sidx * SUB * ROW_TILE[0] + c, SUB, stride=ROW_TILE[0]), :] = (
                    packed[:, c * LANES:(c + 1) * LANES])
            hb_ref[rows, :] = h.astype(BF16)

    @pl.when(i >= N // TM_FFN)
    def _():
        hp_ref[...] = jnp.zeros_like(hp_ref)


def _ffn_pre(x, g, mod, layer, rw):
    last = N // TM_FFN - 1
    row = lambda i: (jnp.minimum(i, last), 0)
    mspec = lambda part, rows, blk: pl.BlockSpec((None, rows, D), lambda i, l=layer, p=part, b=blk: (l, b, p))
    return pl.pallas_call(
        _ffn_pre_body,
        grid=(N // TM_FFN + 1,),
        in_specs=[pl.BlockSpec((TM_FFN, D), row), pl.BlockSpec((1, D), lambda i: (0, 0)),
                  mspec(3, 8, 0), mspec(3, SUB, 1), mspec(4, 8, 0), mspec(4, SUB, 1),
                  pl.BlockSpec((None, D, LANES), lambda i, l=layer: (l, 0, 0))],
        out_specs=[pl.BlockSpec((TM_FFN, D), row), pl.BlockSpec((TM_FFN, LANES), row),
                   pl.BlockSpec((TM_FFN * ROW_TILE[0], LANES), lambda i: (i, 0))],
        out_shape=[jax.ShapeDtypeStruct((N, D), BF16), jax.ShapeDtypeStruct((N, LANES), F32),
                   jax.ShapeDtypeStruct(((N + TM_FFN) * ROW_TILE[0], LANES), jnp.int32)],
        compiler_params=_cparams(1), name="ffn_pre")(x, g.reshape(1, D), mod, mod, mod, mod, rw)


TM_SHARED = 640


def _shared_body(h_ref, w13_ref, w2_ref, o_ref):
    up = jnp.dot(h_ref[...], w13_ref[...], preferred_element_type=F32)
    hid = (_silu(up[:, :D_EXPERT]) * up[:, D_EXPERT:]).astype(BF16)
    o_ref[...] = jnp.dot(hid, w2_ref[...], preferred_element_type=F32)


def _shared(h, w13, w2, layer):
    return pl.pallas_call(
        _shared_body,
        grid=(N // TM_SHARED,),
        in_specs=[pl.BlockSpec((TM_SHARED, D), lambda i: (i, 0)),
                  pl.BlockSpec((None, D, 2 * D_EXPERT), lambda i, l=layer: (l, 0, 0)),
                  pl.BlockSpec((None, D_EXPERT, D), lambda i, l=layer: (l, 0, 0))],
        out_specs=pl.BlockSpec((TM_SHARED, D), lambda i: (i, 0)),
        out_shape=jax.ShapeDtypeStruct((N, D), F32),
        compiler_params=_cparams(1), name="shared")(h, w13, w2)


def _combine_body(g_ref, wt_ref, sh_ref, x_ref, gn_ref, gp_ref, gs_ref, *rest, tile0, n_out, final):
    outs = rest[-n_out:]
    half = D // 2
    acc_lo = sh_ref[:, :half]
    acc_hi = sh_ref[:, half:]
    wt = wt_ref[...]
    per_tok = TOP_K * ROW_TILE[0]
    for k in range(TOP_K):
        packed = jnp.concatenate([g_ref[pl.ds(k * ROW_TILE[0] + c, SUB, stride=per_tok), :]
                                  for c in range(ROW_TILE[0])], axis=1)
        lo, hi = _unpack_bf16_pair(packed)
        w_c = wt[:, k:k + 1]
        acc_lo = acc_lo + w_c * lo
        acc_hi = acc_hi + w_c * hi
    t = tile0 + pl.program_id(0)
    val = _post_value(t, x_ref[...], jnp.concatenate([acc_lo, acc_hi], axis=1), gn_ref, gp_ref, gs_ref)
    if not final:
        outs[0][...] = val
    else:
        @pl.when(t < N_P // SUB)
        def _():
            outs[0][...] = val
        if n_out == 2:
            @pl.when(t >= N_P // SUB)
            def _():
                outs[1][...] = val


def _combine(gathered, wts, shared, x, g_post, mod, layer, tile0, n_tiles, out_prev, final):
    per_tok = TOP_K * ROW_TILE[0]
    row = lambda i: (tile0 + i, 0)
    in_specs = [pl.BlockSpec((SUB * per_tok, LANES), lambda i: (i, 0)),
                pl.BlockSpec((SUB, LANES), row), pl.BlockSpec((SUB, D), row), pl.BlockSpec((SUB, D), row),
                pl.BlockSpec((1, D), lambda i: (0, 0)),
                pl.BlockSpec((None, 8, D), lambda i, l=layer: (l, 0, 5)),
                pl.BlockSpec((None, SUB, D), lambda i, l=layer: (l, 1, 5))]
    args = [gathered.reshape(n_tiles * SUB * per_tok, LANES), wts, shared, x, g_post.reshape(1, D), mod, mod]
    aliases = {}
    if out_prev is not None:
        in_specs.append(pl.BlockSpec(memory_space=pl.ANY))
        args.append(out_prev)
        aliases = {len(args) - 1: 0}
    if not final:
        out_specs = [pl.BlockSpec((SUB, D), row)]
        out_shape = [jax.ShapeDtypeStruct((N, D), F32)]
    else:
        last_p = N_P // SUB - 1
        out_specs = [pl.BlockSpec((SUB, D), lambda i: (jnp.minimum(tile0 + i, last_p), 0))]
        out_shape = [jax.ShapeDtypeStruct((N_P, D), F32)]
        if tile0 + n_tiles > N_P // SUB:
            out_specs.append(pl.BlockSpec((N_S, D), lambda i: (0, 0)))
            out_shape.append(jax.ShapeDtypeStruct((N_S, D), F32))
    return pl.pallas_call(
        functools.partial(_combine_body, tile0=tile0, n_out=len(out_shape), final=final),
        grid=(n_tiles,),
        in_specs=in_specs, out_specs=out_specs, out_shape=out_shape,
        input_output_aliases=aliases,
        compiler_params=_cparams(1), name="combine")(*args)


def _slots_body(start_ref, eid_ref, pos_ref, o_ref):
    eid = eid_ref[...]
    acc = pos_ref[...]
    for e in range(N_EXPERTS):
        acc = acc + jnp.where(eid == e, start_ref[e], 0)
    o_ref[...] = acc


def _slots(pad_start, eid, pos):
    grid_spec = pltpu.PrefetchScalarGridSpec(
        num_scalar_prefetch=1, grid=(1,),
        in_specs=[pl.BlockSpec((TOP_K, N), lambda i, s: (0, 0)), pl.BlockSpec((TOP_K, N), lambda i, s: (0, 0))],
        out_specs=pl.BlockSpec((TOP_K, N), lambda i, s: (0, 0)))
    return pl.pallas_call(_slots_body, grid_spec=grid_spec,
                          out_shape=jax.ShapeDtypeStruct((TOP_K, N), jnp.int32),
                          compiler_params=_cparams(1), name="slots")(pad_start, eid, pos)


def _put_sample_rows_body(*refs):
    n = len(refs) // 3
    for src, dst in zip(refs[:n], refs[2 * n:]):
        dst[...] = src[...].astype(BF16)


def _put_sample_rows(sample_rows, full):
    n = len(full)
    return pl.pallas_call(
        _put_sample_rows_body,
        grid=(1,),
        in_specs=[pl.BlockSpec((N_S, 512), lambda i: (0, 0))] * n + [pl.BlockSpec(memory_space=pl.ANY)] * n,
        out_specs=[pl.BlockSpec((N_S, 512), lambda i: (N_P // N_S, 0))] * n,
        out_shape=[jax.ShapeDtypeStruct((N, 512), BF16)] * n,
        input_output_aliases={n + k: k for k in range(n)},
        compiler_params=_cparams(1), name="put_sample_rows")(*sample_rows, *full)


COMBINE_RANGES = ((0, 33), (33, 32))

def _prepare_weights(w_in, w_merge_gate, w_branch, w_out, router_w, shared_w1, shared_w3, shared_w2):
    return dict(
        w_main=jnp.concatenate([w_in[:, :, :1536], w_in[:, :, 1552:]], axis=2).astype(BF16),
        w_low=jnp.pad(w_in[:, :, 1536:1552], ((0, 0), (0, 0), (0, LANES - 16))).astype(BF16),
        w_mg=w_merge_gate.astype(BF16), w_br=w_branch.astype(BF16), w_out=w_out.astype(BF16),
        rw=jnp.pad(router_w, ((0, 0), (0, 0), (0, LANES - N_EXPERTS))),
        w13=jnp.concatenate([shared_w1, shared_w3], axis=2).astype(BF16), sw2=shared_w2.astype(BF16))


def _rope_tables(pos):
    half = DK // 2
    inv = ROPE_BASE ** (-jnp.arange(half, dtype=F32) / half)
    ang = pos.astype(F32)[:, None] * inv[None, :]
    cos = jnp.cos(ang)
    sin = jnp.sin(ang)
    return jnp.concatenate([cos, cos], axis=1), jnp.concatenate([-sin, sin], axis=1)


def _layer(l, x, mod, s_gla, s_pool, s_ret, wts, prep, final, prev_gla, prev_ret):
    (norm_mix_pre, norm_mix_post, norm_ffn_pre, norm_ffn_post, w_in, w_gla_gate, b_gla_gate, gla_norm,
     pool_w, pool_scale, ret_norm, sgu_norm, sgu_w, sgu_b, w_branch, w_merge_gate, b_merge_gate, w_out,
     router_w, router_bias, expert_w1, expert_w3, expert_w2, shared_w1, shared_w3, shared_w2) = wts

    h = _prenorm(x, norm_mix_pre[l], mod, l)
    p_main = _matmul(h, prep["w_main"], l, 1664, 512, name="inproj")
    p_low = _matmul(h, prep["w_low"], l, 1664, LANES, name="inproj_low")

    w_gate_pad = jnp.pad(w_gla_gate[l], ((0, LANES - 16), (0, 0)))
    b_gate = b_gla_gate[l].reshape(1, HEADS * DK)
    log_gamma = jnp.log1p(-jnp.exp2(-5.0 - jnp.arange(HEADS, dtype=F32)))
    dec_row = jnp.repeat(log_gamma, DK).reshape(1, HEADS * DK)
    cos_p, sin_p = _rope_tables(jnp.arange(T_P))
    cos_p = jnp.tile(cos_p, (1, 2))
    sin_p = jnp.tile(sin_p, (1, 2))
    g_gla = gla_norm[l].reshape(1, HEADS * DV)
    g_ret = ret_norm[l].reshape(1, HEADS * DV)

    oa_p, gla_p = _la_prompt(p_main, C_GQ, C_GK, C_GV, C_GR, p_low, p_low, w_gate_pad, b_gate, g_gla, False)
    oc_p, ret_p = _la_prompt(p_main, C_RQ, C_RK, C_RV, C_RG, cos_p, sin_p, dec_row, b_gate, g_ret, True)
    pw_bf = pool_w[l].astype(BF16)
    pscale = pool_scale[l].reshape(1, 512)
    ob_p = _pool_prompt(p_main, pw_bf, pscale)
    sgu_g = sgu_norm[l].reshape(1, 512)
    od_p = _sgu_prompt(p_main, sgu_g, sgu_w[l], jnp.pad(sgu_b[l].T, ((0, 0), (0, LANES - 4))))

    ps = p_main[N_P:]
    q_t = _to_tiles_t(ps[:, C_GQ:C_GQ + 256])
    k_t = _to_tiles_t(ps[:, C_GK:C_GK + 256])
    glow_t = jnp.pad(_to_tiles_t(p_low[N_P:, :16]), ((0, 0), (0, LANES - 16), (0, 0)))
    w_gate_t = jnp.pad(w_gla_gate[l].T, ((0, 0), (0, LANES - 16)))
    b_col = jnp.broadcast_to(b_gla_gate[l][:, None], (HEADS * DK, LANES))
    logit_t = _gate_logits_t(w_gate_t, glow_t, b_col)
    dummy = jnp.zeros((HEADS * DK, LANES), F32)
    oa_s, gla_s = _la_sample(q_t, k_t, logit_t, dummy, dummy, p_main, C_GV, C_GR, g_gla, s_gla, l, prev_gla, False)
    cos_s, sin_s = _rope_tables(jnp.full((1,), PAST_LEN))
    cos_c = jnp.broadcast_to(jnp.tile(cos_s[0], HEADS)[:, None], (HEADS * DK, LANES))
    sin_c = jnp.broadcast_to(jnp.tile(sin_s[0], HEADS)[:, None], (HEADS * DK, LANES))
    dec_c = jnp.broadcast_to(jnp.repeat(log_gamma, DK)[:, None], (HEADS * DK, LANES))
    rq_t = _to_tiles_t(ps[:, C_RQ:C_RQ + 256])
    rk_t = _to_tiles_t(ps[:, C_RK:C_RK + 256])
    oc_s, ret_s = _la_sample(rq_t, rk_t, dec_c, cos_c, sin_c, p_main, C_RV, C_RG, g_ret, s_ret, l, prev_ret, True)
    sgu_w0 = jnp.repeat(sgu_w[l][:, 0, 0], LANES).reshape(1, 512)
    sgu_b0 = jnp.repeat(sgu_b[l][:, 0], LANES).reshape(1, 512)
    ob_s, od_s, vn_s = _small_sample(p_main, s_pool[l], pw_bf, pscale, sgu_g, sgu_w0, sgu_b0)
    pool_p = jnp.stack([p_main[(b + 1) * T_P - POOL_BUF:(b + 1) * T_P, C_PIN:C_PIN + 512] for b in range(B_P)])
    pool_s = jnp.concatenate([s_pool[l][:, 1:], ps[:, None, C_PIN:C_PIN + 512]], axis=1)

    branches = _put_sample_rows([oa_s, ob_s, oc_s, od_s], [oa_p, ob_p, oc_p, od_p])
    merged = _merge(h, branches, prep["w_mg"], b_merge_gate.reshape(DEPTH, 1, 4 * D), prep["w_br"], l)
    x = _outproj(merged, prep["w_out"], x, norm_mix_post[l], mod, l)

    rb = jnp.broadcast_to(router_bias[l][:, None], (N_EXPERTS, SUB))
    h2, logits, h2_packed = _ffn_pre(x, norm_ffn_pre[l], mod, l, prep["rw"])
    eid, pos, wt, counts = _router(logits, rb)
    counts = counts[:, 0].astype(jnp.int32)
    padded = (counts + EXP_BLOCK - 1) // EXP_BLOCK * EXP_BLOCK
    pad_end = jnp.cumsum(padded)
    pad_start = pad_end - padded
    nused = (pad_end[-1] // EXP_BLOCK).astype(jnp.int32).reshape(1)
    blk_row = jnp.arange(N_BLOCKS, dtype=jnp.int32) * EXP_BLOCK
    block_e = jnp.minimum(jnp.sum((blk_row[:, None] >= pad_end[None, :]).astype(jnp.int32), axis=1),
                          N_EXPERTS - 1)
    first = jnp.concatenate([jnp.ones((1,), jnp.int32), (block_e[1:] != block_e[:-1]).astype(jnp.int32)])
    first = jnp.where(blk_row < pad_end[-1], first, 0)
    par = (jnp.cumsum(first) - 1) % 2
    live = jnp.where(padded > 0, jnp.arange(N_EXPERTS), N_EXPERTS)
    after = jnp.concatenate([lax.cummin(live, reverse=True)[1:], jnp.full((1,), N_EXPERTS)])
    of_block = block_e[:, None] == jnp.arange(N_EXPERTS)
    next_e = jnp.sum(jnp.where(of_block, jnp.where(after < N_EXPERTS, after, -1), 0), axis=1).astype(jnp.int32)
    next_blk = jnp.sum(jnp.where(of_block, pad_end // EXP_BLOCK, 0), axis=1).astype(jnp.int32)
    slots = _slots(pad_start.astype(jnp.int32), eid, pos).T.reshape(N_ASSIGN)
    wt = jnp.pad(wt.T, ((0, 0), (0, LANES - TOP_K)))
    table = h2_packed.reshape((N + TM_FFN,) + ROW_TILE)
    ctl = (block_e, first, par.astype(jnp.int32), next_e, next_blk, nused)
    xs_parts = [_sc_dispatch(table, slots, part) for part in range(N_PARTS)]
    shared = _shared(h2, prep["w13"], prep["sw2"], l)
    ys = None
    for part in range(N_PARTS):
        ys = _experts(ctl, xs_parts[part], expert_w1, expert_w3, expert_w2, l, part, ys)
    ys = ys.reshape((L_SLOTS,) + ROW_TILE)
    outs = [None]
    for tile0, n_tiles in COMBINE_RANGES:
        a0, a1 = tile0 * SUB * TOP_K, (tile0 + n_tiles) * SUB * TOP_K
        outs = _combine(_sc_gather(ys, slots[a0:a1]), wt, shared, x, norm_ffn_post[l], mod, l, tile0, n_tiles,
                        outs[0], final)
    x = tuple(outs) if final else outs[0]
    return x, (gla_p, pool_p, pool_s, ret_p, vn_s), gla_s, ret_s


def kernel(x_prompt, x_sample, c_prompt, c_sample, state_gla, state_pool, state_ret, w_ada, b_ada, norm_mix_pre, norm_mix_post, norm_ffn_pre, norm_ffn_post, w_in, w_gla_gate, b_gla_gate, gla_norm, pool_w, pool_scale, ret_norm, sgu_norm, sgu_w, sgu_b, w_branch, w_merge_gate, b_merge_gate, w_out, router_w, router_bias, expert_w1, expert_w3, expert_w2, shared_w1, shared_w3, shared_w2):
    wts = (norm_mix_pre, norm_mix_post, norm_ffn_pre, norm_ffn_post, w_in, w_gla_gate, b_gla_gate, gla_norm,
           pool_w, pool_scale, ret_norm, sgu_norm, sgu_w, sgu_b, w_branch, w_merge_gate, b_merge_gate, w_out,
           router_w, router_bias, expert_w1, expert_w3, expert_w2, shared_w1, shared_w3, shared_w2)
    c_all = jnp.zeros((MOD_ROWS, D), F32).at[:B_P].set(c_prompt).at[SUB:SUB + N_S].set(c_sample)
    mod = _ada(c_all, w_ada, b_ada)
    x = (x_prompt.reshape(N_P, D), x_sample.reshape(N_S, D))
    prep = _prepare_weights(w_in, w_merge_gate, w_branch, w_out, router_w, shared_w1, shared_w3, shared_w2)
    per_layer = []
    gla_s = ret_s = None
    for l in range(DEPTH):
        x, states, gla_s, ret_s = _layer(l, x, mod, state_gla, state_pool, state_ret, wts, prep, l == DEPTH - 1,
                                         gla_s, ret_s)
        per_layer.append(states)
    gla_p, pool_p, pool_s, ret_p, vn_s = (jnp.stack(z) for z in zip(*per_layer))
    return (x[0].reshape(B_P, T_P, D), x[1].reshape(N_S, 1, D),
            gla_p, gla_s, pool_p, pool_s, ret_p, ret_s, vn_s.reshape(DEPTH, N_S, 1, 512))
```

```python
import functools

import jax
import jax.numpy as jnp
from jax import lax
from jax.experimental import pallas as pl
from jax.experimental.pallas import tpu as pltpu
from jax.experimental.pallas import tpu_sc as plsc

F32 = jnp.float32
BF16 = jnp.bfloat16
HIGHEST = lax.Precision.HIGHEST

D = 2048
B_P, T_P = 4, 2048
N_P = B_P * T_P
N_S = 128
N = N_P + N_S
DEPTH = 2
PAST_LEN = 16384
EPS = 1e-6
HEADS, DK, DV = 4, 64, 128
CHUNK = 64
GATE_TEMP = 16.0
POOL_WINDOWS = (2, 4, 8, 16)
POOL_BUF = 15
ROPE_BASE = 10000.0
N_EXPERTS = 64
TOP_K = 8
D_EXPERT = 512
ROUTED_SCALE = 2.5

LANES = 128
SUB = 128
MOD_ROWS = 256
EXP_BLOCK = 256
N_ASSIGN = N * TOP_K
N_BLOCKS = -(-(N_ASSIGN + N_EXPERTS * (EXP_BLOCK - 1)) // EXP_BLOCK)
L_SLOTS = N_BLOCKS * EXP_BLOCK
VMEM_LIMIT = 56 * 1024 * 1024

C_GQ, C_GK, C_GV, C_GR, C_PIN, C_RQ, C_RK, C_RV, C_RG, C_SU, C_SV = (
    0, 256, 512, 1024, 1536, 2048, 2304, 2560, 3072, 3584, 4096)
P_MAIN = 4608


def _cparams(n_axes=1):
    return pltpu.CompilerParams(dimension_semantics=("arbitrary",) * n_axes,
                                vmem_limit_bytes=VMEM_LIMIT)


def _silu(x):
    return x * jax.nn.sigmoid(x)


def _mod_rows(t, mp_ref, ms_ref):
    b = jnp.minimum(t // (T_P // SUB), B_P - 1)
    return jnp.where(t >= N_P // SUB, ms_ref[...], mp_ref[pl.ds(b, 1), :])


def _mod_specs(layer, part):
    return [pl.BlockSpec((None, 8, D), lambda i, l=layer, p=part: (l, 0, p)),
            pl.BlockSpec((None, SUB, D), lambda i, l=layer, p=part: (l, 1, p))]


def _pack_bf16_pair(lo, hi):
    lo_u = lax.bitcast_convert_type(lo.astype(BF16).astype(F32), jnp.uint32)
    hi_u = lax.bitcast_convert_type(hi.astype(BF16).astype(F32), jnp.uint32)
    return lax.bitcast_convert_type((hi_u & jnp.uint32(0xFFFF0000)) | (lo_u >> 16), jnp.int32)


def _unpack_bf16_pair(w):
    u = lax.bitcast_convert_type(w, jnp.uint32)
    lo = lax.bitcast_convert_type(u << 16, F32)
    hi = lax.bitcast_convert_type(u & jnp.uint32(0xFFFF0000), F32)
    return lo, hi


ROW_TILE = (8, LANES)


def _load_row_tiles(ref):
    return jnp.concatenate([ref[:, c, :] for c in range(ROW_TILE[0])], axis=1)


def _store_row_tiles(ref, val):
    for c in range(ROW_TILE[0]):
        ref[:, c, :] = val[:, c * LANES:(c + 1) * LANES]


def _load_row_tiles_2d(ref, rows):
    return jnp.concatenate([ref[pl.ds(c, rows, stride=ROW_TILE[0]), :] for c in range(ROW_TILE[0])], axis=1)


def _store_row_tiles_2d(ref, val, rows):
    for c in range(ROW_TILE[0]):
        ref[pl.ds(c, rows, stride=ROW_TILE[0]), :] = val[:, c * LANES:(c + 1) * LANES]


def _ada_body(c_ref, w_ref, b_ref, o_ref):
    s = _silu(c_ref[...]).astype(BF16)
    o_ref[...] = jnp.dot(s, w_ref[...].astype(BF16), preferred_element_type=F32) + b_ref[...]


def _ada(c_all, w_ada, b_ada):
    tn = 1024
    return pl.pallas_call(
        _ada_body,
        grid=(DEPTH, 6 * D // tn),
        in_specs=[pl.BlockSpec((MOD_ROWS, D), lambda l, j: (0, 0)),
                  pl.BlockSpec((None, D, tn), lambda l, j: (l, 0, j)),
                  pl.BlockSpec((None, 1, tn), lambda l, j: (l, 0, j))],
        out_specs=pl.BlockSpec((None, MOD_ROWS, tn), lambda l, j: (l, 0, j)),
        out_shape=jax.ShapeDtypeStruct((DEPTH, MOD_ROWS, 6 * D), F32),
        compiler_params=_cparams(2), name="ada")(c_all, w_ada, b_ada.reshape(DEPTH, 1, 6 * D))


ZERO_ROWS = 2 * SUB


def _x_specs(x, tm, n_axes):
    row = (lambda i: (i, 0)) if n_axes == 1 else (lambda i, j: (i, 0))
    if not isinstance(x, tuple):
        return [pl.BlockSpec((tm, D), row)], [x]
    zero = (lambda i: (0, 0)) if n_axes == 1 else (lambda i, j: (0, 0))
    return [pl.BlockSpec((tm, D), row), pl.BlockSpec((N_S, D), zero)], list(x)


def _x_rows(x_refs, t, rows):
    if len(x_refs) == 1:
        return x_refs[0][rows, :]
    return jnp.where(t >= N_P // SUB, x_refs[1][...], x_refs[0][rows, :])


def _prenorm_rows(t, x, g_ref, shp_ref, shs_ref, scp_ref, scs_ref):
    y = x * lax.rsqrt(jnp.mean(x * x, axis=-1, keepdims=True) + EPS) * g_ref[...]
    return y * (1.0 + _mod_rows(t, scp_ref, scs_ref)) + _mod_rows(t, shp_ref, shs_ref)


TM_NORM = 640


def _prenorm_body(*refs):
    g_ref, shp_ref, shs_ref, scp_ref, scs_ref, h_ref = refs[-6:]
    for sidx in range(TM_NORM // SUB):
        rows = pl.ds(sidx * SUB, SUB)
        t = pl.program_id(0) * (TM_NORM // SUB) + sidx
        h_ref[rows, :] = _prenorm_rows(t, _x_rows(refs[:-6], t, rows), g_ref, shp_ref, shs_ref, scp_ref,
                                       scs_ref).astype(BF16)


def _prenorm(x, g, mod, layer):
    x_specs, x_args = _x_specs(x, TM_NORM, 1)
    return pl.pallas_call(
        _prenorm_body,
        grid=(N // TM_NORM,),
        in_specs=x_specs + [pl.BlockSpec((1, D), lambda i: (0, 0))] + _mod_specs(layer, 0) + _mod_specs(layer, 1),
        out_specs=pl.BlockSpec((TM_NORM, D), lambda i: (i, 0)),
        out_shape=jax.ShapeDtypeStruct((N, D), BF16),
        compiler_params=_cparams(1), name="prenorm")(*x_args, g.reshape(1, D), mod, mod, mod, mod)


def _mm_body(x_ref, w_ref, o_ref):
    o_ref[...] = jnp.dot(x_ref[...], w_ref[...], preferred_element_type=F32).astype(o_ref.dtype)


def _matmul(x, w_all, layer, tm, tn, out_dtype=F32, name="mm"):
    m, k = x.shape
    n = w_all.shape[2]
    return pl.pallas_call(
        _mm_body,
        grid=(m // tm, n // tn),
        in_specs=[pl.BlockSpec((tm, k), lambda i, j: (i, 0)),
                  pl.BlockSpec((None, k, tn), lambda i, j, l=layer: (l, 0, j))],
        out_specs=pl.BlockSpec((tm, tn), lambda i, j: (i, j)),
        out_shape=jax.ShapeDtypeStruct((m, n), out_dtype),
        compiler_params=_cparams(2), name=name)(x, w_all)


ROWS_LA = 256


def _swap_halves_lanes(x):
    lane = lax.broadcasted_iota(jnp.int32, x.shape, 1)
    return jnp.where((lane % 64) < 32, pltpu.roll(x, 96, 1), pltpu.roll(x, 32, 1))


def _rope_lanes(x, cos, sin_signed):
    parts = []
    for half in range(2):
        xh = x[:, half * LANES:(half + 1) * LANES]
        parts.append(xh * cos + _swap_halves_lanes(xh) * sin_signed)
    return jnp.concatenate(parts, axis=1)


def _la_prompt_body(q_ref, k_ref, v_ref, r_ref, aux_ref, aux2_ref, dec_ref, bias_ref, g_ref,
                    o_ref, st_out_ref, st_ref, *, retention):
    t = pl.program_id(1)

    @pl.when(t == 0)
    def _():
        st_ref[...] = jnp.zeros_like(st_ref)

    n_ch = ROWS_LA // CHUNK
    ri = lax.broadcasted_iota(jnp.int32, (ROWS_LA, ROWS_LA), 0)
    ci = lax.broadcasted_iota(jnp.int32, (ROWS_LA, ROWS_LA), 1)
    causal = (ri >= ci) & ((ri // CHUNK) == (ci // CHUNK))
    scale = DK ** -0.5

    q = q_ref[...]
    k = k_ref[...]
    v = v_ref[...]
    if retention:
        cos = aux_ref[...]
        sin = aux2_ref[...]
        q = _rope_lanes(q, cos, sin)
        k = _rope_lanes(k, cos, sin) * scale
        step = (lax.broadcasted_iota(jnp.int32, (ROWS_LA, HEADS * DK), 0) % CHUNK + 1).astype(F32)
        bc = step * dec_ref[...]
    else:
        q = q * scale
        logit = jnp.dot(aux_ref[...], dec_ref[...], precision=HIGHEST, preferred_element_type=F32) + bias_ref[...]
        la = jax.nn.log_sigmoid(logit) / GATE_TEMP
        bc = jnp.dot(causal.astype(F32), la, precision=HIGHEST, preferred_element_type=F32)
    bl = bc.reshape(n_ch, CHUNK, HEADS * DK)[:, CHUNK - 1:CHUNK, :]
    bl_rows = jnp.broadcast_to(bl, (n_ch, CHUNK, HEADS * DK)).reshape(ROWS_LA, HEADS * DK)
    qd = q * jnp.exp(bc)
    ki = k * jnp.exp(-bc)
    ke = k * jnp.exp(bl_rows - bc)
    ac = jnp.exp(bl)
    outs = []
    for h in range(HEADS):
        ks = slice(h * DK, (h + 1) * DK)
        vs = slice(h * DV, (h + 1) * DV)
        qd_h = qd[:, ks].astype(BF16)
        ki_h = ki[:, ks].astype(BF16)
        ke_h = ke[:, ks].astype(BF16)
        v_h = v[:, vs].astype(BF16)
        sc = lax.dot_general(qd_h, ki_h, (((1,), (1,)), ((), ())), preferred_element_type=F32)
        sc = jnp.where(causal, sc, 0.0)
        o_h = jnp.dot(sc.astype(BF16), v_h, preferred_element_type=F32)
        inter = []
        for c in range(n_ch):
            rows = slice(c * CHUNK, (c + 1) * CHUNK)
            st = st_ref[h]
            inter.append(lax.dot_general(qd_h[rows], st.astype(BF16), (((1,), (1,)), ((), ())),
                                         preferred_element_type=F32))
            kv_t = lax.dot_general(v_h[rows], ke_h[rows], (((0,), (0,)), ((), ())), preferred_element_type=F32)
            st_ref[h] = st * ac[c][:, ks] + kv_t
        o_h = o_h + jnp.concatenate(inter, axis=0)
        outs.append(o_h * lax.rsqrt(jnp.mean(o_h * o_h, axis=-1, keepdims=True) + EPS) * g_ref[:, vs])
    o_ref[...] = (jnp.concatenate(outs, axis=1) * _silu(r_ref[...])).astype(BF16)
    st_out_ref[...] = st_ref[...]


def _la_prompt(p_main, cq, ck, cv, cr, aux, aux2, dec, bias, g, retention):
    nt = T_P // ROWS_LA
    rowblk = lambda b, t: b * nt + t
    if retention:
        aux_specs = [pl.BlockSpec((ROWS_LA, LANES), lambda b, t: (t, 0)),
                     pl.BlockSpec((ROWS_LA, LANES), lambda b, t: (t, 0))]
    else:
        aux_specs = [pl.BlockSpec((ROWS_LA, LANES), lambda b, t: (rowblk(b, t), 0)),
                     pl.BlockSpec((8, LANES), lambda b, t: (0, 0))]
    o, st = pl.pallas_call(
        functools.partial(_la_prompt_body, retention=retention),
        grid=(B_P, nt),
        in_specs=[pl.BlockSpec((ROWS_LA, 256), lambda b, t: (rowblk(b, t), cq // 256)),
                  pl.BlockSpec((ROWS_LA, 256), lambda b, t: (rowblk(b, t), ck // 256)),
                  pl.BlockSpec((ROWS_LA, 512), lambda b, t: (rowblk(b, t), cv // 512)),
                  pl.BlockSpec((ROWS_LA, 512), lambda b, t: (rowblk(b, t), cr // 512))]
        + aux_specs
        + [pl.BlockSpec(dec.shape, lambda b, t: (0, 0)),
           pl.BlockSpec((1, HEADS * DK), lambda b, t: (0, 0)),
           pl.BlockSpec((1, HEADS * DV), lambda b, t: (0, 0))],
        out_specs=[pl.BlockSpec((ROWS_LA, HEADS * DV), lambda b, t: (rowblk(b, t), 0)),
                   pl.BlockSpec((None, HEADS, DV, DK), lambda b, t: (b, 0, 0, 0))],
        out_shape=[jax.ShapeDtypeStruct((N, HEADS * DV), BF16),
                   jax.ShapeDtypeStruct((B_P, HEADS, DV, DK), F32)],
        scratch_shapes=[pltpu.VMEM((HEADS, DV, DK), F32)],
        compiler_params=_cparams(2), name="ret_prompt" if retention else "gla_prompt",
    )(p_main, p_main, p_main, p_main, aux, aux2, dec, bias, g)
    return o, jnp.swapaxes(st, -1, -2)


SAMPLE_TILE = 8


def _la_sample_body(qt_ref, kt_ref, lt_ref, cos_ref, sin_ref, v_ref, r_ref, g_ref, s_ref, *rest, retention):
    o_ref, s_out_ref = rest[-2:]
    scale = DK ** -0.5
    qt = qt_ref[...]
    kt = kt_ref[...]
    if retention:
        def rope(x):
            sw = jnp.concatenate(
                [x[h * DK + (DK // 2) * (1 - j): h * DK + (DK // 2) * (2 - j), :]
                 for h in range(HEADS) for j in range(2)], axis=0)
            return x * cos_ref[...] + sw * sin_ref[...]
        qt = rope(qt)
        kt = rope(kt) * scale
        la = lt_ref[...]
    else:
        qt = qt * scale
        la = jax.nn.log_sigmoid(lt_ref[...]) / GATE_TEMP
    at = jnp.exp(la)
    qd = qt * at
    ki = kt * jnp.exp(-la)
    prod = qd * ki
    v8 = v_ref[...]
    r8 = r_ref[...]
    g = g_ref[...]
    for j in range(SAMPLE_TILE):
        for h in range(HEADS):
            ks = slice(h * DK, (h + 1) * DK)
            vs = slice(h * DV, (h + 1) * DV)
            a_c = jnp.broadcast_to(at[ks, j:j + 1], (DK, DV))
            k_c = jnp.broadcast_to(kt[ks, j:j + 1], (DK, DV))
            q_c = jnp.broadcast_to(qd[ks, j:j + 1], (DK, DV))
            s_c = jnp.broadcast_to(jnp.sum(prod[ks, j:j + 1], axis=0, keepdims=True), (1, DV))
            s0 = s_ref[j, h]
            v_row = v8[j:j + 1, vs]
            s_out_ref[j, h] = a_c * s0 + k_c * v_row
            o_row = s_c * v_row + jnp.sum(q_c * s0, axis=0, keepdims=True)
            o_n = o_row * lax.rsqrt(jnp.mean(o_row * o_row, axis=-1, keepdims=True) + EPS) * g[:, vs]
            o_ref[j:j + 1, vs] = o_n * _silu(r8[j:j + 1, vs])


def _la_sample(qt, kt, lt, cos_t, sin_t, p_main, cv, cr, g, s0_all, layer, s_prev, retention):
    nt = N_S // SAMPLE_TILE
    row0 = N_P // SAMPLE_TILE
    tile = pl.BlockSpec((None, HEADS * DK, LANES), lambda i: (i, 0, 0))
    full = pl.BlockSpec((HEADS * DK, LANES), lambda i: (0, 0))
    lt_spec = full if retention else tile
    return pl.pallas_call(
        functools.partial(_la_sample_body, retention=retention),
        grid=(nt,),
        in_specs=[tile, tile, lt_spec, full, full,
                  pl.BlockSpec((SAMPLE_TILE, 512), lambda i: (row0 + i, cv // 512)),
                  pl.BlockSpec((SAMPLE_TILE, 512), lambda i: (row0 + i, cr // 512)),
                  pl.BlockSpec((1, HEADS * DV), lambda i: (0, 0)),
                  pl.BlockSpec((None, SAMPLE_TILE, HEADS, DK, DV), lambda i, l=layer: (l, i, 0, 0, 0))]
        + ([] if s_prev is None else [pl.BlockSpec(memory_space=pl.ANY)]),
        out_specs=[pl.BlockSpec((SAMPLE_TILE, HEADS * DV), lambda i: (i, 0)),
                   pl.BlockSpec((None, SAMPLE_TILE, HEADS, DK, DV), lambda i, l=layer: (l, i, 0, 0, 0))],
        out_shape=[jax.ShapeDtypeStruct((N_S, HEADS * DV), F32),
                   jax.ShapeDtypeStruct((DEPTH, N_S, HEADS, DK, DV), F32)],
        input_output_aliases={} if s_prev is None else {9: 1},
        compiler_params=_cparams(1), name="ret_sample" if retention else "gla_sample",
    )(qt, kt, lt, cos_t, sin_t, p_main, p_main, g, s0_all, *([] if s_prev is None else [s_prev]))


def _gate_logits_t_body(w_ref, x_ref, b_ref, o_ref):
    o_ref[...] = jnp.dot(w_ref[...], x_ref[...], precision=HIGHEST, preferred_element_type=F32) + b_ref[...]


def _gate_logits_t(w_gate_t, glow_t, b_col):
    nt = N_S // SAMPLE_TILE
    return pl.pallas_call(
        _gate_logits_t_body,
        grid=(nt,),
        in_specs=[pl.BlockSpec((HEADS * DK, LANES), lambda i: (0, 0)),
                  pl.BlockSpec((None, LANES, LANES), lambda i: (i, 0, 0)),
                  pl.BlockSpec((HEADS * DK, LANES), lambda i: (0, 0))],
        out_specs=pl.BlockSpec((None, HEADS * DK, LANES), lambda i: (i, 0, 0)),
        out_shape=jax.ShapeDtypeStruct((nt, HEADS * DK, LANES), F32),
        compiler_params=_cparams(1), name="gate_logits_t")(w_gate_t, glow_t, b_col)


def _to_tiles_t(x):
    c = x.shape[1]
    xt = jnp.swapaxes(x.reshape(N_S // SAMPLE_TILE, SAMPLE_TILE, c), 1, 2)
    return jnp.pad(xt, ((0, 0), (0, 0), (0, LANES - SAMPLE_TILE)))


ROWS_POOL = 512


def _pool_mix(y, w_ref, sc_ref):
    outs = []
    for gi in range(4):
        cs = slice(gi * LANES, (gi + 1) * LANES)
        outs.append(jnp.dot(y[:, cs].astype(BF16), w_ref[gi], preferred_element_type=F32))
    return jnp.concatenate(outs, axis=1) * sc_ref[...]


def _pool_prompt_body(p_ref, halo_ref, w_ref, sc_ref, o_ref):
    t = pl.program_id(1)
    p = p_ref[...]
    halo = jnp.where(t == 0, 0.0, halo_ref[...])
    full = jnp.concatenate([halo, p], axis=0)
    pos = t * ROWS_POOL + lax.broadcasted_iota(jnp.int32, (ROWS_POOL, LANES), 0)
    means = []
    for gi, w in enumerate(POOL_WINDOWS):
        s = full[:, gi * LANES:(gi + 1) * LANES]
        step = 1
        while step < w:
            s = s + pltpu.roll(s, step, 0)
            step *= 2
        win = s[16:, :]
        cnt = jnp.minimum(w, pos + 1).astype(F32)
        means.append(win / cnt)
    y = jnp.concatenate(means, axis=1) - p
    o_ref[...] = _pool_mix(y, w_ref, sc_ref).astype(BF16)


def _pool_prompt(p_main, w_bf, scale):
    nt = T_P // ROWS_POOL
    return pl.pallas_call(
        _pool_prompt_body,
        grid=(B_P, nt),
        in_specs=[pl.BlockSpec((ROWS_POOL, 512), lambda b, t: (b * nt + t, C_PIN // 512)),
                  pl.BlockSpec((16, 512), lambda b, t: (jnp.maximum((b * nt + t) * (ROWS_POOL // 16) - 1, 0),
                                                        C_PIN // 512)),
                  pl.BlockSpec((4, LANES, LANES), lambda b, t: (0, 0, 0)),
                  pl.BlockSpec((1, 512), lambda b, t: (0, 0))],
        out_specs=pl.BlockSpec((ROWS_POOL, 512), lambda b, t: (b * nt + t, 0)),
        out_shape=jax.ShapeDtypeStruct((N, 512), BF16),
        compiler_params=_cparams(2), name="pool_prompt")(p_main, p_main, w_bf, scale)


def _small_sample_body(p_ref, buf_ref, pw_ref, psc_ref, u_ref, sv_ref, sg_ref, sw_ref, sb_ref,
                       ob_ref, od_ref, vn_ref):
    p = p_ref[...]
    means = []
    for gi, w in enumerate(POOL_WINDOWS):
        cs = slice(gi * LANES, (gi + 1) * LANES)
        s = p[:, cs]
        for j in range(1, w):
            s = s + buf_ref[:, POOL_BUF - j, cs]
        means.append(s / float(min(w, PAST_LEN + 1)))
    y = jnp.concatenate(means, axis=1) - p
    ob_ref[...] = _pool_mix(y, pw_ref, psc_ref)
    sv = sv_ref[...]
    vn = sv * lax.rsqrt(jnp.mean(sv * sv, axis=-1, keepdims=True) + EPS) * sg_ref[...]
    vn_ref[...] = vn
    od_ref[...] = u_ref[...] * (sw_ref[...] * vn + sb_ref[...])


def _small_sample(p_main, buf, pw_bf, pscale, sgu_g, sgu_w0, sgu_b0):
    row = N_P // N_S
    col = lambda c: pl.BlockSpec((N_S, 512), lambda i, c=c: (row, c // 512))
    vec = pl.BlockSpec((1, 512), lambda i: (0, 0))
    return pl.pallas_call(
        _small_sample_body,
        grid=(1,),
        in_specs=[col(C_PIN), pl.BlockSpec((N_S, POOL_BUF, 512), lambda i: (0, 0, 0)),
                  pl.BlockSpec((4, LANES, LANES), lambda i: (0, 0, 0)), vec,
                  col(C_SU), col(C_SV), vec, vec, vec],
        out_specs=[pl.BlockSpec((N_S, 512), lambda i: (0, 0))] * 3,
        out_shape=[jax.ShapeDtypeStruct((N_S, 512), F32)] * 3,
        compiler_params=_cparams(1), name="small_sample",
    )(p_main, buf, pw_bf, pscale, p_main, p_main, sgu_g, sgu_w0, sgu_b0)


ROWS_SGU = 512
SGU_CHUNK = 128


def _sgu_prompt_body(u_ref, v_ref, g_ref, w_ref, bt_ref, o_ref):
    ri = lax.broadcasted_iota(jnp.int32, (SGU_CHUNK, SGU_CHUNK), 0)
    ci = lax.broadcasted_iota(jnp.int32, (SGU_CHUNK, SGU_CHUNK), 1)
    causal = ri >= ci
    for c in range(ROWS_SGU // SGU_CHUNK):
        rows = pl.ds(c * SGU_CHUNK, SGU_CHUNK)
        v = v_ref[rows, :]
        vn = (v * lax.rsqrt(jnp.mean(v * v, axis=-1, keepdims=True) + EPS) * g_ref[...]).astype(BF16)
        outs = []
        for gi in range(4):
            cs = slice(gi * LANES, (gi + 1) * LANES)
            w = jnp.where(causal, w_ref[gi], 0.0).astype(BF16)
            mixed = jnp.dot(w, vn[:, cs], preferred_element_type=F32)
            outs.append(mixed + jnp.broadcast_to(bt_ref[:, gi:gi + 1], (SGU_CHUNK, LANES)))
        o_ref[rows, :] = (u_ref[rows, :] * jnp.concatenate(outs, axis=1)).astype(BF16)


def _sgu_prompt(p_main, g, w, b_t):
    return pl.pallas_call(
        _sgu_prompt_body,
        grid=(N_P // ROWS_SGU,),
        in_specs=[pl.BlockSpec((ROWS_SGU, 512), lambda i: (i, C_SU // 512)),
                  pl.BlockSpec((ROWS_SGU, 512), lambda i: (i, C_SV // 512)),
                  pl.BlockSpec((1, 512), lambda i: (0, 0)),
                  pl.BlockSpec((4, SGU_CHUNK, SGU_CHUNK), lambda i: (0, 0, 0)),
                  pl.BlockSpec((SGU_CHUNK, LANES), lambda i: (0, 0))],
        out_specs=pl.BlockSpec((ROWS_SGU, 512), lambda i: (i, 0)),
        out_shape=jax.ShapeDtypeStruct((N, 512), BF16),
        compiler_params=_cparams(1), name="sgu_prompt")(p_main, p_main, g, w, b_t)


TM_MERGE = 640
TN_MERGE = 512


def _merge_body(h_ref, ba_ref, bb_ref, bc_ref, bd_ref, g0, g1, g2, g3, u0, u1, u2, u3,
                c0, c1, c2, c3, o_ref):
    h = h_ref[...]
    acc = None
    for br, gw, uw, gb in ((ba_ref, g0, u0, c0), (bb_ref, g1, u1, c1), (bc_ref, g2, u2, c2), (bd_ref, g3, u3, c3)):
        gate = jax.nn.sigmoid(jnp.dot(h, gw[...], preferred_element_type=F32) + gb[...])
        up = jnp.dot(br[...], uw[...], preferred_element_type=F32)
        acc = gate * up if acc is None else acc + gate * up
    o_ref[...] = acc.astype(BF16)


def _merge(h, branches, w_mg, b_mg, w_br, layer):
    nj = D // TN_MERGE
    row = lambda w: pl.BlockSpec((TM_MERGE, w), lambda i, j: (i, 0))
    gate_w = [pl.BlockSpec((None, D, TN_MERGE), lambda i, j, b=b, l=layer: (l, 0, b * nj + j)) for b in range(4)]
    up_w = [pl.BlockSpec((None, None, 512, TN_MERGE), lambda i, j, b=b, l=layer: (l, b, 0, j)) for b in range(4)]
    gate_b = [pl.BlockSpec((None, 1, TN_MERGE), lambda i, j, b=b, l=layer: (l, 0, b * nj + j)) for b in range(4)]
    return pl.pallas_call(
        _merge_body,
        grid=(N // TM_MERGE, nj),
        in_specs=[row(D)] + [row(512)] * 4 + gate_w + up_w + gate_b,
        out_specs=pl.BlockSpec((TM_MERGE, TN_MERGE), lambda i, j: (i, j)),
        out_shape=jax.ShapeDtypeStruct((N, D), BF16),
        compiler_params=_cparams(2), name="merge",
    )(h, *branches, w_mg, w_mg, w_mg, w_mg, w_br, w_br, w_br, w_br, b_mg, b_mg, b_mg, b_mg)


TM_OUT = 640
TN_OUT = 512


def _post_value(t, x, y, gn_ref, gp_ref, gs_ref):
    yn = y * lax.rsqrt(jnp.mean(y * y, axis=-1, keepdims=True) + EPS) * gn_ref[...]
    return x + _mod_rows(t, gp_ref, gs_ref) * yn


def _outproj_body(m_ref, w_ref, *refs):
    gn_ref, gp_ref, gs_ref, o_ref, acc_ref = refs[-5:]
    j = pl.program_id(1)
    acc_ref[j] = jnp.dot(m_ref[...], w_ref[...], preferred_element_type=F32)

    @pl.when(j == D // TN_OUT - 1)
    def _():
        for sidx in range(TM_OUT // SUB):
            rows = pl.ds(sidx * SUB, SUB)
            t = pl.program_id(0) * (TM_OUT // SUB) + sidx
            y = jnp.concatenate([acc_ref[c, rows, :] for c in range(D // TN_OUT)], axis=1)
            o_ref[rows, :] = _post_value(t, _x_rows(refs[:-5], t, rows), y, gn_ref, gp_ref, gs_ref)


def _outproj(merged, w_out, x, g_post, mod, layer):
    mspec = [pl.BlockSpec((None, 8, D), lambda i, j, l=layer: (l, 0, 2)),
             pl.BlockSpec((None, SUB, D), lambda i, j, l=layer: (l, 1, 2))]
    x_specs, x_args = _x_specs(x, TM_OUT, 2)
    return pl.pallas_call(
        _outproj_body,
        grid=(N // TM_OUT, D // TN_OUT),
        in_specs=[pl.BlockSpec((TM_OUT, D), lambda i, j: (i, 0)),
                  pl.BlockSpec((None, D, TN_OUT), lambda i, j, l=layer: (l, 0, j))] + x_specs
        + [pl.BlockSpec((1, D), lambda i, j: (0, 0))] + mspec,
        out_specs=pl.BlockSpec((TM_OUT, D), lambda i, j: (i, 0)),
        out_shape=jax.ShapeDtypeStruct((N, D), F32),
        scratch_shapes=[pltpu.VMEM((D // TN_OUT, TM_OUT, TN_OUT), F32)],
        compiler_params=_cparams(2), name="outproj",
    )(merged, w_out, *x_args, g_post.reshape(1, D), mod, mod)


def _router_body(lg_ref, b_ref, eid_ref, pos_ref, wt_ref, cnt_ref, run_ref):
    i = pl.program_id(0)

    @pl.when(i == 0)
    def _():
        run_ref[...] = jnp.zeros_like(run_ref)

    ng, gs = 8, N_EXPERTS // 8
    neg = -jnp.inf
    scores = jax.nn.sigmoid(lg_ref[...].T[:N_EXPERTS, :])
    sel = scores + b_ref[...]
    sel3 = sel.reshape(ng, gs, SUB)
    sub3 = lax.broadcasted_iota(jnp.int32, (ng, gs, SUB), 1)
    gmax = jnp.max(sel3, axis=1, keepdims=True)
    first = jnp.min(jnp.where(sel3 == gmax, sub3, gs), axis=1, keepdims=True)
    gmax2 = jnp.max(jnp.where(sub3 == first, neg, sel3), axis=1, keepdims=True)
    gscore = (gmax + gmax2).reshape(ng, SUB)
    gidx = lax.broadcasted_iota(jnp.int32, (ng, SUB), 0)
    grank = jnp.zeros((ng, SUB), jnp.int32)
    for s in range(1, ng):
        other = pltpu.roll(gscore, s, 0)
        lower = gidx >= s
        grank += ((other > gscore) | ((other == gscore) & lower)).astype(jnp.int32)
    keep = jnp.broadcast_to((grank < 4).reshape(ng, 1, SUB), (ng, gs, SUB))
    masked = jnp.where(keep, sel3, neg).reshape(N_EXPERTS, SUB)
    eidx = lax.broadcasted_iota(jnp.int32, (N_EXPERTS, SUB), 0)
    rank = jnp.zeros((N_EXPERTS, SUB), jnp.int32)
    for s in range(1, N_EXPERTS):
        other = pltpu.roll(masked, s, 0)
        lower = eidx >= s
        rank += ((other > masked) | ((other == masked) & lower)).astype(jnp.int32)
    chosen = rank < TOP_K
    w_sel = jnp.where(chosen, scores, 0.0)
    w_sel = w_sel / jnp.sum(w_sel, axis=0, keepdims=True) * ROUTED_SCALE
    ri = lax.broadcasted_iota(jnp.int32, (SUB, SUB), 0)
    ci = lax.broadcasted_iota(jnp.int32, (SUB, SUB), 1)
    onehot = chosen.astype(BF16)
    pos = jnp.dot(onehot, (ri < ci).astype(BF16), preferred_element_type=F32) + run_ref[...]
    run_ref[...] = run_ref[...] + jnp.sum(chosen.astype(F32), axis=1, keepdims=True)
    cnt_ref[...] = run_ref[...]
    eidx_f = eidx.astype(F32)
    rows_e, rows_p, rows_w = [], [], []
    for kk in range(TOP_K):
        m = chosen & (rank == kk)
        rows_e.append(jnp.sum(jnp.where(m, eidx_f, 0.0), axis=0, keepdims=True))
        rows_p.append(jnp.sum(jnp.where(m, pos, 0.0), axis=0, keepdims=True))
        rows_w.append(jnp.sum(jnp.where(m, w_sel, 0.0), axis=0, keepdims=True))
    eid_ref[...] = jnp.concatenate(rows_e, axis=0).astype(jnp.int32)
    pos_ref[...] = jnp.concatenate(rows_p, axis=0).astype(jnp.int32)
    wt_ref[...] = jnp.concatenate(rows_w, axis=0)


def _router(logits, rb_col):
    tile = pl.BlockSpec((TOP_K, SUB), lambda i: (0, i))
    return pl.pallas_call(
        _router_body,
        grid=(N // SUB,),
        in_specs=[pl.BlockSpec((SUB, LANES), lambda i: (i, 0)),
                  pl.BlockSpec((N_EXPERTS, SUB), lambda i: (0, 0))],
        out_specs=[tile, tile, tile, pl.BlockSpec((N_EXPERTS, SUB), lambda i: (0, 0))],
        out_shape=[jax.ShapeDtypeStruct((TOP_K, N), jnp.int32), jax.ShapeDtypeStruct((TOP_K, N), jnp.int32),
                   jax.ShapeDtypeStruct((TOP_K, N), F32), jax.ShapeDtypeStruct((N_EXPERTS, SUB), F32)],
        scratch_shapes=[pltpu.VMEM((N_EXPERTS, SUB), F32)],
        compiler_params=_cparams(1), name="router")(logits, rb_col)


SC_CORES, SC_SUBCORES = 2, 16
SC_WORKERS = SC_CORES * SC_SUBCORES
SC_LANES = 16
SC_CHUNK = 16
SC_SCAN = N_ASSIGN // SC_WORKERS


def _sc_mesh():
    return plsc.VectorSubcoreMesh(core_axis_name="c", subcore_axis_name="s",
                                  num_cores=SC_CORES, num_subcores=SC_SUBCORES)


def _sc_worker_base(per_w):
    return (lax.axis_index("s") * SC_CORES + lax.axis_index("c")) * per_w


def _sc_gather_rows(table_hbm, out_hbm, idx_v, rows_v, gsem, wsem, base, per_w):
    n_ch = per_w // SC_CHUNK
    assert n_ch % 2 == 0

    def gather(j, p):
        off = pl.multiple_of(j * SC_CHUNK, SC_CHUNK)
        return pltpu.make_async_copy(table_hbm.at[idx_v.at[pl.ds(off, SC_CHUNK)]], rows_v.at[p], gsem.at[p])

    def write(j, p):
        off = pl.multiple_of(j * SC_CHUNK, SC_CHUNK)
        return pltpu.make_async_copy(rows_v.at[p], out_hbm.at[pl.ds(base + off, SC_CHUNK)], wsem.at[p])

    gather(0, 0).start()

    @pl.loop(0, n_ch, step=2)
    def _(j0):
        for p in range(2):
            j = j0 + p
            gather(j, p).wait()

            @pl.when(j >= 1)
            def _():
                write(j - 1, 1 - p).wait()

            @pl.when(j + 1 < n_ch)
            def _():
                gather(j + 1, 1 - p).start()
            write(j, p).start()

    write(n_ch - 1, 1).wait()


_SC_ROW_SCRATCH = [pltpu.VMEM((2, SC_CHUNK) + ROW_TILE, jnp.int32),
                   pltpu.SemaphoreType.DMA((2,)), pltpu.SemaphoreType.DMA((2,))]


def _sc_gather(table, idx):
    n_out = idx.shape[0]
    per_w = n_out // SC_WORKERS
    assert per_w * SC_WORKERS == n_out and per_w % (2 * SC_CHUNK) == 0

    def body(table_hbm, idx_hbm, out_hbm, idx_v, rows_v, gsem, wsem):
        base = _sc_worker_base(per_w)
        pltpu.sync_copy(idx_hbm.at[pl.ds(base, per_w)], idx_v)
        _sc_gather_rows(table_hbm, out_hbm, idx_v, rows_v, gsem, wsem, base, per_w)

    return pl.kernel(
        body, out_type=jax.ShapeDtypeStruct((n_out,) + ROW_TILE, jnp.int32), mesh=_sc_mesh(),
        scratch_types=[pltpu.VMEM((per_w,), jnp.int32)] + _SC_ROW_SCRATCH, name="sc_gather")(table, idx)


N_PARTS = 3
PART_BLOCKS = N_BLOCKS // N_PARTS
PART_SLOTS = PART_BLOCKS * EXP_BLOCK
assert PART_BLOCKS * N_PARTS == N_BLOCKS


def _sc_dispatch(table, slots, part):
    per_w = PART_SLOTS // SC_WORKERS
    assert per_w * SC_WORKERS == PART_SLOTS and per_w % (2 * SC_CHUNK) == 0 and SC_SCAN % SC_LANES == 0
    assert ZERO_ROWS & (ZERO_ROWS - 1) == 0

    def body(table_hbm, slots_hbm, out_hbm, idx_v, sl_v, rows_v, gsem, wsem):
        local = _sc_worker_base(per_w)
        base = part * PART_SLOTS + local
        lane = lax.iota(jnp.int32, SC_LANES)

        @pl.loop(0, per_w // SC_LANES)
        def _(j):
            off = pl.multiple_of(j * SC_LANES, SC_LANES)
            idx_v[pl.ds(off, SC_LANES)] = N + ((base + off + lane) & (ZERO_ROWS - 1))

        @pl.loop(0, N_ASSIGN // SC_SCAN)
        def _(c):
            pltpu.sync_copy(slots_hbm.at[pl.ds(pl.multiple_of(c * SC_SCAN, 8), SC_SCAN)], sl_v)

            @pl.loop(0, SC_SCAN // SC_LANES)
            def _(j):
                off = pl.multiple_of(j * SC_LANES, SC_LANES)
                loc = sl_v[pl.ds(off, SC_LANES)] - base
                mine = (loc >= 0) & (loc < per_w)
                tok = lax.shift_right_logical(c * SC_SCAN + off + lane, 3)
                plsc.store_scatter(idx_v, [jnp.where(mine, loc, 0)], tok, mask=mine)

        _sc_gather_rows(table_hbm, out_hbm, idx_v, rows_v, gsem, wsem, local, per_w)

    return pl.kernel(
        body, out_type=jax.ShapeDtypeStruct((PART_SLOTS,) + ROW_TILE, jnp.int32), mesh=_sc_mesh(),
        scratch_types=[pltpu.VMEM((per_w,), jnp.int32), pltpu.VMEM((SC_SCAN,), jnp.int32)] + _SC_ROW_SCRATCH,
        compiler_params=pltpu.CompilerParams(needs_layout_passes=False),
        name="sc_dispatch")(table, slots)


def _experts_body(be_ref, first_ref, par_ref, next_ref, nextblk_ref, nused_ref, x_ref, w1_hbm, w3_hbm, w2_hbm,
                  *rest, layer, part):
    y_ref, w1f, w3f, w2f, w1b, w3b, w2b, sem = rest[-8:]
    i = pl.program_id(0)
    b = part * PART_BLOCKS + i
    used = b < nused_ref[0]

    def copies(e, slot):
        return (pltpu.make_async_copy(w1_hbm.at[layer, e], w1f.at[slot], sem.at[0, slot]),
                pltpu.make_async_copy(w3_hbm.at[layer, e], w3f.at[slot], sem.at[1, slot]),
                pltpu.make_async_copy(w2_hbm.at[layer, e], w2f.at[slot], sem.at[2, slot]))

    @pl.when(used & (i == 0))
    def _():
        for c in copies(be_ref[b], par_ref[b]):
            c.start()

    @pl.when(used & ((i == 0) | (first_ref[b] == 1)))
    def _():
        slot = par_ref[b]
        for c in copies(be_ref[b], slot):
            c.wait()

        @pl.when((next_ref[b] >= 0) & (nextblk_ref[b] < (part + 1) * PART_BLOCKS))
        def _():
            for c in copies(next_ref[b], 1 - slot):
                c.start(priority=1)
        w1b[...] = w1f[slot].astype(BF16)
        w3b[...] = w3f[slot].astype(BF16)
        w2b[...] = w2f[slot].astype(BF16)

    @pl.when(used)
    def _():
        lo, hi = _unpack_bf16_pair(_load_row_tiles_2d(x_ref, EXP_BLOCK))
        lo = lo.astype(BF16)
        hi = hi.astype(BF16)
        half = D // 2
        h1 = (jnp.dot(lo, w1b[:half, :], preferred_element_type=F32)
              + jnp.dot(hi, w1b[half:, :], preferred_element_type=F32))
        h3 = (jnp.dot(lo, w3b[:half, :], preferred_element_type=F32)
              + jnp.dot(hi, w3b[half:, :], preferred_element_type=F32))
        hid = (_silu(h1) * h3).astype(BF16)
        y = jnp.dot(hid, w2b[...], preferred_element_type=F32)
        _store_row_tiles_2d(y_ref, _pack_bf16_pair(y[:, :half], y[:, half:]), EXP_BLOCK)

    @pl.when(jnp.logical_not(used))
    def _():
        y_ref[...] = jnp.zeros_like(y_ref)


def _experts(ctl, xs_part, w1, w3, w2, layer, part, ys_prev):
    def x_blk(i, *refs):
        n_here = jnp.clip(refs[-1][0] - part * PART_BLOCKS, 1, PART_BLOCKS)
        return (jnp.minimum(i, n_here - 1), 0)
    any_spec = pl.BlockSpec(memory_space=pl.ANY)
    in_specs = [pl.BlockSpec((EXP_BLOCK * ROW_TILE[0], LANES), x_blk), any_spec, any_spec, any_spec]
    args = [xs_part.reshape(PART_SLOTS * ROW_TILE[0], LANES), w1, w3, w2]
    aliases = {}
    if ys_prev is not None:
        in_specs.append(any_spec)
        args.append(ys_prev)
        aliases = {len(ctl) + 4: 0}
    grid_spec = pltpu.PrefetchScalarGridSpec(
        num_scalar_prefetch=len(ctl),
        grid=(PART_BLOCKS,),
        in_specs=in_specs,
        out_specs=pl.BlockSpec((EXP_BLOCK * ROW_TILE[0], LANES), lambda i, *refs: (part * PART_BLOCKS + i, 0)),
        scratch_shapes=[pltpu.VMEM((2, D, D_EXPERT), F32), pltpu.VMEM((2, D, D_EXPERT), F32),
                        pltpu.VMEM((2, D_EXPERT, D), F32),
                        pltpu.VMEM((D, D_EXPERT), BF16), pltpu.VMEM((D, D_EXPERT), BF16),
                        pltpu.VMEM((D_EXPERT, D), BF16), pltpu.SemaphoreType.DMA((3, 2))])
    return pl.pallas_call(
        functools.partial(_experts_body, layer=layer, part=part), grid_spec=grid_spec,
        out_shape=jax.ShapeDtypeStruct((L_SLOTS * ROW_TILE[0], LANES), jnp.int32),
        input_output_aliases=aliases,
        compiler_params=_cparams(1), name="experts")(*ctl, *args)


TM_FFN = 640


def _ffn_pre_body(x_ref, g_ref, shp_ref, shs_ref, scp_ref, scs_ref, rw_ref, hb_ref, lg_ref, hp_ref):
    i = pl.program_id(0)

    @pl.when(i < N // TM_FFN)
    def _():
        for sidx in range(TM_FFN // SUB):
            rows = pl.ds(sidx * SUB, SUB)
            h = _prenorm_rows(i * (TM_FFN // SUB) + sidx, x_ref[rows, :], g_ref, shp_ref, shs_ref, scp_ref, scs_ref)
            lg_ref[rows, :] = jnp.dot(h, rw_ref[...], precision=HIGHEST, preferred_element_type=F32)
            packed = _pack_bf16_pair(h[:, :D // 2], h[:, D // 2:])
            for c in range(ROW_TILE[0]):
                hp_ref[pl.ds(sidx * SUB * ROW_TILE[0] + c, SUB, stride=ROW_TILE[0]), :] = (
                    packed[:, c * LANES:(c + 1) * LANES])
            hb_ref[rows, :] = h.astype(BF16)

    @pl.when(i >= N // TM_FFN)
    def _():
        hp_ref[...] = jnp.zeros_like(hp_ref)


def _ffn_pre(x, g, mod, layer, rw):
    last = N // TM_FFN - 1
    row = lambda i: (jnp.minimum(i, last), 0)
    mspec = lambda part, rows, blk: pl.BlockSpec((None, rows, D), lambda i, l=layer, p=part, b=blk: (l, b, p))
    return pl.pallas_call(
        _ffn_pre_body,
        grid=(N // TM_FFN + 1,),
        in_specs=[pl.BlockSpec((TM_FFN, D), row), pl.BlockSpec((1, D), lambda i: (0, 0)),
                  mspec(3, 8, 0), mspec(3, SUB, 1), mspec(4, 8, 0), mspec(4, SUB, 1),
                  pl.BlockSpec((None, D, LANES), lambda i, l=layer: (l, 0, 0))],
        out_specs=[pl.BlockSpec((TM_FFN, D), row), pl.BlockSpec((TM_FFN, LANES), row),
                   pl.BlockSpec((TM_FFN * ROW_TILE[0], LANES), lambda i: (i, 0))],
        out_shape=[jax.ShapeDtypeStruct((N, D), BF16), jax.ShapeDtypeStruct((N, LANES), F32),
                   jax.ShapeDtypeStruct(((N + TM_FFN) * ROW_TILE[0], LANES), jnp.int32)],
        compiler_params=_cparams(1), name="ffn_pre")(x, g.reshape(1, D), mod, mod, mod, mod, rw)


TM_SHARED = 640


def _shared_body(h_ref, w13_ref, w2_ref, after_ref, o_ref):
    up = jnp.dot(h_ref[...], w13_ref[...], preferred_element_type=F32)
    hid = (_silu(up[:, :D_EXPERT]) * up[:, D_EXPERT:]).astype(BF16)
    o_ref[...] = jnp.dot(hid, w2_ref[...], preferred_element_type=F32)


def _shared(h, w13, w2, layer, after):
    return pl.pallas_call(
        _shared_body,
        grid=(N // TM_SHARED,),
        in_specs=[pl.BlockSpec((TM_SHARED, D), lambda i: (i, 0)),
                  pl.BlockSpec((None, D, 2 * D_EXPERT), lambda i, l=layer: (l, 0, 0)),
                  pl.BlockSpec((None, D_EXPERT, D), lambda i, l=layer: (l, 0, 0)),
                  pl.BlockSpec(memory_space=pl.ANY)],
        out_specs=pl.BlockSpec((TM_SHARED, D), lambda i: (i, 0)),
        out_shape=jax.ShapeDtypeStruct((N, D), F32),
        compiler_params=_cparams(1), name="shared")(h, w13, w2, after)


def _combine_body(g_ref, wt_ref, sh_ref, x_ref, gn_ref, gp_ref, gs_ref, *rest, tile0, n_out, final):
    outs = rest[-n_out:]
    half = D // 2
    acc_lo = sh_ref[:, :half]
    acc_hi = sh_ref[:, half:]
    wt = wt_ref[...]
    per_tok = TOP_K * ROW_TILE[0]
    for k in range(TOP_K):
        packed = jnp.concatenate([g_ref[pl.ds(k * ROW_TILE[0] + c, SUB, stride=per_tok), :]
                                  for c in range(ROW_TILE[0])], axis=1)
        lo, hi = _unpack_bf16_pair(packed)
        w_c = wt[:, k:k + 1]
        acc_lo = acc_lo + w_c * lo
        acc_hi = acc_hi + w_c * hi
    t = tile0 + pl.program_id(0)
    val = _post_value(t, x_ref[...], jnp.concatenate([acc_lo, acc_hi], axis=1), gn_ref, gp_ref, gs_ref)
    if not final:
        outs[0][...] = val
    else:
        @pl.when(t < N_P // SUB)
        def _():
            outs[0][...] = val
        if n_out == 2:
            @pl.when(t >= N_P // SUB)
            def _():
                outs[1][...] = val


def _combine(gathered, wts, shared, x, g_post, mod, layer, tile0, n_tiles, out_prev, final):
    per_tok = TOP_K * ROW_TILE[0]
    row = lambda i: (tile0 + i, 0)
    in_specs = [pl.BlockSpec((SUB * per_tok, LANES), lambda i: (i, 0)),
                pl.BlockSpec((SUB, LANES), row), pl.BlockSpec((SUB, D), row), pl.BlockSpec((SUB, D), row),
                pl.BlockSpec((1, D), lambda i: (0, 0)),
                pl.BlockSpec((None, 8, D), lambda i, l=layer: (l, 0, 5)),
                pl.BlockSpec((None, SUB, D), lambda i, l=layer: (l, 1, 5))]
    args = [gathered.reshape(n_tiles * SUB * per_tok, LANES), wts, shared, x, g_post.reshape(1, D), mod, mod]
    aliases = {}
    if out_prev is not None:
        in_specs.append(pl.BlockSpec(memory_space=pl.ANY))
        args.append(out_prev)
        aliases = {len(args) - 1: 0}
    if not final:
        out_specs = [pl.BlockSpec((SUB, D), row)]
        out_shape = [jax.ShapeDtypeStruct((N, D), F32)]
    else:
        last_p = N_P // SUB - 1
        out_specs = [pl.BlockSpec((SUB, D), lambda i: (jnp.minimum(tile0 + i, last_p), 0))]
        out_shape = [jax.ShapeDtypeStruct((N_P, D), F32)]
        if tile0 + n_tiles > N_P // SUB:
            out_specs.append(pl.BlockSpec((N_S, D), lambda i: (0, 0)))
            out_shape.append(jax.ShapeDtypeStruct((N_S, D), F32))
    return pl.pallas_call(
        functools.partial(_combine_body, tile0=tile0, n_out=len(out_shape), final=final),
        grid=(n_tiles,),
        in_specs=in_specs, out_specs=out_specs, out_shape=out_shape,
        input_output_aliases=aliases,
        compiler_params=_cparams(1), name="combine")(*args)


def _slots_body(start_ref, eid_ref, pos_ref, o_ref):
    eid = eid_ref[...]
    acc = pos_ref[...]
    for e in range(N_EXPERTS):
        acc = acc + jnp.where(eid == e, start_ref[e], 0)
    o_ref[...] = acc


def _slots(pad_start, eid, pos):
    grid_spec = pltpu.PrefetchScalarGridSpec(
        num_scalar_prefetch=1, grid=(1,),
        in_specs=[pl.BlockSpec((TOP_K, N), lambda i, s: (0, 0)), pl.BlockSpec((TOP_K, N), lambda i, s: (0, 0))],
        out_specs=pl.BlockSpec((TOP_K, N), lambda i, s: (0, 0)))
    return pl.pallas_call(_slots_body, grid_spec=grid_spec,
                          out_shape=jax.ShapeDtypeStruct((TOP_K, N), jnp.int32),
                          compiler_params=_cparams(1), name="slots")(pad_start, eid, pos)


def _put_sample_rows_body(*refs):
    n = len(refs) // 3
    for src, dst in zip(refs[:n], refs[2 * n:]):
        dst[...] = src[...].astype(BF16)


def _put_sample_rows(sample_rows, full):
    n = len(full)
    return pl.pallas_call(
        _put_sample_rows_body,
        grid=(1,),
        in_specs=[pl.BlockSpec((N_S, 512), lambda i: (0, 0))] * n + [pl.BlockSpec(memory_space=pl.ANY)] * n,
        out_specs=[pl.BlockSpec((N_S, 512), lambda i: (N_P // N_S, 0))] * n,
        out_shape=[jax.ShapeDtypeStruct((N, 512), BF16)] * n,
        input_output_aliases={n + k: k for k in range(n)},
        compiler_params=_cparams(1), name="put_sample_rows")(*sample_rows, *full)


COMBINE_RANGES = ((0, 33), (33, 32))

def _prepare_weights(w_in, w_merge_gate, w_branch, w_out, router_w, shared_w1, shared_w3, shared_w2):
    return dict(
        w_main=jnp.concatenate([w_in[:, :, :1536], w_in[:, :, 1552:]], axis=2).astype(BF16),
        w_low=jnp.pad(w_in[:, :, 1536:1552], ((0, 0), (0, 0), (0, LANES - 16))).astype(BF16),
        w_mg=w_merge_gate.astype(BF16), w_br=w_branch.astype(BF16), w_out=w_out.astype(BF16),
        rw=jnp.pad(router_w, ((0, 0), (0, 0), (0, LANES - N_EXPERTS))),
        w13=jnp.concatenate([shared_w1, shared_w3], axis=2).astype(BF16), sw2=shared_w2.astype(BF16))


def _rope_tables(pos):
    half = DK // 2
    inv = ROPE_BASE ** (-jnp.arange(half, dtype=F32) / half)
    ang = pos.astype(F32)[:, None] * inv[None, :]
    cos = jnp.cos(ang)
    sin = jnp.sin(ang)
    return jnp.concatenate([cos, cos], axis=1), jnp.concatenate([-sin, sin], axis=1)


def _layer(l, x, mod, s_gla, s_pool, s_ret, wts, prep, final, prev_gla, prev_ret):
    (norm_mix_pre, norm_mix_post, norm_ffn_pre, norm_ffn_post, w_in, w_gla_gate, b_gla_gate, gla_norm,
     pool_w, pool_scale, ret_norm, sgu_norm, sgu_w, sgu_b, w_branch, w_merge_gate, b_merge_gate, w_out,
     router_w, router_bias, expert_w1, expert_w3, expert_w2, shared_w1, shared_w3, shared_w2) = wts

    h = _prenorm(x, norm_mix_pre[l], mod, l)
    p_main = _matmul(h, prep["w_main"], l, 1664, 512, name="inproj")
    p_low = _matmul(h, prep["w_low"], l, 1664, LANES, name="inproj_low")

    w_gate_pad = jnp.pad(w_gla_gate[l], ((0, LANES - 16), (0, 0)))
    b_gate = b_gla_gate[l].reshape(1, HEADS * DK)
    log_gamma = jnp.log1p(-jnp.exp2(-5.0 - jnp.arange(HEADS, dtype=F32)))
    dec_row = jnp.repeat(log_gamma, DK).reshape(1, HEADS * DK)
    cos_p, sin_p = _rope_tables(jnp.arange(T_P))
    cos_p = jnp.tile(cos_p, (1, 2))
    sin_p = jnp.tile(sin_p, (1, 2))
    g_gla = gla_norm[l].reshape(1, HEADS * DV)
    g_ret = ret_norm[l].reshape(1, HEADS * DV)

    oa_p, gla_p = _la_prompt(p_main, C_GQ, C_GK, C_GV, C_GR, p_low, p_low, w_gate_pad, b_gate, g_gla, False)
    oc_p, ret_p = _la_prompt(p_main, C_RQ, C_RK, C_RV, C_RG, cos_p, sin_p, dec_row, b_gate, g_ret, True)
    pw_bf = pool_w[l].astype(BF16)
    pscale = pool_scale[l].reshape(1, 512)
    ob_p = _pool_prompt(p_main, pw_bf, pscale)
    sgu_g = sgu_norm[l].reshape(1, 512)
    od_p = _sgu_prompt(p_main, sgu_g, sgu_w[l], jnp.pad(sgu_b[l].T, ((0, 0), (0, LANES - 4))))

    ps = p_main[N_P:]
    q_t = _to_tiles_t(ps[:, C_GQ:C_GQ + 256])
    k_t = _to_tiles_t(ps[:, C_GK:C_GK + 256])
    glow_t = jnp.pad(_to_tiles_t(p_low[N_P:, :16]), ((0, 0), (0, LANES - 16), (0, 0)))
    w_gate_t = jnp.pad(w_gla_gate[l].T, ((0, 0), (0, LANES - 16)))
    b_col = jnp.broadcast_to(b_gla_gate[l][:, None], (HEADS * DK, LANES))
    logit_t = _gate_logits_t(w_gate_t, glow_t, b_col)
    dummy = jnp.zeros((HEADS * DK, LANES), F32)
    oa_s, gla_s = _la_sample(q_t, k_t, logit_t, dummy, dummy, p_main, C_GV, C_GR, g_gla, s_gla, l, prev_gla, False)
    cos_s, sin_s = _rope_tables(jnp.full((1,), PAST_LEN))
    cos_c = jnp.broadcast_to(jnp.tile(cos_s[0], HEADS)[:, None], (HEADS * DK, LANES))
    sin_c = jnp.broadcast_to(jnp.tile(sin_s[0], HEADS)[:, None], (HEADS * DK, LANES))
    dec_c = jnp.broadcast_to(jnp.repeat(log_gamma, DK)[:, None], (HEADS * DK, LANES))
    rq_t = _to_tiles_t(ps[:, C_RQ:C_RQ + 256])
    rk_t = _to_tiles_t(ps[:, C_RK:C_RK + 256])
    oc_s, ret_s = _la_sample(rq_t, rk_t, dec_c, cos_c, sin_c, p_main, C_RV, C_RG, g_ret, s_ret, l, prev_ret, True)
    sgu_w0 = jnp.repeat(sgu_w[l][:, 0, 0], LANES).reshape(1, 512)
    sgu_b0 = jnp.repeat(sgu_b[l][:, 0], LANES).reshape(1, 512)
    ob_s, od_s, vn_s = _small_sample(p_main, s_pool[l], pw_bf, pscale, sgu_g, sgu_w0, sgu_b0)
    pool_p = jnp.stack([p_main[(b + 1) * T_P - POOL_BUF:(b + 1) * T_P, C_PIN:C_PIN + 512] for b in range(B_P)])
    pool_s = jnp.concatenate([s_pool[l][:, 1:], ps[:, None, C_PIN:C_PIN + 512]], axis=1)

    branches = _put_sample_rows([oa_s, ob_s, oc_s, od_s], [oa_p, ob_p, oc_p, od_p])
    merged = _merge(h, branches, prep["w_mg"], b_merge_gate.reshape(DEPTH, 1, 4 * D), prep["w_br"], l)
    x = _outproj(merged, prep["w_out"], x, norm_mix_post[l], mod, l)

    rb = jnp.broadcast_to(router_bias[l][:, None], (N_EXPERTS, SUB))
    h2, logits, h2_packed = _ffn_pre(x, norm_ffn_pre[l], mod, l, prep["rw"])
    eid, pos, wt, counts = _router(logits, rb)
    counts = counts[:, 0].astype(jnp.int32)
    padded = (counts + EXP_BLOCK - 1) // EXP_BLOCK * EXP_BLOCK
    pad_end = jnp.cumsum(padded)
    pad_start = pad_end - padded
    nused = (pad_end[-1] // EXP_BLOCK).astype(jnp.int32).reshape(1)
    blk_row = jnp.arange(N_BLOCKS, dtype=jnp.int32) * EXP_BLOCK
    block_e = jnp.minimum(jnp.sum((blk_row[:, None] >= pad_end[None, :]).astype(jnp.int32), axis=1),
                          N_EXPERTS - 1)
    first = jnp.concatenate([jnp.ones((1,), jnp.int32), (block_e[1:] != block_e[:-1]).astype(jnp.int32)])
    first = jnp.where(blk_row < pad_end[-1], first, 0)
    par = (jnp.cumsum(first) - 1) % 2
    live = jnp.where(padded > 0, jnp.arange(N_EXPERTS), N_EXPERTS)
    after = jnp.concatenate([lax.cummin(live, reverse=True)[1:], jnp.full((1,), N_EXPERTS)])
    of_block = block_e[:, None] == jnp.arange(N_EXPERTS)
    next_e = jnp.sum(jnp.where(of_block, jnp.where(after < N_EXPERTS, after, -1), 0), axis=1).astype(jnp.int32)
    next_blk = jnp.sum(jnp.where(of_block, pad_end // EXP_BLOCK, 0), axis=1).astype(jnp.int32)
    slots = _slots(pad_start.astype(jnp.int32), eid, pos).T.reshape(N_ASSIGN)
    wt = jnp.pad(wt.T, ((0, 0), (0, LANES - TOP_K)))
    table = h2_packed.reshape((N + TM_FFN,) + ROW_TILE)
    ctl = (block_e, first, par.astype(jnp.int32), next_e, next_blk, nused)
    xs_parts = [_sc_dispatch(table, slots, part) for part in range(N_PARTS)]
    shared = _shared(h2, prep["w13"], prep["sw2"], l, slots)
    ys = None
    for part in range(N_PARTS):
        ys = _experts(ctl, xs_parts[part], expert_w1, expert_w3, expert_w2, l, part, ys)
    ys = ys.reshape((L_SLOTS,) + ROW_TILE)
    outs = [None]
    for tile0, n_tiles in COMBINE_RANGES:
        a0, a1 = tile0 * SUB * TOP_K, (tile0 + n_tiles) * SUB * TOP_K
        outs = _combine(_sc_gather(ys, slots[a0:a1]), wt, shared, x, norm_ffn_post[l], mod, l, tile0, n_tiles,
                        outs[0], final)
    x = tuple(outs) if final else outs[0]
    return x, (gla_p, pool_p, pool_s, ret_p, vn_s), gla_s, ret_s


def kernel(x_prompt, x_sample, c_prompt, c_sample, state_gla, state_pool, state_ret, w_ada, b_ada, norm_mix_pre, norm_mix_post, norm_ffn_pre, norm_ffn_post, w_in, w_gla_gate, b_gla_gate, gla_norm, pool_w, pool_scale, ret_norm, sgu_norm, sgu_w, sgu_b, w_branch, w_merge_gate, b_merge_gate, w_out, router_w, router_bias, expert_w1, expert_w3, expert_w2, shared_w1, shared_w3, shared_w2):
    wts = (norm_mix_pre, norm_mix_post, norm_ffn_pre, norm_ffn_post, w_in, w_gla_gate, b_gla_gate, gla_norm,
           pool_w, pool_scale, ret_norm, sgu_norm, sgu_w, sgu_b, w_branch, w_merge_gate, b_merge_gate, w_out,
           router_w, router_bias, expert_w1, expert_w3, expert_w2, shared_w1, shared_w3, shared_w2)
    c_all = jnp.zeros((MOD_ROWS, D), F32).at[:B_P].set(c_prompt).at[SUB:SUB + N_S].set(c_sample)
    mod = _ada(c_all, w_ada, b_ada)
    x = (x_prompt.reshape(N_P, D), x_sample.reshape(N_S, D))
    prep = _prepare_weights(w_in, w_merge_gate, w_branch, w_out, router_w, shared_w1, shared_w3, shared_w2)
    per_layer = []
    gla_s = ret_s = None
    for l in range(DEPTH):
        x, states, gla_s, ret_s = _layer(l, x, mod, state_gla, state_pool, state_ret, wts, prep, l == DEPTH - 1,
                                         gla_s, ret_s)
        per_layer.append(states)
    gla_p, pool_p, pool_s, ret_p, vn_s = (jnp.stack(z) for z in zip(*per_layer))
    return (x[0].reshape(B_P, T_P, D), x[1].reshape(N_S, 1, D),
            gla_p, gla_s, pool_p, pool_s, ret_p, ret_s, vn_s.reshape(DEPTH, N_S, 1, 512))
```

```python
import functools

import jax
import jax.numpy as jnp
from jax import lax
from jax.experimental import pallas as pl
from jax.experimental.pallas import tpu as pltpu
from jax.experimental.pallas import tpu_sc as plsc

F32 = jnp.float32
BF16 = jnp.bfloat16
HIGHEST = lax.Precision.HIGHEST

D = 2048
B_P, T_P = 4, 2048
N_P = B_P * T_P
N_S = 128
N = N_P + N_S
DEPTH = 2
PAST_LEN = 16384
EPS = 1e-6
HEADS, DK, DV = 4, 64, 128
CHUNK = 64
GATE_TEMP = 16.0
POOL_WINDOWS = (2, 4, 8, 16)
POOL_BUF = 15
ROPE_BASE = 10000.0
N_EXPERTS = 64
TOP_K = 8
D_EXPERT = 512
ROUTED_SCALE = 2.5

LANES = 128
SUB = 128
MOD_ROWS = 256
EXP_BLOCK = 256
N_ASSIGN = N * TOP_K
N_BLOCKS = -(-(N_ASSIGN + N_EXPERTS * (EXP_BLOCK - 1)) // EXP_BLOCK)
L_SLOTS = N_BLOCKS * EXP_BLOCK
VMEM_LIMIT = 56 * 1024 * 1024

C_GQ, C_GK, C_GV, C_GR, C_PIN, C_RQ, C_RK, C_RV, C_RG, C_SU, C_SV = (
    0, 256, 512, 1024, 1536, 2048, 2304, 2560, 3072, 3584, 4096)
P_MAIN = 4608


def _cparams(n_axes=1):
    return pltpu.CompilerParams(dimension_semantics=("arbitrary",) * n_axes,
                                vmem_limit_bytes=VMEM_LIMIT)


def _silu(x):
    return x * jax.nn.sigmoid(x)


def _mod_rows(t, mp_ref, ms_ref):
    b = jnp.minimum(t // (T_P // SUB), B_P - 1)
    return jnp.where(t >= N_P // SUB, ms_ref[...], mp_ref[pl.ds(b, 1), :])


def _mod_specs(layer, part):
    return [pl.BlockSpec((None, 8, D), lambda i, l=layer, p=part: (l, 0, p)),
            pl.BlockSpec((None, SUB, D), lambda i, l=layer, p=part: (l, 1, p))]


def _pack_bf16_pair(lo, hi):
    lo_u = lax.bitcast_convert_type(lo.astype(BF16).astype(F32), jnp.uint32)
    hi_u = lax.bitcast_convert_type(hi.astype(BF16).astype(F32), jnp.uint32)
    return lax.bitcast_convert_type((hi_u & jnp.uint32(0xFFFF0000)) | (lo_u >> 16), jnp.int32)


def _unpack_bf16_pair(w):
    u = lax.bitcast_convert_type(w, jnp.uint32)
    lo = lax.bitcast_convert_type(u << 16, F32)
    hi = lax.bitcast_convert_type(u & jnp.uint32(0xFFFF0000), F32)
    return lo, hi


ROW_TILE = (8, LANES)


def _load_row_tiles(ref):
    return jnp.concatenate([ref[:, c, :] for c in range(ROW_TILE[0])], axis=1)


def _store_row_tiles(ref, val):
    for c in range(ROW_TILE[0]):
        ref[:, c, :] = val[:, c * LANES:(c + 1) * LANES]


def _load_row_tiles_2d(ref, rows):
    return jnp.concatenate([ref[pl.ds(c, rows, stride=ROW_TILE[0]), :] for c in range(ROW_TILE[0])], axis=1)


def _store_row_tiles_2d(ref, val, rows):
    for c in range(ROW_TILE[0]):
        ref[pl.ds(c, rows, stride=ROW_TILE[0]), :] = val[:, c * LANES:(c + 1) * LANES]


def _ada_body(c_ref, w_ref, b_ref, o_ref):
    s = _silu(c_ref[...]).astype(BF16)
    o_ref[...] = jnp.dot(s, w_ref[...].astype(BF16), preferred_element_type=F32) + b_ref[...]


def _ada(c_all, w_ada, b_ada):
    tn = 1024
    return pl.pallas_call(
        _ada_body,
        grid=(DEPTH, 6 * D // tn),
        in_specs=[pl.BlockSpec((MOD_ROWS, D), lambda l, j: (0, 0)),
                  pl.BlockSpec((None, D, tn), lambda l, j: (l, 0, j)),
                  pl.BlockSpec((None, 1, tn), lambda l, j: (l, 0, j))],
        out_specs=pl.BlockSpec((None, MOD_ROWS, tn), lambda l, j: (l, 0, j)),
        out_shape=jax.ShapeDtypeStruct((DEPTH, MOD_ROWS, 6 * D), F32),
        compiler_params=_cparams(2), name="ada")(c_all, w_ada, b_ada.reshape(DEPTH, 1, 6 * D))


ZERO_ROWS = 2 * SUB


def _x_specs(x, tm, n_axes):
    row = (lambda i: (i, 0)) if n_axes == 1 else (lambda i, j: (i, 0))
    if not isinstance(x, tuple):
        return [pl.BlockSpec((tm, D), row)], [x]
    zero = (lambda i: (0, 0)) if n_axes == 1 else (lambda i, j: (0, 0))
    return [pl.BlockSpec((tm, D), row), pl.BlockSpec((N_S, D), zero)], list(x)


def _x_rows(x_refs, t, rows):
    if len(x_refs) == 1:
        return x_refs[0][rows, :]
    return jnp.where(t >= N_P // SUB, x_refs[1][...], x_refs[0][rows, :])


def _prenorm_rows(t, x, g_ref, shp_ref, shs_ref, scp_ref, scs_ref):
    y = x * lax.rsqrt(jnp.mean(x * x, axis=-1, keepdims=True) + EPS) * g_ref[...]
    return y * (1.0 + _mod_rows(t, scp_ref, scs_ref)) + _mod_rows(t, shp_ref, shs_ref)


TM_NORM = 640


def _prenorm_body(*refs):
    g_ref, shp_ref, shs_ref, scp_ref, scs_ref, h_ref = refs[-6:]
    for sidx in range(TM_NORM // SUB):
        rows = pl.ds(sidx * SUB, SUB)
        t = pl.program_id(0) * (TM_NORM // SUB) + sidx
        h_ref[rows, :] = _prenorm_rows(t, _x_rows(refs[:-6], t, rows), g_ref, shp_ref, shs_ref, scp_ref,
                                       scs_ref).astype(BF16)


def _prenorm(x, g, mod, layer):
    x_specs, x_args = _x_specs(x, TM_NORM, 1)
    return pl.pallas_call(
        _prenorm_body,
        grid=(N // TM_NORM,),
        in_specs=x_specs + [pl.BlockSpec((1, D), lambda i: (0, 0))] + _mod_specs(layer, 0) + _mod_specs(layer, 1),
        out_specs=pl.BlockSpec((TM_NORM, D), lambda i: (i, 0)),
        out_shape=jax.ShapeDtypeStruct((N, D), BF16),
        compiler_params=_cparams(1), name="prenorm")(*x_args, g.reshape(1, D), mod, mod, mod, mod)


def _mm_body(x_ref, w_ref, o_ref):
    o_ref[...] = jnp.dot(x_ref[...], w_ref[...], preferred_element_type=F32).astype(o_ref.dtype)


def _matmul(x, w_all, layer, tm, tn, out_dtype=F32, name="mm"):
    m, k = x.shape
    n = w_all.shape[2]
    return pl.pallas_call(
        _mm_body,
        grid=(m // tm, n // tn),
        in_specs=[pl.BlockSpec((tm, k), lambda i, j: (i, 0)),
                  pl.BlockSpec((None, k, tn), lambda i, j, l=layer: (l, 0, j))],
        out_specs=pl.BlockSpec((tm, tn), lambda i, j: (i, j)),
        out_shape=jax.ShapeDtypeStruct((m, n), out_dtype),
        compiler_params=_cparams(2), name=name)(x, w_all)


ROWS_LA = 256


def _swap_halves_lanes(x):
    lane = lax.broadcasted_iota(jnp.int32, x.shape, 1)
    return jnp.where((lane % 64) < 32, pltpu.roll(x, 96, 1), pltpu.roll(x, 32, 1))


def _rope_lanes(x, cos, sin_signed):
    parts = []
    for half in range(2):
        xh = x[:, half * LANES:(half + 1) * LANES]
        parts.append(xh * cos + _swap_halves_lanes(xh) * sin_signed)
    return jnp.concatenate(parts, axis=1)


def _la_prompt_body(q_ref, k_ref, v_ref, r_ref, aux_ref, aux2_ref, dec_ref, bias_ref, g_ref,
                    o_ref, st_out_ref, st_ref, *, retention):
    t = pl.program_id(1)

    @pl.when(t == 0)
    def _():
        st_ref[...] = jnp.zeros_like(st_ref)

    n_ch = ROWS_LA // CHUNK
    ri = lax.broadcasted_iota(jnp.int32, (ROWS_LA, ROWS_LA), 0)
    ci = lax.broadcasted_iota(jnp.int32, (ROWS_LA, ROWS_LA), 1)
    causal = (ri >= ci) & ((ri // CHUNK) == (ci // CHUNK))
    scale = DK ** -0.5

    q = q_ref[...]
    k = k_ref[...]
    v = v_ref[...]
    if retention:
        cos = aux_ref[...]
        sin = aux2_ref[...]
        q = _rope_lanes(q, cos, sin)
        k = _rope_lanes(k, cos, sin) * scale
        step = (lax.broadcasted_iota(jnp.int32, (ROWS_LA, HEADS * DK), 0) % CHUNK + 1).astype(F32)
        bc = step * dec_ref[...]
    else:
        q = q * scale
        logit = jnp.dot(aux_ref[...], dec_ref[...], precision=HIGHEST, preferred_element_type=F32) + bias_ref[...]
        la = jax.nn.log_sigmoid(logit) / GATE_TEMP
        bc = jnp.dot(causal.astype(F32), la, precision=HIGHEST, preferred_element_type=F32)
    bl = bc.reshape(n_ch, CHUNK, HEADS * DK)[:, CHUNK - 1:CHUNK, :]
    bl_rows = jnp.broadcast_to(bl, (n_ch, CHUNK, HEADS * DK)).reshape(ROWS_LA, HEADS * DK)
    qd = q * jnp.exp(bc)
    ki = k * jnp.exp(-bc)
    ke = k * jnp.exp(bl_rows - bc)
    ac = jnp.exp(bl)
    outs = []
    for h in range(HEADS):
        ks = slice(h * DK, (h + 1) * DK)
        vs = slice(h * DV, (h + 1) * DV)
        qd_h = qd[:, ks].astype(BF16)
        ki_h = ki[:, ks].astype(BF16)
        ke_h = ke[:, ks].astype(BF16)
        v_h = v[:, vs].astype(BF16)
        sc = lax.dot_general(qd_h, ki_h, (((1,), (1,)), ((), ())), preferred_element_type=F32)
        sc = jnp.where(causal, sc, 0.0)
        o_h = jnp.dot(sc.astype(BF16), v_h, preferred_element_type=F32)
        inter = []
        for c in range(n_ch):
            rows = slice(c * CHUNK, (c + 1) * CHUNK)
            st = st_ref[h]
            inter.append(lax.dot_general(qd_h[rows], st.astype(BF16), (((1,), (1,)), ((), ())),
                                         preferred_element_type=F32))
            kv_t = lax.dot_general(v_h[rows], ke_h[rows], (((0,), (0,)), ((), ())), preferred_element_type=F32)
            st_ref[h] = st * ac[c][:, ks] + kv_t
        o_h = o_h + jnp.concatenate(inter, axis=0)
        outs.append(o_h * lax.rsqrt(jnp.mean(o_h * o_h, axis=-1, keepdims=True) + EPS) * g_ref[:, vs])
    o_ref[...] = (jnp.concatenate(outs, axis=1) * _silu(r_ref[...])).astype(BF16)
    st_out_ref[...] = st_ref[...]


def _la_prompt(p_main, cq, ck, cv, cr, aux, aux2, dec, bias, g, retention):
    nt = T_P // ROWS_LA
    rowblk = lambda b, t: b * nt + t
    if retention:
        aux_specs = [pl.BlockSpec((ROWS_LA, LANES), lambda b, t: (t, 0)),
                     pl.BlockSpec((ROWS_LA, LANES), lambda b, t: (t, 0))]
    else:
        aux_specs = [pl.BlockSpec((ROWS_LA, LANES), lambda b, t: (rowblk(b, t), 0)),
                     pl.BlockSpec((8, LANES), lambda b, t: (0, 0))]
    o, st = pl.pallas_call(
        functools.partial(_la_prompt_body, retention=retention),
        grid=(B_P, nt),
        in_specs=[pl.BlockSpec((ROWS_LA, 256), lambda b, t: (rowblk(b, t), cq // 256)),
                  pl.BlockSpec((ROWS_LA, 256), lambda b, t: (rowblk(b, t), ck // 256)),
                  pl.BlockSpec((ROWS_LA, 512), lambda b, t: (rowblk(b, t), cv // 512)),
                  pl.BlockSpec((ROWS_LA, 512), lambda b, t: (rowblk(b, t), cr // 512))]
        + aux_specs
        + [pl.BlockSpec(dec.shape, lambda b, t: (0, 0)),
           pl.BlockSpec((1, HEADS * DK), lambda b, t: (0, 0)),
           pl.BlockSpec((1, HEADS * DV), lambda b, t: (0, 0))],
        out_specs=[pl.BlockSpec((ROWS_LA, HEADS * DV), lambda b, t: (rowblk(b, t), 0)),
                   pl.BlockSpec((None, HEADS, DV, DK), lambda b, t: (b, 0, 0, 0))],
        out_shape=[jax.ShapeDtypeStruct((N, HEADS * DV), BF16),
                   jax.ShapeDtypeStruct((B_P, HEADS, DV, DK), F32)],
        scratch_shapes=[pltpu.VMEM((HEADS, DV, DK), F32)],
        compiler_params=_cparams(2), name="ret_prompt" if retention else "gla_prompt",
    )(p_main, p_main, p_main, p_main, aux, aux2, dec, bias, g)
    return o, jnp.swapaxes(st, -1, -2)


SAMPLE_TILE = 8


def _la_sample_body(qt_ref, kt_ref, lt_ref, cos_ref, sin_ref, v_ref, r_ref, g_ref, s_ref, *rest, retention):
    o_ref, s_out_ref = rest[-2:]
    scale = DK ** -0.5
    qt = qt_ref[...]
    kt = kt_ref[...]
    if retention:
        def rope(x):
            sw = jnp.concatenate(
                [x[h * DK + (DK // 2) * (1 - j): h * DK + (DK // 2) * (2 - j), :]
                 for h in range(HEADS) for j in range(2)], axis=0)
            return x * cos_ref[...] + sw * sin_ref[...]
        qt = rope(qt)
        kt = rope(kt) * scale
        la = lt_ref[...]
    else:
        qt = qt * scale
        la = jax.nn.log_sigmoid(lt_ref[...]) / GATE_TEMP
    at = jnp.exp(la)
    qd = qt * at
    ki = kt * jnp.exp(-la)
    prod = qd * ki
    v8 = v_ref[...]
    r8 = r_ref[...]
    g = g_ref[...]
    for j in range(SAMPLE_TILE):
        for h in range(HEADS):
            ks = slice(h * DK, (h + 1) * DK)
            vs = slice(h * DV, (h + 1) * DV)
            a_c = jnp.broadcast_to(at[ks, j:j + 1], (DK, DV))
            k_c = jnp.broadcast_to(kt[ks, j:j + 1], (DK, DV))
            q_c = jnp.broadcast_to(qd[ks, j:j + 1], (DK, DV))
            s_c = jnp.broadcast_to(jnp.sum(prod[ks, j:j + 1], axis=0, keepdims=True), (1, DV))
            s0 = s_ref[j, h]
            v_row = v8[j:j + 1, vs]
            s_out_ref[j, h] = a_c * s0 + k_c * v_row
            o_row = s_c * v_row + jnp.sum(q_c * s0, axis=0, keepdims=True)
            o_n = o_row * lax.rsqrt(jnp.mean(o_row * o_row, axis=-1, keepdims=True) + EPS) * g[:, vs]
            o_ref[j:j + 1, vs] = o_n * _silu(r8[j:j + 1, vs])


def _la_sample(qt, kt, lt, cos_t, sin_t, p_main, cv, cr, g, s0_all, layer, s_prev, retention):
    nt = N_S // SAMPLE_TILE
    row0 = N_P // SAMPLE_TILE
    tile = pl.BlockSpec((None, HEADS * DK, LANES), lambda i: (i, 0, 0))
    full = pl.BlockSpec((HEADS * DK, LANES), lambda i: (0, 0))
    lt_spec = full if retention else tile
    return pl.pallas_call(
        functools.partial(_la_sample_body, retention=retention),
        grid=(nt,),
        in_specs=[tile, tile, lt_spec, full, full,
                  pl.BlockSpec((SAMPLE_TILE, 512), lambda i: (row0 + i, cv // 512)),
                  pl.BlockSpec((SAMPLE_TILE, 512), lambda i: (row0 + i, cr // 512)),
                  pl.BlockSpec((1, HEADS * DV), lambda i: (0, 0)),
                  pl.BlockSpec((None, SAMPLE_TILE, HEADS, DK, DV), lambda i, l=layer: (l, i, 0, 0, 0))]
        + ([] if s_prev is None else [pl.BlockSpec(memory_space=pl.ANY)]),
        out_specs=[pl.BlockSpec((SAMPLE_TILE, HEADS * DV), lambda i: (i, 0)),
                   pl.BlockSpec((None, SAMPLE_TILE, HEADS, DK, DV), lambda i, l=layer: (l, i, 0, 0, 0))],
        out_shape=[jax.ShapeDtypeStruct((N_S, HEADS * DV), F32),
                   jax.ShapeDtypeStruct((DEPTH, N_S, HEADS, DK, DV), F32)],
        input_output_aliases={} if s_prev is None else {9: 1},
        compiler_params=_cparams(1), name="ret_sample" if retention else "gla_sample",
    )(qt, kt, lt, cos_t, sin_t, p_main, p_main, g, s0_all, *([] if s_prev is None else [s_prev]))


def _gate_logits_t_body(w_ref, x_ref, b_ref, o_ref):
    o_ref[...] = jnp.dot(w_ref[...], x_ref[...], precision=HIGHEST, preferred_element_type=F32) + b_ref[...]


def _gate_logits_t(w_gate_t, glow_t, b_col):
    nt = N_S // SAMPLE_TILE
    return pl.pallas_call(
        _gate_logits_t_body,
        grid=(nt,),
        in_specs=[pl.BlockSpec((HEADS * DK, LANES), lambda i: (0, 0)),
                  pl.BlockSpec((None, LANES, LANES), lambda i: (i, 0, 0)),
                  pl.BlockSpec((HEADS * DK, LANES), lambda i: (0, 0))],
        out_specs=pl.BlockSpec((None, HEADS * DK, LANES), lambda i: (i, 0, 0)),
        out_shape=jax.ShapeDtypeStruct((nt, HEADS * DK, LANES), F32),
        compiler_params=_cparams(1), name="gate_logits_t")(w_gate_t, glow_t, b_col)


def _to_tiles_t(x):
    c = x.shape[1]
    xt = jnp.swapaxes(x.reshape(N_S // SAMPLE_TILE, SAMPLE_TILE, c), 1, 2)
    return jnp.pad(xt, ((0, 0), (0, 0), (0, LANES - SAMPLE_TILE)))


ROWS_POOL = 512


def _pool_mix(y, w_ref, sc_ref):
    outs = []
    for gi in range(4):
        cs = slice(gi * LANES, (gi + 1) * LANES)
        outs.append(jnp.dot(y[:, cs].astype(BF16), w_ref[gi], preferred_element_type=F32))
    return jnp.concatenate(outs, axis=1) * sc_ref[...]


def _pool_prompt_body(p_ref, halo_ref, w_ref, sc_ref, o_ref):
    t = pl.program_id(1)
    p = p_ref[...]
    halo = jnp.where(t == 0, 0.0, halo_ref[...])
    full = jnp.concatenate([halo, p], axis=0)
    pos = t * ROWS_POOL + lax.broadcasted_iota(jnp.int32, (ROWS_POOL, LANES), 0)
    means = []
    for gi, w in enumerate(POOL_WINDOWS):
        s = full[:, gi * LANES:(gi + 1) * LANES]
        step = 1
        while step < w:
            s = s + pltpu.roll(s, step, 0)
            step *= 2
        win = s[16:, :]
        cnt = jnp.minimum(w, pos + 1).astype(F32)
        means.append(win / cnt)
    y = jnp.concatenate(means, axis=1) - p
    o_ref[...] = _pool_mix(y, w_ref, sc_ref).astype(BF16)


def _pool_prompt(p_main, w_bf, scale):
    nt = T_P // ROWS_POOL
    return pl.pallas_call(
        _pool_prompt_body,
        grid=(B_P, nt),
        in_specs=[pl.BlockSpec((ROWS_POOL, 512), lambda b, t: (b * nt + t, C_PIN // 512)),
                  pl.BlockSpec((16, 512), lambda b, t: (jnp.maximum((b * nt + t) * (ROWS_POOL // 16) - 1, 0),
                                                        C_PIN // 512)),
                  pl.BlockSpec((4, LANES, LANES), lambda b, t: (0, 0, 0)),
                  pl.BlockSpec((1, 512), lambda b, t: (0, 0))],
        out_specs=pl.BlockSpec((ROWS_POOL, 512), lambda b, t: (b * nt + t, 0)),
        out_shape=jax.ShapeDtypeStruct((N, 512), BF16),
        compiler_params=_cparams(2), name="pool_prompt")(p_main, p_main, w_bf, scale)


def _small_sample_body(p_ref, buf_ref, pw_ref, psc_ref, u_ref, sv_ref, sg_ref, sw_ref, sb_ref,
                       ob_ref, od_ref, vn_ref):
    p = p_ref[...]
    means = []
    for gi, w in enumerate(POOL_WINDOWS):
        cs = slice(gi * LANES, (gi + 1) * LANES)
        s = p[:, cs]
        for j in range(1, w):
            s = s + buf_ref[:, POOL_BUF - j, cs]
        means.append(s / float(min(w, PAST_LEN + 1)))
    y = jnp.concatenate(means, axis=1) - p
    ob_ref[...] = _pool_mix(y, pw_ref, psc_ref)
    sv = sv_ref[...]
    vn = sv * lax.rsqrt(jnp.mean(sv * sv, axis=-1, keepdims=True) + EPS) * sg_ref[...]
    vn_ref[...] = vn
    od_ref[...] = u_ref[...] * (sw_ref[...] * vn + sb_ref[...])


def _small_sample(p_main, buf, pw_bf, pscale, sgu_g, sgu_w0, sgu_b0):
    row = N_P // N_S
    col = lambda c: pl.BlockSpec((N_S, 512), lambda i, c=c: (row, c // 512))
    vec = pl.BlockSpec((1, 512), lambda i: (0, 0))
    return pl.pallas_call(
        _small_sample_body,
        grid=(1,),
        in_specs=[col(C_PIN), pl.BlockSpec((N_S, POOL_BUF, 512), lambda i: (0, 0, 0)),
                  pl.BlockSpec((4, LANES, LANES), lambda i: (0, 0, 0)), vec,
                  col(C_SU), col(C_SV), vec, vec, vec],
        out_specs=[pl.BlockSpec((N_S, 512), lambda i: (0, 0))] * 3,
        out_shape=[jax.ShapeDtypeStruct((N_S, 512), F32)] * 3,
        compiler_params=_cparams(1), name="small_sample",
    )(p_main, buf, pw_bf, pscale, p_main, p_main, sgu_g, sgu_w0, sgu_b0)


ROWS_SGU = 512
SGU_CHUNK = 128


def _sgu_prompt_body(u_ref, v_ref, g_ref, w_ref, bt_ref, o_ref):
    ri = lax.broadcasted_iota(jnp.int32, (SGU_CHUNK, SGU_CHUNK), 0)
    ci = lax.broadcasted_iota(jnp.int32, (SGU_CHUNK, SGU_CHUNK), 1)
    causal = ri >= ci
    for c in range(ROWS_SGU // SGU_CHUNK):
        rows = pl.ds(c * SGU_CHUNK, SGU_CHUNK)
        v = v_ref[rows, :]
        vn = (v * lax.rsqrt(jnp.mean(v * v, axis=-1, keepdims=True) + EPS) * g_ref[...]).astype(BF16)
        outs = []
        for gi in range(4):
            cs = slice(gi * LANES, (gi + 1) * LANES)
            w = jnp.where(causal, w_ref[gi], 0.0).astype(BF16)
            mixed = jnp.dot(w, vn[:, cs], preferred_element_type=F32)
            outs.append(mixed + jnp.broadcast_to(bt_ref[:, gi:gi + 1], (SGU_CHUNK, LANES)))
        o_ref[rows, :] = (u_ref[rows, :] * jnp.concatenate(outs, axis=1)).astype(BF16)


def _sgu_prompt(p_main, g, w, b_t):
    return pl.pallas_call(
        _sgu_prompt_body,
        grid=(N_P // ROWS_SGU,),
        in_specs=[pl.BlockSpec((ROWS_SGU, 512), lambda i: (i, C_SU // 512)),
                  pl.BlockSpec((ROWS_SGU, 512), lambda i: (i, C_SV // 512)),
                  pl.BlockSpec((1, 512), lambda i: (0, 0)),
                  pl.BlockSpec((4, SGU_CHUNK, SGU_CHUNK), lambda i: (0, 0, 0)),
                  pl.BlockSpec((SGU_CHUNK, LANES), lambda i: (0, 0))],
        out_specs=pl.BlockSpec((ROWS_SGU, 512), lambda i: (i, 0)),
        out_shape=jax.ShapeDtypeStruct((N, 512), BF16),
        compiler_params=_cparams(1), name="sgu_prompt")(p_main, p_main, g, w, b_t)


TM_MERGE = 640
TN_MERGE = 512


def _merge_body(h_ref, ba_ref, bb_ref, bc_ref, bd_ref, g0, g1, g2, g3, u0, u1, u2, u3,
                c0, c1, c2, c3, o_ref):
    h = h_ref[...]
    acc = None
    for br, gw, uw, gb in ((ba_ref, g0, u0, c0), (bb_ref, g1, u1, c1), (bc_ref, g2, u2, c2), (bd_ref, g3, u3, c3)):
        gate = jax.nn.sigmoid(jnp.dot(h, gw[...], preferred_element_type=F32) + gb[...])
        up = jnp.dot(br[...], uw[...], preferred_element_type=F32)
        acc = gate * up if acc is None else acc + gate * up
    o_ref[...] = acc.astype(BF16)


def _merge(h, branches, w_mg, b_mg, w_br, layer):
    nj = D // TN_MERGE
    row = lambda w: pl.BlockSpec((TM_MERGE, w), lambda i, j: (i, 0))
    gate_w = [pl.BlockSpec((None, D, TN_MERGE), lambda i, j, b=b, l=layer: (l, 0, b * nj + j)) for b in range(4)]
    up_w = [pl.BlockSpec((None, None, 512, TN_MERGE), lambda i, j, b=b, l=layer: (l, b, 0, j)) for b in range(4)]
    gate_b = [pl.BlockSpec((None, 1, TN_MERGE), lambda i, j, b=b, l=layer: (l, 0, b * nj + j)) for b in range(4)]
    return pl.pallas_call(
        _merge_body,
        grid=(N // TM_MERGE, nj),
        in_specs=[row(D)] + [row(512)] * 4 + gate_w + up_w + gate_b,
        out_specs=pl.BlockSpec((TM_MERGE, TN_MERGE), lambda i, j: (i, j)),
        out_shape=jax.ShapeDtypeStruct((N, D), BF16),
        compiler_params=_cparams(2), name="merge",
    )(h, *branches, w_mg, w_mg, w_mg, w_mg, w_br, w_br, w_br, w_br, b_mg, b_mg, b_mg, b_mg)


TM_OUT = 640
TN_OUT = 512


def _post_value(t, x, y, gn_ref, gp_ref, gs_ref):
    yn = y * lax.rsqrt(jnp.mean(y * y, axis=-1, keepdims=True) + EPS) * gn_ref[...]
    return x + _mod_rows(t, gp_ref, gs_ref) * yn


def _outproj_body(m_ref, w_ref, *refs):
    gn_ref, gp_ref, gs_ref, o_ref, acc_ref = refs[-5:]
    j = pl.program_id(1)
    acc_ref[j] = jnp.dot(m_ref[...], w_ref[...], preferred_element_type=F32)

    @pl.when(j == D // TN_OUT - 1)
    def _():
        for sidx in range(TM_OUT // SUB):
            rows = pl.ds(sidx * SUB, SUB)
            t = pl.program_id(0) * (TM_OUT // SUB) + sidx
            y = jnp.concatenate([acc_ref[c, rows, :] for c in range(D // TN_OUT)], axis=1)
            o_ref[rows, :] = _post_value(t, _x_rows(refs[:-5], t, rows), y, gn_ref, gp_ref, gs_ref)


def _outproj(merged, w_out, x, g_post, mod, layer):
    mspec = [pl.BlockSpec((None, 8, D), lambda i, j, l=layer: (l, 0, 2)),
             pl.BlockSpec((None, SUB, D), lambda i, j, l=layer: (l, 1, 2))]
    x_specs, x_args = _x_specs(x, TM_OUT, 2)
    return pl.pallas_call(
        _outproj_body,
        grid=(N // TM_OUT, D // TN_OUT),
        in_specs=[pl.BlockSpec((TM_OUT, D), lambda i, j: (i, 0)),
                  pl.BlockSpec((None, D, TN_OUT), lambda i, j, l=layer: (l, 0, j))] + x_specs
        + [pl.BlockSpec((1, D), lambda i, j: (0, 0))] + mspec,
        out_specs=pl.BlockSpec((TM_OUT, D), lambda i, j: (i, 0)),
        out_shape=jax.ShapeDtypeStruct((N, D), F32),
        scratch_shapes=[pltpu.VMEM((D // TN_OUT, TM_OUT, TN_OUT), F32)],
        compiler_params=_cparams(2), name="outproj",
    )(merged, w_out, *x_args, g_post.reshape(1, D), mod, mod)


def _router_body(lg_ref, b_ref, eid_ref, pos_ref, wt_ref, cnt_ref, run_ref):
    i = pl.program_id(0)

    @pl.when(i == 0)
    def _():
        run_ref[...] = jnp.zeros_like(run_ref)

    ng, gs = 8, N_EXPERTS // 8
    neg = -jnp.inf
    scores = jax.nn.sigmoid(lg_ref[...].T[:N_EXPERTS, :])
    sel = scores + b_ref[...]
    sel3 = sel.reshape(ng, gs, SUB)
    sub3 = lax.broadcasted_iota(jnp.int32, (ng, gs, SUB), 1)
    gmax = jnp.max(sel3, axis=1, keepdims=True)
    first = jnp.min(jnp.where(sel3 == gmax, sub3, gs), axis=1, keepdims=True)
    gmax2 = jnp.max(jnp.where(sub3 == first, neg, sel3), axis=1, keepdims=True)
    gscore = (gmax + gmax2).reshape(ng, SUB)
    gidx = lax.broadcasted_iota(jnp.int32, (ng, SUB), 0)
    grank = jnp.zeros((ng, SUB), jnp.int32)
    for s in range(1, ng):
        other = pltpu.roll(gscore, s, 0)
        lower = gidx >= s
        grank += ((other > gscore) | ((other == gscore) & lower)).astype(jnp.int32)
    keep = jnp.broadcast_to((grank < 4).reshape(ng, 1, SUB), (ng, gs, SUB))
    masked = jnp.where(keep, sel3, neg).reshape(N_EXPERTS, SUB)
    eidx = lax.broadcasted_iota(jnp.int32, (N_EXPERTS, SUB), 0)
    rank = jnp.zeros((N_EXPERTS, SUB), jnp.int32)
    for s in range(1, N_EXPERTS):
        other = pltpu.roll(masked, s, 0)
        lower = eidx >= s
        rank += ((other > masked) | ((other == masked) & lower)).astype(jnp.int32)
    chosen = rank < TOP_K
    w_sel = jnp.where(chosen, scores, 0.0)
    w_sel = w_sel / jnp.sum(w_sel, axis=0, keepdims=True) * ROUTED_SCALE
    ri = lax.broadcasted_iota(jnp.int32, (SUB, SUB), 0)
    ci = lax.broadcasted_iota(jnp.int32, (SUB, SUB), 1)
    onehot = chosen.astype(BF16)
    pos = jnp.dot(onehot, (ri < ci).astype(BF16), preferred_element_type=F32) + run_ref[...]
    run_ref[...] = run_ref[...] + jnp.sum(chosen.astype(F32), axis=1, keepdims=True)
    cnt_ref[...] = run_ref[...]
    eidx_f = eidx.astype(F32)
    rows_e, rows_p, rows_w = [], [], []
    for kk in range(TOP_K):
        m = chosen & (rank == kk)
        rows_e.append(jnp.sum(jnp.where(m, eidx_f, 0.0), axis=0, keepdims=True))
        rows_p.append(jnp.sum(jnp.where(m, pos, 0.0), axis=0, keepdims=True))
        rows_w.append(jnp.sum(jnp.where(m, w_sel, 0.0), axis=0, keepdims=True))
    eid_ref[...] = jnp.concatenate(rows_e, axis=0).astype(jnp.int32)
    pos_ref[...] = jnp.concatenate(rows_p, axis=0).astype(jnp.int32)
    wt_ref[...] = jnp.concatenate(rows_w, axis=0)


def _router(logits, rb_col):
    tile = pl.BlockSpec((TOP_K, SUB), lambda i: (0, i))
    return pl.pallas_call(
        _router_body,
        grid=(N // SUB,),
        in_specs=[pl.BlockSpec((SUB, LANES), lambda i: (i, 0)),
                  pl.BlockSpec((N_EXPERTS, SUB), lambda i: (0, 0))],
        out_specs=[tile, tile, tile, pl.BlockSpec((N_EXPERTS, SUB), lambda i: (0, 0))],
        out_shape=[jax.ShapeDtypeStruct((TOP_K, N), jnp.int32), jax.ShapeDtypeStruct((TOP_K, N), jnp.int32),
                   jax.ShapeDtypeStruct((TOP_K, N), F32), jax.ShapeDtypeStruct((N_EXPERTS, SUB), F32)],
        scratch_shapes=[pltpu.VMEM((N_EXPERTS, SUB), F32)],
        compiler_params=_cparams(1), name="router")(logits, rb_col)


SC_CORES, SC_SUBCORES = 2, 16
SC_WORKERS = SC_CORES * SC_SUBCORES
SC_LANES = 16
SC_SCAN = N_ASSIGN // SC_WORKERS


def _sc_mesh():
    return plsc.VectorSubcoreMesh(core_axis_name="c", subcore_axis_name="s",
                                  num_cores=SC_CORES, num_subcores=SC_SUBCORES)


def _sc_worker_base(per_w):
    return (lax.axis_index("s") * SC_CORES + lax.axis_index("c")) * per_w


def _sc_chunk(per_w):
    return max(c for c in (8, 16, 24, 32) if per_w % c == 0)


def _sc_gather_rows(table_hbm, out_hbm, idx_v, rows_v, gsem, wsem, base, per_w):
    chunk = _sc_chunk(per_w)
    n_ch = per_w // chunk

    def gather(j, p):
        off = pl.multiple_of(j * chunk, 8)
        return pltpu.make_async_copy(table_hbm.at[idx_v.at[pl.ds(off, chunk)]], rows_v.at[p], gsem.at[p])

    def write(j, p):
        off = pl.multiple_of(j * chunk, 8)
        return pltpu.make_async_copy(rows_v.at[p], out_hbm.at[pl.ds(base + off, chunk)], wsem.at[p])

    def when(cond, fn):
        if isinstance(cond, bool):
            if cond:
                fn()
        else:
            pl.when(cond)(fn)

    def step(j, p):
        gather(j, p).wait()
        when(j >= 1, lambda: write(j - 1, 1 - p).wait())
        when(j + 1 < n_ch, lambda: gather(j + 1, 1 - p).start())
        write(j, p).start()

    gather(0, 0).start()

    @pl.loop(0, n_ch - n_ch % 2, step=2)
    def _(j0):
        for p in range(2):
            step(j0 + p, p)

    if n_ch % 2:
        step(n_ch - 1, 0)
    write(n_ch - 1, (n_ch - 1) % 2).wait()


def _sc_row_scratch(per_w):
    return [pltpu.VMEM((2, _sc_chunk(per_w)) + ROW_TILE, jnp.int32),
            pltpu.SemaphoreType.DMA((2,)), pltpu.SemaphoreType.DMA((2,))]


def _sc_gather(table, idx):
    n_out = idx.shape[0]
    per_w = n_out // SC_WORKERS
    assert per_w * SC_WORKERS == n_out and per_w % 8 == 0

    def body(table_hbm, idx_hbm, out_hbm, idx_v, rows_v, gsem, wsem):
        base = _sc_worker_base(per_w)
        pltpu.sync_copy(idx_hbm.at[pl.ds(base, per_w)], idx_v)
        _sc_gather_rows(table_hbm, out_hbm, idx_v, rows_v, gsem, wsem, base, per_w)

    return pl.kernel(
        body, out_type=jax.ShapeDtypeStruct((n_out,) + ROW_TILE, jnp.int32), mesh=_sc_mesh(),
        scratch_types=[pltpu.VMEM((per_w,), jnp.int32)] + _sc_row_scratch(per_w), name="sc_gather")(table, idx)


PARTS = ((0, 27), (27, 54), (81, 81), (162, 162))
assert sum(n for _, n in PARTS) == N_BLOCKS and all(PARTS[i][0] + PARTS[i][1] == PARTS[i + 1][0]
                                                    for i in range(len(PARTS) - 1))


def _sc_dispatch(table, slots, blk0, n_blk):
    part_slots = n_blk * EXP_BLOCK
    per_w = part_slots // SC_WORKERS
    per_w_pad = -(-per_w // SC_LANES) * SC_LANES
    assert per_w * SC_WORKERS == part_slots and per_w % 8 == 0 and SC_SCAN % SC_LANES == 0
    assert ZERO_ROWS & (ZERO_ROWS - 1) == 0

    def body(table_hbm, slots_hbm, out_hbm, idx_v, sl_v, rows_v, gsem, wsem):
        local = _sc_worker_base(per_w)
        base = blk0 * EXP_BLOCK + local
        lane = lax.iota(jnp.int32, SC_LANES)

        @pl.loop(0, per_w_pad // SC_LANES)
        def _(j):
            off = pl.multiple_of(j * SC_LANES, SC_LANES)
            idx_v[pl.ds(off, SC_LANES)] = N + ((base + off + lane) & (ZERO_ROWS - 1))

        @pl.loop(0, N_ASSIGN // SC_SCAN)
        def _(c):
            pltpu.sync_copy(slots_hbm.at[pl.ds(pl.multiple_of(c * SC_SCAN, 8), SC_SCAN)], sl_v)

            @pl.loop(0, SC_SCAN // SC_LANES)
            def _(j):
                off = pl.multiple_of(j * SC_LANES, SC_LANES)
                loc = sl_v[pl.ds(off, SC_LANES)] - base
                mine = (loc >= 0) & (loc < per_w)
                tok = lax.shift_right_logical(c * SC_SCAN + off + lane, 3)
                plsc.store_scatter(idx_v, [jnp.where(mine, loc, 0)], tok, mask=mine)

        _sc_gather_rows(table_hbm, out_hbm, idx_v, rows_v, gsem, wsem, local, per_w)

    return pl.kernel(
        body, out_type=jax.ShapeDtypeStruct((part_slots,) + ROW_TILE, jnp.int32), mesh=_sc_mesh(),
        scratch_types=[pltpu.VMEM((per_w_pad,), jnp.int32), pltpu.VMEM((SC_SCAN,), jnp.int32)]
        + _sc_row_scratch(per_w),
        compiler_params=pltpu.CompilerParams(needs_layout_passes=False),
        name="sc_dispatch")(table, slots)


def _experts_body(be_ref, first_ref, par_ref, next_ref, nextblk_ref, nused_ref, x_ref, w1_hbm, w3_hbm, w2_hbm,
                  *rest, layer, blk0, n_blk):
    y_ref, w1f, w3f, w2f, w1b, w3b, w2b, sem = rest[-8:]
    i = pl.program_id(0)
    b = blk0 + i
    used = b < nused_ref[0]

    def copies(e, slot):
        return (pltpu.make_async_copy(w1_hbm.at[layer, e], w1f.at[slot], sem.at[0, slot]),
                pltpu.make_async_copy(w3_hbm.at[layer, e], w3f.at[slot], sem.at[1, slot]),
                pltpu.make_async_copy(w2_hbm.at[layer, e], w2f.at[slot], sem.at[2, slot]))

    @pl.when(used & (i == 0))
    def _():
        for c in copies(be_ref[b], par_ref[b]):
            c.start()

    @pl.when(used & ((i == 0) | (first_ref[b] == 1)))
    def _():
        slot = par_ref[b]
        for c in copies(be_ref[b], slot):
            c.wait()

        @pl.when((next_ref[b] >= 0) & (nextblk_ref[b] < blk0 + n_blk))
        def _():
            for c in copies(next_ref[b], 1 - slot):
                c.start(priority=1)
        w1b[...] = w1f[slot].astype(BF16)
        w3b[...] = w3f[slot].astype(BF16)
        w2b[...] = w2f[slot].astype(BF16)

    @pl.when(used)
    def _():
        lo, hi = _unpack_bf16_pair(_load_row_tiles_2d(x_ref, EXP_BLOCK))
        lo = lo.astype(BF16)
        hi = hi.astype(BF16)
        half = D // 2
        h1 = (jnp.dot(lo, w1b[:half, :], preferred_element_type=F32)
              + jnp.dot(hi, w1b[half:, :], preferred_element_type=F32))
        h3 = (jnp.dot(lo, w3b[:half, :], preferred_element_type=F32)
              + jnp.dot(hi, w3b[half:, :], preferred_element_type=F32))
        hid = (_silu(h1) * h3).astype(BF16)
        y = jnp.dot(hid, w2b[...], preferred_element_type=F32)
        _store_row_tiles_2d(y_ref, _pack_bf16_pair(y[:, :half], y[:, half:]), EXP_BLOCK)

    @pl.when(jnp.logical_not(used))
    def _():
        y_ref[...] = jnp.zeros_like(y_ref)


def _experts(ctl, xs_part, w1, w3, w2, layer, blk0, n_blk, ys_prev):
    def x_blk(i, *refs):
        n_here = jnp.clip(refs[-1][0] - blk0, 1, n_blk)
        return (jnp.minimum(i, n_here - 1), 0)
    any_spec = pl.BlockSpec(memory_space=pl.ANY)
    in_specs = [pl.BlockSpec((EXP_BLOCK * ROW_TILE[0], LANES), x_blk), any_spec, any_spec, any_spec]
    args = [xs_part.reshape(n_blk * EXP_BLOCK * ROW_TILE[0], LANES), w1, w3, w2]
    aliases = {}
    if ys_prev is not None:
        in_specs.append(any_spec)
        args.append(ys_prev)
        aliases = {len(ctl) + 4: 0}
    grid_spec = pltpu.PrefetchScalarGridSpec(
        num_scalar_prefetch=len(ctl),
        grid=(n_blk,),
        in_specs=in_specs,
        out_specs=pl.BlockSpec((EXP_BLOCK * ROW_TILE[0], LANES), lambda i, *refs: (blk0 + i, 0)),
        scratch_shapes=[pltpu.VMEM((2, D, D_EXPERT), F32), pltpu.VMEM((2, D, D_EXPERT), F32),
                        pltpu.VMEM((2, D_EXPERT, D), F32),
                        pltpu.VMEM((D, D_EXPERT), BF16), pltpu.VMEM((D, D_EXPERT), BF16),
                        pltpu.VMEM((D_EXPERT, D), BF16), pltpu.SemaphoreType.DMA((3, 2))])
    return pl.pallas_call(
        functools.partial(_experts_body, layer=layer, blk0=blk0, n_blk=n_blk), grid_spec=grid_spec,
        out_shape=jax.ShapeDtypeStruct((L_SLOTS * ROW_TILE[0], LANES), jnp.int32),
        input_output_aliases=aliases,
        compiler_params=_cparams(1), name="experts")(*ctl, *args)


TM_FFN = 640


def _ffn_pre_body(x_ref, g_ref, shp_ref, shs_ref, scp_ref, scs_ref, rw_ref, hb_ref, lg_ref, hp_ref):
    i = pl.program_id(0)

    @pl.when(i < N // TM_FFN)
    def _():
        for sidx in range(TM_FFN // SUB):
            rows = pl.ds(sidx * SUB, SUB)
            h = _prenorm_rows(i * (TM_FFN // SUB) + sidx, x_ref[rows, :], g_ref, shp_ref, shs_ref, scp_ref, scs_ref)
            lg_ref[rows, :] = jnp.dot(h, rw_ref[...], precision=HIGHEST, preferred_element_type=F32)
            packed = _pack_bf16_pair(h[:, :D // 2], h[:, D // 2:])
            for c in range(ROW_TILE[0]):
                hp_ref[pl.ds(sidx * SUB * ROW_TILE[0] + c, SUB, stride=ROW_TILE[0]), :] = (
                    packed[:, c * LANES:(c + 1) * LANES])
            hb_ref[rows, :] = h.astype(BF16)

    @pl.when(i >= N // TM_FFN)
    def _():
        hp_ref[...] = jnp.zeros_like(hp_ref)


def _ffn_pre(x, g, mod, layer, rw):
    last = N // TM_FFN - 1
    row = lambda i: (jnp.minimum(i, last), 0)
    mspec = lambda part, rows, blk: pl.BlockSpec((None, rows, D), lambda i, l=layer, p=part, b=blk: (l, b, p))
    return pl.pallas_call(
        _ffn_pre_body,
        grid=(N // TM_FFN + 1,),
        in_specs=[pl.BlockSpec((TM_FFN, D), row), pl.BlockSpec((1, D), lambda i: (0, 0)),
                  mspec(3, 8, 0), mspec(3, SUB, 1), mspec(4, 8, 0), mspec(4, SUB, 1),
                  pl.BlockSpec((None, D, LANES), lambda i, l=layer: (l, 0, 0))],
        out_specs=[pl.BlockSpec((TM_FFN, D), row), pl.BlockSpec((TM_FFN, LANES), row),
                   pl.BlockSpec((TM_FFN * ROW_TILE[0], LANES), lambda i: (i, 0))],
        out_shape=[jax.ShapeDtypeStruct((N, D), BF16), jax.ShapeDtypeStruct((N, LANES), F32),
                   jax.ShapeDtypeStruct(((N + TM_FFN) * ROW_TILE[0], LANES), jnp.int32)],
        compiler_params=_cparams(1), name="ffn_pre")(x, g.reshape(1, D), mod, mod, mod, mod, rw)


TM_SHARED = 640


def _shared_body(h_ref, w13_ref, w2_ref, after_ref, o_ref):
    up = jnp.dot(h_ref[...], w13_ref[...], preferred_element_type=F32)
    hid = (_silu(up[:, :D_EXPERT]) * up[:, D_EXPERT:]).astype(BF16)
    o_ref[...] = jnp.dot(hid, w2_ref[...], preferred_element_type=F32)


def _shared(h, w13, w2, layer, after):
    return pl.pallas_call(
        _shared_body,
        grid=(N // TM_SHARED,),
        in_specs=[pl.BlockSpec((TM_SHARED, D), lambda i: (i, 0)),
                  pl.BlockSpec((None, D, 2 * D_EXPERT), lambda i, l=layer: (l, 0, 0)),
                  pl.BlockSpec((None, D_EXPERT, D), lambda i, l=layer: (l, 0, 0)),
                  pl.BlockSpec(memory_space=pl.ANY)],
        out_specs=pl.BlockSpec((TM_SHARED, D), lambda i: (i, 0)),
        out_shape=jax.ShapeDtypeStruct((N, D), F32),
        compiler_params=_cparams(1), name="shared")(h, w13, w2, after)


def _combine_body(g_ref, wt_ref, sh_ref, x_ref, gn_ref, gp_ref, gs_ref, *rest, tile0, n_out, final):
    outs = rest[-n_out:]
    half = D // 2
    acc_lo = sh_ref[:, :half]
    acc_hi = sh_ref[:, half:]
    wt = wt_ref[...]
    per_tok = TOP_K * ROW_TILE[0]
    for k in range(TOP_K):
        packed = jnp.concatenate([g_ref[pl.ds(k * ROW_TILE[0] + c, SUB, stride=per_tok), :]
                                  for c in range(ROW_TILE[0])], axis=1)
        lo, hi = _unpack_bf16_pair(packed)
        w_c = wt[:, k:k + 1]
        acc_lo = acc_lo + w_c * lo
        acc_hi = acc_hi + w_c * hi
    t = tile0 + pl.program_id(0)
    val = _post_value(t, x_ref[...], jnp.concatenate([acc_lo, acc_hi], axis=1), gn_ref, gp_ref, gs_ref)
    if not final:
        outs[0][...] = val
    else:
        @pl.when(t < N_P // SUB)
        def _():
            outs[0][...] = val
        if n_out == 2:
            @pl.when(t >= N_P // SUB)
            def _():
                outs[1][...] = val


def _combine(gathered, wts, shared, x, g_post, mod, layer, tile0, n_tiles, out_prev, final):
    per_tok = TOP_K * ROW_TILE[0]
    row = lambda i: (tile0 + i, 0)
    in_specs = [pl.BlockSpec((SUB * per_tok, LANES), lambda i: (i, 0)),
                pl.BlockSpec((SUB, LANES), row), pl.BlockSpec((SUB, D), row), pl.BlockSpec((SUB, D), row),
                pl.BlockSpec((1, D), lambda i: (0, 0)),
                pl.BlockSpec((None, 8, D), lambda i, l=layer: (l, 0, 5)),
                pl.BlockSpec((None, SUB, D), lambda i, l=layer: (l, 1, 5))]
    args = [gathered.reshape(n_tiles * SUB * per_tok, LANES), wts, shared, x, g_post.reshape(1, D), mod, mod]
    aliases = {}
    if out_prev is not None:
        in_specs.append(pl.BlockSpec(memory_space=pl.ANY))
        args.append(out_prev)
        aliases = {len(args) - 1: 0}
    if not final:
        out_specs = [pl.BlockSpec((SUB, D), row)]
        out_shape = [jax.ShapeDtypeStruct((N, D), F32)]
    else:
        last_p = N_P // SUB - 1
        out_specs = [pl.BlockSpec((SUB, D), lambda i: (jnp.minimum(tile0 + i, last_p), 0))]
        out_shape = [jax.ShapeDtypeStruct((N_P, D), F32)]
        if tile0 + n_tiles > N_P // SUB:
            out_specs.append(pl.BlockSpec((N_S, D), lambda i: (0, 0)))
            out_shape.append(jax.ShapeDtypeStruct((N_S, D), F32))
    return pl.pallas_call(
        functools.partial(_combine_body, tile0=tile0, n_out=len(out_shape), final=final),
        grid=(n_tiles,),
        in_specs=in_specs, out_specs=out_specs, out_shape=out_shape,
        input_output_aliases=aliases,
        compiler_params=_cparams(1), name="combine")(*args)


def _slots_body(start_ref, eid_ref, pos_ref, o_ref):
    eid = eid_ref[...]
    acc = pos_ref[...]
    for e in range(N_EXPERTS):
        acc = acc + jnp.where(eid == e, start_ref[e], 0)
    o_ref[...] = acc


def _slots(pad_start, eid, pos):
    grid_spec = pltpu.PrefetchScalarGridSpec(
        num_scalar_prefetch=1, grid=(1,),
        in_specs=[pl.BlockSpec((TOP_K, N), lambda i, s: (0, 0)), pl.BlockSpec((TOP_K, N), lambda i, s: (0, 0))],
        out_specs=pl.BlockSpec((TOP_K, N), lambda i, s: (0, 0)))
    return pl.pallas_call(_slots_body, grid_spec=grid_spec,
                          out_shape=jax.ShapeDtypeStruct((TOP_K, N), jnp.int32),
                          compiler_params=_cparams(1), name="slots")(pad_start, eid, pos)


def _put_sample_rows_body(*refs):
    n = len(refs) // 3
    for src, dst in zip(refs[:n], refs[2 * n:]):
        dst[...] = src[...].astype(BF16)


def _put_sample_rows(sample_rows, full):
    n = len(full)
    return pl.pallas_call(
        _put_sample_rows_body,
        grid=(1,),
        in_specs=[pl.BlockSpec((N_S, 512), lambda i: (0, 0))] * n + [pl.BlockSpec(memory_space=pl.ANY)] * n,
        out_specs=[pl.BlockSpec((N_S, 512), lambda i: (N_P // N_S, 0))] * n,
        out_shape=[jax.ShapeDtypeStruct((N, 512), BF16)] * n,
        input_output_aliases={n + k: k for k in range(n)},
        compiler_params=_cparams(1), name="put_sample_rows")(*sample_rows, *full)


COMBINE_RANGES = ((0, 33), (33, 32))

def _prepare_weights(w_in, w_merge_gate, w_branch, w_out, router_w, shared_w1, shared_w3, shared_w2):
    return dict(
        w_main=jnp.concatenate([w_in[:, :, :1536], w_in[:, :, 1552:]], axis=2).astype(BF16),
        w_low=jnp.pad(w_in[:, :, 1536:1552], ((0, 0), (0, 0), (0, LANES - 16))).astype(BF16),
        w_mg=w_merge_gate.astype(BF16), w_br=w_branch.astype(BF16), w_out=w_out.astype(BF16),
        rw=jnp.pad(router_w, ((0, 0), (0, 0), (0, LANES - N_EXPERTS))),
        w13=jnp.concatenate([shared_w1, shared_w3], axis=2).astype(BF16), sw2=shared_w2.astype(BF16))


def _rope_tables(pos):
    half = DK // 2
    inv = ROPE_BASE ** (-jnp.arange(half, dtype=F32) / half)
    ang = pos.astype(F32)[:, None] * inv[None, :]
    cos = jnp.cos(ang)
    sin = jnp.sin(ang)
    return jnp.concatenate([cos, cos], axis=1), jnp.concatenate([-sin, sin], axis=1)


def _layer(l, x, mod, s_gla, s_pool, s_ret, wts, prep, final, prev_gla, prev_ret):
    (norm_mix_pre, norm_mix_post, norm_ffn_pre, norm_ffn_post, w_in, w_gla_gate, b_gla_gate, gla_norm,
     pool_w, pool_scale, ret_norm, sgu_norm, sgu_w, sgu_b, w_branch, w_merge_gate, b_merge_gate, w_out,
     router_w, router_bias, expert_w1, expert_w3, expert_w2, shared_w1, shared_w3, shared_w2) = wts

    h = _prenorm(x, norm_mix_pre[l], mod, l)
    p_main = _matmul(h, prep["w_main"], l, 1664, 512, name="inproj")
    p_low = _matmul(h, prep["w_low"], l, 1664, LANES, name="inproj_low")

    w_gate_pad = jnp.pad(w_gla_gate[l], ((0, LANES - 16), (0, 0)))
    b_gate = b_gla_gate[l].reshape(1, HEADS * DK)
    log_gamma = jnp.log1p(-jnp.exp2(-5.0 - jnp.arange(HEADS, dtype=F32)))
    dec_row = jnp.repeat(log_gamma, DK).reshape(1, HEADS * DK)
    cos_p, sin_p = _rope_tables(jnp.arange(T_P))
    cos_p = jnp.tile(cos_p, (1, 2))
    sin_p = jnp.tile(sin_p, (1, 2))
    g_gla = gla_norm[l].reshape(1, HEADS * DV)
    g_ret = ret_norm[l].reshape(1, HEADS * DV)

    oa_p, gla_p = _la_prompt(p_main, C_GQ, C_GK, C_GV, C_GR, p_low, p_low, w_gate_pad, b_gate, g_gla, False)
    oc_p, ret_p = _la_prompt(p_main, C_RQ, C_RK, C_RV, C_RG, cos_p, sin_p, dec_row, b_gate, g_ret, True)
    pw_bf = pool_w[l].astype(BF16)
    pscale = pool_scale[l].reshape(1, 512)
    ob_p = _pool_prompt(p_main, pw_bf, pscale)
    sgu_g = sgu_norm[l].reshape(1, 512)
    od_p = _sgu_prompt(p_main, sgu_g, sgu_w[l], jnp.pad(sgu_b[l].T, ((0, 0), (0, LANES - 4))))

    ps = p_main[N_P:]
    q_t = _to_tiles_t(ps[:, C_GQ:C_GQ + 256])
    k_t = _to_tiles_t(ps[:, C_GK:C_GK + 256])
    glow_t = jnp.pad(_to_tiles_t(p_low[N_P:, :16]), ((0, 0), (0, LANES - 16), (0, 0)))
    w_gate_t = jnp.pad(w_gla_gate[l].T, ((0, 0), (0, LANES - 16)))
    b_col = jnp.broadcast_to(b_gla_gate[l][:, None], (HEADS * DK, LANES))
    logit_t = _gate_logits_t(w_gate_t, glow_t, b_col)
    dummy = jnp.zeros((HEADS * DK, LANES), F32)
    oa_s, gla_s = _la_sample(q_t, k_t, logit_t, dummy, dummy, p_main, C_GV, C_GR, g_gla, s_gla, l, prev_gla, False)
    cos_s, sin_s = _rope_tables(jnp.full((1,), PAST_LEN))
    cos_c = jnp.broadcast_to(jnp.tile(cos_s[0], HEADS)[:, None], (HEADS * DK, LANES))
    sin_c = jnp.broadcast_to(jnp.tile(sin_s[0], HEADS)[:, None], (HEADS * DK, LANES))
    dec_c = jnp.broadcast_to(jnp.repeat(log_gamma, DK)[:, None], (HEADS * DK, LANES))
    rq_t = _to_tiles_t(ps[:, C_RQ:C_RQ + 256])
    rk_t = _to_tiles_t(ps[:, C_RK:C_RK + 256])
    oc_s, ret_s = _la_sample(rq_t, rk_t, dec_c, cos_c, sin_c, p_main, C_RV, C_RG, g_ret, s_ret, l, prev_ret, True)
    sgu_w0 = jnp.repeat(sgu_w[l][:, 0, 0], LANES).reshape(1, 512)
    sgu_b0 = jnp.repeat(sgu_b[l][:, 0], LANES).reshape(1, 512)
    ob_s, od_s, vn_s = _small_sample(p_main, s_pool[l], pw_bf, pscale, sgu_g, sgu_w0, sgu_b0)
    pool_p = jnp.stack([p_main[(b + 1) * T_P - POOL_BUF:(b + 1) * T_P, C_PIN:C_PIN + 512] for b in range(B_P)])
    pool_s = jnp.concatenate([s_pool[l][:, 1:], ps[:, None, C_PIN:C_PIN + 512]], axis=1)

    branches = _put_sample_rows([oa_s, ob_s, oc_s, od_s], [oa_p, ob_p, oc_p, od_p])
    merged = _merge(h, branches, prep["w_mg"], b_merge_gate.reshape(DEPTH, 1, 4 * D), prep["w_br"], l)
    x = _outproj(merged, prep["w_out"], x, norm_mix_post[l], mod, l)

    rb = jnp.broadcast_to(router_bias[l][:, None], (N_EXPERTS, SUB))
    h2, logits, h2_packed = _ffn_pre(x, norm_ffn_pre[l], mod, l, prep["rw"])
    eid, pos, wt, counts = _router(logits, rb)
    counts = counts[:, 0].astype(jnp.int32)
    padded = (counts + EXP_BLOCK - 1) // EXP_BLOCK * EXP_BLOCK
    pad_end = jnp.cumsum(padded)
    pad_start = pad_end - padded
    nused = (pad_end[-1] // EXP_BLOCK).astype(jnp.int32).reshape(1)
    blk_row = jnp.arange(N_BLOCKS, dtype=jnp.int32) * EXP_BLOCK
    block_e = jnp.minimum(jnp.sum((blk_row[:, None] >= pad_end[None, :]).astype(jnp.int32), axis=1),
                          N_EXPERTS - 1)
    first = jnp.concatenate([jnp.ones((1,), jnp.int32), (block_e[1:] != block_e[:-1]).astype(jnp.int32)])
    first = jnp.where(blk_row < pad_end[-1], first, 0)
    par = (jnp.cumsum(first) - 1) % 2
    live = jnp.where(padded > 0, jnp.arange(N_EXPERTS), N_EXPERTS)
    after = jnp.concatenate([lax.cummin(live, reverse=True)[1:], jnp.full((1,), N_EXPERTS)])
    of_block = block_e[:, None] == jnp.arange(N_EXPERTS)
    next_e = jnp.sum(jnp.where(of_block, jnp.where(after < N_EXPERTS, after, -1), 0), axis=1).astype(jnp.int32)
    next_blk = jnp.sum(jnp.where(of_block, pad_end // EXP_BLOCK, 0), axis=1).astype(jnp.int32)
    slots = _slots(pad_start.astype(jnp.int32), eid, pos).T.reshape(N_ASSIGN)
    wt = jnp.pad(wt.T, ((0, 0), (0, LANES - TOP_K)))
    table = h2_packed.reshape((N + TM_FFN,) + ROW_TILE)
    ctl = (block_e, first, par.astype(jnp.int32), next_e, next_blk, nused)
    xs_parts = [_sc_dispatch(table, slots, blk0, n_blk) for blk0, n_blk in PARTS]
    shared = _shared(h2, prep["w13"], prep["sw2"], l, slots)
    ys = None
    for xs_part, (blk0, n_blk) in zip(xs_parts, PARTS):
        ys = _experts(ctl, xs_part, expert_w1, expert_w3, expert_w2, l, blk0, n_blk, ys)
    ys = ys.reshape((L_SLOTS,) + ROW_TILE)
    outs = [None]
    for tile0, n_tiles in COMBINE_RANGES:
        a0, a1 = tile0 * SUB * TOP_K, (tile0 + n_tiles) * SUB * TOP_K
        outs = _combine(_sc_gather(ys, slots[a0:a1]), wt, shared, x, norm_ffn_post[l], mod, l, tile0, n_tiles,
                        outs[0], final)
    x = tuple(outs) if final else outs[0]
    return x, (gla_p, pool_p, pool_s, ret_p, vn_s), gla_s, ret_s


def kernel(x_prompt, x_sample, c_prompt, c_sample, state_gla, state_pool, state_ret, w_ada, b_ada, norm_mix_pre, norm_mix_post, norm_ffn_pre, norm_ffn_post, w_in, w_gla_gate, b_gla_gate, gla_norm, pool_w, pool_scale, ret_norm, sgu_norm, sgu_w, sgu_b, w_branch, w_merge_gate, b_merge_gate, w_out, router_w, router_bias, expert_w1, expert_w3, expert_w2, shared_w1, shared_w3, shared_w2):
    wts = (norm_mix_pre, norm_mix_post, norm_ffn_pre, norm_ffn_post, w_in, w_gla_gate, b_gla_gate, gla_norm,
           pool_w, pool_scale, ret_norm, sgu_norm, sgu_w, sgu_b, w_branch, w_merge_gate, b_merge_gate, w_out,
           router_w, router_bias, expert_w1, expert_w3, expert_w2, shared_w1, shared_w3, shared_w2)
    c_all = jnp.zeros((MOD_ROWS, D), F32).at[:B_P].set(c_prompt).at[SUB:SUB + N_S].set(c_sample)
    mod = _ada(c_all, w_ada, b_ada)
    x = (x_prompt.reshape(N_P, D), x_sample.reshape(N_S, D))
    prep = _prepare_weights(w_in, w_merge_gate, w_branch, w_out, router_w, shared_w1, shared_w3, shared_w2)
    per_layer = []
    gla_s = ret_s = None
    for l in range(DEPTH):
        x, states, gla_s, ret_s = _layer(l, x, mod, state_gla, state_pool, state_ret, wts, prep, l == DEPTH - 1,
                                         gla_s, ret_s)
        per_layer.append(states)
    gla_p, pool_p, pool_s, ret_p, vn_s = (jnp.stack(z) for z in zip(*per_layer))
    return (x[0].reshape(B_P, T_P, D), x[1].reshape(N_S, 1, D),
            gla_p, gla_s, pool_p, pool_s, ret_p, ret_s, vn_s.reshape(DEPTH, N_S, 1, 512))
```

```python
import functools

import jax
import jax.numpy as jnp
from jax import lax
from jax.experimental import pallas as pl
from jax.experimental.pallas import tpu as pltpu
from jax.experimental.pallas import tpu_sc as plsc

F32 = jnp.float32
BF16 = jnp.bfloat16
HIGHEST = lax.Precision.HIGHEST

D = 2048
B_P, T_P = 4, 2048
N_P = B_P * T_P
N_S = 128
N = N_P + N_S
DEPTH = 2
PAST_LEN = 16384
EPS = 1e-6
HEADS, DK, DV = 4, 64, 128
CHUNK = 64
GATE_TEMP = 16.0
POOL_WINDOWS = (2, 4, 8, 16)
POOL_BUF = 15
ROPE_BASE = 10000.0
N_EXPERTS = 64
TOP_K = 8
D_EXPERT = 512
ROUTED_SCALE = 2.5

LANES = 128
SUB = 128
MOD_ROWS = 256
EXP_BLOCK = 256
N_ASSIGN = N * TOP_K
N_BLOCKS = -(-(N_ASSIGN + N_EXPERTS * (EXP_BLOCK - 1)) // EXP_BLOCK)
L_SLOTS = N_BLOCKS * EXP_BLOCK
VMEM_LIMIT = 56 * 1024 * 1024

C_GQ, C_GK, C_GV, C_GR, C_PIN, C_RQ, C_RK, C_RV, C_RG, C_SU, C_SV = (
    0, 256, 512, 1024, 1536, 2048, 2304, 2560, 3072, 3584, 4096)
P_MAIN = 4608


def _cparams(n_axes=1):
    return pltpu.CompilerParams(dimension_semantics=("arbitrary",) * n_axes,
                                vmem_limit_bytes=VMEM_LIMIT)


def _silu(x):
    return x * jax.nn.sigmoid(x)


def _mod_rows(t, mp_ref, ms_ref):
    b = jnp.minimum(t // (T_P // SUB), B_P - 1)
    return jnp.where(t >= N_P // SUB, ms_ref[...], mp_ref[pl.ds(b, 1), :])


def _mod_specs(layer, part):
    return [pl.BlockSpec((None, 8, D), lambda i, l=layer, p=part: (l, 0, p)),
            pl.BlockSpec((None, SUB, D), lambda i, l=layer, p=part: (l, 1, p))]


def _pack_bf16_pair(lo, hi):
    lo_u = lax.bitcast_convert_type(lo.astype(BF16).astype(F32), jnp.uint32)
    hi_u = lax.bitcast_convert_type(hi.astype(BF16).astype(F32), jnp.uint32)
    return lax.bitcast_convert_type((hi_u & jnp.uint32(0xFFFF0000)) | (lo_u >> 16), jnp.int32)


def _unpack_bf16_pair(w):
    u = lax.bitcast_convert_type(w, jnp.uint32)
    lo = lax.bitcast_convert_type(u << 16, F32)
    hi = lax.bitcast_convert_type(u & jnp.uint32(0xFFFF0000), F32)
    return lo, hi


ROW_TILE = (8, LANES)


def _load_row_tiles(ref):
    return jnp.concatenate([ref[:, c, :] for c in range(ROW_TILE[0])], axis=1)


def _store_row_tiles(ref, val):
    for c in range(ROW_TILE[0]):
        ref[:, c, :] = val[:, c * LANES:(c + 1) * LANES]


def _load_row_tiles_2d(ref, rows):
    return jnp.concatenate([ref[pl.ds(c, rows, stride=ROW_TILE[0]), :] for c in range(ROW_TILE[0])], axis=1)


def _store_row_tiles_2d(ref, val, rows):
    for c in range(ROW_TILE[0]):
        ref[pl.ds(c, rows, stride=ROW_TILE[0]), :] = val[:, c * LANES:(c + 1) * LANES]


def _ada_body(c_ref, w_ref, b_ref, o_ref):
    s = _silu(c_ref[...]).astype(BF16)
    o_ref[...] = jnp.dot(s, w_ref[...].astype(BF16), preferred_element_type=F32) + b_ref[...]


def _ada(c_all, w_ada, b_ada):
    tn = 1024
    return pl.pallas_call(
        _ada_body,
        grid=(DEPTH, 6 * D // tn),
        in_specs=[pl.BlockSpec((MOD_ROWS, D), lambda l, j: (0, 0)),
                  pl.BlockSpec((None, D, tn), lambda l, j: (l, 0, j)),
                  pl.BlockSpec((None, 1, tn), lambda l, j: (l, 0, j))],
        out_specs=pl.BlockSpec((None, MOD_ROWS, tn), lambda l, j: (l, 0, j)),
        out_shape=jax.ShapeDtypeStruct((DEPTH, MOD_ROWS, 6 * D), F32),
        compiler_params=_cparams(2), name="ada")(c_all, w_ada, b_ada.reshape(DEPTH, 1, 6 * D))


ZERO_ROWS = 2 * SUB


def _x_specs(x, tm, n_axes):
    row = (lambda i: (i, 0)) if n_axes == 1 else (lambda i, j: (i, 0))
    if not isinstance(x, tuple):
        return [pl.BlockSpec((tm, D), row)], [x]
    zero = (lambda i: (0, 0)) if n_axes == 1 else (lambda i, j: (0, 0))
    return [pl.BlockSpec((tm, D), row), pl.BlockSpec((N_S, D), zero)], list(x)


def _x_rows(x_refs, t, rows):
    if len(x_refs) == 1:
        return x_refs[0][rows, :]
    return jnp.where(t >= N_P // SUB, x_refs[1][...], x_refs[0][rows, :])


def _prenorm_rows(t, x, g_ref, shp_ref, shs_ref, scp_ref, scs_ref):
    y = x * lax.rsqrt(jnp.mean(x * x, axis=-1, keepdims=True) + EPS) * g_ref[...]
    return y * (1.0 + _mod_rows(t, scp_ref, scs_ref)) + _mod_rows(t, shp_ref, shs_ref)


TM_NORM = 640


def _prenorm_body(*refs):
    g_ref, shp_ref, shs_ref, scp_ref, scs_ref, h_ref = refs[-6:]
    for sidx in range(TM_NORM // SUB):
        rows = pl.ds(sidx * SUB, SUB)
        t = pl.program_id(0) * (TM_NORM // SUB) + sidx
        h_ref[rows, :] = _prenorm_rows(t, _x_rows(refs[:-6], t, rows), g_ref, shp_ref, shs_ref, scp_ref,
                                       scs_ref).astype(BF16)


def _prenorm(x, g, mod, layer):
    x_specs, x_args = _x_specs(x, TM_NORM, 1)
    return pl.pallas_call(
        _prenorm_body,
        grid=(N // TM_NORM,),
        in_specs=x_specs + [pl.BlockSpec((1, D), lambda i: (0, 0))] + _mod_specs(layer, 0) + _mod_specs(layer, 1),
        out_specs=pl.BlockSpec((TM_NORM, D), lambda i: (i, 0)),
        out_shape=jax.ShapeDtypeStruct((N, D), BF16),
        compiler_params=_cparams(1), name="prenorm")(*x_args, g.reshape(1, D), mod, mod, mod, mod)


def _mm_body(x_ref, w_ref, o_ref):
    o_ref[...] = jnp.dot(x_ref[...], w_ref[...], preferred_element_type=F32).astype(o_ref.dtype)


def _matmul(x, w_all, layer, tm, tn, out_dtype=F32, name="mm"):
    m, k = x.shape
    n = w_all.shape[2]
    return pl.pallas_call(
        _mm_body,
        grid=(m // tm, n // tn),
        in_specs=[pl.BlockSpec((tm, k), lambda i, j: (i, 0)),
                  pl.BlockSpec((None, k, tn), lambda i, j, l=layer: (l, 0, j))],
        out_specs=pl.BlockSpec((tm, tn), lambda i, j: (i, j)),
        out_shape=jax.ShapeDtypeStruct((m, n), out_dtype),
        compiler_params=_cparams(2), name=name)(x, w_all)


ROWS_LA = 256


def _swap_halves_lanes(x):
    lane = lax.broadcasted_iota(jnp.int32, x.shape, 1)
    return jnp.where((lane % 64) < 32, pltpu.roll(x, 96, 1), pltpu.roll(x, 32, 1))


def _rope_lanes(x, cos, sin_signed):
    parts = []
    for half in range(2):
        xh = x[:, half * LANES:(half + 1) * LANES]
        parts.append(xh * cos + _swap_halves_lanes(xh) * sin_signed)
    return jnp.concatenate(parts, axis=1)


def _la_prompt_body(q_ref, k_ref, v_ref, r_ref, aux_ref, aux2_ref, dec_ref, bias_ref, g_ref,
                    o_ref, st_out_ref, st_ref, *, retention):
    t = pl.program_id(1)

    @pl.when(t == 0)
    def _():
        st_ref[...] = jnp.zeros_like(st_ref)

    n_ch = ROWS_LA // CHUNK
    ri = lax.broadcasted_iota(jnp.int32, (ROWS_LA, ROWS_LA), 0)
    ci = lax.broadcasted_iota(jnp.int32, (ROWS_LA, ROWS_LA), 1)
    causal = (ri >= ci) & ((ri // CHUNK) == (ci // CHUNK))
    scale = DK ** -0.5

    q = q_ref[...]
    k = k_ref[...]
    v = v_ref[...]
    if retention:
        cos = aux_ref[...]
        sin = aux2_ref[...]
        q = _rope_lanes(q, cos, sin)
        k = _rope_lanes(k, cos, sin) * scale
        step = (lax.broadcasted_iota(jnp.int32, (ROWS_LA, HEADS * DK), 0) % CHUNK + 1).astype(F32)
        bc = step * dec_ref[...]
    else:
        q = q * scale
        logit = jnp.dot(aux_ref[...], dec_ref[...], precision=HIGHEST, preferred_element_type=F32) + bias_ref[...]
        la = jax.nn.log_sigmoid(logit) / GATE_TEMP
        bc = jnp.dot(causal.astype(F32), la, precision=HIGHEST, preferred_element_type=F32)
    bl = bc.reshape(n_ch, CHUNK, HEADS * DK)[:, CHUNK - 1:CHUNK, :]
    bl_rows = jnp.broadcast_to(bl, (n_ch, CHUNK, HEADS * DK)).reshape(ROWS_LA, HEADS * DK)
    qd = q * jnp.exp(bc)
    ki = k * jnp.exp(-bc)
    ke = k * jnp.exp(bl_rows - bc)
    ac = jnp.exp(bl)
    outs = []
    for h in range(HEADS):
        ks = slice(h * DK, (h + 1) * DK)
        vs = slice(h * DV, (h + 1) * DV)
        qd_h = qd[:, ks].astype(BF16)
        ki_h = ki[:, ks].astype(BF16)
        ke_h = ke[:, ks].astype(BF16)
        v_h = v[:, vs].astype(BF16)
        sc = lax.dot_general(qd_h, ki_h, (((1,), (1,)), ((), ())), preferred_element_type=F32)
        sc = jnp.where(causal, sc, 0.0)
        o_h = jnp.dot(sc.astype(BF16), v_h, preferred_element_type=F32)
        inter = []
        for c in range(n_ch):
            rows = slice(c * CHUNK, (c + 1) * CHUNK)
            st = st_ref[h]
            inter.append(lax.dot_general(qd_h[rows], st.astype(BF16), (((1,), (1,)), ((), ())),
                                         preferred_element_type=F32))
            kv_t = lax.dot_general(v_h[rows], ke_h[rows], (((0,), (0,)), ((), ())), preferred_element_type=F32)
            st_ref[h] = st * ac[c][:, ks] + kv_t
        o_h = o_h + jnp.concatenate(inter, axis=0)
        outs.append(o_h * lax.rsqrt(jnp.mean(o_h * o_h, axis=-1, keepdims=True) + EPS) * g_ref[:, vs])
    o_ref[...] = (jnp.concatenate(outs, axis=1) * _silu(r_ref[...])).astype(BF16)
    st_out_ref[...] = st_ref[...]


def _la_prompt(p_main, cq, ck, cv, cr, aux, aux2, dec, bias, g, retention):
    nt = T_P // ROWS_LA
    rowblk = lambda b, t: b * nt + t
    if retention:
        aux_specs = [pl.BlockSpec((ROWS_LA, LANES), lambda b, t: (t, 0)),
                     pl.BlockSpec((ROWS_LA, LANES), lambda b, t: (t, 0))]
    else:
        aux_specs = [pl.BlockSpec((ROWS_LA, LANES), lambda b, t: (rowblk(b, t), 0)),
                     pl.BlockSpec((8, LANES), lambda b, t: (0, 0))]
    o, st = pl.pallas_call(
        functools.partial(_la_prompt_body, retention=retention),
        grid=(B_P, nt),
        in_specs=[pl.BlockSpec((ROWS_LA, 256), lambda b, t: (rowblk(b, t), cq // 256)),
                  pl.BlockSpec((ROWS_LA, 256), lambda b, t: (rowblk(b, t), ck // 256)),
                  pl.BlockSpec((ROWS_LA, 512), lambda b, t: (rowblk(b, t), cv // 512)),
                  pl.BlockSpec((ROWS_LA, 512), lambda b, t: (rowblk(b, t), cr // 512))]
        + aux_specs
        + [pl.BlockSpec(dec.shape, lambda b, t: (0, 0)),
           pl.BlockSpec((1, HEADS * DK), lambda b, t: (0, 0)),
           pl.BlockSpec((1, HEADS * DV), lambda b, t: (0, 0))],
        out_specs=[pl.BlockSpec((ROWS_LA, HEADS * DV), lambda b, t: (rowblk(b, t), 0)),
                   pl.BlockSpec((None, HEADS, DV, DK), lambda b, t: (b, 0, 0, 0))],
        out_shape=[jax.ShapeDtypeStruct((N, HEADS * DV), BF16),
                   jax.ShapeDtypeStruct((B_P, HEADS, DV, DK), F32)],
        scratch_shapes=[pltpu.VMEM((HEADS, DV, DK), F32)],
        compiler_params=_cparams(2), name="ret_prompt" if retention else "gla_prompt",
    )(p_main, p_main, p_main, p_main, aux, aux2, dec, bias, g)
    return o, jnp.swapaxes(st, -1, -2)


SAMPLE_TILE = 8


def _la_sample_body(qt_ref, kt_ref, lt_ref, cos_ref, sin_ref, v_ref, r_ref, g_ref, s_ref, *rest, retention):
    o_ref, s_out_ref = rest[-2:]
    scale = DK ** -0.5
    qt = qt_ref[...]
    kt = kt_ref[...]
    if retention:
        def rope(x):
            sw = jnp.concatenate(
                [x[h * DK + (DK // 2) * (1 - j): h * DK + (DK // 2) * (2 - j), :]
                 for h in range(HEADS) for j in range(2)], axis=0)
            return x * cos_ref[...] + sw * sin_ref[...]
        qt = rope(qt)
        kt = rope(kt) * scale
        la = lt_ref[...]
    else:
        qt = qt * scale
        la = jax.nn.log_sigmoid(lt_ref[...]) / GATE_TEMP
    at = jnp.exp(la)
    qd = qt * at
    ki = kt * jnp.exp(-la)
    prod = qd * ki
    v8 = v_ref[...]
    r8 = r_ref[...]
    g = g_ref[...]
    for j in range(SAMPLE_TILE):
        for h in range(HEADS):
            ks = slice(h * DK, (h + 1) * DK)
            vs = slice(h * DV, (h + 1) * DV)
            a_c = jnp.broadcast_to(at[ks, j:j + 1], (DK, DV))
            k_c = jnp.broadcast_to(kt[ks, j:j + 1], (DK, DV))
            q_c = jnp.broadcast_to(qd[ks, j:j + 1], (DK, DV))
            s_c = jnp.broadcast_to(jnp.sum(prod[ks, j:j + 1], axis=0, keepdims=True), (1, DV))
            s0 = s_ref[j, h]
            v_row = v8[j:j + 1, vs]
            s_out_ref[j, h] = a_c * s0 + k_c * v_row
            o_row = s_c * v_row + jnp.sum(q_c * s0, axis=0, keepdims=True)
            o_n = o_row * lax.rsqrt(jnp.mean(o_row * o_row, axis=-1, keepdims=True) + EPS) * g[:, vs]
            o_ref[j:j + 1, vs] = o_n * _silu(r8[j:j + 1, vs])


def _la_sample(qt, kt, lt, cos_t, sin_t, p_main, cv, cr, g, s0_all, layer, s_prev, retention):
    nt = N_S // SAMPLE_TILE
    row0 = N_P // SAMPLE_TILE
    tile = pl.BlockSpec((None, HEADS * DK, LANES), lambda i: (i, 0, 0))
    full = pl.BlockSpec((HEADS * DK, LANES), lambda i: (0, 0))
    lt_spec = full if retention else tile
    return pl.pallas_call(
        functools.partial(_la_sample_body, retention=retention),
        grid=(nt,),
        in_specs=[tile, tile, lt_spec, full, full,
                  pl.BlockSpec((SAMPLE_TILE, 512), lambda i: (row0 + i, cv // 512)),
                  pl.BlockSpec((SAMPLE_TILE, 512), lambda i: (row0 + i, cr // 512)),
                  pl.BlockSpec((1, HEADS * DV), lambda i: (0, 0)),
                  pl.BlockSpec((None, SAMPLE_TILE, HEADS, DK, DV), lambda i, l=layer: (l, i, 0, 0, 0))]
        + ([] if s_prev is None else [pl.BlockSpec(memory_space=pl.ANY)]),
        out_specs=[pl.BlockSpec((SAMPLE_TILE, HEADS * DV), lambda i: (i, 0)),
                   pl.BlockSpec((None, SAMPLE_TILE, HEADS, DK, DV), lambda i, l=layer: (l, i, 0, 0, 0))],
        out_shape=[jax.ShapeDtypeStruct((N_S, HEADS * DV), F32),
                   jax.ShapeDtypeStruct((DEPTH, N_S, HEADS, DK, DV), F32)],
        input_output_aliases={} if s_prev is None else {9: 1},
        compiler_params=_cparams(1), name="ret_sample" if retention else "gla_sample",
    )(qt, kt, lt, cos_t, sin_t, p_main, p_main, g, s0_all, *([] if s_prev is None else [s_prev]))


def _gate_logits_t_body(w_ref, x_ref, b_ref, o_ref):
    o_ref[...] = jnp.dot(w_ref[...], x_ref[...], precision=HIGHEST, preferred_element_type=F32) + b_ref[...]


def _gate_logits_t(w_gate_t, glow_t, b_col):
    nt = N_S // SAMPLE_TILE
    return pl.pallas_call(
        _gate_logits_t_body,
        grid=(nt,),
        in_specs=[pl.BlockSpec((HEADS * DK, LANES), lambda i: (0, 0)),
                  pl.BlockSpec((None, LANES, LANES), lambda i: (i, 0, 0)),
                  pl.BlockSpec((HEADS * DK, LANES), lambda i: (0, 0))],
        out_specs=pl.BlockSpec((None, HEADS * DK, LANES), lambda i: (i, 0, 0)),
        out_shape=jax.ShapeDtypeStruct((nt, HEADS * DK, LANES), F32),
        compiler_params=_cparams(1), name="gate_logits_t")(w_gate_t, glow_t, b_col)


def _to_tiles_t(x):
    c = x.shape[1]
    xt = jnp.swapaxes(x.reshape(N_S // SAMPLE_TILE, SAMPLE_TILE, c), 1, 2)
    return jnp.pad(xt, ((0, 0), (0, 0), (0, LANES - SAMPLE_TILE)))


ROWS_POOL = 512


def _pool_mix(y, w_ref, sc_ref):
    outs = []
    for gi in range(4):
        cs = slice(gi * LANES, (gi + 1) * LANES)
        outs.append(jnp.dot(y[:, cs].astype(BF16), w_ref[gi], preferred_element_type=F32))
    return jnp.concatenate(outs, axis=1) * sc_ref[...]


def _pool_prompt_body(p_ref, halo_ref, w_ref, sc_ref, o_ref):
    t = pl.program_id(1)
    p = p_ref[...]
    halo = jnp.where(t == 0, 0.0, halo_ref[...])
    full = jnp.concatenate([halo, p], axis=0)
    pos = t * ROWS_POOL + lax.broadcasted_iota(jnp.int32, (ROWS_POOL, LANES), 0)
    means = []
    for gi, w in enumerate(POOL_WINDOWS):
        s = full[:, gi * LANES:(gi + 1) * LANES]
        step = 1
        while step < w:
            s = s + pltpu.roll(s, step, 0)
            step *= 2
        win = s[16:, :]
        cnt = jnp.minimum(w, pos + 1).astype(F32)
        means.append(win / cnt)
    y = jnp.concatenate(means, axis=1) - p
    o_ref[...] = _pool_mix(y, w_ref, sc_ref).astype(BF16)


def _pool_prompt(p_main, w_bf, scale):
    nt = T_P // ROWS_POOL
    return pl.pallas_call(
        _pool_prompt_body,
        grid=(B_P, nt),
        in_specs=[pl.BlockSpec((ROWS_POOL, 512), lambda b, t: (b * nt + t, C_PIN // 512)),
                  pl.BlockSpec((16, 512), lambda b, t: (jnp.maximum((b * nt + t) * (ROWS_POOL // 16) - 1, 0),
                                                        C_PIN // 512)),
                  pl.BlockSpec((4, LANES, LANES), lambda b, t: (0, 0, 0)),
                  pl.BlockSpec((1, 512), lambda b, t: (0, 0))],
        out_specs=pl.BlockSpec((ROWS_POOL, 512), lambda b, t: (b * nt + t, 0)),
        out_shape=jax.ShapeDtypeStruct((N, 512), BF16),
        compiler_params=_cparams(2), name="pool_prompt")(p_main, p_main, w_bf, scale)


def _small_sample_body(p_ref, buf_ref, pw_ref, psc_ref, u_ref, sv_ref, sg_ref, sw_ref, sb_ref,
                       ob_ref, od_ref, vn_ref):
    p = p_ref[...]
    means = []
    for gi, w in enumerate(POOL_WINDOWS):
        cs = slice(gi * LANES, (gi + 1) * LANES)
        s = p[:, cs]
        for j in range(1, w):
            s = s + buf_ref[:, POOL_BUF - j, cs]
        means.append(s / float(min(w, PAST_LEN + 1)))
    y = jnp.concatenate(means, axis=1) - p
    ob_ref[...] = _pool_mix(y, pw_ref, psc_ref)
    sv = sv_ref[...]
    vn = sv * lax.rsqrt(jnp.mean(sv * sv, axis=-1, keepdims=True) + EPS) * sg_ref[...]
    vn_ref[...] = vn
    od_ref[...] = u_ref[...] * (sw_ref[...] * vn + sb_ref[...])


def _small_sample(p_main, buf, pw_bf, pscale, sgu_g, sgu_w0, sgu_b0):
    row = N_P // N_S
    col = lambda c: pl.BlockSpec((N_S, 512), lambda i, c=c: (row, c // 512))
    vec = pl.BlockSpec((1, 512), lambda i: (0, 0))
    return pl.pallas_call(
        _small_sample_body,
        grid=(1,),
        in_specs=[col(C_PIN), pl.BlockSpec((N_S, POOL_BUF, 512), lambda i: (0, 0, 0)),
                  pl.BlockSpec((4, LANES, LANES), lambda i: (0, 0, 0)), vec,
                  col(C_SU), col(C_SV), vec, vec, vec],
        out_specs=[pl.BlockSpec((N_S, 512), lambda i: (0, 0))] * 3,
        out_shape=[jax.ShapeDtypeStruct((N_S, 512), F32)] * 3,
        compiler_params=_cparams(1), name="small_sample",
    )(p_main, buf, pw_bf, pscale, p_main, p_main, sgu_g, sgu_w0, sgu_b0)


ROWS_SGU = 512
SGU_CHUNK = 128


def _sgu_prompt_body(u_ref, v_ref, g_ref, w_ref, bt_ref, o_ref):
    ri = lax.broadcasted_iota(jnp.int32, (SGU_CHUNK, SGU_CHUNK), 0)
    ci = lax.broadcasted_iota(jnp.int32, (SGU_CHUNK, SGU_CHUNK), 1)
    causal = ri >= ci
    for c in range(ROWS_SGU // SGU_CHUNK):
        rows = pl.ds(c * SGU_CHUNK, SGU_CHUNK)
        v = v_ref[rows, :]
        vn = (v * lax.rsqrt(jnp.mean(v * v, axis=-1, keepdims=True) + EPS) * g_ref[...]).astype(BF16)
        outs = []
        for gi in range(4):
            cs = slice(gi * LANES, (gi + 1) * LANES)
            w = jnp.where(causal, w_ref[gi], 0.0).astype(BF16)
            mixed = jnp.dot(w, vn[:, cs], preferred_element_type=F32)
            outs.append(mixed + jnp.broadcast_to(bt_ref[:, gi:gi + 1], (SGU_CHUNK, LANES)))
        o_ref[rows, :] = (u_ref[rows, :] * jnp.concatenate(outs, axis=1)).astype(BF16)


def _sgu_prompt(p_main, g, w, b_t):
    return pl.pallas_call(
        _sgu_prompt_body,
        grid=(N_P // ROWS_SGU,),
        in_specs=[pl.BlockSpec((ROWS_SGU, 512), lambda i: (i, C_SU // 512)),
                  pl.BlockSpec((ROWS_SGU, 512), lambda i: (i, C_SV // 512)),
                  pl.BlockSpec((1, 512), lambda i: (0, 0)),
                  pl.BlockSpec((4, SGU_CHUNK, SGU_CHUNK), lambda i: (0, 0, 0)),
                  pl.BlockSpec((SGU_CHUNK, LANES), lambda i: (0, 0))],
        out_specs=pl.BlockSpec((ROWS_SGU, 512), lambda i: (i, 0)),
        out_shape=jax.ShapeDtypeStruct((N, 512), BF16),
        compiler_params=_cparams(1), name="sgu_prompt")(p_main, p_main, g, w, b_t)


TM_MERGE = 640
TN_MERGE = 512


def _merge_body(h_ref, ba_ref, bb_ref, bc_ref, bd_ref, g0, g1, g2, g3, u0, u1, u2, u3,
                c0, c1, c2, c3, o_ref):
    h = h_ref[...]
    acc = None
    for br, gw, uw, gb in ((ba_ref, g0, u0, c0), (bb_ref, g1, u1, c1), (bc_ref, g2, u2, c2), (bd_ref, g3, u3, c3)):
        gate = jax.nn.sigmoid(jnp.dot(h, gw[...], preferred_element_type=F32) + gb[...])
        up = jnp.dot(br[...], uw[...], preferred_element_type=F32)
        acc = gate * up if acc is None else acc + gate * up
    o_ref[...] = acc.astype(BF16)


def _merge(h, branches, w_mg, b_mg, w_br, layer):
    nj = D // TN_MERGE
    row = lambda w: pl.BlockSpec((TM_MERGE, w), lambda i, j: (i, 0))
    gate_w = [pl.BlockSpec((None, D, TN_MERGE), lambda i, j, b=b, l=layer: (l, 0, b * nj + j)) for b in range(4)]
    up_w = [pl.BlockSpec((None, None, 512, TN_MERGE), lambda i, j, b=b, l=layer: (l, b, 0, j)) for b in range(4)]
    gate_b = [pl.BlockSpec((None, 1, TN_MERGE), lambda i, j, b=b, l=layer: (l, 0, b * nj + j)) for b in range(4)]
    return pl.pallas_call(
        _merge_body,
        grid=(N // TM_MERGE, nj),
        in_specs=[row(D)] + [row(512)] * 4 + gate_w + up_w + gate_b,
        out_specs=pl.BlockSpec((TM_MERGE, TN_MERGE), lambda i, j: (i, j)),
        out_shape=jax.ShapeDtypeStruct((N, D), BF16),
        compiler_params=_cparams(2), name="merge",
    )(h, *branches, w_mg, w_mg, w_mg, w_mg, w_br, w_br, w_br, w_br, b_mg, b_mg, b_mg, b_mg)


TM_OUT = 640
TN_OUT = 1024


def _post_value(t, x, y, gn_ref, gp_ref, gs_ref):
    yn = y * lax.rsqrt(jnp.mean(y * y, axis=-1, keepdims=True) + EPS) * gn_ref[...]
    return x + _mod_rows(t, gp_ref, gs_ref) * yn


def _outproj_body(m_ref, w_ref, *refs):
    gn_ref, gp_ref, gs_ref, o_ref, acc_ref = refs[-5:]
    j = pl.program_id(1)
    acc_ref[j] = jnp.dot(m_ref[...], w_ref[...], preferred_element_type=F32)

    @pl.when(j == D // TN_OUT - 1)
    def _():
        for sidx in range(TM_OUT // SUB):
            rows = pl.ds(sidx * SUB, SUB)
            t = pl.program_id(0) * (TM_OUT // SUB) + sidx
            y = jnp.concatenate([acc_ref[c, rows, :] for c in range(D // TN_OUT)], axis=1)
            o_ref[rows, :] = _post_value(t, _x_rows(refs[:-5], t, rows), y, gn_ref, gp_ref, gs_ref)


def _outproj(merged, w_out, x, g_post, mod, layer):
    mspec = [pl.BlockSpec((None, 8, D), lambda i, j, l=layer: (l, 0, 2)),
             pl.BlockSpec((None, SUB, D), lambda i, j, l=layer: (l, 1, 2))]
    x_specs, x_args = _x_specs(x, TM_OUT, 2)
    return pl.pallas_call(
        _outproj_body,
        grid=(N // TM_OUT, D // TN_OUT),
        in_specs=[pl.BlockSpec((TM_OUT, D), lambda i, j: (i, 0)),
                  pl.BlockSpec((None, D, TN_OUT), lambda i, j, l=layer: (l, 0, j))] + x_specs
        + [pl.BlockSpec((1, D), lambda i, j: (0, 0))] + mspec,
        out_specs=pl.BlockSpec((TM_OUT, D), lambda i, j: (i, 0)),
        out_shape=jax.ShapeDtypeStruct((N, D), F32),
        scratch_shapes=[pltpu.VMEM((D // TN_OUT, TM_OUT, TN_OUT), F32)],
        compiler_params=_cparams(2), name="outproj",
    )(merged, w_out, *x_args, g_post.reshape(1, D), mod, mod)


def _router_body(lg_ref, b_ref, eid_ref, pos_ref, wt_ref, cnt_ref, run_ref):
    i = pl.program_id(0)

    @pl.when(i == 0)
    def _():
        run_ref[...] = jnp.zeros_like(run_ref)

    ng, gs = 8, N_EXPERTS // 8
    neg = -jnp.inf
    scores = jax.nn.sigmoid(lg_ref[...].T[:N_EXPERTS, :])
    sel = scores + b_ref[...]
    sel3 = sel.reshape(ng, gs, SUB)
    sub3 = lax.broadcasted_iota(jnp.int32, (ng, gs, SUB), 1)
    gmax = jnp.max(sel3, axis=1, keepdims=True)
    first = jnp.min(jnp.where(sel3 == gmax, sub3, gs), axis=1, keepdims=True)
    gmax2 = jnp.max(jnp.where(sub3 == first, neg, sel3), axis=1, keepdims=True)
    gscore = (gmax + gmax2).reshape(ng, SUB)
    gidx = lax.broadcasted_iota(jnp.int32, (ng, SUB), 0)
    grank = jnp.zeros((ng, SUB), jnp.int32)
    for s in range(1, ng):
        other = pltpu.roll(gscore, s, 0)
        lower = gidx >= s
        grank += ((other > gscore) | ((other == gscore) & lower)).astype(jnp.int32)
    keep = jnp.broadcast_to((grank < 4).reshape(ng, 1, SUB), (ng, gs, SUB))
    masked = jnp.where(keep, sel3, neg).reshape(N_EXPERTS, SUB)
    eidx = lax.broadcasted_iota(jnp.int32, (N_EXPERTS, SUB), 0)
    rank = jnp.zeros((N_EXPERTS, SUB), jnp.int32)
    for s in range(1, N_EXPERTS):
        other = pltpu.roll(masked, s, 0)
        lower = eidx >= s
        rank += ((other > masked) | ((other == masked) & lower)).astype(jnp.int32)
    chosen = rank < TOP_K
    w_sel = jnp.where(chosen, scores, 0.0)
    w_sel = w_sel / jnp.sum(w_sel, axis=0, keepdims=True) * ROUTED_SCALE
    ri = lax.broadcasted_iota(jnp.int32, (SUB, SUB), 0)
    ci = lax.broadcasted_iota(jnp.int32, (SUB, SUB), 1)
    onehot = chosen.astype(BF16)
    pos = jnp.dot(onehot, (ri < ci).astype(BF16), preferred_element_type=F32) + run_ref[...]
    run_ref[...] = run_ref[...] + jnp.sum(chosen.astype(F32), axis=1, keepdims=True)
    cnt_ref[...] = run_ref[...]
    eidx_f = eidx.astype(F32)
    rows_e, rows_p, rows_w = [], [], []
    for kk in range(TOP_K):
        m = chosen & (rank == kk)
        rows_e.append(jnp.sum(jnp.where(m, eidx_f, 0.0), axis=0, keepdims=True))
        rows_p.append(jnp.sum(jnp.where(m, pos, 0.0), axis=0, keepdims=True))
        rows_w.append(jnp.sum(jnp.where(m, w_sel, 0.0), axis=0, keepdims=True))
    eid_ref[...] = jnp.concatenate(rows_e, axis=0).astype(jnp.int32)
    pos_ref[...] = jnp.concatenate(rows_p, axis=0).astype(jnp.int32)
    wt_ref[...] = jnp.concatenate(rows_w, axis=0)


def _router(logits, rb_col):
    tile = pl.BlockSpec((TOP_K, SUB), lambda i: (0, i))
    return pl.pallas_call(
        _router_body,
        grid=(N // SUB,),
        in_specs=[pl.BlockSpec((SUB, LANES), lambda i: (i, 0)),
                  pl.BlockSpec((N_EXPERTS, SUB), lambda i: (0, 0))],
        out_specs=[tile, tile, tile, pl.BlockSpec((N_EXPERTS, SUB), lambda i: (0, 0))],
        out_shape=[jax.ShapeDtypeStruct((TOP_K, N), jnp.int32), jax.ShapeDtypeStruct((TOP_K, N), jnp.int32),
                   jax.ShapeDtypeStruct((TOP_K, N), F32), jax.ShapeDtypeStruct((N_EXPERTS, SUB), F32)],
        scratch_shapes=[pltpu.VMEM((N_EXPERTS, SUB), F32)],
        compiler_params=_cparams(1), name="router")(logits, rb_col)


SC_CORES, SC_SUBCORES = 2, 16
SC_WORKERS = SC_CORES * SC_SUBCORES
SC_LANES = 16
SC_SCAN = N_ASSIGN // SC_WORKERS


def _sc_mesh():
    return plsc.VectorSubcoreMesh(core_axis_name="c", subcore_axis_name="s",
                                  num_cores=SC_CORES, num_subcores=SC_SUBCORES)


def _sc_worker_base(per_w):
    return (lax.axis_index("s") * SC_CORES + lax.axis_index("c")) * per_w


def _sc_chunk(per_w):
    return max(c for c in (8, 16, 24, 32) if per_w % c == 0)


def _sc_gather_rows(table_hbm, out_hbm, idx_v, rows_v, gsem, wsem, base, per_w):
    chunk = _sc_chunk(per_w)
    n_ch = per_w // chunk

    def gather(j, p):
        off = pl.multiple_of(j * chunk, 8)
        return pltpu.make_async_copy(table_hbm.at[idx_v.at[pl.ds(off, chunk)]], rows_v.at[p], gsem.at[p])

    def write(j, p):
        off = pl.multiple_of(j * chunk, 8)
        return pltpu.make_async_copy(rows_v.at[p], out_hbm.at[pl.ds(base + off, chunk)], wsem.at[p])

    def when(cond, fn):
        if isinstance(cond, bool):
            if cond:
                fn()
        else:
            pl.when(cond)(fn)

    def step(j, p):
        gather(j, p).wait()
        when(j >= 1, lambda: write(j - 1, 1 - p).wait())
        when(j + 1 < n_ch, lambda: gather(j + 1, 1 - p).start())
        write(j, p).start()

    gather(0, 0).start()

    @pl.loop(0, n_ch - n_ch % 2, step=2)
    def _(j0):
        for p in range(2):
            step(j0 + p, p)

    if n_ch % 2:
        step(n_ch - 1, 0)
    write(n_ch - 1, (n_ch - 1) % 2).wait()


def _sc_row_scratch(per_w):
    return [pltpu.VMEM((2, _sc_chunk(per_w)) + ROW_TILE, jnp.int32),
            pltpu.SemaphoreType.DMA((2,)), pltpu.SemaphoreType.DMA((2,))]


def _sc_gather(table, idx):
    n_out = idx.shape[0]
    per_w = n_out // SC_WORKERS
    assert per_w * SC_WORKERS == n_out and per_w % 8 == 0

    def body(table_hbm, idx_hbm, out_hbm, idx_v, rows_v, gsem, wsem):
        base = _sc_worker_base(per_w)
        pltpu.sync_copy(idx_hbm.at[pl.ds(base, per_w)], idx_v)
        _sc_gather_rows(table_hbm, out_hbm, idx_v, rows_v, gsem, wsem, base, per_w)

    return pl.kernel(
        body, out_type=jax.ShapeDtypeStruct((n_out,) + ROW_TILE, jnp.int32), mesh=_sc_mesh(),
        scratch_types=[pltpu.VMEM((per_w,), jnp.int32)] + _sc_row_scratch(per_w), name="sc_gather")(table, idx)


PARTS = ((0, 54), (54, 108), (162, 162))
assert sum(n for _, n in PARTS) == N_BLOCKS and all(PARTS[i][0] + PARTS[i][1] == PARTS[i + 1][0]
                                                    for i in range(len(PARTS) - 1))


def _sc_dispatch(table, slots, blk0, n_blk):
    part_slots = n_blk * EXP_BLOCK
    per_w = part_slots // SC_WORKERS
    per_w_pad = -(-per_w // SC_LANES) * SC_LANES
    assert per_w * SC_WORKERS == part_slots and per_w % 8 == 0 and SC_SCAN % SC_LANES == 0
    assert ZERO_ROWS & (ZERO_ROWS - 1) == 0

    def body(table_hbm, slots_hbm, out_hbm, idx_v, sl_v, rows_v, gsem, wsem):
        local = _sc_worker_base(per_w)
        base = blk0 * EXP_BLOCK + local
        lane = lax.iota(jnp.int32, SC_LANES)

        @pl.loop(0, per_w_pad // SC_LANES)
        def _(j):
            off = pl.multiple_of(j * SC_LANES, SC_LANES)
            idx_v[pl.ds(off, SC_LANES)] = N + ((base + off + lane) & (ZERO_ROWS - 1))

        @pl.loop(0, N_ASSIGN // SC_SCAN)
        def _(c):
            pltpu.sync_copy(slots_hbm.at[pl.ds(pl.multiple_of(c * SC_SCAN, 8), SC_SCAN)], sl_v)

            @pl.loop(0, SC_SCAN // SC_LANES)
            def _(j):
                off = pl.multiple_of(j * SC_LANES, SC_LANES)
                loc = sl_v[pl.ds(off, SC_LANES)] - base
                mine = (loc >= 0) & (loc < per_w)
                tok = lax.shift_right_logical(c * SC_SCAN + off + lane, 3)
                plsc.store_scatter(idx_v, [jnp.where(mine, loc, 0)], tok, mask=mine)

        _sc_gather_rows(table_hbm, out_hbm, idx_v, rows_v, gsem, wsem, local, per_w)

    return pl.kernel(
        body, out_type=jax.ShapeDtypeStruct((part_slots,) + ROW_TILE, jnp.int32), mesh=_sc_mesh(),
        scratch_types=[pltpu.VMEM((per_w_pad,), jnp.int32), pltpu.VMEM((SC_SCAN,), jnp.int32)]
        + _sc_row_scratch(per_w),
        compiler_params=pltpu.CompilerParams(needs_layout_passes=False),
        name="sc_dispatch")(table, slots)


def _experts_body(be_ref, first_ref, par_ref, next_ref, nextblk_ref, nused_ref, x_ref, w1_hbm, w3_hbm, w2_hbm,
                  *rest, layer, blk0, n_blk):
    y_ref, w1f, w3f, w2f, w1b, w3b, w2b, sem = rest[-8:]
    i = pl.program_id(0)
    b = blk0 + i
    used = b < nused_ref[0]

    def copies(e, slot):
        return (pltpu.make_async_copy(w1_hbm.at[layer, e], w1f.at[slot], sem.at[0, slot]),
                pltpu.make_async_copy(w3_hbm.at[layer, e], w3f.at[slot], sem.at[1, slot]),
                pltpu.make_async_copy(w2_hbm.at[layer, e], w2f.at[slot], sem.at[2, slot]))

    @pl.when(used & (i == 0))
    def _():
        for c in copies(be_ref[b], par_ref[b]):
            c.start()

    @pl.when(used & ((i == 0) | (first_ref[b] == 1)))
    def _():
        slot = par_ref[b]
        for c in copies(be_ref[b], slot):
            c.wait()

        @pl.when((next_ref[b] >= 0) & (nextblk_ref[b] < blk0 + n_blk))
        def _():
            for c in copies(next_ref[b], 1 - slot):
                c.start(priority=1)
        w1b[...] = w1f[slot].astype(BF16)
        w3b[...] = w3f[slot].astype(BF16)
        w2b[...] = w2f[slot].astype(BF16)

    @pl.when(used)
    def _():
        lo, hi = _unpack_bf16_pair(_load_row_tiles_2d(x_ref, EXP_BLOCK))
        lo = lo.astype(BF16)
        hi = hi.astype(BF16)
        half = D // 2
        h1 = (jnp.dot(lo, w1b[:half, :], preferred_element_type=F32)
              + jnp.dot(hi, w1b[half:, :], preferred_element_type=F32))
        h3 = (jnp.dot(lo, w3b[:half, :], preferred_element_type=F32)
              + jnp.dot(hi, w3b[half:, :], preferred_element_type=F32))
        hid = (_silu(h1) * h3).astype(BF16)
        y = jnp.dot(hid, w2b[...], preferred_element_type=F32)
        _store_row_tiles_2d(y_ref, _pack_bf16_pair(y[:, :half], y[:, half:]), EXP_BLOCK)

    @pl.when(jnp.logical_not(used))
    def _():
        y_ref[...] = jnp.zeros_like(y_ref)


def _experts(ctl, xs_part, w1, w3, w2, layer, blk0, n_blk, ys_prev):
    def x_blk(i, *refs):
        n_here = jnp.clip(refs[-1][0] - blk0, 1, n_blk)
        return (jnp.minimum(i, n_here - 1), 0)
    any_spec = pl.BlockSpec(memory_space=pl.ANY)
    in_specs = [pl.BlockSpec((EXP_BLOCK * ROW_TILE[0], LANES), x_blk), any_spec, any_spec, any_spec]
    args = [xs_part.reshape(n_blk * EXP_BLOCK * ROW_TILE[0], LANES), w1, w3, w2]
    aliases = {}
    if ys_prev is not None:
        in_specs.append(any_spec)
        args.append(ys_prev)
        aliases = {len(ctl) + 4: 0}
    grid_spec = pltpu.PrefetchScalarGridSpec(
        num_scalar_prefetch=len(ctl),
        grid=(n_blk,),
        in_specs=in_specs,
        out_specs=pl.BlockSpec((EXP_BLOCK * ROW_TILE[0], LANES), lambda i, *refs: (blk0 + i, 0)),
        scratch_shapes=[pltpu.VMEM((2, D, D_EXPERT), F32), pltpu.VMEM((2, D, D_EXPERT), F32),
                        pltpu.VMEM((2, D_EXPERT, D), F32),
                        pltpu.VMEM((D, D_EXPERT), BF16), pltpu.VMEM((D, D_EXPERT), BF16),
                        pltpu.VMEM((D_EXPERT, D), BF16), pltpu.SemaphoreType.DMA((3, 2))])
    return pl.pallas_call(
        functools.partial(_experts_body, layer=layer, blk0=blk0, n_blk=n_blk), grid_spec=grid_spec,
        out_shape=jax.ShapeDtypeStruct((L_SLOTS * ROW_TILE[0], LANES), jnp.int32),
        input_output_aliases=aliases,
        compiler_params=_cparams(1), name="experts")(*ctl, *args)


TM_FFN = 640


def _ffn_pre_body(x_ref, g_ref, shp_ref, shs_ref, scp_ref, scs_ref, rw_ref, hb_ref, lg_ref, hp_ref):
    i = pl.program_id(0)

    @pl.when(i < N // TM_FFN)
    def _():
        for sidx in range(TM_FFN // SUB):
            rows = pl.ds(sidx * SUB, SUB)
            h = _prenorm_rows(i * (TM_FFN // SUB) + sidx, x_ref[rows, :], g_ref, shp_ref, shs_ref, scp_ref, scs_ref)
            lg_ref[rows, :] = jnp.dot(h, rw_ref[...], precision=HIGHEST, preferred_element_type=F32)
            packed = _pack_bf16_pair(h[:, :D // 2], h[:, D // 2:])
            for c in range(ROW_TILE[0]):
                hp_ref[pl.ds(sidx * SUB * ROW_TILE[0] + c, SUB, stride=ROW_TILE[0]), :] = (
                    packed[:, c * LANES:(c + 1) * LANES])
            hb_ref[rows, :] = h.astype(BF16)

    @pl.when(i >= N // TM_FFN)
    def _():
        hp_ref[...] = jnp.zeros_like(hp_ref)


def _ffn_pre(x, g, mod, layer, rw):
    last = N // TM_FFN - 1
    row = lambda i: (jnp.minimum(i, last), 0)
    mspec = lambda part, rows, blk: pl.BlockSpec((None, rows, D), lambda i, l=layer, p=part, b=blk: (l, b, p))
    return pl.pallas_call(
        _ffn_pre_body,
        grid=(N // TM_FFN + 1,),
        in_specs=[pl.BlockSpec((TM_FFN, D), row), pl.BlockSpec((1, D), lambda i: (0, 0)),
                  mspec(3, 8, 0), mspec(3, SUB, 1), mspec(4, 8, 0), mspec(4, SUB, 1),
                  pl.BlockSpec((None, D, LANES), lambda i, l=layer: (l, 0, 0))],
        out_specs=[pl.BlockSpec((TM_FFN, D), row), pl.BlockSpec((TM_FFN, LANES), row),
                   pl.BlockSpec((TM_FFN * ROW_TILE[0], LANES), lambda i: (i, 0))],
        out_shape=[jax.ShapeDtypeStruct((N, D), BF16), jax.ShapeDtypeStruct((N, LANES), F32),
                   jax.ShapeDtypeStruct(((N + TM_FFN) * ROW_TILE[0], LANES), jnp.int32)],
        compiler_params=_cparams(1), name="ffn_pre")(x, g.reshape(1, D), mod, mod, mod, mod, rw)


TM_SHARED = 640


def _shared_body(h_ref, w13_ref, w2_ref, after_ref, o_ref):
    up = jnp.dot(h_ref[...], w13_ref[...], preferred_element_type=F32)
    hid = (_silu(up[:, :D_EXPERT]) * up[:, D_EXPERT:]).astype(BF16)
    o_ref[...] = jnp.dot(hid, w2_ref[...], preferred_element_type=F32)


def _shared(h, w13, w2, layer, after):
    return pl.pallas_call(
        _shared_body,
        grid=(N // TM_SHARED,),
        in_specs=[pl.BlockSpec((TM_SHARED, D), lambda i: (i, 0)),
                  pl.BlockSpec((None, D, 2 * D_EXPERT), lambda i, l=layer: (l, 0, 0)),
                  pl.BlockSpec((None, D_EXPERT, D), lambda i, l=layer: (l, 0, 0)),
                  pl.BlockSpec(memory_space=pl.ANY)],
        out_specs=pl.BlockSpec((TM_SHARED, D), lambda i: (i, 0)),
        out_shape=jax.ShapeDtypeStruct((N, D), F32),
        compiler_params=_cparams(1), name="shared")(h, w13, w2, after)


def _combine_body(g_ref, wt_ref, sh_ref, x_ref, gn_ref, gp_ref, gs_ref, *rest, tile0, n_out, final):
    outs = rest[-n_out:]
    half = D // 2
    acc_lo = sh_ref[:, :half]
    acc_hi = sh_ref[:, half:]
    wt = wt_ref[...]
    per_tok = TOP_K * ROW_TILE[0]
    for k in range(TOP_K):
        packed = jnp.concatenate([g_ref[pl.ds(k * ROW_TILE[0] + c, SUB, stride=per_tok), :]
                                  for c in range(ROW_TILE[0])], axis=1)
        lo, hi = _unpack_bf16_pair(packed)
        w_c = wt[:, k:k + 1]
        acc_lo = acc_lo + w_c * lo
        acc_hi = acc_hi + w_c * hi
    t = tile0 + pl.program_id(0)
    val = _post_value(t, x_ref[...], jnp.concatenate([acc_lo, acc_hi], axis=1), gn_ref, gp_ref, gs_ref)
    if not final:
        outs[0][...] = val
    else:
        @pl.when(t < N_P // SUB)
        def _():
            outs[0][...] = val
        if n_out == 2:
            @pl.when(t >= N_P // SUB)
            def _():
                outs[1][...] = val


def _combine(gathered, wts, shared, x, g_post, mod, layer, tile0, n_tiles, out_prev, final):
    per_tok = TOP_K * ROW_TILE[0]
    row = lambda i: (tile0 + i, 0)
    in_specs = [pl.BlockSpec((SUB * per_tok, LANES), lambda i: (i, 0)),
                pl.BlockSpec((SUB, LANES), row), pl.BlockSpec((SUB, D), row), pl.BlockSpec((SUB, D), row),
                pl.BlockSpec((1, D), lambda i: (0, 0)),
                pl.BlockSpec((None, 8, D), lambda i, l=layer: (l, 0, 5)),
                pl.BlockSpec((None, SUB, D), lambda i, l=layer: (l, 1, 5))]
    args = [gathered.reshape(n_tiles * SUB * per_tok, LANES), wts, shared, x, g_post.reshape(1, D), mod, mod]
    aliases = {}
    if out_prev is not None:
        in_specs.append(pl.BlockSpec(memory_space=pl.ANY))
        args.append(out_prev)
        aliases = {len(args) - 1: 0}
    if not final:
        out_specs = [pl.BlockSpec((SUB, D), row)]
        out_shape = [jax.ShapeDtypeStruct((N, D), F32)]
    else:
        last_p = N_P // SUB - 1
        out_specs = [pl.BlockSpec((SUB, D), lambda i: (jnp.minimum(tile0 + i, last_p), 0))]
        out_shape = [jax.ShapeDtypeStruct((N_P, D), F32)]
        if tile0 + n_tiles > N_P // SUB:
            out_specs.append(pl.BlockSpec((N_S, D), lambda i: (0, 0)))
            out_shape.append(jax.ShapeDtypeStruct((N_S, D), F32))
    return pl.pallas_call(
        functools.partial(_combine_body, tile0=tile0, n_out=len(out_shape), final=final),
        grid=(n_tiles,),
        in_specs=in_specs, out_specs=out_specs, out_shape=out_shape,
        input_output_aliases=aliases,
        compiler_params=_cparams(1), name="combine")(*args)


def _slots_body(start_ref, eid_ref, pos_ref, o_ref):
    eid = eid_ref[...]
    acc = pos_ref[...]
    for e in range(N_EXPERTS):
        acc = acc + jnp.where(eid == e, start_ref[e], 0)
    o_ref[...] = acc


def _slots(pad_start, eid, pos):
    grid_spec = pltpu.PrefetchScalarGridSpec(
        num_scalar_prefetch=1, grid=(1,),
        in_specs=[pl.BlockSpec((TOP_K, N), lambda i, s: (0, 0)), pl.BlockSpec((TOP_K, N), lambda i, s: (0, 0))],
        out_specs=pl.BlockSpec((TOP_K, N), lambda i, s: (0, 0)))
    return pl.pallas_call(_slots_body, grid_spec=grid_spec,
                          out_shape=jax.ShapeDtypeStruct((TOP_K, N), jnp.int32),
                          compiler_params=_cparams(1), name="slots")(pad_start, eid, pos)


def _put_sample_rows_body(*refs):
    n = len(refs) // 3
    for src, dst in zip(refs[:n], refs[2 * n:]):
        dst[...] = src[...].astype(BF16)


def _put_sample_rows(sample_rows, full):
    n = len(full)
    return pl.pallas_call(
        _put_sample_rows_body,
        grid=(1,),
        in_specs=[pl.BlockSpec((N_S, 512), lambda i: (0, 0))] * n + [pl.BlockSpec(memory_space=pl.ANY)] * n,
        out_specs=[pl.BlockSpec((N_S, 512), lambda i: (N_P // N_S, 0))] * n,
        out_shape=[jax.ShapeDtypeStruct((N, 512), BF16)] * n,
        input_output_aliases={n + k: k for k in range(n)},
        compiler_params=_cparams(1), name="put_sample_rows")(*sample_rows, *full)


COMBINE_RANGES = ((0, 33), (33, 32))

def _prepare_weights(w_in, w_merge_gate, w_branch, w_out, router_w, shared_w1, shared_w3, shared_w2):
    return dict(
        w_main=jnp.concatenate([w_in[:, :, :1536], w_in[:, :, 1552:]], axis=2).astype(BF16),
        w_low=jnp.pad(w_in[:, :, 1536:1552], ((0, 0), (0, 0), (0, LANES - 16))).astype(BF16),
        w_mg=w_merge_gate.astype(BF16), w_br=w_branch.astype(BF16), w_out=w_out.astype(BF16),
        rw=jnp.pad(router_w, ((0, 0), (0, 0), (0, LANES - N_EXPERTS))),
        w13=jnp.concatenate([shared_w1, shared_w3], axis=2).astype(BF16), sw2=shared_w2.astype(BF16))


def _rope_tables(pos):
    half = DK // 2
    inv = ROPE_BASE ** (-jnp.arange(half, dtype=F32) / half)
    ang = pos.astype(F32)[:, None] * inv[None, :]
    cos = jnp.cos(ang)
    sin = jnp.sin(ang)
    return jnp.concatenate([cos, cos], axis=1), jnp.concatenate([-sin, sin], axis=1)


def _layer(l, x, mod, s_gla, s_pool, s_ret, wts, prep, final, prev_gla, prev_ret):
    (norm_mix_pre, norm_mix_post, norm_ffn_pre, norm_ffn_post, w_in, w_gla_gate, b_gla_gate, gla_norm,
     pool_w, pool_scale, ret_norm, sgu_norm, sgu_w, sgu_b, w_branch, w_merge_gate, b_merge_gate, w_out,
     router_w, router_bias, expert_w1, expert_w3, expert_w2, shared_w1, shared_w3, shared_w2) = wts

    h = _prenorm(x, norm_mix_pre[l], mod, l)
    p_main = _matmul(h, prep["w_main"], l, 1664, 512, name="inproj")
    p_low = _matmul(h, prep["w_low"], l, 1664, LANES, name="inproj_low")

    w_gate_pad = jnp.pad(w_gla_gate[l], ((0, LANES - 16), (0, 0)))
    b_gate = b_gla_gate[l].reshape(1, HEADS * DK)
    log_gamma = jnp.log1p(-jnp.exp2(-5.0 - jnp.arange(HEADS, dtype=F32)))
    dec_row = jnp.repeat(log_gamma, DK).reshape(1, HEADS * DK)
    cos_p, sin_p = _rope_tables(jnp.arange(T_P))
    cos_p = jnp.tile(cos_p, (1, 2))
    sin_p = jnp.tile(sin_p, (1, 2))
    g_gla = gla_norm[l].reshape(1, HEADS * DV)
    g_ret = ret_norm[l].reshape(1, HEADS * DV)

    oa_p, gla_p = _la_prompt(p_main, C_GQ, C_GK, C_GV, C_GR, p_low, p_low, w_gate_pad, b_gate, g_gla, False)
    oc_p, ret_p = _la_prompt(p_main, C_RQ, C_RK, C_RV, C_RG, cos_p, sin_p, dec_row, b_gate, g_ret, True)
    pw_bf = pool_w[l].astype(BF16)
    pscale = pool_scale[l].reshape(1, 512)
    ob_p = _pool_prompt(p_main, pw_bf, pscale)
    sgu_g = sgu_norm[l].reshape(1, 512)
    od_p = _sgu_prompt(p_main, sgu_g, sgu_w[l], jnp.pad(sgu_b[l].T, ((0, 0), (0, LANES - 4))))

    ps = p_main[N_P:]
    q_t = _to_tiles_t(ps[:, C_GQ:C_GQ + 256])
    k_t = _to_tiles_t(ps[:, C_GK:C_GK + 256])
    glow_t = jnp.pad(_to_tiles_t(p_low[N_P:, :16]), ((0, 0), (0, LANES - 16), (0, 0)))
    w_gate_t = jnp.pad(w_gla_gate[l].T, ((0, 0), (0, LANES - 16)))
    b_col = jnp.broadcast_to(b_gla_gate[l][:, None], (HEADS * DK, LANES))
    logit_t = _gate_logits_t(w_gate_t, glow_t, b_col)
    dummy = jnp.zeros((HEADS * DK, LANES), F32)
    oa_s, gla_s = _la_sample(q_t, k_t, logit_t, dummy, dummy, p_main, C_GV, C_GR, g_gla, s_gla, l, prev_gla, False)
    cos_s, sin_s = _rope_tables(jnp.full((1,), PAST_LEN))
    cos_c = jnp.broadcast_to(jnp.tile(cos_s[0], HEADS)[:, None], (HEADS * DK, LANES))
    sin_c = jnp.broadcast_to(jnp.tile(sin_s[0], HEADS)[:, None], (HEADS * DK, LANES))
    dec_c = jnp.broadcast_to(jnp.repeat(log_gamma, DK)[:, None], (HEADS * DK, LANES))
    rq_t = _to_tiles_t(ps[:, C_RQ:C_RQ + 256])
    rk_t = _to_tiles_t(ps[:, C_RK:C_RK + 256])
    oc_s, ret_s = _la_sample(rq_t, rk_t, dec_c, cos_c, sin_c, p_main, C_RV, C_RG, g_ret, s_ret, l, prev_ret, True)
    sgu_w0 = jnp.repeat(sgu_w[l][:, 0, 0], LANES).reshape(1, 512)
    sgu_b0 = jnp.repeat(sgu_b[l][:, 0], LANES).reshape(1, 512)
    ob_s, od_s, vn_s = _small_sample(p_main, s_pool[l], pw_bf, pscale, sgu_g, sgu_w0, sgu_b0)
    pool_p = jnp.stack([p_main[(b + 1) * T_P - POOL_BUF:(b + 1) * T_P, C_PIN:C_PIN + 512] for b in range(B_P)])
    pool_s = jnp.concatenate([s_pool[l][:, 1:], ps[:, None, C_PIN:C_PIN + 512]], axis=1)

    branches = _put_sample_rows([oa_s, ob_s, oc_s, od_s], [oa_p, ob_p, oc_p, od_p])
    merged = _merge(h, branches, prep["w_mg"], b_merge_gate.reshape(DEPTH, 1, 4 * D), prep["w_br"], l)
    x = _outproj(merged, prep["w_out"], x, norm_mix_post[l], mod, l)

    rb = jnp.broadcast_to(router_bias[l][:, None], (N_EXPERTS, SUB))
    h2, logits, h2_packed = _ffn_pre(x, norm_ffn_pre[l], mod, l, prep["rw"])
    eid, pos, wt, counts = _router(logits, rb)
    counts = counts[:, 0].astype(jnp.int32)
    padded = (counts + EXP_BLOCK - 1) // EXP_BLOCK * EXP_BLOCK
    pad_end = jnp.cumsum(padded)
    pad_start = pad_end - padded
    nused = (pad_end[-1] // EXP_BLOCK).astype(jnp.int32).reshape(1)
    blk_row = jnp.arange(N_BLOCKS, dtype=jnp.int32) * EXP_BLOCK
    block_e = jnp.minimum(jnp.sum((blk_row[:, None] >= pad_end[None, :]).astype(jnp.int32), axis=1),
                          N_EXPERTS - 1)
    first = jnp.concatenate([jnp.ones((1,), jnp.int32), (block_e[1:] != block_e[:-1]).astype(jnp.int32)])
    first = jnp.where(blk_row < pad_end[-1], first, 0)
    par = (jnp.cumsum(first) - 1) % 2
    live = jnp.where(padded > 0, jnp.arange(N_EXPERTS), N_EXPERTS)
    after = jnp.concatenate([lax.cummin(live, reverse=True)[1:], jnp.full((1,), N_EXPERTS)])
    of_block = block_e[:, None] == jnp.arange(N_EXPERTS)
    next_e = jnp.sum(jnp.where(of_block, jnp.where(after < N_EXPERTS, after, -1), 0), axis=1).astype(jnp.int32)
    next_blk = jnp.sum(jnp.where(of_block, pad_end // EXP_BLOCK, 0), axis=1).astype(jnp.int32)
    slots = _slots(pad_start.astype(jnp.int32), eid, pos).T.reshape(N_ASSIGN)
    wt = jnp.pad(wt.T, ((0, 0), (0, LANES - TOP_K)))
    table = h2_packed.reshape((N + TM_FFN,) + ROW_TILE)
    ctl = (block_e, first, par.astype(jnp.int32), next_e, next_blk, nused)
    xs_parts = [_sc_dispatch(table, slots, blk0, n_blk) for blk0, n_blk in PARTS]
    shared = _shared(h2, prep["w13"], prep["sw2"], l, slots)
    ys = None
    for xs_part, (blk0, n_blk) in zip(xs_parts, PARTS):
        ys = _experts(ctl, xs_part, expert_w1, expert_w3, expert_w2, l, blk0, n_blk, ys)
    ys = ys.reshape((L_SLOTS,) + ROW_TILE)
    outs = [None]
    for tile0, n_tiles in COMBINE_RANGES:
        a0, a1 = tile0 * SUB * TOP_K, (tile0 + n_tiles) * SUB * TOP_K
        outs = _combine(_sc_gather(ys, slots[a0:a1]), wt, shared, x, norm_ffn_post[l], mod, l, tile0, n_tiles,
                        outs[0], final)
    x = tuple(outs) if final else outs[0]
    return x, (gla_p, pool_p, pool_s, ret_p, vn_s), gla_s, ret_s


def kernel(x_prompt, x_sample, c_prompt, c_sample, state_gla, state_pool, state_ret, w_ada, b_ada, norm_mix_pre, norm_mix_post, norm_ffn_pre, norm_ffn_post, w_in, w_gla_gate, b_gla_gate, gla_norm, pool_w, pool_scale, ret_norm, sgu_norm, sgu_w, sgu_b, w_branch, w_merge_gate, b_merge_gate, w_out, router_w, router_bias, expert_w1, expert_w3, expert_w2, shared_w1, shared_w3, shared_w2):
    wts = (norm_mix_pre, norm_mix_post, norm_ffn_pre, norm_ffn_post, w_in, w_gla_gate, b_gla_gate, gla_norm,
           pool_w, pool_scale, ret_norm, sgu_norm, sgu_w, sgu_b, w_branch, w_merge_gate, b_merge_gate, w_out,
           router_w, router_bias, expert_w1, expert_w3, expert_w2, shared_w1, shared_w3, shared_w2)
    c_all = jnp.zeros((MOD_ROWS, D), F32).at[:B_P].set(c_prompt).at[SUB:SUB + N_S].set(c_sample)
    mod = _ada(c_all, w_ada, b_ada)
    x = (x_prompt.reshape(N_P, D), x_sample.reshape(N_S, D))
    prep = _prepare_weights(w_in, w_merge_gate, w_branch, w_out, router_w, shared_w1, shared_w3, shared_w2)
    per_layer = []
    gla_s = ret_s = None
    for l in range(DEPTH):
        x, states, gla_s, ret_s = _layer(l, x, mod, state_gla, state_pool, state_ret, wts, prep, l == DEPTH - 1,
                                         gla_s, ret_s)
        per_layer.append(states)
    gla_p, pool_p, pool_s, ret_p, vn_s = (jnp.stack(z) for z in zip(*per_layer))
    return (x[0].reshape(B_P, T_P, D), x[1].reshape(N_S, 1, D),
            gla_p, gla_s, pool_p, pool_s, ret_p, ret_s, vn_s.reshape(DEPTH, N_S, 1, 512))
```

```python
import functools

import jax
import jax.numpy as jnp
from jax import lax
from jax.experimental import pallas as pl
from jax.experimental.pallas import tpu as pltpu
from jax.experimental.pallas import tpu_sc as plsc

F32 = jnp.float32
BF16 = jnp.bfloat16
HIGHEST = lax.Precision.HIGHEST

D = 2048
B_P, T_P = 4, 2048
N_P = B_P * T_P
N_S = 128
N = N_P + N_S
DEPTH = 2
PAST_LEN = 16384
EPS = 1e-6
HEADS, DK, DV = 4, 64, 128
CHUNK = 64
GATE_TEMP = 16.0
POOL_WINDOWS = (2, 4, 8, 16)
POOL_BUF = 15
ROPE_BASE = 10000.0
N_EXPERTS = 64
TOP_K = 8
D_EXPERT = 512
ROUTED_SCALE = 2.5

LANES = 128
SUB = 128
MOD_ROWS = 256
EXP_BLOCK = 256
N_ASSIGN = N * TOP_K
N_BLOCKS = -(-(N_ASSIGN + N_EXPERTS * (EXP_BLOCK - 1)) // EXP_BLOCK)
L_SLOTS = N_BLOCKS * EXP_BLOCK
VMEM_LIMIT = 56 * 1024 * 1024

C_GQ, C_GK, C_GV, C_GR, C_PIN, C_RQ, C_RK, C_RV, C_RG, C_SU, C_SV = (
    0, 256, 512, 1024, 1536, 2048, 2304, 2560, 3072, 3584, 4096)
P_MAIN = 4608


def _cparams(n_axes=1):
    return pltpu.CompilerParams(dimension_semantics=("arbitrary",) * n_axes,
                                vmem_limit_bytes=VMEM_LIMIT)


def _silu(x):
    return x * jax.nn.sigmoid(x)


def _mod_rows(t, mp_ref, ms_ref):
    b = jnp.minimum(t // (T_P // SUB), B_P - 1)
    return jnp.where(t >= N_P // SUB, ms_ref[...], mp_ref[pl.ds(b, 1), :])


def _mod_specs(layer, part):
    return [pl.BlockSpec((None, 8, D), lambda i, l=layer, p=part: (l, 0, p)),
            pl.BlockSpec((None, SUB, D), lambda i, l=layer, p=part: (l, 1, p))]


def _pack_bf16_pair(lo, hi):
    lo_u = lax.bitcast_convert_type(lo.astype(BF16).astype(F32), jnp.uint32)
    hi_u = lax.bitcast_convert_type(hi.astype(BF16).astype(F32), jnp.uint32)
    return lax.bitcast_convert_type((hi_u & jnp.uint32(0xFFFF0000)) | (lo_u >> 16), jnp.int32)


def _unpack_bf16_pair(w):
    u = lax.bitcast_convert_type(w, jnp.uint32)
    lo = lax.bitcast_convert_type(u << 16, F32)
    hi = lax.bitcast_convert_type(u & jnp.uint32(0xFFFF0000), F32)
    return lo, hi


ROW_TILE = (8, LANES)


def _load_row_tiles(ref):
    return jnp.concatenate([ref[:, c, :] for c in range(ROW_TILE[0])], axis=1)


def _store_row_tiles(ref, val):
    for c in range(ROW_TILE[0]):
        ref[:, c, :] = val[:, c * LANES:(c + 1) * LANES]


def _load_row_tiles_2d(ref, rows):
    return jnp.concatenate([ref[pl.ds(c, rows, stride=ROW_TILE[0]), :] for c in range(ROW_TILE[0])], axis=1)


def _store_row_tiles_2d(ref, val, rows):
    for c in range(ROW_TILE[0]):
        ref[pl.ds(c, rows, stride=ROW_TILE[0]), :] = val[:, c * LANES:(c + 1) * LANES]


def _ada_body(c_ref, w_ref, b_ref, o_ref):
    s = _silu(c_ref[...]).astype(BF16)
    o_ref[...] = jnp.dot(s, w_ref[...].astype(BF16), preferred_element_type=F32) + b_ref[...]


def _ada(c_all, w_ada, b_ada):
    tn = 1024
    return pl.pallas_call(
        _ada_body,
        grid=(DEPTH, 6 * D // tn),
        in_specs=[pl.BlockSpec((MOD_ROWS, D), lambda l, j: (0, 0)),
                  pl.BlockSpec((None, D, tn), lambda l, j: (l, 0, j)),
                  pl.BlockSpec((None, 1, tn), lambda l, j: (l, 0, j))],
        out_specs=pl.BlockSpec((None, MOD_ROWS, tn), lambda l, j: (l, 0, j)),
        out_shape=jax.ShapeDtypeStruct((DEPTH, MOD_ROWS, 6 * D), F32),
        compiler_params=_cparams(2), name="ada")(c_all, w_ada, b_ada.reshape(DEPTH, 1, 6 * D))


ZERO_ROWS = 2 * SUB


def _x_specs(x, tm, n_axes):
    row = (lambda i: (i, 0)) if n_axes == 1 else (lambda i, j: (i, 0))
    if not isinstance(x, tuple):
        return [pl.BlockSpec((tm, D), row)], [x]
    zero = (lambda i: (0, 0)) if n_axes == 1 else (lambda i, j: (0, 0))
    return [pl.BlockSpec((tm, D), row), pl.BlockSpec((N_S, D), zero)], list(x)


def _x_rows(x_refs, t, rows):
    if len(x_refs) == 1:
        return x_refs[0][rows, :]
    return jnp.where(t >= N_P // SUB, x_refs[1][...], x_refs[0][rows, :])


def _prenorm_rows(t, x, g_ref, shp_ref, shs_ref, scp_ref, scs_ref):
    y = x * lax.rsqrt(jnp.mean(x * x, axis=-1, keepdims=True) + EPS) * g_ref[...]
    return y * (1.0 + _mod_rows(t, scp_ref, scs_ref)) + _mod_rows(t, shp_ref, shs_ref)


TM_NORM = 640


def _prenorm_body(*refs):
    g_ref, shp_ref, shs_ref, scp_ref, scs_ref, h_ref = refs[-6:]
    for sidx in range(TM_NORM // SUB):
        rows = pl.ds(sidx * SUB, SUB)
        t = pl.program_id(0) * (TM_NORM // SUB) + sidx
        h_ref[rows, :] = _prenorm_rows(t, _x_rows(refs[:-6], t, rows), g_ref, shp_ref, shs_ref, scp_ref,
                                       scs_ref).astype(BF16)


def _prenorm(x, g, mod, layer):
    x_specs, x_args = _x_specs(x, TM_NORM, 1)
    return pl.pallas_call(
        _prenorm_body,
        grid=(N // TM_NORM,),
        in_specs=x_specs + [pl.BlockSpec((1, D), lambda i: (0, 0))] + _mod_specs(layer, 0) + _mod_specs(layer, 1),
        out_specs=pl.BlockSpec((TM_NORM, D), lambda i: (i, 0)),
        out_shape=jax.ShapeDtypeStruct((N, D), BF16),
        compiler_params=_cparams(1), name="prenorm")(*x_args, g.reshape(1, D), mod, mod, mod, mod)


def _mm_body(x_ref, w_ref, o_ref):
    o_ref[...] = jnp.dot(x_ref[...], w_ref[...], preferred_element_type=F32).astype(o_ref.dtype)


def _matmul(x, w_all, layer, tm, tn, out_dtype=F32, name="mm"):
    m, k = x.shape
    n = w_all.shape[2]
    return pl.pallas_call(
        _mm_body,
        grid=(m // tm, n // tn),
        in_specs=[pl.BlockSpec((tm, k), lambda i, j: (i, 0)),
                  pl.BlockSpec((None, k, tn), lambda i, j, l=layer: (l, 0, j))],
        out_specs=pl.BlockSpec((tm, tn), lambda i, j: (i, j)),
        out_shape=jax.ShapeDtypeStruct((m, n), out_dtype),
        compiler_params=_cparams(2), name=name)(x, w_all)


ROWS_LA = 256


def _swap_halves_lanes(x):
    lane = lax.broadcasted_iota(jnp.int32, x.shape, 1)
    return jnp.where((lane % 64) < 32, pltpu.roll(x, 96, 1), pltpu.roll(x, 32, 1))


def _rope_lanes(x, cos, sin_signed):
    parts = []
    for half in range(2):
        xh = x[:, half * LANES:(half + 1) * LANES]
        parts.append(xh * cos + _swap_halves_lanes(xh) * sin_signed)
    return jnp.concatenate(parts, axis=1)


def _la_prompt_body(q_ref, k_ref, v_ref, r_ref, aux_ref, aux2_ref, dec_ref, bias_ref, g_ref,
                    o_ref, st_out_ref, st_ref, *, retention):
    t = pl.program_id(1)

    @pl.when(t == 0)
    def _():
        st_ref[...] = jnp.zeros_like(st_ref)

    n_ch = ROWS_LA // CHUNK
    ri = lax.broadcasted_iota(jnp.int32, (ROWS_LA, ROWS_LA), 0)
    ci = lax.broadcasted_iota(jnp.int32, (ROWS_LA, ROWS_LA), 1)
    causal = (ri >= ci) & ((ri // CHUNK) == (ci // CHUNK))
    scale = DK ** -0.5

    q = q_ref[...]
    k = k_ref[...]
    v = v_ref[...]
    if retention:
        cos = aux_ref[...]
        sin = aux2_ref[...]
        q = _rope_lanes(q, cos, sin)
        k = _rope_lanes(k, cos, sin) * scale
        step = (lax.broadcasted_iota(jnp.int32, (ROWS_LA, HEADS * DK), 0) % CHUNK + 1).astype(F32)
        bc = step * dec_ref[...]
    else:
        q = q * scale
        logit = jnp.dot(aux_ref[...], dec_ref[...], precision=HIGHEST, preferred_element_type=F32) + bias_ref[...]
        la = jax.nn.log_sigmoid(logit) / GATE_TEMP
        bc = jnp.dot(causal.astype(F32), la, precision=HIGHEST, preferred_element_type=F32)
    bl = bc.reshape(n_ch, CHUNK, HEADS * DK)[:, CHUNK - 1:CHUNK, :]
    bl_rows = jnp.broadcast_to(bl, (n_ch, CHUNK, HEADS * DK)).reshape(ROWS_LA, HEADS * DK)
    qd = q * jnp.exp(bc)
    ki = k * jnp.exp(-bc)
    ke = k * jnp.exp(bl_rows - bc)
    ac = jnp.exp(bl)
    outs = []
    for h in range(HEADS):
        ks = slice(h * DK, (h + 1) * DK)
        vs = slice(h * DV, (h + 1) * DV)
        qd_h = qd[:, ks].astype(BF16)
        ki_h = ki[:, ks].astype(BF16)
        ke_h = ke[:, ks].astype(BF16)
        v_h = v[:, vs].astype(BF16)
        sc = lax.dot_general(qd_h, ki_h, (((1,), (1,)), ((), ())), preferred_element_type=F32)
        sc = jnp.where(causal, sc, 0.0)
        o_h = jnp.dot(sc.astype(BF16), v_h, preferred_element_type=F32)
        inter = []
        for c in range(n_ch):
            rows = slice(c * CHUNK, (c + 1) * CHUNK)
            st = st_ref[h]
            inter.append(lax.dot_general(qd_h[rows], st.astype(BF16), (((1,), (1,)), ((), ())),
                                         preferred_element_type=F32))
            kv_t = lax.dot_general(v_h[rows], ke_h[rows], (((0,), (0,)), ((), ())), preferred_element_type=F32)
            st_ref[h] = st * ac[c][:, ks] + kv_t
        o_h = o_h + jnp.concatenate(inter, axis=0)
        outs.append(o_h * lax.rsqrt(jnp.mean(o_h * o_h, axis=-1, keepdims=True) + EPS) * g_ref[:, vs])
    o_ref[...] = (jnp.concatenate(outs, axis=1) * _silu(r_ref[...])).astype(BF16)
    st_out_ref[...] = st_ref[...]


def _la_prompt(p_main, cq, ck, cv, cr, aux, aux2, dec, bias, g, retention):
    nt = T_P // ROWS_LA
    rowblk = lambda b, t: b * nt + t
    if retention:
        aux_specs = [pl.BlockSpec((ROWS_LA, LANES), lambda b, t: (t, 0)),
                     pl.BlockSpec((ROWS_LA, LANES), lambda b, t: (t, 0))]
    else:
        aux_specs = [pl.BlockSpec((ROWS_LA, LANES), lambda b, t: (rowblk(b, t), 0)),
                     pl.BlockSpec((8, LANES), lambda b, t: (0, 0))]
    o, st = pl.pallas_call(
        functools.partial(_la_prompt_body, retention=retention),
        grid=(B_P, nt),
        in_specs=[pl.BlockSpec((ROWS_LA, 256), lambda b, t: (rowblk(b, t), cq // 256)),
                  pl.BlockSpec((ROWS_LA, 256), lambda b, t: (rowblk(b, t), ck // 256)),
                  pl.BlockSpec((ROWS_LA, 512), lambda b, t: (rowblk(b, t), cv // 512)),
                  pl.BlockSpec((ROWS_LA, 512), lambda b, t: (rowblk(b, t), cr // 512))]
        + aux_specs
        + [pl.BlockSpec(dec.shape, lambda b, t: (0, 0)),
           pl.BlockSpec((1, HEADS * DK), lambda b, t: (0, 0)),
           pl.BlockSpec((1, HEADS * DV), lambda b, t: (0, 0))],
        out_specs=[pl.BlockSpec((ROWS_LA, HEADS * DV), lambda b, t: (rowblk(b, t), 0)),
                   pl.BlockSpec((None, HEADS, DV, DK), lambda b, t: (b, 0, 0, 0))],
        out_shape=[jax.ShapeDtypeStruct((N, HEADS * DV), BF16),
                   jax.ShapeDtypeStruct((B_P, HEADS, DV, DK), F32)],
        scratch_shapes=[pltpu.VMEM((HEADS, DV, DK), F32)],
        compiler_params=_cparams(2), name="ret_prompt" if retention else "gla_prompt",
    )(p_main, p_main, p_main, p_main, aux, aux2, dec, bias, g)
    return o, jnp.swapaxes(st, -1, -2)


SAMPLE_TILE = 8


def _la_sample_body(qt_ref, kt_ref, lt_ref, cos_ref, sin_ref, v_ref, r_ref, g_ref, s_ref, *rest, retention):
    o_ref, s_out_ref = rest[-2:]
    scale = DK ** -0.5
    qt = qt_ref[...]
    kt = kt_ref[...]
    if retention:
        def rope(x):
            sw = jnp.concatenate(
                [x[h * DK + (DK // 2) * (1 - j): h * DK + (DK // 2) * (2 - j), :]
                 for h in range(HEADS) for j in range(2)], axis=0)
            return x * cos_ref[...] + sw * sin_ref[...]
        qt = rope(qt)
        kt = rope(kt) * scale
        la = lt_ref[...]
    else:
        qt = qt * scale
        la = jax.nn.log_sigmoid(lt_ref[...]) / GATE_TEMP
    at = jnp.exp(la)
    qd = qt * at
    ki = kt * jnp.exp(-la)
    prod = qd * ki
    v8 = v_ref[...]
    r8 = r_ref[...]
    g = g_ref[...]
    for j in range(SAMPLE_TILE):
        for h in range(HEADS):
            ks = slice(h * DK, (h + 1) * DK)
            vs = slice(h * DV, (h + 1) * DV)
            a_c = jnp.broadcast_to(at[ks, j:j + 1], (DK, DV))
            k_c = jnp.broadcast_to(kt[ks, j:j + 1], (DK, DV))
            q_c = jnp.broadcast_to(qd[ks, j:j + 1], (DK, DV))
            s_c = jnp.broadcast_to(jnp.sum(prod[ks, j:j + 1], axis=0, keepdims=True), (1, DV))
            s0 = s_ref[j, h]
            v_row = v8[j:j + 1, vs]
            s_out_ref[j, h] = a_c * s0 + k_c * v_row
            o_row = s_c * v_row + jnp.sum(q_c * s0, axis=0, keepdims=True)
            o_n = o_row * lax.rsqrt(jnp.mean(o_row * o_row, axis=-1, keepdims=True) + EPS) * g[:, vs]
            o_ref[j:j + 1, vs] = o_n * _silu(r8[j:j + 1, vs])


def _la_sample(qt, kt, lt, cos_t, sin_t, p_main, cv, cr, g, s0_all, layer, s_prev, retention):
    nt = N_S // SAMPLE_TILE
    row0 = N_P // SAMPLE_TILE
    tile = pl.BlockSpec((None, HEADS * DK, LANES), lambda i: (i, 0, 0))
    full = pl.BlockSpec((HEADS * DK, LANES), lambda i: (0, 0))
    lt_spec = full if retention else tile
    return pl.pallas_call(
        functools.partial(_la_sample_body, retention=retention),
        grid=(nt,),
        in_specs=[tile, tile, lt_spec, full, full,
                  pl.BlockSpec((SAMPLE_TILE, 512), lambda i: (row0 + i, cv // 512)),
                  pl.BlockSpec((SAMPLE_TILE, 512), lambda i: (row0 + i, cr // 512)),
                  pl.BlockSpec((1, HEADS * DV), lambda i: (0, 0)),
                  pl.BlockSpec((None, SAMPLE_TILE, HEADS, DK, DV), lambda i, l=layer: (l, i, 0, 0, 0))]
        + ([] if s_prev is None else [pl.BlockSpec(memory_space=pl.ANY)]),
        out_specs=[pl.BlockSpec((SAMPLE_TILE, HEADS * DV), lambda i: (i, 0)),
                   pl.BlockSpec((None, SAMPLE_TILE, HEADS, DK, DV), lambda i, l=layer: (l, i, 0, 0, 0))],
        out_shape=[jax.ShapeDtypeStruct((N_S, HEADS * DV), F32),
                   jax.ShapeDtypeStruct((DEPTH, N_S, HEADS, DK, DV), F32)],
        input_output_aliases={} if s_prev is None else {9: 1},
        compiler_params=_cparams(1), name="ret_sample" if retention else "gla_sample",
    )(qt, kt, lt, cos_t, sin_t, p_main, p_main, g, s0_all, *([] if s_prev is None else [s_prev]))


def _gate_logits_t_body(w_ref, x_ref, b_ref, o_ref):
    o_ref[...] = jnp.dot(w_ref[...], x_ref[...], precision=HIGHEST, preferred_element_type=F32) + b_ref[...]


def _gate_logits_t(w_gate_t, glow_t, b_col):
    nt = N_S // SAMPLE_TILE
    return pl.pallas_call(
        _gate_logits_t_body,
        grid=(nt,),
        in_specs=[pl.BlockSpec((HEADS * DK, LANES), lambda i: (0, 0)),
                  pl.BlockSpec((None, LANES, LANES), lambda i: (i, 0, 0)),
                  pl.BlockSpec((HEADS * DK, LANES), lambda i: (0, 0))],
        out_specs=pl.BlockSpec((None, HEADS * DK, LANES), lambda i: (i, 0, 0)),
        out_shape=jax.ShapeDtypeStruct((nt, HEADS * DK, LANES), F32),
        compiler_params=_cparams(1), name="gate_logits_t")(w_gate_t, glow_t, b_col)


def _to_tiles_t(x):
    c = x.shape[1]
    xt = jnp.swapaxes(x.reshape(N_S // SAMPLE_TILE, SAMPLE_TILE, c), 1, 2)
    return jnp.pad(xt, ((0, 0), (0, 0), (0, LANES - SAMPLE_TILE)))


ROWS_POOL = 512


def _pool_mix(y, w_ref, sc_ref):
    outs = []
    for gi in range(4):
        cs = slice(gi * LANES, (gi + 1) * LANES)
        outs.append(jnp.dot(y[:, cs].astype(BF16), w_ref[gi], preferred_element_type=F32))
    return jnp.concatenate(outs, axis=1) * sc_ref[...]


def _pool_prompt_body(p_ref, halo_ref, w_ref, sc_ref, o_ref):
    t = pl.program_id(1)
    p = p_ref[...]
    halo = jnp.where(t == 0, 0.0, halo_ref[...])
    full = jnp.concatenate([halo, p], axis=0)
    pos = t * ROWS_POOL + lax.broadcasted_iota(jnp.int32, (ROWS_POOL, LANES), 0)
    means = []
    for gi, w in enumerate(POOL_WINDOWS):
        s = full[:, gi * LANES:(gi + 1) * LANES]
        step = 1
        while step < w:
            s = s + pltpu.roll(s, step, 0)
            step *= 2
        win = s[16:, :]
        cnt = jnp.minimum(w, pos + 1).astype(F32)
        means.append(win / cnt)
    y = jnp.concatenate(means, axis=1) - p
    o_ref[...] = _pool_mix(y, w_ref, sc_ref).astype(BF16)


def _pool_prompt(p_main, w_bf, scale):
    nt = T_P // ROWS_POOL
    return pl.pallas_call(
        _pool_prompt_body,
        grid=(B_P, nt),
        in_specs=[pl.BlockSpec((ROWS_POOL, 512), lambda b, t: (b * nt + t, C_PIN // 512)),
                  pl.BlockSpec((16, 512), lambda b, t: (jnp.maximum((b * nt + t) * (ROWS_POOL // 16) - 1, 0),
                                                        C_PIN // 512)),
                  pl.BlockSpec((4, LANES, LANES), lambda b, t: (0, 0, 0)),
                  pl.BlockSpec((1, 512), lambda b, t: (0, 0))],
        out_specs=pl.BlockSpec((ROWS_POOL, 512), lambda b, t: (b * nt + t, 0)),
        out_shape=jax.ShapeDtypeStruct((N, 512), BF16),
        compiler_params=_cparams(2), name="pool_prompt")(p_main, p_main, w_bf, scale)


def _small_sample_body(p_ref, buf_ref, pw_ref, psc_ref, u_ref, sv_ref, sg_ref, sw_ref, sb_ref,
                       ob_ref, od_ref, vn_ref):
    p = p_ref[...]
    means = []
    for gi, w in enumerate(POOL_WINDOWS):
        cs = slice(gi * LANES, (gi + 1) * LANES)
        s = p[:, cs]
        for j in range(1, w):
            s = s + buf_ref[:, POOL_BUF - j, cs]
        means.append(s / float(min(w, PAST_LEN + 1)))
    y = jnp.concatenate(means, axis=1) - p
    ob_ref[...] = _pool_mix(y, pw_ref, psc_ref)
    sv = sv_ref[...]
    vn = sv * lax.rsqrt(jnp.mean(sv * sv, axis=-1, keepdims=True) + EPS) * sg_ref[...]
    vn_ref[...] = vn
    od_ref[...] = u_ref[...] * (sw_ref[...] * vn + sb_ref[...])


def _small_sample(p_main, buf, pw_bf, pscale, sgu_g, sgu_w0, sgu_b0):
    row = N_P // N_S
    col = lambda c: pl.BlockSpec((N_S, 512), lambda i, c=c: (row, c // 512))
    vec = pl.BlockSpec((1, 512), lambda i: (0, 0))
    return pl.pallas_call(
        _small_sample_body,
        grid=(1,),
        in_specs=[col(C_PIN), pl.BlockSpec((N_S, POOL_BUF, 512), lambda i: (0, 0, 0)),
                  pl.BlockSpec((4, LANES, LANES), lambda i: (0, 0, 0)), vec,
                  col(C_SU), col(C_SV), vec, vec, vec],
        out_specs=[pl.BlockSpec((N_S, 512), lambda i: (0, 0))] * 3,
        out_shape=[jax.ShapeDtypeStruct((N_S, 512), F32)] * 3,
        compiler_params=_cparams(1), name="small_sample",
    )(p_main, buf, pw_bf, pscale, p_main, p_main, sgu_g, sgu_w0, sgu_b0)


ROWS_SGU = 512
SGU_CHUNK = 128


def _sgu_prompt_body(u_ref, v_ref, g_ref, w_ref, bt_ref, o_ref):
    ri = lax.broadcasted_iota(jnp.int32, (SGU_CHUNK, SGU_CHUNK), 0)
    ci = lax.broadcasted_iota(jnp.int32, (SGU_CHUNK, SGU_CHUNK), 1)
    causal = ri >= ci
    for c in range(ROWS_SGU // SGU_CHUNK):
        rows = pl.ds(c * SGU_CHUNK, SGU_CHUNK)
        v = v_ref[rows, :]
        vn = (v * lax.rsqrt(jnp.mean(v * v, axis=-1, keepdims=True) + EPS) * g_ref[...]).astype(BF16)
        outs = []
        for gi in range(4):
            cs = slice(gi * LANES, (gi + 1) * LANES)
            w = jnp.where(causal, w_ref[gi], 0.0).astype(BF16)
            mixed = jnp.dot(w, vn[:, cs], preferred_element_type=F32)
            outs.append(mixed + jnp.broadcast_to(bt_ref[:, gi:gi + 1], (SGU_CHUNK, LANES)))
        o_ref[rows, :] = (u_ref[rows, :] * jnp.concatenate(outs, axis=1)).astype(BF16)


def _sgu_prompt(p_main, g, w, b_t):
    return pl.pallas_call(
        _sgu_prompt_body,
        grid=(N_P // ROWS_SGU,),
        in_specs=[pl.BlockSpec((ROWS_SGU, 512), lambda i: (i, C_SU // 512)),
                  pl.BlockSpec((ROWS_SGU, 512), lambda i: (i, C_SV // 512)),
                  pl.BlockSpec((1, 512), lambda i: (0, 0)),
                  pl.BlockSpec((4, SGU_CHUNK, SGU_CHUNK), lambda i: (0, 0, 0)),
                  pl.BlockSpec((SGU_CHUNK, LANES), lambda i: (0, 0))],
        out_specs=pl.BlockSpec((ROWS_SGU, 512), lambda i: (i, 0)),
        out_shape=jax.ShapeDtypeStruct((N, 512), BF16),
        compiler_params=_cparams(1), name="sgu_prompt")(p_main, p_main, g, w, b_t)


TM_MERGE = 640
TN_MERGE = 512


def _merge_body(h_ref, ba_ref, bb_ref, bc_ref, bd_ref, g0, g1, g2, g3, u0, u1, u2, u3,
                c0, c1, c2, c3, o_ref):
    h = h_ref[...]
    acc = None
    for br, gw, uw, gb in ((ba_ref, g0, u0, c0), (bb_ref, g1, u1, c1), (bc_ref, g2, u2, c2), (bd_ref, g3, u3, c3)):
        gate = jax.nn.sigmoid(jnp.dot(h, gw[...], preferred_element_type=F32) + gb[...])
        up = jnp.dot(br[...], uw[...], preferred_element_type=F32)
        acc = gate * up if acc is None else acc + gate * up
    o_ref[...] = acc.astype(BF16)


def _merge(h, branches, w_mg, b_mg, w_br, layer):
    nj = D // TN_MERGE
    row = lambda w: pl.BlockSpec((TM_MERGE, w), lambda i, j: (i, 0))
    gate_w = [pl.BlockSpec((None, D, TN_MERGE), lambda i, j, b=b, l=layer: (l, 0, b * nj + j)) for b in range(4)]
    up_w = [pl.BlockSpec((None, None, 512, TN_MERGE), lambda i, j, b=b, l=layer: (l, b, 0, j)) for b in range(4)]
    gate_b = [pl.BlockSpec((None, 1, TN_MERGE), lambda i, j, b=b, l=layer: (l, 0, b * nj + j)) for b in range(4)]
    return pl.pallas_call(
        _merge_body,
        grid=(N // TM_MERGE, nj),
        in_specs=[row(D)] + [row(512)] * 4 + gate_w + up_w + gate_b,
        out_specs=pl.BlockSpec((TM_MERGE, TN_MERGE), lambda i, j: (i, j)),
        out_shape=jax.ShapeDtypeStruct((N, D), BF16),
        compiler_params=_cparams(2), name="merge",
    )(h, *branches, w_mg, w_mg, w_mg, w_mg, w_br, w_br, w_br, w_br, b_mg, b_mg, b_mg, b_mg)


TM_OUT = 640
TN_OUT = 1024


def _post_value(t, x, y, gn_ref, gp_ref, gs_ref):
    yn = y * lax.rsqrt(jnp.mean(y * y, axis=-1, keepdims=True) + EPS) * gn_ref[...]
    return x + _mod_rows(t, gp_ref, gs_ref) * yn


def _outproj_body(m_ref, w_ref, *refs):
    gn_ref, gp_ref, gs_ref, o_ref, acc_ref = refs[-5:]
    j = pl.program_id(1)
    acc_ref[j] = jnp.dot(m_ref[...], w_ref[...], preferred_element_type=F32)

    @pl.when(j == D // TN_OUT - 1)
    def _():
        for sidx in range(TM_OUT // SUB):
            rows = pl.ds(sidx * SUB, SUB)
            t = pl.program_id(0) * (TM_OUT // SUB) + sidx
            y = jnp.concatenate([acc_ref[c, rows, :] for c in range(D // TN_OUT)], axis=1)
            o_ref[rows, :] = _post_value(t, _x_rows(refs[:-5], t, rows), y, gn_ref, gp_ref, gs_ref)


def _outproj(merged, w_out, x, g_post, mod, layer):
    mspec = [pl.BlockSpec((None, 8, D), lambda i, j, l=layer: (l, 0, 2)),
             pl.BlockSpec((None, SUB, D), lambda i, j, l=layer: (l, 1, 2))]
    x_specs, x_args = _x_specs(x, TM_OUT, 2)
    return pl.pallas_call(
        _outproj_body,
        grid=(N // TM_OUT, D // TN_OUT),
        in_specs=[pl.BlockSpec((TM_OUT, D), lambda i, j: (i, 0)),
                  pl.BlockSpec((None, D, TN_OUT), lambda i, j, l=layer: (l, 0, j))] + x_specs
        + [pl.BlockSpec((1, D), lambda i, j: (0, 0))] + mspec,
        out_specs=pl.BlockSpec((TM_OUT, D), lambda i, j: (i, 0)),
        out_shape=jax.ShapeDtypeStruct((N, D), F32),
        scratch_shapes=[pltpu.VMEM((D // TN_OUT, TM_OUT, TN_OUT), F32)],
        compiler_params=_cparams(2), name="outproj",
    )(merged, w_out, *x_args, g_post.reshape(1, D), mod, mod)


def _router_body(lg_ref, b_ref, eid_ref, pos_ref, wt_ref, cnt_ref, run_ref):
    i = pl.program_id(0)

    @pl.when(i == 0)
    def _():
        run_ref[...] = jnp.zeros_like(run_ref)

    ng, gs = 8, N_EXPERTS // 8
    neg = -jnp.inf
    scores = jax.nn.sigmoid(lg_ref[...].T[:N_EXPERTS, :])
    sel = scores + b_ref[...]
    sel3 = sel.reshape(ng, gs, SUB)
    sub3 = lax.broadcasted_iota(jnp.int32, (ng, gs, SUB), 1)
    gmax = jnp.max(sel3, axis=1, keepdims=True)
    first = jnp.min(jnp.where(sel3 == gmax, sub3, gs), axis=1, keepdims=True)
    gmax2 = jnp.max(jnp.where(sub3 == first, neg, sel3), axis=1, keepdims=True)
    gscore = (gmax + gmax2).reshape(ng, SUB)
    gidx = lax.broadcasted_iota(jnp.int32, (ng, SUB), 0)
    grank = jnp.zeros((ng, SUB), jnp.int32)
    for s in range(1, ng):
        other = pltpu.roll(gscore, s, 0)
        lower = gidx >= s
        grank += ((other > gscore) | ((other == gscore) & lower)).astype(jnp.int32)
    keep = jnp.broadcast_to((grank < 4).reshape(ng, 1, SUB), (ng, gs, SUB))
    masked = jnp.where(keep, sel3, neg).reshape(N_EXPERTS, SUB)
    eidx = lax.broadcasted_iota(jnp.int32, (N_EXPERTS, SUB), 0)
    rank = jnp.zeros((N_EXPERTS, SUB), jnp.int32)
    for s in range(1, N_EXPERTS):
        other = pltpu.roll(masked, s, 0)
        lower = eidx >= s
        rank += ((other > masked) | ((other == masked) & lower)).astype(jnp.int32)
    chosen = rank < TOP_K
    w_sel = jnp.where(chosen, scores, 0.0)
    w_sel = w_sel / jnp.sum(w_sel, axis=0, keepdims=True) * ROUTED_SCALE
    ri = lax.broadcasted_iota(jnp.int32, (SUB, SUB), 0)
    ci = lax.broadcasted_iota(jnp.int32, (SUB, SUB), 1)
    onehot = chosen.astype(BF16)
    pos = jnp.dot(onehot, (ri < ci).astype(BF16), preferred_element_type=F32) + run_ref[...]
    run_ref[...] = run_ref[...] + jnp.sum(chosen.astype(F32), axis=1, keepdims=True)
    cnt_ref[...] = run_ref[...]
    eidx_f = eidx.astype(F32)
    rows_e, rows_p, rows_w = [], [], []
    for kk in range(TOP_K):
        m = chosen & (rank == kk)
        rows_e.append(jnp.sum(jnp.where(m, eidx_f, 0.0), axis=0, keepdims=True))
        rows_p.append(jnp.sum(jnp.where(m, pos, 0.0), axis=0, keepdims=True))
        rows_w.append(jnp.sum(jnp.where(m, w_sel, 0.0), axis=0, keepdims=True))
    eid_ref[...] = jnp.concatenate(rows_e, axis=0).astype(jnp.int32)
    pos_ref[...] = jnp.concatenate(rows_p, axis=0).astype(jnp.int32)
    wt_ref[...] = jnp.concatenate(rows_w, axis=0)


def _router(logits, rb_col):
    tile = pl.BlockSpec((TOP_K, SUB), lambda i: (0, i))
    return pl.pallas_call(
        _router_body,
        grid=(N // SUB,),
        in_specs=[pl.BlockSpec((SUB, LANES), lambda i: (i, 0)),
                  pl.BlockSpec((N_EXPERTS, SUB), lambda i: (0, 0))],
        out_specs=[tile, tile, tile, pl.BlockSpec((N_EXPERTS, SUB), lambda i: (0, 0))],
        out_shape=[jax.ShapeDtypeStruct((TOP_K, N), jnp.int32), jax.ShapeDtypeStruct((TOP_K, N), jnp.int32),
                   jax.ShapeDtypeStruct((TOP_K, N), F32), jax.ShapeDtypeStruct((N_EXPERTS, SUB), F32)],
        scratch_shapes=[pltpu.VMEM((N_EXPERTS, SUB), F32)],
        compiler_params=_cparams(1), name="router")(logits, rb_col)


SC_CORES, SC_SUBCORES = 2, 16
SC_WORKERS = SC_CORES * SC_SUBCORES
SC_LANES = 16
SC_SCAN = N_ASSIGN // SC_WORKERS


def _sc_mesh():
    return plsc.VectorSubcoreMesh(core_axis_name="c", subcore_axis_name="s",
                                  num_cores=SC_CORES, num_subcores=SC_SUBCORES)


def _sc_worker_base(per_w):
    return (lax.axis_index("s") * SC_CORES + lax.axis_index("c")) * per_w


def _sc_chunk(per_w):
    return max(c for c in (8, 16, 24, 32) if per_w % c == 0)


def _sc_gather_rows(table_hbm, out_hbm, idx_v, rows_v, gsem, wsem, base, per_w):
    chunk = _sc_chunk(per_w)
    n_ch = per_w // chunk

    def gather(j, p):
        off = pl.multiple_of(j * chunk, 8)
        return pltpu.make_async_copy(table_hbm.at[idx_v.at[pl.ds(off, chunk)]], rows_v.at[p], gsem.at[p])

    def write(j, p):
        off = pl.multiple_of(j * chunk, 8)
        return pltpu.make_async_copy(rows_v.at[p], out_hbm.at[pl.ds(base + off, chunk)], wsem.at[p])

    def when(cond, fn):
        if isinstance(cond, bool):
            if cond:
                fn()
        else:
            pl.when(cond)(fn)

    def step(j, p):
        gather(j, p).wait()
        when(j >= 1, lambda: write(j - 1, 1 - p).wait())
        when(j + 1 < n_ch, lambda: gather(j + 1, 1 - p).start())
        write(j, p).start()

    gather(0, 0).start()

    @pl.loop(0, n_ch - n_ch % 2, step=2)
    def _(j0):
        for p in range(2):
            step(j0 + p, p)

    if n_ch % 2:
        step(n_ch - 1, 0)
    write(n_ch - 1, (n_ch - 1) % 2).wait()


def _sc_row_scratch(per_w):
    return [pltpu.VMEM((2, _sc_chunk(per_w)) + ROW_TILE, jnp.int32),
            pltpu.SemaphoreType.DMA((2,)), pltpu.SemaphoreType.DMA((2,))]


def _sc_gather(table, idx):
    n_out = idx.shape[0]
    per_w = n_out // SC_WORKERS
    assert per_w * SC_WORKERS == n_out and per_w % 8 == 0

    def body(table_hbm, idx_hbm, out_hbm, idx_v, rows_v, gsem, wsem):
        base = _sc_worker_base(per_w)
        pltpu.sync_copy(idx_hbm.at[pl.ds(base, per_w)], idx_v)
        _sc_gather_rows(table_hbm, out_hbm, idx_v, rows_v, gsem, wsem, base, per_w)

    return pl.kernel(
        body, out_type=jax.ShapeDtypeStruct((n_out,) + ROW_TILE, jnp.int32), mesh=_sc_mesh(),
        scratch_types=[pltpu.VMEM((per_w,), jnp.int32)] + _sc_row_scratch(per_w), name="sc_gather")(table, idx)


PARTS = ((0, 54), (54, 108), (162, 162))
assert sum(n for _, n in PARTS) == N_BLOCKS and all(PARTS[i][0] + PARTS[i][1] == PARTS[i + 1][0]
                                                    for i in range(len(PARTS) - 1))


def _sc_dispatch(table, slots, blk0, n_blk):
    part_slots = n_blk * EXP_BLOCK
    per_w = part_slots // SC_WORKERS
    per_w_pad = -(-per_w // SC_LANES) * SC_LANES
    assert per_w * SC_WORKERS == part_slots and per_w % 8 == 0 and SC_SCAN % SC_LANES == 0
    assert ZERO_ROWS & (ZERO_ROWS - 1) == 0

    def body(table_hbm, slots_hbm, out_hbm, idx_v, sl_v, rows_v, gsem, wsem):
        local = _sc_worker_base(per_w)
        base = blk0 * EXP_BLOCK + local
        lane = lax.iota(jnp.int32, SC_LANES)

        @pl.loop(0, per_w_pad // SC_LANES)
        def _(j):
            off = pl.multiple_of(j * SC_LANES, SC_LANES)
            idx_v[pl.ds(off, SC_LANES)] = N + ((base + off + lane) & (ZERO_ROWS - 1))

        @pl.loop(0, N_ASSIGN // SC_SCAN)
        def _(c):
            pltpu.sync_copy(slots_hbm.at[pl.ds(pl.multiple_of(c * SC_SCAN, 8), SC_SCAN)], sl_v)

            @pl.loop(0, SC_SCAN // SC_LANES)
            def _(j):
                off = pl.multiple_of(j * SC_LANES, SC_LANES)
                loc = sl_v[pl.ds(off, SC_LANES)] - base
                mine = (loc >= 0) & (loc < per_w)
                tok = lax.shift_right_logical(c * SC_SCAN + off + lane, 3)
                plsc.store_scatter(idx_v, [jnp.where(mine, loc, 0)], tok, mask=mine)

        _sc_gather_rows(table_hbm, out_hbm, idx_v, rows_v, gsem, wsem, local, per_w)

    return pl.kernel(
        body, out_type=jax.ShapeDtypeStruct((part_slots,) + ROW_TILE, jnp.int32), mesh=_sc_mesh(),
        scratch_types=[pltpu.VMEM((per_w_pad,), jnp.int32), pltpu.VMEM((SC_SCAN,), jnp.int32)]
        + _sc_row_scratch(per_w),
        compiler_params=pltpu.CompilerParams(needs_layout_passes=False),
        name="sc_dispatch")(table, slots)


def _experts_body(be_ref, first_ref, par_ref, next_ref, nextblk_ref, nused_ref, x_ref, w1_hbm, w3_hbm, w2_hbm,
                  *rest, layer, blk0, n_blk):
    y_ref, w1f, w3f, w2f, w1b, w3b, w2b, sem = rest[-8:]
    i = pl.program_id(0)
    b = blk0 + i
    used = b < nused_ref[0]

    def copies(e, slot):
        return (pltpu.make_async_copy(w1_hbm.at[layer, e], w1f.at[slot], sem.at[0, slot]),
                pltpu.make_async_copy(w3_hbm.at[layer, e], w3f.at[slot], sem.at[1, slot]),
                pltpu.make_async_copy(w2_hbm.at[layer, e], w2f.at[slot], sem.at[2, slot]))

    @pl.when(used & (i == 0))
    def _():
        for c in copies(be_ref[b], par_ref[b]):
            c.start()

    @pl.when(used & ((i == 0) | (first_ref[b] == 1)))
    def _():
        slot = par_ref[b]
        for c in copies(be_ref[b], slot):
            c.wait()

        @pl.when((next_ref[b] >= 0) & (nextblk_ref[b] < blk0 + n_blk))
        def _():
            for c in copies(next_ref[b], 1 - slot):
                c.start(priority=1)
        w1b[...] = w1f[slot].astype(BF16)
        w3b[...] = w3f[slot].astype(BF16)
        w2b[...] = w2f[slot].astype(BF16)

    @pl.when(used)
    def _():
        lo, hi = _unpack_bf16_pair(_load_row_tiles_2d(x_ref, EXP_BLOCK))
        lo = lo.astype(BF16)
        hi = hi.astype(BF16)
        half = D // 2
        h1 = (jnp.dot(lo, w1b[:half, :], preferred_element_type=F32)
              + jnp.dot(hi, w1b[half:, :], preferred_element_type=F32))
        h3 = (jnp.dot(lo, w3b[:half, :], preferred_element_type=F32)
              + jnp.dot(hi, w3b[half:, :], preferred_element_type=F32))
        hid = (_silu(h1) * h3).astype(BF16)
        y = jnp.dot(hid, w2b[...], preferred_element_type=F32)
        _store_row_tiles_2d(y_ref, _pack_bf16_pair(y[:, :half], y[:, half:]), EXP_BLOCK)

    @pl.when(jnp.logical_not(used))
    def _():
        y_ref[...] = jnp.zeros_like(y_ref)


def _experts(ctl, xs_part, w1, w3, w2, layer, blk0, n_blk, ys_prev):
    def x_blk(i, *refs):
        n_here = jnp.clip(refs[-1][0] - blk0, 1, n_blk)
        return (jnp.minimum(i, n_here - 1), 0)
    any_spec = pl.BlockSpec(memory_space=pl.ANY)
    in_specs = [pl.BlockSpec((EXP_BLOCK * ROW_TILE[0], LANES), x_blk), any_spec, any_spec, any_spec]
    args = [xs_part.reshape(n_blk * EXP_BLOCK * ROW_TILE[0], LANES), w1, w3, w2]
    aliases = {}
    if ys_prev is not None:
        in_specs.append(any_spec)
        args.append(ys_prev)
        aliases = {len(ctl) + 4: 0}
    grid_spec = pltpu.PrefetchScalarGridSpec(
        num_scalar_prefetch=len(ctl),
        grid=(n_blk,),
        in_specs=in_specs,
        out_specs=pl.BlockSpec((EXP_BLOCK * ROW_TILE[0], LANES), lambda i, *refs: (blk0 + i, 0)),
        scratch_shapes=[pltpu.VMEM((2, D, D_EXPERT), F32), pltpu.VMEM((2, D, D_EXPERT), F32),
                        pltpu.VMEM((2, D_EXPERT, D), F32),
                        pltpu.VMEM((D, D_EXPERT), BF16), pltpu.VMEM((D, D_EXPERT), BF16),
                        pltpu.VMEM((D_EXPERT, D), BF16), pltpu.SemaphoreType.DMA((3, 2))])
    return pl.pallas_call(
        functools.partial(_experts_body, layer=layer, blk0=blk0, n_blk=n_blk), grid_spec=grid_spec,
        out_shape=jax.ShapeDtypeStruct((L_SLOTS * ROW_TILE[0], LANES), jnp.int32),
        input_output_aliases=aliases,
        compiler_params=_cparams(1), name="experts")(*ctl, *args)


TM_FFN = 640


def _ffn_pre_body(x_ref, g_ref, shp_ref, shs_ref, scp_ref, scs_ref, rw_ref, hb_ref, lg_ref, hp_ref):
    i = pl.program_id(0)

    @pl.when(i < N // TM_FFN)
    def _():
        for sidx in range(TM_FFN // SUB):
            rows = pl.ds(sidx * SUB, SUB)
            h = _prenorm_rows(i * (TM_FFN // SUB) + sidx, x_ref[rows, :], g_ref, shp_ref, shs_ref, scp_ref, scs_ref)
            lg_ref[rows, :] = jnp.dot(h, rw_ref[...], precision=HIGHEST, preferred_element_type=F32)
            packed = _pack_bf16_pair(h[:, :D // 2], h[:, D // 2:])
            for c in range(ROW_TILE[0]):
                hp_ref[pl.ds(sidx * SUB * ROW_TILE[0] + c, SUB, stride=ROW_TILE[0]), :] = (
                    packed[:, c * LANES:(c + 1) * LANES])
            hb_ref[rows, :] = h.astype(BF16)

    @pl.when(i >= N // TM_FFN)
    def _():
        hp_ref[...] = jnp.zeros_like(hp_ref)


def _ffn_pre(x, g, mod, layer, rw):
    last = N // TM_FFN - 1
    row = lambda i: (jnp.minimum(i, last), 0)
    mspec = lambda part, rows, blk: pl.BlockSpec((None, rows, D), lambda i, l=layer, p=part, b=blk: (l, b, p))
    return pl.pallas_call(
        _ffn_pre_body,
        grid=(N // TM_FFN + 1,),
        in_specs=[pl.BlockSpec((TM_FFN, D), row), pl.BlockSpec((1, D), lambda i: (0, 0)),
                  mspec(3, 8, 0), mspec(3, SUB, 1), mspec(4, 8, 0), mspec(4, SUB, 1),
                  pl.BlockSpec((None, D, LANES), lambda i, l=layer: (l, 0, 0))],
        out_specs=[pl.BlockSpec((TM_FFN, D), row), pl.BlockSpec((TM_FFN, LANES), row),
                   pl.BlockSpec((TM_FFN * ROW_TILE[0], LANES), lambda i: (i, 0))],
        out_shape=[jax.ShapeDtypeStruct((N, D), BF16), jax.ShapeDtypeStruct((N, LANES), F32),
                   jax.ShapeDtypeStruct(((N + TM_FFN) * ROW_TILE[0], LANES), jnp.int32)],
        compiler_params=_cparams(1), name="ffn_pre")(x, g.reshape(1, D), mod, mod, mod, mod, rw)


TM_SHARED = 640


def _shared_body(h_ref, w13_ref, w2_ref, after_ref, o_ref):
    up = jnp.dot(h_ref[...], w13_ref[...], preferred_element_type=F32)
    hid = (_silu(up[:, :D_EXPERT]) * up[:, D_EXPERT:]).astype(BF16)
    o_ref[...] = jnp.dot(hid, w2_ref[...], preferred_element_type=F32)


def _shared(h, w13, w2, layer, after):
    return pl.pallas_call(
        _shared_body,
        grid=(N // TM_SHARED,),
        in_specs=[pl.BlockSpec((TM_SHARED, D), lambda i: (i, 0)),
                  pl.BlockSpec((None, D, 2 * D_EXPERT), lambda i, l=layer: (l, 0, 0)),
                  pl.BlockSpec((None, D_EXPERT, D), lambda i, l=layer: (l, 0, 0)),
                  pl.BlockSpec(memory_space=pl.ANY)],
        out_specs=pl.BlockSpec((TM_SHARED, D), lambda i: (i, 0)),
        out_shape=jax.ShapeDtypeStruct((N, D), F32),
        compiler_params=_cparams(1), name="shared")(h, w13, w2, after)


def _combine_body(g_ref, wt_ref, sh_ref, x_ref, gn_ref, gp_ref, gs_ref, *rest, tile0, n_out, final):
    outs = rest[-n_out:]
    half = D // 2
    acc_lo = sh_ref[:, :half]
    acc_hi = sh_ref[:, half:]
    wt = wt_ref[...]
    per_tok = TOP_K * ROW_TILE[0]
    for k in range(TOP_K):
        packed = jnp.concatenate([g_ref[pl.ds(k * ROW_TILE[0] + c, SUB, stride=per_tok), :]
                                  for c in range(ROW_TILE[0])], axis=1)
        lo, hi = _unpack_bf16_pair(packed)
        w_c = wt[:, k:k + 1]
        acc_lo = acc_lo + w_c * lo
        acc_hi = acc_hi + w_c * hi
    t = tile0 + pl.program_id(0)
    val = _post_value(t, x_ref[...], jnp.concatenate([acc_lo, acc_hi], axis=1), gn_ref, gp_ref, gs_ref)
    if not final:
        outs[0][...] = val
        outs[1][...] = _prenorm_rows(t, val, *rest[:5]).astype(BF16)
    else:
        @pl.when(t < N_P // SUB)
        def _():
            outs[0][...] = val
        if n_out == 2:
            @pl.when(t >= N_P // SUB)
            def _():
                outs[1][...] = val


def _combine(gathered, wts, shared, x, g_post, mod, layer, tile0, n_tiles, out_prev, final, g_next):
    per_tok = TOP_K * ROW_TILE[0]
    row = lambda i: (tile0 + i, 0)
    in_specs = [pl.BlockSpec((SUB * per_tok, LANES), lambda i: (i, 0)),
                pl.BlockSpec((SUB, LANES), row), pl.BlockSpec((SUB, D), row), pl.BlockSpec((SUB, D), row),
                pl.BlockSpec((1, D), lambda i: (0, 0)),
                pl.BlockSpec((None, 8, D), lambda i, l=layer: (l, 0, 5)),
                pl.BlockSpec((None, SUB, D), lambda i, l=layer: (l, 1, 5))]
    args = [gathered.reshape(n_tiles * SUB * per_tok, LANES), wts, shared, x, g_post.reshape(1, D), mod, mod]
    if not final:
        in_specs += [pl.BlockSpec((1, D), lambda i: (0, 0))] + _mod_specs(layer + 1, 0) + _mod_specs(layer + 1, 1)
        args += [g_next.reshape(1, D), mod, mod, mod, mod]
    aliases = {}
    for k, prev in enumerate(out_prev or ()):
        if k == 0 or not final:
            in_specs.append(pl.BlockSpec(memory_space=pl.ANY))
            args.append(prev)
            aliases[len(args) - 1] = k
    if not final:
        out_specs = [pl.BlockSpec((SUB, D), row), pl.BlockSpec((SUB, D), row)]
        out_shape = [jax.ShapeDtypeStruct((N, D), F32), jax.ShapeDtypeStruct((N, D), BF16)]
    else:
        last_p = N_P // SUB - 1
        out_specs = [pl.BlockSpec((SUB, D), lambda i: (jnp.minimum(tile0 + i, last_p), 0))]
        out_shape = [jax.ShapeDtypeStruct((N_P, D), F32)]
        if tile0 + n_tiles > N_P // SUB:
            out_specs.append(pl.BlockSpec((N_S, D), lambda i: (0, 0)))
            out_shape.append(jax.ShapeDtypeStruct((N_S, D), F32))
    return pl.pallas_call(
        functools.partial(_combine_body, tile0=tile0, n_out=len(out_shape), final=final),
        grid=(n_tiles,),
        in_specs=in_specs, out_specs=out_specs, out_shape=out_shape,
        input_output_aliases=aliases,
        compiler_params=_cparams(1), name="combine")(*args)


def _slots_body(start_ref, eid_ref, pos_ref, o_ref):
    eid = eid_ref[...]
    acc = pos_ref[...]
    for e in range(N_EXPERTS):
        acc = acc + jnp.where(eid == e, start_ref[e], 0)
    o_ref[...] = acc


def _slots(pad_start, eid, pos):
    grid_spec = pltpu.PrefetchScalarGridSpec(
        num_scalar_prefetch=1, grid=(1,),
        in_specs=[pl.BlockSpec((TOP_K, N), lambda i, s: (0, 0)), pl.BlockSpec((TOP_K, N), lambda i, s: (0, 0))],
        out_specs=pl.BlockSpec((TOP_K, N), lambda i, s: (0, 0)))
    return pl.pallas_call(_slots_body, grid_spec=grid_spec,
                          out_shape=jax.ShapeDtypeStruct((TOP_K, N), jnp.int32),
                          compiler_params=_cparams(1), name="slots")(pad_start, eid, pos)


def _put_sample_rows_body(*refs):
    n = len(refs) // 3
    for src, dst in zip(refs[:n], refs[2 * n:]):
        dst[...] = src[...].astype(BF16)


def _put_sample_rows(sample_rows, full):
    n = len(full)
    return pl.pallas_call(
        _put_sample_rows_body,
        grid=(1,),
        in_specs=[pl.BlockSpec((N_S, 512), lambda i: (0, 0))] * n + [pl.BlockSpec(memory_space=pl.ANY)] * n,
        out_specs=[pl.BlockSpec((N_S, 512), lambda i: (N_P // N_S, 0))] * n,
        out_shape=[jax.ShapeDtypeStruct((N, 512), BF16)] * n,
        input_output_aliases={n + k: k for k in range(n)},
        compiler_params=_cparams(1), name="put_sample_rows")(*sample_rows, *full)


COMBINE_RANGES = ((0, 33), (33, 32))

def _prepare_weights(w_in, w_merge_gate, w_branch, w_out, router_w, shared_w1, shared_w3, shared_w2):
    return dict(
        w_main=jnp.concatenate([w_in[:, :, :1536], w_in[:, :, 1552:]], axis=2).astype(BF16),
        w_low=jnp.pad(w_in[:, :, 1536:1552], ((0, 0), (0, 0), (0, LANES - 16))).astype(BF16),
        w_mg=w_merge_gate.astype(BF16), w_br=w_branch.astype(BF16), w_out=w_out.astype(BF16),
        rw=jnp.pad(router_w, ((0, 0), (0, 0), (0, LANES - N_EXPERTS))),
        w13=jnp.concatenate([shared_w1, shared_w3], axis=2).astype(BF16), sw2=shared_w2.astype(BF16))


def _rope_tables(pos):
    half = DK // 2
    inv = ROPE_BASE ** (-jnp.arange(half, dtype=F32) / half)
    ang = pos.astype(F32)[:, None] * inv[None, :]
    cos = jnp.cos(ang)
    sin = jnp.sin(ang)
    return jnp.concatenate([cos, cos], axis=1), jnp.concatenate([-sin, sin], axis=1)


def _layer(l, x, mod, s_gla, s_pool, s_ret, wts, prep, final, prev_gla, prev_ret, h_in):
    (norm_mix_pre, norm_mix_post, norm_ffn_pre, norm_ffn_post, w_in, w_gla_gate, b_gla_gate, gla_norm,
     pool_w, pool_scale, ret_norm, sgu_norm, sgu_w, sgu_b, w_branch, w_merge_gate, b_merge_gate, w_out,
     router_w, router_bias, expert_w1, expert_w3, expert_w2, shared_w1, shared_w3, shared_w2) = wts

    h = _prenorm(x, norm_mix_pre[l], mod, l) if h_in is None else h_in
    p_main = _matmul(h, prep["w_main"], l, 1664, 512, name="inproj")
    p_low = _matmul(h, prep["w_low"], l, 1664, LANES, name="inproj_low")

    w_gate_pad = jnp.pad(w_gla_gate[l], ((0, LANES - 16), (0, 0)))
    b_gate = b_gla_gate[l].reshape(1, HEADS * DK)
    log_gamma = jnp.log1p(-jnp.exp2(-5.0 - jnp.arange(HEADS, dtype=F32)))
    dec_row = jnp.repeat(log_gamma, DK).reshape(1, HEADS * DK)
    cos_p, sin_p = _rope_tables(jnp.arange(T_P))
    cos_p = jnp.tile(cos_p, (1, 2))
    sin_p = jnp.tile(sin_p, (1, 2))
    g_gla = gla_norm[l].reshape(1, HEADS * DV)
    g_ret = ret_norm[l].reshape(1, HEADS * DV)

    oa_p, gla_p = _la_prompt(p_main, C_GQ, C_GK, C_GV, C_GR, p_low, p_low, w_gate_pad, b_gate, g_gla, False)
    oc_p, ret_p = _la_prompt(p_main, C_RQ, C_RK, C_RV, C_RG, cos_p, sin_p, dec_row, b_gate, g_ret, True)
    pw_bf = pool_w[l].astype(BF16)
    pscale = pool_scale[l].reshape(1, 512)
    ob_p = _pool_prompt(p_main, pw_bf, pscale)
    sgu_g = sgu_norm[l].reshape(1, 512)
    od_p = _sgu_prompt(p_main, sgu_g, sgu_w[l], jnp.pad(sgu_b[l].T, ((0, 0), (0, LANES - 4))))

    ps = p_main[N_P:]
    q_t = _to_tiles_t(ps[:, C_GQ:C_GQ + 256])
    k_t = _to_tiles_t(ps[:, C_GK:C_GK + 256])
    glow_t = jnp.pad(_to_tiles_t(p_low[N_P:, :16]), ((0, 0), (0, LANES - 16), (0, 0)))
    w_gate_t = jnp.pad(w_gla_gate[l].T, ((0, 0), (0, LANES - 16)))
    b_col = jnp.broadcast_to(b_gla_gate[l][:, None], (HEADS * DK, LANES))
    logit_t = _gate_logits_t(w_gate_t, glow_t, b_col)
    dummy = jnp.zeros((HEADS * DK, LANES), F32)
    oa_s, gla_s = _la_sample(q_t, k_t, logit_t, dummy, dummy, p_main, C_GV, C_GR, g_gla, s_gla, l, prev_gla, False)
    cos_s, sin_s = _rope_tables(jnp.full((1,), PAST_LEN))
    cos_c = jnp.broadcast_to(jnp.tile(cos_s[0], HEADS)[:, None], (HEADS * DK, LANES))
    sin_c = jnp.broadcast_to(jnp.tile(sin_s[0], HEADS)[:, None], (HEADS * DK, LANES))
    dec_c = jnp.broadcast_to(jnp.repeat(log_gamma, DK)[:, None], (HEADS * DK, LANES))
    rq_t = _to_tiles_t(ps[:, C_RQ:C_RQ + 256])
    rk_t = _to_tiles_t(ps[:, C_RK:C_RK + 256])
    oc_s, ret_s = _la_sample(rq_t, rk_t, dec_c, cos_c, sin_c, p_main, C_RV, C_RG, g_ret, s_ret, l, prev_ret, True)
    sgu_w0 = jnp.repeat(sgu_w[l][:, 0, 0], LANES).reshape(1, 512)
    sgu_b0 = jnp.repeat(sgu_b[l][:, 0], LANES).reshape(1, 512)
    ob_s, od_s, vn_s = _small_sample(p_main, s_pool[l], pw_bf, pscale, sgu_g, sgu_w0, sgu_b0)
    pool_p = jnp.stack([p_main[(b + 1) * T_P - POOL_BUF:(b + 1) * T_P, C_PIN:C_PIN + 512] for b in range(B_P)])
    pool_s = jnp.concatenate([s_pool[l][:, 1:], ps[:, None, C_PIN:C_PIN + 512]], axis=1)

    branches = _put_sample_rows([oa_s, ob_s, oc_s, od_s], [oa_p, ob_p, oc_p, od_p])
    merged = _merge(h, branches, prep["w_mg"], b_merge_gate.reshape(DEPTH, 1, 4 * D), prep["w_br"], l)
    x = _outproj(merged, prep["w_out"], x, norm_mix_post[l], mod, l)

    rb = jnp.broadcast_to(router_bias[l][:, None], (N_EXPERTS, SUB))
    h2, logits, h2_packed = _ffn_pre(x, norm_ffn_pre[l], mod, l, prep["rw"])
    eid, pos, wt, counts = _router(logits, rb)
    counts = counts[:, 0].astype(jnp.int32)
    padded = (counts + EXP_BLOCK - 1) // EXP_BLOCK * EXP_BLOCK
    pad_end = jnp.cumsum(padded)
    pad_start = pad_end - padded
    nused = (pad_end[-1] // EXP_BLOCK).astype(jnp.int32).reshape(1)
    blk_row = jnp.arange(N_BLOCKS, dtype=jnp.int32) * EXP_BLOCK
    block_e = jnp.minimum(jnp.sum((blk_row[:, None] >= pad_end[None, :]).astype(jnp.int32), axis=1),
                          N_EXPERTS - 1)
    first = jnp.concatenate([jnp.ones((1,), jnp.int32), (block_e[1:] != block_e[:-1]).astype(jnp.int32)])
    first = jnp.where(blk_row < pad_end[-1], first, 0)
    par = (jnp.cumsum(first) - 1) % 2
    live = jnp.where(padded > 0, jnp.arange(N_EXPERTS), N_EXPERTS)
    after = jnp.concatenate([lax.cummin(live, reverse=True)[1:], jnp.full((1,), N_EXPERTS)])
    of_block = block_e[:, None] == jnp.arange(N_EXPERTS)
    next_e = jnp.sum(jnp.where(of_block, jnp.where(after < N_EXPERTS, after, -1), 0), axis=1).astype(jnp.int32)
    next_blk = jnp.sum(jnp.where(of_block, pad_end // EXP_BLOCK, 0), axis=1).astype(jnp.int32)
    slots = _slots(pad_start.astype(jnp.int32), eid, pos).T.reshape(N_ASSIGN)
    wt = jnp.pad(wt.T, ((0, 0), (0, LANES - TOP_K)))
    table = h2_packed.reshape((N + TM_FFN,) + ROW_TILE)
    ctl = (block_e, first, par.astype(jnp.int32), next_e, next_blk, nused)
    xs_parts = [_sc_dispatch(table, slots, blk0, n_blk) for blk0, n_blk in PARTS]
    shared = _shared(h2, prep["w13"], prep["sw2"], l, slots)
    ys = None
    for xs_part, (blk0, n_blk) in zip(xs_parts, PARTS):
        ys = _experts(ctl, xs_part, expert_w1, expert_w3, expert_w2, l, blk0, n_blk, ys)
    ys = ys.reshape((L_SLOTS,) + ROW_TILE)
    outs = None
    for tile0, n_tiles in COMBINE_RANGES:
        a0, a1 = tile0 * SUB * TOP_K, (tile0 + n_tiles) * SUB * TOP_K
        outs = _combine(_sc_gather(ys, slots[a0:a1]), wt, shared, x, norm_ffn_post[l], mod, l, tile0, n_tiles,
                        outs, final, None if final else norm_mix_pre[l + 1])
    x, h_next = (tuple(outs), None) if final else outs
    return x, (gla_p, pool_p, pool_s, ret_p, vn_s), gla_s, ret_s, h_next


def kernel(x_prompt, x_sample, c_prompt, c_sample, state_gla, state_pool, state_ret, w_ada, b_ada, norm_mix_pre, norm_mix_post, norm_ffn_pre, norm_ffn_post, w_in, w_gla_gate, b_gla_gate, gla_norm, pool_w, pool_scale, ret_norm, sgu_norm, sgu_w, sgu_b, w_branch, w_merge_gate, b_merge_gate, w_out, router_w, router_bias, expert_w1, expert_w3, expert_w2, shared_w1, shared_w3, shared_w2):
    wts = (norm_mix_pre, norm_mix_post, norm_ffn_pre, norm_ffn_post, w_in, w_gla_gate, b_gla_gate, gla_norm,
           pool_w, pool_scale, ret_norm, sgu_norm, sgu_w, sgu_b, w_branch, w_merge_gate, b_merge_gate, w_out,
           router_w, router_bias, expert_w1, expert_w3, expert_w2, shared_w1, shared_w3, shared_w2)
    c_all = jnp.zeros((MOD_ROWS, D), F32).at[:B_P].set(c_prompt).at[SUB:SUB + N_S].set(c_sample)
    mod = _ada(c_all, w_ada, b_ada)
    x = (x_prompt.reshape(N_P, D), x_sample.reshape(N_S, D))
    prep = _prepare_weights(w_in, w_merge_gate, w_branch, w_out, router_w, shared_w1, shared_w3, shared_w2)
    per_layer = []
    gla_s = ret_s = h = None
    for l in range(DEPTH):
        x, states, gla_s, ret_s, h = _layer(l, x, mod, state_gla, state_pool, state_ret, wts, prep, l == DEPTH - 1,
                                            gla_s, ret_s, h)
        per_layer.append(states)
    gla_p, pool_p, pool_s, ret_p, vn_s = (jnp.stack(z) for z in zip(*per_layer))
    return (x[0].reshape(B_P, T_P, D), x[1].reshape(N_S, 1, D),
            gla_p, gla_s, pool_p, pool_s, ret_p, ret_s, vn_s.reshape(DEPTH, N_S, 1, 512))
```

```python
import functools

import jax
import jax.numpy as jnp
from jax import lax
from jax.experimental import pallas as pl
from jax.experimental.pallas import tpu as pltpu
from jax.experimental.pallas import tpu_sc as plsc

F32 = jnp.float32
BF16 = jnp.bfloat16
HIGHEST = lax.Precision.HIGHEST

D = 2048
B_P, T_P = 4, 2048
N_P = B_P * T_P
N_S = 128
N = N_P + N_S
DEPTH = 2
PAST_LEN = 16384
EPS = 1e-6
HEADS, DK, DV = 4, 64, 128
CHUNK = 64
GATE_TEMP = 16.0
POOL_WINDOWS = (2, 4, 8, 16)
POOL_BUF = 15
ROPE_BASE = 10000.0
N_EXPERTS = 64
TOP_K = 8
D_EXPERT = 512
ROUTED_SCALE = 2.5

LANES = 128
SUB = 128
MOD_ROWS = 256
EXP_BLOCK = 256
N_ASSIGN = N * TOP_K
N_BLOCKS = -(-(N_ASSIGN + N_EXPERTS * (EXP_BLOCK - 1)) // EXP_BLOCK)
L_SLOTS = N_BLOCKS * EXP_BLOCK
VMEM_LIMIT = 56 * 1024 * 1024

C_GQ, C_GK, C_GV, C_GR, C_PIN, C_RQ, C_RK, C_RV, C_RG, C_SU, C_SV = (
    0, 256, 512, 1024, 1536, 2048, 2304, 2560, 3072, 3584, 4096)
P_MAIN = 4608


def _cparams(n_axes=1):
    return pltpu.CompilerParams(dimension_semantics=("arbitrary",) * n_axes,
                                vmem_limit_bytes=VMEM_LIMIT)


def _silu(x):
    return x * jax.nn.sigmoid(x)


def _mod_rows(t, mp_ref, ms_ref):
    b = jnp.minimum(t // (T_P // SUB), B_P - 1)
    return jnp.where(t >= N_P // SUB, ms_ref[...], mp_ref[pl.ds(b, 1), :])


def _mod_specs(layer, part):
    return [pl.BlockSpec((None, 8, D), lambda i, l=layer, p=part: (l, 0, p)),
            pl.BlockSpec((None, SUB, D), lambda i, l=layer, p=part: (l, 1, p))]


def _pack_bf16_pair(lo, hi):
    lo_u = lax.bitcast_convert_type(lo.astype(BF16).astype(F32), jnp.uint32)
    hi_u = lax.bitcast_convert_type(hi.astype(BF16).astype(F32), jnp.uint32)
    return lax.bitcast_convert_type((hi_u & jnp.uint32(0xFFFF0000)) | (lo_u >> 16), jnp.int32)


def _unpack_bf16_pair(w):
    u = lax.bitcast_convert_type(w, jnp.uint32)
    lo = lax.bitcast_convert_type(u << 16, F32)
    hi = lax.bitcast_convert_type(u & jnp.uint32(0xFFFF0000), F32)
    return lo, hi


ROW_TILE = (8, LANES)


def _load_row_tiles(ref):
    return jnp.concatenate([ref[:, c, :] for c in range(ROW_TILE[0])], axis=1)


def _store_row_tiles(ref, val):
    for c in range(ROW_TILE[0]):
        ref[:, c, :] = val[:, c * LANES:(c + 1) * LANES]


def _load_row_tiles_2d(ref, rows):
    return jnp.concatenate([ref[pl.ds(c, rows, stride=ROW_TILE[0]), :] for c in range(ROW_TILE[0])], axis=1)


def _store_row_tiles_2d(ref, val, rows):
    for c in range(ROW_TILE[0]):
        ref[pl.ds(c, rows, stride=ROW_TILE[0]), :] = val[:, c * LANES:(c + 1) * LANES]


def _ada_body(c_ref, w_ref, b_ref, o_ref):
    s = _silu(c_ref[...]).astype(BF16)
    o_ref[...] = jnp.dot(s, w_ref[...].astype(BF16), preferred_element_type=F32) + b_ref[...]


def _ada(c_all, w_ada, b_ada):
    tn = 2048
    return pl.pallas_call(
        _ada_body,
        grid=(DEPTH, 6 * D // tn),
        in_specs=[pl.BlockSpec((MOD_ROWS, D), lambda l, j: (0, 0)),
                  pl.BlockSpec((None, D, tn), lambda l, j: (l, 0, j)),
                  pl.BlockSpec((None, 1, tn), lambda l, j: (l, 0, j))],
        out_specs=pl.BlockSpec((None, MOD_ROWS, tn), lambda l, j: (l, 0, j)),
        out_shape=jax.ShapeDtypeStruct((DEPTH, MOD_ROWS, 6 * D), F32),
        compiler_params=_cparams(2), name="ada")(c_all, w_ada, b_ada.reshape(DEPTH, 1, 6 * D))


ZERO_ROWS = 2 * SUB


def _x_specs(x, tm, n_axes):
    row = (lambda i: (i, 0)) if n_axes == 1 else (lambda i, j: (i, 0))
    if not isinstance(x, tuple):
        return [pl.BlockSpec((tm, D), row)], [x]
    zero = (lambda i: (0, 0)) if n_axes == 1 else (lambda i, j: (0, 0))
    return [pl.BlockSpec((tm, D), row), pl.BlockSpec((N_S, D), zero)], list(x)


def _x_rows(x_refs, t, rows):
    if len(x_refs) == 1:
        return x_refs[0][rows, :]
    return jnp.where(t >= N_P // SUB, x_refs[1][...], x_refs[0][rows, :])


def _prenorm_rows(t, x, g_ref, shp_ref, shs_ref, scp_ref, scs_ref):
    y = x * lax.rsqrt(jnp.mean(x * x, axis=-1, keepdims=True) + EPS) * g_ref[...]
    return y * (1.0 + _mod_rows(t, scp_ref, scs_ref)) + _mod_rows(t, shp_ref, shs_ref)


TM_NORM = 640


def _prenorm_body(*refs):
    g_ref, shp_ref, shs_ref, scp_ref, scs_ref, h_ref = refs[-6:]
    for sidx in range(TM_NORM // SUB):
        rows = pl.ds(sidx * SUB, SUB)
        t = pl.program_id(0) * (TM_NORM // SUB) + sidx
        h_ref[rows, :] = _prenorm_rows(t, _x_rows(refs[:-6], t, rows), g_ref, shp_ref, shs_ref, scp_ref,
                                       scs_ref).astype(BF16)


def _prenorm(x, g, mod, layer):
    x_specs, x_args = _x_specs(x, TM_NORM, 1)
    return pl.pallas_call(
        _prenorm_body,
        grid=(N // TM_NORM,),
        in_specs=x_specs + [pl.BlockSpec((1, D), lambda i: (0, 0))] + _mod_specs(layer, 0) + _mod_specs(layer, 1),
        out_specs=pl.BlockSpec((TM_NORM, D), lambda i: (i, 0)),
        out_shape=jax.ShapeDtypeStruct((N, D), BF16),
        compiler_params=_cparams(1), name="prenorm")(*x_args, g.reshape(1, D), mod, mod, mod, mod)


def _mm_body(x_ref, w_ref, o_ref):
    o_ref[...] = jnp.dot(x_ref[...], w_ref[...], preferred_element_type=F32).astype(o_ref.dtype)


def _matmul(x, w_all, layer, tm, tn, out_dtype=F32, name="mm"):
    m, k = x.shape
    n = w_all.shape[2]
    return pl.pallas_call(
        _mm_body,
        grid=(m // tm, n // tn),
        in_specs=[pl.BlockSpec((tm, k), lambda i, j: (i, 0)),
                  pl.BlockSpec((None, k, tn), lambda i, j, l=layer: (l, 0, j))],
        out_specs=pl.BlockSpec((tm, tn), lambda i, j: (i, j)),
        out_shape=jax.ShapeDtypeStruct((m, n), out_dtype),
        compiler_params=_cparams(2), name=name)(x, w_all)


ROWS_LA = 256


def _swap_halves_lanes(x):
    lane = lax.broadcasted_iota(jnp.int32, x.shape, 1)
    return jnp.where((lane % 64) < 32, pltpu.roll(x, 96, 1), pltpu.roll(x, 32, 1))


def _rope_lanes(x, cos, sin_signed):
    parts = []
    for half in range(2):
        xh = x[:, half * LANES:(half + 1) * LANES]
        parts.append(xh * cos + _swap_halves_lanes(xh) * sin_signed)
    return jnp.concatenate(parts, axis=1)


def _la_prompt_body(q_ref, k_ref, v_ref, r_ref, aux_ref, aux2_ref, dec_ref, bias_ref, g_ref,
                    o_ref, st_out_ref, st_ref, *, retention):
    t = pl.program_id(1)

    @pl.when(t == 0)
    def _():
        st_ref[...] = jnp.zeros_like(st_ref)

    n_ch = ROWS_LA // CHUNK
    ri = lax.broadcasted_iota(jnp.int32, (ROWS_LA, ROWS_LA), 0)
    ci = lax.broadcasted_iota(jnp.int32, (ROWS_LA, ROWS_LA), 1)
    causal = (ri >= ci) & ((ri // CHUNK) == (ci // CHUNK))
    scale = DK ** -0.5

    q = q_ref[...]
    k = k_ref[...]
    v = v_ref[...]
    if retention:
        cos = aux_ref[...]
        sin = aux2_ref[...]
        q = _rope_lanes(q, cos, sin)
        k = _rope_lanes(k, cos, sin) * scale
        step = (lax.broadcasted_iota(jnp.int32, (ROWS_LA, HEADS * DK), 0) % CHUNK + 1).astype(F32)
        bc = step * dec_ref[...]
    else:
        q = q * scale
        logit = jnp.dot(aux_ref[...], dec_ref[...], precision=HIGHEST, preferred_element_type=F32) + bias_ref[...]
        la = jax.nn.log_sigmoid(logit) / GATE_TEMP
        bc = jnp.dot(causal.astype(F32), la, precision=HIGHEST, preferred_element_type=F32)
    bl = bc.reshape(n_ch, CHUNK, HEADS * DK)[:, CHUNK - 1:CHUNK, :]
    bl_rows = jnp.broadcast_to(bl, (n_ch, CHUNK, HEADS * DK)).reshape(ROWS_LA, HEADS * DK)
    qd = q * jnp.exp(bc)
    ki = k * jnp.exp(-bc)
    ke = k * jnp.exp(bl_rows - bc)
    ac = jnp.exp(bl)
    outs = []
    for h in range(HEADS):
        ks = slice(h * DK, (h + 1) * DK)
        vs = slice(h * DV, (h + 1) * DV)
        qd_h = qd[:, ks].astype(BF16)
        ki_h = ki[:, ks].astype(BF16)
        ke_h = ke[:, ks].astype(BF16)
        v_h = v[:, vs].astype(BF16)
        sc = lax.dot_general(qd_h, ki_h, (((1,), (1,)), ((), ())), preferred_element_type=F32)
        sc = jnp.where(causal, sc, 0.0)
        o_h = jnp.dot(sc.astype(BF16), v_h, preferred_element_type=F32)
        inter = []
        for c in range(n_ch):
            rows = slice(c * CHUNK, (c + 1) * CHUNK)
            st = st_ref[h]
            inter.append(lax.dot_general(qd_h[rows], st.astype(BF16), (((1,), (1,)), ((), ())),
                                         preferred_element_type=F32))
            kv_t = lax.dot_general(v_h[rows], ke_h[rows], (((0,), (0,)), ((), ())), preferred_element_type=F32)
            st_ref[h] = st * ac[c][:, ks] + kv_t
        o_h = o_h + jnp.concatenate(inter, axis=0)
        outs.append(o_h * lax.rsqrt(jnp.mean(o_h * o_h, axis=-1, keepdims=True) + EPS) * g_ref[:, vs])
    o_ref[...] = (jnp.concatenate(outs, axis=1) * _silu(r_ref[...])).astype(BF16)
    st_out_ref[...] = st_ref[...]


def _la_prompt(p_main, cq, ck, cv, cr, aux, aux2, dec, bias, g, retention):
    nt = T_P // ROWS_LA
    rowblk = lambda b, t: b * nt + t
    if retention:
        aux_specs = [pl.BlockSpec((ROWS_LA, LANES), lambda b, t: (t, 0)),
                     pl.BlockSpec((ROWS_LA, LANES), lambda b, t: (t, 0))]
    else:
        aux_specs = [pl.BlockSpec((ROWS_LA, LANES), lambda b, t: (rowblk(b, t), 0)),
                     pl.BlockSpec((8, LANES), lambda b, t: (0, 0))]
    o, st = pl.pallas_call(
        functools.partial(_la_prompt_body, retention=retention),
        grid=(B_P, nt),
        in_specs=[pl.BlockSpec((ROWS_LA, 256), lambda b, t: (rowblk(b, t), cq // 256)),
                  pl.BlockSpec((ROWS_LA, 256), lambda b, t: (rowblk(b, t), ck // 256)),
                  pl.BlockSpec((ROWS_LA, 512), lambda b, t: (rowblk(b, t), cv // 512)),
                  pl.BlockSpec((ROWS_LA, 512), lambda b, t: (rowblk(b, t), cr // 512))]
        + aux_specs
        + [pl.BlockSpec(dec.shape, lambda b, t: (0, 0)),
           pl.BlockSpec((1, HEADS * DK), lambda b, t: (0, 0)),
           pl.BlockSpec((1, HEADS * DV), lambda b, t: (0, 0))],
        out_specs=[pl.BlockSpec((ROWS_LA, HEADS * DV), lambda b, t: (rowblk(b, t), 0)),
                   pl.BlockSpec((None, HEADS, DV, DK), lambda b, t: (b, 0, 0, 0))],
        out_shape=[jax.ShapeDtypeStruct((N, HEADS * DV), BF16),
                   jax.ShapeDtypeStruct((B_P, HEADS, DV, DK), F32)],
        scratch_shapes=[pltpu.VMEM((HEADS, DV, DK), F32)],
        compiler_params=_cparams(2), name="ret_prompt" if retention else "gla_prompt",
    )(p_main, p_main, p_main, p_main, aux, aux2, dec, bias, g)
    return o, jnp.swapaxes(st, -1, -2)


SAMPLE_TILE = 8


def _la_sample_body(qt_ref, kt_ref, lt_ref, cos_ref, sin_ref, v_ref, r_ref, g_ref, s_ref, *rest, retention):
    o_ref, s_out_ref = rest[-2:]
    scale = DK ** -0.5
    qt = qt_ref[...]
    kt = kt_ref[...]
    if retention:
        def rope(x):
            sw = jnp.concatenate(
                [x[h * DK + (DK // 2) * (1 - j): h * DK + (DK // 2) * (2 - j), :]
                 for h in range(HEADS) for j in range(2)], axis=0)
            return x * cos_ref[...] + sw * sin_ref[...]
        qt = rope(qt)
        kt = rope(kt) * scale
        la = lt_ref[...]
    else:
        qt = qt * scale
        la = jax.nn.log_sigmoid(lt_ref[...]) / GATE_TEMP
    at = jnp.exp(la)
    qd = qt * at
    ki = kt * jnp.exp(-la)
    prod = qd * ki
    v8 = v_ref[...]
    r8 = r_ref[...]
    g = g_ref[...]
    for j in range(SAMPLE_TILE):
        for h in range(HEADS):
            ks = slice(h * DK, (h + 1) * DK)
            vs = slice(h * DV, (h + 1) * DV)
            a_c = jnp.broadcast_to(at[ks, j:j + 1], (DK, DV))
            k_c = jnp.broadcast_to(kt[ks, j:j + 1], (DK, DV))
            q_c = jnp.broadcast_to(qd[ks, j:j + 1], (DK, DV))
            s_c = jnp.broadcast_to(jnp.sum(prod[ks, j:j + 1], axis=0, keepdims=True), (1, DV))
            s0 = s_ref[j, h]
            v_row = v8[j:j + 1, vs]
            s_out_ref[j, h] = a_c * s0 + k_c * v_row
            o_row = s_c * v_row + jnp.sum(q_c * s0, axis=0, keepdims=True)
            o_n = o_row * lax.rsqrt(jnp.mean(o_row * o_row, axis=-1, keepdims=True) + EPS) * g[:, vs]
            o_ref[j:j + 1, vs] = o_n * _silu(r8[j:j + 1, vs])


def _la_sample(qt, kt, lt, cos_t, sin_t, p_main, cv, cr, g, s0_all, layer, s_prev, retention):
    nt = N_S // SAMPLE_TILE
    row0 = N_P // SAMPLE_TILE
    tile = pl.BlockSpec((None, HEADS * DK, LANES), lambda i: (i, 0, 0))
    full = pl.BlockSpec((HEADS * DK, LANES), lambda i: (0, 0))
    lt_spec = full if retention else tile
    return pl.pallas_call(
        functools.partial(_la_sample_body, retention=retention),
        grid=(nt,),
        in_specs=[tile, tile, lt_spec, full, full,
                  pl.BlockSpec((SAMPLE_TILE, 512), lambda i: (row0 + i, cv // 512)),
                  pl.BlockSpec((SAMPLE_TILE, 512), lambda i: (row0 + i, cr // 512)),
                  pl.BlockSpec((1, HEADS * DV), lambda i: (0, 0)),
                  pl.BlockSpec((None, SAMPLE_TILE, HEADS, DK, DV), lambda i, l=layer: (l, i, 0, 0, 0))]
        + ([] if s_prev is None else [pl.BlockSpec(memory_space=pl.ANY)]),
        out_specs=[pl.BlockSpec((SAMPLE_TILE, HEADS * DV), lambda i: (i, 0)),
                   pl.BlockSpec((None, SAMPLE_TILE, HEADS, DK, DV), lambda i, l=layer: (l, i, 0, 0, 0))],
        out_shape=[jax.ShapeDtypeStruct((N_S, HEADS * DV), F32),
                   jax.ShapeDtypeStruct((DEPTH, N_S, HEADS, DK, DV), F32)],
        input_output_aliases={} if s_prev is None else {9: 1},
        compiler_params=_cparams(1), name="ret_sample" if retention else "gla_sample",
    )(qt, kt, lt, cos_t, sin_t, p_main, p_main, g, s0_all, *([] if s_prev is None else [s_prev]))


def _gate_logits_t_body(w_ref, x_ref, b_ref, o_ref):
    o_ref[...] = jnp.dot(w_ref[...], x_ref[...], precision=HIGHEST, preferred_element_type=F32) + b_ref[...]


def _gate_logits_t(w_gate_t, glow_t, b_col):
    nt = N_S // SAMPLE_TILE
    return pl.pallas_call(
        _gate_logits_t_body,
        grid=(nt,),
        in_specs=[pl.BlockSpec((HEADS * DK, LANES), lambda i: (0, 0)),
                  pl.BlockSpec((None, LANES, LANES), lambda i: (i, 0, 0)),
                  pl.BlockSpec((HEADS * DK, LANES), lambda i: (0, 0))],
        out_specs=pl.BlockSpec((None, HEADS * DK, LANES), lambda i: (i, 0, 0)),
        out_shape=jax.ShapeDtypeStruct((nt, HEADS * DK, LANES), F32),
        compiler_params=_cparams(1), name="gate_logits_t")(w_gate_t, glow_t, b_col)


def _to_tiles_t(x):
    c = x.shape[1]
    xt = jnp.swapaxes(x.reshape(N_S // SAMPLE_TILE, SAMPLE_TILE, c), 1, 2)
    return jnp.pad(xt, ((0, 0), (0, 0), (0, LANES - SAMPLE_TILE)))


ROWS_POOL = 512


def _pool_mix(y, w_ref, sc_ref):
    outs = []
    for gi in range(4):
        cs = slice(gi * LANES, (gi + 1) * LANES)
        outs.append(jnp.dot(y[:, cs].astype(BF16), w_ref[gi], preferred_element_type=F32))
    return jnp.concatenate(outs, axis=1) * sc_ref[...]


def _pool_prompt_body(p_ref, halo_ref, w_ref, sc_ref, o_ref):
    t = pl.program_id(1)
    p = p_ref[...]
    halo = jnp.where(t == 0, 0.0, halo_ref[...])
    full = jnp.concatenate([halo, p], axis=0)
    pos = t * ROWS_POOL + lax.broadcasted_iota(jnp.int32, (ROWS_POOL, LANES), 0)
    means = []
    for gi, w in enumerate(POOL_WINDOWS):
        s = full[:, gi * LANES:(gi + 1) * LANES]
        step = 1
        while step < w:
            s = s + pltpu.roll(s, step, 0)
            step *= 2
        win = s[16:, :]
        cnt = jnp.minimum(w, pos + 1).astype(F32)
        means.append(win / cnt)
    y = jnp.concatenate(means, axis=1) - p
    o_ref[...] = _pool_mix(y, w_ref, sc_ref).astype(BF16)


def _pool_prompt(p_main, w_bf, scale):
    nt = T_P // ROWS_POOL
    return pl.pallas_call(
        _pool_prompt_body,
        grid=(B_P, nt),
        in_specs=[pl.BlockSpec((ROWS_POOL, 512), lambda b, t: (b * nt + t, C_PIN // 512)),
                  pl.BlockSpec((16, 512), lambda b, t: (jnp.maximum((b * nt + t) * (ROWS_POOL // 16) - 1, 0),
                                                        C_PIN // 512)),
                  pl.BlockSpec((4, LANES, LANES), lambda b, t: (0, 0, 0)),
                  pl.BlockSpec((1, 512), lambda b, t: (0, 0))],
        out_specs=pl.BlockSpec((ROWS_POOL, 512), lambda b, t: (b * nt + t, 0)),
        out_shape=jax.ShapeDtypeStruct((N, 512), BF16),
        compiler_params=_cparams(2), name="pool_prompt")(p_main, p_main, w_bf, scale)


def _small_sample_body(p_ref, buf_ref, pw_ref, psc_ref, u_ref, sv_ref, sg_ref, sw_ref, sb_ref,
                       ob_ref, od_ref, vn_ref):
    p = p_ref[...]
    means = []
    for gi, w in enumerate(POOL_WINDOWS):
        cs = slice(gi * LANES, (gi + 1) * LANES)
        s = p[:, cs]
        for j in range(1, w):
            s = s + buf_ref[:, POOL_BUF - j, cs]
        means.append(s / float(min(w, PAST_LEN + 1)))
    y = jnp.concatenate(means, axis=1) - p
    ob_ref[...] = _pool_mix(y, pw_ref, psc_ref)
    sv = sv_ref[...]
    vn = sv * lax.rsqrt(jnp.mean(sv * sv, axis=-1, keepdims=True) + EPS) * sg_ref[...]
    vn_ref[...] = vn
    od_ref[...] = u_ref[...] * (sw_ref[...] * vn + sb_ref[...])


def _small_sample(p_main, buf, pw_bf, pscale, sgu_g, sgu_w0, sgu_b0):
    row = N_P // N_S
    col = lambda c: pl.BlockSpec((N_S, 512), lambda i, c=c: (row, c // 512))
    vec = pl.BlockSpec((1, 512), lambda i: (0, 0))
    return pl.pallas_call(
        _small_sample_body,
        grid=(1,),
        in_specs=[col(C_PIN), pl.BlockSpec((N_S, POOL_BUF, 512), lambda i: (0, 0, 0)),
                  pl.BlockSpec((4, LANES, LANES), lambda i: (0, 0, 0)), vec,
                  col(C_SU), col(C_SV), vec, vec, vec],
        out_specs=[pl.BlockSpec((N_S, 512), lambda i: (0, 0))] * 3,
        out_shape=[jax.ShapeDtypeStruct((N_S, 512), F32)] * 3,
        compiler_params=_cparams(1), name="small_sample",
    )(p_main, buf, pw_bf, pscale, p_main, p_main, sgu_g, sgu_w0, sgu_b0)


ROWS_SGU = 512
SGU_CHUNK = 128


def _sgu_prompt_body(u_ref, v_ref, g_ref, w_ref, bt_ref, o_ref):
    ri = lax.broadcasted_iota(jnp.int32, (SGU_CHUNK, SGU_CHUNK), 0)
    ci = lax.broadcasted_iota(jnp.int32, (SGU_CHUNK, SGU_CHUNK), 1)
    causal = ri >= ci
    for c in range(ROWS_SGU // SGU_CHUNK):
        rows = pl.ds(c * SGU_CHUNK, SGU_CHUNK)
        v = v_ref[rows, :]
        vn = (v * lax.rsqrt(jnp.mean(v * v, axis=-1, keepdims=True) + EPS) * g_ref[...]).astype(BF16)
        outs = []
        for gi in range(4):
            cs = slice(gi * LANES, (gi + 1) * LANES)
            w = jnp.where(causal, w_ref[gi], 0.0).astype(BF16)
            mixed = jnp.dot(w, vn[:, cs], preferred_element_type=F32)
            outs.append(mixed + jnp.broadcast_to(bt_ref[:, gi:gi + 1], (SGU_CHUNK, LANES)))
        o_ref[rows, :] = (u_ref[rows, :] * jnp.concatenate(outs, axis=1)).astype(BF16)


def _sgu_prompt(p_main, g, w, b_t):
    return pl.pallas_call(
        _sgu_prompt_body,
        grid=(N_P // ROWS_SGU,),
        in_specs=[pl.BlockSpec((ROWS_SGU, 512), lambda i: (i, C_SU // 512)),
                  pl.BlockSpec((ROWS_SGU, 512), lambda i: (i, C_SV // 512)),
                  pl.BlockSpec((1, 512), lambda i: (0, 0)),
                  pl.BlockSpec((4, SGU_CHUNK, SGU_CHUNK), lambda i: (0, 0, 0)),
                  pl.BlockSpec((SGU_CHUNK, LANES), lambda i: (0, 0))],
        out_specs=pl.BlockSpec((ROWS_SGU, 512), lambda i: (i, 0)),
        out_shape=jax.ShapeDtypeStruct((N, 512), BF16),
        compiler_params=_cparams(1), name="sgu_prompt")(p_main, p_main, g, w, b_t)


TM_MERGE = 640
TN_MERGE = 512


def _merge_body(h_ref, ba_ref, bb_ref, bc_ref, bd_ref, g0, g1, g2, g3, u0, u1, u2, u3,
                c0, c1, c2, c3, o_ref):
    h = h_ref[...]
    acc = None
    for br, gw, uw, gb in ((ba_ref, g0, u0, c0), (bb_ref, g1, u1, c1), (bc_ref, g2, u2, c2), (bd_ref, g3, u3, c3)):
        gate = jax.nn.sigmoid(jnp.dot(h, gw[...], preferred_element_type=F32) + gb[...])
        up = jnp.dot(br[...], uw[...], preferred_element_type=F32)
        acc = gate * up if acc is None else acc + gate * up
    o_ref[...] = acc.astype(BF16)


def _merge(h, branches, w_mg, b_mg, w_br, layer):
    nj = D // TN_MERGE
    row = lambda w: pl.BlockSpec((TM_MERGE, w), lambda i, j: (i, 0))
    gate_w = [pl.BlockSpec((None, D, TN_MERGE), lambda i, j, b=b, l=layer: (l, 0, b * nj + j)) for b in range(4)]
    up_w = [pl.BlockSpec((None, None, 512, TN_MERGE), lambda i, j, b=b, l=layer: (l, b, 0, j)) for b in range(4)]
    gate_b = [pl.BlockSpec((None, 1, TN_MERGE), lambda i, j, b=b, l=layer: (l, 0, b * nj + j)) for b in range(4)]
    return pl.pallas_call(
        _merge_body,
        grid=(N // TM_MERGE, nj),
        in_specs=[row(D)] + [row(512)] * 4 + gate_w + up_w + gate_b,
        out_specs=pl.BlockSpec((TM_MERGE, TN_MERGE), lambda i, j: (i, j)),
        out_shape=jax.ShapeDtypeStruct((N, D), BF16),
        compiler_params=_cparams(2), name="merge",
    )(h, *branches, w_mg, w_mg, w_mg, w_mg, w_br, w_br, w_br, w_br, b_mg, b_mg, b_mg, b_mg)


TM_OUT = 640
TN_OUT = 1024


def _post_value(t, x, y, gn_ref, gp_ref, gs_ref):
    yn = y * lax.rsqrt(jnp.mean(y * y, axis=-1, keepdims=True) + EPS) * gn_ref[...]
    return x + _mod_rows(t, gp_ref, gs_ref) * yn


def _outproj_body(m_ref, w_ref, *refs):
    gn_ref, gp_ref, gs_ref, o_ref, acc_ref = refs[-5:]
    j = pl.program_id(1)
    acc_ref[j] = jnp.dot(m_ref[...], w_ref[...], preferred_element_type=F32)

    @pl.when(j == D // TN_OUT - 1)
    def _():
        for sidx in range(TM_OUT // SUB):
            rows = pl.ds(sidx * SUB, SUB)
            t = pl.program_id(0) * (TM_OUT // SUB) + sidx
            y = jnp.concatenate([acc_ref[c, rows, :] for c in range(D // TN_OUT)], axis=1)
            o_ref[rows, :] = _post_value(t, _x_rows(refs[:-5], t, rows), y, gn_ref, gp_ref, gs_ref)


def _outproj(merged, w_out, x, g_post, mod, layer):
    mspec = [pl.BlockSpec((None, 8, D), lambda i, j, l=layer: (l, 0, 2)),
             pl.BlockSpec((None, SUB, D), lambda i, j, l=layer: (l, 1, 2))]
    x_specs, x_args = _x_specs(x, TM_OUT, 2)
    return pl.pallas_call(
        _outproj_body,
        grid=(N // TM_OUT, D // TN_OUT),
        in_specs=[pl.BlockSpec((TM_OUT, D), lambda i, j: (i, 0)),
                  pl.BlockSpec((None, D, TN_OUT), lambda i, j, l=layer: (l, 0, j))] + x_specs
        + [pl.BlockSpec((1, D), lambda i, j: (0, 0))] + mspec,
        out_specs=pl.BlockSpec((TM_OUT, D), lambda i, j: (i, 0)),
        out_shape=jax.ShapeDtypeStruct((N, D), F32),
        scratch_shapes=[pltpu.VMEM((D // TN_OUT, TM_OUT, TN_OUT), F32)],
        compiler_params=_cparams(2), name="outproj",
    )(merged, w_out, *x_args, g_post.reshape(1, D), mod, mod)


def _router_body(lg_ref, b_ref, eid_ref, pos_ref, wt_ref, cnt_ref, run_ref):
    i = pl.program_id(0)

    @pl.when(i == 0)
    def _():
        run_ref[...] = jnp.zeros_like(run_ref)

    ng, gs = 8, N_EXPERTS // 8
    neg = -jnp.inf
    scores = jax.nn.sigmoid(lg_ref[...].T[:N_EXPERTS, :])
    sel = scores + b_ref[...]
    sel3 = sel.reshape(ng, gs, SUB)
    sub3 = lax.broadcasted_iota(jnp.int32, (ng, gs, SUB), 1)
    gmax = jnp.max(sel3, axis=1, keepdims=True)
    first = jnp.min(jnp.where(sel3 == gmax, sub3, gs), axis=1, keepdims=True)
    gmax2 = jnp.max(jnp.where(sub3 == first, neg, sel3), axis=1, keepdims=True)
    gscore = (gmax + gmax2).reshape(ng, SUB)
    gidx = lax.broadcasted_iota(jnp.int32, (ng, SUB), 0)
    grank = jnp.zeros((ng, SUB), jnp.int32)
    for s in range(1, ng):
        other = pltpu.roll(gscore, s, 0)
        lower = gidx >= s
        grank += ((other > gscore) | ((other == gscore) & lower)).astype(jnp.int32)
    keep = jnp.broadcast_to((grank < 4).reshape(ng, 1, SUB), (ng, gs, SUB))
    masked = jnp.where(keep, sel3, neg).reshape(N_EXPERTS, SUB)
    eidx = lax.broadcasted_iota(jnp.int32, (N_EXPERTS, SUB), 0)
    rank = jnp.zeros((N_EXPERTS, SUB), jnp.int32)
    for s in range(1, N_EXPERTS):
        other = pltpu.roll(masked, s, 0)
        lower = eidx >= s
        rank += ((other > masked) | ((other == masked) & lower)).astype(jnp.int32)
    chosen = rank < TOP_K
    w_sel = jnp.where(chosen, scores, 0.0)
    w_sel = w_sel / jnp.sum(w_sel, axis=0, keepdims=True) * ROUTED_SCALE
    ri = lax.broadcasted_iota(jnp.int32, (SUB, SUB), 0)
    ci = lax.broadcasted_iota(jnp.int32, (SUB, SUB), 1)
    onehot = chosen.astype(BF16)
    pos = jnp.dot(onehot, (ri < ci).astype(BF16), preferred_element_type=F32) + run_ref[...]
    run_ref[...] = run_ref[...] + jnp.sum(chosen.astype(F32), axis=1, keepdims=True)
    cnt_ref[...] = run_ref[...]
    eidx_f = eidx.astype(F32)
    rows_e, rows_p, rows_w = [], [], []
    for kk in range(TOP_K):
        m = chosen & (rank == kk)
        rows_e.append(jnp.sum(jnp.where(m, eidx_f, 0.0), axis=0, keepdims=True))
        rows_p.append(jnp.sum(jnp.where(m, pos, 0.0), axis=0, keepdims=True))
        rows_w.append(jnp.sum(jnp.where(m, w_sel, 0.0), axis=0, keepdims=True))
    eid_ref[...] = jnp.concatenate(rows_e, axis=0).astype(jnp.int32)
    pos_ref[...] = jnp.concatenate(rows_p, axis=0).astype(jnp.int32)
    wt_ref[...] = jnp.concatenate(rows_w, axis=0)


def _router(logits, rb_col):
    tile = pl.BlockSpec((TOP_K, SUB), lambda i: (0, i))
    return pl.pallas_call(
        _router_body,
        grid=(N // SUB,),
        in_specs=[pl.BlockSpec((SUB, LANES), lambda i: (i, 0)),
                  pl.BlockSpec((N_EXPERTS, SUB), lambda i: (0, 0))],
        out_specs=[tile, tile, tile, pl.BlockSpec((N_EXPERTS, SUB), lambda i: (0, 0))],
        out_shape=[jax.ShapeDtypeStruct((TOP_K, N), jnp.int32), jax.ShapeDtypeStruct((TOP_K, N), jnp.int32),
                   jax.ShapeDtypeStruct((TOP_K, N), F32), jax.ShapeDtypeStruct((N_EXPERTS, SUB), F32)],
        scratch_shapes=[pltpu.VMEM((N_EXPERTS, SUB), F32)],
        compiler_params=_cparams(1), name="router")(logits, rb_col)


SC_CORES, SC_SUBCORES = 2, 16
SC_WORKERS = SC_CORES * SC_SUBCORES
SC_LANES = 16
SC_SCAN = N_ASSIGN // SC_WORKERS


def _sc_mesh():
    return plsc.VectorSubcoreMesh(core_axis_name="c", subcore_axis_name="s",
                                  num_cores=SC_CORES, num_subcores=SC_SUBCORES)


def _sc_worker_base(per_w):
    return (lax.axis_index("s") * SC_CORES + lax.axis_index("c")) * per_w


def _sc_chunk(per_w):
    return max(c for c in (8, 16, 24, 32) if per_w % c == 0)


def _sc_gather_rows(table_hbm, out_hbm, idx_v, rows_v, gsem, wsem, base, per_w):
    chunk = _sc_chunk(per_w)
    n_ch = per_w // chunk

    def gather(j, p):
        off = pl.multiple_of(j * chunk, 8)
        return pltpu.make_async_copy(table_hbm.at[idx_v.at[pl.ds(off, chunk)]], rows_v.at[p], gsem.at[p])

    def write(j, p):
        off = pl.multiple_of(j * chunk, 8)
        return pltpu.make_async_copy(rows_v.at[p], out_hbm.at[pl.ds(base + off, chunk)], wsem.at[p])

    def when(cond, fn):
        if isinstance(cond, bool):
            if cond:
                fn()
        else:
            pl.when(cond)(fn)

    def step(j, p):
        gather(j, p).wait()
        when(j >= 1, lambda: write(j - 1, 1 - p).wait())
        when(j + 1 < n_ch, lambda: gather(j + 1, 1 - p).start())
        write(j, p).start()

    gather(0, 0).start()

    @pl.loop(0, n_ch - n_ch % 2, step=2)
    def _(j0):
        for p in range(2):
            step(j0 + p, p)

    if n_ch % 2:
        step(n_ch - 1, 0)
    write(n_ch - 1, (n_ch - 1) % 2).wait()


def _sc_row_scratch(per_w):
    return [pltpu.VMEM((2, _sc_chunk(per_w)) + ROW_TILE, jnp.int32),
            pltpu.SemaphoreType.DMA((2,)), pltpu.SemaphoreType.DMA((2,))]


def _sc_gather(table, idx):
    n_out = idx.shape[0]
    per_w = n_out // SC_WORKERS
    assert per_w * SC_WORKERS == n_out and per_w % 8 == 0

    def body(table_hbm, idx_hbm, out_hbm, idx_v, rows_v, gsem, wsem):
        base = _sc_worker_base(per_w)
        pltpu.sync_copy(idx_hbm.at[pl.ds(base, per_w)], idx_v)
        _sc_gather_rows(table_hbm, out_hbm, idx_v, rows_v, gsem, wsem, base, per_w)

    return pl.kernel(
        body, out_type=jax.ShapeDtypeStruct((n_out,) + ROW_TILE, jnp.int32), mesh=_sc_mesh(),
        scratch_types=[pltpu.VMEM((per_w,), jnp.int32)] + _sc_row_scratch(per_w), name="sc_gather")(table, idx)


PARTS = ((0, 54), (54, 108), (162, 162))
assert sum(n for _, n in PARTS) == N_BLOCKS and all(PARTS[i][0] + PARTS[i][1] == PARTS[i + 1][0]
                                                    for i in range(len(PARTS) - 1))


def _sc_dispatch(table, slots, blk0, n_blk):
    part_slots = n_blk * EXP_BLOCK
    per_w = part_slots // SC_WORKERS
    per_w_pad = -(-per_w // SC_LANES) * SC_LANES
    assert per_w * SC_WORKERS == part_slots and per_w % 8 == 0 and SC_SCAN % SC_LANES == 0
    assert ZERO_ROWS & (ZERO_ROWS - 1) == 0

    def body(table_hbm, slots_hbm, out_hbm, idx_v, sl_v, rows_v, gsem, wsem):
        local = _sc_worker_base(per_w)
        base = blk0 * EXP_BLOCK + local
        lane = lax.iota(jnp.int32, SC_LANES)

        @pl.loop(0, per_w_pad // SC_LANES)
        def _(j):
            off = pl.multiple_of(j * SC_LANES, SC_LANES)
            idx_v[pl.ds(off, SC_LANES)] = N + ((base + off + lane) & (ZERO_ROWS - 1))

        @pl.loop(0, N_ASSIGN // SC_SCAN)
        def _(c):
            pltpu.sync_copy(slots_hbm.at[pl.ds(pl.multiple_of(c * SC_SCAN, 8), SC_SCAN)], sl_v)

            @pl.loop(0, SC_SCAN // SC_LANES)
            def _(j):
                off = pl.multiple_of(j * SC_LANES, SC_LANES)
                loc = sl_v[pl.ds(off, SC_LANES)] - base
                mine = (loc >= 0) & (loc < per_w)
                tok = lax.shift_right_logical(c * SC_SCAN + off + lane, 3)
                plsc.store_scatter(idx_v, [jnp.where(mine, loc, 0)], tok, mask=mine)

        _sc_gather_rows(table_hbm, out_hbm, idx_v, rows_v, gsem, wsem, local, per_w)

    return pl.kernel(
        body, out_type=jax.ShapeDtypeStruct((part_slots,) + ROW_TILE, jnp.int32), mesh=_sc_mesh(),
        scratch_types=[pltpu.VMEM((per_w_pad,), jnp.int32), pltpu.VMEM((SC_SCAN,), jnp.int32)]
        + _sc_row_scratch(per_w),
        compiler_params=pltpu.CompilerParams(needs_layout_passes=False),
        name="sc_dispatch")(table, slots)


def _experts_body(be_ref, first_ref, par_ref, next_ref, nextblk_ref, nused_ref, x_ref, w1_hbm, w3_hbm, w2_hbm,
                  *rest, layer, blk0, n_blk):
    y_ref, w1f, w3f, w2f, w1b, w3b, w2b, sem = rest[-8:]
    i = pl.program_id(0)
    b = blk0 + i
    used = b < nused_ref[0]

    def copies(e, slot):
        return (pltpu.make_async_copy(w1_hbm.at[layer, e], w1f.at[slot], sem.at[0, slot]),
                pltpu.make_async_copy(w3_hbm.at[layer, e], w3f.at[slot], sem.at[1, slot]),
                pltpu.make_async_copy(w2_hbm.at[layer, e], w2f.at[slot], sem.at[2, slot]))

    @pl.when(used & (i == 0))
    def _():
        for c in copies(be_ref[b], par_ref[b]):
            c.start()

    @pl.when(used & ((i == 0) | (first_ref[b] == 1)))
    def _():
        slot = par_ref[b]
        for c in copies(be_ref[b], slot):
            c.wait()

        @pl.when((next_ref[b] >= 0) & (nextblk_ref[b] < blk0 + n_blk))
        def _():
            for c in copies(next_ref[b], 1 - slot):
                c.start(priority=1)
        w1b[...] = w1f[slot].astype(BF16)
        w3b[...] = w3f[slot].astype(BF16)
        w2b[...] = w2f[slot].astype(BF16)

    @pl.when(used)
    def _():
        lo, hi = _unpack_bf16_pair(_load_row_tiles_2d(x_ref, EXP_BLOCK))
        lo = lo.astype(BF16)
        hi = hi.astype(BF16)
        half = D // 2
        h1 = (jnp.dot(lo, w1b[:half, :], preferred_element_type=F32)
              + jnp.dot(hi, w1b[half:, :], preferred_element_type=F32))
        h3 = (jnp.dot(lo, w3b[:half, :], preferred_element_type=F32)
              + jnp.dot(hi, w3b[half:, :], preferred_element_type=F32))
        hid = (_silu(h1) * h3).astype(BF16)
        y = jnp.dot(hid, w2b[...], preferred_element_type=F32)
        _store_row_tiles_2d(y_ref, _pack_bf16_pair(y[:, :half], y[:, half:]), EXP_BLOCK)

    @pl.when(jnp.logical_not(used))
    def _():
        y_ref[...] = jnp.zeros_like(y_ref)


def _experts(ctl, xs_part, w1, w3, w2, layer, blk0, n_blk, ys_prev):
    def x_blk(i, *refs):
        n_here = jnp.clip(refs[-1][0] - blk0, 1, n_blk)
        return (jnp.minimum(i, n_here - 1), 0)
    any_spec = pl.BlockSpec(memory_space=pl.ANY)
    in_specs = [pl.BlockSpec((EXP_BLOCK * ROW_TILE[0], LANES), x_blk), any_spec, any_spec, any_spec]
    args = [xs_part.reshape(n_blk * EXP_BLOCK * ROW_TILE[0], LANES), w1, w3, w2]
    aliases = {}
    if ys_prev is not None:
        in_specs.append(any_spec)
        args.append(ys_prev)
        aliases = {len(ctl) + 4: 0}
    grid_spec = pltpu.PrefetchScalarGridSpec(
        num_scalar_prefetch=len(ctl),
        grid=(n_blk,),
        in_specs=in_specs,
        out_specs=pl.BlockSpec((EXP_BLOCK * ROW_TILE[0], LANES), lambda i, *refs: (blk0 + i, 0)),
        scratch_shapes=[pltpu.VMEM((2, D, D_EXPERT), F32), pltpu.VMEM((2, D, D_EXPERT), F32),
                        pltpu.VMEM((2, D_EXPERT, D), F32),
                        pltpu.VMEM((D, D_EXPERT), BF16), pltpu.VMEM((D, D_EXPERT), BF16),
                        pltpu.VMEM((D_EXPERT, D), BF16), pltpu.SemaphoreType.DMA((3, 2))])
    return pl.pallas_call(
        functools.partial(_experts_body, layer=layer, blk0=blk0, n_blk=n_blk), grid_spec=grid_spec,
        out_shape=jax.ShapeDtypeStruct((L_SLOTS * ROW_TILE[0], LANES), jnp.int32),
        input_output_aliases=aliases,
        compiler_params=_cparams(1), name="experts")(*ctl, *args)


TM_FFN = 640


def _ffn_pre_body(x_ref, g_ref, shp_ref, shs_ref, scp_ref, scs_ref, rw_ref, hb_ref, lg_ref, hp_ref):
    i = pl.program_id(0)

    @pl.when(i < N // TM_FFN)
    def _():
        for sidx in range(TM_FFN // SUB):
            rows = pl.ds(sidx * SUB, SUB)
            h = _prenorm_rows(i * (TM_FFN // SUB) + sidx, x_ref[rows, :], g_ref, shp_ref, shs_ref, scp_ref, scs_ref)
            lg_ref[rows, :] = jnp.dot(h, rw_ref[...], precision=HIGHEST, preferred_element_type=F32)
            packed = _pack_bf16_pair(h[:, :D // 2], h[:, D // 2:])
            for c in range(ROW_TILE[0]):
                hp_ref[pl.ds(sidx * SUB * ROW_TILE[0] + c, SUB, stride=ROW_TILE[0]), :] = (
                    packed[:, c * LANES:(c + 1) * LANES])
            hb_ref[rows, :] = h.astype(BF16)

    @pl.when(i >= N // TM_FFN)
    def _():
        hp_ref[...] = jnp.zeros_like(hp_ref)


def _ffn_pre(x, g, mod, layer, rw):
    last = N // TM_FFN - 1
    row = lambda i: (jnp.minimum(i, last), 0)
    mspec = lambda part, rows, blk: pl.BlockSpec((None, rows, D), lambda i, l=layer, p=part, b=blk: (l, b, p))
    return pl.pallas_call(
        _ffn_pre_body,
        grid=(N // TM_FFN + 1,),
        in_specs=[pl.BlockSpec((TM_FFN, D), row), pl.BlockSpec((1, D), lambda i: (0, 0)),
                  mspec(3, 8, 0), mspec(3, SUB, 1), mspec(4, 8, 0), mspec(4, SUB, 1),
                  pl.BlockSpec((None, D, LANES), lambda i, l=layer: (l, 0, 0))],
        out_specs=[pl.BlockSpec((TM_FFN, D), row), pl.BlockSpec((TM_FFN, LANES), row),
                   pl.BlockSpec((TM_FFN * ROW_TILE[0], LANES), lambda i: (i, 0))],
        out_shape=[jax.ShapeDtypeStruct((N, D), BF16), jax.ShapeDtypeStruct((N, LANES), F32),
                   jax.ShapeDtypeStruct(((N + TM_FFN) * ROW_TILE[0], LANES), jnp.int32)],
        compiler_params=_cparams(1), name="ffn_pre")(x, g.reshape(1, D), mod, mod, mod, mod, rw)


TM_SHARED = 640


def _shared_body(h_ref, w13_ref, w2_ref, after_ref, o_ref):
    up = jnp.dot(h_ref[...], w13_ref[...], preferred_element_type=F32)
    hid = (_silu(up[:, :D_EXPERT]) * up[:, D_EXPERT:]).astype(BF16)
    o_ref[...] = jnp.dot(hid, w2_ref[...], preferred_element_type=F32)


def _shared(h, w13, w2, layer, after):
    return pl.pallas_call(
        _shared_body,
        grid=(N // TM_SHARED,),
        in_specs=[pl.BlockSpec((TM_SHARED, D), lambda i: (i, 0)),
                  pl.BlockSpec((None, D, 2 * D_EXPERT), lambda i, l=layer: (l, 0, 0)),
                  pl.BlockSpec((None, D_EXPERT, D), lambda i, l=layer: (l, 0, 0)),
                  pl.BlockSpec(memory_space=pl.ANY)],
        out_specs=pl.BlockSpec((TM_SHARED, D), lambda i: (i, 0)),
        out_shape=jax.ShapeDtypeStruct((N, D), F32),
        compiler_params=_cparams(1), name="shared")(h, w13, w2, after)


def _combine_body(g_ref, wt_ref, sh_ref, x_ref, gn_ref, gp_ref, gs_ref, *rest, tile0, n_out, final):
    outs = rest[-n_out:]
    half = D // 2
    acc_lo = sh_ref[:, :half]
    acc_hi = sh_ref[:, half:]
    wt = wt_ref[...]
    per_tok = TOP_K * ROW_TILE[0]
    for k in range(TOP_K):
        packed = jnp.concatenate([g_ref[pl.ds(k * ROW_TILE[0] + c, SUB, stride=per_tok), :]
                                  for c in range(ROW_TILE[0])], axis=1)
        lo, hi = _unpack_bf16_pair(packed)
        w_c = wt[:, k:k + 1]
        acc_lo = acc_lo + w_c * lo
        acc_hi = acc_hi + w_c * hi
    t = tile0 + pl.program_id(0)
    val = _post_value(t, x_ref[...], jnp.concatenate([acc_lo, acc_hi], axis=1), gn_ref, gp_ref, gs_ref)
    if not final:
        outs[0][...] = val
        outs[1][...] = _prenorm_rows(t, val, *rest[:5]).astype(BF16)
    else:
        @pl.when(t < N_P // SUB)
        def _():
            outs[0][...] = val
        if n_out == 2:
            @pl.when(t >= N_P // SUB)
            def _():
                outs[1][...] = val


def _combine(gathered, wts, shared, x, g_post, mod, layer, tile0, n_tiles, out_prev, final, g_next):
    per_tok = TOP_K * ROW_TILE[0]
    row = lambda i: (tile0 + i, 0)
    in_specs = [pl.BlockSpec((SUB * per_tok, LANES), lambda i: (i, 0)),
                pl.BlockSpec((SUB, LANES), row), pl.BlockSpec((SUB, D), row), pl.BlockSpec((SUB, D), row),
                pl.BlockSpec((1, D), lambda i: (0, 0)),
                pl.BlockSpec((None, 8, D), lambda i, l=layer: (l, 0, 5)),
                pl.BlockSpec((None, SUB, D), lambda i, l=layer: (l, 1, 5))]
    args = [gathered.reshape(n_tiles * SUB * per_tok, LANES), wts, shared, x, g_post.reshape(1, D), mod, mod]
    if not final:
        in_specs += [pl.BlockSpec((1, D), lambda i: (0, 0))] + _mod_specs(layer + 1, 0) + _mod_specs(layer + 1, 1)
        args += [g_next.reshape(1, D), mod, mod, mod, mod]
    aliases = {}
    for k, prev in enumerate(out_prev or ()):
        if k == 0 or not final:
            in_specs.append(pl.BlockSpec(memory_space=pl.ANY))
            args.append(prev)
            aliases[len(args) - 1] = k
    if not final:
        out_specs = [pl.BlockSpec((SUB, D), row), pl.BlockSpec((SUB, D), row)]
        out_shape = [jax.ShapeDtypeStruct((N, D), F32), jax.ShapeDtypeStruct((N, D), BF16)]
    else:
        last_p = N_P // SUB - 1
        out_specs = [pl.BlockSpec((SUB, D), lambda i: (jnp.minimum(tile0 + i, last_p), 0))]
        out_shape = [jax.ShapeDtypeStruct((N_P, D), F32)]
        if tile0 + n_tiles > N_P // SUB:
            out_specs.append(pl.BlockSpec((N_S, D), lambda i: (0, 0)))
            out_shape.append(jax.ShapeDtypeStruct((N_S, D), F32))
    return pl.pallas_call(
        functools.partial(_combine_body, tile0=tile0, n_out=len(out_shape), final=final),
        grid=(n_tiles,),
        in_specs=in_specs, out_specs=out_specs, out_shape=out_shape,
        input_output_aliases=aliases,
        compiler_params=_cparams(1), name="combine")(*args)


def _slots_body(start_ref, eid_ref, pos_ref, o_ref):
    eid = eid_ref[...]
    acc = pos_ref[...]
    for e in range(N_EXPERTS):
        acc = acc + jnp.where(eid == e, start_ref[e], 0)
    o_ref[...] = acc


def _slots(pad_start, eid, pos):
    grid_spec = pltpu.PrefetchScalarGridSpec(
        num_scalar_prefetch=1, grid=(1,),
        in_specs=[pl.BlockSpec((TOP_K, N), lambda i, s: (0, 0)), pl.BlockSpec((TOP_K, N), lambda i, s: (0, 0))],
        out_specs=pl.BlockSpec((TOP_K, N), lambda i, s: (0, 0)))
    return pl.pallas_call(_slots_body, grid_spec=grid_spec,
                          out_shape=jax.ShapeDtypeStruct((TOP_K, N), jnp.int32),
                          compiler_params=_cparams(1), name="slots")(pad_start, eid, pos)


def _put_sample_rows_body(*refs):
    n = len(refs) // 3
    for src, dst in zip(refs[:n], refs[2 * n:]):
        dst[...] = src[...].astype(BF16)


def _put_sample_rows(sample_rows, full):
    n = len(full)
    return pl.pallas_call(
        _put_sample_rows_body,
        grid=(1,),
        in_specs=[pl.BlockSpec((N_S, 512), lambda i: (0, 0))] * n + [pl.BlockSpec(memory_space=pl.ANY)] * n,
        out_specs=[pl.BlockSpec((N_S, 512), lambda i: (N_P // N_S, 0))] * n,
        out_shape=[jax.ShapeDtypeStruct((N, 512), BF16)] * n,
        input_output_aliases={n + k: k for k in range(n)},
        compiler_params=_cparams(1), name="put_sample_rows")(*sample_rows, *full)


COMBINE_RANGES = ((0, 33), (33, 32))

def _prepare_weights(w_in, w_merge_gate, w_branch, w_out, router_w, shared_w1, shared_w3, shared_w2):
    return dict(
        w_main=jnp.concatenate([w_in[:, :, :1536], w_in[:, :, 1552:]], axis=2).astype(BF16),
        w_low=jnp.pad(w_in[:, :, 1536:1552], ((0, 0), (0, 0), (0, LANES - 16))).astype(BF16),
        w_mg=w_merge_gate.astype(BF16), w_br=w_branch.astype(BF16), w_out=w_out.astype(BF16),
        rw=jnp.pad(router_w, ((0, 0), (0, 0), (0, LANES - N_EXPERTS))),
        w13=jnp.concatenate([shared_w1, shared_w3], axis=2).astype(BF16), sw2=shared_w2.astype(BF16))


def _rope_tables(pos):
    half = DK // 2
    inv = ROPE_BASE ** (-jnp.arange(half, dtype=F32) / half)
    ang = pos.astype(F32)[:, None] * inv[None, :]
    cos = jnp.cos(ang)
    sin = jnp.sin(ang)
    return jnp.concatenate([cos, cos], axis=1), jnp.concatenate([-sin, sin], axis=1)


def _layer(l, x, mod, s_gla, s_pool, s_ret, wts, prep, final, prev_gla, prev_ret, h_in):
    (norm_mix_pre, norm_mix_post, norm_ffn_pre, norm_ffn_post, w_in, w_gla_gate, b_gla_gate, gla_norm,
     pool_w, pool_scale, ret_norm, sgu_norm, sgu_w, sgu_b, w_branch, w_merge_gate, b_merge_gate, w_out,
     router_w, router_bias, expert_w1, expert_w3, expert_w2, shared_w1, shared_w3, shared_w2) = wts

    h = _prenorm(x, norm_mix_pre[l], mod, l) if h_in is None else h_in
    p_main = _matmul(h, prep["w_main"], l, 1664, 768, name="inproj")
    p_low = _matmul(h, prep["w_low"], l, 1664, LANES, name="inproj_low")

    w_gate_pad = jnp.pad(w_gla_gate[l], ((0, LANES - 16), (0, 0)))
    b_gate = b_gla_gate[l].reshape(1, HEADS * DK)
    log_gamma = jnp.log1p(-jnp.exp2(-5.0 - jnp.arange(HEADS, dtype=F32)))
    dec_row = jnp.repeat(log_gamma, DK).reshape(1, HEADS * DK)
    cos_p, sin_p = _rope_tables(jnp.arange(T_P))
    cos_p = jnp.tile(cos_p, (1, 2))
    sin_p = jnp.tile(sin_p, (1, 2))
    g_gla = gla_norm[l].reshape(1, HEADS * DV)
    g_ret = ret_norm[l].reshape(1, HEADS * DV)

    oa_p, gla_p = _la_prompt(p_main, C_GQ, C_GK, C_GV, C_GR, p_low, p_low, w_gate_pad, b_gate, g_gla, False)
    oc_p, ret_p = _la_prompt(p_main, C_RQ, C_RK, C_RV, C_RG, cos_p, sin_p, dec_row, b_gate, g_ret, True)
    pw_bf = pool_w[l].astype(BF16)
    pscale = pool_scale[l].reshape(1, 512)
    ob_p = _pool_prompt(p_main, pw_bf, pscale)
    sgu_g = sgu_norm[l].reshape(1, 512)
    od_p = _sgu_prompt(p_main, sgu_g, sgu_w[l], jnp.pad(sgu_b[l].T, ((0, 0), (0, LANES - 4))))

    ps = p_main[N_P:]
    q_t = _to_tiles_t(ps[:, C_GQ:C_GQ + 256])
    k_t = _to_tiles_t(ps[:, C_GK:C_GK + 256])
    glow_t = jnp.pad(_to_tiles_t(p_low[N_P:, :16]), ((0, 0), (0, LANES - 16), (0, 0)))
    w_gate_t = jnp.pad(w_gla_gate[l].T, ((0, 0), (0, LANES - 16)))
    b_col = jnp.broadcast_to(b_gla_gate[l][:, None], (HEADS * DK, LANES))
    logit_t = _gate_logits_t(w_gate_t, glow_t, b_col)
    dummy = jnp.zeros((HEADS * DK, LANES), F32)
    oa_s, gla_s = _la_sample(q_t, k_t, logit_t, dummy, dummy, p_main, C_GV, C_GR, g_gla, s_gla, l, prev_gla, False)
    cos_s, sin_s = _rope_tables(jnp.full((1,), PAST_LEN))
    cos_c = jnp.broadcast_to(jnp.tile(cos_s[0], HEADS)[:, None], (HEADS * DK, LANES))
    sin_c = jnp.broadcast_to(jnp.tile(sin_s[0], HEADS)[:, None], (HEADS * DK, LANES))
    dec_c = jnp.broadcast_to(jnp.repeat(log_gamma, DK)[:, None], (HEADS * DK, LANES))
    rq_t = _to_tiles_t(ps[:, C_RQ:C_RQ + 256])
    rk_t = _to_tiles_t(ps[:, C_RK:C_RK + 256])
    oc_s, ret_s = _la_sample(rq_t, rk_t, dec_c, cos_c, sin_c, p_main, C_RV, C_RG, g_ret, s_ret, l, prev_ret, True)
    sgu_w0 = jnp.repeat(sgu_w[l][:, 0, 0], LANES).reshape(1, 512)
    sgu_b0 = jnp.repeat(sgu_b[l][:, 0], LANES).reshape(1, 512)
    ob_s, od_s, vn_s = _small_sample(p_main, s_pool[l], pw_bf, pscale, sgu_g, sgu_w0, sgu_b0)
    pool_p = jnp.stack([p_main[(b + 1) * T_P - POOL_BUF:(b + 1) * T_P, C_PIN:C_PIN + 512] for b in range(B_P)])
    pool_s = jnp.concatenate([s_pool[l][:, 1:], ps[:, None, C_PIN:C_PIN + 512]], axis=1)

    branches = _put_sample_rows([oa_s, ob_s, oc_s, od_s], [oa_p, ob_p, oc_p, od_p])
    merged = _merge(h, branches, prep["w_mg"], b_merge_gate.reshape(DEPTH, 1, 4 * D), prep["w_br"], l)
    x = _outproj(merged, prep["w_out"], x, norm_mix_post[l], mod, l)

    rb = jnp.broadcast_to(router_bias[l][:, None], (N_EXPERTS, SUB))
    h2, logits, h2_packed = _ffn_pre(x, norm_ffn_pre[l], mod, l, prep["rw"])
    eid, pos, wt, counts = _router(logits, rb)
    counts = counts[:, 0].astype(jnp.int32)
    padded = (counts + EXP_BLOCK - 1) // EXP_BLOCK * EXP_BLOCK
    pad_end = jnp.cumsum(padded)
    pad_start = pad_end - padded
    nused = (pad_end[-1] // EXP_BLOCK).astype(jnp.int32).reshape(1)
    blk_row = jnp.arange(N_BLOCKS, dtype=jnp.int32) * EXP_BLOCK
    block_e = jnp.minimum(jnp.sum((blk_row[:, None] >= pad_end[None, :]).astype(jnp.int32), axis=1),
                          N_EXPERTS - 1)
    first = jnp.concatenate([jnp.ones((1,), jnp.int32), (block_e[1:] != block_e[:-1]).astype(jnp.int32)])
    first = jnp.where(blk_row < pad_end[-1], first, 0)
    par = (jnp.cumsum(first) - 1) % 2
    live = jnp.where(padded > 0, jnp.arange(N_EXPERTS), N_EXPERTS)
    after = jnp.concatenate([lax.cummin(live, reverse=True)[1:], jnp.full((1,), N_EXPERTS)])
    of_block = block_e[:, None] == jnp.arange(N_EXPERTS)
    next_e = jnp.sum(jnp.where(of_block, jnp.where(after < N_EXPERTS, after, -1), 0), axis=1).astype(jnp.int32)
    next_blk = jnp.sum(jnp.where(of_block, pad_end // EXP_BLOCK, 0), axis=1).astype(jnp.int32)
    slots = _slots(pad_start.astype(jnp.int32), eid, pos).T.reshape(N_ASSIGN)
    wt = jnp.pad(wt.T, ((0, 0), (0, LANES - TOP_K)))
    table = h2_packed.reshape((N + TM_FFN,) + ROW_TILE)
    ctl = (block_e, first, par.astype(jnp.int32), next_e, next_blk, nused)
    xs_parts = [_sc_dispatch(table, slots, blk0, n_blk) for blk0, n_blk in PARTS]
    shared = _shared(h2, prep["w13"], prep["sw2"], l, slots)
    ys = None
    for xs_part, (blk0, n_blk) in zip(xs_parts, PARTS):
        ys = _experts(ctl, xs_part, expert_w1, expert_w3, expert_w2, l, blk0, n_blk, ys)
    ys = ys.reshape((L_SLOTS,) + ROW_TILE)
    outs = None
    for tile0, n_tiles in COMBINE_RANGES:
        a0, a1 = tile0 * SUB * TOP_K, (tile0 + n_tiles) * SUB * TOP_K
        outs = _combine(_sc_gather(ys, slots[a0:a1]), wt, shared, x, norm_ffn_post[l], mod, l, tile0, n_tiles,
                        outs, final, None if final else norm_mix_pre[l + 1])
    x, h_next = (tuple(outs), None) if final else outs
    return x, (gla_p, pool_p, pool_s, ret_p, vn_s), gla_s, ret_s, h_next


def kernel(x_prompt, x_sample, c_prompt, c_sample, state_gla, state_pool, state_ret, w_ada, b_ada, norm_mix_pre, norm_mix_post, norm_ffn_pre, norm_ffn_post, w_in, w_gla_gate, b_gla_gate, gla_norm, pool_w, pool_scale, ret_norm, sgu_norm, sgu_w, sgu_b, w_branch, w_merge_gate, b_merge_gate, w_out, router_w, router_bias, expert_w1, expert_w3, expert_w2, shared_w1, shared_w3, shared_w2):
    wts = (norm_mix_pre, norm_mix_post, norm_ffn_pre, norm_ffn_post, w_in, w_gla_gate, b_gla_gate, gla_norm,
           pool_w, pool_scale, ret_norm, sgu_norm, sgu_w, sgu_b, w_branch, w_merge_gate, b_merge_gate, w_out,
           router_w, router_bias, expert_w1, expert_w3, expert_w2, shared_w1, shared_w3, shared_w2)
    c_all = jnp.zeros((MOD_ROWS, D), F32).at[:B_P].set(c_prompt).at[SUB:SUB + N_S].set(c_sample)
    mod = _ada(c_all, w_ada, b_ada)
    x = (x_prompt.reshape(N_P, D), x_sample.reshape(N_S, D))
    prep = _prepare_weights(w_in, w_merge_gate, w_branch, w_out, router_w, shared_w1, shared_w3, shared_w2)
    per_layer = []
    gla_s = ret_s = h = None
    for l in range(DEPTH):
        x, states, gla_s, ret_s, h = _layer(l, x, mod, state_gla, state_pool, state_ret, wts, prep, l == DEPTH - 1,
                                            gla_s, ret_s, h)
        per_layer.append(states)
    gla_p, pool_p, pool_s, ret_p, vn_s = (jnp.stack(z) for z in zip(*per_layer))
    return (x[0].reshape(B_P, T_P, D), x[1].reshape(N_S, 1, D),
            gla_p, gla_s, pool_p, pool_s, ret_p, ret_s, vn_s.reshape(DEPTH, N_S, 1, 512))
```

```python
import functools

import jax
import jax.numpy as jnp
from jax import lax
from jax.experimental import pallas as pl
from jax.experimental.pallas import tpu as pltpu
from jax.experimental.pallas import tpu_sc as plsc

F32 = jnp.float32
BF16 = jnp.bfloat16
HIGHEST = lax.Precision.HIGHEST

D = 2048
B_P, T_P = 4, 2048
N_P = B_P * T_P
N_S = 128
N = N_P + N_S
DEPTH = 2
PAST_LEN = 16384
EPS = 1e-6
HEADS, DK, DV = 4, 64, 128
CHUNK = 64
GATE_TEMP = 16.0
POOL_WINDOWS = (2, 4, 8, 16)
POOL_BUF = 15
ROPE_BASE = 10000.0
N_EXPERTS = 64
TOP_K = 8
D_EXPERT = 512
ROUTED_SCALE = 2.5

LANES = 128
SUB = 128
MOD_ROWS = 256
EXP_BLOCK = 256
N_ASSIGN = N * TOP_K
N_BLOCKS = -(-(N_ASSIGN + N_EXPERTS * (EXP_BLOCK - 1)) // EXP_BLOCK)
L_SLOTS = N_BLOCKS * EXP_BLOCK
VMEM_LIMIT = 56 * 1024 * 1024

C_GQ, C_GK, C_GV, C_GR, C_PIN, C_RQ, C_RK, C_RV, C_RG, C_SU, C_SV = (
    0, 256, 512, 1024, 1536, 2048, 2304, 2560, 3072, 3584, 4096)
P_MAIN = 4608


def _cparams(n_axes=1):
    return pltpu.CompilerParams(dimension_semantics=("arbitrary",) * n_axes,
                                vmem_limit_bytes=VMEM_LIMIT)


def _silu(x):
    return x * jax.nn.sigmoid(x)


def _mod_rows(t, mp_ref, ms_ref):
    b = jnp.minimum(t // (T_P // SUB), B_P - 1)
    return jnp.where(t >= N_P // SUB, ms_ref[...], mp_ref[pl.ds(b, 1), :])


def _mod_specs(layer, part):
    return [pl.BlockSpec((None, 8, D), lambda i, l=layer, p=part: (l, 0, p)),
            pl.BlockSpec((None, SUB, D), lambda i, l=layer, p=part: (l, 1, p))]


def _pack_bf16_pair(lo, hi):
    lo_u = lax.bitcast_convert_type(lo.astype(BF16).astype(F32), jnp.uint32)
    hi_u = lax.bitcast_convert_type(hi.astype(BF16).astype(F32), jnp.uint32)
    return lax.bitcast_convert_type((hi_u & jnp.uint32(0xFFFF0000)) | (lo_u >> 16), jnp.int32)


def _unpack_bf16_pair(w):
    u = lax.bitcast_convert_type(w, jnp.uint32)
    lo = lax.bitcast_convert_type(u << 16, F32)
    hi = lax.bitcast_convert_type(u & jnp.uint32(0xFFFF0000), F32)
    return lo, hi


ROW_TILE = (8, LANES)


def _load_row_tiles(ref):
    return jnp.concatenate([ref[:, c, :] for c in range(ROW_TILE[0])], axis=1)


def _store_row_tiles(ref, val):
    for c in range(ROW_TILE[0]):
        ref[:, c, :] = val[:, c * LANES:(c + 1) * LANES]


def _load_row_tiles_2d(ref, rows):
    return jnp.concatenate([ref[pl.ds(c, rows, stride=ROW_TILE[0]), :] for c in range(ROW_TILE[0])], axis=1)


def _store_row_tiles_2d(ref, val, rows):
    for c in range(ROW_TILE[0]):
        ref[pl.ds(c, rows, stride=ROW_TILE[0]), :] = val[:, c * LANES:(c + 1) * LANES]


def _ada_body(c_ref, w_ref, b_ref, o_ref):
    s = _silu(c_ref[...]).astype(BF16)
    o_ref[...] = jnp.dot(s, w_ref[...].astype(BF16), preferred_element_type=F32) + b_ref[...]


def _ada(c_all, w_ada, b_ada):
    tn = 2048
    return pl.pallas_call(
        _ada_body,
        grid=(DEPTH, 6 * D // tn),
        in_specs=[pl.BlockSpec((MOD_ROWS, D), lambda l, j: (0, 0)),
                  pl.BlockSpec((None, D, tn), lambda l, j: (l, 0, j)),
                  pl.BlockSpec((None, 1, tn), lambda l, j: (l, 0, j))],
        out_specs=pl.BlockSpec((None, MOD_ROWS, tn), lambda l, j: (l, 0, j)),
        out_shape=jax.ShapeDtypeStruct((DEPTH, MOD_ROWS, 6 * D), F32),
        compiler_params=_cparams(2), name="ada")(c_all, w_ada, b_ada.reshape(DEPTH, 1, 6 * D))


ZERO_ROWS = 2 * SUB


def _x_specs(x, tm, n_axes):
    row = (lambda i: (i, 0)) if n_axes == 1 else (lambda i, j: (i, 0))
    if not isinstance(x, tuple):
        return [pl.BlockSpec((tm, D), row)], [x]
    zero = (lambda i: (0, 0)) if n_axes == 1 else (lambda i, j: (0, 0))
    return [pl.BlockSpec((tm, D), row), pl.BlockSpec((N_S, D), zero)], list(x)


def _x_rows(x_refs, t, rows):
    if len(x_refs) == 1:
        return x_refs[0][rows, :]
    return jnp.where(t >= N_P // SUB, x_refs[1][...], x_refs[0][rows, :])


def _prenorm_rows(t, x, g_ref, shp_ref, shs_ref, scp_ref, scs_ref):
    y = x * lax.rsqrt(jnp.mean(x * x, axis=-1, keepdims=True) + EPS) * g_ref[...]
    return y * (1.0 + _mod_rows(t, scp_ref, scs_ref)) + _mod_rows(t, shp_ref, shs_ref)


TM_NORM = 640


def _prenorm_body(*refs):
    g_ref, shp_ref, shs_ref, scp_ref, scs_ref, h_ref = refs[-6:]
    for sidx in range(TM_NORM // SUB):
        rows = pl.ds(sidx * SUB, SUB)
        t = pl.program_id(0) * (TM_NORM // SUB) + sidx
        h_ref[rows, :] = _prenorm_rows(t, _x_rows(refs[:-6], t, rows), g_ref, shp_ref, shs_ref, scp_ref,
                                       scs_ref).astype(BF16)


def _prenorm(x, g, mod, layer):
    x_specs, x_args = _x_specs(x, TM_NORM, 1)
    return pl.pallas_call(
        _prenorm_body,
        grid=(N // TM_NORM,),
        in_specs=x_specs + [pl.BlockSpec((1, D), lambda i: (0, 0))] + _mod_specs(layer, 0) + _mod_specs(layer, 1),
        out_specs=pl.BlockSpec((TM_NORM, D), lambda i: (i, 0)),
        out_shape=jax.ShapeDtypeStruct((N, D), BF16),
        compiler_params=_cparams(1), name="prenorm")(*x_args, g.reshape(1, D), mod, mod, mod, mod)


def _mm_body(x_ref, w_ref, o_ref):
    o_ref[...] = jnp.dot(x_ref[...], w_ref[...], preferred_element_type=F32).astype(o_ref.dtype)


def _matmul(x, w_all, layer, tm, tn, out_dtype=F32, name="mm"):
    m, k = x.shape
    n = w_all.shape[2]
    return pl.pallas_call(
        _mm_body,
        grid=(m // tm, n // tn),
        in_specs=[pl.BlockSpec((tm, k), lambda i, j: (i, 0)),
                  pl.BlockSpec((None, k, tn), lambda i, j, l=layer: (l, 0, j))],
        out_specs=pl.BlockSpec((tm, tn), lambda i, j: (i, j)),
        out_shape=jax.ShapeDtypeStruct((m, n), out_dtype),
        compiler_params=_cparams(2), name=name)(x, w_all)


ROWS_LA = 256


def _swap_halves_lanes(x):
    lane = lax.broadcasted_iota(jnp.int32, x.shape, 1)
    return jnp.where((lane % 64) < 32, pltpu.roll(x, 96, 1), pltpu.roll(x, 32, 1))


def _rope_lanes(x, cos, sin_signed):
    parts = []
    for half in range(2):
        xh = x[:, half * LANES:(half + 1) * LANES]
        parts.append(xh * cos + _swap_halves_lanes(xh) * sin_signed)
    return jnp.concatenate(parts, axis=1)


def _la_prompt_body(q_ref, k_ref, v_ref, r_ref, aux_ref, aux2_ref, dec_ref, bias_ref, g_ref,
                    o_ref, st_out_ref, st_ref, *, retention):
    t = pl.program_id(1)

    @pl.when(t == 0)
    def _():
        st_ref[...] = jnp.zeros_like(st_ref)

    n_ch = ROWS_LA // CHUNK
    ri = lax.broadcasted_iota(jnp.int32, (ROWS_LA, ROWS_LA), 0)
    ci = lax.broadcasted_iota(jnp.int32, (ROWS_LA, ROWS_LA), 1)
    causal = (ri >= ci) & ((ri // CHUNK) == (ci // CHUNK))
    scale = DK ** -0.5

    q = q_ref[...]
    k = k_ref[...]
    v = v_ref[...]
    if retention:
        cos = aux_ref[...]
        sin = aux2_ref[...]
        q = _rope_lanes(q, cos, sin)
        k = _rope_lanes(k, cos, sin) * scale
        step = (lax.broadcasted_iota(jnp.int32, (ROWS_LA, HEADS * DK), 0) % CHUNK + 1).astype(F32)
        bc = step * dec_ref[...]
    else:
        q = q * scale
        logit = jnp.dot(aux_ref[...], dec_ref[...], precision=HIGHEST, preferred_element_type=F32) + bias_ref[...]
        la = jax.nn.log_sigmoid(logit) / GATE_TEMP
        bc = jnp.dot(causal.astype(F32), la, precision=HIGHEST, preferred_element_type=F32)
    bl = bc.reshape(n_ch, CHUNK, HEADS * DK)[:, CHUNK - 1:CHUNK, :]
    bl_rows = jnp.broadcast_to(bl, (n_ch, CHUNK, HEADS * DK)).reshape(ROWS_LA, HEADS * DK)
    qd = q * jnp.exp(bc)
    ki = k * jnp.exp(-bc)
    ke = k * jnp.exp(bl_rows - bc)
    ac = jnp.exp(bl)
    outs = []
    for h in range(HEADS):
        ks = slice(h * DK, (h + 1) * DK)
        vs = slice(h * DV, (h + 1) * DV)
        qd_h = qd[:, ks].astype(BF16)
        ki_h = ki[:, ks].astype(BF16)
        ke_h = ke[:, ks].astype(BF16)
        v_h = v[:, vs].astype(BF16)
        sc = lax.dot_general(qd_h, ki_h, (((1,), (1,)), ((), ())), preferred_element_type=F32)
        sc = jnp.where(causal, sc, 0.0)
        o_h = jnp.dot(sc.astype(BF16), v_h, preferred_element_type=F32)
        inter = []
        for c in range(n_ch):
            rows = slice(c * CHUNK, (c + 1) * CHUNK)
            st = st_ref[h]
            inter.append(lax.dot_general(qd_h[rows], st.astype(BF16), (((1,), (1,)), ((), ())),
                                         preferred_element_type=F32))
            kv_t = lax.dot_general(v_h[rows], ke_h[rows], (((0,), (0,)), ((), ())), preferred_element_type=F32)
            st_ref[h] = st * ac[c][:, ks] + kv_t
        o_h = o_h + jnp.concatenate(inter, axis=0)
        outs.append(o_h * lax.rsqrt(jnp.mean(o_h * o_h, axis=-1, keepdims=True) + EPS) * g_ref[:, vs])
    o_ref[...] = (jnp.concatenate(outs, axis=1) * _silu(r_ref[...])).astype(BF16)
    st_out_ref[...] = st_ref[...]


def _la_prompt(p_main, cq, ck, cv, cr, aux, aux2, dec, bias, g, retention):
    nt = T_P // ROWS_LA
    rowblk = lambda b, t: b * nt + t
    if retention:
        aux_specs = [pl.BlockSpec((ROWS_LA, LANES), lambda b, t: (t, 0)),
                     pl.BlockSpec((ROWS_LA, LANES), lambda b, t: (t, 0))]
    else:
        aux_specs = [pl.BlockSpec((ROWS_LA, LANES), lambda b, t: (rowblk(b, t), 0)),
                     pl.BlockSpec((8, LANES), lambda b, t: (0, 0))]
    o, st = pl.pallas_call(
        functools.partial(_la_prompt_body, retention=retention),
        grid=(B_P, nt),
        in_specs=[pl.BlockSpec((ROWS_LA, 256), lambda b, t: (rowblk(b, t), cq // 256)),
                  pl.BlockSpec((ROWS_LA, 256), lambda b, t: (rowblk(b, t), ck // 256)),
                  pl.BlockSpec((ROWS_LA, 512), lambda b, t: (rowblk(b, t), cv // 512)),
                  pl.BlockSpec((ROWS_LA, 512), lambda b, t: (rowblk(b, t), cr // 512))]
        + aux_specs
        + [pl.BlockSpec(dec.shape, lambda b, t: (0, 0)),
           pl.BlockSpec((1, HEADS * DK), lambda b, t: (0, 0)),
           pl.BlockSpec((1, HEADS * DV), lambda b, t: (0, 0))],
        out_specs=[pl.BlockSpec((ROWS_LA, HEADS * DV), lambda b, t: (rowblk(b, t), 0)),
                   pl.BlockSpec((None, HEADS, DV, DK), lambda b, t: (b, 0, 0, 0))],
        out_shape=[jax.ShapeDtypeStruct((N, HEADS * DV), BF16),
                   jax.ShapeDtypeStruct((B_P, HEADS, DV, DK), F32)],
        scratch_shapes=[pltpu.VMEM((HEADS, DV, DK), F32)],
        compiler_params=_cparams(2), name="ret_prompt" if retention else "gla_prompt",
    )(p_main, p_main, p_main, p_main, aux, aux2, dec, bias, g)
    return o, jnp.swapaxes(st, -1, -2)


SAMPLE_TILE = 8


def _la_sample_body(qt_ref, kt_ref, lt_ref, cos_ref, sin_ref, v_ref, r_ref, g_ref, s_ref, *rest, retention):
    o_ref, s_out_ref = rest[-2:]
    scale = DK ** -0.5
    qt = qt_ref[...]
    kt = kt_ref[...]
    if retention:
        def rope(x):
            sw = jnp.concatenate(
                [x[h * DK + (DK // 2) * (1 - j): h * DK + (DK // 2) * (2 - j), :]
                 for h in range(HEADS) for j in range(2)], axis=0)
            return x * cos_ref[...] + sw * sin_ref[...]
        qt = rope(qt)
        kt = rope(kt) * scale
        la = lt_ref[...]
    else:
        qt = qt * scale
        la = jax.nn.log_sigmoid(lt_ref[...]) / GATE_TEMP
    at = jnp.exp(la)
    qd = qt * at
    ki = kt * jnp.exp(-la)
    prod = qd * ki
    v8 = v_ref[...]
    r8 = r_ref[...]
    g = g_ref[...]
    for j in range(SAMPLE_TILE):
        for h in range(HEADS):
            ks = slice(h * DK, (h + 1) * DK)
            vs = slice(h * DV, (h + 1) * DV)
            a_c = jnp.broadcast_to(at[ks, j:j + 1], (DK, DV))
            k_c = jnp.broadcast_to(kt[ks, j:j + 1], (DK, DV))
            q_c = jnp.broadcast_to(qd[ks, j:j + 1], (DK, DV))
            s_c = jnp.broadcast_to(jnp.sum(prod[ks, j:j + 1], axis=0, keepdims=True), (1, DV))
            s0 = s_ref[j, h]
            v_row = v8[j:j + 1, vs]
            s_out_ref[j, h] = a_c * s0 + k_c * v_row
            o_row = s_c * v_row + jnp.sum(q_c * s0, axis=0, keepdims=True)
            o_n = o_row * lax.rsqrt(jnp.mean(o_row * o_row, axis=-1, keepdims=True) + EPS) * g[:, vs]
            o_ref[j:j + 1, vs] = o_n * _silu(r8[j:j + 1, vs])


def _la_sample(qt, kt, lt, cos_t, sin_t, p_main, cv, cr, g, s0_all, layer, s_prev, retention):
    nt = N_S // SAMPLE_TILE
    row0 = N_P // SAMPLE_TILE
    tile = pl.BlockSpec((None, HEADS * DK, LANES), lambda i: (i, 0, 0))
    full = pl.BlockSpec((HEADS * DK, LANES), lambda i: (0, 0))
    lt_spec = full if retention else tile
    return pl.pallas_call(
        functools.partial(_la_sample_body, retention=retention),
        grid=(nt,),
        in_specs=[tile, tile, lt_spec, full, full,
                  pl.BlockSpec((SAMPLE_TILE, 512), lambda i: (row0 + i, cv // 512)),
                  pl.BlockSpec((SAMPLE_TILE, 512), lambda i: (row0 + i, cr // 512)),
                  pl.BlockSpec((1, HEADS * DV), lambda i: (0, 0)),
                  pl.BlockSpec((None, SAMPLE_TILE, HEADS, DK, DV), lambda i, l=layer: (l, i, 0, 0, 0))]
        + ([] if s_prev is None else [pl.BlockSpec(memory_space=pl.ANY)]),
        out_specs=[pl.BlockSpec((SAMPLE_TILE, HEADS * DV), lambda i: (i, 0)),
                   pl.BlockSpec((None, SAMPLE_TILE, HEADS, DK, DV), lambda i, l=layer: (l, i, 0, 0, 0))],
        out_shape=[jax.ShapeDtypeStruct((N_S, HEADS * DV), F32),
                   jax.ShapeDtypeStruct((DEPTH, N_S, HEADS, DK, DV), F32)],
        input_output_aliases={} if s_prev is None else {9: 1},
        compiler_params=_cparams(1), name="ret_sample" if retention else "gla_sample",
    )(qt, kt, lt, cos_t, sin_t, p_main, p_main, g, s0_all, *([] if s_prev is None else [s_prev]))


def _gate_logits_t_body(w_ref, x_ref, b_ref, o_ref):
    o_ref[...] = jnp.dot(w_ref[...], x_ref[...], precision=HIGHEST, preferred_element_type=F32) + b_ref[...]


def _gate_logits_t(w_gate_t, glow_t, b_col):
    nt = N_S // SAMPLE_TILE
    return pl.pallas_call(
        _gate_logits_t_body,
        grid=(nt,),
        in_specs=[pl.BlockSpec((HEADS * DK, LANES), lambda i: (0, 0)),
                  pl.BlockSpec((None, LANES, LANES), lambda i: (i, 0, 0)),
                  pl.BlockSpec((HEADS * DK, LANES), lambda i: (0, 0))],
        out_specs=pl.BlockSpec((None, HEADS * DK, LANES), lambda i: (i, 0, 0)),
        out_shape=jax.ShapeDtypeStruct((nt, HEADS * DK, LANES), F32),
        compiler_params=_cparams(1), name="gate_logits_t")(w_gate_t, glow_t, b_col)


def _to_tiles_t(x):
    c = x.shape[1]
    xt = jnp.swapaxes(x.reshape(N_S // SAMPLE_TILE, SAMPLE_TILE, c), 1, 2)
    return jnp.pad(xt, ((0, 0), (0, 0), (0, LANES - SAMPLE_TILE)))


ROWS_POOL = 512


def _pool_mix(y, w_ref, sc_ref):
    outs = []
    for gi in range(4):
        cs = slice(gi * LANES, (gi + 1) * LANES)
        outs.append(jnp.dot(y[:, cs].astype(BF16), w_ref[gi], preferred_element_type=F32))
    return jnp.concatenate(outs, axis=1) * sc_ref[...]


def _pool_prompt_body(p_ref, halo_ref, w_ref, sc_ref, o_ref):
    t = pl.program_id(1)
    p = p_ref[...]
    halo = jnp.where(t == 0, 0.0, halo_ref[...])
    full = jnp.concatenate([halo, p], axis=0)
    pos = t * ROWS_POOL + lax.broadcasted_iota(jnp.int32, (ROWS_POOL, LANES), 0)
    means = []
    for gi, w in enumerate(POOL_WINDOWS):
        s = full[:, gi * LANES:(gi + 1) * LANES]
        step = 1
        while step < w:
            s = s + pltpu.roll(s, step, 0)
            step *= 2
        win = s[16:, :]
        cnt = jnp.minimum(w, pos + 1).astype(F32)
        means.append(win / cnt)
    y = jnp.concatenate(means, axis=1) - p
    o_ref[...] = _pool_mix(y, w_ref, sc_ref).astype(BF16)


def _pool_prompt(p_main, w_bf, scale):
    nt = T_P // ROWS_POOL
    return pl.pallas_call(
        _pool_prompt_body,
        grid=(B_P, nt),
        in_specs=[pl.BlockSpec((ROWS_POOL, 512), lambda b, t: (b * nt + t, C_PIN // 512)),
                  pl.BlockSpec((16, 512), lambda b, t: (jnp.maximum((b * nt + t) * (ROWS_POOL // 16) - 1, 0),
                                                        C_PIN // 512)),
                  pl.BlockSpec((4, LANES, LANES), lambda b, t: (0, 0, 0)),
                  pl.BlockSpec((1, 512), lambda b, t: (0, 0))],
        out_specs=pl.BlockSpec((ROWS_POOL, 512), lambda b, t: (b * nt + t, 0)),
        out_shape=jax.ShapeDtypeStruct((N, 512), BF16),
        compiler_params=_cparams(2), name="pool_prompt")(p_main, p_main, w_bf, scale)


def _small_sample_body(p_ref, buf_ref, pw_ref, psc_ref, u_ref, sv_ref, sg_ref, sw_ref, sb_ref,
                       ob_ref, od_ref, vn_ref):
    p = p_ref[...]
    means = []
    for gi, w in enumerate(POOL_WINDOWS):
        cs = slice(gi * LANES, (gi + 1) * LANES)
        s = p[:, cs]
        for j in range(1, w):
            s = s + buf_ref[:, POOL_BUF - j, cs]
        means.append(s / float(min(w, PAST_LEN + 1)))
    y = jnp.concatenate(means, axis=1) - p
    ob_ref[...] = _pool_mix(y, pw_ref, psc_ref)
    sv = sv_ref[...]
    vn = sv * lax.rsqrt(jnp.mean(sv * sv, axis=-1, keepdims=True) + EPS) * sg_ref[...]
    vn_ref[...] = vn
    od_ref[...] = u_ref[...] * (sw_ref[...] * vn + sb_ref[...])


def _small_sample(p_main, buf, pw_bf, pscale, sgu_g, sgu_w0, sgu_b0):
    row = N_P // N_S
    col = lambda c: pl.BlockSpec((N_S, 512), lambda i, c=c: (row, c // 512))
    vec = pl.BlockSpec((1, 512), lambda i: (0, 0))
    return pl.pallas_call(
        _small_sample_body,
        grid=(1,),
        in_specs=[col(C_PIN), pl.BlockSpec((N_S, POOL_BUF, 512), lambda i: (0, 0, 0)),
                  pl.BlockSpec((4, LANES, LANES), lambda i: (0, 0, 0)), vec,
                  col(C_SU), col(C_SV), vec, vec, vec],
        out_specs=[pl.BlockSpec((N_S, 512), lambda i: (0, 0))] * 3,
        out_shape=[jax.ShapeDtypeStruct((N_S, 512), F32)] * 3,
        compiler_params=_cparams(1), name="small_sample",
    )(p_main, buf, pw_bf, pscale, p_main, p_main, sgu_g, sgu_w0, sgu_b0)


ROWS_SGU = 512
SGU_CHUNK = 128


def _sgu_prompt_body(u_ref, v_ref, g_ref, w_ref, bt_ref, o_ref):
    ri = lax.broadcasted_iota(jnp.int32, (SGU_CHUNK, SGU_CHUNK), 0)
    ci = lax.broadcasted_iota(jnp.int32, (SGU_CHUNK, SGU_CHUNK), 1)
    causal = ri >= ci
    for c in range(ROWS_SGU // SGU_CHUNK):
        rows = pl.ds(c * SGU_CHUNK, SGU_CHUNK)
        v = v_ref[rows, :]
        vn = (v * lax.rsqrt(jnp.mean(v * v, axis=-1, keepdims=True) + EPS) * g_ref[...]).astype(BF16)
        outs = []
        for gi in range(4):
            cs = slice(gi * LANES, (gi + 1) * LANES)
            w = jnp.where(causal, w_ref[gi], 0.0).astype(BF16)
            mixed = jnp.dot(w, vn[:, cs], preferred_element_type=F32)
            outs.append(mixed + jnp.broadcast_to(bt_ref[:, gi:gi + 1], (SGU_CHUNK, LANES)))
        o_ref[rows, :] = (u_ref[rows, :] * jnp.concatenate(outs, axis=1)).astype(BF16)


def _sgu_prompt(p_main, g, w, b_t):
    return pl.pallas_call(
        _sgu_prompt_body,
        grid=(N_P // ROWS_SGU,),
        in_specs=[pl.BlockSpec((ROWS_SGU, 512), lambda i: (i, C_SU // 512)),
                  pl.BlockSpec((ROWS_SGU, 512), lambda i: (i, C_SV // 512)),
                  pl.BlockSpec((1, 512), lambda i: (0, 0)),
                  pl.BlockSpec((4, SGU_CHUNK, SGU_CHUNK), lambda i: (0, 0, 0)),
                  pl.BlockSpec((SGU_CHUNK, LANES), lambda i: (0, 0))],
        out_specs=pl.BlockSpec((ROWS_SGU, 512), lambda i: (i, 0)),
        out_shape=jax.ShapeDtypeStruct((N, 512), BF16),
        compiler_params=_cparams(1), name="sgu_prompt")(p_main, p_main, g, w, b_t)


TM_MERGE = 640
TN_MERGE = 512


def _merge_body(h_ref, ba_ref, bb_ref, bc_ref, bd_ref, g0, g1, g2, g3, u0, u1, u2, u3,
                c0, c1, c2, c3, o_ref):
    h = h_ref[...]
    acc = None
    for br, gw, uw, gb in ((ba_ref, g0, u0, c0), (bb_ref, g1, u1, c1), (bc_ref, g2, u2, c2), (bd_ref, g3, u3, c3)):
        gate = jax.nn.sigmoid(jnp.dot(h, gw[...], preferred_element_type=F32) + gb[...])
        up = jnp.dot(br[...], uw[...], preferred_element_type=F32)
        acc = gate * up if acc is None else acc + gate * up
    o_ref[...] = acc.astype(BF16)


def _merge(h, branches, w_mg, b_mg, w_br, layer):
    nj = D // TN_MERGE
    row = lambda w: pl.BlockSpec((TM_MERGE, w), lambda i, j: (i, 0))
    gate_w = [pl.BlockSpec((None, D, TN_MERGE), lambda i, j, b=b, l=layer: (l, 0, b * nj + j)) for b in range(4)]
    up_w = [pl.BlockSpec((None, None, 512, TN_MERGE), lambda i, j, b=b, l=layer: (l, b, 0, j)) for b in range(4)]
    gate_b = [pl.BlockSpec((None, 1, TN_MERGE), lambda i, j, b=b, l=layer: (l, 0, b * nj + j)) for b in range(4)]
    return pl.pallas_call(
        _merge_body,
        grid=(N // TM_MERGE, nj),
        in_specs=[row(D)] + [row(512)] * 4 + gate_w + up_w + gate_b,
        out_specs=pl.BlockSpec((TM_MERGE, TN_MERGE), lambda i, j: (i, j)),
        out_shape=jax.ShapeDtypeStruct((N, D), BF16),
        compiler_params=_cparams(2), name="merge",
    )(h, *branches, w_mg, w_mg, w_mg, w_mg, w_br, w_br, w_br, w_br, b_mg, b_mg, b_mg, b_mg)


TM_OUT = 640
TN_OUT = 1024


def _post_value(t, x, y, gn_ref, gp_ref, gs_ref):
    yn = y * lax.rsqrt(jnp.mean(y * y, axis=-1, keepdims=True) + EPS) * gn_ref[...]
    return x + _mod_rows(t, gp_ref, gs_ref) * yn


def _outproj_body(m_ref, w_ref, *refs):
    gn_ref, gp_ref, gs_ref, o_ref, acc_ref = refs[-5:]
    j = pl.program_id(1)
    acc_ref[j] = jnp.dot(m_ref[...], w_ref[...], preferred_element_type=F32)

    @pl.when(j == D // TN_OUT - 1)
    def _():
        for sidx in range(TM_OUT // SUB):
            rows = pl.ds(sidx * SUB, SUB)
            t = pl.program_id(0) * (TM_OUT // SUB) + sidx
            y = jnp.concatenate([acc_ref[c, rows, :] for c in range(D // TN_OUT)], axis=1)
            o_ref[rows, :] = _post_value(t, _x_rows(refs[:-5], t, rows), y, gn_ref, gp_ref, gs_ref)


def _outproj(merged, w_out, x, g_post, mod, layer):
    mspec = [pl.BlockSpec((None, 8, D), lambda i, j, l=layer: (l, 0, 2)),
             pl.BlockSpec((None, SUB, D), lambda i, j, l=layer: (l, 1, 2))]
    x_specs, x_args = _x_specs(x, TM_OUT, 2)
    return pl.pallas_call(
        _outproj_body,
        grid=(N // TM_OUT, D // TN_OUT),
        in_specs=[pl.BlockSpec((TM_OUT, D), lambda i, j: (i, 0)),
                  pl.BlockSpec((None, D, TN_OUT), lambda i, j, l=layer: (l, 0, j))] + x_specs
        + [pl.BlockSpec((1, D), lambda i, j: (0, 0))] + mspec,
        out_specs=pl.BlockSpec((TM_OUT, D), lambda i, j: (i, 0)),
        out_shape=jax.ShapeDtypeStruct((N, D), F32),
        scratch_shapes=[pltpu.VMEM((D // TN_OUT, TM_OUT, TN_OUT), F32)],
        compiler_params=_cparams(2), name="outproj",
    )(merged, w_out, *x_args, g_post.reshape(1, D), mod, mod)


def _router_body(lg_ref, b_ref, eid_ref, pos_ref, wt_ref, cnt_ref, run_ref):
    i = pl.program_id(0)

    @pl.when(i == 0)
    def _():
        run_ref[...] = jnp.zeros_like(run_ref)

    ng, gs = 8, N_EXPERTS // 8
    neg = -jnp.inf
    scores = jax.nn.sigmoid(lg_ref[...].T[:N_EXPERTS, :])
    sel = scores + b_ref[...]
    sel3 = sel.reshape(ng, gs, SUB)
    sub3 = lax.broadcasted_iota(jnp.int32, (ng, gs, SUB), 1)
    gmax = jnp.max(sel3, axis=1, keepdims=True)
    first = jnp.min(jnp.where(sel3 == gmax, sub3, gs), axis=1, keepdims=True)
    gmax2 = jnp.max(jnp.where(sub3 == first, neg, sel3), axis=1, keepdims=True)
    gscore = (gmax + gmax2).reshape(ng, SUB)
    gidx = lax.broadcasted_iota(jnp.int32, (ng, SUB), 0)
    grank = jnp.zeros((ng, SUB), jnp.int32)
    for s in range(1, ng):
        other = pltpu.roll(gscore, s, 0)
        lower = gidx >= s
        grank += ((other > gscore) | ((other == gscore) & lower)).astype(jnp.int32)
    keep = jnp.broadcast_to((grank < 4).reshape(ng, 1, SUB), (ng, gs, SUB))
    masked = jnp.where(keep, sel3, neg).reshape(N_EXPERTS, SUB)
    eidx = lax.broadcasted_iota(jnp.int32, (N_EXPERTS, SUB), 0)
    rank = jnp.zeros((N_EXPERTS, SUB), jnp.int32)
    for s in range(1, N_EXPERTS):
        other = pltpu.roll(masked, s, 0)
        lower = eidx >= s
        rank += ((other > masked) | ((other == masked) & lower)).astype(jnp.int32)
    chosen = rank < TOP_K
    w_sel = jnp.where(chosen, scores, 0.0)
    w_sel = w_sel / jnp.sum(w_sel, axis=0, keepdims=True) * ROUTED_SCALE
    ri = lax.broadcasted_iota(jnp.int32, (SUB, SUB), 0)
    ci = lax.broadcasted_iota(jnp.int32, (SUB, SUB), 1)
    onehot = chosen.astype(BF16)
    pos = jnp.dot(onehot, (ri < ci).astype(BF16), preferred_element_type=F32) + run_ref[...]
    run_ref[...] = run_ref[...] + jnp.sum(chosen.astype(F32), axis=1, keepdims=True)
    cnt_ref[...] = run_ref[...]
    eidx_f = eidx.astype(F32)
    rows_e, rows_p, rows_w = [], [], []
    for kk in range(TOP_K):
        m = chosen & (rank == kk)
        rows_e.append(jnp.sum(jnp.where(m, eidx_f, 0.0), axis=0, keepdims=True))
        rows_p.append(jnp.sum(jnp.where(m, pos, 0.0), axis=0, keepdims=True))
        rows_w.append(jnp.sum(jnp.where(m, w_sel, 0.0), axis=0, keepdims=True))
    eid_ref[...] = jnp.concatenate(rows_e, axis=0).astype(jnp.int32)
    pos_ref[...] = jnp.concatenate(rows_p, axis=0).astype(jnp.int32)
    wt_ref[...] = jnp.concatenate(rows_w, axis=0)


def _router(logits, rb_col):
    tile = pl.BlockSpec((TOP_K, SUB), lambda i: (0, i))
    return pl.pallas_call(
        _router_body,
        grid=(N // SUB,),
        in_specs=[pl.BlockSpec((SUB, LANES), lambda i: (i, 0)),
                  pl.BlockSpec((N_EXPERTS, SUB), lambda i: (0, 0))],
        out_specs=[tile, tile, tile, pl.BlockSpec((N_EXPERTS, SUB), lambda i: (0, 0))],
        out_shape=[jax.ShapeDtypeStruct((TOP_K, N), jnp.int32), jax.ShapeDtypeStruct((TOP_K, N), jnp.int32),
                   jax.ShapeDtypeStruct((TOP_K, N), F32), jax.ShapeDtypeStruct((N_EXPERTS, SUB), F32)],
        scratch_shapes=[pltpu.VMEM((N_EXPERTS, SUB), F32)],
        compiler_params=_cparams(1), name="router")(logits, rb_col)


SC_CORES, SC_SUBCORES = 2, 16
SC_WORKERS = SC_CORES * SC_SUBCORES
SC_LANES = 16
SC_SCAN = N_ASSIGN // SC_WORKERS


def _sc_mesh():
    return plsc.VectorSubcoreMesh(core_axis_name="c", subcore_axis_name="s",
                                  num_cores=SC_CORES, num_subcores=SC_SUBCORES)


def _sc_worker_base(per_w):
    return (lax.axis_index("s") * SC_CORES + lax.axis_index("c")) * per_w


def _sc_chunk(per_w):
    return max(c for c in (8, 16, 24, 32) if per_w % c == 0)


def _sc_gather_rows(table_hbm, out_hbm, idx_v, rows_v, gsem, wsem, base, per_w):
    chunk = _sc_chunk(per_w)
    n_ch = per_w // chunk

    def gather(j, p):
        off = pl.multiple_of(j * chunk, 8)
        return pltpu.make_async_copy(table_hbm.at[idx_v.at[pl.ds(off, chunk)]], rows_v.at[p], gsem.at[p])

    def write(j, p):
        off = pl.multiple_of(j * chunk, 8)
        return pltpu.make_async_copy(rows_v.at[p], out_hbm.at[pl.ds(base + off, chunk)], wsem.at[p])

    def when(cond, fn):
        if isinstance(cond, bool):
            if cond:
                fn()
        else:
            pl.when(cond)(fn)

    def step(j, p):
        gather(j, p).wait()
        when(j >= 1, lambda: write(j - 1, 1 - p).wait())
        when(j + 1 < n_ch, lambda: gather(j + 1, 1 - p).start())
        write(j, p).start()

    gather(0, 0).start()

    @pl.loop(0, n_ch - n_ch % 2, step=2)
    def _(j0):
        for p in range(2):
            step(j0 + p, p)

    if n_ch % 2:
        step(n_ch - 1, 0)
    write(n_ch - 1, (n_ch - 1) % 2).wait()


def _sc_row_scratch(per_w):
    return [pltpu.VMEM((2, _sc_chunk(per_w)) + ROW_TILE, jnp.int32),
            pltpu.SemaphoreType.DMA((2,)), pltpu.SemaphoreType.DMA((2,))]


def _sc_gather(table, idx):
    n_out = idx.shape[0]
    per_w = n_out // SC_WORKERS
    assert per_w * SC_WORKERS == n_out and per_w % 8 == 0

    def body(table_hbm, idx_hbm, out_hbm, idx_v, rows_v, gsem, wsem):
        base = _sc_worker_base(per_w)
        pltpu.sync_copy(idx_hbm.at[pl.ds(base, per_w)], idx_v)
        _sc_gather_rows(table_hbm, out_hbm, idx_v, rows_v, gsem, wsem, base, per_w)

    return pl.kernel(
        body, out_type=jax.ShapeDtypeStruct((n_out,) + ROW_TILE, jnp.int32), mesh=_sc_mesh(),
        scratch_types=[pltpu.VMEM((per_w,), jnp.int32)] + _sc_row_scratch(per_w), name="sc_gather")(table, idx)


PARTS = ((0, 54), (54, 108), (162, 162))
assert sum(n for _, n in PARTS) == N_BLOCKS and all(PARTS[i][0] + PARTS[i][1] == PARTS[i + 1][0]
                                                    for i in range(len(PARTS) - 1))


def _sc_dispatch(table, slots, blk0, n_blk):
    part_slots = n_blk * EXP_BLOCK
    per_w = part_slots // SC_WORKERS
    per_w_pad = -(-per_w // SC_LANES) * SC_LANES
    assert per_w * SC_WORKERS == part_slots and per_w % 8 == 0 and SC_SCAN % SC_LANES == 0
    assert ZERO_ROWS & (ZERO_ROWS - 1) == 0

    def body(table_hbm, slots_hbm, out_hbm, idx_v, sl_v, rows_v, gsem, wsem):
        local = _sc_worker_base(per_w)
        base = blk0 * EXP_BLOCK + local
        lane = lax.iota(jnp.int32, SC_LANES)

        @pl.loop(0, per_w_pad // SC_LANES)
        def _(j):
            off = pl.multiple_of(j * SC_LANES, SC_LANES)
            idx_v[pl.ds(off, SC_LANES)] = N + ((base + off + lane) & (ZERO_ROWS - 1))

        @pl.loop(0, N_ASSIGN // SC_SCAN)
        def _(c):
            pltpu.sync_copy(slots_hbm.at[pl.ds(pl.multiple_of(c * SC_SCAN, 8), SC_SCAN)], sl_v)

            @pl.loop(0, SC_SCAN // SC_LANES)
            def _(j):
                off = pl.multiple_of(j * SC_LANES, SC_LANES)
                loc = sl_v[pl.ds(off, SC_LANES)] - base
                mine = (loc >= 0) & (loc < per_w)
                tok = lax.shift_right_logical(c * SC_SCAN + off + lane, 3)
                plsc.store_scatter(idx_v, [jnp.where(mine, loc, 0)], tok, mask=mine)

        _sc_gather_rows(table_hbm, out_hbm, idx_v, rows_v, gsem, wsem, local, per_w)

    return pl.kernel(
        body, out_type=jax.ShapeDtypeStruct((part_slots,) + ROW_TILE, jnp.int32), mesh=_sc_mesh(),
        scratch_types=[pltpu.VMEM((per_w_pad,), jnp.int32), pltpu.VMEM((SC_SCAN,), jnp.int32)]
        + _sc_row_scratch(per_w),
        compiler_params=pltpu.CompilerParams(needs_layout_passes=False),
        name="sc_dispatch")(table, slots)


def _experts_body(be_ref, first_ref, par_ref, next_ref, nextblk_ref, nused_ref, x_ref, w1_hbm, w3_hbm, w2_hbm,
                  *rest, layer, blk0, n_blk):
    y_ref, w1f, w3f, w2f, w1b, w3b, w2b, sem = rest[-8:]
    i = pl.program_id(0)
    b = blk0 + i
    used = b < nused_ref[0]

    def copies(e, slot):
        return (pltpu.make_async_copy(w1_hbm.at[layer, e], w1f.at[slot], sem.at[0, slot]),
                pltpu.make_async_copy(w3_hbm.at[layer, e], w3f.at[slot], sem.at[1, slot]),
                pltpu.make_async_copy(w2_hbm.at[layer, e], w2f.at[slot], sem.at[2, slot]))

    @pl.when(used & (i == 0))
    def _():
        for c in copies(be_ref[b], par_ref[b]):
            c.start()

    @pl.when(used & ((i == 0) | (first_ref[b] == 1)))
    def _():
        slot = par_ref[b]
        for c in copies(be_ref[b], slot):
            c.wait()

        @pl.when((next_ref[b] >= 0) & (nextblk_ref[b] < blk0 + n_blk))
        def _():
            for c in copies(next_ref[b], 1 - slot):
                c.start(priority=1)
        w1b[...] = w1f[slot].astype(BF16)
        w3b[...] = w3f[slot].astype(BF16)
        w2b[...] = w2f[slot].astype(BF16)

    @pl.when(used)
    def _():
        lo, hi = _unpack_bf16_pair(_load_row_tiles_2d(x_ref, EXP_BLOCK))
        lo = lo.astype(BF16)
        hi = hi.astype(BF16)
        half = D // 2
        h1 = (jnp.dot(lo, w1b[:half, :], preferred_element_type=F32)
              + jnp.dot(hi, w1b[half:, :], preferred_element_type=F32))
        h3 = (jnp.dot(lo, w3b[:half, :], preferred_element_type=F32)
              + jnp.dot(hi, w3b[half:, :], preferred_element_type=F32))
        hid = (_silu(h1) * h3).astype(BF16)
        y = jnp.dot(hid, w2b[...], preferred_element_type=F32)
        _store_row_tiles_2d(y_ref, _pack_bf16_pair(y[:, :half], y[:, half:]), EXP_BLOCK)

    @pl.when(jnp.logical_not(used))
    def _():
        y_ref[...] = jnp.zeros_like(y_ref)


def _experts(ctl, xs_part, w1, w3, w2, layer, blk0, n_blk, ys_prev):
    def x_blk(i, *refs):
        n_here = jnp.clip(refs[-1][0] - blk0, 1, n_blk)
        return (jnp.minimum(i, n_here - 1), 0)
    any_spec = pl.BlockSpec(memory_space=pl.ANY)
    in_specs = [pl.BlockSpec((EXP_BLOCK * ROW_TILE[0], LANES), x_blk), any_spec, any_spec, any_spec]
    args = [xs_part.reshape(n_blk * EXP_BLOCK * ROW_TILE[0], LANES), w1, w3, w2]
    aliases = {}
    if ys_prev is not None:
        in_specs.append(any_spec)
        args.append(ys_prev)
        aliases = {len(ctl) + 4: 0}
    grid_spec = pltpu.PrefetchScalarGridSpec(
        num_scalar_prefetch=len(ctl),
        grid=(n_blk,),
        in_specs=in_specs,
        out_specs=pl.BlockSpec((EXP_BLOCK * ROW_TILE[0], LANES), lambda i, *refs: (blk0 + i, 0)),
        scratch_shapes=[pltpu.VMEM((2, D, D_EXPERT), F32), pltpu.VMEM((2, D, D_EXPERT), F32),
                        pltpu.VMEM((2, D_EXPERT, D), F32),
                        pltpu.VMEM((D, D_EXPERT), BF16), pltpu.VMEM((D, D_EXPERT), BF16),
                        pltpu.VMEM((D_EXPERT, D), BF16), pltpu.SemaphoreType.DMA((3, 2))])
    return pl.pallas_call(
        functools.partial(_experts_body, layer=layer, blk0=blk0, n_blk=n_blk), grid_spec=grid_spec,
        out_shape=jax.ShapeDtypeStruct((L_SLOTS * ROW_TILE[0], LANES), jnp.int32),
        input_output_aliases=aliases,
        compiler_params=_cparams(1), name="experts")(*ctl, *args)


TM_FFN = 640


def _ffn_pre_body(x_ref, g_ref, shp_ref, shs_ref, scp_ref, scs_ref, rw_ref, hb_ref, lg_ref, hp_ref):
    i = pl.program_id(0)

    @pl.when(i < N // TM_FFN)
    def _():
        for sidx in range(TM_FFN // SUB):
            rows = pl.ds(sidx * SUB, SUB)
            h = _prenorm_rows(i * (TM_FFN // SUB) + sidx, x_ref[rows, :], g_ref, shp_ref, shs_ref, scp_ref, scs_ref)
            lg_ref[rows, :] = jnp.dot(h, rw_ref[...], precision=HIGHEST, preferred_element_type=F32)
            packed = _pack_bf16_pair(h[:, :D // 2], h[:, D // 2:])
            for c in range(ROW_TILE[0]):
                hp_ref[pl.ds(sidx * SUB * ROW_TILE[0] + c, SUB, stride=ROW_TILE[0]), :] = (
                    packed[:, c * LANES:(c + 1) * LANES])
            hb_ref[rows, :] = h.astype(BF16)

    @pl.when(i >= N // TM_FFN)
    def _():
        hp_ref[...] = jnp.zeros_like(hp_ref)


def _ffn_pre(x, g, mod, layer, rw):
    last = N // TM_FFN - 1
    row = lambda i: (jnp.minimum(i, last), 0)
    mspec = lambda part, rows, blk: pl.BlockSpec((None, rows, D), lambda i, l=layer, p=part, b=blk: (l, b, p))
    return pl.pallas_call(
        _ffn_pre_body,
        grid=(N // TM_FFN + 1,),
        in_specs=[pl.BlockSpec((TM_FFN, D), row), pl.BlockSpec((1, D), lambda i: (0, 0)),
                  mspec(3, 8, 0), mspec(3, SUB, 1), mspec(4, 8, 0), mspec(4, SUB, 1),
                  pl.BlockSpec((None, D, LANES), lambda i, l=layer: (l, 0, 0))],
        out_specs=[pl.BlockSpec((TM_FFN, D), row), pl.BlockSpec((TM_FFN, LANES), row),
                   pl.BlockSpec((TM_FFN * ROW_TILE[0], LANES), lambda i: (i, 0))],
        out_shape=[jax.ShapeDtypeStruct((N, D), BF16), jax.ShapeDtypeStruct((N, LANES), F32),
                   jax.ShapeDtypeStruct(((N + TM_FFN) * ROW_TILE[0], LANES), jnp.int32)],
        compiler_params=_cparams(1), name="ffn_pre")(x, g.reshape(1, D), mod, mod, mod, mod, rw)


TM_SHARED = 640


def _shared_body(h_ref, w13_ref, w2_ref, after_ref, o_ref):
    up = jnp.dot(h_ref[...], w13_ref[...], preferred_element_type=F32)
    hid = (_silu(up[:, :D_EXPERT]) * up[:, D_EXPERT:]).astype(BF16)
    o_ref[...] = jnp.dot(hid, w2_ref[...], preferred_element_type=F32).astype(BF16)


def _shared(h, w13, w2, layer, after):
    return pl.pallas_call(
        _shared_body,
        grid=(N // TM_SHARED,),
        in_specs=[pl.BlockSpec((TM_SHARED, D), lambda i: (i, 0)),
                  pl.BlockSpec((None, D, 2 * D_EXPERT), lambda i, l=layer: (l, 0, 0)),
                  pl.BlockSpec((None, D_EXPERT, D), lambda i, l=layer: (l, 0, 0)),
                  pl.BlockSpec(memory_space=pl.ANY)],
        out_specs=pl.BlockSpec((TM_SHARED, D), lambda i: (i, 0)),
        out_shape=jax.ShapeDtypeStruct((N, D), BF16),
        compiler_params=_cparams(1), name="shared")(h, w13, w2, after)


def _combine_body(g_ref, wt_ref, sh_ref, x_ref, gn_ref, gp_ref, gs_ref, *rest, tile0, n_out, final):
    outs = rest[-n_out:]
    half = D // 2
    acc_lo = sh_ref[:, :half].astype(F32)
    acc_hi = sh_ref[:, half:].astype(F32)
    wt = wt_ref[...]
    per_tok = TOP_K * ROW_TILE[0]
    for k in range(TOP_K):
        packed = jnp.concatenate([g_ref[pl.ds(k * ROW_TILE[0] + c, SUB, stride=per_tok), :]
                                  for c in range(ROW_TILE[0])], axis=1)
        lo, hi = _unpack_bf16_pair(packed)
        w_c = wt[:, k:k + 1]
        acc_lo = acc_lo + w_c * lo
        acc_hi = acc_hi + w_c * hi
    t = tile0 + pl.program_id(0)
    val = _post_value(t, x_ref[...], jnp.concatenate([acc_lo, acc_hi], axis=1), gn_ref, gp_ref, gs_ref)
    if not final:
        outs[0][...] = val
        outs[1][...] = _prenorm_rows(t, val, *rest[:5]).astype(BF16)
    else:
        @pl.when(t < N_P // SUB)
        def _():
            outs[0][...] = val
        if n_out == 2:
            @pl.when(t >= N_P // SUB)
            def _():
                outs[1][...] = val


def _combine(gathered, wts, shared, x, g_post, mod, layer, tile0, n_tiles, out_prev, final, g_next):
    per_tok = TOP_K * ROW_TILE[0]
    row = lambda i: (tile0 + i, 0)
    in_specs = [pl.BlockSpec((SUB * per_tok, LANES), lambda i: (i, 0)),
                pl.BlockSpec((SUB, LANES), row), pl.BlockSpec((SUB, D), row), pl.BlockSpec((SUB, D), row),
                pl.BlockSpec((1, D), lambda i: (0, 0)),
                pl.BlockSpec((None, 8, D), lambda i, l=layer: (l, 0, 5)),
                pl.BlockSpec((None, SUB, D), lambda i, l=layer: (l, 1, 5))]
    args = [gathered.reshape(n_tiles * SUB * per_tok, LANES), wts, shared, x, g_post.reshape(1, D), mod, mod]
    if not final:
        in_specs += [pl.BlockSpec((1, D), lambda i: (0, 0))] + _mod_specs(layer + 1, 0) + _mod_specs(layer + 1, 1)
        args += [g_next.reshape(1, D), mod, mod, mod, mod]
    aliases = {}
    for k, prev in enumerate(out_prev or ()):
        if k == 0 or not final:
            in_specs.append(pl.BlockSpec(memory_space=pl.ANY))
            args.append(prev)
            aliases[len(args) - 1] = k
    if not final:
        out_specs = [pl.BlockSpec((SUB, D), row), pl.BlockSpec((SUB, D), row)]
        out_shape = [jax.ShapeDtypeStruct((N, D), F32), jax.ShapeDtypeStruct((N, D), BF16)]
    else:
        last_p = N_P // SUB - 1
        out_specs = [pl.BlockSpec((SUB, D), lambda i: (jnp.minimum(tile0 + i, last_p), 0))]
        out_shape = [jax.ShapeDtypeStruct((N_P, D), F32)]
        if tile0 + n_tiles > N_P // SUB:
            out_specs.append(pl.BlockSpec((N_S, D), lambda i: (0, 0)))
            out_shape.append(jax.ShapeDtypeStruct((N_S, D), F32))
    return pl.pallas_call(
        functools.partial(_combine_body, tile0=tile0, n_out=len(out_shape), final=final),
        grid=(n_tiles,),
        in_specs=in_specs, out_specs=out_specs, out_shape=out_shape,
        input_output_aliases=aliases,
        compiler_params=_cparams(1), name="combine")(*args)


def _slots_body(start_ref, eid_ref, pos_ref, o_ref):
    eid = eid_ref[...]
    acc = pos_ref[...]
    for e in range(N_EXPERTS):
        acc = acc + jnp.where(eid == e, start_ref[e], 0)
    o_ref[...] = acc


def _slots(pad_start, eid, pos):
    grid_spec = pltpu.PrefetchScalarGridSpec(
        num_scalar_prefetch=1, grid=(1,),
        in_specs=[pl.BlockSpec((TOP_K, N), lambda i, s: (0, 0)), pl.BlockSpec((TOP_K, N), lambda i, s: (0, 0))],
        out_specs=pl.BlockSpec((TOP_K, N), lambda i, s: (0, 0)))
    return pl.pallas_call(_slots_body, grid_spec=grid_spec,
                          out_shape=jax.ShapeDtypeStruct((TOP_K, N), jnp.int32),
                          compiler_params=_cparams(1), name="slots")(pad_start, eid, pos)


def _put_sample_rows_body(*refs):
    n = len(refs) // 3
    for src, dst in zip(refs[:n], refs[2 * n:]):
        dst[...] = src[...].astype(BF16)


def _put_sample_rows(sample_rows, full):
    n = len(full)
    return pl.pallas_call(
        _put_sample_rows_body,
        grid=(1,),
        in_specs=[pl.BlockSpec((N_S, 512), lambda i: (0, 0))] * n + [pl.BlockSpec(memory_space=pl.ANY)] * n,
        out_specs=[pl.BlockSpec((N_S, 512), lambda i: (N_P // N_S, 0))] * n,
        out_shape=[jax.ShapeDtypeStruct((N, 512), BF16)] * n,
        input_output_aliases={n + k: k for k in range(n)},
        compiler_params=_cparams(1), name="put_sample_rows")(*sample_rows, *full)


COMBINE_RANGES = ((0, 33), (33, 32))

def _prepare_weights(w_in, w_merge_gate, w_branch, w_out, router_w, shared_w1, shared_w3, shared_w2):
    return dict(
        w_main=jnp.concatenate([w_in[:, :, :1536], w_in[:, :, 1552:]], axis=2).astype(BF16),
        w_low=jnp.pad(w_in[:, :, 1536:1552], ((0, 0), (0, 0), (0, LANES - 16))).astype(BF16),
        w_mg=w_merge_gate.astype(BF16), w_br=w_branch.astype(BF16), w_out=w_out.astype(BF16),
        rw=jnp.pad(router_w, ((0, 0), (0, 0), (0, LANES - N_EXPERTS))),
        w13=jnp.concatenate([shared_w1, shared_w3], axis=2).astype(BF16), sw2=shared_w2.astype(BF16))


def _rope_tables(pos):
    half = DK // 2
    inv = ROPE_BASE ** (-jnp.arange(half, dtype=F32) / half)
    ang = pos.astype(F32)[:, None] * inv[None, :]
    cos = jnp.cos(ang)
    sin = jnp.sin(ang)
    return jnp.concatenate([cos, cos], axis=1), jnp.concatenate([-sin, sin], axis=1)


def _layer(l, x, mod, s_gla, s_pool, s_ret, wts, prep, final, prev_gla, prev_ret, h_in):
    (norm_mix_pre, norm_mix_post, norm_ffn_pre, norm_ffn_post, w_in, w_gla_gate, b_gla_gate, gla_norm,
     pool_w, pool_scale, ret_norm, sgu_norm, sgu_w, sgu_b, w_branch, w_merge_gate, b_merge_gate, w_out,
     router_w, router_bias, expert_w1, expert_w3, expert_w2, shared_w1, shared_w3, shared_w2) = wts

    h = _prenorm(x, norm_mix_pre[l], mod, l) if h_in is None else h_in
    p_main = _matmul(h, prep["w_main"], l, 1664, 768, name="inproj")
    p_low = _matmul(h, prep["w_low"], l, 1664, LANES, name="inproj_low")

    w_gate_pad = jnp.pad(w_gla_gate[l], ((0, LANES - 16), (0, 0)))
    b_gate = b_gla_gate[l].reshape(1, HEADS * DK)
    log_gamma = jnp.log1p(-jnp.exp2(-5.0 - jnp.arange(HEADS, dtype=F32)))
    dec_row = jnp.repeat(log_gamma, DK).reshape(1, HEADS * DK)
    cos_p, sin_p = _rope_tables(jnp.arange(T_P))
    cos_p = jnp.tile(cos_p, (1, 2))
    sin_p = jnp.tile(sin_p, (1, 2))
    g_gla = gla_norm[l].reshape(1, HEADS * DV)
    g_ret = ret_norm[l].reshape(1, HEADS * DV)

    oa_p, gla_p = _la_prompt(p_main, C_GQ, C_GK, C_GV, C_GR, p_low, p_low, w_gate_pad, b_gate, g_gla, False)
    oc_p, ret_p = _la_prompt(p_main, C_RQ, C_RK, C_RV, C_RG, cos_p, sin_p, dec_row, b_gate, g_ret, True)
    pw_bf = pool_w[l].astype(BF16)
    pscale = pool_scale[l].reshape(1, 512)
    ob_p = _pool_prompt(p_main, pw_bf, pscale)
    sgu_g = sgu_norm[l].reshape(1, 512)
    od_p = _sgu_prompt(p_main, sgu_g, sgu_w[l], jnp.pad(sgu_b[l].T, ((0, 0), (0, LANES - 4))))

    ps = p_main[N_P:]
    q_t = _to_tiles_t(ps[:, C_GQ:C_GQ + 256])
    k_t = _to_tiles_t(ps[:, C_GK:C_GK + 256])
    glow_t = jnp.pad(_to_tiles_t(p_low[N_P:, :16]), ((0, 0), (0, LANES - 16), (0, 0)))
    w_gate_t = jnp.pad(w_gla_gate[l].T, ((0, 0), (0, LANES - 16)))
    b_col = jnp.broadcast_to(b_gla_gate[l][:, None], (HEADS * DK, LANES))
    logit_t = _gate_logits_t(w_gate_t, glow_t, b_col)
    dummy = jnp.zeros((HEADS * DK, LANES), F32)
    oa_s, gla_s = _la_sample(q_t, k_t, logit_t, dummy, dummy, p_main, C_GV, C_GR, g_gla, s_gla, l, prev_gla, False)
    cos_s, sin_s = _rope_tables(jnp.full((1,), PAST_LEN))
    cos_c = jnp.broadcast_to(jnp.tile(cos_s[0], HEADS)[:, None], (HEADS * DK, LANES))
    sin_c = jnp.broadcast_to(jnp.tile(sin_s[0], HEADS)[:, None], (HEADS * DK, LANES))
    dec_c = jnp.broadcast_to(jnp.repeat(log_gamma, DK)[:, None], (HEADS * DK, LANES))
    rq_t = _to_tiles_t(ps[:, C_RQ:C_RQ + 256])
    rk_t = _to_tiles_t(ps[:, C_RK:C_RK + 256])
    oc_s, ret_s = _la_sample(rq_t, rk_t, dec_c, cos_c, sin_c, p_main, C_RV, C_RG, g_ret, s_ret, l, prev_ret, True)
    sgu_w0 = jnp.repeat(sgu_w[l][:, 0, 0], LANES).reshape(1, 512)
    sgu_b0 = jnp.repeat(sgu_b[l][:, 0], LANES).reshape(1, 512)
    ob_s, od_s, vn_s = _small_sample(p_main, s_pool[l], pw_bf, pscale, sgu_g, sgu_w0, sgu_b0)
    pool_p = jnp.stack([p_main[(b + 1) * T_P - POOL_BUF:(b + 1) * T_P, C_PIN:C_PIN + 512] for b in range(B_P)])
    pool_s = jnp.concatenate([s_pool[l][:, 1:], ps[:, None, C_PIN:C_PIN + 512]], axis=1)

    branches = _put_sample_rows([oa_s, ob_s, oc_s, od_s], [oa_p, ob_p, oc_p, od_p])
    merged = _merge(h, branches, prep["w_mg"], b_merge_gate.reshape(DEPTH, 1, 4 * D), prep["w_br"], l)
    x = _outproj(merged, prep["w_out"], x, norm_mix_post[l], mod, l)

    rb = jnp.broadcast_to(router_bias[l][:, None], (N_EXPERTS, SUB))
    h2, logits, h2_packed = _ffn_pre(x, norm_ffn_pre[l], mod, l, prep["rw"])
    eid, pos, wt, counts = _router(logits, rb)
    counts = counts[:, 0].astype(jnp.int32)
    padded = (counts + EXP_BLOCK - 1) // EXP_BLOCK * EXP_BLOCK
    pad_end = jnp.cumsum(padded)
    pad_start = pad_end - padded
    nused = (pad_end[-1] // EXP_BLOCK).astype(jnp.int32).reshape(1)
    blk_row = jnp.arange(N_BLOCKS, dtype=jnp.int32) * EXP_BLOCK
    block_e = jnp.minimum(jnp.sum((blk_row[:, None] >= pad_end[None, :]).astype(jnp.int32), axis=1),
                          N_EXPERTS - 1)
    first = jnp.concatenate([jnp.ones((1,), jnp.int32), (block_e[1:] != block_e[:-1]).astype(jnp.int32)])
    first = jnp.where(blk_row < pad_end[-1], first, 0)
    par = (jnp.cumsum(first) - 1) % 2
    live = jnp.where(padded > 0, jnp.arange(N_EXPERTS), N_EXPERTS)
    after = jnp.concatenate([lax.cummin(live, reverse=True)[1:], jnp.full((1,), N_EXPERTS)])
    of_block = block_e[:, None] == jnp.arange(N_EXPERTS)
    next_e = jnp.sum(jnp.where(of_block, jnp.where(after < N_EXPERTS, after, -1), 0), axis=1).astype(jnp.int32)
    next_blk = jnp.sum(jnp.where(of_block, pad_end // EXP_BLOCK, 0), axis=1).astype(jnp.int32)
    slots = _slots(pad_start.astype(jnp.int32), eid, pos).T.reshape(N_ASSIGN)
    wt = jnp.pad(wt.T, ((0, 0), (0, LANES - TOP_K)))
    table = h2_packed.reshape((N + TM_FFN,) + ROW_TILE)
    ctl = (block_e, first, par.astype(jnp.int32), next_e, next_blk, nused)
    xs_parts = [_sc_dispatch(table, slots, blk0, n_blk) for blk0, n_blk in PARTS]
    shared = _shared(h2, prep["w13"], prep["sw2"], l, slots)
    ys = None
    for xs_part, (blk0, n_blk) in zip(xs_parts, PARTS):
        ys = _experts(ctl, xs_part, expert_w1, expert_w3, expert_w2, l, blk0, n_blk, ys)
    ys = ys.reshape((L_SLOTS,) + ROW_TILE)
    outs = None
    for tile0, n_tiles in COMBINE_RANGES:
        a0, a1 = tile0 * SUB * TOP_K, (tile0 + n_tiles) * SUB * TOP_K
        outs = _combine(_sc_gather(ys, slots[a0:a1]), wt, shared, x, norm_ffn_post[l], mod, l, tile0, n_tiles,
                        outs, final, None if final else norm_mix_pre[l + 1])
    x, h_next = (tuple(outs), None) if final else outs
    return x, (gla_p, pool_p, pool_s, ret_p, vn_s), gla_s, ret_s, h_next


def kernel(x_prompt, x_sample, c_prompt, c_sample, state_gla, state_pool, state_ret, w_ada, b_ada, norm_mix_pre, norm_mix_post, norm_ffn_pre, norm_ffn_post, w_in, w_gla_gate, b_gla_gate, gla_norm, pool_w, pool_scale, ret_norm, sgu_norm, sgu_w, sgu_b, w_branch, w_merge_gate, b_merge_gate, w_out, router_w, router_bias, expert_w1, expert_w3, expert_w2, shared_w1, shared_w3, shared_w2):
    wts = (norm_mix_pre, norm_mix_post, norm_ffn_pre, norm_ffn_post, w_in, w_gla_gate, b_gla_gate, gla_norm,
           pool_w, pool_scale, ret_norm, sgu_norm, sgu_w, sgu_b, w_branch, w_merge_gate, b_merge_gate, w_out,
           router_w, router_bias, expert_w1, expert_w3, expert_w2, shared_w1, shared_w3, shared_w2)
    c_all = jnp.zeros((MOD_ROWS, D), F32).at[:B_P].set(c_prompt).at[SUB:SUB + N_S].set(c_sample)
    mod = _ada(c_all, w_ada, b_ada)
    x = (x_prompt.reshape(N_P, D), x_sample.reshape(N_S, D))
    prep = _prepare_weights(w_in, w_merge_gate, w_branch, w_out, router_w, shared_w1, shared_w3, shared_w2)
    per_layer = []
    gla_s = ret_s = h = None
    for l in range(DEPTH):
        x, states, gla_s, ret_s, h = _layer(l, x, mod, state_gla, state_pool, state_ret, wts, prep, l == DEPTH - 1,
                                            gla_s, ret_s, h)
        per_layer.append(states)
    gla_p, pool_p, pool_s, ret_p, vn_s = (jnp.stack(z) for z in zip(*per_layer))
    return (x[0].reshape(B_P, T_P, D), x[1].reshape(N_S, 1, D),
            gla_p, gla_s, pool_p, pool_s, ret_p, ret_s, vn_s.reshape(DEPTH, N_S, 1, 512))
```

```python
import functools

import jax
import jax.numpy as jnp
from jax import lax
from jax.experimental import pallas as pl
from jax.experimental.pallas import tpu as pltpu
from jax.experimental.pallas import tpu_sc as plsc

F32 = jnp.float32
BF16 = jnp.bfloat16
HIGHEST = lax.Precision.HIGHEST

D = 2048
B_P, T_P = 4, 2048
N_P = B_P * T_P
N_S = 128
N = N_P + N_S
DEPTH = 2
PAST_LEN = 16384
EPS = 1e-6
HEADS, DK, DV = 4, 64, 128
CHUNK = 64
GATE_TEMP = 16.0
POOL_WINDOWS = (2, 4, 8, 16)
POOL_BUF = 15
ROPE_BASE = 10000.0
N_EXPERTS = 64
TOP_K = 8
D_EXPERT = 512
ROUTED_SCALE = 2.5

LANES = 128
SUB = 128
MOD_ROWS = 256
EXP_BLOCK = 256
N_ASSIGN = N * TOP_K
N_BLOCKS = -(-(N_ASSIGN + N_EXPERTS * (EXP_BLOCK - 1)) // EXP_BLOCK)
L_SLOTS = N_BLOCKS * EXP_BLOCK
VMEM_LIMIT = 56 * 1024 * 1024

C_GQ, C_GK, C_GV, C_GR, C_PIN, C_RQ, C_RK, C_RV, C_RG, C_SU, C_SV = (
    0, 256, 512, 1024, 1536, 2048, 2304, 2560, 3072, 3584, 4096)


def _cparams(n_axes=1):
    return pltpu.CompilerParams(dimension_semantics=("arbitrary",) * n_axes,
                                vmem_limit_bytes=VMEM_LIMIT)


def _silu(x):
    return x * jax.nn.sigmoid(x)


def _mod_rows(t, mp_ref, ms_ref):
    b = jnp.minimum(t // (T_P // SUB), B_P - 1)
    return jnp.where(t >= N_P // SUB, ms_ref[...], mp_ref[pl.ds(b, 1), :])


def _mod_specs(layer, part):
    return [pl.BlockSpec((None, 8, D), lambda i, l=layer, p=part: (l, 0, p)),
            pl.BlockSpec((None, SUB, D), lambda i, l=layer, p=part: (l, 1, p))]


def _pack_bf16_pair(lo, hi):
    lo_u = lax.bitcast_convert_type(lo.astype(BF16).astype(F32), jnp.uint32)
    hi_u = lax.bitcast_convert_type(hi.astype(BF16).astype(F32), jnp.uint32)
    return lax.bitcast_convert_type((hi_u & jnp.uint32(0xFFFF0000)) | (lo_u >> 16), jnp.int32)


def _unpack_bf16_pair(w):
    u = lax.bitcast_convert_type(w, jnp.uint32)
    lo = lax.bitcast_convert_type(u << 16, F32)
    hi = lax.bitcast_convert_type(u & jnp.uint32(0xFFFF0000), F32)
    return lo, hi


ROW_TILE = (8, LANES)


def _load_row_tiles_2d(ref, rows):
    return jnp.concatenate([ref[pl.ds(c, rows, stride=ROW_TILE[0]), :] for c in range(ROW_TILE[0])], axis=1)


def _store_row_tiles_2d(ref, val, rows):
    for c in range(ROW_TILE[0]):
        ref[pl.ds(c, rows, stride=ROW_TILE[0]), :] = val[:, c * LANES:(c + 1) * LANES]


def _ada_body(c_ref, w_ref, b_ref, o_ref):
    s = _silu(c_ref[...]).astype(BF16)
    o_ref[...] = jnp.dot(s, w_ref[...].astype(BF16), preferred_element_type=F32) + b_ref[...]


def _ada(c_all, w_ada, b_ada):
    tn = 2048
    return pl.pallas_call(
        _ada_body,
        grid=(DEPTH, 6 * D // tn),
        in_specs=[pl.BlockSpec((MOD_ROWS, D), lambda l, j: (0, 0)),
                  pl.BlockSpec((None, D, tn), lambda l, j: (l, 0, j)),
                  pl.BlockSpec((None, 1, tn), lambda l, j: (l, 0, j))],
        out_specs=pl.BlockSpec((None, MOD_ROWS, tn), lambda l, j: (l, 0, j)),
        out_shape=jax.ShapeDtypeStruct((DEPTH, MOD_ROWS, 6 * D), F32),
        compiler_params=_cparams(2), name="ada")(c_all, w_ada, b_ada.reshape(DEPTH, 1, 6 * D))


ZERO_ROWS = 2 * SUB


def _x_specs(x, tm, n_axes):
    row = (lambda i: (i, 0)) if n_axes == 1 else (lambda i, j: (i, 0))
    if not isinstance(x, tuple):
        return [pl.BlockSpec((tm, D), row)], [x]
    zero = (lambda i: (0, 0)) if n_axes == 1 else (lambda i, j: (0, 0))
    return [pl.BlockSpec((tm, D), row), pl.BlockSpec((N_S, D), zero)], list(x)


def _x_rows(x_refs, t, rows):
    if len(x_refs) == 1:
        return x_refs[0][rows, :]
    return jnp.where(t >= N_P // SUB, x_refs[1][...], x_refs[0][rows, :])


def _prenorm_rows(t, x, g_ref, shp_ref, shs_ref, scp_ref, scs_ref):
    y = x * lax.rsqrt(jnp.mean(x * x, axis=-1, keepdims=True) + EPS) * g_ref[...]
    return y * (1.0 + _mod_rows(t, scp_ref, scs_ref)) + _mod_rows(t, shp_ref, shs_ref)


TM_NORM = 640


def _prenorm_body(*refs):
    g_ref, shp_ref, shs_ref, scp_ref, scs_ref, h_ref = refs[-6:]
    for sidx in range(TM_NORM // SUB):
        rows = pl.ds(sidx * SUB, SUB)
        t = pl.program_id(0) * (TM_NORM // SUB) + sidx
        h_ref[rows, :] = _prenorm_rows(t, _x_rows(refs[:-6], t, rows), g_ref, shp_ref, shs_ref, scp_ref,
                                       scs_ref).astype(BF16)


def _prenorm(x, g, mod, layer):
    x_specs, x_args = _x_specs(x, TM_NORM, 1)
    return pl.pallas_call(
        _prenorm_body,
        grid=(N // TM_NORM,),
        in_specs=x_specs + [pl.BlockSpec((1, D), lambda i: (0, 0))] + _mod_specs(layer, 0) + _mod_specs(layer, 1),
        out_specs=pl.BlockSpec((TM_NORM, D), lambda i: (i, 0)),
        out_shape=jax.ShapeDtypeStruct((N, D), BF16),
        compiler_params=_cparams(1), name="prenorm")(*x_args, g.reshape(1, D), mod, mod, mod, mod)


def _mm_body(x_ref, w_ref, o_ref):
    o_ref[...] = jnp.dot(x_ref[...], w_ref[...], preferred_element_type=F32).astype(o_ref.dtype)


def _matmul(x, w_all, layer, tm, tn, out_dtype=F32, name="mm"):
    m, k = x.shape
    n = w_all.shape[2]
    return pl.pallas_call(
        _mm_body,
        grid=(m // tm, n // tn),
        in_specs=[pl.BlockSpec((tm, k), lambda i, j: (i, 0)),
                  pl.BlockSpec((None, k, tn), lambda i, j, l=layer: (l, 0, j))],
        out_specs=pl.BlockSpec((tm, tn), lambda i, j: (i, j)),
        out_shape=jax.ShapeDtypeStruct((m, n), out_dtype),
        compiler_params=_cparams(2), name=name)(x, w_all)


ROWS_LA = 256


def _swap_halves_lanes(x):
    lane = lax.broadcasted_iota(jnp.int32, x.shape, 1)
    return jnp.where((lane % 64) < 32, pltpu.roll(x, 96, 1), pltpu.roll(x, 32, 1))


def _rope_lanes(x, cos, sin_signed):
    parts = []
    for half in range(2):
        xh = x[:, half * LANES:(half + 1) * LANES]
        parts.append(xh * cos + _swap_halves_lanes(xh) * sin_signed)
    return jnp.concatenate(parts, axis=1)


def _la_prompt_body(q_ref, k_ref, v_ref, r_ref, aux_ref, aux2_ref, dec_ref, bias_ref, g_ref,
                    o_ref, st_out_ref, st_ref, *, retention):
    t = pl.program_id(1)

    @pl.when(t == 0)
    def _():
        st_ref[...] = jnp.zeros_like(st_ref)

    n_ch = ROWS_LA // CHUNK
    ri = lax.broadcasted_iota(jnp.int32, (ROWS_LA, ROWS_LA), 0)
    ci = lax.broadcasted_iota(jnp.int32, (ROWS_LA, ROWS_LA), 1)
    causal = (ri >= ci) & ((ri // CHUNK) == (ci // CHUNK))
    scale = DK ** -0.5

    q = q_ref[...]
    k = k_ref[...]
    v = v_ref[...]
    if retention:
        cos = aux_ref[...]
        sin = aux2_ref[...]
        q = _rope_lanes(q, cos, sin)
        k = _rope_lanes(k, cos, sin) * scale
        step = (lax.broadcasted_iota(jnp.int32, (ROWS_LA, HEADS * DK), 0) % CHUNK + 1).astype(F32)
        bc = step * dec_ref[...]
    else:
        q = q * scale
        logit = jnp.dot(aux_ref[...], dec_ref[...], precision=HIGHEST, preferred_element_type=F32) + bias_ref[...]
        la = jax.nn.log_sigmoid(logit) / GATE_TEMP
        bc = jnp.dot(causal.astype(F32), la, precision=HIGHEST, preferred_element_type=F32)
    bl = bc.reshape(n_ch, CHUNK, HEADS * DK)[:, CHUNK - 1:CHUNK, :]
    bl_rows = jnp.broadcast_to(bl, (n_ch, CHUNK, HEADS * DK)).reshape(ROWS_LA, HEADS * DK)
    qd = q * jnp.exp(bc)
    ki = k * jnp.exp(-bc)
    ke = k * jnp.exp(bl_rows - bc)
    ac = jnp.exp(bl)
    outs = []
    for h in range(HEADS):
        ks = slice(h * DK, (h + 1) * DK)
        vs = slice(h * DV, (h + 1) * DV)
        qd_h = qd[:, ks].astype(BF16)
        ki_h = ki[:, ks].astype(BF16)
        ke_h = ke[:, ks].astype(BF16)
        v_h = v[:, vs].astype(BF16)
        sc = lax.dot_general(qd_h, ki_h, (((1,), (1,)), ((), ())), preferred_element_type=F32)
        sc = jnp.where(causal, sc, 0.0)
        o_h = jnp.dot(sc.astype(BF16), v_h, preferred_element_type=F32)
        inter = []
        for c in range(n_ch):
            rows = slice(c * CHUNK, (c + 1) * CHUNK)
            st = st_ref[h]
            inter.append(lax.dot_general(qd_h[rows], st.astype(BF16), (((1,), (1,)), ((), ())),
                                         preferred_element_type=F32))
            kv_t = lax.dot_general(v_h[rows], ke_h[rows], (((0,), (0,)), ((), ())), preferred_element_type=F32)
            st_ref[h] = st * ac[c][:, ks] + kv_t
        o_h = o_h + jnp.concatenate(inter, axis=0)
        outs.append(o_h * lax.rsqrt(jnp.mean(o_h * o_h, axis=-1, keepdims=True) + EPS) * g_ref[:, vs])
    o_ref[...] = (jnp.concatenate(outs, axis=1) * _silu(r_ref[...])).astype(BF16)
    st_out_ref[...] = st_ref[...]


def _la_prompt(p_main, cq, ck, cv, cr, aux, aux2, dec, bias, g, retention):
    nt = T_P // ROWS_LA
    rowblk = lambda b, t: b * nt + t
    if retention:
        aux_specs = [pl.BlockSpec((ROWS_LA, LANES), lambda b, t: (t, 0)),
                     pl.BlockSpec((ROWS_LA, LANES), lambda b, t: (t, 0))]
    else:
        aux_specs = [pl.BlockSpec((ROWS_LA, LANES), lambda b, t: (rowblk(b, t), 0)),
                     pl.BlockSpec((8, LANES), lambda b, t: (0, 0))]
    o, st = pl.pallas_call(
        functools.partial(_la_prompt_body, retention=retention),
        grid=(B_P, nt),
        in_specs=[pl.BlockSpec((ROWS_LA, 256), lambda b, t: (rowblk(b, t), cq // 256)),
                  pl.BlockSpec((ROWS_LA, 256), lambda b, t: (rowblk(b, t), ck // 256)),
                  pl.BlockSpec((ROWS_LA, 512), lambda b, t: (rowblk(b, t), cv // 512)),
                  pl.BlockSpec((ROWS_LA, 512), lambda b, t: (rowblk(b, t), cr // 512))]
        + aux_specs
        + [pl.BlockSpec(dec.shape, lambda b, t: (0, 0)),
           pl.BlockSpec((1, HEADS * DK), lambda b, t: (0, 0)),
           pl.BlockSpec((1, HEADS * DV), lambda b, t: (0, 0))],
        out_specs=[pl.BlockSpec((ROWS_LA, HEADS * DV), lambda b, t: (rowblk(b, t), 0)),
                   pl.BlockSpec((None, HEADS, DV, DK), lambda b, t: (b, 0, 0, 0))],
        out_shape=[jax.ShapeDtypeStruct((N, HEADS * DV), BF16),
                   jax.ShapeDtypeStruct((B_P, HEADS, DV, DK), F32)],
        scratch_shapes=[pltpu.VMEM((HEADS, DV, DK), F32)],
        compiler_params=_cparams(2), name="ret_prompt" if retention else "gla_prompt",
    )(p_main, p_main, p_main, p_main, aux, aux2, dec, bias, g)
    return o, jnp.swapaxes(st, -1, -2)


SAMPLE_TILE = 8


def _la_sample_body(qt_ref, kt_ref, lt_ref, cos_ref, sin_ref, v_ref, r_ref, g_ref, s_ref, *rest, retention):
    o_ref, s_out_ref = rest[-2:]
    scale = DK ** -0.5
    qt = qt_ref[...]
    kt = kt_ref[...]
    if retention:
        def rope(x):
            sw = jnp.concatenate(
                [x[h * DK + (DK // 2) * (1 - j): h * DK + (DK // 2) * (2 - j), :]
                 for h in range(HEADS) for j in range(2)], axis=0)
            return x * cos_ref[...] + sw * sin_ref[...]
        qt = rope(qt)
        kt = rope(kt) * scale
        la = lt_ref[...]
    else:
        qt = qt * scale
        la = jax.nn.log_sigmoid(lt_ref[...]) / GATE_TEMP
    at = jnp.exp(la)
    qd = qt * at
    ki = kt * jnp.exp(-la)
    prod = qd * ki
    v8 = v_ref[...]
    r8 = r_ref[...]
    g = g_ref[...]
    for j in range(SAMPLE_TILE):
        for h in range(HEADS):
            ks = slice(h * DK, (h + 1) * DK)
            vs = slice(h * DV, (h + 1) * DV)
            a_c = jnp.broadcast_to(at[ks, j:j + 1], (DK, DV))
            k_c = jnp.broadcast_to(kt[ks, j:j + 1], (DK, DV))
            q_c = jnp.broadcast_to(qd[ks, j:j + 1], (DK, DV))
            s_c = jnp.broadcast_to(jnp.sum(prod[ks, j:j + 1], axis=0, keepdims=True), (1, DV))
            s0 = s_ref[j, h]
            v_row = v8[j:j + 1, vs]
            s_out_ref[j, h] = a_c * s0 + k_c * v_row
            o_row = s_c * v_row + jnp.sum(q_c * s0, axis=0, keepdims=True)
            o_n = o_row * lax.rsqrt(jnp.mean(o_row * o_row, axis=-1, keepdims=True) + EPS) * g[:, vs]
            o_ref[j:j + 1, vs] = o_n * _silu(r8[j:j + 1, vs])


def _la_sample(qt, kt, lt, cos_t, sin_t, p_main, cv, cr, g, s0_all, layer, s_prev, retention):
    nt = N_S // SAMPLE_TILE
    row0 = N_P // SAMPLE_TILE
    tile = pl.BlockSpec((None, HEADS * DK, LANES), lambda i: (i, 0, 0))
    full = pl.BlockSpec((HEADS * DK, LANES), lambda i: (0, 0))
    lt_spec = full if retention else tile
    return pl.pallas_call(
        functools.partial(_la_sample_body, retention=retention),
        grid=(nt,),
        in_specs=[tile, tile, lt_spec, full, full,
                  pl.BlockSpec((SAMPLE_TILE, 512), lambda i: (row0 + i, cv // 512)),
                  pl.BlockSpec((SAMPLE_TILE, 512), lambda i: (row0 + i, cr // 512)),
                  pl.BlockSpec((1, HEADS * DV), lambda i: (0, 0)),
                  pl.BlockSpec((None, SAMPLE_TILE, HEADS, DK, DV), lambda i, l=layer: (l, i, 0, 0, 0))]
        + ([] if s_prev is None else [pl.BlockSpec(memory_space=pl.ANY)]),
        out_specs=[pl.BlockSpec((SAMPLE_TILE, HEADS * DV), lambda i: (i, 0)),
                   pl.BlockSpec((None, SAMPLE_TILE, HEADS, DK, DV), lambda i, l=layer: (l, i, 0, 0, 0))],
        out_shape=[jax.ShapeDtypeStruct((N_S, HEADS * DV), F32),
                   jax.ShapeDtypeStruct((DEPTH, N_S, HEADS, DK, DV), F32)],
        input_output_aliases={} if s_prev is None else {9: 1},
        compiler_params=_cparams(1), name="ret_sample" if retention else "gla_sample",
    )(qt, kt, lt, cos_t, sin_t, p_main, p_main, g, s0_all, *([] if s_prev is None else [s_prev]))


def _gate_logits_t_body(w_ref, x_ref, b_ref, o_ref):
    o_ref[...] = jnp.dot(w_ref[...], x_ref[...], precision=HIGHEST, preferred_element_type=F32) + b_ref[...]


def _gate_logits_t(w_gate_t, glow_t, b_col):
    nt = N_S // SAMPLE_TILE
    return pl.pallas_call(
        _gate_logits_t_body,
        grid=(nt,),
        in_specs=[pl.BlockSpec((HEADS * DK, LANES), lambda i: (0, 0)),
                  pl.BlockSpec((None, LANES, LANES), lambda i: (i, 0, 0)),
                  pl.BlockSpec((HEADS * DK, LANES), lambda i: (0, 0))],
        out_specs=pl.BlockSpec((None, HEADS * DK, LANES), lambda i: (i, 0, 0)),
        out_shape=jax.ShapeDtypeStruct((nt, HEADS * DK, LANES), F32),
        compiler_params=_cparams(1), name="gate_logits_t")(w_gate_t, glow_t, b_col)


def _to_tiles_t(x):
    c = x.shape[1]
    xt = jnp.swapaxes(x.reshape(N_S // SAMPLE_TILE, SAMPLE_TILE, c), 1, 2)
    return jnp.pad(xt, ((0, 0), (0, 0), (0, LANES - SAMPLE_TILE)))


ROWS_POOL = 512


def _pool_mix(y, w_ref, sc_ref):
    outs = []
    for gi in range(4):
        cs = slice(gi * LANES, (gi + 1) * LANES)
        outs.append(jnp.dot(y[:, cs].astype(BF16), w_ref[gi], preferred_element_type=F32))
    return jnp.concatenate(outs, axis=1) * sc_ref[...]


def _pool_prompt_body(p_ref, halo_ref, w_ref, sc_ref, o_ref):
    t = pl.program_id(1)
    p = p_ref[...]
    halo = jnp.where(t == 0, 0.0, halo_ref[...])
    full = jnp.concatenate([halo, p], axis=0)
    pos = t * ROWS_POOL + lax.broadcasted_iota(jnp.int32, (ROWS_POOL, LANES), 0)
    means = []
    for gi, w in enumerate(POOL_WINDOWS):
        s = full[:, gi * LANES:(gi + 1) * LANES]
        step = 1
        while step < w:
            s = s + pltpu.roll(s, step, 0)
            step *= 2
        win = s[16:, :]
        cnt = jnp.minimum(w, pos + 1).astype(F32)
        means.append(win / cnt)
    y = jnp.concatenate(means, axis=1) - p
    o_ref[...] = _pool_mix(y, w_ref, sc_ref).astype(BF16)


def _pool_prompt(p_main, w_bf, scale):
    nt = T_P // ROWS_POOL
    return pl.pallas_call(
        _pool_prompt_body,
        grid=(B_P, nt),
        in_specs=[pl.BlockSpec((ROWS_POOL, 512), lambda b, t: (b * nt + t, C_PIN // 512)),
                  pl.BlockSpec((16, 512), lambda b, t: (jnp.maximum((b * nt + t) * (ROWS_POOL // 16) - 1, 0),
                                                        C_PIN // 512)),
                  pl.BlockSpec((4, LANES, LANES), lambda b, t: (0, 0, 0)),
                  pl.BlockSpec((1, 512), lambda b, t: (0, 0))],
        out_specs=pl.BlockSpec((ROWS_POOL, 512), lambda b, t: (b * nt + t, 0)),
        out_shape=jax.ShapeDtypeStruct((N, 512), BF16),
        compiler_params=_cparams(2), name="pool_prompt")(p_main, p_main, w_bf, scale)


def _small_sample_body(p_ref, buf_ref, pw_ref, psc_ref, u_ref, sv_ref, sg_ref, sw_ref, sb_ref,
                       ob_ref, od_ref, vn_ref):
    p = p_ref[...]
    means = []
    for gi, w in enumerate(POOL_WINDOWS):
        cs = slice(gi * LANES, (gi + 1) * LANES)
        s = p[:, cs]
        for j in range(1, w):
            s = s + buf_ref[:, POOL_BUF - j, cs]
        means.append(s / float(min(w, PAST_LEN + 1)))
    y = jnp.concatenate(means, axis=1) - p
    ob_ref[...] = _pool_mix(y, pw_ref, psc_ref)
    sv = sv_ref[...]
    vn = sv * lax.rsqrt(jnp.mean(sv * sv, axis=-1, keepdims=True) + EPS) * sg_ref[...]
    vn_ref[...] = vn
    od_ref[...] = u_ref[...] * (sw_ref[...] * vn + sb_ref[...])


def _small_sample(p_main, buf, pw_bf, pscale, sgu_g, sgu_w0, sgu_b0):
    row = N_P // N_S
    col = lambda c: pl.BlockSpec((N_S, 512), lambda i, c=c: (row, c // 512))
    vec = pl.BlockSpec((1, 512), lambda i: (0, 0))
    return pl.pallas_call(
        _small_sample_body,
        grid=(1,),
        in_specs=[col(C_PIN), pl.BlockSpec((N_S, POOL_BUF, 512), lambda i: (0, 0, 0)),
                  pl.BlockSpec((4, LANES, LANES), lambda i: (0, 0, 0)), vec,
                  col(C_SU), col(C_SV), vec, vec, vec],
        out_specs=[pl.BlockSpec((N_S, 512), lambda i: (0, 0))] * 3,
        out_shape=[jax.ShapeDtypeStruct((N_S, 512), F32)] * 3,
        compiler_params=_cparams(1), name="small_sample",
    )(p_main, buf, pw_bf, pscale, p_main, p_main, sgu_g, sgu_w0, sgu_b0)


ROWS_SGU = 512
SGU_CHUNK = 128


def _sgu_prompt_body(u_ref, v_ref, g_ref, w_ref, bt_ref, o_ref):
    ri = lax.broadcasted_iota(jnp.int32, (SGU_CHUNK, SGU_CHUNK), 0)
    ci = lax.broadcasted_iota(jnp.int32, (SGU_CHUNK, SGU_CHUNK), 1)
    causal = ri >= ci
    for c in range(ROWS_SGU // SGU_CHUNK):
        rows = pl.ds(c * SGU_CHUNK, SGU_CHUNK)
        v = v_ref[rows, :]
        vn = (v * lax.rsqrt(jnp.mean(v * v, axis=-1, keepdims=True) + EPS) * g_ref[...]).astype(BF16)
        outs = []
        for gi in range(4):
            cs = slice(gi * LANES, (gi + 1) * LANES)
            w = jnp.where(causal, w_ref[gi], 0.0).astype(BF16)
            mixed = jnp.dot(w, vn[:, cs], preferred_element_type=F32)
            outs.append(mixed + jnp.broadcast_to(bt_ref[:, gi:gi + 1], (SGU_CHUNK, LANES)))
        o_ref[rows, :] = (u_ref[rows, :] * jnp.concatenate(outs, axis=1)).astype(BF16)


def _sgu_prompt(p_main, g, w, b_t):
    return pl.pallas_call(
        _sgu_prompt_body,
        grid=(N_P // ROWS_SGU,),
        in_specs=[pl.BlockSpec((ROWS_SGU, 512), lambda i: (i, C_SU // 512)),
                  pl.BlockSpec((ROWS_SGU, 512), lambda i: (i, C_SV // 512)),
                  pl.BlockSpec((1, 512), lambda i: (0, 0)),
                  pl.BlockSpec((4, SGU_CHUNK, SGU_CHUNK), lambda i: (0, 0, 0)),
                  pl.BlockSpec((SGU_CHUNK, LANES), lambda i: (0, 0))],
        out_specs=pl.BlockSpec((ROWS_SGU, 512), lambda i: (i, 0)),
        out_shape=jax.ShapeDtypeStruct((N, 512), BF16),
        compiler_params=_cparams(1), name="sgu_prompt")(p_main, p_main, g, w, b_t)


TM_MERGE = 640
TN_MERGE = 512


def _merge_body(h_ref, ba_ref, bb_ref, bc_ref, bd_ref, g0, g1, g2, g3, u0, u1, u2, u3,
                c0, c1, c2, c3, o_ref):
    h = h_ref[...]
    acc = None
    for br, gw, uw, gb in ((ba_ref, g0, u0, c0), (bb_ref, g1, u1, c1), (bc_ref, g2, u2, c2), (bd_ref, g3, u3, c3)):
        gate = jax.nn.sigmoid(jnp.dot(h, gw[...], preferred_element_type=F32) + gb[...])
        up = jnp.dot(br[...], uw[...], preferred_element_type=F32)
        acc = gate * up if acc is None else acc + gate * up
    o_ref[...] = acc.astype(BF16)


def _merge(h, branches, w_mg, b_mg, w_br, layer):
    nj = D // TN_MERGE
    row = lambda w: pl.BlockSpec((TM_MERGE, w), lambda i, j: (i, 0))
    gate_w = [pl.BlockSpec((None, D, TN_MERGE), lambda i, j, b=b, l=layer: (l, 0, b * nj + j)) for b in range(4)]
    up_w = [pl.BlockSpec((None, None, 512, TN_MERGE), lambda i, j, b=b, l=layer: (l, b, 0, j)) for b in range(4)]
    gate_b = [pl.BlockSpec((None, 1, TN_MERGE), lambda i, j, b=b, l=layer: (l, 0, b * nj + j)) for b in range(4)]
    return pl.pallas_call(
        _merge_body,
        grid=(N // TM_MERGE, nj),
        in_specs=[row(D)] + [row(512)] * 4 + gate_w + up_w + gate_b,
        out_specs=pl.BlockSpec((TM_MERGE, TN_MERGE), lambda i, j: (i, j)),
        out_shape=jax.ShapeDtypeStruct((N, D), BF16),
        compiler_params=_cparams(2), name="merge",
    )(h, *branches, w_mg, w_mg, w_mg, w_mg, w_br, w_br, w_br, w_br, b_mg, b_mg, b_mg, b_mg)


TM_OUT = 640
TN_OUT = 1024


def _post_value(t, x, y, gn_ref, gp_ref, gs_ref):
    yn = y * lax.rsqrt(jnp.mean(y * y, axis=-1, keepdims=True) + EPS) * gn_ref[...]
    return x + _mod_rows(t, gp_ref, gs_ref) * yn


def _outproj_body(m_ref, w_ref, *refs):
    gn_ref, gp_ref, gs_ref, o_ref, acc_ref = refs[-5:]
    j = pl.program_id(1)
    acc_ref[j] = jnp.dot(m_ref[...], w_ref[...], preferred_element_type=F32)

    @pl.when(j == D // TN_OUT - 1)
    def _():
        for sidx in range(TM_OUT // SUB):
            rows = pl.ds(sidx * SUB, SUB)
            t = pl.program_id(0) * (TM_OUT // SUB) + sidx
            y = jnp.concatenate([acc_ref[c, rows, :] for c in range(D // TN_OUT)], axis=1)
            o_ref[rows, :] = _post_value(t, _x_rows(refs[:-5], t, rows), y, gn_ref, gp_ref, gs_ref)


def _outproj(merged, w_out, x, g_post, mod, layer):
    mspec = [pl.BlockSpec((None, 8, D), lambda i, j, l=layer: (l, 0, 2)),
             pl.BlockSpec((None, SUB, D), lambda i, j, l=layer: (l, 1, 2))]
    x_specs, x_args = _x_specs(x, TM_OUT, 2)
    return pl.pallas_call(
        _outproj_body,
        grid=(N // TM_OUT, D // TN_OUT),
        in_specs=[pl.BlockSpec((TM_OUT, D), lambda i, j: (i, 0)),
                  pl.BlockSpec((None, D, TN_OUT), lambda i, j, l=layer: (l, 0, j))] + x_specs
        + [pl.BlockSpec((1, D), lambda i, j: (0, 0))] + mspec,
        out_specs=pl.BlockSpec((TM_OUT, D), lambda i, j: (i, 0)),
        out_shape=jax.ShapeDtypeStruct((N, D), F32),
        scratch_shapes=[pltpu.VMEM((D // TN_OUT, TM_OUT, TN_OUT), F32)],
        compiler_params=_cparams(2), name="outproj",
    )(merged, w_out, *x_args, g_post.reshape(1, D), mod, mod)


def _router_body(lg_ref, b_ref, eid_ref, pos_ref, wt_ref, cnt_ref, run_ref):
    i = pl.program_id(0)

    @pl.when(i == 0)
    def _():
        run_ref[...] = jnp.zeros_like(run_ref)

    ng, gs = 8, N_EXPERTS // 8
    neg = -jnp.inf
    scores = jax.nn.sigmoid(lg_ref[...].T[:N_EXPERTS, :])
    sel = scores + b_ref[...]
    sel3 = sel.reshape(ng, gs, SUB)
    sub3 = lax.broadcasted_iota(jnp.int32, (ng, gs, SUB), 1)
    gmax = jnp.max(sel3, axis=1, keepdims=True)
    first = jnp.min(jnp.where(sel3 == gmax, sub3, gs), axis=1, keepdims=True)
    gmax2 = jnp.max(jnp.where(sub3 == first, neg, sel3), axis=1, keepdims=True)
    gscore = (gmax + gmax2).reshape(ng, SUB)
    gidx = lax.broadcasted_iota(jnp.int32, (ng, SUB), 0)
    grank = jnp.zeros((ng, SUB), jnp.int32)
    for s in range(1, ng):
        other = pltpu.roll(gscore, s, 0)
        lower = gidx >= s
        grank += ((other > gscore) | ((other == gscore) & lower)).astype(jnp.int32)
    keep = jnp.broadcast_to((grank < 4).reshape(ng, 1, SUB), (ng, gs, SUB))
    masked = jnp.where(keep, sel3, neg).reshape(N_EXPERTS, SUB)
    eidx = lax.broadcasted_iota(jnp.int32, (N_EXPERTS, SUB), 0)
    rank = jnp.zeros((N_EXPERTS, SUB), jnp.int32)
    for s in range(1, N_EXPERTS):
        other = pltpu.roll(masked, s, 0)
        lower = eidx >= s
        rank += ((other > masked) | ((other == masked) & lower)).astype(jnp.int32)
    chosen = rank < TOP_K
    w_sel = jnp.where(chosen, scores, 0.0)
    w_sel = w_sel / jnp.sum(w_sel, axis=0, keepdims=True) * ROUTED_SCALE
    ri = lax.broadcasted_iota(jnp.int32, (SUB, SUB), 0)
    ci = lax.broadcasted_iota(jnp.int32, (SUB, SUB), 1)
    onehot = chosen.astype(BF16)
    pos = jnp.dot(onehot, (ri < ci).astype(BF16), preferred_element_type=F32) + run_ref[...]
    run_ref[...] = run_ref[...] + jnp.sum(chosen.astype(F32), axis=1, keepdims=True)
    cnt_ref[...] = run_ref[...]
    eidx_f = eidx.astype(F32)
    rows_e, rows_p, rows_w = [], [], []
    for kk in range(TOP_K):
        m = chosen & (rank == kk)
        rows_e.append(jnp.sum(jnp.where(m, eidx_f, 0.0), axis=0, keepdims=True))
        rows_p.append(jnp.sum(jnp.where(m, pos, 0.0), axis=0, keepdims=True))
        rows_w.append(jnp.sum(jnp.where(m, w_sel, 0.0), axis=0, keepdims=True))
    eid_ref[...] = jnp.concatenate(rows_e, axis=0).astype(jnp.int32)
    pos_ref[...] = jnp.concatenate(rows_p, axis=0).astype(jnp.int32)
    wt_ref[...] = jnp.concatenate(rows_w, axis=0)


def _router(logits, rb_col):
    tile = pl.BlockSpec((TOP_K, SUB), lambda i: (0, i))
    return pl.pallas_call(
        _router_body,
        grid=(N // SUB,),
        in_specs=[pl.BlockSpec((SUB, LANES), lambda i: (i, 0)),
                  pl.BlockSpec((N_EXPERTS, SUB), lambda i: (0, 0))],
        out_specs=[tile, tile, tile, pl.BlockSpec((N_EXPERTS, SUB), lambda i: (0, 0))],
        out_shape=[jax.ShapeDtypeStruct((TOP_K, N), jnp.int32), jax.ShapeDtypeStruct((TOP_K, N), jnp.int32),
                   jax.ShapeDtypeStruct((TOP_K, N), F32), jax.ShapeDtypeStruct((N_EXPERTS, SUB), F32)],
        scratch_shapes=[pltpu.VMEM((N_EXPERTS, SUB), F32)],
        compiler_params=_cparams(1), name="router")(logits, rb_col)


SC_CORES, SC_SUBCORES = 2, 16
SC_WORKERS = SC_CORES * SC_SUBCORES
SC_LANES = 16
SC_SCAN = N_ASSIGN // SC_WORKERS


def _sc_mesh():
    return plsc.VectorSubcoreMesh(core_axis_name="c", subcore_axis_name="s",
                                  num_cores=SC_CORES, num_subcores=SC_SUBCORES)


def _sc_worker_base(per_w):
    return (lax.axis_index("s") * SC_CORES + lax.axis_index("c")) * per_w


def _sc_chunk(per_w):
    return max(c for c in (8, 16, 24, 32) if per_w % c == 0)


def _sc_gather_rows(table_hbm, out_hbm, idx_v, rows_v, gsem, wsem, base, per_w):
    chunk = _sc_chunk(per_w)
    n_ch = per_w // chunk

    def gather(j, p):
        off = pl.multiple_of(j * chunk, 8)
        return pltpu.make_async_copy(table_hbm.at[idx_v.at[pl.ds(off, chunk)]], rows_v.at[p], gsem.at[p])

    def write(j, p):
        off = pl.multiple_of(j * chunk, 8)
        return pltpu.make_async_copy(rows_v.at[p], out_hbm.at[pl.ds(base + off, chunk)], wsem.at[p])

    def when(cond, fn):
        if isinstance(cond, bool):
            if cond:
                fn()
        else:
            pl.when(cond)(fn)

    def step(j, p):
        gather(j, p).wait()
        when(j >= 1, lambda: write(j - 1, 1 - p).wait())
        when(j + 1 < n_ch, lambda: gather(j + 1, 1 - p).start())
        write(j, p).start()

    gather(0, 0).start()

    @pl.loop(0, n_ch - n_ch % 2, step=2)
    def _(j0):
        for p in range(2):
            step(j0 + p, p)

    if n_ch % 2:
        step(n_ch - 1, 0)
    write(n_ch - 1, (n_ch - 1) % 2).wait()


def _sc_row_scratch(per_w):
    return [pltpu.VMEM((2, _sc_chunk(per_w)) + ROW_TILE, jnp.int32),
            pltpu.SemaphoreType.DMA((2,)), pltpu.SemaphoreType.DMA((2,))]


def _sc_gather(table, idx):
    n_out = idx.shape[0]
    per_w = n_out // SC_WORKERS
    assert per_w * SC_WORKERS == n_out and per_w % 8 == 0

    def body(table_hbm, idx_hbm, out_hbm, idx_v, rows_v, gsem, wsem):
        base = _sc_worker_base(per_w)
        pltpu.sync_copy(idx_hbm.at[pl.ds(base, per_w)], idx_v)
        _sc_gather_rows(table_hbm, out_hbm, idx_v, rows_v, gsem, wsem, base, per_w)

    return pl.kernel(
        body, out_type=jax.ShapeDtypeStruct((n_out,) + ROW_TILE, jnp.int32), mesh=_sc_mesh(),
        scratch_types=[pltpu.VMEM((per_w,), jnp.int32)] + _sc_row_scratch(per_w), name="sc_gather")(table, idx)


PARTS = ((0, 54), (54, 108), (162, 162))
assert sum(n for _, n in PARTS) == N_BLOCKS and all(PARTS[i][0] + PARTS[i][1] == PARTS[i + 1][0]
                                                    for i in range(len(PARTS) - 1))


def _sc_dispatch(table, slots, blk0, n_blk):
    part_slots = n_blk * EXP_BLOCK
    per_w = part_slots // SC_WORKERS
    per_w_pad = -(-per_w // SC_LANES) * SC_LANES
    assert per_w * SC_WORKERS == part_slots and per_w % 8 == 0 and SC_SCAN % SC_LANES == 0
    assert ZERO_ROWS & (ZERO_ROWS - 1) == 0

    def body(table_hbm, slots_hbm, out_hbm, idx_v, sl_v, rows_v, gsem, wsem):
        local = _sc_worker_base(per_w)
        base = blk0 * EXP_BLOCK + local
        lane = lax.iota(jnp.int32, SC_LANES)

        @pl.loop(0, per_w_pad // SC_LANES)
        def _(j):
            off = pl.multiple_of(j * SC_LANES, SC_LANES)
            idx_v[pl.ds(off, SC_LANES)] = N + ((base + off + lane) & (ZERO_ROWS - 1))

        @pl.loop(0, N_ASSIGN // SC_SCAN)
        def _(c):
            pltpu.sync_copy(slots_hbm.at[pl.ds(pl.multiple_of(c * SC_SCAN, 8), SC_SCAN)], sl_v)

            @pl.loop(0, SC_SCAN // SC_LANES)
            def _(j):
                off = pl.multiple_of(j * SC_LANES, SC_LANES)
                loc = sl_v[pl.ds(off, SC_LANES)] - base
                mine = (loc >= 0) & (loc < per_w)
                tok = lax.shift_right_logical(c * SC_SCAN + off + lane, 3)
                plsc.store_scatter(idx_v, [jnp.where(mine, loc, 0)], tok, mask=mine)

        _sc_gather_rows(table_hbm, out_hbm, idx_v, rows_v, gsem, wsem, local, per_w)

    return pl.kernel(
        body, out_type=jax.ShapeDtypeStruct((part_slots,) + ROW_TILE, jnp.int32), mesh=_sc_mesh(),
        scratch_types=[pltpu.VMEM((per_w_pad,), jnp.int32), pltpu.VMEM((SC_SCAN,), jnp.int32)]
        + _sc_row_scratch(per_w),
        compiler_params=pltpu.CompilerParams(needs_layout_passes=False),
        name="sc_dispatch")(table, slots)


def _experts_body(be_ref, first_ref, par_ref, next_ref, nextblk_ref, nused_ref, x_ref, w1_hbm, w3_hbm, w2_hbm,
                  *rest, layer, blk0, n_blk):
    y_ref, w1f, w3f, w2f, w1b, w3b, w2b, sem = rest[-8:]
    i = pl.program_id(0)
    b = blk0 + i
    used = b < nused_ref[0]

    def copies(e, slot):
        return (pltpu.make_async_copy(w1_hbm.at[layer, e], w1f.at[slot], sem.at[0, slot]),
                pltpu.make_async_copy(w3_hbm.at[layer, e], w3f.at[slot], sem.at[1, slot]),
                pltpu.make_async_copy(w2_hbm.at[layer, e], w2f.at[slot], sem.at[2, slot]))

    @pl.when(used & (i == 0))
    def _():
        for c in copies(be_ref[b], par_ref[b]):
            c.start()

    @pl.when(used & ((i == 0) | (first_ref[b] == 1)))
    def _():
        slot = par_ref[b]
        for c in copies(be_ref[b], slot):
            c.wait()

        @pl.when((next_ref[b] >= 0) & (nextblk_ref[b] < blk0 + n_blk))
        def _():
            for c in copies(next_ref[b], 1 - slot):
                c.start(priority=1)
        w1b[...] = w1f[slot].astype(BF16)
        w3b[...] = w3f[slot].astype(BF16)
        w2b[...] = w2f[slot].astype(BF16)

    @pl.when(used)
    def _():
        lo, hi = _unpack_bf16_pair(_load_row_tiles_2d(x_ref, EXP_BLOCK))
        lo = lo.astype(BF16)
        hi = hi.astype(BF16)
        half = D // 2
        h1 = (jnp.dot(lo, w1b[:half, :], preferred_element_type=F32)
              + jnp.dot(hi, w1b[half:, :], preferred_element_type=F32))
        h3 = (jnp.dot(lo, w3b[:half, :], preferred_element_type=F32)
              + jnp.dot(hi, w3b[half:, :], preferred_element_type=F32))
        hid = (_silu(h1) * h3).astype(BF16)
        y = jnp.dot(hid, w2b[...], preferred_element_type=F32)
        _store_row_tiles_2d(y_ref, _pack_bf16_pair(y[:, :half], y[:, half:]), EXP_BLOCK)

    @pl.when(jnp.logical_not(used))
    def _():
        y_ref[...] = jnp.zeros_like(y_ref)


def _experts(ctl, xs_part, w1, w3, w2, layer, blk0, n_blk, ys_prev):
    def x_blk(i, *refs):
        n_here = jnp.clip(refs[-1][0] - blk0, 1, n_blk)
        return (jnp.minimum(i, n_here - 1), 0)
    any_spec = pl.BlockSpec(memory_space=pl.ANY)
    in_specs = [pl.BlockSpec((EXP_BLOCK * ROW_TILE[0], LANES), x_blk), any_spec, any_spec, any_spec]
    args = [xs_part.reshape(n_blk * EXP_BLOCK * ROW_TILE[0], LANES), w1, w3, w2]
    aliases = {}
    if ys_prev is not None:
        in_specs.append(any_spec)
        args.append(ys_prev)
        aliases = {len(ctl) + 4: 0}
    grid_spec = pltpu.PrefetchScalarGridSpec(
        num_scalar_prefetch=len(ctl),
        grid=(n_blk,),
        in_specs=in_specs,
        out_specs=pl.BlockSpec((EXP_BLOCK * ROW_TILE[0], LANES), lambda i, *refs: (blk0 + i, 0)),
        scratch_shapes=[pltpu.VMEM((2, D, D_EXPERT), F32), pltpu.VMEM((2, D, D_EXPERT), F32),
                        pltpu.VMEM((2, D_EXPERT, D), F32),
                        pltpu.VMEM((D, D_EXPERT), BF16), pltpu.VMEM((D, D_EXPERT), BF16),
                        pltpu.VMEM((D_EXPERT, D), BF16), pltpu.SemaphoreType.DMA((3, 2))])
    return pl.pallas_call(
        functools.partial(_experts_body, layer=layer, blk0=blk0, n_blk=n_blk), grid_spec=grid_spec,
        out_shape=jax.ShapeDtypeStruct((L_SLOTS * ROW_TILE[0], LANES), jnp.int32),
        input_output_aliases=aliases,
        compiler_params=_cparams(1), name="experts")(*ctl, *args)


TM_FFN = 640


def _ffn_pre_body(x_ref, g_ref, shp_ref, shs_ref, scp_ref, scs_ref, rw_ref, hb_ref, lg_ref, hp_ref):
    i = pl.program_id(0)

    @pl.when(i < N // TM_FFN)
    def _():
        for sidx in range(TM_FFN // SUB):
            rows = pl.ds(sidx * SUB, SUB)
            h = _prenorm_rows(i * (TM_FFN // SUB) + sidx, x_ref[rows, :], g_ref, shp_ref, shs_ref, scp_ref, scs_ref)
            lg_ref[rows, :] = jnp.dot(h, rw_ref[...], precision=HIGHEST, preferred_element_type=F32)
            packed = _pack_bf16_pair(h[:, :D // 2], h[:, D // 2:])
            for c in range(ROW_TILE[0]):
                hp_ref[pl.ds(sidx * SUB * ROW_TILE[0] + c, SUB, stride=ROW_TILE[0]), :] = (
                    packed[:, c * LANES:(c + 1) * LANES])
            hb_ref[rows, :] = h.astype(BF16)

    @pl.when(i >= N // TM_FFN)
    def _():
        hp_ref[...] = jnp.zeros_like(hp_ref)


def _ffn_pre(x, g, mod, layer, rw):
    last = N // TM_FFN - 1
    row = lambda i: (jnp.minimum(i, last), 0)
    mspec = lambda part, rows, blk: pl.BlockSpec((None, rows, D), lambda i, l=layer, p=part, b=blk: (l, b, p))
    return pl.pallas_call(
        _ffn_pre_body,
        grid=(N // TM_FFN + 1,),
        in_specs=[pl.BlockSpec((TM_FFN, D), row), pl.BlockSpec((1, D), lambda i: (0, 0)),
                  mspec(3, 8, 0), mspec(3, SUB, 1), mspec(4, 8, 0), mspec(4, SUB, 1),
                  pl.BlockSpec((None, D, LANES), lambda i, l=layer: (l, 0, 0))],
        out_specs=[pl.BlockSpec((TM_FFN, D), row), pl.BlockSpec((TM_FFN, LANES), row),
                   pl.BlockSpec((TM_FFN * ROW_TILE[0], LANES), lambda i: (i, 0))],
        out_shape=[jax.ShapeDtypeStruct((N, D), BF16), jax.ShapeDtypeStruct((N, LANES), F32),
                   jax.ShapeDtypeStruct(((N + TM_FFN) * ROW_TILE[0], LANES), jnp.int32)],
        compiler_params=_cparams(1), name="ffn_pre")(x, g.reshape(1, D), mod, mod, mod, mod, rw)


TM_SHARED = 640


def _shared_body(h_ref, w13_ref, w2_ref, after_ref, o_ref):
    up = jnp.dot(h_ref[...], w13_ref[...], preferred_element_type=F32)
    hid = (_silu(up[:, :D_EXPERT]) * up[:, D_EXPERT:]).astype(BF16)
    o_ref[...] = jnp.dot(hid, w2_ref[...], preferred_element_type=F32).astype(BF16)


def _shared(h, w13, w2, layer, after):
    return pl.pallas_call(
        _shared_body,
        grid=(N // TM_SHARED,),
        in_specs=[pl.BlockSpec((TM_SHARED, D), lambda i: (i, 0)),
                  pl.BlockSpec((None, D, 2 * D_EXPERT), lambda i, l=layer: (l, 0, 0)),
                  pl.BlockSpec((None, D_EXPERT, D), lambda i, l=layer: (l, 0, 0)),
                  pl.BlockSpec(memory_space=pl.ANY)],
        out_specs=pl.BlockSpec((TM_SHARED, D), lambda i: (i, 0)),
        out_shape=jax.ShapeDtypeStruct((N, D), BF16),
        compiler_params=_cparams(1), name="shared")(h, w13, w2, after)


def _combine_body(g_ref, wt_ref, sh_ref, x_ref, gn_ref, gp_ref, gs_ref, *rest, tile0, n_out, final):
    outs = rest[-n_out:]
    half = D // 2
    acc_lo = sh_ref[:, :half].astype(F32)
    acc_hi = sh_ref[:, half:].astype(F32)
    wt = wt_ref[...]
    per_tok = TOP_K * ROW_TILE[0]
    for k in range(TOP_K):
        packed = jnp.concatenate([g_ref[pl.ds(k * ROW_TILE[0] + c, SUB, stride=per_tok), :]
                                  for c in range(ROW_TILE[0])], axis=1)
        lo, hi = _unpack_bf16_pair(packed)
        w_c = wt[:, k:k + 1]
        acc_lo = acc_lo + w_c * lo
        acc_hi = acc_hi + w_c * hi
    t = tile0 + pl.program_id(0)
    val = _post_value(t, x_ref[...], jnp.concatenate([acc_lo, acc_hi], axis=1), gn_ref, gp_ref, gs_ref)
    if not final:
        outs[0][...] = val
        outs[1][...] = _prenorm_rows(t, val, *rest[:5]).astype(BF16)
    else:
        @pl.when(t < N_P // SUB)
        def _():
            outs[0][...] = val
        if n_out == 2:
            @pl.when(t >= N_P // SUB)
            def _():
                outs[1][...] = val


def _combine(gathered, wts, shared, x, g_post, mod, layer, tile0, n_tiles, out_prev, final, g_next):
    per_tok = TOP_K * ROW_TILE[0]
    row = lambda i: (tile0 + i, 0)
    in_specs = [pl.BlockSpec((SUB * per_tok, LANES), lambda i: (i, 0)),
                pl.BlockSpec((SUB, LANES), row), pl.BlockSpec((SUB, D), row), pl.BlockSpec((SUB, D), row),
                pl.BlockSpec((1, D), lambda i: (0, 0)),
                pl.BlockSpec((None, 8, D), lambda i, l=layer: (l, 0, 5)),
                pl.BlockSpec((None, SUB, D), lambda i, l=layer: (l, 1, 5))]
    args = [gathered.reshape(n_tiles * SUB * per_tok, LANES), wts, shared, x, g_post.reshape(1, D), mod, mod]
    if not final:
        in_specs += [pl.BlockSpec((1, D), lambda i: (0, 0))] + _mod_specs(layer + 1, 0) + _mod_specs(layer + 1, 1)
        args += [g_next.reshape(1, D), mod, mod, mod, mod]
    aliases = {}
    for k, prev in enumerate(out_prev or ()):
        if k == 0 or not final:
            in_specs.append(pl.BlockSpec(memory_space=pl.ANY))
            args.append(prev)
            aliases[len(args) - 1] = k
    if not final:
        out_specs = [pl.BlockSpec((SUB, D), row), pl.BlockSpec((SUB, D), row)]
        out_shape = [jax.ShapeDtypeStruct((N, D), F32), jax.ShapeDtypeStruct((N, D), BF16)]
    else:
        last_p = N_P // SUB - 1
        out_specs = [pl.BlockSpec((SUB, D), lambda i: (jnp.minimum(tile0 + i, last_p), 0))]
        out_shape = [jax.ShapeDtypeStruct((N_P, D), F32)]
        if tile0 + n_tiles > N_P // SUB:
            out_specs.append(pl.BlockSpec((N_S, D), lambda i: (0, 0)))
            out_shape.append(jax.ShapeDtypeStruct((N_S, D), F32))
    return pl.pallas_call(
        functools.partial(_combine_body, tile0=tile0, n_out=len(out_shape), final=final),
        grid=(n_tiles,),
        in_specs=in_specs, out_specs=out_specs, out_shape=out_shape,
        input_output_aliases=aliases,
        compiler_params=_cparams(1), name="combine")(*args)


def _slots_body(start_ref, eid_ref, pos_ref, o_ref):
    eid = eid_ref[...]
    acc = pos_ref[...]
    for e in range(N_EXPERTS):
        acc = acc + jnp.where(eid == e, start_ref[e], 0)
    o_ref[...] = acc


def _slots(pad_start, eid, pos):
    grid_spec = pltpu.PrefetchScalarGridSpec(
        num_scalar_prefetch=1, grid=(1,),
        in_specs=[pl.BlockSpec((TOP_K, N), lambda i, s: (0, 0)), pl.BlockSpec((TOP_K, N), lambda i, s: (0, 0))],
        out_specs=pl.BlockSpec((TOP_K, N), lambda i, s: (0, 0)))
    return pl.pallas_call(_slots_body, grid_spec=grid_spec,
                          out_shape=jax.ShapeDtypeStruct((TOP_K, N), jnp.int32),
                          compiler_params=_cparams(1), name="slots")(pad_start, eid, pos)


def _put_sample_rows_body(*refs):
    n = len(refs) // 3
    for src, dst in zip(refs[:n], refs[2 * n:]):
        dst[...] = src[...].astype(BF16)


def _put_sample_rows(sample_rows, full):
    n = len(full)
    return pl.pallas_call(
        _put_sample_rows_body,
        grid=(1,),
        in_specs=[pl.BlockSpec((N_S, 512), lambda i: (0, 0))] * n + [pl.BlockSpec(memory_space=pl.ANY)] * n,
        out_specs=[pl.BlockSpec((N_S, 512), lambda i: (N_P // N_S, 0))] * n,
        out_shape=[jax.ShapeDtypeStruct((N, 512), BF16)] * n,
        input_output_aliases={n + k: k for k in range(n)},
        compiler_params=_cparams(1), name="put_sample_rows")(*sample_rows, *full)


COMBINE_RANGES = ((0, 33), (33, 32))

def _prepare_weights(w_in, w_merge_gate, w_branch, w_out, router_w, shared_w1, shared_w3, shared_w2):
    return dict(
        w_main=jnp.concatenate([w_in[:, :, :1536], w_in[:, :, 1552:]], axis=2).astype(BF16),
        w_low=jnp.pad(w_in[:, :, 1536:1552], ((0, 0), (0, 0), (0, LANES - 16))).astype(BF16),
        w_mg=w_merge_gate.astype(BF16), w_br=w_branch.astype(BF16), w_out=w_out.astype(BF16),
        rw=jnp.pad(router_w, ((0, 0), (0, 0), (0, LANES - N_EXPERTS))),
        w13=jnp.concatenate([shared_w1, shared_w3], axis=2).astype(BF16), sw2=shared_w2.astype(BF16))


def _rope_tables(pos):
    half = DK // 2
    inv = ROPE_BASE ** (-jnp.arange(half, dtype=F32) / half)
    ang = pos.astype(F32)[:, None] * inv[None, :]
    cos = jnp.cos(ang)
    sin = jnp.sin(ang)
    return jnp.concatenate([cos, cos], axis=1), jnp.concatenate([-sin, sin], axis=1)


def _layer(l, x, mod, s_gla, s_pool, s_ret, wts, prep, final, prev_gla, prev_ret, h_in):
    (norm_mix_pre, norm_mix_post, norm_ffn_pre, norm_ffn_post, w_in, w_gla_gate, b_gla_gate, gla_norm,
     pool_w, pool_scale, ret_norm, sgu_norm, sgu_w, sgu_b, w_branch, w_merge_gate, b_merge_gate, w_out,
     router_w, router_bias, expert_w1, expert_w3, expert_w2, shared_w1, shared_w3, shared_w2) = wts

    h = _prenorm(x, norm_mix_pre[l], mod, l) if h_in is None else h_in
    p_main = _matmul(h, prep["w_main"], l, 1664, 768, name="inproj")
    p_low = _matmul(h, prep["w_low"], l, 1664, LANES, name="inproj_low")

    w_gate_pad = jnp.pad(w_gla_gate[l], ((0, LANES - 16), (0, 0)))
    b_gate = b_gla_gate[l].reshape(1, HEADS * DK)
    log_gamma = jnp.log1p(-jnp.exp2(-5.0 - jnp.arange(HEADS, dtype=F32)))
    dec_row = jnp.repeat(log_gamma, DK).reshape(1, HEADS * DK)
    cos_p, sin_p = _rope_tables(jnp.arange(T_P))
    cos_p = jnp.tile(cos_p, (1, 2))
    sin_p = jnp.tile(sin_p, (1, 2))
    g_gla = gla_norm[l].reshape(1, HEADS * DV)
    g_ret = ret_norm[l].reshape(1, HEADS * DV)

    oa_p, gla_p = _la_prompt(p_main, C_GQ, C_GK, C_GV, C_GR, p_low, p_low, w_gate_pad, b_gate, g_gla, False)
    oc_p, ret_p = _la_prompt(p_main, C_RQ, C_RK, C_RV, C_RG, cos_p, sin_p, dec_row, b_gate, g_ret, True)
    pw_bf = pool_w[l].astype(BF16)
    pscale = pool_scale[l].reshape(1, 512)
    ob_p = _pool_prompt(p_main, pw_bf, pscale)
    sgu_g = sgu_norm[l].reshape(1, 512)
    od_p = _sgu_prompt(p_main, sgu_g, sgu_w[l], jnp.pad(sgu_b[l].T, ((0, 0), (0, LANES - 4))))

    ps = p_main[N_P:]
    q_t = _to_tiles_t(ps[:, C_GQ:C_GQ + 256])
    k_t = _to_tiles_t(ps[:, C_GK:C_GK + 256])
    glow_t = jnp.pad(_to_tiles_t(p_low[N_P:, :16]), ((0, 0), (0, LANES - 16), (0, 0)))
    w_gate_t = jnp.pad(w_gla_gate[l].T, ((0, 0), (0, LANES - 16)))
    b_col = jnp.broadcast_to(b_gla_gate[l][:, None], (HEADS * DK, LANES))
    logit_t = _gate_logits_t(w_gate_t, glow_t, b_col)
    dummy = jnp.zeros((HEADS * DK, LANES), F32)
    oa_s, gla_s = _la_sample(q_t, k_t, logit_t, dummy, dummy, p_main, C_GV, C_GR, g_gla, s_gla, l, prev_gla, False)
    cos_s, sin_s = _rope_tables(jnp.full((1,), PAST_LEN))
    cos_c = jnp.broadcast_to(jnp.tile(cos_s[0], HEADS)[:, None], (HEADS * DK, LANES))
    sin_c = jnp.broadcast_to(jnp.tile(sin_s[0], HEADS)[:, None], (HEADS * DK, LANES))
    dec_c = jnp.broadcast_to(jnp.repeat(log_gamma, DK)[:, None], (HEADS * DK, LANES))
    rq_t = _to_tiles_t(ps[:, C_RQ:C_RQ + 256])
    rk_t = _to_tiles_t(ps[:, C_RK:C_RK + 256])
    oc_s, ret_s = _la_sample(rq_t, rk_t, dec_c, cos_c, sin_c, p_main, C_RV, C_RG, g_ret, s_ret, l, prev_ret, True)
    sgu_w0 = jnp.repeat(sgu_w[l][:, 0, 0], LANES).reshape(1, 512)
    sgu_b0 = jnp.repeat(sgu_b[l][:, 0], LANES).reshape(1, 512)
    ob_s, od_s, vn_s = _small_sample(p_main, s_pool[l], pw_bf, pscale, sgu_g, sgu_w0, sgu_b0)
    pool_p = jnp.stack([p_main[(b + 1) * T_P - POOL_BUF:(b + 1) * T_P, C_PIN:C_PIN + 512] for b in range(B_P)])
    pool_s = jnp.concatenate([s_pool[l][:, 1:], ps[:, None, C_PIN:C_PIN + 512]], axis=1)

    branches = _put_sample_rows([oa_s, ob_s, oc_s, od_s], [oa_p, ob_p, oc_p, od_p])
    merged = _merge(h, branches, prep["w_mg"], b_merge_gate.reshape(DEPTH, 1, 4 * D), prep["w_br"], l)
    x = _outproj(merged, prep["w_out"], x, norm_mix_post[l], mod, l)

    rb = jnp.broadcast_to(router_bias[l][:, None], (N_EXPERTS, SUB))
    h2, logits, h2_packed = _ffn_pre(x, norm_ffn_pre[l], mod, l, prep["rw"])
    eid, pos, wt, counts = _router(logits, rb)
    counts = counts[:, 0].astype(jnp.int32)
    padded = (counts + EXP_BLOCK - 1) // EXP_BLOCK * EXP_BLOCK
    pad_end = jnp.cumsum(padded)
    pad_start = pad_end - padded
    nused = (pad_end[-1] // EXP_BLOCK).astype(jnp.int32).reshape(1)
    blk_row = jnp.arange(N_BLOCKS, dtype=jnp.int32) * EXP_BLOCK
    block_e = jnp.minimum(jnp.sum((blk_row[:, None] >= pad_end[None, :]).astype(jnp.int32), axis=1),
                          N_EXPERTS - 1)
    first = jnp.concatenate([jnp.ones((1,), jnp.int32), (block_e[1:] != block_e[:-1]).astype(jnp.int32)])
    first = jnp.where(blk_row < pad_end[-1], first, 0)
    par = (jnp.cumsum(first) - 1) % 2
    live = jnp.where(padded > 0, jnp.arange(N_EXPERTS), N_EXPERTS)
    after = jnp.concatenate([lax.cummin(live, reverse=True)[1:], jnp.full((1,), N_EXPERTS)])
    of_block = block_e[:, None] == jnp.arange(N_EXPERTS)
    next_e = jnp.sum(jnp.where(of_block, jnp.where(after < N_EXPERTS, after, -1), 0), axis=1).astype(jnp.int32)
    next_blk = jnp.sum(jnp.where(of_block, pad_end // EXP_BLOCK, 0), axis=1).astype(jnp.int32)
    slots = _slots(pad_start.astype(jnp.int32), eid, pos).T.reshape(N_ASSIGN)
    wt = jnp.pad(wt.T, ((0, 0), (0, LANES - TOP_K)))
    table = h2_packed.reshape((N + TM_FFN,) + ROW_TILE)
    ctl = (block_e, first, par.astype(jnp.int32), next_e, next_blk, nused)
    xs_parts = [_sc_dispatch(table, slots, blk0, n_blk) for blk0, n_blk in PARTS]
    shared = _shared(h2, prep["w13"], prep["sw2"], l, slots)
    ys = None
    for xs_part, (blk0, n_blk) in zip(xs_parts, PARTS):
        ys = _experts(ctl, xs_part, expert_w1, expert_w3, expert_w2, l, blk0, n_blk, ys)
    ys = ys.reshape((L_SLOTS,) + ROW_TILE)
    outs = None
    for tile0, n_tiles in COMBINE_RANGES:
        a0, a1 = tile0 * SUB * TOP_K, (tile0 + n_tiles) * SUB * TOP_K
        outs = _combine(_sc_gather(ys, slots[a0:a1]), wt, shared, x, norm_ffn_post[l], mod, l, tile0, n_tiles,
                        outs, final, None if final else norm_mix_pre[l + 1])
    x, h_next = (tuple(outs), None) if final else outs
    return x, (gla_p, pool_p, pool_s, ret_p, vn_s), gla_s, ret_s, h_next


def kernel(x_prompt, x_sample, c_prompt, c_sample, state_gla, state_pool, state_ret, w_ada, b_ada, norm_mix_pre, norm_mix_post, norm_ffn_pre, norm_ffn_post, w_in, w_gla_gate, b_gla_gate, gla_norm, pool_w, pool_scale, ret_norm, sgu_norm, sgu_w, sgu_b, w_branch, w_merge_gate, b_merge_gate, w_out, router_w, router_bias, expert_w1, expert_w3, expert_w2, shared_w1, shared_w3, shared_w2):
    wts = (norm_mix_pre, norm_mix_post, norm_ffn_pre, norm_ffn_post, w_in, w_gla_gate, b_gla_gate, gla_norm,
           pool_w, pool_scale, ret_norm, sgu_norm, sgu_w, sgu_b, w_branch, w_merge_gate, b_merge_gate, w_out,
           router_w, router_bias, expert_w1, expert_w3, expert_w2, shared_w1, shared_w3, shared_w2)
    c_all = jnp.zeros((MOD_ROWS, D), F32).at[:B_P].set(c_prompt).at[SUB:SUB + N_S].set(c_sample)
    mod = _ada(c_all, w_ada, b_ada)
    x = (x_prompt.reshape(N_P, D), x_sample.reshape(N_S, D))
    prep = _prepare_weights(w_in, w_merge_gate, w_branch, w_out, router_w, shared_w1, shared_w3, shared_w2)
    per_layer = []
    gla_s = ret_s = h = None
    for l in range(DEPTH):
        x, states, gla_s, ret_s, h = _layer(l, x, mod, state_gla, state_pool, state_ret, wts, prep, l == DEPTH - 1,
                                            gla_s, ret_s, h)
        per_layer.append(states)
    gla_p, pool_p, pool_s, ret_p, vn_s = (jnp.stack(z) for z in zip(*per_layer))
    return (x[0].reshape(B_P, T_P, D), x[1].reshape(N_S, 1, D),
            gla_p, gla_s, pool_p, pool_s, ret_p, ret_s, vn_s.reshape(DEPTH, N_S, 1, 512))
```
